```python
import math
import jax, jax.numpy as jnp
from jax import lax
import numpy as np

D_MODEL = 1024
BATCH = 8
SEQ = 8192
DEPTH = 1

D_PLE = 256
GRID_W = 64
D_SSM = 512
SSM_GROUP = 16
N_SSM_GROUPS = D_SSM // SSM_GROUP
SSM_STATE = 64
DT_MIN = 1e-3
DT_MAX = 1e-1
D_NA = 512
NA_HEADS = 8
NA_HEAD_DIM = D_NA // NA_HEADS
NA_ROWS_MAX = 8
NA_COLS = 16
D_MIX = D_SSM + D_NA
D_IN_PROJ = 2 * D_SSM + 4 * D_NA
EPS = 1e-6

kernel_name = "hybrid_s5_natten_sandwich_ple_encoder"


def rms_norm(x, gain):
    xf = x.astype(jnp.float32)
    y = xf * lax.rsqrt(jnp.mean(xf * xf, axis=-1, keepdims=True) + EPS)
    return (y * gain.astype(jnp.float32)).astype(x.dtype)


def _complex_linear_combine(left, right):
    a1r, a1i, b1r, b1i = left
    a2r, a2i, b2r, b2i = right
    return (a2r * a1r - a2i * a1i,
            a2r * a1i + a2i * a1r,
            a2r * b1r - a2i * b1i + b2r,
            a2r * b1i + a2i * b1r + b2i)


def s5_bidirectional(u, a_re, a_im, log_dt, b_re, b_im, c_re, c_im, d):
    f32 = jnp.float32
    bsz, seqlen, _ = u.shape
    uf = u.astype(f32).reshape(bsz, seqlen, N_SSM_GROUPS, SSM_GROUP)
    y = uf * d.astype(f32)
    for direction in range(2):
        ar = a_re[direction].astype(f32)
        ai = a_im[direction].astype(f32)
        dt = jnp.exp(log_dt[direction].astype(f32))[:, None]
        mag = jnp.exp(dt * ar)
        abar_re = mag * jnp.cos(dt * ai)
        abar_im = mag * jnp.sin(dt * ai)
        num_re = abar_re - 1.0
        num_im = abar_im
        denom = ar * ar + ai * ai
        coef_re = (num_re * ar + num_im * ai) / denom
        coef_im = (num_im * ar - num_re * ai) / denom
        br = b_re[direction].astype(f32)
        bi = b_im[direction].astype(f32)
        bbar_re = coef_re[..., None] * br - coef_im[..., None] * bi
        bbar_im = coef_re[..., None] * bi + coef_im[..., None] * br
        bu_re = jnp.einsum('blgh,gph->blgp', uf, bbar_re)
        bu_im = jnp.einsum('blgh,gph->blgp', uf, bbar_im)
        shp = (1, seqlen, N_SSM_GROUPS, SSM_STATE)
        a_seq_re = jnp.broadcast_to(abar_re, shp)
        a_seq_im = jnp.broadcast_to(abar_im, shp)
        _, _, h_re, h_im = lax.associative_scan(
            _complex_linear_combine, (a_seq_re, a_seq_im, bu_re, bu_im),
            reverse=(direction == 1), axis=1)
        cr = c_re[direction].astype(f32)
        ci = c_im[direction].astype(f32)
        y = y + jnp.einsum('blgp,ghp->blgh', h_re, cr) - jnp.einsum('blgp,ghp->blgh', h_im, ci)
    return y.reshape(bsz, seqlen, D_SSM)


def neighborhood_attention_2d(q, k, v, rpb):
    f32 = jnp.float32
    bsz, seqlen, _ = q.shape
    rows = seqlen // GRID_W
    kh = min(NA_ROWS_MAX, rows)
    shp = (bsz, rows, GRID_W, NA_HEADS, NA_HEAD_DIM)
    q = q.reshape(shp)
    k = k.reshape(shp)
    v = v.reshape(shp)
    r = jnp.arange(rows)
    row_start = jnp.clip(r - kh // 2, 0, rows - kh)
    row_idx = row_start[:, None] + jnp.arange(kh)[None, :]
    k_blk = k[:, row_idx]
    v_blk = v[:, row_idx]
    c = jnp.arange(GRID_W)
    col_start = jnp.clip(c - NA_COLS // 2, 0, GRID_W - NA_COLS)
    col_in = (c[None, :] >= col_start[:, None]) & (c[None, :] < col_start[:, None] + NA_COLS)
    dr = row_idx - r[:, None] + (NA_ROWS_MAX - 1)
    dc = jnp.clip(c[None, :] - c[:, None] + (NA_COLS - 1), 0, 2 * NA_COLS - 2)
    bias = rpb.astype(f32)[:, dr[:, None, :, None], dc[None, :, None, :]]
    scale = NA_HEAD_DIM ** -0.5
    scores = jnp.einsum('brqhd,brikhd->bhrqik', q, k_blk,
                        preferred_element_type=f32) * scale + bias
    scores = jnp.where(col_in[:, None, :], scores, jnp.finfo(f32).min)
    probs = jax.nn.softmax(scores, axis=(-2, -1))
    out = jnp.einsum('bhrqik,brikhd->brqhd', probs.astype(v.dtype), v_blk)
    return out.reshape(bsz, seqlen, D_NA)


def _fwd_setup_inputs(seed: int = 0) -> dict:
    key = jax.random.key(seed)
    ks = jax.random.split(key, 24)
    f32 = jnp.float32
    G, P, H = N_SSM_GROUPS, SSM_STATE, SSM_GROUP
    nrm = lambda kk, shape, s: jax.random.normal(kk, shape, f32) * s
    n_idx = jnp.arange(P, dtype=f32)
    return {
        "x": nrm(ks[0], (BATCH, SEQ, D_MODEL), 1.0),
        "p": nrm(ks[1], (DEPTH, BATCH, SEQ, D_PLE), 1.0),
        "norm_pre": 1.0 + nrm(ks[2], (DEPTH, D_MODEL), 0.02),
        "norm_post": 1.0 + nrm(ks[3], (DEPTH, D_MODEL), 0.02),
        "w_in": nrm(ks[4], (DEPTH, D_MODEL, D_IN_PROJ), D_MODEL ** -0.5),
        "ssm_a_re": -0.5 + nrm(ks[5], (DEPTH, 2, G, P), 0.01),
        "ssm_a_im": math.pi * n_idx + nrm(ks[6], (DEPTH, 2, G, P), 0.01),
        "ssm_log_dt": jax.random.uniform(ks[7], (DEPTH, 2, G), f32,
                                         minval=math.log(DT_MIN), maxval=math.log(DT_MAX)),
        "ssm_b_re": nrm(ks[8], (DEPTH, 2, G, P, H), H ** -0.5),
        "ssm_b_im": nrm(ks[9], (DEPTH, 2, G, P, H), H ** -0.5),
        "ssm_c_re": nrm(ks[10], (DEPTH, 2, G, H, P), P ** -0.5),
        "ssm_c_im": nrm(ks[11], (DEPTH, 2, G, H, P), P ** -0.5),
        "ssm_d": nrm(ks[12], (DEPTH, G, H), 1.0),
        "w_glu": nrm(ks[13], (DEPTH, D_SSM, D_SSM), D_SSM ** -0.5),
        "b_glu": nrm(ks[14], (DEPTH, D_SSM), 0.01),
        "na_rpb": nrm(ks[15], (DEPTH, NA_HEADS, 2 * NA_ROWS_MAX - 1, 2 * NA_COLS - 1), 0.02),
        "w_out": nrm(ks[16], (DEPTH, D_MIX, D_MODEL), D_MIX ** -0.5),
        "w_ple": nrm(ks[17], (DEPTH, D_PLE, D_MODEL), D_PLE ** -0.5),
        "ple_norm": 1.0 + nrm(ks[18], (DEPTH, D_MODEL), 0.02),
        "w_ple_gate": nrm(ks[19], (DEPTH, D_MODEL, D_MODEL), D_MODEL ** -0.5),
    }


def _fwd_reference(x, p, norm_pre, norm_post, w_in, ssm_a_re, ssm_a_im, ssm_log_dt,
              ssm_b_re, ssm_b_im, ssm_c_re, ssm_c_im, ssm_d, w_glu, b_glu,
              na_rpb, w_out, w_ple, ple_norm, w_ple_gate):
    h = x
    splits = [D_SSM, 2 * D_SSM, 2 * D_SSM + D_NA, 2 * D_SSM + 2 * D_NA, 2 * D_SSM + 3 * D_NA]
    for i in range(DEPTH):
        hn = rms_norm(h, norm_pre[i])
        proj = hn @ w_in[i]
        u_s, z_s, q, k, v, z_n = jnp.split(proj, splits, axis=-1)
        y_s = s5_bidirectional(u_s, ssm_a_re[i], ssm_a_im[i], ssm_log_dt[i],
                               ssm_b_re[i], ssm_b_im[i], ssm_c_re[i], ssm_c_im[i], ssm_d[i])
        y_s = jax.nn.gelu(y_s.astype(hn.dtype))
        y_s = y_s * jax.nn.sigmoid(y_s @ w_glu[i] + b_glu[i])
        y_s = y_s * jax.nn.silu(z_s)
        y_n = neighborhood_attention_2d(q, k, v, na_rpb[i]) * jax.nn.silu(z_n)
        mix = jnp.concatenate([y_s, y_n], axis=-1) @ w_out[i]
        h = h + rms_norm(mix, norm_post[i])
        e = rms_norm(p[i] @ w_ple[i], ple_norm[i])
        h = h + jax.nn.sigmoid(h @ w_ple_gate[i]) * e
    return h


import jax as _jax
import jax.numpy as _jnp

TWIN_FORMAT = 'train_step'
FWD_PARAMS = ['x', 'p', 'norm_pre', 'norm_post', 'w_in', 'ssm_a_re', 'ssm_a_im', 'ssm_log_dt', 'ssm_b_re', 'ssm_b_im', 'ssm_c_re', 'ssm_c_im', 'ssm_d', 'w_glu', 'b_glu', 'na_rpb', 'w_out', 'w_ple', 'ple_norm', 'w_ple_gate']
TWIN_WEIGHTS = ['norm_pre', 'norm_post', 'w_in', 'ssm_a_re', 'ssm_a_im', 'ssm_log_dt', 'ssm_b_re', 'ssm_b_im', 'ssm_c_re', 'ssm_c_im', 'ssm_d', 'w_glu', 'b_glu', 'na_rpb', 'w_out', 'w_ple', 'ple_norm', 'w_ple_gate']
TWIN_DIFF_INPUT = 'x'
TWIN_INPUTS = ['x', 'p', 'norm_pre', 'norm_post', 'w_in', 'ssm_a_re', 'ssm_a_im', 'ssm_log_dt', 'ssm_b_re', 'ssm_b_im', 'ssm_c_re', 'ssm_c_im', 'ssm_d', 'w_glu', 'b_glu', 'na_rpb', 'w_out', 'w_ple', 'ple_norm', 'w_ple_gate', 'loss_target', 'm_norm_pre', 'm_norm_post', 'm_w_in', 'm_ssm_a_re', 'm_ssm_a_im', 'm_ssm_log_dt', 'm_ssm_b_re', 'm_ssm_b_im', 'm_ssm_c_re', 'm_ssm_c_im', 'm_ssm_d', 'm_w_glu', 'm_b_glu', 'm_na_rpb', 'm_w_out', 'm_w_ple', 'm_ple_norm', 'm_w_ple_gate', 'v_norm_pre', 'v_norm_post', 'v_w_in', 'v_ssm_a_re', 'v_ssm_a_im', 'v_ssm_log_dt', 'v_ssm_b_re', 'v_ssm_b_im', 'v_ssm_c_re', 'v_ssm_c_im', 'v_ssm_d', 'v_w_glu', 'v_b_glu', 'v_na_rpb', 'v_w_out', 'v_w_ple', 'v_ple_norm', 'v_w_ple_gate']
TWIN_OUTPUTS = ['loss', 'grad_x', 'grad_norm_pre', 'grad_norm_post', 'grad_w_in', 'grad_ssm_a_re', 'grad_ssm_a_im', 'grad_ssm_log_dt', 'grad_ssm_b_re', 'grad_ssm_b_im', 'grad_ssm_c_re', 'grad_ssm_c_im', 'grad_ssm_d', 'grad_w_glu', 'grad_b_glu', 'grad_na_rpb', 'grad_w_out', 'grad_w_ple', 'grad_ple_norm', 'grad_w_ple_gate', 'delta_norm_pre', 'delta_norm_post', 'delta_w_in', 'delta_ssm_a_re', 'delta_ssm_a_im', 'delta_ssm_log_dt', 'delta_ssm_b_re', 'delta_ssm_b_im', 'delta_ssm_c_re', 'delta_ssm_c_im', 'delta_ssm_d', 'delta_w_glu', 'delta_b_glu', 'delta_na_rpb', 'delta_w_out', 'delta_w_ple', 'delta_ple_norm', 'delta_w_ple_gate', 'new_m_norm_pre', 'new_m_norm_post', 'new_m_w_in', 'new_m_ssm_a_re', 'new_m_ssm_a_im', 'new_m_ssm_log_dt', 'new_m_ssm_b_re', 'new_m_ssm_b_im', 'new_m_ssm_c_re', 'new_m_ssm_c_im', 'new_m_ssm_d', 'new_m_w_glu', 'new_m_b_glu', 'new_m_na_rpb', 'new_m_w_out', 'new_m_w_ple', 'new_m_ple_norm', 'new_m_w_ple_gate', 'new_v_norm_pre', 'new_v_norm_post', 'new_v_w_in', 'new_v_ssm_a_re', 'new_v_ssm_a_im', 'new_v_ssm_log_dt', 'new_v_ssm_b_re', 'new_v_ssm_b_im', 'new_v_ssm_c_re', 'new_v_ssm_c_im', 'new_v_ssm_d', 'new_v_w_glu', 'new_v_b_glu', 'new_v_na_rpb', 'new_v_w_out', 'new_v_w_ple', 'new_v_ple_norm', 'new_v_w_ple_gate']
TWIN_LEAF_KINDS = {'loss': 'loss', 'grad_x': 'grad_x', 'grad_norm_pre': 'grad_w', 'grad_norm_post': 'grad_w', 'grad_w_in': 'grad_w', 'grad_ssm_a_re': 'grad_w', 'grad_ssm_a_im': 'grad_w', 'grad_ssm_log_dt': 'grad_w', 'grad_ssm_b_re': 'grad_w', 'grad_ssm_b_im': 'grad_w', 'grad_ssm_c_re': 'grad_w', 'grad_ssm_c_im': 'grad_w', 'grad_ssm_d': 'grad_w', 'grad_w_glu': 'grad_w', 'grad_b_glu': 'grad_w', 'grad_na_rpb': 'grad_w', 'grad_w_out': 'grad_w', 'grad_w_ple': 'grad_w', 'grad_ple_norm': 'grad_w', 'grad_w_ple_gate': 'grad_w', 'delta_norm_pre': 'delta_w', 'delta_norm_post': 'delta_w', 'delta_w_in': 'delta_w', 'delta_ssm_a_re': 'delta_w', 'delta_ssm_a_im': 'delta_w', 'delta_ssm_log_dt': 'delta_w', 'delta_ssm_b_re': 'delta_w', 'delta_ssm_b_im': 'delta_w', 'delta_ssm_c_re': 'delta_w', 'delta_ssm_c_im': 'delta_w', 'delta_ssm_d': 'delta_w', 'delta_w_glu': 'delta_w', 'delta_b_glu': 'delta_w', 'delta_na_rpb': 'delta_w', 'delta_w_out': 'delta_w', 'delta_w_ple': 'delta_w', 'delta_ple_norm': 'delta_w', 'delta_w_ple_gate': 'delta_w', 'new_m_norm_pre': 'new_m', 'new_m_norm_post': 'new_m', 'new_m_w_in': 'new_m', 'new_m_ssm_a_re': 'new_m', 'new_m_ssm_a_im': 'new_m', 'new_m_ssm_log_dt': 'new_m', 'new_m_ssm_b_re': 'new_m', 'new_m_ssm_b_im': 'new_m', 'new_m_ssm_c_re': 'new_m', 'new_m_ssm_c_im': 'new_m', 'new_m_ssm_d': 'new_m', 'new_m_w_glu': 'new_m', 'new_m_b_glu': 'new_m', 'new_m_na_rpb': 'new_m', 'new_m_w_out': 'new_m', 'new_m_w_ple': 'new_m', 'new_m_ple_norm': 'new_m', 'new_m_w_ple_gate': 'new_m', 'new_v_norm_pre': 'new_v', 'new_v_norm_post': 'new_v', 'new_v_w_in': 'new_v', 'new_v_ssm_a_re': 'new_v', 'new_v_ssm_a_im': 'new_v', 'new_v_ssm_log_dt': 'new_v', 'new_v_ssm_b_re': 'new_v', 'new_v_ssm_b_im': 'new_v', 'new_v_ssm_c_re': 'new_v', 'new_v_ssm_c_im': 'new_v', 'new_v_ssm_d': 'new_v', 'new_v_w_glu': 'new_v', 'new_v_b_glu': 'new_v', 'new_v_na_rpb': 'new_v', 'new_v_w_out': 'new_v', 'new_v_w_ple': 'new_v', 'new_v_ple_norm': 'new_v', 'new_v_w_ple_gate': 'new_v'}


def _forward(args):
    return _fwd_reference(*[args[k] for k in FWD_PARAMS])


def _output_shape():
    def fwd():
        inp = _fwd_setup_inputs(0)
        return _fwd_reference(*[inp[k] for k in FWD_PARAMS])
    out = _jax.eval_shape(fwd)
    return out.shape, out.dtype

N_MICROBATCH = 1
ADAM_LR = 0.001
ADAM_B1 = 0.9
ADAM_B2 = 0.999
ADAM_EPS = 1e-08
ADAM_WD = 0.01
ADAM_STEP = 10
PER_EXAMPLE_BATCH_AXIS = {'x': 0, 'p': 1, 'loss_target': 0}
SHARED_INPUTS = []
_WEIGHT_DTYPES = {'norm_pre': _jnp.float32, 'norm_post': _jnp.float32, 'w_in': _jnp.float32, 'ssm_a_re': _jnp.float32, 'ssm_a_im': _jnp.float32, 'ssm_log_dt': _jnp.float32, 'ssm_b_re': _jnp.float32, 'ssm_b_im': _jnp.float32, 'ssm_c_re': _jnp.float32, 'ssm_c_im': _jnp.float32, 'ssm_d': _jnp.float32, 'w_glu': _jnp.float32, 'b_glu': _jnp.float32, 'na_rpb': _jnp.float32, 'w_out': _jnp.float32, 'w_ple': _jnp.float32, 'ple_norm': _jnp.float32, 'w_ple_gate': _jnp.float32}
MOMENT_SCALE = {'norm_pre': 7.883791e-01, 'norm_post': 6.556780e+01, 'w_in': 4.506917e-01, 'ssm_a_re': 6.776809e-02, 'ssm_a_im': 6.832989e-02, 'ssm_log_dt': 5.616295e+01, 'ssm_b_re': 3.035112e-02, 'ssm_b_im': 3.128051e-02, 'ssm_c_re': 6.102854e-02, 'ssm_c_im': 6.212880e-02, 'ssm_d': 1.404001e+00, 'w_glu': 2.199659e-01, 'b_glu': 6.041071e-01, 'na_rpb': 9.817532e-02, 'w_out': 8.737507e-01, 'w_ple': 2.634110e-01, 'ple_norm': 2.060221e+01, 'w_ple_gate': 7.350154e-01}


def _to_microbatches(a, axis):
    t = _jnp.moveaxis(a, axis, 0)
    t = t.reshape((N_MICROBATCH, t.shape[0] // N_MICROBATCH) + t.shape[1:])
    return _jnp.moveaxis(t, 1, axis + 1)


def setup_inputs(seed: int = 0) -> dict:
    inp = _fwd_setup_inputs(seed)
    key = _jax.random.fold_in(_jax.random.key(seed), 7919)
    shape, _ = _output_shape()
    out = dict(inp)
    out["loss_target"] = _jax.random.normal(_jax.random.fold_in(key, 0), shape, _jnp.float32)
    for i, name in enumerate(TWIN_WEIGHTS):
        w = inp[name].astype(_jnp.float32)
        if MOMENT_SCALE is None:
            s = _jnp.sqrt(_jnp.mean(_jnp.square(w)) + 1e-30)
        else:
            s = MOMENT_SCALE[name]
        km, kv = _jax.random.split(_jax.random.fold_in(key, i + 1))
        out[name] = w
        out["m_" + name] = s * _jax.random.normal(km, w.shape, _jnp.float32)
        out["v_" + name] = (s * s) * _jax.random.uniform(kv, w.shape, _jnp.float32, 0.5, 1.5)
    if N_MICROBATCH > 1:
        for name, axis in PER_EXAMPLE_BATCH_AXIS.items():
            out[name] = _to_microbatches(out[name], axis)
    return {'x': out['x'], 'p': out['p'], 'norm_pre': out['norm_pre'], 'norm_post': out['norm_post'], 'w_in': out['w_in'], 'ssm_a_re': out['ssm_a_re'], 'ssm_a_im': out['ssm_a_im'], 'ssm_log_dt': out['ssm_log_dt'], 'ssm_b_re': out['ssm_b_re'], 'ssm_b_im': out['ssm_b_im'], 'ssm_c_re': out['ssm_c_re'], 'ssm_c_im': out['ssm_c_im'], 'ssm_d': out['ssm_d'], 'w_glu': out['w_glu'], 'b_glu': out['b_glu'], 'na_rpb': out['na_rpb'], 'w_out': out['w_out'], 'w_ple': out['w_ple'], 'ple_norm': out['ple_norm'], 'w_ple_gate': out['w_ple_gate'], 'loss_target': out['loss_target'], 'm_norm_pre': out['m_norm_pre'], 'm_norm_post': out['m_norm_post'], 'm_w_in': out['m_w_in'], 'm_ssm_a_re': out['m_ssm_a_re'], 'm_ssm_a_im': out['m_ssm_a_im'], 'm_ssm_log_dt': out['m_ssm_log_dt'], 'm_ssm_b_re': out['m_ssm_b_re'], 'm_ssm_b_im': out['m_ssm_b_im'], 'm_ssm_c_re': out['m_ssm_c_re'], 'm_ssm_c_im': out['m_ssm_c_im'], 'm_ssm_d': out['m_ssm_d'], 'm_w_glu': out['m_w_glu'], 'm_b_glu': out['m_b_glu'], 'm_na_rpb': out['m_na_rpb'], 'm_w_out': out['m_w_out'], 'm_w_ple': out['m_w_ple'], 'm_ple_norm': out['m_ple_norm'], 'm_w_ple_gate': out['m_w_ple_gate'], 'v_norm_pre': out['v_norm_pre'], 'v_norm_post': out['v_norm_post'], 'v_w_in': out['v_w_in'], 'v_ssm_a_re': out['v_ssm_a_re'], 'v_ssm_a_im': out['v_ssm_a_im'], 'v_ssm_log_dt': out['v_ssm_log_dt'], 'v_ssm_b_re': out['v_ssm_b_re'], 'v_ssm_b_im': out['v_ssm_b_im'], 'v_ssm_c_re': out['v_ssm_c_re'], 'v_ssm_c_im': out['v_ssm_c_im'], 'v_ssm_d': out['v_ssm_d'], 'v_w_glu': out['v_w_glu'], 'v_b_glu': out['v_b_glu'], 'v_na_rpb': out['v_na_rpb'], 'v_w_out': out['v_w_out'], 'v_w_ple': out['v_w_ple'], 'v_ple_norm': out['v_ple_norm'], 'v_w_ple_gate': out['v_w_ple_gate']}


def _loss(weights, diff, rest, loss_target):
    with _jax.named_scope("forward"):
        args = {**rest, TWIN_DIFF_INPUT: diff, **{k: w.astype(_WEIGHT_DTYPES[k]) for k, w in weights.items()}}
        y = _forward(args)
    with _jax.named_scope("loss_head"):
        err = _jnp.square(y.astype(_jnp.float32) - loss_target)
        return 0.5 * _jnp.sum(_jnp.mean(err, axis=-1)) if err.ndim else 0.5 * err


def _adamw(w, g, m, v):
    m = ADAM_B1 * m + (1.0 - ADAM_B1) * g
    v = ADAM_B2 * v + (1.0 - ADAM_B2) * _jnp.square(g)
    m_hat = m / (1.0 - ADAM_B1 ** ADAM_STEP)
    v_hat = v / (1.0 - ADAM_B2 ** ADAM_STEP)
    delta = -ADAM_LR * (m_hat / (_jnp.sqrt(v_hat) + ADAM_EPS) + ADAM_WD * w)
    return delta, m, v


def reference(x, p, norm_pre, norm_post, w_in, ssm_a_re, ssm_a_im, ssm_log_dt, ssm_b_re, ssm_b_im, ssm_c_re, ssm_c_im, ssm_d, w_glu, b_glu, na_rpb, w_out, w_ple, ple_norm, w_ple_gate, loss_target, m_norm_pre, m_norm_post, m_w_in, m_ssm_a_re, m_ssm_a_im, m_ssm_log_dt, m_ssm_b_re, m_ssm_b_im, m_ssm_c_re, m_ssm_c_im, m_ssm_d, m_w_glu, m_b_glu, m_na_rpb, m_w_out, m_w_ple, m_ple_norm, m_w_ple_gate, v_norm_pre, v_norm_post, v_w_in, v_ssm_a_re, v_ssm_a_im, v_ssm_log_dt, v_ssm_b_re, v_ssm_b_im, v_ssm_c_re, v_ssm_c_im, v_ssm_d, v_w_glu, v_b_glu, v_na_rpb, v_w_out, v_w_ple, v_ple_norm, v_w_ple_gate):
    given = dict(x=x, p=p, norm_pre=norm_pre, norm_post=norm_post, w_in=w_in, ssm_a_re=ssm_a_re, ssm_a_im=ssm_a_im, ssm_log_dt=ssm_log_dt, ssm_b_re=ssm_b_re, ssm_b_im=ssm_b_im, ssm_c_re=ssm_c_re, ssm_c_im=ssm_c_im, ssm_d=ssm_d, w_glu=w_glu, b_glu=b_glu, na_rpb=na_rpb, w_out=w_out, w_ple=w_ple, ple_norm=ple_norm, w_ple_gate=w_ple_gate, loss_target=loss_target, m_norm_pre=m_norm_pre, m_norm_post=m_norm_post, m_w_in=m_w_in, m_ssm_a_re=m_ssm_a_re, m_ssm_a_im=m_ssm_a_im, m_ssm_log_dt=m_ssm_log_dt, m_ssm_b_re=m_ssm_b_re, m_ssm_b_im=m_ssm_b_im, m_ssm_c_re=m_ssm_c_re, m_ssm_c_im=m_ssm_c_im, m_ssm_d=m_ssm_d, m_w_glu=m_w_glu, m_b_glu=m_b_glu, m_na_rpb=m_na_rpb, m_w_out=m_w_out, m_w_ple=m_w_ple, m_ple_norm=m_ple_norm, m_w_ple_gate=m_w_ple_gate, v_norm_pre=v_norm_pre, v_norm_post=v_norm_post, v_w_in=v_w_in, v_ssm_a_re=v_ssm_a_re, v_ssm_a_im=v_ssm_a_im, v_ssm_log_dt=v_ssm_log_dt, v_ssm_b_re=v_ssm_b_re, v_ssm_b_im=v_ssm_b_im, v_ssm_c_re=v_ssm_c_re, v_ssm_c_im=v_ssm_c_im, v_ssm_d=v_ssm_d, v_w_glu=v_w_glu, v_b_glu=v_b_glu, v_na_rpb=v_na_rpb, v_w_out=v_w_out, v_w_ple=v_w_ple, v_ple_norm=v_ple_norm, v_w_ple_gate=v_w_ple_gate)
    weights = {n: given[n] for n in TWIN_WEIGHTS}
    shared = {n: given[n] for n in SHARED_INPUTS}
    per_example = {n: given[n] for n in ['x', 'p']}
    grad_fn = _jax.value_and_grad(_loss, argnums=(0, 1))

    def one_microbatch(ex, loss_target):
        ex = dict(ex)
        diff = ex.pop(TWIN_DIFF_INPUT)
        return grad_fn(weights, diff, {**shared, **ex}, loss_target)

    if N_MICROBATCH == 1:
        loss, (grad_w, grad_x) = one_microbatch(per_example, given["loss_target"])
    else:
        def body(carry, xs):
            loss_sum, grad_sum = carry
            l_k, (gw_k, gx_k) = one_microbatch(xs[0], xs[1])
            with _jax.named_scope("update"):
                return (loss_sum + l_k, _jax.tree.map(_jnp.add, grad_sum, gw_k)), gx_k

        init = (_jnp.zeros((), _jnp.float32), _jax.tree.map(_jnp.zeros_like, weights))
        (loss, grad_w), grad_x = _jax.lax.scan(body, init, (per_example, given["loss_target"]))
    with _jax.named_scope("update"):
        delta_w, new_m, new_v = {}, {}, {}
        for n in TWIN_WEIGHTS:
            delta_w[n], new_m[n], new_v[n] = _adamw(weights[n], grad_w[n], given["m_" + n], given["v_" + n])
    return (loss, grad_x, *[grad_w[n] for n in TWIN_WEIGHTS], *[delta_w[n] for n in TWIN_WEIGHTS],
            *[new_m[n] for n in TWIN_WEIGHTS], *[new_v[n] for n in TWIN_WEIGHTS])
```

```python
import functools

import jax
import jax.numpy as jnp
from jax import lax
from jax.experimental import pallas as pl
from jax.experimental.pallas import tpu as pltpu

F32 = jnp.float32
BF16 = jnp.bfloat16
HIGHEST = lax.Precision.HIGHEST

D_MODEL = 1024
D_PLE = 256
GRID_W = 64
D_SSM = 512
SSM_GROUP = 16
N_GROUPS = 32
N_PAIRS = 16
SSM_STATE = 64
D_NA = 512
NA_HEADS = 8
NA_HEAD_DIM = 64
NA_ROWS = 8
NA_COLS = 16
D_IN_PROJ = 3072
EPS = 1e-6
N_DEV = 8
SHARD_IN = D_IN_PROJ // N_DEV
LANES = 128
SSM_CHUNK = 16
TOK_TILE = 256
NA_QROWS = 4
NA_KROWS = 12
NEG = -1e30
VMEM_LIMIT = 56 * 1024 * 1024

ADAM_LR = 0.001
ADAM_B1 = 0.9
ADAM_B2 = 0.999
ADAM_EPS = 1e-08
ADAM_WD = 0.01
ADAM_STEP = 10

MESH = pl.DeviceIdType.MESH


def _params(*sem):
    return pltpu.CompilerParams(dimension_semantics=sem or None, vmem_limit_bytes=VMEM_LIMIT)


def _whole_vmem():
    return pl.BlockSpec(memory_space=pltpu.VMEM)


def _nt(a, b):
    return lax.dot_general(a, b, (((1,), (1,)), ((), ())), preferred_element_type=F32)


def _tn(a, b):
    return lax.dot_general(a, b, (((0,), (0,)), ((), ())), preferred_element_type=F32)


def _mm(a, b):
    return jnp.dot(a, b, preferred_element_type=F32)


def _sigmoid(x):
    return 1.0 / (1.0 + jnp.exp(-x))


_GELU_C = 0.7978845608028654


def _gelu(x):
    return 0.5 * x * (1.0 + jnp.tanh(_GELU_C * (x + 0.044715 * x * x * x)))


def _gelu_grad(x):
    th = jnp.tanh(_GELU_C * (x + 0.044715 * x * x * x))
    return 0.5 * (1.0 + th) + 0.5 * x * (1.0 - th * th) * _GELU_C * (1.0 + 3.0 * 0.044715 * x * x)


def _ssm_build(a_re, a_im, log_dt, b_re, b_im, c_re, c_im, d):
    T, G, P, H = SSM_CHUNK, N_GROUPS, SSM_STATE, SSM_GROUP
    dt = jnp.exp(log_dt)[..., None]
    xr = dt * a_re
    xi = dt * a_im
    mag = jnp.exp(xr)
    lr = mag * jnp.cos(xi)
    li = mag * jnp.sin(xi)
    den = a_re * a_re + a_im * a_im
    cr = ((lr - 1.0) * a_re + li * a_im) / den
    ci = (li * a_re - (lr - 1.0) * a_im) / den
    bbr = cr[..., None] * b_re - ci[..., None] * b_im
    bbi = cr[..., None] * b_im + ci[..., None] * b_re
    kk = jnp.arange(T + 1, dtype=F32)[:, None, None, None]
    pm = jnp.exp(kk * xr)
    pr = pm * jnp.cos(kk * xi)
    pi = pm * jnp.sin(kk * xi)
    wr = pr[..., None] * bbr[None] - pi[..., None] * bbi[None]
    wi = pr[..., None] * bbi[None] + pi[..., None] * bbr[None]
    kern = (jnp.einsum('dgap,kdgpb->kdgab', c_re, wr[:T], precision=HIGHEST)
            - jnp.einsum('dgap,kdgpb->kdgab', c_im, wi[:T], precision=HIGHEST))
    s_idx = jnp.arange(T)[None, :, None]
    t_idx = jnp.arange(T)[None, None, :]
    k_idx = jnp.arange(T)[:, None, None]
    sel0 = (t_idx - s_idx == k_idx).astype(F32)
    sel1 = (s_idx - t_idx == k_idx).astype(F32)
    toep = (jnp.einsum('kst,kgab->gsbta', sel0, kern[:, 0], precision=HIGHEST)
            + jnp.einsum('kst,kgab->gsbta', sel1, kern[:, 1], precision=HIGHEST)
            + jnp.einsum('st,ab,ga->gsbta', jnp.eye(T, dtype=F32), jnp.eye(H, dtype=F32), d, precision=HIGHEST))
    toep = toep.reshape(N_PAIRS, 2, T * H, T * H)

    eye2 = jnp.eye(2, dtype=F32)
    m_q = jnp.stack([wr[:T][::-1][:, 0], wi[:T][::-1][:, 0], wr[:T][:, 1], wi[:T][:, 1]], axis=0)
    m_g = m_q.reshape(4, T, N_PAIRS, 2, P, H)
    mmat = jnp.einsum('qsrjph,jk->rjshqkp', m_g, eye2, precision=HIGHEST).reshape(N_PAIRS, 2 * T * H, 4 * 2 * P)

    def c_times_pow(pw_r, pw_i, dirn):
        cre = c_re[dirn][None]
        cim = c_im[dirn][None]
        return (cre * pw_r[:, :, None, :] - cim * pw_i[:, :, None, :],
                cre * pw_i[:, :, None, :] + cim * pw_r[:, :, None, :])

    w0r, w0i = c_times_pow(pr[1:, 0], pi[1:, 0], 0)
    w1r, w1i = c_times_pow(pr[1:, 1][::-1], pi[1:, 1][::-1], 1)
    c_q = jnp.stack([w0r, -w0i, w1r, -w1i], axis=0)
    c_g = c_q.reshape(4, T, N_PAIRS, 2, H, P)
    cmat = jnp.einsum('qtrjap,jk->rqjpkta', c_g, eye2, precision=HIGHEST).reshape(N_PAIRS, 4 * 2 * P, 2 * T * H)

    lam_q = jnp.stack([pr[T, 0], pi[T, 0], pr[T, 1], pi[T, 1]], axis=0)
    lam = lam_q.reshape(4, N_PAIRS, 2 * P).transpose(1, 0, 2)
    lam = jnp.concatenate([lam, jnp.zeros_like(lam)], axis=1)
    return toep, mmat, cmat, lam


def _na_bias(rpb, rows):
    ri = jnp.arange(NA_QROWS)[:, None, None]
    kr = jnp.arange(NA_KROWS)[None, :, None]
    dr = jnp.arange(2 * NA_ROWS - 1)[None, None, :]
    r0 = ((kr <= 7) & (kr - ri + 7 == dr))
    r1 = ((kr >= ri) & (kr <= ri + 7) & (kr - ri + 3 == dr))
    r2 = ((kr >= 4) & (kr - ri - 1 == dr))
    rsel = jnp.stack([r0, r1, r2], axis=0).astype(F32)
    qc = jnp.arange(GRID_W)[:, None, None]
    kc = jnp.arange(GRID_W)[None, :, None]
    dc = jnp.arange(2 * NA_COLS - 1)[None, None, :]
    cstart = jnp.clip(qc - NA_COLS // 2, 0, GRID_W - NA_COLS)
    csel = ((kc >= cstart) & (kc < cstart + NA_COLS) & (kc - qc + NA_COLS - 1 == dc)).astype(F32)
    part = jnp.einsum('hrc,qjc->hrqj', rpb, csel, precision=HIGHEST)
    core = jnp.einsum('hrqj,yikr->hyiqkj', part, rsel, precision=HIGHEST)
    valid = jnp.einsum('yikr,qjc->yiqkj', rsel, csel) > 0.5
    bias = jnp.where(valid[None], core, NEG)
    return bias.reshape(NA_HEADS, 3, NA_QROWS * GRID_W, NA_KROWS * GRID_W)


def _fwd_in(x2, g_pre, w_in_g):
    seq = x2.shape[0]

    def body(x_ref, g_ref, w_ref, u_ref, zs_ref, qkv_ref, zn_ref, hn_ref):
        x = x_ref[...]
        r = lax.rsqrt(jnp.mean(x * x, axis=-1, keepdims=True) + EPS)
        hn = (x * r * g_ref[...]).astype(BF16)
        hn_ref[...] = hn
        for j in range(N_DEV):
            pj = _mm(hn, w_ref[j])
            for i in range(SHARD_IN // LANES):
                blk = (SHARD_IN // LANES) * j + i
                piece = pj[:, i * LANES:(i + 1) * LANES]
                if blk < 4:
                    u_ref[:, blk * LANES:(blk + 1) * LANES] = piece
                elif blk < 8:
                    zs_ref[:, (blk - 4) * LANES:(blk - 3) * LANES] = piece
                elif blk < 20:
                    qkv_ref[:, (blk - 8) * LANES:(blk - 7) * LANES] = piece.astype(BF16)
                else:
                    zn_ref[:, (blk - 20) * LANES:(blk - 19) * LANES] = piece

    tok = lambda w: pl.BlockSpec((TOK_TILE, w), lambda i: (i, 0))
    return pl.pallas_call(
        body, name="fwd_in", grid=(seq // TOK_TILE,),
        in_specs=[tok(D_MODEL), pl.BlockSpec((1, D_MODEL), lambda i: (0, 0)), _whole_vmem()],
        out_specs=[tok(D_SSM), tok(D_SSM), tok(3 * D_NA), tok(D_NA), tok(D_MODEL)],
        out_shape=[jax.ShapeDtypeStruct((seq, D_SSM), F32), jax.ShapeDtypeStruct((seq, D_SSM), F32),
                   jax.ShapeDtypeStruct((seq, 3 * D_NA), BF16), jax.ShapeDtypeStruct((seq, D_NA), F32),
                   jax.ShapeDtypeStruct((seq, D_MODEL), BF16)],
        compiler_params=_params("arbitrary"),
    )(x2, g_pre, w_in_g)


def _mid_fwd(yssm, zs, o, zn, x2, p2, tgt, w_glu, b_glu, w_out, g_post, w_ple_g, g_ple, w_pg):
    seq = x2.shape[0]

    def body(yssm_ref, zs_ref, o_ref, zn_ref, x_ref, p_ref, tgt_ref, wglu_ref, bglu_ref, wout_ref, gpost_ref,
             wple_ref, gple_ref, wpg_ref,
             dcat_ref, dh1_ref, loss_ref, dwout_ref, dwple_ref, dwpg_ref, dgpost_ref, dgple_ref):
        @pl.when(pl.program_id(0) == 0)
        def _():
            loss_ref[...] = jnp.zeros_like(loss_ref)
            dwout_ref[...] = jnp.zeros_like(dwout_ref)
            dwple_ref[...] = jnp.zeros_like(dwple_ref)
            dwpg_ref[...] = jnp.zeros_like(dwpg_ref)
            dgpost_ref[...] = jnp.zeros_like(dgpost_ref)
            dgple_ref[...] = jnp.zeros_like(dgple_ref)

        g1 = _gelu(yssm_ref[...])
        t = _mm(g1.astype(BF16), wglu_ref[...]) + bglu_ref[...]
        zs_v = zs_ref[...]
        ys = g1 * _sigmoid(t) * (zs_v * _sigmoid(zs_v))
        zn_v = zn_ref[...]
        yn = o_ref[...] * (zn_v * _sigmoid(zn_v))
        cat = jnp.concatenate([ys, yn], axis=1).astype(BF16)
        mix = _mm(cat, wout_ref[...])
        r2 = lax.rsqrt(jnp.mean(mix * mix, axis=-1, keepdims=True) + EPS)
        n2 = mix * r2
        gpost = gpost_ref[...]
        h1 = x_ref[...] + n2 * gpost
        pb = p_ref[...].astype(BF16)
        epre = jnp.concatenate([_mm(pb, wple_ref[j]) for j in range(N_DEV)], axis=1)
        r3 = lax.rsqrt(jnp.mean(epre * epre, axis=-1, keepdims=True) + EPS)
        n3 = epre * r3
        gple = gple_ref[...]
        e = n3 * gple
        h1b = h1.astype(BF16)
        gate = _sigmoid(_mm(h1b, wpg_ref[...]))
        diff = h1 + gate * e - tgt_ref[...]
        loss_ref[...] += jnp.sum(diff * diff, axis=0, keepdims=True)

        dy = diff * (1.0 / D_MODEL)
        dgp = (dy * e * gate * (1.0 - gate)).astype(BF16)
        de = dy * gate
        dh1 = dy + _nt(dgp, wpg_ref[...])
        dh1_ref[...] = dh1
        dwpg_ref[...] += _tn(h1b, dgp)
        dgple_ref[...] += jnp.sum(de * n3, axis=0, keepdims=True)
        dn3 = de * gple
        depre = (r3 * (dn3 - n3 * jnp.mean(dn3 * n3, axis=-1, keepdims=True))).astype(BF16)
        for j in range(N_DEV):
            dwple_ref[j] += _tn(pb, depre[:, j * LANES:(j + 1) * LANES])
        dgpost_ref[...] += jnp.sum(dh1 * n2, axis=0, keepdims=True)
        dn2 = dh1 * gpost
        dmix = (r2 * (dn2 - n2 * jnp.mean(dn2 * n2, axis=-1, keepdims=True))).astype(BF16)
        dcat_ref[...] = _nt(dmix, wout_ref[...])
        dwout_ref[...] += _tn(cat, dmix)

    tok = lambda w: pl.BlockSpec((TOK_TILE, w), lambda i: (i, 0))
    row = lambda w: pl.BlockSpec((1, w), lambda i: (0, 0))
    vm = _whole_vmem()
    return pl.pallas_call(
        body, name="mid_fwd", grid=(seq // TOK_TILE,),
        in_specs=[tok(D_SSM), tok(D_SSM), tok(D_NA), tok(D_NA), tok(D_MODEL), tok(D_PLE), tok(D_MODEL),
                  vm, row(D_SSM), vm, row(D_MODEL), vm, row(D_MODEL), vm],
        out_specs=[tok(D_MODEL), tok(D_MODEL), vm, vm, vm, vm, vm, vm],
        out_shape=[jax.ShapeDtypeStruct((seq, D_MODEL), F32), jax.ShapeDtypeStruct((seq, D_MODEL), F32),
                   jax.ShapeDtypeStruct((1, D_MODEL), F32),
                   jax.ShapeDtypeStruct((D_MODEL, D_MODEL), F32),
                   jax.ShapeDtypeStruct((N_DEV, D_PLE, LANES), F32),
                   jax.ShapeDtypeStruct((D_MODEL, D_MODEL), F32),
                   jax.ShapeDtypeStruct((1, D_MODEL), F32), jax.ShapeDtypeStruct((1, D_MODEL), F32)],
        compiler_params=_params("arbitrary"),
    )(yssm, zs, o, zn, x2, p2, tgt, w_glu, b_glu, w_out, g_post, w_ple_g, g_ple, w_pg)


def _mid_bwd(yssm, zs, o, zn, dcat, w_glu, b_glu):
    seq = yssm.shape[0]

    def body(yssm_ref, zs_ref, o_ref, zn_ref, dcat_ref, wglu_ref, bglu_ref,
             dyssm_ref, dzs_ref, do_ref, dzn_ref, dwglu_ref, dbglu_ref):
        @pl.when(pl.program_id(0) == 0)
        def _():
            dwglu_ref[...] = jnp.zeros_like(dwglu_ref)
            dbglu_ref[...] = jnp.zeros_like(dbglu_ref)

        dys = dcat_ref[:, :D_SSM]
        dyn = dcat_ref[:, D_SSM:]
        yv = yssm_ref[...]
        g1 = _gelu(yv)
        g1b = g1.astype(BF16)
        sg = _sigmoid(_mm(g1b, wglu_ref[...]) + bglu_ref[...])
        zs_v = zs_ref[...]
        s_zs = _sigmoid(zs_v)
        dg2 = dys * (zs_v * s_zs)
        dzs_ref[...] = dys * (g1 * sg) * (s_zs * (1.0 + zs_v * (1.0 - s_zs)))
        dt = dg2 * g1 * sg * (1.0 - sg)
        dtb = dt.astype(BF16)
        dg1 = dg2 * sg + _nt(dtb, wglu_ref[...])
        dwglu_ref[...] += _tn(g1b, dtb)
        dbglu_ref[...] += jnp.sum(dt, axis=0, keepdims=True)
        dyssm_ref[...] = dg1 * _gelu_grad(yv)
        zn_v = zn_ref[...]
        s_zn = _sigmoid(zn_v)
        do_ref[...] = dyn * (zn_v * s_zn)
        dzn_ref[...] = dyn * o_ref[...] * (s_zn * (1.0 + zn_v * (1.0 - s_zn)))

    tok = lambda w: pl.BlockSpec((TOK_TILE, w), lambda i: (i, 0))
    vm = _whole_vmem()
    half = jax.ShapeDtypeStruct((seq, D_SSM), F32)
    return pl.pallas_call(
        body, name="mid_bwd", grid=(seq // TOK_TILE,),
        in_specs=[tok(D_SSM), tok(D_SSM), tok(D_NA), tok(D_NA), tok(D_MODEL), vm,
                  pl.BlockSpec((1, D_SSM), lambda i: (0, 0))],
        out_specs=[tok(D_SSM), tok(D_SSM), tok(D_NA), tok(D_NA), vm, vm],
        out_shape=[half, half, half, half, jax.ShapeDtypeStruct((D_SSM, D_SSM), F32),
                   jax.ShapeDtypeStruct((1, D_SSM), F32)],
        compiler_params=_params("arbitrary"),
    )(yssm, zs, o, zn, dcat, w_glu, b_glu)


def _bwd_in(du, dzs, dq, dk, dv, dzn, hn, x2, dh1, g_pre, w_in_g):
    seq = x2.shape[0]

    def body(du_ref, dzs_ref, dq_ref, dk_ref, dv_ref, dzn_ref, hn_ref, x_ref, dh1_ref, g_ref, w_ref,
             dx_ref, dw_ref, dg_ref, dproj_ref):
        @pl.when(pl.program_id(0) == 0)
        def _():
            dw_ref[...] = jnp.zeros_like(dw_ref)
            dg_ref[...] = jnp.zeros_like(dg_ref)

        for k, ref in enumerate((du_ref, dzs_ref, dq_ref, dk_ref, dv_ref, dzn_ref)):
            dproj_ref[:, k * D_SSM:(k + 1) * D_SSM] = ref[...].astype(BF16)
        hn = hn_ref[...]
        dhn = jnp.zeros((TOK_TILE, D_MODEL), F32)
        for j in range(N_DEV):
            dpj = dproj_ref[:, j * SHARD_IN:(j + 1) * SHARD_IN]
            dhn += _nt(dpj, w_ref[j])
            dw_ref[j] += _tn(hn, dpj)
        x = x_ref[...]
        r = lax.rsqrt(jnp.mean(x * x, axis=-1, keepdims=True) + EPS)
        n1 = x * r
        dg_ref[...] += jnp.sum(dhn * n1, axis=0, keepdims=True)
        dn1 = dhn * g_ref[...]
        dx_ref[...] = dh1_ref[...] + r * (dn1 - n1 * jnp.mean(dn1 * n1, axis=-1, keepdims=True))

    tok = lambda w: pl.BlockSpec((TOK_TILE, w), lambda i: (i, 0))
    vm = _whole_vmem()
    return pl.pallas_call(
        body, name="bwd_in", grid=(seq // TOK_TILE,),
        in_specs=[tok(D_SSM)] * 6 + [tok(D_MODEL), tok(D_MODEL), tok(D_MODEL),
                                      pl.BlockSpec((1, D_MODEL), lambda i: (0, 0)), vm],
        out_specs=[tok(D_MODEL), vm, vm],
        out_shape=[jax.ShapeDtypeStruct((seq, D_MODEL), F32),
                   jax.ShapeDtypeStruct((N_DEV, D_MODEL, SHARD_IN), F32),
                   jax.ShapeDtypeStruct((1, D_MODEL), F32)],
        scratch_shapes=[pltpu.VMEM((TOK_TILE, D_IN_PROJ), BF16)],
        compiler_params=_params("arbitrary"),
    )(du, dzs, dq, dk, dv, dzn, hn, x2, dh1, g_pre, w_in_g)


RELAYOUT_CHUNKS = 128
PAIRS_PER_BLOCK = LANES // (2 * SSM_GROUP)


def _lane_window(lo, width):
    lane = lax.broadcasted_iota(jnp.int32, (1, LANES), 1)
    return (lane >= lo) & (lane < lo + width)


def _to_pairs(a, nc):
    ncb = min(RELAYOUT_CHUNKS, nc)

    def body(x_ref, out_ref):
        xs = [x_ref[pl.ds(s, ncb, stride=SSM_CHUNK), :] for s in range(SSM_CHUNK)]
        for a_ in range(PAIRS_PER_BLOCK):
            for j in range(2):
                for half in range(2):
                    acc = None
                    for i in range(8):
                        shift = (SSM_GROUP * i - (2 * SSM_GROUP * a_ + SSM_GROUP * j)) % LANES
                        piece = xs[8 * half + i]
                        piece = pltpu.roll(piece, shift, axis=1) if shift else piece
                        acc = piece if acc is None else jnp.where(_lane_window(SSM_GROUP * i, SSM_GROUP), piece, acc)
                    col = 256 * j + LANES * half
                    out_ref[a_, :, col:col + LANES] = acc.astype(BF16)

    return pl.pallas_call(
        body, name="to_pairs", grid=(D_SSM // LANES, nc // ncb),
        in_specs=[pl.BlockSpec((ncb * SSM_CHUNK, LANES), lambda cl, cb: (cb, cl))],
        out_specs=pl.BlockSpec((PAIRS_PER_BLOCK, ncb, 512), lambda cl, cb: (cl, cb, 0)),
        out_shape=jax.ShapeDtypeStruct((N_PAIRS, nc, 512), BF16),
        compiler_params=_params("arbitrary", "arbitrary"),
    )(a)


def _from_pairs(a, nc):
    ncb = min(RELAYOUT_CHUNKS, nc)

    def body(y_ref, out_ref):
        for s in range(SSM_CHUNK):
            half, i = divmod(s, 8)
            acc = None
            for a_ in range(PAIRS_PER_BLOCK):
                for j in range(2):
                    col = 256 * j + LANES * half
                    dst = 2 * SSM_GROUP * a_ + SSM_GROUP * j
                    shift = (dst - SSM_GROUP * i) % LANES
                    piece = y_ref[a_, :, col:col + LANES]
                    piece = pltpu.roll(piece, shift, axis=1) if shift else piece
                    acc = piece if acc is None else jnp.where(_lane_window(dst, SSM_GROUP), piece, acc)
            out_ref[pl.ds(s, ncb, stride=SSM_CHUNK), :] = acc

    return pl.pallas_call(
        body, name="from_pairs", grid=(D_SSM // LANES, nc // ncb),
        in_specs=[pl.BlockSpec((PAIRS_PER_BLOCK, ncb, 512), lambda cl, cb: (cl, cb, 0))],
        out_specs=pl.BlockSpec((ncb * SSM_CHUNK, LANES), lambda cl, cb: (cb, cl)),
        out_shape=jax.ShapeDtypeStruct((nc * SSM_CHUNK, D_SSM), F32),
        compiler_params=_params("arbitrary", "arbitrary"),
    )(a)


def _boundary_scan(nc, lam_ref, src_ref, dst_ref, conj, shift):
    nblk = nc // 8
    lr0, li0, lr1, li1 = (lam_ref[0, q:q + 1, :] for q in range(4))
    if conj:
        li0, li1 = -li0, -li1

    def step(i, carry):
        hr0, hi0, hr1, hi1 = carry
        up = pl.multiple_of(i * 8, 8)
        dn = pl.multiple_of((nblk - 1 - i) * 8, 8)
        a_rows, b_rows = (dn, up) if conj else (up, dn)
        s_r0 = src_ref[pl.ds(a_rows, 8), 0:128]
        s_i0 = src_ref[pl.ds(a_rows, 8), 128:256]
        s_r1 = src_ref[pl.ds(b_rows, 8), 256:384]
        s_i1 = src_ref[pl.ds(b_rows, 8), 384:512]
        o_r0, o_i0, o_r1, o_i1 = [], [], [], []
        for k in range(8):
            ka = 7 - k if conj else k
            kb = k if conj else 7 - k
            o_r0.append((ka, hr0))
            o_i0.append((ka, hi0))
            o_r1.append((kb, hr1))
            o_i1.append((kb, hi1))
            hr0, hi0 = (lr0 * hr0 - li0 * hi0 + s_r0[ka:ka + 1], lr0 * hi0 + li0 * hr0 + s_i0[ka:ka + 1])
            hr1, hi1 = (lr1 * hr1 - li1 * hi1 + s_r1[kb:kb + 1], lr1 * hi1 + li1 * hr1 + s_i1[kb:kb + 1])

        def rows(items):
            return jnp.concatenate([v for _, v in sorted(items, key=lambda kv: kv[0])], axis=0)

        dst_ref[pl.ds(a_rows, 8), 0:128] = rows(o_r0)
        dst_ref[pl.ds(a_rows, 8), 128:256] = rows(o_i0)
        dst_ref[pl.ds(b_rows, 8), 256:384] = rows(o_r1)
        dst_ref[pl.ds(b_rows, 8), 384:512] = rows(o_i1)
        return hr0, hi0, hr1, hi1

    z = jnp.zeros((1, LANES), F32)
    del shift
    lax.fori_loop(0, nblk, step, (z, z, z, z))


def _ssm_fwd(u, toep, mmat, cmat, lam):
    npair, nc, width = u.shape
    half = width // 2

    def body(u_ref, toep_ref, m_ref, c_ref, lam_ref, y_ref, hin_ref, s_scr, h_scr):
        uv = u_ref[0]
        s_scr[...] = _mm(uv, m_ref[0])
        _boundary_scan(nc, lam_ref, s_scr, h_scr, conj=False, shift=True)
        hin = h_scr[...]
        hin_ref[0] = hin
        inter = _mm(hin.astype(BF16), c_ref[0])
        y_ref[0, :, 0:half] = _mm(uv[:, 0:half], toep_ref[0, 0]) + inter[:, 0:half]
        y_ref[0, :, half:width] = _mm(uv[:, half:width], toep_ref[0, 1]) + inter[:, half:width]

    per = lambda *shape: pl.BlockSpec((1,) + shape, lambda g: (g,) + (0,) * len(shape))
    return pl.pallas_call(
        body, name="ssm_fwd", grid=(npair,),
        in_specs=[per(nc, width), per(2, half, half), per(width, width), per(width, width), per(8, LANES)],
        out_specs=[per(nc, width), per(nc, width)],
        out_shape=[jax.ShapeDtypeStruct((npair, nc, width), F32), jax.ShapeDtypeStruct((npair, nc, width), F32)],
        scratch_shapes=[pltpu.VMEM((nc, width), F32), pltpu.VMEM((nc, width), F32)],
        compiler_params=_params("arbitrary"),
    )(u, toep, mmat, cmat, lam)


def _ssm_bwd(u, dy, hin, toep, mmat, cmat, lam):
    npair, nc, width = u.shape
    half = width // 2

    def body(u_ref, dy_ref, hin_ref, toep_ref, m_ref, c_ref, lam_ref,
             du_ref, dtoep_ref, dm_ref, dc_ref, dlam_ref, dh_scr, ds_scr):
        uv = u_ref[0]
        dyb = dy_ref[0]
        hin = hin_ref[0]
        dh_scr[...] = _nt(dyb, c_ref[0])
        _boundary_scan(nc, lam_ref, dh_scr, ds_scr, conj=True, shift=True)
        ds = ds_scr[...]
        dsb = ds.astype(BF16)
        du = _nt(dsb, m_ref[0])
        du_ref[0, :, 0:half] = du[:, 0:half] + _nt(dyb[:, 0:half], toep_ref[0, 0])
        du_ref[0, :, half:width] = du[:, half:width] + _nt(dyb[:, half:width], toep_ref[0, 1])
        dtoep_ref[0, 0] = _tn(uv[:, 0:half], dyb[:, 0:half])
        dtoep_ref[0, 1] = _tn(uv[:, half:width], dyb[:, half:width])
        dm_ref[0] = _tn(uv, dsb)
        dc_ref[0] = _tn(hin.astype(BF16), dyb)
        rows = []
        for d in range(2):
            g_r = ds[:, 256 * d:256 * d + 128]
            g_i = ds[:, 256 * d + 128:256 * d + 256]
            h_r = hin[:, 256 * d:256 * d + 128]
            h_i = hin[:, 256 * d + 128:256 * d + 256]
            rows.append(jnp.sum(g_r * h_r + g_i * h_i, axis=0, keepdims=True))
            rows.append(jnp.sum(g_i * h_r - g_r * h_i, axis=0, keepdims=True))
        rows.append(jnp.zeros((4, LANES), F32))
        dlam_ref[0] = jnp.concatenate(rows, axis=0)

    per = lambda *shape: pl.BlockSpec((1,) + shape, lambda g: (g,) + (0,) * len(shape))
    return pl.pallas_call(
        body, name="ssm_bwd", grid=(npair,),
        in_specs=[per(nc, width), per(nc, width), per(nc, width), per(2, half, half), per(width, width),
                  per(width, width), per(8, LANES)],
        out_specs=[per(nc, width), per(2, half, half), per(width, width), per(width, width), per(8, LANES)],
        out_shape=[jax.ShapeDtypeStruct((npair, nc, width), F32),
                   jax.ShapeDtypeStruct((npair, 2, half, half), F32),
                   jax.ShapeDtypeStruct((npair, width, width), F32),
                   jax.ShapeDtypeStruct((npair, width, width), F32),
                   jax.ShapeDtypeStruct((npair, 8, LANES), F32)],
        scratch_shapes=[pltpu.VMEM((nc, width), F32), pltpu.VMEM((nc, width), F32)],
        compiler_params=_params("arbitrary"),
    )(u, dy, hin, toep, mmat, cmat, lam)


NA_Q = NA_QROWS * GRID_W
NA_K = NA_KROWS * GRID_W
NA_SCALE = NA_HEAD_DIM ** -0.5


def _na_block(b, nb, rows):
    start = jnp.clip(NA_QROWS * b - NA_ROWS // 2, 0, rows - NA_KROWS) * GRID_W
    kind = jnp.where(b == 0, 0, jnp.where(b == nb - 1, 2, 1))
    return pl.multiple_of(start, GRID_W), kind


def _na_probs(q2, kw, bias, in_head):
    qh = jnp.where(in_head, q2, jnp.zeros_like(q2))
    s = _nt(qh, kw) * NA_SCALE + bias
    p = jnp.exp(s - jnp.max(s, axis=-1, keepdims=True))
    return p * (1.0 / jnp.sum(p, axis=-1, keepdims=True))


def _na_fwd(qkv, bias):
    seq = qkv.shape[0]
    rows = seq // GRID_W
    nb = rows // NA_QROWS

    def body(q_ref, k_ref, v_ref, bias_ref, o_ref):
        start, kind = _na_block(pl.program_id(1), nb, rows)
        q2 = q_ref[...]
        kw = k_ref[pl.ds(start, NA_K), :]
        vw = v_ref[pl.ds(start, NA_K), :]
        lane = lax.broadcasted_iota(jnp.int32, (1, LANES), 1)
        out = None
        for hh in range(2):
            in_head = (lane < NA_HEAD_DIM) if hh == 0 else (lane >= NA_HEAD_DIM)
            p = _na_probs(q2, kw, bias_ref[hh, kind], in_head)
            oh = _mm(p.astype(BF16), vw)
            out = oh if out is None else jnp.where(in_head, oh, out)
        o_ref[...] = out

    return pl.pallas_call(
        body, name="na_fwd", grid=(NA_HEADS // 2, nb),
        in_specs=[pl.BlockSpec((NA_Q, LANES), lambda hp, b: (b, hp)),
                  pl.BlockSpec((seq, LANES), lambda hp, b: (0, 4 + hp)),
                  pl.BlockSpec((seq, LANES), lambda hp, b: (0, 8 + hp)),
                  pl.BlockSpec((2, 3, NA_Q, NA_K), lambda hp, b: (hp, 0, 0, 0))],
        out_specs=pl.BlockSpec((NA_Q, LANES), lambda hp, b: (b, hp)),
        out_shape=jax.ShapeDtypeStruct((seq, D_NA), F32),
        compiler_params=_params("arbitrary", "arbitrary"),
    )(qkv, qkv, qkv, bias)


def _na_bwd(qkv, do, bias):
    seq = qkv.shape[0]
    rows = seq // GRID_W
    nb = rows // NA_QROWS

    def body(q_ref, k_ref, v_ref, do_ref, bias_ref, dq_ref, dk_ref, dv_ref, dbias_ref):
        b = pl.program_id(1)
        start, kind = _na_block(b, nb, rows)

        @pl.when(b == 0)
        def _():
            dk_ref[...] = jnp.zeros_like(dk_ref)
            dv_ref[...] = jnp.zeros_like(dv_ref)
            dbias_ref[...] = jnp.zeros_like(dbias_ref)

        q2 = q_ref[...]
        kw = k_ref[pl.ds(start, NA_K), :]
        vw = v_ref[pl.ds(start, NA_K), :]
        do2 = do_ref[...].astype(BF16)
        lane = lax.broadcasted_iota(jnp.int32, (1, LANES), 1)
        dq = None
        dkw = None
        dvw = None
        for hh in range(2):
            in_head = (lane < NA_HEAD_DIM) if hh == 0 else (lane >= NA_HEAD_DIM)
            p = _na_probs(q2, kw, bias_ref[hh, kind], in_head)
            doh = jnp.where(in_head, do2, jnp.zeros_like(do2))
            dp = _nt(doh, vw)
            ds = p * (dp - jnp.sum(p * dp, axis=-1, keepdims=True))
            dbias_ref[hh, kind] += ds
            dsb = (ds * NA_SCALE).astype(BF16)
            dq_h = _mm(dsb, kw)
            dk_h = _tn(dsb, q2)
            dv_h = _tn(p.astype(BF16), do2)
            dq = dq_h if dq is None else jnp.where(in_head, dq_h, dq)
            dkw = dk_h if dkw is None else jnp.where(in_head, dk_h, dkw)
            dvw = dv_h if dvw is None else jnp.where(in_head, dv_h, dvw)
        dq_ref[...] = dq
        dk_ref[pl.ds(start, NA_K), :] += dkw
        dv_ref[pl.ds(start, NA_K), :] += dvw

    return pl.pallas_call(
        body, name="na_bwd", grid=(NA_HEADS // 2, nb),
        in_specs=[pl.BlockSpec((NA_Q, LANES), lambda hp, b: (b, hp)),
                  pl.BlockSpec((seq, LANES), lambda hp, b: (0, 4 + hp)),
                  pl.BlockSpec((seq, LANES), lambda hp, b: (0, 8 + hp)),
                  pl.BlockSpec((NA_Q, LANES), lambda hp, b: (b, hp)),
                  pl.BlockSpec((2, 3, NA_Q, NA_K), lambda hp, b: (hp, 0, 0, 0))],
        out_specs=[pl.BlockSpec((NA_Q, LANES), lambda hp, b: (b, hp)),
                   pl.BlockSpec((seq, LANES), lambda hp, b: (0, hp)),
                   pl.BlockSpec((seq, LANES), lambda hp, b: (0, hp)),
                   pl.BlockSpec((2, 3, NA_Q, NA_K), lambda hp, b: (hp, 0, 0, 0))],
        out_shape=[jax.ShapeDtypeStruct((seq, D_NA), F32), jax.ShapeDtypeStruct((seq, D_NA), F32),
                   jax.ShapeDtypeStruct((seq, D_NA), F32),
                   jax.ShapeDtypeStruct((NA_HEADS, 3, NA_Q, NA_K), F32)],
        compiler_params=_params("arbitrary", "arbitrary"),
    )(qkv, qkv, qkv, do, bias)


def _local_step(x2, p2, tgt, g_pre, g_post, w_in_g, ssm, w_glu, b_glu, rpb, w_out, w_ple_g, g_ple, w_pg):
    seq = x2.shape[0]
    nc = seq // SSM_CHUNK
    rows = seq // GRID_W

    (toep, mmat, cmat, lam), ssm_vjp = jax.vjp(_ssm_build, *ssm)
    bias, bias_vjp = jax.vjp(functools.partial(_na_bias, rows=rows), rpb)
    toep_b, mmat_b, cmat_b = toep.astype(BF16), mmat.astype(BF16), cmat.astype(BF16)

    u, zs, qkv, zn, hn = _fwd_in(x2, g_pre, w_in_g)
    u_p = _to_pairs(u, nc)
    y_p, hin = _ssm_fwd(u_p, toep_b, mmat_b, cmat_b, lam)
    yssm = _from_pairs(y_p, nc)
    o = _na_fwd(qkv, bias)
    dcat, dh1, sq, d_wout, d_wple, d_wpg, d_gpost, d_gple = _mid_fwd(
        yssm, zs, o, zn, x2, p2, tgt, w_glu, b_glu, w_out, g_post, w_ple_g, g_ple, w_pg)
    dyssm, dzs, do, dzn, d_wglu, d_bglu = _mid_bwd(yssm, zs, o, zn, dcat, w_glu, b_glu)
    dq, dk, dv, dbias = _na_bwd(qkv, do, bias)
    (d_rpb,) = bias_vjp(dbias)
    du_p, dtoep, dmm, dcm, dlam = _ssm_bwd(u_p, _to_pairs(dyssm, nc), hin, toep_b, mmat_b, cmat_b, lam)
    d_ssm = ssm_vjp((dtoep, dmm, dcm, dlam))
    du = _from_pairs(du_p, nc)
    dx, d_win, d_gpre = _bwd_in(du, dzs, dq, dk, dv, dzn, hn, x2, dh1, g_pre, w_in_g)
    return sq, dx, dict(norm_pre=d_gpre, norm_post=d_gpost, w_in=d_win, ssm=d_ssm, w_glu=d_wglu, b_glu=d_bglu,
                        na_rpb=d_rpb, w_out=d_wout, w_ple=d_wple, ple_norm=d_gple, w_ple_gate=d_wpg)


def _place():
    return lax.axis_index("x"), lax.axis_index("y"), lax.axis_index("c")


def _all_gather(shard):
    m_per, n = shard.shape

    def body(x_ref, out_ref, send_sems, recv_sems, local_sem):
        x, y, c = _place()
        me, sibling = (x, y, c), (x, y, 1 - c)
        chips = [(1 - x, y), (x, 1 - y), (1 - x, 1 - y)]

        def rows(px, py, pc):
            return out_ref.at[pl.ds((4 * px + 2 * py + pc) * m_per, m_per), :]

        def copy(k, block, to, src=None):
            return pltpu.make_async_remote_copy(
                src_ref=rows(*block) if src is None else src, dst_ref=rows(*block),
                send_sem=send_sems.at[k], recv_sem=recv_sems.at[k], device_id=to, device_id_type=MESH)

        mine = pltpu.make_async_copy(x_ref, rows(*me), local_sem)
        mine.start()
        first = [copy(0, me, sibling, src=x_ref)]
        first += [copy(1 + j, me, (*chip, c), src=x_ref) for j, chip in enumerate(chips)]
        for cp in first:
            cp.start()
        passed = [copy(4 + j, (*chip, c), sibling) for j, chip in enumerate(chips)]
        for j, chip in enumerate(chips):
            copy(1 + j, (*chip, c), me).wait_recv()
            passed[j].start()
        copy(0, sibling, me).wait_recv()
        for j, chip in enumerate(chips):
            copy(4 + j, (*chip, 1 - c), me).wait_recv()
        for cp in first + passed:
            cp.wait_send()
        mine.wait()

    return pl.pallas_call(
        body, name="all_gather",
        out_shape=jax.ShapeDtypeStruct((N_DEV * m_per, n), shard.dtype),
        in_specs=[_whole_vmem()], out_specs=_whole_vmem(),
        scratch_shapes=[pltpu.SemaphoreType.DMA((7,)), pltpu.SemaphoreType.DMA((7,)), pltpu.SemaphoreType.DMA],
        compiler_params=pltpu.CompilerParams(vmem_limit_bytes=VMEM_LIMIT),
    )(shard)


def _exchange_partials(part):
    _, r, n = part.shape

    def body(part_ref, land_ref, send_sems, recv_sems, local_sem):
        x, y, c = _place()
        me = 4 * x + 2 * y + c
        local = pltpu.make_async_copy(part_ref.at[me], land_ref.at[me], local_sem)
        local.start()
        copies = []
        for k in range(1, N_DEV):
            px, py, pc = x ^ ((k >> 2) & 1), y ^ ((k >> 1) & 1), c ^ (k & 1)
            peer = 4 * px + 2 * py + pc
            cp = pltpu.make_async_remote_copy(
                src_ref=part_ref.at[peer], dst_ref=land_ref.at[me],
                send_sem=send_sems.at[k - 1], recv_sem=recv_sems.at[k - 1],
                device_id=(px, py, pc), device_id_type=MESH)
            cp.start()
            copies.append(cp)
        for cp in copies:
            cp.wait_recv()
        for cp in copies:
            cp.wait_send()
        local.wait()

    anyspace = pl.BlockSpec(memory_space=pl.ANY)
    return pl.pallas_call(
        body, name="exchange_partials",
        out_shape=jax.ShapeDtypeStruct(part.shape, part.dtype),
        in_specs=[anyspace], out_specs=anyspace,
        scratch_shapes=[pltpu.SemaphoreType.DMA((N_DEV - 1,)), pltpu.SemaphoreType.DMA((N_DEV - 1,)),
                        pltpu.SemaphoreType.DMA],
    )(part)


def _sum_slots(land):
    _, r, n = land.shape
    tile = 128
    assert r % tile == 0

    def body(land_ref, out_ref):
        acc = land_ref[0]
        for j in range(1, N_DEV):
            acc = acc + land_ref[j]
        out_ref[...] = acc

    return pl.pallas_call(
        body, name="sum_slots", grid=(r // tile,),
        in_specs=[pl.BlockSpec((N_DEV, tile, n), lambda i: (0, i, 0))],
        out_specs=pl.BlockSpec((tile, n), lambda i: (i, 0)),
        out_shape=jax.ShapeDtypeStruct((r, n), land.dtype),
        compiler_params=_params("arbitrary"),
    )(land)


def _adamw(w, g, m, v):
    r, c = w.shape
    tile = 256 if r % 256 == 0 else r

    def body(w_ref, g_ref, m_ref, v_ref, d_ref, nm_ref, nv_ref):
        g = g_ref[...]
        nm = ADAM_B1 * m_ref[...] + (1.0 - ADAM_B1) * g
        nv = ADAM_B2 * v_ref[...] + (1.0 - ADAM_B2) * (g * g)
        m_hat = nm / (1.0 - ADAM_B1 ** ADAM_STEP)
        v_hat = nv / (1.0 - ADAM_B2 ** ADAM_STEP)
        d_ref[...] = -ADAM_LR * (m_hat / (jnp.sqrt(v_hat) + ADAM_EPS) + ADAM_WD * w_ref[...])
        nm_ref[...] = nm
        nv_ref[...] = nv

    spec = pl.BlockSpec((tile, c), lambda i: (i, 0))
    out = jax.ShapeDtypeStruct((r, c), F32)
    return pl.pallas_call(
        body, name="adamw", grid=(r // tile,), in_specs=[spec] * 4, out_specs=[spec] * 3,
        out_shape=[out, out, out], compiler_params=_params("arbitrary"),
    )(w, g, m, v)


_SLAB = 8 * LANES


def _flat_rows(a):
    flat = a.reshape(-1)
    pad = (-flat.shape[0]) % _SLAB
    if pad:
        flat = jnp.concatenate([flat, jnp.zeros((pad,), flat.dtype)])
    return flat.reshape(-1, LANES)


def _pack(arrays):
    slabs = [_flat_rows(a) for a in arrays]
    return jnp.concatenate(slabs, axis=0), [s.shape[0] for s in slabs]


def _unpack(packed, like, rows):
    out, at = [], 0
    for a, r in zip(like, rows):
        out.append(packed[at:at + r].reshape(-1)[:a.size].reshape(a.shape))
        at += r
    return out


SMALL = ('norm_pre', 'norm_post', 'ssm_a_re', 'ssm_a_im', 'ssm_log_dt', 'ssm_b_re', 'ssm_b_im', 'ssm_c_re',
         'ssm_c_im', 'ssm_d', 'b_glu', 'na_rpb', 'ple_norm')
BIG = ('w_in', 'w_glu', 'w_out', 'w_ple', 'w_ple_gate')
ORDER = ('norm_pre', 'norm_post', 'w_in', 'ssm_a_re', 'ssm_a_im', 'ssm_log_dt', 'ssm_b_re', 'ssm_b_im', 'ssm_c_re',
         'ssm_c_im', 'ssm_d', 'w_glu', 'b_glu', 'na_rpb', 'w_out', 'w_ple', 'ple_norm', 'w_ple_gate')


def kernel(x, p, norm_pre, norm_post, w_in, ssm_a_re, ssm_a_im, ssm_log_dt, ssm_b_re, ssm_b_im, ssm_c_re, ssm_c_im, ssm_d, w_glu, b_glu, na_rpb, w_out, w_ple, ple_norm, w_ple_gate, loss_target, m_norm_pre, m_norm_post, m_w_in, m_ssm_a_re, m_ssm_a_im, m_ssm_log_dt, m_ssm_b_re, m_ssm_b_im, m_ssm_c_re, m_ssm_c_im, m_ssm_d, m_w_glu, m_b_glu, m_na_rpb, m_w_out, m_w_ple, m_ple_norm, m_w_ple_gate, v_norm_pre, v_norm_post, v_w_in, v_ssm_a_re, v_ssm_a_im, v_ssm_log_dt, v_ssm_b_re, v_ssm_b_im, v_ssm_c_re, v_ssm_c_im, v_ssm_d, v_w_glu, v_b_glu, v_na_rpb, v_w_out, v_w_ple, v_ple_norm, v_w_ple_gate):
    args = dict(locals())
    weights = {n: args[n] for n in ORDER}
    mom_m = {n: args["m_" + n] for n in ORDER}
    mom_v = {n: args["v_" + n] for n in ORDER}

    shards = [weights[n][0].astype(BF16) for n in BIG]
    packed, shard_rows = _pack(shards)
    gathered = _all_gather(packed).reshape(N_DEV, -1, LANES)
    full = []
    at = 0
    for s, r in zip(shards, shard_rows):
        full.append(gathered[:, at:at + r].reshape((N_DEV,) + s.shape))
        at += r
    w_in_g, w_glu_g, w_out_g, w_ple_g, w_pg_g = full
    w_glu_f = w_glu_g.reshape(D_SSM, D_SSM)
    w_out_f = w_out_g.reshape(D_MODEL, D_MODEL)
    w_pg_f = w_pg_g.reshape(D_MODEL, D_MODEL)

    ssm = tuple(weights[n][0] for n in ('ssm_a_re', 'ssm_a_im', 'ssm_log_dt', 'ssm_b_re', 'ssm_b_im',
                                        'ssm_c_re', 'ssm_c_im', 'ssm_d'))
    sq, dx, grads = _local_step(x[0], p[0, 0], loss_target[0], norm_pre, norm_post, w_in_g, ssm, w_glu_f, b_glu,
                                na_rpb[0], w_out_f, w_ple_g, ple_norm, w_pg_f)
    loss = lax.psum(0.5 / D_MODEL * jnp.sum(sq), ("x", "y", "c"))

    local = dict(norm_pre=grads['norm_pre'], norm_post=grads['norm_post'], b_glu=grads['b_glu'],
                 na_rpb=grads['na_rpb'][None], ple_norm=grads['ple_norm'])
    for n, g in zip(('ssm_a_re', 'ssm_a_im', 'ssm_log_dt', 'ssm_b_re', 'ssm_b_im', 'ssm_c_re', 'ssm_c_im', 'ssm_d'),
                    grads['ssm']):
        local[n] = g[None]

    big_parts = [grads['w_in'],
                 grads['w_glu'].reshape(N_DEV, D_SSM // N_DEV, D_SSM),
                 grads['w_out'].reshape(N_DEV, D_MODEL // N_DEV, D_MODEL),
                 grads['w_ple'],
                 grads['w_ple_gate'].reshape(N_DEV, D_MODEL // N_DEV, D_MODEL)]
    small_flat, small_rows = _pack([local[n] for n in SMALL])
    pad = (-small_flat.shape[0]) % (128 * N_DEV)
    if pad:
        small_flat = jnp.concatenate([small_flat, jnp.zeros((pad, LANES), F32)], axis=0)
    small_per = small_flat.shape[0] // N_DEV
    slabs = [bp.reshape(N_DEV, -1, LANES) for bp in big_parts] + [small_flat.reshape(N_DEV, small_per, LANES)]
    part = jnp.concatenate(slabs, axis=1)
    reduced = _sum_slots(_exchange_partials(part))
    big_grads = {}
    at = 0
    for n, bp in zip(BIG, big_parts):
        r = bp[0].size // LANES
        big_grads[n] = reduced[at:at + r].reshape(bp.shape[1:])
        at += r
    small_all = _all_gather(reduced[at:at + small_per])

    outs = {}
    for n in BIG:
        w2 = weights[n][0]
        d, nm, nv = _adamw(w2, big_grads[n], mom_m[n][0], mom_v[n][0])
        outs[n] = (big_grads[n][None], d[None], nm[None], nv[None])
    like = [weights[n] for n in SMALL]
    w_s, rows_s = _pack(like)
    m_s, _ = _pack([mom_m[n] for n in SMALL])
    v_s, _ = _pack([mom_v[n] for n in SMALL])
    g_s = small_all[:w_s.shape[0]]
    d_s, nm_s, nv_s = _adamw(w_s, g_s, m_s, v_s)
    unpacked = [_unpack(a, like, rows_s) for a in (g_s, d_s, nm_s, nv_s)]
    for i, n in enumerate(SMALL):
        outs[n] = tuple(u[i] for u in unpacked)

    return (loss, dx[None], *[outs[n][0] for n in ORDER], *[outs[n][1] for n in ORDER],
            *[outs[n][2] for n in ORDER], *[outs[n][3] for n in ORDER])
```

```python
import functools

import jax
import jax.numpy as jnp
from jax import lax
from jax.experimental import pallas as pl
from jax.experimental.pallas import tpu as pltpu

F32 = jnp.float32
BF16 = jnp.bfloat16
HIGHEST = lax.Precision.HIGHEST

D_MODEL = 1024
D_PLE = 256
GRID_W = 64
D_SSM = 512
SSM_GROUP = 16
N_GROUPS = 32
N_PAIRS = 16
SSM_STATE = 64
D_NA = 512
NA_HEADS = 8
NA_HEAD_DIM = 64
NA_ROWS = 8
NA_COLS = 16
D_IN_PROJ = 3072
EPS = 1e-6
N_DEV = 8
SHARD_IN = D_IN_PROJ // N_DEV
LANES = 128
SSM_CHUNK = 16
TOK_TILE = 256
NA_QROWS = 4
NA_KROWS = 12
NEG = -1e30
VMEM_LIMIT = 56 * 1024 * 1024

ADAM_LR = 0.001
ADAM_B1 = 0.9
ADAM_B2 = 0.999
ADAM_EPS = 1e-08
ADAM_WD = 0.01
ADAM_STEP = 10

MESH = pl.DeviceIdType.MESH


def _params(*sem):
    return pltpu.CompilerParams(dimension_semantics=sem or None, vmem_limit_bytes=VMEM_LIMIT)


def _whole_vmem():
    return pl.BlockSpec(memory_space=pltpu.VMEM)


def _nt(a, b):
    return lax.dot_general(a, b, (((1,), (1,)), ((), ())), preferred_element_type=F32)


def _tn(a, b):
    return lax.dot_general(a, b, (((0,), (0,)), ((), ())), preferred_element_type=F32)


def _mm(a, b):
    return jnp.dot(a, b, preferred_element_type=F32)


def _sigmoid(x):
    return 1.0 / (1.0 + jnp.exp(-x))


_GELU_C = 0.7978845608028654


def _gelu(x):
    return 0.5 * x * (1.0 + jnp.tanh(_GELU_C * (x + 0.044715 * x * x * x)))


def _gelu_grad(x):
    th = jnp.tanh(_GELU_C * (x + 0.044715 * x * x * x))
    return 0.5 * (1.0 + th) + 0.5 * x * (1.0 - th * th) * _GELU_C * (1.0 + 3.0 * 0.044715 * x * x)


def _ssm_tables(a_re, a_im, log_dt, b_re, b_im, c_re, c_im, d):
    T, P, H = SSM_CHUNK, SSM_STATE, SSM_GROUP
    dt = jnp.exp(log_dt)[..., None]
    xr = dt * a_re
    xi = dt * a_im
    mag = jnp.exp(xr)
    lr = mag * jnp.cos(xi)
    li = mag * jnp.sin(xi)
    den = a_re * a_re + a_im * a_im
    cr = ((lr - 1.0) * a_re + li * a_im) / den
    ci = (li * a_re - (lr - 1.0) * a_im) / den
    bbr = cr[..., None] * b_re - ci[..., None] * b_im
    bbi = cr[..., None] * b_im + ci[..., None] * b_re
    kk = jnp.arange(T + 1, dtype=F32)[:, None, None, None]
    pm = jnp.exp(kk * xr)
    pw = jnp.stack([pm * jnp.cos(kk * xi), pm * jnp.sin(kk * xi)], axis=2)
    pw = pw.reshape(T + 1, 2, 2, N_PAIRS, 2 * P).transpose(3, 0, 1, 2, 4).reshape(N_PAIRS, T + 1, 8 * P)
    pw = jnp.concatenate([pw, jnp.zeros((N_PAIRS, 24 - (T + 1), 8 * P), F32)], axis=1)
    eye2 = jnp.eye(2, dtype=F32)

    def expand(t):
        t = t.transpose(2, 0, 1, 3, 4, 5)
        t = t[:, :, :, :, :, None, :] * eye2[None, None, None, :, None, :, None]
        return t.reshape(N_PAIRS, 4, 2 * H, 2 * P)

    bb = expand(jnp.stack([bbr, bbi], axis=1).reshape(2, 2, N_PAIRS, 2, P, H).transpose(0, 1, 2, 3, 5, 4))
    cc = expand(jnp.stack([c_re, c_im], axis=1).reshape(2, 2, N_PAIRS, 2, H, P))
    dd = d.reshape(N_PAIRS, 2 * H)[:, :, None] * jnp.eye(2 * H, dtype=F32)[None]
    dd = jnp.concatenate([dd, jnp.zeros((N_PAIRS, 2 * H, LANES - 2 * H), F32)], axis=2)
    return pw, bb, cc, dd


def _na_bias(rpb, rows):
    ri = jnp.arange(NA_QROWS)[:, None, None]
    kr = jnp.arange(NA_KROWS)[None, :, None]
    dr = jnp.arange(2 * NA_ROWS - 1)[None, None, :]
    r0 = ((kr <= 7) & (kr - ri + 7 == dr))
    r1 = ((kr >= ri) & (kr <= ri + 7) & (kr - ri + 3 == dr))
    r2 = ((kr >= 4) & (kr - ri - 1 == dr))
    rsel = jnp.stack([r0, r1, r2], axis=0).astype(F32)
    qc = jnp.arange(GRID_W)[:, None, None]
    kc = jnp.arange(GRID_W)[None, :, None]
    dc = jnp.arange(2 * NA_COLS - 1)[None, None, :]
    cstart = jnp.clip(qc - NA_COLS // 2, 0, GRID_W - NA_COLS)
    csel = ((kc >= cstart) & (kc < cstart + NA_COLS) & (kc - qc + NA_COLS - 1 == dc)).astype(F32)
    part = jnp.einsum('hrc,qjc->hrqj', rpb, csel, precision=HIGHEST)
    core = jnp.einsum('hrqj,yikr->hyiqkj', part, rsel, precision=HIGHEST)
    valid = jnp.einsum('yikr,qjc->yiqkj', rsel, csel) > 0.5
    bias = jnp.where(valid[None], core, NEG)
    return bias.reshape(NA_HEADS, 3, NA_QROWS * GRID_W, NA_KROWS * GRID_W)


def _fwd_in(x2, g_pre, w_in_g):
    seq = x2.shape[0]

    def body(x_ref, g_ref, w_ref, u_ref, zs_ref, qkv_ref, zn_ref, hn_ref):
        x = x_ref[...]
        r = lax.rsqrt(jnp.mean(x * x, axis=-1, keepdims=True) + EPS)
        hn = (x * r * g_ref[...]).astype(BF16)
        hn_ref[...] = hn
        for j in range(N_DEV):
            pj = _mm(hn, w_ref[j])
            for i in range(SHARD_IN // LANES):
                blk = (SHARD_IN // LANES) * j + i
                piece = pj[:, i * LANES:(i + 1) * LANES]
                if blk < 4:
                    u_ref[:, blk * LANES:(blk + 1) * LANES] = piece
                elif blk < 8:
                    zs_ref[:, (blk - 4) * LANES:(blk - 3) * LANES] = piece
                elif blk < 20:
                    qkv_ref[:, (blk - 8) * LANES:(blk - 7) * LANES] = piece.astype(BF16)
                else:
                    zn_ref[:, (blk - 20) * LANES:(blk - 19) * LANES] = piece

    tok = lambda w: pl.BlockSpec((TOK_TILE, w), lambda i: (i, 0))
    return pl.pallas_call(
        body, name="fwd_in", grid=(seq // TOK_TILE,),
        in_specs=[tok(D_MODEL), pl.BlockSpec((1, D_MODEL), lambda i: (0, 0)), _whole_vmem()],
        out_specs=[tok(D_SSM), tok(D_SSM), tok(3 * D_NA), tok(D_NA), tok(D_MODEL)],
        out_shape=[jax.ShapeDtypeStruct((seq, D_SSM), F32), jax.ShapeDtypeStruct((seq, D_SSM), F32),
                   jax.ShapeDtypeStruct((seq, 3 * D_NA), BF16), jax.ShapeDtypeStruct((seq, D_NA), F32),
                   jax.ShapeDtypeStruct((seq, D_MODEL), BF16)],
        compiler_params=_params("arbitrary"),
    )(x2, g_pre, w_in_g)


def _mid_fwd(yssm, zs, o, zn, x2, p2, tgt, w_glu, b_glu, w_out, g_post, w_ple_g, g_ple, w_pg):
    seq = x2.shape[0]

    def body(yssm_ref, zs_ref, o_ref, zn_ref, x_ref, p_ref, tgt_ref, wglu_ref, bglu_ref, wout_ref, gpost_ref,
             wple_ref, gple_ref, wpg_ref,
             dcat_ref, dh1_ref, loss_ref, dwout_ref, dwple_ref, dwpg_ref, dgpost_ref, dgple_ref):
        @pl.when(pl.program_id(0) == 0)
        def _():
            loss_ref[...] = jnp.zeros_like(loss_ref)
            dwout_ref[...] = jnp.zeros_like(dwout_ref)
            dwple_ref[...] = jnp.zeros_like(dwple_ref)
            dwpg_ref[...] = jnp.zeros_like(dwpg_ref)
            dgpost_ref[...] = jnp.zeros_like(dgpost_ref)
            dgple_ref[...] = jnp.zeros_like(dgple_ref)

        g1 = _gelu(yssm_ref[...])
        t = _mm(g1.astype(BF16), wglu_ref[...]) + bglu_ref[...]
        zs_v = zs_ref[...]
        ys = g1 * _sigmoid(t) * (zs_v * _sigmoid(zs_v))
        zn_v = zn_ref[...]
        yn = o_ref[...] * (zn_v * _sigmoid(zn_v))
        cat = jnp.concatenate([ys, yn], axis=1).astype(BF16)
        mix = _mm(cat, wout_ref[...])
        r2 = lax.rsqrt(jnp.mean(mix * mix, axis=-1, keepdims=True) + EPS)
        n2 = mix * r2
        gpost = gpost_ref[...]
        h1 = x_ref[...] + n2 * gpost
        pb = p_ref[...].astype(BF16)
        epre = jnp.concatenate([_mm(pb, wple_ref[j]) for j in range(N_DEV)], axis=1)
        r3 = lax.rsqrt(jnp.mean(epre * epre, axis=-1, keepdims=True) + EPS)
        n3 = epre * r3
        gple = gple_ref[...]
        e = n3 * gple
        h1b = h1.astype(BF16)
        gate = _sigmoid(_mm(h1b, wpg_ref[...]))
        diff = h1 + gate * e - tgt_ref[...]
        loss_ref[...] += jnp.sum(diff * diff, axis=0, keepdims=True)

        dy = diff * (1.0 / D_MODEL)
        dgp = (dy * e * gate * (1.0 - gate)).astype(BF16)
        de = dy * gate
        dh1 = dy + _nt(dgp, wpg_ref[...])
        dh1_ref[...] = dh1
        dwpg_ref[...] += _tn(h1b, dgp)
        dgple_ref[...] += jnp.sum(de * n3, axis=0, keepdims=True)
        dn3 = de * gple
        depre = (r3 * (dn3 - n3 * jnp.mean(dn3 * n3, axis=-1, keepdims=True))).astype(BF16)
        for j in range(N_DEV):
            dwple_ref[j] += _tn(pb, depre[:, j * LANES:(j + 1) * LANES])
        dgpost_ref[...] += jnp.sum(dh1 * n2, axis=0, keepdims=True)
        dn2 = dh1 * gpost
        dmix = (r2 * (dn2 - n2 * jnp.mean(dn2 * n2, axis=-1, keepdims=True))).astype(BF16)
        dcat_ref[...] = _nt(dmix, wout_ref[...])
        dwout_ref[...] += _tn(cat, dmix)

    tok = lambda w: pl.BlockSpec((TOK_TILE, w), lambda i: (i, 0))
    row = lambda w: pl.BlockSpec((1, w), lambda i: (0, 0))
    vm = _whole_vmem()
    return pl.pallas_call(
        body, name="mid_fwd", grid=(seq // TOK_TILE,),
        in_specs=[tok(D_SSM), tok(D_SSM), tok(D_NA), tok(D_NA), tok(D_MODEL), tok(D_PLE), tok(D_MODEL),
                  vm, row(D_SSM), vm, row(D_MODEL), vm, row(D_MODEL), vm],
        out_specs=[tok(D_MODEL), tok(D_MODEL), vm, vm, vm, vm, vm, vm],
        out_shape=[jax.ShapeDtypeStruct((seq, D_MODEL), F32), jax.ShapeDtypeStruct((seq, D_MODEL), F32),
                   jax.ShapeDtypeStruct((1, D_MODEL), F32),
                   jax.ShapeDtypeStruct((D_MODEL, D_MODEL), F32),
                   jax.ShapeDtypeStruct((N_DEV, D_PLE, LANES), F32),
                   jax.ShapeDtypeStruct((D_MODEL, D_MODEL), F32),
                   jax.ShapeDtypeStruct((1, D_MODEL), F32), jax.ShapeDtypeStruct((1, D_MODEL), F32)],
        compiler_params=_params("arbitrary"),
    )(yssm, zs, o, zn, x2, p2, tgt, w_glu, b_glu, w_out, g_post, w_ple_g, g_ple, w_pg)


def _mid_bwd(yssm, zs, o, zn, dcat, w_glu, b_glu):
    seq = yssm.shape[0]

    def body(yssm_ref, zs_ref, o_ref, zn_ref, dcat_ref, wglu_ref, bglu_ref,
             dyssm_ref, dzs_ref, do_ref, dzn_ref, dwglu_ref, dbglu_ref):
        @pl.when(pl.program_id(0) == 0)
        def _():
            dwglu_ref[...] = jnp.zeros_like(dwglu_ref)
            dbglu_ref[...] = jnp.zeros_like(dbglu_ref)

        dys = dcat_ref[:, :D_SSM]
        dyn = dcat_ref[:, D_SSM:]
        yv = yssm_ref[...]
        g1 = _gelu(yv)
        g1b = g1.astype(BF16)
        sg = _sigmoid(_mm(g1b, wglu_ref[...]) + bglu_ref[...])
        zs_v = zs_ref[...]
        s_zs = _sigmoid(zs_v)
        dg2 = dys * (zs_v * s_zs)
        dzs_ref[...] = dys * (g1 * sg) * (s_zs * (1.0 + zs_v * (1.0 - s_zs)))
        dt = dg2 * g1 * sg * (1.0 - sg)
        dtb = dt.astype(BF16)
        dg1 = dg2 * sg + _nt(dtb, wglu_ref[...])
        dwglu_ref[...] += _tn(g1b, dtb)
        dbglu_ref[...] += jnp.sum(dt, axis=0, keepdims=True)
        dyssm_ref[...] = dg1 * _gelu_grad(yv)
        zn_v = zn_ref[...]
        s_zn = _sigmoid(zn_v)
        do_ref[...] = dyn * (zn_v * s_zn)
        dzn_ref[...] = dyn * o_ref[...] * (s_zn * (1.0 + zn_v * (1.0 - s_zn)))

    tok = lambda w: pl.BlockSpec((TOK_TILE, w), lambda i: (i, 0))
    vm = _whole_vmem()
    half = jax.ShapeDtypeStruct((seq, D_SSM), F32)
    return pl.pallas_call(
        body, name="mid_bwd", grid=(seq // TOK_TILE,),
        in_specs=[tok(D_SSM), tok(D_SSM), tok(D_NA), tok(D_NA), tok(D_MODEL), vm,
                  pl.BlockSpec((1, D_SSM), lambda i: (0, 0))],
        out_specs=[tok(D_SSM), tok(D_SSM), tok(D_NA), tok(D_NA), vm, vm],
        out_shape=[half, half, half, half, jax.ShapeDtypeStruct((D_SSM, D_SSM), F32),
                   jax.ShapeDtypeStruct((1, D_SSM), F32)],
        compiler_params=_params("arbitrary"),
    )(yssm, zs, o, zn, dcat, w_glu, b_glu)


def _bwd_in(du, dzs, dq, dk, dv, dzn, hn, x2, dh1, g_pre, w_in_g):
    seq = x2.shape[0]

    def body(du_ref, dzs_ref, dq_ref, dk_ref, dv_ref, dzn_ref, hn_ref, x_ref, dh1_ref, g_ref, w_ref,
             dx_ref, dw_ref, dg_ref, dproj_ref):
        @pl.when(pl.program_id(0) == 0)
        def _():
            dw_ref[...] = jnp.zeros_like(dw_ref)
            dg_ref[...] = jnp.zeros_like(dg_ref)

        for k, ref in enumerate((du_ref, dzs_ref, dq_ref, dk_ref, dv_ref, dzn_ref)):
            dproj_ref[:, k * D_SSM:(k + 1) * D_SSM] = ref[...].astype(BF16)
        hn = hn_ref[...]
        dhn = jnp.zeros((TOK_TILE, D_MODEL), F32)
        for j in range(N_DEV):
            dpj = dproj_ref[:, j * SHARD_IN:(j + 1) * SHARD_IN]
            dhn += _nt(dpj, w_ref[j])
            dw_ref[j] += _tn(hn, dpj)
        x = x_ref[...]
        r = lax.rsqrt(jnp.mean(x * x, axis=-1, keepdims=True) + EPS)
        n1 = x * r
        dg_ref[...] += jnp.sum(dhn * n1, axis=0, keepdims=True)
        dn1 = dhn * g_ref[...]
        dx_ref[...] = dh1_ref[...] + r * (dn1 - n1 * jnp.mean(dn1 * n1, axis=-1, keepdims=True))

    tok = lambda w: pl.BlockSpec((TOK_TILE, w), lambda i: (i, 0))
    vm = _whole_vmem()
    return pl.pallas_call(
        body, name="bwd_in", grid=(seq // TOK_TILE,),
        in_specs=[tok(D_SSM)] * 6 + [tok(D_MODEL), tok(D_MODEL), tok(D_MODEL),
                                      pl.BlockSpec((1, D_MODEL), lambda i: (0, 0)), vm],
        out_specs=[tok(D_MODEL), vm, vm],
        out_shape=[jax.ShapeDtypeStruct((seq, D_MODEL), F32),
                   jax.ShapeDtypeStruct((N_DEV, D_MODEL, SHARD_IN), F32),
                   jax.ShapeDtypeStruct((1, D_MODEL), F32)],
        scratch_shapes=[pltpu.VMEM((TOK_TILE, D_IN_PROJ), BF16)],
        compiler_params=_params("arbitrary"),
    )(du, dzs, dq, dk, dv, dzn, hn, x2, dh1, g_pre, w_in_g)


RELAYOUT_CHUNKS = 128
PAIR_W = 2 * SSM_GROUP
PAIRS_PER_BLOCK = LANES // PAIR_W
CHUNK_W = SSM_CHUNK * PAIR_W
PW_ROWS = 24


def _lane_window(lo, width):
    lane = lax.broadcasted_iota(jnp.int32, (1, LANES), 1)
    return (lane >= lo) & (lane < lo + width)


def _to_pairs(a, nc):
    ncb = min(RELAYOUT_CHUNKS, nc)

    def body(x_ref, out_ref):
        xs = [x_ref[pl.ds(s, ncb, stride=SSM_CHUNK), :] for s in range(SSM_CHUNK)]
        for a_ in range(PAIRS_PER_BLOCK):
            for v in range(SSM_CHUNK // PAIRS_PER_BLOCK):
                acc = None
                for i in range(PAIRS_PER_BLOCK):
                    shift = (PAIR_W * (i - a_)) % LANES
                    piece = xs[PAIRS_PER_BLOCK * v + i]
                    piece = pltpu.roll(piece, shift, axis=1) if shift else piece
                    acc = piece if acc is None else jnp.where(_lane_window(PAIR_W * i, PAIR_W), piece, acc)
                out_ref[a_, :, LANES * v:LANES * (v + 1)] = acc.astype(BF16)

    return pl.pallas_call(
        body, name="to_pairs", grid=(D_SSM // LANES, nc // ncb),
        in_specs=[pl.BlockSpec((ncb * SSM_CHUNK, LANES), lambda cl, cb: (cb, cl))],
        out_specs=pl.BlockSpec((PAIRS_PER_BLOCK, ncb, CHUNK_W), lambda cl, cb: (cl, cb, 0)),
        out_shape=jax.ShapeDtypeStruct((N_PAIRS, nc, CHUNK_W), BF16),
        compiler_params=_params("arbitrary", "arbitrary"),
    )(a)


def _from_pairs(a, nc):
    ncb = min(RELAYOUT_CHUNKS, nc)

    def body(y_ref, out_ref):
        for s in range(SSM_CHUNK):
            v, i = divmod(s, PAIRS_PER_BLOCK)
            acc = None
            for a_ in range(PAIRS_PER_BLOCK):
                shift = (PAIR_W * (a_ - i)) % LANES
                piece = y_ref[a_, :, LANES * v:LANES * (v + 1)]
                piece = pltpu.roll(piece, shift, axis=1) if shift else piece
                acc = piece if acc is None else jnp.where(_lane_window(PAIR_W * a_, PAIR_W), piece, acc)
            out_ref[pl.ds(s, ncb, stride=SSM_CHUNK), :] = acc

    return pl.pallas_call(
        body, name="from_pairs", grid=(D_SSM // LANES, nc // ncb),
        in_specs=[pl.BlockSpec((PAIRS_PER_BLOCK, ncb, CHUNK_W), lambda cl, cb: (cl, cb, 0))],
        out_specs=pl.BlockSpec((ncb * SSM_CHUNK, LANES), lambda cl, cb: (cb, cl)),
        out_shape=jax.ShapeDtypeStruct((nc * SSM_CHUNK, D_SSM), F32),
        compiler_params=_params("arbitrary", "arbitrary"),
    )(a)


def _boundary_scan(nc, pw_ref, src_ref, dst_ref, conj):
    nblk = nc // 8
    lr0, li0, lr1, li1 = (pw_ref[0, SSM_CHUNK:SSM_CHUNK + 1, LANES * q:LANES * (q + 1)] for q in range(4))
    if conj:
        li0, li1 = -li0, -li1

    def step(i, carry):
        hr0, hi0, hr1, hi1 = carry
        up = pl.multiple_of(i * 8, 8)
        dn = pl.multiple_of((nblk - 1 - i) * 8, 8)
        a_rows, b_rows = (dn, up) if conj else (up, dn)
        s_r0 = src_ref[pl.ds(a_rows, 8), 0:128]
        s_i0 = src_ref[pl.ds(a_rows, 8), 128:256]
        s_r1 = src_ref[pl.ds(b_rows, 8), 256:384]
        s_i1 = src_ref[pl.ds(b_rows, 8), 384:512]
        o_r0, o_i0, o_r1, o_i1 = [], [], [], []
        for k in range(8):
            ka = 7 - k if conj else k
            kb = k if conj else 7 - k
            o_r0.append((ka, hr0))
            o_i0.append((ka, hi0))
            o_r1.append((kb, hr1))
            o_i1.append((kb, hi1))
            hr0, hi0 = (lr0 * hr0 - li0 * hi0 + s_r0[ka:ka + 1], lr0 * hi0 + li0 * hr0 + s_i0[ka:ka + 1])
            hr1, hi1 = (lr1 * hr1 - li1 * hi1 + s_r1[kb:kb + 1], lr1 * hi1 + li1 * hr1 + s_i1[kb:kb + 1])

        def rows(items):
            return jnp.concatenate([v for _, v in sorted(items, key=lambda kv: kv[0])], axis=0)

        dst_ref[pl.ds(a_rows, 8), 0:128] = rows(o_r0)
        dst_ref[pl.ds(a_rows, 8), 128:256] = rows(o_i0)
        dst_ref[pl.ds(b_rows, 8), 256:384] = rows(o_r1)
        dst_ref[pl.ds(b_rows, 8), 384:512] = rows(o_i1)
        return hr0, hi0, hr1, hi1

    z = jnp.zeros((1, LANES), F32)
    lax.fori_loop(0, nblk, step, (z, z, z, z))


def _pw_row(pw_ref, k, q):
    return pw_ref[0, k:k + 1, LANES * q:LANES * (q + 1)]


def _mm_f32(a, b, dims):
    return lax.dot_general(a, b, (dims, ((), ())), precision=HIGHEST, preferred_element_type=F32)


_POW_M = (lambda s: SSM_CHUNK - 1 - s, lambda s: s)
_POW_C = (lambda s: s + 1, lambda s: SSM_CHUNK - s)
_POW_K = (lambda s: s, lambda s: SSM_CHUNK - 1 - s)


def _chunk_matrices(pw_ref, bb_ref, cc_ref, dd_ref, m_scr, ct_scr, toep_scr, g_scr, kt_scr):
    blk = lambda s: slice(PAIR_W * s, PAIR_W * (s + 1))
    col = lambda q: slice(LANES * q, LANES * (q + 1))
    for d in range(2):
        bbr, bbi = bb_ref[0, 2 * d], bb_ref[0, 2 * d + 1]
        ccr, cci = cc_ref[0, 2 * d], cc_ref[0, 2 * d + 1]
        for s in range(SSM_CHUNK):
            pr, pi = _pw_row(pw_ref, _POW_M[d](s), 2 * d), _pw_row(pw_ref, _POW_M[d](s), 2 * d + 1)
            m_scr[blk(s), col(2 * d)] = (pr * bbr - pi * bbi).astype(m_scr.dtype)
            m_scr[blk(s), col(2 * d + 1)] = (pr * bbi + pi * bbr).astype(m_scr.dtype)
            pr, pi = _pw_row(pw_ref, _POW_C[d](s), 2 * d), _pw_row(pw_ref, _POW_C[d](s), 2 * d + 1)
            ct_scr[blk(s), col(2 * d)] = (ccr * pr - cci * pi).astype(ct_scr.dtype)
            ct_scr[blk(s), col(2 * d + 1)] = (-(ccr * pi + cci * pr)).astype(ct_scr.dtype)
            pr, pi = _pw_row(pw_ref, _POW_K[d](s), 2 * d), _pw_row(pw_ref, _POW_K[d](s), 2 * d + 1)
            g_scr[d, blk(s), 0:LANES] = ccr * pr - cci * pi
            g_scr[d, blk(s), LANES:2 * LANES] = -(ccr * pi + cci * pr)
        kt = _mm_f32(jnp.concatenate([bbr, bbi], axis=1), g_scr[d], ((1,), (1,)))
        if d == 0:
            kt = jnp.concatenate([kt[:, 0:LANES] + dd_ref[0], kt[:, LANES:]], axis=1)
        kt_scr[d] = kt
    lane = lax.broadcasted_iota(jnp.int32, (1, CHUNK_W), 1)
    for s in range(SSM_CHUNK):
        lo = PAIR_W * s
        hi = PAIR_W * (s + 1)
        fwd = kt_scr[0] if s == 0 else pltpu.roll(kt_scr[0], lo, axis=1)
        bwd = kt_scr[1] if hi == CHUNK_W else pltpu.roll(kt_scr[1], hi, axis=1)
        row = jnp.where(lane >= lo, fwd, 0.0) + jnp.where(lane < hi, bwd, 0.0)
        toep_scr[blk(s), :] = row.astype(toep_scr.dtype)


def _ssm_scratch(nc, mat_dtype):
    return [pltpu.VMEM((CHUNK_W, CHUNK_W), mat_dtype), pltpu.VMEM((CHUNK_W, CHUNK_W), mat_dtype),
            pltpu.VMEM((CHUNK_W, CHUNK_W), mat_dtype), pltpu.VMEM((2, CHUNK_W, 2 * LANES), F32),
            pltpu.VMEM((2, PAIR_W, CHUNK_W), F32), pltpu.VMEM((nc, CHUNK_W), F32), pltpu.VMEM((nc, CHUNK_W), F32)]


def _per_pair(*shape):
    return pl.BlockSpec((1,) + shape, lambda g: (g,) + (0,) * len(shape))


_TABLE_SPECS = lambda: [_per_pair(PW_ROWS, CHUNK_W), _per_pair(4, PAIR_W, LANES), _per_pair(4, PAIR_W, LANES),
                        _per_pair(PAIR_W, LANES)]


def _ssm_fwd(u, pw, bb, cc, dd):
    npair, nc, width = u.shape

    def body(u_ref, pw_ref, bb_ref, cc_ref, dd_ref, y_ref, hin_ref, m_scr, ct_scr, toep_scr, g_scr, kt_scr, s_scr, h_scr):
        _chunk_matrices(pw_ref, bb_ref, cc_ref, dd_ref, m_scr, ct_scr, toep_scr, g_scr, kt_scr)
        uv = u_ref[0]
        s_scr[...] = _mm(uv, m_scr[...])
        _boundary_scan(nc, pw_ref, s_scr, h_scr, conj=False)
        hin = h_scr[...]
        hin_ref[0] = hin
        y_ref[0] = _mm(uv, toep_scr[...]) + _nt(hin.astype(u.dtype), ct_scr[...])

    return pl.pallas_call(
        body, name="ssm_fwd", grid=(npair,),
        in_specs=[_per_pair(nc, width)] + _TABLE_SPECS(),
        out_specs=[_per_pair(nc, width), _per_pair(nc, width)],
        out_shape=[jax.ShapeDtypeStruct((npair, nc, width), F32), jax.ShapeDtypeStruct((npair, nc, width), F32)],
        scratch_shapes=_ssm_scratch(nc, u.dtype),
        compiler_params=_params("arbitrary"),
    )(u, pw, bb, cc, dd)


def _ssm_bwd(u, dy, hin, pw, bb, cc, dd):
    npair, nc, width = u.shape

    def body(u_ref, dy_ref, hin_ref, pw_ref, bb_ref, cc_ref, dd_ref, du_ref, dpw_ref, dbb_ref, dcc_ref, ddd_ref,
             m_scr, ct_scr, toep_scr, g_scr, kt_scr, dh_scr, ds_scr):
        _chunk_matrices(pw_ref, bb_ref, cc_ref, dd_ref, m_scr, ct_scr, toep_scr, g_scr, kt_scr)
        uv = u_ref[0]
        dyb = dy_ref[0]
        hin = hin_ref[0]
        dh_scr[...] = _mm(dyb, ct_scr[...])
        _boundary_scan(nc, pw_ref, dh_scr, ds_scr, conj=True)
        ds = ds_scr[...]
        dsb = ds.astype(u.dtype)
        du_ref[0] = _nt(dsb, m_scr[...]) + _nt(dyb, toep_scr[...])
        dm = _tn(uv, dsb)
        dct = _tn(dyb, hin.astype(u.dtype))
        dtoep = _tn(uv, dyb)

        dpw_ref[...] = jnp.zeros_like(dpw_ref)
        blk = lambda s: slice(PAIR_W * s, PAIR_W * (s + 1))
        col = lambda q: slice(LANES * q, LANES * (q + 1))

        def add_pw(k, q, val):
            dpw_ref[0, k:k + 1, col(q)] += jnp.sum(val, axis=0, keepdims=True)

        for d in range(2):
            g_r, g_i = ds[:, col(2 * d)], ds[:, col(2 * d + 1)]
            h_r, h_i = hin[:, col(2 * d)], hin[:, col(2 * d + 1)]
            add_pw(SSM_CHUNK, 2 * d, g_r * h_r + g_i * h_i)
            add_pw(SSM_CHUNK, 2 * d + 1, g_i * h_r - g_r * h_i)

        lane = lax.broadcasted_iota(jnp.int32, (1, CHUNK_W), 1)
        dkt0 = jnp.zeros((PAIR_W, CHUNK_W), F32)
        dkt1 = jnp.zeros((PAIR_W, CHUNK_W), F32)
        for s in range(SSM_CHUNK):
            lo = PAIR_W * s
            hi = PAIR_W * (s + 1)
            row = dtoep[blk(s), :]
            fwd = jnp.where(lane >= lo, row, 0.0)
            bwd = jnp.where(lane < hi, row, 0.0)
            dkt0 += fwd if s == 0 else pltpu.roll(fwd, CHUNK_W - lo, axis=1)
            dkt1 += bwd if hi == CHUNK_W else pltpu.roll(bwd, CHUNK_W - hi, axis=1)
        ddd_ref[0] = dkt0[:, 0:LANES]

        for d, dkt in enumerate((dkt0, dkt1)):
            bbr, bbi = bb_ref[0, 2 * d], bb_ref[0, 2 * d + 1]
            ccr, cci = cc_ref[0, 2 * d], cc_ref[0, 2 * d + 1]
            dbbcat = _mm_f32(dkt, g_scr[d], ((1,), (0,)))
            dg = _mm_f32(dkt, jnp.concatenate([bbr, bbi], axis=1), ((0,), (0,)))
            dbbr, dbbi = dbbcat[:, 0:LANES], dbbcat[:, LANES:]
            dccr = jnp.zeros((PAIR_W, LANES), F32)
            dcci = jnp.zeros((PAIR_W, LANES), F32)
            for s in range(SSM_CHUNK):
                k = _POW_M[d](s)
                pr, pi = _pw_row(pw_ref, k, 2 * d), _pw_row(pw_ref, k, 2 * d + 1)
                gr, gi = dm[blk(s), col(2 * d)], dm[blk(s), col(2 * d + 1)]
                dbbr += gr * pr + gi * pi
                dbbi += gi * pr - gr * pi
                add_pw(k, 2 * d, gr * bbr + gi * bbi)
                add_pw(k, 2 * d + 1, gi * bbr - gr * bbi)
                for k, gr, gi in ((_POW_C[d](s), dct[blk(s), col(2 * d)], dct[blk(s), col(2 * d + 1)]),
                                  (_POW_K[d](s), dg[blk(s), 0:LANES], dg[blk(s), LANES:])):
                    pr, pi = _pw_row(pw_ref, k, 2 * d), _pw_row(pw_ref, k, 2 * d + 1)
                    dccr += gr * pr - gi * pi
                    dcci += -(gr * pi + gi * pr)
                    add_pw(k, 2 * d, gr * ccr - gi * cci)
                    add_pw(k, 2 * d + 1, -(gr * cci + gi * ccr))
            dbb_ref[0, 2 * d] = dbbr
            dbb_ref[0, 2 * d + 1] = dbbi
            dcc_ref[0, 2 * d] = dccr
            dcc_ref[0, 2 * d + 1] = dcci

    return pl.pallas_call(
        body, name="ssm_bwd", grid=(npair,),
        in_specs=[_per_pair(nc, width), _per_pair(nc, width), _per_pair(nc, width)] + _TABLE_SPECS(),
        out_specs=[_per_pair(nc, width)] + _TABLE_SPECS(),
        out_shape=[jax.ShapeDtypeStruct((npair, nc, width), F32), jax.ShapeDtypeStruct(pw.shape, F32),
                   jax.ShapeDtypeStruct(bb.shape, F32), jax.ShapeDtypeStruct(cc.shape, F32),
                   jax.ShapeDtypeStruct(dd.shape, F32)],
        scratch_shapes=_ssm_scratch(nc, u.dtype),
        compiler_params=_params("arbitrary"),
    )(u, dy, hin, pw, bb, cc, dd)


NA_Q = NA_QROWS * GRID_W
NA_K = NA_KROWS * GRID_W
NA_SCALE = NA_HEAD_DIM ** -0.5


def _na_block(b, nb, rows):
    start = jnp.clip(NA_QROWS * b - NA_ROWS // 2, 0, rows - NA_KROWS) * GRID_W
    kind = jnp.where(b == 0, 0, jnp.where(b == nb - 1, 2, 1))
    return pl.multiple_of(start, GRID_W), kind


def _na_probs(q2, kw, bias, in_head):
    qh = jnp.where(in_head, q2, jnp.zeros_like(q2))
    s = _nt(qh, kw) * NA_SCALE + bias
    p = jnp.exp(s - jnp.max(s, axis=-1, keepdims=True))
    return p * (1.0 / jnp.sum(p, axis=-1, keepdims=True))


def _na_fwd(qkv, bias):
    seq = qkv.shape[0]
    rows = seq // GRID_W
    nb = rows // NA_QROWS

    def body(q_ref, k_ref, v_ref, bias_ref, o_ref):
        start, kind = _na_block(pl.program_id(1), nb, rows)
        q2 = q_ref[...]
        kw = k_ref[pl.ds(start, NA_K), :]
        vw = v_ref[pl.ds(start, NA_K), :]
        lane = lax.broadcasted_iota(jnp.int32, (1, LANES), 1)
        out = None
        for hh in range(2):
            in_head = (lane < NA_HEAD_DIM) if hh == 0 else (lane >= NA_HEAD_DIM)
            p = _na_probs(q2, kw, bias_ref[hh, kind], in_head)
            oh = _mm(p.astype(BF16), vw)
            out = oh if out is None else jnp.where(in_head, oh, out)
        o_ref[...] = out

    return pl.pallas_call(
        body, name="na_fwd", grid=(NA_HEADS // 2, nb),
        in_specs=[pl.BlockSpec((NA_Q, LANES), lambda hp, b: (b, hp)),
                  pl.BlockSpec((seq, LANES), lambda hp, b: (0, 4 + hp)),
                  pl.BlockSpec((seq, LANES), lambda hp, b: (0, 8 + hp)),
                  pl.BlockSpec((2, 3, NA_Q, NA_K), lambda hp, b: (hp, 0, 0, 0))],
        out_specs=pl.BlockSpec((NA_Q, LANES), lambda hp, b: (b, hp)),
        out_shape=jax.ShapeDtypeStruct((seq, D_NA), F32),
        compiler_params=_params("arbitrary", "arbitrary"),
    )(qkv, qkv, qkv, bias)


def _na_bwd(qkv, do, bias):
    seq = qkv.shape[0]
    rows = seq // GRID_W
    nb = rows // NA_QROWS

    def body(q_ref, k_ref, v_ref, do_ref, bias_ref, dq_ref, dk_ref, dv_ref, dbias_ref):
        b = pl.program_id(1)
        start, kind = _na_block(b, nb, rows)

        @pl.when(b == 0)
        def _():
            dk_ref[...] = jnp.zeros_like(dk_ref)
            dv_ref[...] = jnp.zeros_like(dv_ref)
            dbias_ref[...] = jnp.zeros_like(dbias_ref)

        q2 = q_ref[...]
        kw = k_ref[pl.ds(start, NA_K), :]
        vw = v_ref[pl.ds(start, NA_K), :]
        do2 = do_ref[...].astype(BF16)
        lane = lax.broadcasted_iota(jnp.int32, (1, LANES), 1)
        dq = None
        dkw = None
        dvw = None
        for hh in range(2):
            in_head = (lane < NA_HEAD_DIM) if hh == 0 else (lane >= NA_HEAD_DIM)
            p = _na_probs(q2, kw, bias_ref[hh, kind], in_head)
            doh = jnp.where(in_head, do2, jnp.zeros_like(do2))
            dp = _nt(doh, vw)
            ds = p * (dp - jnp.sum(p * dp, axis=-1, keepdims=True))
            dbias_ref[hh, kind] += ds
            dsb = (ds * NA_SCALE).astype(BF16)
            dq_h = _mm(dsb, kw)
            dk_h = _tn(dsb, q2)
            dv_h = _tn(p.astype(BF16), do2)
            dq = dq_h if dq is None else jnp.where(in_head, dq_h, dq)
            dkw = dk_h if dkw is None else jnp.where(in_head, dk_h, dkw)
            dvw = dv_h if dvw is None else jnp.where(in_head, dv_h, dvw)
        dq_ref[...] = dq
        dk_ref[pl.ds(start, NA_K), :] += dkw
        dv_ref[pl.ds(start, NA_K), :] += dvw

    return pl.pallas_call(
        body, name="na_bwd", grid=(NA_HEADS // 2, nb),
        in_specs=[pl.BlockSpec((NA_Q, LANES), lambda hp, b: (b, hp)),
                  pl.BlockSpec((seq, LANES), lambda hp, b: (0, 4 + hp)),
                  pl.BlockSpec((seq, LANES), lambda hp, b: (0, 8 + hp)),
                  pl.BlockSpec((NA_Q, LANES), lambda hp, b: (b, hp)),
                  pl.BlockSpec((2, 3, NA_Q, NA_K), lambda hp, b: (hp, 0, 0, 0))],
        out_specs=[pl.BlockSpec((NA_Q, LANES), lambda hp, b: (b, hp)),
                   pl.BlockSpec((seq, LANES), lambda hp, b: (0, hp)),
                   pl.BlockSpec((seq, LANES), lambda hp, b: (0, hp)),
                   pl.BlockSpec((2, 3, NA_Q, NA_K), lambda hp, b: (hp, 0, 0, 0))],
        out_shape=[jax.ShapeDtypeStruct((seq, D_NA), F32), jax.ShapeDtypeStruct((seq, D_NA), F32),
                   jax.ShapeDtypeStruct((seq, D_NA), F32),
                   jax.ShapeDtypeStruct((NA_HEADS, 3, NA_Q, NA_K), F32)],
        compiler_params=_params("arbitrary", "arbitrary"),
    )(qkv, qkv, qkv, do, bias)


def _local_step(x2, p2, tgt, g_pre, g_post, w_in_g, ssm, w_glu, b_glu, rpb, w_out, w_ple_g, g_ple, w_pg):
    seq = x2.shape[0]
    nc = seq // SSM_CHUNK
    rows = seq // GRID_W

    (pw, bb, cc, dd), ssm_vjp = jax.vjp(_ssm_tables, *ssm)
    bias, bias_vjp = jax.vjp(functools.partial(_na_bias, rows=rows), rpb)

    u, zs, qkv, zn, hn = _fwd_in(x2, g_pre, w_in_g)
    u_p = _to_pairs(u, nc)
    y_p, hin = _ssm_fwd(u_p, pw, bb, cc, dd)
    yssm = _from_pairs(y_p, nc)
    o = _na_fwd(qkv, bias)
    dcat, dh1, sq, d_wout, d_wple, d_wpg, d_gpost, d_gple = _mid_fwd(
        yssm, zs, o, zn, x2, p2, tgt, w_glu, b_glu, w_out, g_post, w_ple_g, g_ple, w_pg)
    dyssm, dzs, do, dzn, d_wglu, d_bglu = _mid_bwd(yssm, zs, o, zn, dcat, w_glu, b_glu)
    dq, dk, dv, dbias = _na_bwd(qkv, do, bias)
    (d_rpb,) = bias_vjp(dbias)
    du_p, dpw, dbb, dcc, ddd = _ssm_bwd(u_p, _to_pairs(dyssm, nc), hin, pw, bb, cc, dd)
    d_ssm = ssm_vjp((dpw, dbb, dcc, ddd))
    du = _from_pairs(du_p, nc)
    dx, d_win, d_gpre = _bwd_in(du, dzs, dq, dk, dv, dzn, hn, x2, dh1, g_pre, w_in_g)
    return sq, dx, dict(norm_pre=d_gpre, norm_post=d_gpost, w_in=d_win, ssm=d_ssm, w_glu=d_wglu, b_glu=d_bglu,
                        na_rpb=d_rpb, w_out=d_wout, w_ple=d_wple, ple_norm=d_gple, w_ple_gate=d_wpg)


def _place():
    return lax.axis_index("x"), lax.axis_index("y"), lax.axis_index("c")


def _all_gather(shard):
    m_per, n = shard.shape

    def body(x_ref, out_ref, send_sems, recv_sems, local_sem):
        x, y, c = _place()
        me, sibling = (x, y, c), (x, y, 1 - c)
        chips = [(1 - x, y), (x, 1 - y), (1 - x, 1 - y)]

        def rows(px, py, pc):
            return out_ref.at[pl.ds((4 * px + 2 * py + pc) * m_per, m_per), :]

        def copy(k, block, to, src=None):
            return pltpu.make_async_remote_copy(
                src_ref=rows(*block) if src is None else src, dst_ref=rows(*block),
                send_sem=send_sems.at[k], recv_sem=recv_sems.at[k], device_id=to, device_id_type=MESH)

        mine = pltpu.make_async_copy(x_ref, rows(*me), local_sem)
        mine.start()
        first = [copy(0, me, sibling, src=x_ref)]
        first += [copy(1 + j, me, (*chip, c), src=x_ref) for j, chip in enumerate(chips)]
        for cp in first:
            cp.start()
        passed = [copy(4 + j, (*chip, c), sibling) for j, chip in enumerate(chips)]
        for j, chip in enumerate(chips):
            copy(1 + j, (*chip, c), me).wait_recv()
            passed[j].start()
        copy(0, sibling, me).wait_recv()
        for j, chip in enumerate(chips):
            copy(4 + j, (*chip, 1 - c), me).wait_recv()
        for cp in first + passed:
            cp.wait_send()
        mine.wait()

    return pl.pallas_call(
        body, name="all_gather",
        out_shape=jax.ShapeDtypeStruct((N_DEV * m_per, n), shard.dtype),
        in_specs=[_whole_vmem()], out_specs=_whole_vmem(),
        scratch_shapes=[pltpu.SemaphoreType.DMA((7,)), pltpu.SemaphoreType.DMA((7,)), pltpu.SemaphoreType.DMA],
        compiler_params=pltpu.CompilerParams(vmem_limit_bytes=VMEM_LIMIT),
    )(shard)


def _exchange_partials(part):
    _, r, n = part.shape

    def body(part_ref, land_ref, send_sems, recv_sems, local_sem):
        x, y, c = _place()
        me = 4 * x + 2 * y + c
        local = pltpu.make_async_copy(part_ref.at[me], land_ref.at[me], local_sem)
        local.start()
        copies = []
        for k in range(1, N_DEV):
            px, py, pc = x ^ ((k >> 2) & 1), y ^ ((k >> 1) & 1), c ^ (k & 1)
            peer = 4 * px + 2 * py + pc
            cp = pltpu.make_async_remote_copy(
                src_ref=part_ref.at[peer], dst_ref=land_ref.at[me],
                send_sem=send_sems.at[k - 1], recv_sem=recv_sems.at[k - 1],
                device_id=(px, py, pc), device_id_type=MESH)
            cp.start()
            copies.append(cp)
        for cp in copies:
            cp.wait_recv()
        for cp in copies:
            cp.wait_send()
        local.wait()

    anyspace = pl.BlockSpec(memory_space=pl.ANY)
    return pl.pallas_call(
        body, name="exchange_partials",
        out_shape=jax.ShapeDtypeStruct(part.shape, part.dtype),
        in_specs=[anyspace], out_specs=anyspace,
        scratch_shapes=[pltpu.SemaphoreType.DMA((N_DEV - 1,)), pltpu.SemaphoreType.DMA((N_DEV - 1,)),
                        pltpu.SemaphoreType.DMA],
    )(part)


def _sum_slots(land):
    _, r, n = land.shape
    tile = 128
    assert r % tile == 0

    def body(land_ref, out_ref):
        acc = land_ref[0]
        for j in range(1, N_DEV):
            acc = acc + land_ref[j]
        out_ref[...] = acc

    return pl.pallas_call(
        body, name="sum_slots", grid=(r // tile,),
        in_specs=[pl.BlockSpec((N_DEV, tile, n), lambda i: (0, i, 0))],
        out_specs=pl.BlockSpec((tile, n), lambda i: (i, 0)),
        out_shape=jax.ShapeDtypeStruct((r, n), land.dtype),
        compiler_params=_params("arbitrary"),
    )(land)


def _adamw(w, g, m, v):
    r, c = w.shape
    tile = 256 if r % 256 == 0 else r

    def body(w_ref, g_ref, m_ref, v_ref, d_ref, nm_ref, nv_ref):
        g = g_ref[...]
        nm = ADAM_B1 * m_ref[...] + (1.0 - ADAM_B1) * g
        nv = ADAM_B2 * v_ref[...] + (1.0 - ADAM_B2) * (g * g)
        m_hat = nm / (1.0 - ADAM_B1 ** ADAM_STEP)
        v_hat = nv / (1.0 - ADAM_B2 ** ADAM_STEP)
        d_ref[...] = -ADAM_LR * (m_hat / (jnp.sqrt(v_hat) + ADAM_EPS) + ADAM_WD * w_ref[...])
        nm_ref[...] = nm
        nv_ref[...] = nv

    spec = pl.BlockSpec((tile, c), lambda i: (i, 0))
    out = jax.ShapeDtypeStruct((r, c), F32)
    return pl.pallas_call(
        body, name="adamw", grid=(r // tile,), in_specs=[spec] * 4, out_specs=[spec] * 3,
        out_shape=[out, out, out], compiler_params=_params("arbitrary"),
    )(w, g, m, v)


_SLAB = 8 * LANES


def _flat_rows(a):
    flat = a.reshape(-1)
    pad = (-flat.shape[0]) % _SLAB
    if pad:
        flat = jnp.concatenate([flat, jnp.zeros((pad,), flat.dtype)])
    return flat.reshape(-1, LANES)


def _pack(arrays):
    slabs = [_flat_rows(a) for a in arrays]
    return jnp.concatenate(slabs, axis=0), [s.shape[0] for s in slabs]


def _unpack(packed, like, rows):
    out, at = [], 0
    for a, r in zip(like, rows):
        out.append(packed[at:at + r].reshape(-1)[:a.size].reshape(a.shape))
        at += r
    return out


SMALL = ('norm_pre', 'norm_post', 'ssm_a_re', 'ssm_a_im', 'ssm_log_dt', 'ssm_b_re', 'ssm_b_im', 'ssm_c_re',
         'ssm_c_im', 'ssm_d', 'b_glu', 'na_rpb', 'ple_norm')
BIG = ('w_in', 'w_glu', 'w_out', 'w_ple', 'w_ple_gate')
ORDER = ('norm_pre', 'norm_post', 'w_in', 'ssm_a_re', 'ssm_a_im', 'ssm_log_dt', 'ssm_b_re', 'ssm_b_im', 'ssm_c_re',
         'ssm_c_im', 'ssm_d', 'w_glu', 'b_glu', 'na_rpb', 'w_out', 'w_ple', 'ple_norm', 'w_ple_gate')


def kernel(x, p, norm_pre, norm_post, w_in, ssm_a_re, ssm_a_im, ssm_log_dt, ssm_b_re, ssm_b_im, ssm_c_re, ssm_c_im, ssm_d, w_glu, b_glu, na_rpb, w_out, w_ple, ple_norm, w_ple_gate, loss_target, m_norm_pre, m_norm_post, m_w_in, m_ssm_a_re, m_ssm_a_im, m_ssm_log_dt, m_ssm_b_re, m_ssm_b_im, m_ssm_c_re, m_ssm_c_im, m_ssm_d, m_w_glu, m_b_glu, m_na_rpb, m_w_out, m_w_ple, m_ple_norm, m_w_ple_gate, v_norm_pre, v_norm_post, v_w_in, v_ssm_a_re, v_ssm_a_im, v_ssm_log_dt, v_ssm_b_re, v_ssm_b_im, v_ssm_c_re, v_ssm_c_im, v_ssm_d, v_w_glu, v_b_glu, v_na_rpb, v_w_out, v_w_ple, v_ple_norm, v_w_ple_gate):
    args = dict(locals())
    weights = {n: args[n] for n in ORDER}
    mom_m = {n: args["m_" + n] for n in ORDER}
    mom_v = {n: args["v_" + n] for n in ORDER}

    shards = [weights[n][0].astype(BF16) for n in BIG]
    packed, shard_rows = _pack(shards)
    gathered = _all_gather(packed).reshape(N_DEV, -1, LANES)
    full = []
    at = 0
    for s, r in zip(shards, shard_rows):
        full.append(gathered[:, at:at + r].reshape((N_DEV,) + s.shape))
        at += r
    w_in_g, w_glu_g, w_out_g, w_ple_g, w_pg_g = full
    w_glu_f = w_glu_g.reshape(D_SSM, D_SSM)
    w_out_f = w_out_g.reshape(D_MODEL, D_MODEL)
    w_pg_f = w_pg_g.reshape(D_MODEL, D_MODEL)

    ssm = tuple(weights[n][0] for n in ('ssm_a_re', 'ssm_a_im', 'ssm_log_dt', 'ssm_b_re', 'ssm_b_im',
                                        'ssm_c_re', 'ssm_c_im', 'ssm_d'))
    sq, dx, grads = _local_step(x[0], p[0, 0], loss_target[0], norm_pre, norm_post, w_in_g, ssm, w_glu_f, b_glu,
                                na_rpb[0], w_out_f, w_ple_g, ple_norm, w_pg_f)
    loss = lax.psum(0.5 / D_MODEL * jnp.sum(sq), ("x", "y", "c"))

    local = dict(norm_pre=grads['norm_pre'], norm_post=grads['norm_post'], b_glu=grads['b_glu'],
                 na_rpb=grads['na_rpb'][None], ple_norm=grads['ple_norm'])
    for n, g in zip(('ssm_a_re', 'ssm_a_im', 'ssm_log_dt', 'ssm_b_re', 'ssm_b_im', 'ssm_c_re', 'ssm_c_im', 'ssm_d'),
                    grads['ssm']):
        local[n] = g[None]

    big_parts = [grads['w_in'],
                 grads['w_glu'].reshape(N_DEV, D_SSM // N_DEV, D_SSM),
                 grads['w_out'].reshape(N_DEV, D_MODEL // N_DEV, D_MODEL),
                 grads['w_ple'],
                 grads['w_ple_gate'].reshape(N_DEV, D_MODEL // N_DEV, D_MODEL)]
    small_flat, small_rows = _pack([local[n] for n in SMALL])
    pad = (-small_flat.shape[0]) % (128 * N_DEV)
    if pad:
        small_flat = jnp.concatenate([small_flat, jnp.zeros((pad, LANES), F32)], axis=0)
    small_per = small_flat.shape[0] // N_DEV
    slabs = [bp.reshape(N_DEV, -1, LANES) for bp in big_parts] + [small_flat.reshape(N_DEV, small_per, LANES)]
    part = jnp.concatenate(slabs, axis=1)
    reduced = _sum_slots(_exchange_partials(part))
    big_grads = {}
    at = 0
    for n, bp in zip(BIG, big_parts):
        r = bp[0].size // LANES
        big_grads[n] = reduced[at:at + r].reshape(bp.shape[1:])
        at += r
    small_all = _all_gather(reduced[at:at + small_per])

    outs = {}
    for n in BIG:
        w2 = weights[n][0]
        d, nm, nv = _adamw(w2, big_grads[n], mom_m[n][0], mom_v[n][0])
        outs[n] = (big_grads[n][None], d[None], nm[None], nv[None])
    like = [weights[n] for n in SMALL]
    w_s, rows_s = _pack(like)
    m_s, _ = _pack([mom_m[n] for n in SMALL])
    v_s, _ = _pack([mom_v[n] for n in SMALL])
    g_s = small_all[:w_s.shape[0]]
    d_s, nm_s, nv_s = _adamw(w_s, g_s, m_s, v_s)
    unpacked = [_unpack(a, like, rows_s) for a in (g_s, d_s, nm_s, nv_s)]
    for i, n in enumerate(SMALL):
        outs[n] = tuple(u[i] for u in unpacked)

    return (loss, dx[None], *[outs[n][0] for n in ORDER], *[outs[n][1] for n in ORDER],
            *[outs[n][2] for n in ORDER], *[outs[n][3] for n in ORDER])
```

```python
import functools

import jax
import jax.numpy as jnp
from jax import lax
from jax.experimental import pallas as pl
from jax.experimental.pallas import tpu as pltpu

F32 = jnp.float32
BF16 = jnp.bfloat16
HIGHEST = lax.Precision.HIGHEST

D_MODEL = 1024
D_PLE = 256
GRID_W = 64
D_SSM = 512
SSM_GROUP = 16
N_GROUPS = 32
N_PAIRS = 16
SSM_STATE = 64
D_NA = 512
NA_HEADS = 8
NA_HEAD_DIM = 64
NA_ROWS = 8
NA_COLS = 16
D_IN_PROJ = 3072
EPS = 1e-6
N_DEV = 8
SHARD_IN = D_IN_PROJ // N_DEV
LANES = 128
SSM_CHUNK = 16
TOK_TILE = 256
NA_QROWS = 4
NA_KROWS = 12
NEG = -1e30
VMEM_LIMIT = 56 * 1024 * 1024

ADAM_LR = 0.001
ADAM_B1 = 0.9
ADAM_B2 = 0.999
ADAM_EPS = 1e-08
ADAM_WD = 0.01
ADAM_STEP = 10

MESH = pl.DeviceIdType.MESH


def _params(*sem):
    return pltpu.CompilerParams(dimension_semantics=sem or None, vmem_limit_bytes=VMEM_LIMIT)


def _whole_vmem():
    return pl.BlockSpec(memory_space=pltpu.VMEM)


def _nt(a, b):
    return lax.dot_general(a, b, (((1,), (1,)), ((), ())), preferred_element_type=F32)


def _tn(a, b):
    return lax.dot_general(a, b, (((0,), (0,)), ((), ())), preferred_element_type=F32)


def _mm(a, b):
    return jnp.dot(a, b, preferred_element_type=F32)


def _sigmoid(x):
    return 1.0 / (1.0 + jnp.exp(-x))


_GELU_C = 0.7978845608028654


def _gelu(x):
    return 0.5 * x * (1.0 + jnp.tanh(_GELU_C * (x + 0.044715 * x * x * x)))


def _gelu_grad(x):
    th = jnp.tanh(_GELU_C * (x + 0.044715 * x * x * x))
    return 0.5 * (1.0 + th) + 0.5 * x * (1.0 - th * th) * _GELU_C * (1.0 + 3.0 * 0.044715 * x * x)


def _ssm_tables(a_re, a_im, log_dt, b_re, b_im, c_re, c_im, d):
    T, P, H = SSM_CHUNK, SSM_STATE, SSM_GROUP
    dt = jnp.exp(log_dt)[..., None]
    xr = dt * a_re
    xi = dt * a_im
    mag = jnp.exp(xr)
    lr = mag * jnp.cos(xi)
    li = mag * jnp.sin(xi)
    den = a_re * a_re + a_im * a_im
    cr = ((lr - 1.0) * a_re + li * a_im) / den
    ci = (li * a_re - (lr - 1.0) * a_im) / den
    bbr = cr[..., None] * b_re - ci[..., None] * b_im
    bbi = cr[..., None] * b_im + ci[..., None] * b_re
    kk = jnp.arange(T + 1, dtype=F32)[:, None, None, None]
    pm = jnp.exp(kk * xr)
    pw = jnp.stack([pm * jnp.cos(kk * xi), pm * jnp.sin(kk * xi)], axis=2)
    pw = pw.reshape(T + 1, 2, 2, N_PAIRS, 2 * P).transpose(3, 0, 1, 2, 4).reshape(N_PAIRS, T + 1, 8 * P)
    pw = jnp.concatenate([pw, jnp.zeros((N_PAIRS, 24 - (T + 1), 8 * P), F32)], axis=1)
    eye2 = jnp.eye(2, dtype=F32)

    def expand(t):
        t = t.transpose(2, 0, 1, 3, 4, 5)
        t = t[:, :, :, :, :, None, :] * eye2[None, None, None, :, None, :, None]
        return t.reshape(N_PAIRS, 4, 2 * H, 2 * P)

    bb = expand(jnp.stack([bbr, bbi], axis=1).reshape(2, 2, N_PAIRS, 2, P, H).transpose(0, 1, 2, 3, 5, 4))
    cc = expand(jnp.stack([c_re, c_im], axis=1).reshape(2, 2, N_PAIRS, 2, H, P))
    dd = d.reshape(N_PAIRS, 2 * H)[:, :, None] * jnp.eye(2 * H, dtype=F32)[None]
    dd = jnp.concatenate([dd, jnp.zeros((N_PAIRS, 2 * H, LANES - 2 * H), F32)], axis=2)
    return pw, bb, cc, dd


def _na_bias(rpb, rows):
    del rows
    ri = jnp.arange(NA_QROWS)[:, None, None]
    kr = jnp.arange(NA_KROWS)[None, :, None]
    dr = jnp.arange(2 * NA_ROWS - 1)[None, None, :]
    r0 = ((kr <= 7) & (kr - ri + 7 == dr))
    r1 = ((kr >= ri) & (kr <= ri + 7) & (kr - ri + 3 == dr))
    r2 = ((kr >= 4) & (kr - ri - 1 == dr))
    rsel = jnp.stack([r0, r1, r2], axis=0).astype(F32)
    rsel = rsel.reshape(3, NA_QROWS, NA_KROWS // 2, 2, 2 * NA_ROWS - 1)
    qc = jnp.arange(GRID_W)[:, None, None]
    kc = (jnp.arange(2 * GRID_W) % GRID_W)[None, :, None]
    dc = jnp.arange(2 * NA_COLS - 1)[None, None, :]
    cstart = jnp.clip(qc - NA_COLS // 2, 0, GRID_W - NA_COLS)
    csel = ((kc >= cstart) & (kc < cstart + NA_COLS) & (kc - qc + NA_COLS - 1 == dc)).astype(F32)
    odd = (jnp.arange(2 * GRID_W) >= GRID_W)
    part = jnp.einsum('hrc,qmc->hrqm', rpb, csel, precision=HIGHEST)
    col_ok = jnp.sum(csel, axis=-1) > 0.5
    core = jnp.where(odd, jnp.einsum('hrqm,yikr->hyikqm', part, rsel[:, :, :, 1], precision=HIGHEST),
                     jnp.einsum('hrqm,yikr->hyikqm', part, rsel[:, :, :, 0], precision=HIGHEST))
    row_ok = jnp.sum(rsel, axis=-1) > 0.5
    valid = jnp.where(odd, row_ok[..., 1, None, None], row_ok[..., 0, None, None]) & col_ok
    return jnp.where(valid[None], core, NEG)


def _fwd_in(x2, g_pre, w_in_g):
    seq = x2.shape[0]

    def body(x_ref, g_ref, w_ref, u_ref, zs_ref, qkv_ref, zn_ref, hn_ref):
        x = x_ref[...]
        r = lax.rsqrt(jnp.mean(x * x, axis=-1, keepdims=True) + EPS)
        hn = (x * r * g_ref[...]).astype(BF16)
        hn_ref[...] = hn
        for j in range(N_DEV):
            pj = _mm(hn, w_ref[j])
            for i in range(SHARD_IN // LANES):
                blk = (SHARD_IN // LANES) * j + i
                piece = pj[:, i * LANES:(i + 1) * LANES]
                if blk < 4:
                    u_ref[:, blk * LANES:(blk + 1) * LANES] = piece
                elif blk < 8:
                    zs_ref[:, (blk - 4) * LANES:(blk - 3) * LANES] = piece
                elif blk < 20:
                    qkv_ref[:, (blk - 8) * LANES:(blk - 7) * LANES] = piece.astype(BF16)
                else:
                    zn_ref[:, (blk - 20) * LANES:(blk - 19) * LANES] = piece

    tok = lambda w: pl.BlockSpec((TOK_TILE, w), lambda i: (i, 0))
    return pl.pallas_call(
        body, name="fwd_in", grid=(seq // TOK_TILE,),
        in_specs=[tok(D_MODEL), pl.BlockSpec((1, D_MODEL), lambda i: (0, 0)), _whole_vmem()],
        out_specs=[tok(D_SSM), tok(D_SSM), tok(3 * D_NA), tok(D_NA), tok(D_MODEL)],
        out_shape=[jax.ShapeDtypeStruct((seq, D_SSM), F32), jax.ShapeDtypeStruct((seq, D_SSM), F32),
                   jax.ShapeDtypeStruct((seq, 3 * D_NA), BF16), jax.ShapeDtypeStruct((seq, D_NA), F32),
                   jax.ShapeDtypeStruct((seq, D_MODEL), BF16)],
        compiler_params=_params("arbitrary"),
    )(x2, g_pre, w_in_g)


def _mid_fwd(yssm, zs, o, zn, x2, p2, tgt, w_glu, b_glu, w_out, g_post, w_ple_g, g_ple, w_pg):
    seq = x2.shape[0]

    def body(yssm_ref, zs_ref, o_ref, zn_ref, x_ref, p_ref, tgt_ref, wglu_ref, bglu_ref, wout_ref, gpost_ref,
             wple_ref, gple_ref, wpg_ref,
             dcat_ref, dh1_ref, loss_ref, dwout_ref, dwple_ref, dwpg_ref, dgpost_ref, dgple_ref):
        @pl.when(pl.program_id(0) == 0)
        def _():
            loss_ref[...] = jnp.zeros_like(loss_ref)
            dwout_ref[...] = jnp.zeros_like(dwout_ref)
            dwple_ref[...] = jnp.zeros_like(dwple_ref)
            dwpg_ref[...] = jnp.zeros_like(dwpg_ref)
            dgpost_ref[...] = jnp.zeros_like(dgpost_ref)
            dgple_ref[...] = jnp.zeros_like(dgple_ref)

        g1 = _gelu(yssm_ref[...])
        t = _mm(g1.astype(BF16), wglu_ref[...]) + bglu_ref[...]
        zs_v = zs_ref[...]
        ys = g1 * _sigmoid(t) * (zs_v * _sigmoid(zs_v))
        zn_v = zn_ref[...]
        yn = o_ref[...] * (zn_v * _sigmoid(zn_v))
        cat = jnp.concatenate([ys, yn], axis=1).astype(BF16)
        mix = _mm(cat, wout_ref[...])
        r2 = lax.rsqrt(jnp.mean(mix * mix, axis=-1, keepdims=True) + EPS)
        n2 = mix * r2
        gpost = gpost_ref[...]
        h1 = x_ref[...] + n2 * gpost
        pb = p_ref[...].astype(BF16)
        epre = jnp.concatenate([_mm(pb, wple_ref[j]) for j in range(N_DEV)], axis=1)
        r3 = lax.rsqrt(jnp.mean(epre * epre, axis=-1, keepdims=True) + EPS)
        n3 = epre * r3
        gple = gple_ref[...]
        e = n3 * gple
        h1b = h1.astype(BF16)
        gate = _sigmoid(_mm(h1b, wpg_ref[...]))
        diff = h1 + gate * e - tgt_ref[...]
        loss_ref[...] += jnp.sum(diff * diff, axis=0, keepdims=True)

        dy = diff * (1.0 / D_MODEL)
        dgp = (dy * e * gate * (1.0 - gate)).astype(BF16)
        de = dy * gate
        dh1 = dy + _nt(dgp, wpg_ref[...])
        dh1_ref[...] = dh1
        dwpg_ref[...] += _tn(h1b, dgp)
        dgple_ref[...] += jnp.sum(de * n3, axis=0, keepdims=True)
        dn3 = de * gple
        depre = (r3 * (dn3 - n3 * jnp.mean(dn3 * n3, axis=-1, keepdims=True))).astype(BF16)
        for j in range(N_DEV):
            dwple_ref[j] += _tn(pb, depre[:, j * LANES:(j + 1) * LANES])
        dgpost_ref[...] += jnp.sum(dh1 * n2, axis=0, keepdims=True)
        dn2 = dh1 * gpost
        dmix = (r2 * (dn2 - n2 * jnp.mean(dn2 * n2, axis=-1, keepdims=True))).astype(BF16)
        dcat_ref[...] = _nt(dmix, wout_ref[...])
        dwout_ref[...] += _tn(cat, dmix)

    tok = lambda w: pl.BlockSpec((TOK_TILE, w), lambda i: (i, 0))
    row = lambda w: pl.BlockSpec((1, w), lambda i: (0, 0))
    vm = _whole_vmem()
    return pl.pallas_call(
        body, name="mid_fwd", grid=(seq // TOK_TILE,),
        in_specs=[tok(D_SSM), tok(D_SSM), tok(D_NA), tok(D_NA), tok(D_MODEL), tok(D_PLE), tok(D_MODEL),
                  vm, row(D_SSM), vm, row(D_MODEL), vm, row(D_MODEL), vm],
        out_specs=[tok(D_MODEL), tok(D_MODEL), vm, vm, vm, vm, vm, vm],
        out_shape=[jax.ShapeDtypeStruct((seq, D_MODEL), F32), jax.ShapeDtypeStruct((seq, D_MODEL), F32),
                   jax.ShapeDtypeStruct((1, D_MODEL), F32),
                   jax.ShapeDtypeStruct((D_MODEL, D_MODEL), F32),
                   jax.ShapeDtypeStruct((N_DEV, D_PLE, LANES), F32),
                   jax.ShapeDtypeStruct((D_MODEL, D_MODEL), F32),
                   jax.ShapeDtypeStruct((1, D_MODEL), F32), jax.ShapeDtypeStruct((1, D_MODEL), F32)],
        compiler_params=_params("arbitrary"),
    )(yssm, zs, o, zn, x2, p2, tgt, w_glu, b_glu, w_out, g_post, w_ple_g, g_ple, w_pg)


def _mid_bwd(yssm, zs, o, zn, dcat, w_glu, b_glu):
    seq = yssm.shape[0]

    def body(yssm_ref, zs_ref, o_ref, zn_ref, dcat_ref, wglu_ref, bglu_ref,
             dyssm_ref, dzs_ref, do_ref, dzn_ref, dwglu_ref, dbglu_ref):
        @pl.when(pl.program_id(0) == 0)
        def _():
            dwglu_ref[...] = jnp.zeros_like(dwglu_ref)
            dbglu_ref[...] = jnp.zeros_like(dbglu_ref)

        dys = dcat_ref[:, :D_SSM]
        dyn = dcat_ref[:, D_SSM:]
        yv = yssm_ref[...]
        g1 = _gelu(yv)
        g1b = g1.astype(BF16)
        sg = _sigmoid(_mm(g1b, wglu_ref[...]) + bglu_ref[...])
        zs_v = zs_ref[...]
        s_zs = _sigmoid(zs_v)
        dg2 = dys * (zs_v * s_zs)
        dzs_ref[...] = dys * (g1 * sg) * (s_zs * (1.0 + zs_v * (1.0 - s_zs)))
        dt = dg2 * g1 * sg * (1.0 - sg)
        dtb = dt.astype(BF16)
        dg1 = dg2 * sg + _nt(dtb, wglu_ref[...])
        dwglu_ref[...] += _tn(g1b, dtb)
        dbglu_ref[...] += jnp.sum(dt, axis=0, keepdims=True)
        dyssm_ref[...] = dg1 * _gelu_grad(yv)
        zn_v = zn_ref[...]
        s_zn = _sigmoid(zn_v)
        do_ref[...] = dyn * (zn_v * s_zn)
        dzn_ref[...] = dyn * o_ref[...] * (s_zn * (1.0 + zn_v * (1.0 - s_zn)))

    tok = lambda w: pl.BlockSpec((TOK_TILE, w), lambda i: (i, 0))
    vm = _whole_vmem()
    half = jax.ShapeDtypeStruct((seq, D_SSM), F32)
    return pl.pallas_call(
        body, name="mid_bwd", grid=(seq // TOK_TILE,),
        in_specs=[tok(D_SSM), tok(D_SSM), tok(D_NA), tok(D_NA), tok(D_MODEL), vm,
                  pl.BlockSpec((1, D_SSM), lambda i: (0, 0))],
        out_specs=[tok(D_SSM), tok(D_SSM), tok(D_NA), tok(D_NA), vm, vm],
        out_shape=[half, half, half, half, jax.ShapeDtypeStruct((D_SSM, D_SSM), F32),
                   jax.ShapeDtypeStruct((1, D_SSM), F32)],
        compiler_params=_params("arbitrary"),
    )(yssm, zs, o, zn, dcat, w_glu, b_glu)


def _bwd_in(du, dzs, dq, dk, dv, dzn, hn, x2, dh1, g_pre, w_in_g):
    seq = x2.shape[0]

    def body(du_ref, dzs_ref, dq_ref, dk_ref, dv_ref, dzn_ref, hn_ref, x_ref, dh1_ref, g_ref, w_ref,
             dx_ref, dw_ref, dg_ref, dproj_ref):
        @pl.when(pl.program_id(0) == 0)
        def _():
            dw_ref[...] = jnp.zeros_like(dw_ref)
            dg_ref[...] = jnp.zeros_like(dg_ref)

        for k, ref in enumerate((du_ref, dzs_ref, dq_ref, dk_ref, dv_ref, dzn_ref)):
            dproj_ref[:, k * D_SSM:(k + 1) * D_SSM] = ref[...].astype(BF16)
        hn = hn_ref[...]
        dhn = jnp.zeros((TOK_TILE, D_MODEL), F32)
        for j in range(N_DEV):
            dpj = dproj_ref[:, j * SHARD_IN:(j + 1) * SHARD_IN]
            dhn += _nt(dpj, w_ref[j])
            dw_ref[j] += _tn(hn, dpj)
        x = x_ref[...]
        r = lax.rsqrt(jnp.mean(x * x, axis=-1, keepdims=True) + EPS)
        n1 = x * r
        dg_ref[...] += jnp.sum(dhn * n1, axis=0, keepdims=True)
        dn1 = dhn * g_ref[...]
        dx_ref[...] = dh1_ref[...] + r * (dn1 - n1 * jnp.mean(dn1 * n1, axis=-1, keepdims=True))

    tok = lambda w: pl.BlockSpec((TOK_TILE, w), lambda i: (i, 0))
    vm = _whole_vmem()
    return pl.pallas_call(
        body, name="bwd_in", grid=(seq // TOK_TILE,),
        in_specs=[tok(D_SSM)] * 6 + [tok(D_MODEL), tok(D_MODEL), tok(D_MODEL),
                                      pl.BlockSpec((1, D_MODEL), lambda i: (0, 0)), vm],
        out_specs=[tok(D_MODEL), vm, vm],
        out_shape=[jax.ShapeDtypeStruct((seq, D_MODEL), F32),
                   jax.ShapeDtypeStruct((N_DEV, D_MODEL, SHARD_IN), F32),
                   jax.ShapeDtypeStruct((1, D_MODEL), F32)],
        scratch_shapes=[pltpu.VMEM((TOK_TILE, D_IN_PROJ), BF16)],
        compiler_params=_params("arbitrary"),
    )(du, dzs, dq, dk, dv, dzn, hn, x2, dh1, g_pre, w_in_g)


RELAYOUT_CHUNKS = 128
PAIR_W = 2 * SSM_GROUP
PAIRS_PER_BLOCK = LANES // PAIR_W
CHUNK_W = SSM_CHUNK * PAIR_W
PW_ROWS = 24


def _lane_window(lo, width):
    lane = lax.broadcasted_iota(jnp.int32, (1, LANES), 1)
    return (lane >= lo) & (lane < lo + width)


def _to_pairs(a, nc):
    ncb = min(RELAYOUT_CHUNKS, nc)

    def body(x_ref, out_ref):
        xs = [x_ref[pl.ds(s, ncb, stride=SSM_CHUNK), :] for s in range(SSM_CHUNK)]
        for a_ in range(PAIRS_PER_BLOCK):
            for v in range(SSM_CHUNK // PAIRS_PER_BLOCK):
                acc = None
                for i in range(PAIRS_PER_BLOCK):
                    shift = (PAIR_W * (i - a_)) % LANES
                    piece = xs[PAIRS_PER_BLOCK * v + i]
                    piece = pltpu.roll(piece, shift, axis=1) if shift else piece
                    acc = piece if acc is None else jnp.where(_lane_window(PAIR_W * i, PAIR_W), piece, acc)
                out_ref[a_, :, LANES * v:LANES * (v + 1)] = acc.astype(BF16)

    return pl.pallas_call(
        body, name="to_pairs", grid=(D_SSM // LANES, nc // ncb),
        in_specs=[pl.BlockSpec((ncb * SSM_CHUNK, LANES), lambda cl, cb: (cb, cl))],
        out_specs=pl.BlockSpec((PAIRS_PER_BLOCK, ncb, CHUNK_W), lambda cl, cb: (cl, cb, 0)),
        out_shape=jax.ShapeDtypeStruct((N_PAIRS, nc, CHUNK_W), BF16),
        compiler_params=_params("arbitrary", "arbitrary"),
    )(a)


def _from_pairs(a, nc):
    ncb = min(RELAYOUT_CHUNKS, nc)

    def body(y_ref, out_ref):
        for s in range(SSM_CHUNK):
            v, i = divmod(s, PAIRS_PER_BLOCK)
            acc = None
            for a_ in range(PAIRS_PER_BLOCK):
                shift = (PAIR_W * (a_ - i)) % LANES
                piece = y_ref[a_, :, LANES * v:LANES * (v + 1)]
                piece = pltpu.roll(piece, shift, axis=1) if shift else piece
                acc = piece if acc is None else jnp.where(_lane_window(PAIR_W * a_, PAIR_W), piece, acc)
            out_ref[pl.ds(s, ncb, stride=SSM_CHUNK), :] = acc

    return pl.pallas_call(
        body, name="from_pairs", grid=(D_SSM // LANES, nc // ncb),
        in_specs=[pl.BlockSpec((PAIRS_PER_BLOCK, ncb, CHUNK_W), lambda cl, cb: (cl, cb, 0))],
        out_specs=pl.BlockSpec((ncb * SSM_CHUNK, LANES), lambda cl, cb: (cb, cl)),
        out_shape=jax.ShapeDtypeStruct((nc * SSM_CHUNK, D_SSM), F32),
        compiler_params=_params("arbitrary", "arbitrary"),
    )(a)


def _boundary_scan(nc, pw_ref, src_ref, dst_ref, conj):
    nblk = nc // 8
    lr0, li0, lr1, li1 = (pw_ref[0, SSM_CHUNK:SSM_CHUNK + 1, LANES * q:LANES * (q + 1)] for q in range(4))
    if conj:
        li0, li1 = -li0, -li1

    def step(i, carry):
        hr0, hi0, hr1, hi1 = carry
        up = pl.multiple_of(i * 8, 8)
        dn = pl.multiple_of((nblk - 1 - i) * 8, 8)
        a_rows, b_rows = (dn, up) if conj else (up, dn)
        s_r0 = src_ref[pl.ds(a_rows, 8), 0:128]
        s_i0 = src_ref[pl.ds(a_rows, 8), 128:256]
        s_r1 = src_ref[pl.ds(b_rows, 8), 256:384]
        s_i1 = src_ref[pl.ds(b_rows, 8), 384:512]
        o_r0, o_i0, o_r1, o_i1 = [], [], [], []
        for k in range(8):
            ka = 7 - k if conj else k
            kb = k if conj else 7 - k
            o_r0.append((ka, hr0))
            o_i0.append((ka, hi0))
            o_r1.append((kb, hr1))
            o_i1.append((kb, hi1))
            hr0, hi0 = (lr0 * hr0 - li0 * hi0 + s_r0[ka:ka + 1], lr0 * hi0 + li0 * hr0 + s_i0[ka:ka + 1])
            hr1, hi1 = (lr1 * hr1 - li1 * hi1 + s_r1[kb:kb + 1], lr1 * hi1 + li1 * hr1 + s_i1[kb:kb + 1])

        def rows(items):
            return jnp.concatenate([v for _, v in sorted(items, key=lambda kv: kv[0])], axis=0)

        dst_ref[pl.ds(a_rows, 8), 0:128] = rows(o_r0)
        dst_ref[pl.ds(a_rows, 8), 128:256] = rows(o_i0)
        dst_ref[pl.ds(b_rows, 8), 256:384] = rows(o_r1)
        dst_ref[pl.ds(b_rows, 8), 384:512] = rows(o_i1)
        return hr0, hi0, hr1, hi1

    z = jnp.zeros((1, LANES), F32)
    lax.fori_loop(0, nblk, step, (z, z, z, z))


def _pw_row(pw_ref, k, q):
    return pw_ref[0, k:k + 1, LANES * q:LANES * (q + 1)]


def _mm_f32(a, b, dims):
    return lax.dot_general(a, b, (dims, ((), ())), precision=HIGHEST, preferred_element_type=F32)


_POW_M = (lambda s: SSM_CHUNK - 1 - s, lambda s: s)
_POW_C = (lambda s: s + 1, lambda s: SSM_CHUNK - s)
_POW_K = (lambda s: s, lambda s: SSM_CHUNK - 1 - s)


def _chunk_matrices(pw_ref, bb_ref, cc_ref, dd_ref, m_scr, ct_scr, toep_scr, g_scr, kt_scr):
    blk = lambda s: slice(PAIR_W * s, PAIR_W * (s + 1))
    col = lambda q: slice(LANES * q, LANES * (q + 1))
    for d in range(2):
        bbr, bbi = bb_ref[0, 2 * d], bb_ref[0, 2 * d + 1]
        ccr, cci = cc_ref[0, 2 * d], cc_ref[0, 2 * d + 1]
        for s in range(SSM_CHUNK):
            pr, pi = _pw_row(pw_ref, _POW_M[d](s), 2 * d), _pw_row(pw_ref, _POW_M[d](s), 2 * d + 1)
            m_scr[blk(s), col(2 * d)] = (pr * bbr - pi * bbi).astype(m_scr.dtype)
            m_scr[blk(s), col(2 * d + 1)] = (pr * bbi + pi * bbr).astype(m_scr.dtype)
            pr, pi = _pw_row(pw_ref, _POW_C[d](s), 2 * d), _pw_row(pw_ref, _POW_C[d](s), 2 * d + 1)
            ct_scr[blk(s), col(2 * d)] = (ccr * pr - cci * pi).astype(ct_scr.dtype)
            ct_scr[blk(s), col(2 * d + 1)] = (-(ccr * pi + cci * pr)).astype(ct_scr.dtype)
            pr, pi = _pw_row(pw_ref, _POW_K[d](s), 2 * d), _pw_row(pw_ref, _POW_K[d](s), 2 * d + 1)
            g_scr[d, blk(s), 0:LANES] = ccr * pr - cci * pi
            g_scr[d, blk(s), LANES:2 * LANES] = -(ccr * pi + cci * pr)
        kt = _mm_f32(jnp.concatenate([bbr, bbi], axis=1), g_scr[d], ((1,), (1,)))
        if d == 0:
            kt = jnp.concatenate([kt[:, 0:LANES] + dd_ref[0], kt[:, LANES:]], axis=1)
        kt_scr[d] = kt
    lane = lax.broadcasted_iota(jnp.int32, (1, CHUNK_W), 1)
    for s in range(SSM_CHUNK):
        lo = PAIR_W * s
        hi = PAIR_W * (s + 1)
        fwd = kt_scr[0] if s == 0 else pltpu.roll(kt_scr[0], lo, axis=1)
        bwd = kt_scr[1] if hi == CHUNK_W else pltpu.roll(kt_scr[1], hi, axis=1)
        row = jnp.where(lane >= lo, fwd, 0.0) + jnp.where(lane < hi, bwd, 0.0)
        toep_scr[blk(s), :] = row.astype(toep_scr.dtype)


def _ssm_scratch(nc, mat_dtype):
    return [pltpu.VMEM((CHUNK_W, CHUNK_W), mat_dtype), pltpu.VMEM((CHUNK_W, CHUNK_W), mat_dtype),
            pltpu.VMEM((CHUNK_W, CHUNK_W), mat_dtype), pltpu.VMEM((2, CHUNK_W, 2 * LANES), F32),
            pltpu.VMEM((2, PAIR_W, CHUNK_W), F32), pltpu.VMEM((nc, CHUNK_W), F32), pltpu.VMEM((nc, CHUNK_W), F32)]


def _per_pair(*shape):
    return pl.BlockSpec((1,) + shape, lambda g: (g,) + (0,) * len(shape))


_TABLE_SPECS = lambda: [_per_pair(PW_ROWS, CHUNK_W), _per_pair(4, PAIR_W, LANES), _per_pair(4, PAIR_W, LANES),
                        _per_pair(PAIR_W, LANES)]


def _ssm_fwd(u, pw, bb, cc, dd):
    npair, nc, width = u.shape

    def body(u_ref, pw_ref, bb_ref, cc_ref, dd_ref, y_ref, hin_ref, m_scr, ct_scr, toep_scr, g_scr, kt_scr, s_scr, h_scr):
        _chunk_matrices(pw_ref, bb_ref, cc_ref, dd_ref, m_scr, ct_scr, toep_scr, g_scr, kt_scr)
        uv = u_ref[0]
        s_scr[...] = _mm(uv, m_scr[...])
        _boundary_scan(nc, pw_ref, s_scr, h_scr, conj=False)
        hin = h_scr[...]
        hin_ref[0] = hin
        y_ref[0] = _mm(uv, toep_scr[...]) + _nt(hin.astype(u.dtype), ct_scr[...])

    return pl.pallas_call(
        body, name="ssm_fwd", grid=(npair,),
        in_specs=[_per_pair(nc, width)] + _TABLE_SPECS(),
        out_specs=[_per_pair(nc, width), _per_pair(nc, width)],
        out_shape=[jax.ShapeDtypeStruct((npair, nc, width), F32), jax.ShapeDtypeStruct((npair, nc, width), F32)],
        scratch_shapes=_ssm_scratch(nc, u.dtype),
        compiler_params=_params("arbitrary"),
    )(u, pw, bb, cc, dd)


def _ssm_bwd(u, dy, hin, pw, bb, cc, dd):
    npair, nc, width = u.shape

    def body(u_ref, dy_ref, hin_ref, pw_ref, bb_ref, cc_ref, dd_ref, du_ref, dpw_ref, dbb_ref, dcc_ref, ddd_ref,
             m_scr, ct_scr, toep_scr, g_scr, kt_scr, dh_scr, ds_scr):
        _chunk_matrices(pw_ref, bb_ref, cc_ref, dd_ref, m_scr, ct_scr, toep_scr, g_scr, kt_scr)
        uv = u_ref[0]
        dyb = dy_ref[0]
        hin = hin_ref[0]
        dh_scr[...] = _mm(dyb, ct_scr[...])
        _boundary_scan(nc, pw_ref, dh_scr, ds_scr, conj=True)
        ds = ds_scr[...]
        dsb = ds.astype(u.dtype)
        du_ref[0] = _nt(dsb, m_scr[...]) + _nt(dyb, toep_scr[...])
        dm = _tn(uv, dsb)
        dct = _tn(dyb, hin.astype(u.dtype))
        dtoep = _tn(uv, dyb)

        dpw_ref[...] = jnp.zeros_like(dpw_ref)
        blk = lambda s: slice(PAIR_W * s, PAIR_W * (s + 1))
        col = lambda q: slice(LANES * q, LANES * (q + 1))

        def add_pw(k, q, val):
            dpw_ref[0, k:k + 1, col(q)] += jnp.sum(val, axis=0, keepdims=True)

        for d in range(2):
            g_r, g_i = ds[:, col(2 * d)], ds[:, col(2 * d + 1)]
            h_r, h_i = hin[:, col(2 * d)], hin[:, col(2 * d + 1)]
            add_pw(SSM_CHUNK, 2 * d, g_r * h_r + g_i * h_i)
            add_pw(SSM_CHUNK, 2 * d + 1, g_i * h_r - g_r * h_i)

        lane = lax.broadcasted_iota(jnp.int32, (1, CHUNK_W), 1)
        dkt0 = jnp.zeros((PAIR_W, CHUNK_W), F32)
        dkt1 = jnp.zeros((PAIR_W, CHUNK_W), F32)
        for s in range(SSM_CHUNK):
            lo = PAIR_W * s
            hi = PAIR_W * (s + 1)
            row = dtoep[blk(s), :]
            fwd = jnp.where(lane >= lo, row, 0.0)
            bwd = jnp.where(lane < hi, row, 0.0)
            dkt0 += fwd if s == 0 else pltpu.roll(fwd, CHUNK_W - lo, axis=1)
            dkt1 += bwd if hi == CHUNK_W else pltpu.roll(bwd, CHUNK_W - hi, axis=1)
        ddd_ref[0] = dkt0[:, 0:LANES]

        for d, dkt in enumerate((dkt0, dkt1)):
            bbr, bbi = bb_ref[0, 2 * d], bb_ref[0, 2 * d + 1]
            ccr, cci = cc_ref[0, 2 * d], cc_ref[0, 2 * d + 1]
            dbbcat = _mm_f32(dkt, g_scr[d], ((1,), (0,)))
            dg = _mm_f32(dkt, jnp.concatenate([bbr, bbi], axis=1), ((0,), (0,)))
            dbbr, dbbi = dbbcat[:, 0:LANES], dbbcat[:, LANES:]
            dccr = jnp.zeros((PAIR_W, LANES), F32)
            dcci = jnp.zeros((PAIR_W, LANES), F32)
            for s in range(SSM_CHUNK):
                k = _POW_M[d](s)
                pr, pi = _pw_row(pw_ref, k, 2 * d), _pw_row(pw_ref, k, 2 * d + 1)
                gr, gi = dm[blk(s), col(2 * d)], dm[blk(s), col(2 * d + 1)]
                dbbr += gr * pr + gi * pi
                dbbi += gi * pr - gr * pi
                add_pw(k, 2 * d, gr * bbr + gi * bbi)
                add_pw(k, 2 * d + 1, gi * bbr - gr * bbi)
                for k, gr, gi in ((_POW_C[d](s), dct[blk(s), col(2 * d)], dct[blk(s), col(2 * d + 1)]),
                                  (_POW_K[d](s), dg[blk(s), 0:LANES], dg[blk(s), LANES:])):
                    pr, pi = _pw_row(pw_ref, k, 2 * d), _pw_row(pw_ref, k, 2 * d + 1)
                    dccr += gr * pr - gi * pi
                    dcci += -(gr * pi + gi * pr)
                    add_pw(k, 2 * d, gr * ccr - gi * cci)
                    add_pw(k, 2 * d + 1, -(gr * cci + gi * ccr))
            dbb_ref[0, 2 * d] = dbbr
            dbb_ref[0, 2 * d + 1] = dbbi
            dcc_ref[0, 2 * d] = dccr
            dcc_ref[0, 2 * d + 1] = dcci

    return pl.pallas_call(
        body, name="ssm_bwd", grid=(npair,),
        in_specs=[_per_pair(nc, width), _per_pair(nc, width), _per_pair(nc, width)] + _TABLE_SPECS(),
        out_specs=[_per_pair(nc, width)] + _TABLE_SPECS(),
        out_shape=[jax.ShapeDtypeStruct((npair, nc, width), F32), jax.ShapeDtypeStruct(pw.shape, F32),
                   jax.ShapeDtypeStruct(bb.shape, F32), jax.ShapeDtypeStruct(cc.shape, F32),
                   jax.ShapeDtypeStruct(dd.shape, F32)],
        scratch_shapes=_ssm_scratch(nc, u.dtype),
        compiler_params=_params("arbitrary"),
    )(u, dy, hin, pw, bb, cc, dd)


NA_Q = NA_QROWS * GRID_W
NA_K = NA_KROWS * GRID_W
NA_SCALE = NA_HEAD_DIM ** -0.5
NA_BIAS_BLOCK = (3, NA_QROWS, NA_KROWS // 2, GRID_W, LANES)


def _na_block(b, nb, rows):
    start = jnp.clip(NA_QROWS * b - NA_ROWS // 2, 0, rows - NA_KROWS) * GRID_W
    kind = jnp.where(b == 0, 0, jnp.where(b == nb - 1, 2, 1))
    return pl.multiple_of(start, GRID_W), kind


NA_CHUNK = 16


def _na_bias_rows(bias_ref, hh, kind, i):
    ri, q0 = divmod(i * NA_CHUNK, GRID_W)
    return [(hh, kind, ri, k2, slice(q0, q0 + NA_CHUNK), slice(None)) for k2 in range(NA_KROWS // 2)]


def _na_softmax_rows(s_ref, bias_ref, hh, kind, i):
    rows = slice(i * NA_CHUNK, (i + 1) * NA_CHUNK)
    s = s_ref[hh, rows, :] + jnp.concatenate([bias_ref[ix] for ix in _na_bias_rows(bias_ref, hh, kind, i)], axis=1)
    e = jnp.exp(s - jnp.max(s, axis=-1, keepdims=True))
    return e * (1.0 / jnp.sum(e, axis=-1, keepdims=True))


def _na_heads():
    lane = lax.broadcasted_iota(jnp.int32, (1, LANES), 1)
    return [lane < NA_HEAD_DIM, lane >= NA_HEAD_DIM]


def _na_fwd(qkv, bias):
    seq = qkv.shape[0]
    rows = seq // GRID_W
    nb = rows // NA_QROWS

    def body(q_ref, k_ref, v_ref, bias_ref, o_ref, s_scr, p_scr):
        start, kind = _na_block(pl.program_id(1), nb, rows)
        q2 = q_ref[...] * NA_SCALE
        kw = k_ref[pl.ds(start, NA_K), :]
        vw = v_ref[pl.ds(start, NA_K), :]
        heads = _na_heads()
        for hh in range(2):
            s_scr[hh] = _nt(jnp.where(heads[hh], q2, jnp.zeros_like(q2)), kw)
        for hh in range(2):
            for i in range(NA_Q // NA_CHUNK):
                r = slice(i * NA_CHUNK, (i + 1) * NA_CHUNK)
                p_scr[hh, r, :] = _na_softmax_rows(s_scr, bias_ref, hh, kind, i).astype(p_scr.dtype)
        o_ref[...] = jnp.where(heads[0], _mm(p_scr[0], vw), _mm(p_scr[1], vw))

    return pl.pallas_call(
        body, name="na_fwd", grid=(NA_HEADS // 2, nb),
        in_specs=[pl.BlockSpec((NA_Q, LANES), lambda hp, b: (b, hp)),
                  pl.BlockSpec((seq, LANES), lambda hp, b: (0, 4 + hp)),
                  pl.BlockSpec((seq, LANES), lambda hp, b: (0, 8 + hp)),
                  pl.BlockSpec((2,) + NA_BIAS_BLOCK, lambda hp, b: (hp, 0, 0, 0, 0, 0))],
        out_specs=pl.BlockSpec((NA_Q, LANES), lambda hp, b: (b, hp)),
        out_shape=jax.ShapeDtypeStruct((seq, D_NA), F32),
        scratch_shapes=[pltpu.VMEM((2, NA_Q, NA_K), F32), pltpu.VMEM((2, NA_Q, NA_K), qkv.dtype)],
        compiler_params=_params("arbitrary", "arbitrary"),
    )(qkv, qkv, qkv, bias)


def _na_bwd(qkv, do, bias):
    seq = qkv.shape[0]
    rows = seq // GRID_W
    nb = rows // NA_QROWS

    def body(q_ref, k_ref, v_ref, do_ref, bias_ref, dq_ref, dk_ref, dv_ref, dbias_ref, s_scr, dp_scr, p_scr, ds_scr):
        b = pl.program_id(1)
        start, kind = _na_block(b, nb, rows)

        @pl.when(b == 0)
        def _():
            dk_ref[...] = jnp.zeros_like(dk_ref)
            dv_ref[...] = jnp.zeros_like(dv_ref)
            dbias_ref[...] = jnp.zeros_like(dbias_ref)

        q2 = q_ref[...] * NA_SCALE
        kw = k_ref[pl.ds(start, NA_K), :]
        vw = v_ref[pl.ds(start, NA_K), :]
        do2 = do_ref[...].astype(q2.dtype)
        heads = _na_heads()
        for hh in range(2):
            s_scr[hh] = _nt(jnp.where(heads[hh], q2, jnp.zeros_like(q2)), kw)
            dp_scr[hh] = _nt(jnp.where(heads[hh], do2, jnp.zeros_like(do2)), vw)
        for hh in range(2):
            for i in range(NA_Q // NA_CHUNK):
                r = slice(i * NA_CHUNK, (i + 1) * NA_CHUNK)
                p = _na_softmax_rows(s_scr, bias_ref, hh, kind, i)
                dp = dp_scr[hh, r, :]
                ds = p * (dp - jnp.sum(p * dp, axis=-1, keepdims=True))
                for k2, ix in enumerate(_na_bias_rows(dbias_ref, hh, kind, i)):
                    dbias_ref[ix] += ds[:, k2 * LANES:(k2 + 1) * LANES]
                p_scr[hh, r, :] = p.astype(p_scr.dtype)
                ds_scr[hh, r, :] = ds.astype(ds_scr.dtype)
        dq_ref[...] = jnp.where(heads[0], _mm(ds_scr[0], kw), _mm(ds_scr[1], kw)) * NA_SCALE
        dk_ref[pl.ds(start, NA_K), :] += jnp.where(heads[0], _tn(ds_scr[0], q2), _tn(ds_scr[1], q2))
        dv_ref[pl.ds(start, NA_K), :] += jnp.where(heads[0], _tn(p_scr[0], do2), _tn(p_scr[1], do2))

    return pl.pallas_call(
        body, name="na_bwd", grid=(NA_HEADS // 2, nb),
        in_specs=[pl.BlockSpec((NA_Q, LANES), lambda hp, b: (b, hp)),
                  pl.BlockSpec((seq, LANES), lambda hp, b: (0, 4 + hp)),
                  pl.BlockSpec((seq, LANES), lambda hp, b: (0, 8 + hp)),
                  pl.BlockSpec((NA_Q, LANES), lambda hp, b: (b, hp)),
                  pl.BlockSpec((2,) + NA_BIAS_BLOCK, lambda hp, b: (hp, 0, 0, 0, 0, 0))],
        out_specs=[pl.BlockSpec((NA_Q, LANES), lambda hp, b: (b, hp)),
                   pl.BlockSpec((seq, LANES), lambda hp, b: (0, hp)),
                   pl.BlockSpec((seq, LANES), lambda hp, b: (0, hp)),
                   pl.BlockSpec((2,) + NA_BIAS_BLOCK, lambda hp, b: (hp, 0, 0, 0, 0, 0))],
        out_shape=[jax.ShapeDtypeStruct((seq, D_NA), F32), jax.ShapeDtypeStruct((seq, D_NA), F32),
                   jax.ShapeDtypeStruct((seq, D_NA), F32),
                   jax.ShapeDtypeStruct((NA_HEADS,) + NA_BIAS_BLOCK, F32)],
        scratch_shapes=[pltpu.VMEM((2, NA_Q, NA_K), F32), pltpu.VMEM((2, NA_Q, NA_K), F32),
                        pltpu.VMEM((2, NA_Q, NA_K), qkv.dtype), pltpu.VMEM((2, NA_Q, NA_K), qkv.dtype)],
        compiler_params=_params("arbitrary", "arbitrary"),
    )(qkv, qkv, qkv, do, bias)


def _local_step(x2, p2, tgt, g_pre, g_post, w_in_g, ssm, w_glu, b_glu, rpb, w_out, w_ple_g, g_ple, w_pg):
    seq = x2.shape[0]
    nc = seq // SSM_CHUNK
    rows = seq // GRID_W

    (pw, bb, cc, dd), ssm_vjp = jax.vjp(_ssm_tables, *ssm)
    bias, bias_vjp = jax.vjp(functools.partial(_na_bias, rows=rows), rpb)

    u, zs, qkv, zn, hn = _fwd_in(x2, g_pre, w_in_g)
    u_p = _to_pairs(u, nc)
    y_p, hin = _ssm_fwd(u_p, pw, bb, cc, dd)
    yssm = _from_pairs(y_p, nc)
    o = _na_fwd(qkv, bias)
    dcat, dh1, sq, d_wout, d_wple, d_wpg, d_gpost, d_gple = _mid_fwd(
        yssm, zs, o, zn, x2, p2, tgt, w_glu, b_glu, w_out, g_post, w_ple_g, g_ple, w_pg)
    dyssm, dzs, do, dzn, d_wglu, d_bglu = _mid_bwd(yssm, zs, o, zn, dcat, w_glu, b_glu)
    dq, dk, dv, dbias = _na_bwd(qkv, do, bias)
    (d_rpb,) = bias_vjp(dbias)
    du_p, dpw, dbb, dcc, ddd = _ssm_bwd(u_p, _to_pairs(dyssm, nc), hin, pw, bb, cc, dd)
    d_ssm = ssm_vjp((dpw, dbb, dcc, ddd))
    du = _from_pairs(du_p, nc)
    dx, d_win, d_gpre = _bwd_in(du, dzs, dq, dk, dv, dzn, hn, x2, dh1, g_pre, w_in_g)
    return sq, dx, dict(norm_pre=d_gpre, norm_post=d_gpost, w_in=d_win, ssm=d_ssm, w_glu=d_wglu, b_glu=d_bglu,
                        na_rpb=d_rpb, w_out=d_wout, w_ple=d_wple, ple_norm=d_gple, w_ple_gate=d_wpg)


def _place():
    return lax.axis_index("x"), lax.axis_index("y"), lax.axis_index("c")


def _all_gather(shard):
    m_per, n = shard.shape

    def body(x_ref, out_ref, send_sems, recv_sems, local_sem):
        x, y, c = _place()
        me, sibling = (x, y, c), (x, y, 1 - c)
        chips = [(1 - x, y), (x, 1 - y), (1 - x, 1 - y)]

        def rows(px, py, pc):
            return out_ref.at[pl.ds((4 * px + 2 * py + pc) * m_per, m_per), :]

        def copy(k, block, to, src=None):
            return pltpu.make_async_remote_copy(
                src_ref=rows(*block) if src is None else src, dst_ref=rows(*block),
                send_sem=send_sems.at[k], recv_sem=recv_sems.at[k], device_id=to, device_id_type=MESH)

        mine = pltpu.make_async_copy(x_ref, rows(*me), local_sem)
        mine.start()
        first = [copy(0, me, sibling, src=x_ref)]
        first += [copy(1 + j, me, (*chip, c), src=x_ref) for j, chip in enumerate(chips)]
        for cp in first:
            cp.start()
        passed = [copy(4 + j, (*chip, c), sibling) for j, chip in enumerate(chips)]
        for j, chip in enumerate(chips):
            copy(1 + j, (*chip, c), me).wait_recv()
            passed[j].start()
        copy(0, sibling, me).wait_recv()
        for j, chip in enumerate(chips):
            copy(4 + j, (*chip, 1 - c), me).wait_recv()
        for cp in first + passed:
            cp.wait_send()
        mine.wait()

    return pl.pallas_call(
        body, name="all_gather",
        out_shape=jax.ShapeDtypeStruct((N_DEV * m_per, n), shard.dtype),
        in_specs=[_whole_vmem()], out_specs=_whole_vmem(),
        scratch_shapes=[pltpu.SemaphoreType.DMA((7,)), pltpu.SemaphoreType.DMA((7,)), pltpu.SemaphoreType.DMA],
        compiler_params=pltpu.CompilerParams(vmem_limit_bytes=VMEM_LIMIT),
    )(shard)


def _exchange_partials(part):
    _, r, n = part.shape

    def body(part_ref, land_ref, send_sems, recv_sems, local_sem):
        x, y, c = _place()
        me = 4 * x + 2 * y + c
        local = pltpu.make_async_copy(part_ref.at[me], land_ref.at[me], local_sem)
        local.start()
        copies = []
        for k in range(1, N_DEV):
            px, py, pc = x ^ ((k >> 2) & 1), y ^ ((k >> 1) & 1), c ^ (k & 1)
            peer = 4 * px + 2 * py + pc
            cp = pltpu.make_async_remote_copy(
                src_ref=part_ref.at[peer], dst_ref=land_ref.at[me],
                send_sem=send_sems.at[k - 1], recv_sem=recv_sems.at[k - 1],
                device_id=(px, py, pc), device_id_type=MESH)
            cp.start()
            copies.append(cp)
        for cp in copies:
            cp.wait_recv()
        for cp in copies:
            cp.wait_send()
        local.wait()

    anyspace = pl.BlockSpec(memory_space=pl.ANY)
    return pl.pallas_call(
        body, name="exchange_partials",
        out_shape=jax.ShapeDtypeStruct(part.shape, part.dtype),
        in_specs=[anyspace], out_specs=anyspace,
        scratch_shapes=[pltpu.SemaphoreType.DMA((N_DEV - 1,)), pltpu.SemaphoreType.DMA((N_DEV - 1,)),
                        pltpu.SemaphoreType.DMA],
    )(part)


def _pair_exchange(part):
    _, r, n = part.shape

    def body(part_ref, land_ref, send_sems, recv_sems):
        x, y, c = _place()
        copies = []
        for q in range(4):
            cp = pltpu.make_async_remote_copy(
                src_ref=part_ref.at[2 * q + (1 - c)], dst_ref=land_ref.at[q],
                send_sem=send_sems.at[q], recv_sem=recv_sems.at[q],
                device_id=(x, y, 1 - c), device_id_type=MESH)
            cp.start()
            copies.append(cp)
        for cp in copies:
            cp.wait_recv()
        for cp in copies:
            cp.wait_send()

    anyspace = pl.BlockSpec(memory_space=pl.ANY)
    return pl.pallas_call(
        body, name="pair_exchange",
        out_shape=jax.ShapeDtypeStruct((4, r, n), part.dtype),
        in_specs=[anyspace], out_specs=anyspace,
        scratch_shapes=[pltpu.SemaphoreType.DMA((4,)), pltpu.SemaphoreType.DMA((4,))],
    )(part)


def _pair_sum(part, land):
    _, r, n = part.shape
    tile = 256
    assert r % tile == 0

    def body(part_ref, land_ref, out_ref):
        c = lax.axis_index("c")
        for q in range(4):
            out_ref[q] = (part_ref[q, c] + land_ref[q]).astype(BF16)

    return pl.pallas_call(
        body, name="pair_sum", grid=(r // tile,),
        in_specs=[pl.BlockSpec((4, 2, tile, n), lambda i: (0, 0, i, 0)), pl.BlockSpec((4, tile, n), lambda i: (0, i, 0))],
        out_specs=pl.BlockSpec((4, tile, n), lambda i: (0, i, 0)),
        out_shape=jax.ShapeDtypeStruct((4, r, n), BF16),
        compiler_params=_params("arbitrary"),
    )(part.reshape(4, 2, r, n), land)


def _chip_exchange(chip_part):
    _, r, n = chip_part.shape

    def body(part_ref, land_ref, send_sems, recv_sems, local_sem):
        x, y, c = _place()
        mine = 2 * x + y
        local = pltpu.make_async_copy(part_ref.at[mine], land_ref.at[mine], local_sem)
        local.start()
        copies = []
        for k in range(1, 4):
            px, py = x ^ (k >> 1), y ^ (k & 1)
            cp = pltpu.make_async_remote_copy(
                src_ref=part_ref.at[2 * px + py], dst_ref=land_ref.at[mine],
                send_sem=send_sems.at[k - 1], recv_sem=recv_sems.at[k - 1],
                device_id=(px, py, c), device_id_type=MESH)
            cp.start()
            copies.append(cp)
        for cp in copies:
            cp.wait_recv()
        for cp in copies:
            cp.wait_send()
        local.wait()

    anyspace = pl.BlockSpec(memory_space=pl.ANY)
    return pl.pallas_call(
        body, name="chip_exchange",
        out_shape=jax.ShapeDtypeStruct(chip_part.shape, chip_part.dtype),
        in_specs=[anyspace], out_specs=anyspace,
        scratch_shapes=[pltpu.SemaphoreType.DMA((3,)), pltpu.SemaphoreType.DMA((3,)), pltpu.SemaphoreType.DMA],
    )(chip_part)


def _sum_slots(land):
    slots, r, n = land.shape
    tile = next(t for t in (512, 256, 128) if r % t == 0)

    def body(land_ref, out_ref):
        acc = land_ref[0].astype(F32)
        for j in range(1, slots):
            acc = acc + land_ref[j].astype(F32)
        out_ref[...] = acc

    return pl.pallas_call(
        body, name="sum_slots", grid=(r // tile,),
        in_specs=[pl.BlockSpec((slots, tile, n), lambda i: (0, i, 0))],
        out_specs=pl.BlockSpec((tile, n), lambda i: (i, 0)),
        out_shape=jax.ShapeDtypeStruct((r, n), F32),
        compiler_params=_params("arbitrary"),
    )(land)


def _adamw(w, g, m, v):
    r, c = w.shape
    tile = 256 if r % 256 == 0 else r

    def body(w_ref, g_ref, m_ref, v_ref, d_ref, nm_ref, nv_ref):
        g = g_ref[...]
        nm = ADAM_B1 * m_ref[...] + (1.0 - ADAM_B1) * g
        nv = ADAM_B2 * v_ref[...] + (1.0 - ADAM_B2) * (g * g)
        m_hat = nm / (1.0 - ADAM_B1 ** ADAM_STEP)
        v_hat = nv / (1.0 - ADAM_B2 ** ADAM_STEP)
        d_ref[...] = -ADAM_LR * (m_hat / (jnp.sqrt(v_hat) + ADAM_EPS) + ADAM_WD * w_ref[...])
        nm_ref[...] = nm
        nv_ref[...] = nv

    spec = pl.BlockSpec((tile, c), lambda i: (i, 0))
    out = jax.ShapeDtypeStruct((r, c), F32)
    return pl.pallas_call(
        body, name="adamw", grid=(r // tile,), in_specs=[spec] * 4, out_specs=[spec] * 3,
        out_shape=[out, out, out], compiler_params=_params("arbitrary"),
    )(w, g, m, v)


_SLAB = 8 * LANES


def _flat_rows(a):
    flat = a.reshape(-1)
    pad = (-flat.shape[0]) % _SLAB
    if pad:
        flat = jnp.concatenate([flat, jnp.zeros((pad,), flat.dtype)])
    return flat.reshape(-1, LANES)


def _pack(arrays):
    slabs = [_flat_rows(a) for a in arrays]
    return jnp.concatenate(slabs, axis=0), [s.shape[0] for s in slabs]


def _unpack(packed, like, rows):
    out, at = [], 0
    for a, r in zip(like, rows):
        out.append(packed[at:at + r].reshape(-1)[:a.size].reshape(a.shape))
        at += r
    return out


SMALL = ('norm_pre', 'norm_post', 'ssm_a_re', 'ssm_a_im', 'ssm_log_dt', 'ssm_b_re', 'ssm_b_im', 'ssm_c_re',
         'ssm_c_im', 'ssm_d', 'b_glu', 'na_rpb', 'ple_norm')
BIG = ('w_in', 'w_glu', 'w_out', 'w_ple', 'w_ple_gate')
ORDER = ('norm_pre', 'norm_post', 'w_in', 'ssm_a_re', 'ssm_a_im', 'ssm_log_dt', 'ssm_b_re', 'ssm_b_im', 'ssm_c_re',
         'ssm_c_im', 'ssm_d', 'w_glu', 'b_glu', 'na_rpb', 'w_out', 'w_ple', 'ple_norm', 'w_ple_gate')


def kernel(x, p, norm_pre, norm_post, w_in, ssm_a_re, ssm_a_im, ssm_log_dt, ssm_b_re, ssm_b_im, ssm_c_re, ssm_c_im, ssm_d, w_glu, b_glu, na_rpb, w_out, w_ple, ple_norm, w_ple_gate, loss_target, m_norm_pre, m_norm_post, m_w_in, m_ssm_a_re, m_ssm_a_im, m_ssm_log_dt, m_ssm_b_re, m_ssm_b_im, m_ssm_c_re, m_ssm_c_im, m_ssm_d, m_w_glu, m_b_glu, m_na_rpb, m_w_out, m_w_ple, m_ple_norm, m_w_ple_gate, v_norm_pre, v_norm_post, v_w_in, v_ssm_a_re, v_ssm_a_im, v_ssm_log_dt, v_ssm_b_re, v_ssm_b_im, v_ssm_c_re, v_ssm_c_im, v_ssm_d, v_w_glu, v_b_glu, v_na_rpb, v_w_out, v_w_ple, v_ple_norm, v_w_ple_gate):
    args = dict(locals())
    weights = {n: args[n] for n in ORDER}
    mom_m = {n: args["m_" + n] for n in ORDER}
    mom_v = {n: args["v_" + n] for n in ORDER}

    shards = [weights[n][0].astype(BF16) for n in BIG]
    packed, shard_rows = _pack(shards)
    gathered = _all_gather(packed).reshape(N_DEV, -1, LANES)
    full = []
    at = 0
    for s, r in zip(shards, shard_rows):
        full.append(gathered[:, at:at + r].reshape((N_DEV,) + s.shape))
        at += r
    w_in_g, w_glu_g, w_out_g, w_ple_g, w_pg_g = full
    w_glu_f = w_glu_g.reshape(D_SSM, D_SSM)
    w_out_f = w_out_g.reshape(D_MODEL, D_MODEL)
    w_pg_f = w_pg_g.reshape(D_MODEL, D_MODEL)

    ssm = tuple(weights[n][0] for n in ('ssm_a_re', 'ssm_a_im', 'ssm_log_dt', 'ssm_b_re', 'ssm_b_im',
                                        'ssm_c_re', 'ssm_c_im', 'ssm_d'))
    sq, dx, grads = _local_step(x[0], p[0, 0], loss_target[0], norm_pre, norm_post, w_in_g, ssm, w_glu_f, b_glu,
                                na_rpb[0], w_out_f, w_ple_g, ple_norm, w_pg_f)
    loss = lax.psum(0.5 / D_MODEL * jnp.sum(sq), ("x", "y", "c"))

    local = dict(norm_pre=grads['norm_pre'], norm_post=grads['norm_post'], b_glu=grads['b_glu'],
                 na_rpb=grads['na_rpb'][None], ple_norm=grads['ple_norm'])
    for n, g in zip(('ssm_a_re', 'ssm_a_im', 'ssm_log_dt', 'ssm_b_re', 'ssm_b_im', 'ssm_c_re', 'ssm_c_im', 'ssm_d'),
                    grads['ssm']):
        local[n] = g[None]

    big_parts = [grads['w_in'],
                 grads['w_glu'].reshape(N_DEV, D_SSM // N_DEV, D_SSM),
                 grads['w_out'].reshape(N_DEV, D_MODEL // N_DEV, D_MODEL),
                 grads['w_ple'],
                 grads['w_ple_gate'].reshape(N_DEV, D_MODEL // N_DEV, D_MODEL)]
    small_flat, small_rows = _pack([local[n] for n in SMALL])
    pad = (-small_flat.shape[0]) % (128 * N_DEV)
    if pad:
        small_flat = jnp.concatenate([small_flat, jnp.zeros((pad, LANES), F32)], axis=0)
    small_per = small_flat.shape[0] // N_DEV
    part = jnp.concatenate([bp.reshape(N_DEV, -1, LANES) for bp in big_parts], axis=1)
    reduced = _sum_slots(_chip_exchange(_pair_sum(part, _pair_exchange(part))))
    big_grads = {}
    at = 0
    for n, bp in zip(BIG, big_parts):
        r = bp[0].size // LANES
        big_grads[n] = reduced[at:at + r].reshape(bp.shape[1:])
        at += r
    small_all = _all_gather(_sum_slots(_exchange_partials(small_flat.reshape(N_DEV, small_per, LANES))))

    outs = {}
    for n in BIG:
        w2 = weights[n][0]
        d, nm, nv = _adamw(w2, big_grads[n], mom_m[n][0], mom_v[n][0])
        outs[n] = (big_grads[n][None], d[None], nm[None], nv[None])
    like = [weights[n] for n in SMALL]
    w_s, rows_s = _pack(like)
    m_s, _ = _pack([mom_m[n] for n in SMALL])
    v_s, _ = _pack([mom_v[n] for n in SMALL])
    g_s = small_all[:w_s.shape[0]]
    d_s, nm_s, nv_s = _adamw(w_s, g_s, m_s, v_s)
    unpacked = [_unpack(a, like, rows_s) for a in (g_s, d_s, nm_s, nv_s)]
    for i, n in enumerate(SMALL):
        outs[n] = tuple(u[i] for u in unpacked)

    return (loss, dx[None], *[outs[n][0] for n in ORDER], *[outs[n][1] for n in ORDER],
            *[outs[n][2] for n in ORDER], *[outs[n][3] for n in ORDER])
```

```python
import jax
import jax.numpy as jnp
from jax import lax
from jax.experimental import pallas as pl
from jax.experimental.pallas import tpu as pltpu

F32 = jnp.float32
BF16 = jnp.bfloat16
HIGHEST = lax.Precision.HIGHEST

D_MODEL = 1024
D_PLE = 256
GRID_W = 64
D_SSM = 512
SSM_GROUP = 16
N_GROUPS = 32
N_PAIRS = 16
SSM_STATE = 64
D_NA = 512
NA_HEADS = 8
NA_HEAD_DIM = 64
NA_ROWS = 8
NA_COLS = 16
D_IN_PROJ = 3072
EPS = 1e-6
N_DEV = 8
SHARD_IN = D_IN_PROJ // N_DEV
LANES = 128
SSM_CHUNK = 16
TOK_TILE = 256
NA_QROWS = 4
NA_KROWS = 12
NEG = -1e30
VMEM_LIMIT = 56 * 1024 * 1024

ADAM_LR = 0.001
ADAM_B1 = 0.9
ADAM_B2 = 0.999
ADAM_EPS = 1e-08
ADAM_WD = 0.01
ADAM_STEP = 10

MESH = pl.DeviceIdType.MESH


def _params(*sem):
    return pltpu.CompilerParams(dimension_semantics=sem or None, vmem_limit_bytes=VMEM_LIMIT)


def _whole_vmem():
    return pl.BlockSpec(memory_space=pltpu.VMEM)


def _nt(a, b):
    return lax.dot_general(a, b, (((1,), (1,)), ((), ())), preferred_element_type=F32)


def _tn(a, b):
    return lax.dot_general(a, b, (((0,), (0,)), ((), ())), preferred_element_type=F32)


def _mm(a, b):
    return jnp.dot(a, b, preferred_element_type=F32)


def _sigmoid(x):
    return 1.0 / (1.0 + jnp.exp(-x))


_GELU_C = 0.7978845608028654


def _gelu(x):
    return 0.5 * x * (1.0 + jnp.tanh(_GELU_C * (x + 0.044715 * x * x * x)))


def _gelu_grad(x):
    th = jnp.tanh(_GELU_C * (x + 0.044715 * x * x * x))
    return 0.5 * (1.0 + th) + 0.5 * x * (1.0 - th * th) * _GELU_C * (1.0 + 3.0 * 0.044715 * x * x)


def _ssm_tables(a_re, a_im, log_dt, b_re, b_im, c_re, c_im, d):
    T, P, H = SSM_CHUNK, SSM_STATE, SSM_GROUP
    dt = jnp.exp(log_dt)[..., None]
    xr = dt * a_re
    xi = dt * a_im
    mag = jnp.exp(xr)
    lr = mag * jnp.cos(xi)
    li = mag * jnp.sin(xi)
    den = a_re * a_re + a_im * a_im
    cr = ((lr - 1.0) * a_re + li * a_im) / den
    ci = (li * a_re - (lr - 1.0) * a_im) / den
    bbr = cr[..., None] * b_re - ci[..., None] * b_im
    bbi = cr[..., None] * b_im + ci[..., None] * b_re
    kk = jnp.arange(T + 1, dtype=F32)[:, None, None, None]
    pm = jnp.exp(kk * xr)
    pw = jnp.stack([pm * jnp.cos(kk * xi), pm * jnp.sin(kk * xi)], axis=2)
    pw = pw.reshape(T + 1, 2, 2, N_PAIRS, 2 * P).transpose(3, 0, 1, 2, 4).reshape(N_PAIRS, T + 1, 8 * P)
    pw = jnp.concatenate([pw, jnp.zeros((N_PAIRS, 24 - (T + 1), 8 * P), F32)], axis=1)
    eye2 = jnp.eye(2, dtype=F32)

    def expand(t):
        t = t.transpose(2, 0, 1, 3, 4, 5)
        t = t[:, :, :, :, :, None, :] * eye2[None, None, None, :, None, :, None]
        return t.reshape(N_PAIRS, 4, 2 * H, 2 * P)

    bb = expand(jnp.stack([bbr, bbi], axis=1).reshape(2, 2, N_PAIRS, 2, P, H).transpose(0, 1, 2, 3, 5, 4))
    cc = expand(jnp.stack([c_re, c_im], axis=1).reshape(2, 2, N_PAIRS, 2, H, P))
    dd = d.reshape(N_PAIRS, 2 * H)[:, :, None] * jnp.eye(2 * H, dtype=F32)[None]
    dd = jnp.concatenate([dd, jnp.zeros((N_PAIRS, 2 * H, LANES - 2 * H), F32)], axis=2)
    return pw, bb, cc, dd


NA_TAB = 2 * NA_ROWS


def _na_table(rpb):
    qc = jnp.arange(GRID_W)[:, None, None]
    kc = (jnp.arange(2 * GRID_W) % GRID_W)[None, :, None]
    dc = jnp.arange(2 * NA_COLS - 1)[None, None, :]
    cstart = jnp.clip(qc - NA_COLS // 2, 0, GRID_W - NA_COLS)
    csel = ((kc >= cstart) & (kc < cstart + NA_COLS) & (kc - qc + NA_COLS - 1 == dc)).astype(F32)
    col_ok = jnp.sum(csel, axis=-1) > 0.5
    part = jnp.einsum('hrc,qmc->hrqm', rpb, csel, precision=HIGHEST)
    zero = jnp.zeros_like(part[:, :1])
    odd = jnp.arange(2 * GRID_W) >= GRID_W
    tab = jnp.where(odd, jnp.concatenate([part, zero], axis=1), jnp.concatenate([zero, part], axis=1))
    return jnp.where(col_ok, tab, NEG)


def _fwd_in(x2, g_pre, w_in_g):
    seq = x2.shape[0]

    def body(x_ref, g_ref, w_ref, u_ref, zs_ref, qkv_ref, zn_ref, hn_ref):
        x = x_ref[...]
        r = lax.rsqrt(jnp.mean(x * x, axis=-1, keepdims=True) + EPS)
        hn = (x * r * g_ref[...]).astype(BF16)
        hn_ref[...] = hn
        for j in range(N_DEV):
            pj = _mm(hn, w_ref[j])
            for i in range(SHARD_IN // LANES):
                blk = (SHARD_IN // LANES) * j + i
                piece = pj[:, i * LANES:(i + 1) * LANES]
                if blk < 4:
                    u_ref[:, blk * LANES:(blk + 1) * LANES] = piece
                elif blk < 8:
                    zs_ref[:, (blk - 4) * LANES:(blk - 3) * LANES] = piece
                elif blk < 20:
                    qkv_ref[:, (blk - 8) * LANES:(blk - 7) * LANES] = piece.astype(BF16)
                else:
                    zn_ref[:, (blk - 20) * LANES:(blk - 19) * LANES] = piece

    tok = lambda w: pl.BlockSpec((TOK_TILE, w), lambda i: (i, 0))
    return pl.pallas_call(
        body, name="fwd_in", grid=(seq // TOK_TILE,),
        in_specs=[tok(D_MODEL), pl.BlockSpec((1, D_MODEL), lambda i: (0, 0)), _whole_vmem()],
        out_specs=[tok(D_SSM), tok(D_SSM), tok(3 * D_NA), tok(D_NA), tok(D_MODEL)],
        out_shape=[jax.ShapeDtypeStruct((seq, D_SSM), F32), jax.ShapeDtypeStruct((seq, D_SSM), F32),
                   jax.ShapeDtypeStruct((seq, 3 * D_NA), BF16), jax.ShapeDtypeStruct((seq, D_NA), F32),
                   jax.ShapeDtypeStruct((seq, D_MODEL), BF16)],
        compiler_params=_params("arbitrary"),
    )(x2, g_pre, w_in_g)


def _mid_fwd(yssm, zs, o, zn, x2, p2, tgt, w_glu, b_glu, w_out, g_post, w_ple_g, g_ple, w_pg):
    seq = x2.shape[0]

    def body(yssm_ref, zs_ref, o_ref, zn_ref, x_ref, p_ref, tgt_ref, wglu_ref, bglu_ref, wout_ref, gpost_ref,
             wple_ref, gple_ref, wpg_ref,
             dcat_ref, dh1_ref, loss_ref, dwout_ref, dwple_ref, dwpg_ref, dgpost_ref, dgple_ref):
        @pl.when(pl.program_id(0) == 0)
        def _():
            loss_ref[...] = jnp.zeros_like(loss_ref)
            dwout_ref[...] = jnp.zeros_like(dwout_ref)
            dwple_ref[...] = jnp.zeros_like(dwple_ref)
            dwpg_ref[...] = jnp.zeros_like(dwpg_ref)
            dgpost_ref[...] = jnp.zeros_like(dgpost_ref)
            dgple_ref[...] = jnp.zeros_like(dgple_ref)

        g1 = _gelu(yssm_ref[...])
        t = _mm(g1.astype(BF16), wglu_ref[...]) + bglu_ref[...]
        zs_v = zs_ref[...]
        ys = g1 * _sigmoid(t) * (zs_v * _sigmoid(zs_v))
        zn_v = zn_ref[...]
        yn = o_ref[...] * (zn_v * _sigmoid(zn_v))
        cat = jnp.concatenate([ys, yn], axis=1).astype(BF16)
        mix = _mm(cat, wout_ref[...])
        r2 = lax.rsqrt(jnp.mean(mix * mix, axis=-1, keepdims=True) + EPS)
        n2 = mix * r2
        gpost = gpost_ref[...]
        h1 = x_ref[...] + n2 * gpost
        pb = p_ref[...].astype(BF16)
        epre = jnp.concatenate([_mm(pb, wple_ref[j]) for j in range(N_DEV)], axis=1)
        r3 = lax.rsqrt(jnp.mean(epre * epre, axis=-1, keepdims=True) + EPS)
        n3 = epre * r3
        gple = gple_ref[...]
        e = n3 * gple
        h1b = h1.astype(BF16)
        gate = _sigmoid(_mm(h1b, wpg_ref[...]))
        diff = h1 + gate * e - tgt_ref[...]
        loss_ref[...] += jnp.sum(diff * diff, axis=0, keepdims=True)

        dy = diff * (1.0 / D_MODEL)
        dgp = (dy * e * gate * (1.0 - gate)).astype(BF16)
        de = dy * gate
        dh1 = dy + _nt(dgp, wpg_ref[...])
        dh1_ref[...] = dh1
        dwpg_ref[...] += _tn(h1b, dgp)
        dgple_ref[...] += jnp.sum(de * n3, axis=0, keepdims=True)
        dn3 = de * gple
        depre = (r3 * (dn3 - n3 * jnp.mean(dn3 * n3, axis=-1, keepdims=True))).astype(BF16)
        for j in range(N_DEV):
            dwple_ref[j] += _tn(pb, depre[:, j * LANES:(j + 1) * LANES])
        dgpost_ref[...] += jnp.sum(dh1 * n2, axis=0, keepdims=True)
        dn2 = dh1 * gpost
        dmix = (r2 * (dn2 - n2 * jnp.mean(dn2 * n2, axis=-1, keepdims=True))).astype(BF16)
        dcat_ref[...] = _nt(dmix, wout_ref[...])
        dwout_ref[...] += _tn(cat, dmix)

    tok = lambda w: pl.BlockSpec((TOK_TILE, w), lambda i: (i, 0))
    row = lambda w: pl.BlockSpec((1, w), lambda i: (0, 0))
    vm = _whole_vmem()
    return pl.pallas_call(
        body, name="mid_fwd", grid=(seq // TOK_TILE,),
        in_specs=[tok(D_SSM), tok(D_SSM), tok(D_NA), tok(D_NA), tok(D_MODEL), tok(D_PLE), tok(D_MODEL),
                  vm, row(D_SSM), vm, row(D_MODEL), vm, row(D_MODEL), vm],
        out_specs=[tok(D_MODEL), tok(D_MODEL), vm, vm, vm, vm, vm, vm],
        out_shape=[jax.ShapeDtypeStruct((seq, D_MODEL), F32), jax.ShapeDtypeStruct((seq, D_MODEL), F32),
                   jax.ShapeDtypeStruct((1, D_MODEL), F32),
                   jax.ShapeDtypeStruct((D_MODEL, D_MODEL), F32),
                   jax.ShapeDtypeStruct((N_DEV, D_PLE, LANES), F32),
                   jax.ShapeDtypeStruct((D_MODEL, D_MODEL), F32),
                   jax.ShapeDtypeStruct((1, D_MODEL), F32), jax.ShapeDtypeStruct((1, D_MODEL), F32)],
        compiler_params=_params("arbitrary"),
    )(yssm, zs, o, zn, x2, p2, tgt, w_glu, b_glu, w_out, g_post, w_ple_g, g_ple, w_pg)


def _mid_bwd(yssm, zs, o, zn, dcat, w_glu, b_glu):
    seq = yssm.shape[0]

    def body(yssm_ref, zs_ref, o_ref, zn_ref, dcat_ref, wglu_ref, bglu_ref,
             dyssm_ref, dzs_ref, do_ref, dzn_ref, dwglu_ref, dbglu_ref):
        @pl.when(pl.program_id(0) == 0)
        def _():
            dwglu_ref[...] = jnp.zeros_like(dwglu_ref)
            dbglu_ref[...] = jnp.zeros_like(dbglu_ref)

        dys = dcat_ref[:, :D_SSM]
        dyn = dcat_ref[:, D_SSM:]
        yv = yssm_ref[...]
        g1 = _gelu(yv)
        g1b = g1.astype(BF16)
        sg = _sigmoid(_mm(g1b, wglu_ref[...]) + bglu_ref[...])
        zs_v = zs_ref[...]
        s_zs = _sigmoid(zs_v)
        dg2 = dys * (zs_v * s_zs)
        dzs_ref[...] = dys * (g1 * sg) * (s_zs * (1.0 + zs_v * (1.0 - s_zs)))
        dt = dg2 * g1 * sg * (1.0 - sg)
        dtb = dt.astype(BF16)
        dg1 = dg2 * sg + _nt(dtb, wglu_ref[...])
        dwglu_ref[...] += _tn(g1b, dtb)
        dbglu_ref[...] += jnp.sum(dt, axis=0, keepdims=True)
        dyssm_ref[...] = dg1 * _gelu_grad(yv)
        zn_v = zn_ref[...]
        s_zn = _sigmoid(zn_v)
        do_ref[...] = dyn * (zn_v * s_zn)
        dzn_ref[...] = dyn * o_ref[...] * (s_zn * (1.0 + zn_v * (1.0 - s_zn)))

    tok = lambda w: pl.BlockSpec((TOK_TILE, w), lambda i: (i, 0))
    vm = _whole_vmem()
    half = jax.ShapeDtypeStruct((seq, D_SSM), F32)
    return pl.pallas_call(
        body, name="mid_bwd", grid=(seq // TOK_TILE,),
        in_specs=[tok(D_SSM), tok(D_SSM), tok(D_NA), tok(D_NA), tok(D_MODEL), vm,
                  pl.BlockSpec((1, D_SSM), lambda i: (0, 0))],
        out_specs=[tok(D_SSM), tok(D_SSM), tok(D_NA), tok(D_NA), vm, vm],
        out_shape=[half, half, half, half, jax.ShapeDtypeStruct((D_SSM, D_SSM), F32),
                   jax.ShapeDtypeStruct((1, D_SSM), F32)],
        compiler_params=_params("arbitrary"),
    )(yssm, zs, o, zn, dcat, w_glu, b_glu)


def _bwd_in(du, dzs, dq, dk, dv, dzn, hn, x2, dh1, g_pre, w_in_g):
    seq = x2.shape[0]

    def body(du_ref, dzs_ref, dq_ref, dk_ref, dv_ref, dzn_ref, hn_ref, x_ref, dh1_ref, g_ref, w_ref,
             dx_ref, dw_ref, dg_ref, dproj_ref):
        @pl.when(pl.program_id(0) == 0)
        def _():
            dw_ref[...] = jnp.zeros_like(dw_ref)
            dg_ref[...] = jnp.zeros_like(dg_ref)

        for k, ref in enumerate((du_ref, dzs_ref, dq_ref, dk_ref, dv_ref, dzn_ref)):
            dproj_ref[:, k * D_SSM:(k + 1) * D_SSM] = ref[...].astype(BF16)
        hn = hn_ref[...]
        dhn = jnp.zeros((TOK_TILE, D_MODEL), F32)
        for j in range(N_DEV):
            dpj = dproj_ref[:, j * SHARD_IN:(j + 1) * SHARD_IN]
            dhn += _nt(dpj, w_ref[j])
            dw_ref[j] += _tn(hn, dpj)
        x = x_ref[...]
        r = lax.rsqrt(jnp.mean(x * x, axis=-1, keepdims=True) + EPS)
        n1 = x * r
        dg_ref[...] += jnp.sum(dhn * n1, axis=0, keepdims=True)
        dn1 = dhn * g_ref[...]
        dx_ref[...] = dh1_ref[...] + r * (dn1 - n1 * jnp.mean(dn1 * n1, axis=-1, keepdims=True))

    tok = lambda w: pl.BlockSpec((TOK_TILE, w), lambda i: (i, 0))
    vm = _whole_vmem()
    return pl.pallas_call(
        body, name="bwd_in", grid=(seq // TOK_TILE,),
        in_specs=[tok(D_SSM)] * 6 + [tok(D_MODEL), tok(D_MODEL), tok(D_MODEL),
                                      pl.BlockSpec((1, D_MODEL), lambda i: (0, 0)), vm],
        out_specs=[tok(D_MODEL), vm, vm],
        out_shape=[jax.ShapeDtypeStruct((seq, D_MODEL), F32),
                   jax.ShapeDtypeStruct((N_DEV, D_MODEL, SHARD_IN), F32),
                   jax.ShapeDtypeStruct((1, D_MODEL), F32)],
        scratch_shapes=[pltpu.VMEM((TOK_TILE, D_IN_PROJ), BF16)],
        compiler_params=_params("arbitrary"),
    )(du, dzs, dq, dk, dv, dzn, hn, x2, dh1, g_pre, w_in_g)


RELAYOUT_CHUNKS = 128
PAIR_W = 2 * SSM_GROUP
PAIRS_PER_BLOCK = LANES // PAIR_W
CHUNK_W = SSM_CHUNK * PAIR_W
PW_ROWS = 24


def _lane_window(lo, width):
    lane = lax.broadcasted_iota(jnp.int32, (1, LANES), 1)
    return (lane >= lo) & (lane < lo + width)


def _to_pairs(a, nc):
    ncb = min(RELAYOUT_CHUNKS, nc)

    def body(x_ref, out_ref):
        xs = [x_ref[pl.ds(s, ncb, stride=SSM_CHUNK), :] for s in range(SSM_CHUNK)]
        for a_ in range(PAIRS_PER_BLOCK):
            for v in range(SSM_CHUNK // PAIRS_PER_BLOCK):
                acc = None
                for i in range(PAIRS_PER_BLOCK):
                    shift = (PAIR_W * (i - a_)) % LANES
                    piece = xs[PAIRS_PER_BLOCK * v + i]
                    piece = pltpu.roll(piece, shift, axis=1) if shift else piece
                    acc = piece if acc is None else jnp.where(_lane_window(PAIR_W * i, PAIR_W), piece, acc)
                out_ref[a_, :, LANES * v:LANES * (v + 1)] = acc.astype(BF16)

    return pl.pallas_call(
        body, name="to_pairs", grid=(D_SSM // LANES, nc // ncb),
        in_specs=[pl.BlockSpec((ncb * SSM_CHUNK, LANES), lambda cl, cb: (cb, cl))],
        out_specs=pl.BlockSpec((PAIRS_PER_BLOCK, ncb, CHUNK_W), lambda cl, cb: (cl, cb, 0)),
        out_shape=jax.ShapeDtypeStruct((N_PAIRS, nc, CHUNK_W), BF16),
        compiler_params=_params("arbitrary", "arbitrary"),
    )(a)


def _from_pairs(a, nc):
    ncb = min(RELAYOUT_CHUNKS, nc)

    def body(y_ref, out_ref):
        for s in range(SSM_CHUNK):
            v, i = divmod(s, PAIRS_PER_BLOCK)
            acc = None
            for a_ in range(PAIRS_PER_BLOCK):
                shift = (PAIR_W * (a_ - i)) % LANES
                piece = y_ref[a_, :, LANES * v:LANES * (v + 1)]
                piece = pltpu.roll(piece, shift, axis=1) if shift else piece
                acc = piece if acc is None else jnp.where(_lane_window(PAIR_W * a_, PAIR_W), piece, acc)
            out_ref[pl.ds(s, ncb, stride=SSM_CHUNK), :] = acc

    return pl.pallas_call(
        body, name="from_pairs", grid=(D_SSM // LANES, nc // ncb),
        in_specs=[pl.BlockSpec((PAIRS_PER_BLOCK, ncb, CHUNK_W), lambda cl, cb: (cl, cb, 0))],
        out_specs=pl.BlockSpec((ncb * SSM_CHUNK, LANES), lambda cl, cb: (cb, cl)),
        out_shape=jax.ShapeDtypeStruct((nc * SSM_CHUNK, D_SSM), F32),
        compiler_params=_params("arbitrary", "arbitrary"),
    )(a)


def _boundary_scan(nc, pw_ref, src_ref, dst_ref, conj):
    nblk = nc // 8
    lr0, li0, lr1, li1 = (pw_ref[0, SSM_CHUNK:SSM_CHUNK + 1, LANES * q:LANES * (q + 1)] for q in range(4))
    if conj:
        li0, li1 = -li0, -li1

    def step(i, carry):
        hr0, hi0, hr1, hi1 = carry
        up = pl.multiple_of(i * 8, 8)
        dn = pl.multiple_of((nblk - 1 - i) * 8, 8)
        a_rows, b_rows = (dn, up) if conj else (up, dn)
        s_r0 = src_ref[pl.ds(a_rows, 8), 0:128]
        s_i0 = src_ref[pl.ds(a_rows, 8), 128:256]
        s_r1 = src_ref[pl.ds(b_rows, 8), 256:384]
        s_i1 = src_ref[pl.ds(b_rows, 8), 384:512]
        o_r0, o_i0, o_r1, o_i1 = [], [], [], []
        for k in range(8):
            ka = 7 - k if conj else k
            kb = k if conj else 7 - k
            o_r0.append((ka, hr0))
            o_i0.append((ka, hi0))
            o_r1.append((kb, hr1))
            o_i1.append((kb, hi1))
            hr0, hi0 = (lr0 * hr0 - li0 * hi0 + s_r0[ka:ka + 1], lr0 * hi0 + li0 * hr0 + s_i0[ka:ka + 1])
            hr1, hi1 = (lr1 * hr1 - li1 * hi1 + s_r1[kb:kb + 1], lr1 * hi1 + li1 * hr1 + s_i1[kb:kb + 1])

        def rows(items):
            return jnp.concatenate([v for _, v in sorted(items, key=lambda kv: kv[0])], axis=0)

        dst_ref[pl.ds(a_rows, 8), 0:128] = rows(o_r0)
        dst_ref[pl.ds(a_rows, 8), 128:256] = rows(o_i0)
        dst_ref[pl.ds(b_rows, 8), 256:384] = rows(o_r1)
        dst_ref[pl.ds(b_rows, 8), 384:512] = rows(o_i1)
        return hr0, hi0, hr1, hi1

    z = jnp.zeros((1, LANES), F32)
    lax.fori_loop(0, nblk, step, (z, z, z, z))


def _pw_row(pw_ref, k, q):
    return pw_ref[0, k:k + 1, LANES * q:LANES * (q + 1)]


def _mm_f32(a, b, dims):
    return lax.dot_general(a, b, (dims, ((), ())), precision=HIGHEST, preferred_element_type=F32)


_POW_M = (lambda s: SSM_CHUNK - 1 - s, lambda s: s)
_POW_C = (lambda s: s + 1, lambda s: SSM_CHUNK - s)
_POW_K = (lambda s: s, lambda s: SSM_CHUNK - 1 - s)


def _chunk_matrices(pw_ref, bb_ref, cc_ref, dd_ref, m_scr, ct_scr, toep_scr, g_scr, kt_scr):
    blk = lambda s: slice(PAIR_W * s, PAIR_W * (s + 1))
    col = lambda q: slice(LANES * q, LANES * (q + 1))
    for d in range(2):
        bbr, bbi = bb_ref[0, 2 * d], bb_ref[0, 2 * d + 1]
        ccr, cci = cc_ref[0, 2 * d], cc_ref[0, 2 * d + 1]
        for s in range(SSM_CHUNK):
            pr, pi = _pw_row(pw_ref, _POW_M[d](s), 2 * d), _pw_row(pw_ref, _POW_M[d](s), 2 * d + 1)
            m_scr[blk(s), col(2 * d)] = (pr * bbr - pi * bbi).astype(m_scr.dtype)
            m_scr[blk(s), col(2 * d + 1)] = (pr * bbi + pi * bbr).astype(m_scr.dtype)
            pr, pi = _pw_row(pw_ref, _POW_C[d](s), 2 * d), _pw_row(pw_ref, _POW_C[d](s), 2 * d + 1)
            ct_scr[blk(s), col(2 * d)] = (ccr * pr - cci * pi).astype(ct_scr.dtype)
            ct_scr[blk(s), col(2 * d + 1)] = (-(ccr * pi + cci * pr)).astype(ct_scr.dtype)
            pr, pi = _pw_row(pw_ref, _POW_K[d](s), 2 * d), _pw_row(pw_ref, _POW_K[d](s), 2 * d + 1)
            g_scr[d, blk(s), 0:LANES] = ccr * pr - cci * pi
            g_scr[d, blk(s), LANES:2 * LANES] = -(ccr * pi + cci * pr)
        kt = _mm_f32(jnp.concatenate([bbr, bbi], axis=1), g_scr[d], ((1,), (1,)))
        if d == 0:
            kt = jnp.concatenate([kt[:, 0:LANES] + dd_ref[0], kt[:, LANES:]], axis=1)
        kt_scr[d] = kt
    lane = lax.broadcasted_iota(jnp.int32, (1, CHUNK_W), 1)
    for s in range(SSM_CHUNK):
        lo = PAIR_W * s
        hi = PAIR_W * (s + 1)
        fwd = kt_scr[0] if s == 0 else pltpu.roll(kt_scr[0], lo, axis=1)
        bwd = kt_scr[1] if hi == CHUNK_W else pltpu.roll(kt_scr[1], hi, axis=1)
        row = jnp.where(lane >= lo, fwd, 0.0) + jnp.where(lane < hi, bwd, 0.0)
        toep_scr[blk(s), :] = row.astype(toep_scr.dtype)


def _ssm_scratch(nc, mat_dtype):
    return [pltpu.VMEM((CHUNK_W, CHUNK_W), mat_dtype), pltpu.VMEM((CHUNK_W, CHUNK_W), mat_dtype),
            pltpu.VMEM((CHUNK_W, CHUNK_W), mat_dtype), pltpu.VMEM((2, CHUNK_W, 2 * LANES), F32),
            pltpu.VMEM((2, PAIR_W, CHUNK_W), F32), pltpu.VMEM((nc, CHUNK_W), F32), pltpu.VMEM((nc, CHUNK_W), F32)]


def _per_pair(*shape):
    return pl.BlockSpec((1,) + shape, lambda g: (g,) + (0,) * len(shape))


_TABLE_SPECS = lambda: [_per_pair(PW_ROWS, CHUNK_W), _per_pair(4, PAIR_W, LANES), _per_pair(4, PAIR_W, LANES),
                        _per_pair(PAIR_W, LANES)]


def _ssm_fwd(u, pw, bb, cc, dd):
    npair, nc, width = u.shape

    def body(u_ref, pw_ref, bb_ref, cc_ref, dd_ref, y_ref, hin_ref, m_scr, ct_scr, toep_scr, g_scr, kt_scr, s_scr, h_scr):
        _chunk_matrices(pw_ref, bb_ref, cc_ref, dd_ref, m_scr, ct_scr, toep_scr, g_scr, kt_scr)
        uv = u_ref[0]
        s_scr[...] = _mm(uv, m_scr[...])
        _boundary_scan(nc, pw_ref, s_scr, h_scr, conj=False)
        hin = h_scr[...]
        hin_ref[0] = hin
        y_ref[0] = _mm(uv, toep_scr[...]) + _nt(hin.astype(u.dtype), ct_scr[...])

    return pl.pallas_call(
        body, name="ssm_fwd", grid=(npair,),
        in_specs=[_per_pair(nc, width)] + _TABLE_SPECS(),
        out_specs=[_per_pair(nc, width), _per_pair(nc, width)],
        out_shape=[jax.ShapeDtypeStruct((npair, nc, width), F32), jax.ShapeDtypeStruct((npair, nc, width), F32)],
        scratch_shapes=_ssm_scratch(nc, u.dtype),
        compiler_params=_params("arbitrary"),
    )(u, pw, bb, cc, dd)


def _ssm_bwd(u, dy, hin, pw, bb, cc, dd):
    npair, nc, width = u.shape

    def body(u_ref, dy_ref, hin_ref, pw_ref, bb_ref, cc_ref, dd_ref, du_ref, dpw_ref, dbb_ref, dcc_ref, ddd_ref,
             m_scr, ct_scr, toep_scr, g_scr, kt_scr, dh_scr, ds_scr):
        _chunk_matrices(pw_ref, bb_ref, cc_ref, dd_ref, m_scr, ct_scr, toep_scr, g_scr, kt_scr)
        uv = u_ref[0]
        dyb = dy_ref[0]
        hin = hin_ref[0]
        dh_scr[...] = _mm(dyb, ct_scr[...])
        _boundary_scan(nc, pw_ref, dh_scr, ds_scr, conj=True)
        ds = ds_scr[...]
        dsb = ds.astype(u.dtype)
        du_ref[0] = _nt(dsb, m_scr[...]) + _nt(dyb, toep_scr[...])
        dm = _tn(uv, dsb)
        dct = _tn(dyb, hin.astype(u.dtype))
        dtoep = _tn(uv, dyb)

        dpw_ref[...] = jnp.zeros_like(dpw_ref)
        blk = lambda s: slice(PAIR_W * s, PAIR_W * (s + 1))
        col = lambda q: slice(LANES * q, LANES * (q + 1))

        def add_pw(k, q, val):
            dpw_ref[0, k:k + 1, col(q)] += jnp.sum(val, axis=0, keepdims=True)

        for d in range(2):
            g_r, g_i = ds[:, col(2 * d)], ds[:, col(2 * d + 1)]
            h_r, h_i = hin[:, col(2 * d)], hin[:, col(2 * d + 1)]
            add_pw(SSM_CHUNK, 2 * d, g_r * h_r + g_i * h_i)
            add_pw(SSM_CHUNK, 2 * d + 1, g_i * h_r - g_r * h_i)

        lane = lax.broadcasted_iota(jnp.int32, (1, CHUNK_W), 1)
        dkt0 = jnp.zeros((PAIR_W, CHUNK_W), F32)
        dkt1 = jnp.zeros((PAIR_W, CHUNK_W), F32)
        for s in range(SSM_CHUNK):
            lo = PAIR_W * s
            hi = PAIR_W * (s + 1)
            row = dtoep[blk(s), :]
            fwd = jnp.where(lane >= lo, row, 0.0)
            bwd = jnp.where(lane < hi, row, 0.0)
            dkt0 += fwd if s == 0 else pltpu.roll(fwd, CHUNK_W - lo, axis=1)
            dkt1 += bwd if hi == CHUNK_W else pltpu.roll(bwd, CHUNK_W - hi, axis=1)
        ddd_ref[0] = dkt0[:, 0:LANES]

        for d, dkt in enumerate((dkt0, dkt1)):
            bbr, bbi = bb_ref[0, 2 * d], bb_ref[0, 2 * d + 1]
            ccr, cci = cc_ref[0, 2 * d], cc_ref[0, 2 * d + 1]
            dbbcat = _mm_f32(dkt, g_scr[d], ((1,), (0,)))
            dg = _mm_f32(dkt, jnp.concatenate([bbr, bbi], axis=1), ((0,), (0,)))
            dbbr, dbbi = dbbcat[:, 0:LANES], dbbcat[:, LANES:]
            dccr = jnp.zeros((PAIR_W, LANES), F32)
            dcci = jnp.zeros((PAIR_W, LANES), F32)
            for s in range(SSM_CHUNK):
                k = _POW_M[d](s)
                pr, pi = _pw_row(pw_ref, k, 2 * d), _pw_row(pw_ref, k, 2 * d + 1)
                gr, gi = dm[blk(s), col(2 * d)], dm[blk(s), col(2 * d + 1)]
                dbbr += gr * pr + gi * pi
                dbbi += gi * pr - gr * pi
                add_pw(k, 2 * d, gr * bbr + gi * bbi)
                add_pw(k, 2 * d + 1, gi * bbr - gr * bbi)
                for k, gr, gi in ((_POW_C[d](s), dct[blk(s), col(2 * d)], dct[blk(s), col(2 * d + 1)]),
                                  (_POW_K[d](s), dg[blk(s), 0:LANES], dg[blk(s), LANES:])):
                    pr, pi = _pw_row(pw_ref, k, 2 * d), _pw_row(pw_ref, k, 2 * d + 1)
                    dccr += gr * pr - gi * pi
                    dcci += -(gr * pi + gi * pr)
                    add_pw(k, 2 * d, gr * ccr - gi * cci)
                    add_pw(k, 2 * d + 1, -(gr * cci + gi * ccr))
            dbb_ref[0, 2 * d] = dbbr
            dbb_ref[0, 2 * d + 1] = dbbi
            dcc_ref[0, 2 * d] = dccr
            dcc_ref[0, 2 * d + 1] = dcci

    return pl.pallas_call(
        body, name="ssm_bwd", grid=(npair,),
        in_specs=[_per_pair(nc, width), _per_pair(nc, width), _per_pair(nc, width)] + _TABLE_SPECS(),
        out_specs=[_per_pair(nc, width)] + _TABLE_SPECS(),
        out_shape=[jax.ShapeDtypeStruct((npair, nc, width), F32), jax.ShapeDtypeStruct(pw.shape, F32),
                   jax.ShapeDtypeStruct(bb.shape, F32), jax.ShapeDtypeStruct(cc.shape, F32),
                   jax.ShapeDtypeStruct(dd.shape, F32)],
        scratch_shapes=_ssm_scratch(nc, u.dtype),
        compiler_params=_params("arbitrary"),
    )(u, dy, hin, pw, bb, cc, dd)


NA_Q = NA_QROWS * GRID_W
NA_K = NA_KROWS * GRID_W
NA_SCALE = NA_HEAD_DIM ** -0.5


def _na_block(b, nb, rows):
    start = jnp.clip(NA_QROWS * b - NA_ROWS // 2, 0, rows - NA_KROWS) * GRID_W
    kind = jnp.where(b == 0, 0, jnp.where(b == nb - 1, 2, 1))
    return pl.multiple_of(start, GRID_W), kind


NA_CHUNK = 16


def _na_bias_pieces(kind, i):
    ri, q0 = divmod(i * NA_CHUNK, GRID_W)
    off = jnp.where(kind == 0, NA_ROWS - 1, jnp.where(kind == 1, NA_ROWS // 2 - 1, -1))
    lo = jnp.where(kind == 0, 0, jnp.where(kind == 1, ri, NA_KROWS - NA_ROWS))
    lane = lax.broadcasted_iota(jnp.int32, (1, LANES), 1)
    out = []
    for k2 in range(NA_KROWS // 2):
        t = jnp.clip(2 * k2 - ri + off + 1, 0, NA_TAB - 1)
        kr = jnp.where(lane < GRID_W, 2 * k2, 2 * k2 + 1)
        out.append((t, slice(q0, q0 + NA_CHUNK), (kr >= lo) & (kr < lo + NA_ROWS)))
    return out


def _na_softmax_rows(s_ref, tab_ref, hh, kind, i):
    rows = slice(i * NA_CHUNK, (i + 1) * NA_CHUNK)
    bias = [jnp.where(ok, tab_ref[hh, t, q, :], NEG) for t, q, ok in _na_bias_pieces(kind, i)]
    s = s_ref[hh, rows, :] + jnp.concatenate(bias, axis=1)
    e = jnp.exp(s - jnp.max(s, axis=-1, keepdims=True))
    return e * (1.0 / jnp.sum(e, axis=-1, keepdims=True))


def _na_heads():
    lane = lax.broadcasted_iota(jnp.int32, (1, LANES), 1)
    return [lane < NA_HEAD_DIM, lane >= NA_HEAD_DIM]


def _na_fwd(qkv, bias):
    seq = qkv.shape[0]
    rows = seq // GRID_W
    nb = rows // NA_QROWS

    def body(q_ref, k_ref, v_ref, bias_ref, o_ref, s_scr, p_scr):
        start, kind = _na_block(pl.program_id(1), nb, rows)
        q2 = q_ref[...] * NA_SCALE
        kw = k_ref[pl.ds(start, NA_K), :]
        vw = v_ref[pl.ds(start, NA_K), :]
        heads = _na_heads()
        for hh in range(2):
            s_scr[hh] = _nt(jnp.where(heads[hh], q2, jnp.zeros_like(q2)), kw)
        for hh in range(2):
            for i in range(NA_Q // NA_CHUNK):
                r = slice(i * NA_CHUNK, (i + 1) * NA_CHUNK)
                p_scr[hh, r, :] = _na_softmax_rows(s_scr, bias_ref, hh, kind, i).astype(p_scr.dtype)
        o_ref[...] = jnp.where(heads[0], _mm(p_scr[0], vw), _mm(p_scr[1], vw))

    return pl.pallas_call(
        body, name="na_fwd", grid=(NA_HEADS // 2, nb),
        in_specs=[pl.BlockSpec((NA_Q, LANES), lambda hp, b: (b, hp)),
                  pl.BlockSpec((seq, LANES), lambda hp, b: (0, 4 + hp)),
                  pl.BlockSpec((seq, LANES), lambda hp, b: (0, 8 + hp)),
                  pl.BlockSpec((2, NA_TAB, GRID_W, LANES), lambda hp, b: (hp, 0, 0, 0))],
        out_specs=pl.BlockSpec((NA_Q, LANES), lambda hp, b: (b, hp)),
        out_shape=jax.ShapeDtypeStruct((seq, D_NA), F32),
        scratch_shapes=[pltpu.VMEM((2, NA_Q, NA_K), F32), pltpu.VMEM((2, NA_Q, NA_K), qkv.dtype)],
        compiler_params=_params("arbitrary", "arbitrary"),
    )(qkv, qkv, qkv, bias)


def _na_bwd(qkv, do, bias):
    seq = qkv.shape[0]
    rows = seq // GRID_W
    nb = rows // NA_QROWS

    def body(q_ref, k_ref, v_ref, do_ref, bias_ref, dq_ref, dk_ref, dv_ref, dbias_ref, s_scr, dp_scr, p_scr, ds_scr):
        b = pl.program_id(1)
        start, kind = _na_block(b, nb, rows)

        @pl.when(b == 0)
        def _():
            dk_ref[...] = jnp.zeros_like(dk_ref)
            dv_ref[...] = jnp.zeros_like(dv_ref)
            dbias_ref[...] = jnp.zeros_like(dbias_ref)

        q2 = q_ref[...] * NA_SCALE
        kw = k_ref[pl.ds(start, NA_K), :]
        vw = v_ref[pl.ds(start, NA_K), :]
        do2 = do_ref[...].astype(q2.dtype)
        heads = _na_heads()
        for hh in range(2):
            s_scr[hh] = _nt(jnp.where(heads[hh], q2, jnp.zeros_like(q2)), kw)
            dp_scr[hh] = _nt(jnp.where(heads[hh], do2, jnp.zeros_like(do2)), vw)
        for hh in range(2):
            for i in range(NA_Q // NA_CHUNK):
                r = slice(i * NA_CHUNK, (i + 1) * NA_CHUNK)
                p = _na_softmax_rows(s_scr, bias_ref, hh, kind, i)
                dp = dp_scr[hh, r, :]
                ds = p * (dp - jnp.sum(p * dp, axis=-1, keepdims=True))
                for k2, (t, q, _) in enumerate(_na_bias_pieces(kind, i)):
                    dbias_ref[hh, t, q, :] += ds[:, k2 * LANES:(k2 + 1) * LANES]
                p_scr[hh, r, :] = p.astype(p_scr.dtype)
                ds_scr[hh, r, :] = ds.astype(ds_scr.dtype)
        dq_ref[...] = jnp.where(heads[0], _mm(ds_scr[0], kw), _mm(ds_scr[1], kw)) * NA_SCALE
        dk_ref[pl.ds(start, NA_K), :] += jnp.where(heads[0], _tn(ds_scr[0], q2), _tn(ds_scr[1], q2))
        dv_ref[pl.ds(start, NA_K), :] += jnp.where(heads[0], _tn(p_scr[0], do2), _tn(p_scr[1], do2))

    return pl.pallas_call(
        body, name="na_bwd", grid=(NA_HEADS // 2, nb),
        in_specs=[pl.BlockSpec((NA_Q, LANES), lambda hp, b: (b, hp)),
                  pl.BlockSpec((seq, LANES), lambda hp, b: (0, 4 + hp)),
                  pl.BlockSpec((seq, LANES), lambda hp, b: (0, 8 + hp)),
                  pl.BlockSpec((NA_Q, LANES), lambda hp, b: (b, hp)),
                  pl.BlockSpec((2, NA_TAB, GRID_W, LANES), lambda hp, b: (hp, 0, 0, 0))],
        out_specs=[pl.BlockSpec((NA_Q, LANES), lambda hp, b: (b, hp)),
                   pl.BlockSpec((seq, LANES), lambda hp, b: (0, hp)),
                   pl.BlockSpec((seq, LANES), lambda hp, b: (0, hp)),
                   pl.BlockSpec((2, NA_TAB, GRID_W, LANES), lambda hp, b: (hp, 0, 0, 0))],
        out_shape=[jax.ShapeDtypeStruct((seq, D_NA), F32), jax.ShapeDtypeStruct((seq, D_NA), F32),
                   jax.ShapeDtypeStruct((seq, D_NA), F32),
                   jax.ShapeDtypeStruct((NA_HEADS, NA_TAB, GRID_W, LANES), F32)],
        scratch_shapes=[pltpu.VMEM((2, NA_Q, NA_K), F32), pltpu.VMEM((2, NA_Q, NA_K), F32),
                        pltpu.VMEM((2, NA_Q, NA_K), qkv.dtype), pltpu.VMEM((2, NA_Q, NA_K), qkv.dtype)],
        compiler_params=_params("arbitrary", "arbitrary"),
    )(qkv, qkv, qkv, do, bias)


def _local_step(x2, p2, tgt, g_pre, g_post, w_in_g, ssm, w_glu, b_glu, rpb, w_out, w_ple_g, g_ple, w_pg):
    seq = x2.shape[0]
    nc = seq // SSM_CHUNK

    (pw, bb, cc, dd), ssm_vjp = jax.vjp(_ssm_tables, *ssm)
    bias, bias_vjp = jax.vjp(_na_table, rpb)

    u, zs, qkv, zn, hn = _fwd_in(x2, g_pre, w_in_g)
    u_p = _to_pairs(u, nc)
    y_p, hin = _ssm_fwd(u_p, pw, bb, cc, dd)
    yssm = _from_pairs(y_p, nc)
    o = _na_fwd(qkv, bias)
    dcat, dh1, sq, d_wout, d_wple, d_wpg, d_gpost, d_gple = _mid_fwd(
        yssm, zs, o, zn, x2, p2, tgt, w_glu, b_glu, w_out, g_post, w_ple_g, g_ple, w_pg)
    dyssm, dzs, do, dzn, d_wglu, d_bglu = _mid_bwd(yssm, zs, o, zn, dcat, w_glu, b_glu)
    dq, dk, dv, dbias = _na_bwd(qkv, do, bias)
    (d_rpb,) = bias_vjp(dbias)
    du_p, dpw, dbb, dcc, ddd = _ssm_bwd(u_p, _to_pairs(dyssm, nc), hin, pw, bb, cc, dd)
    d_ssm = ssm_vjp((dpw, dbb, dcc, ddd))
    du = _from_pairs(du_p, nc)
    dx, d_win, d_gpre = _bwd_in(du, dzs, dq, dk, dv, dzn, hn, x2, dh1, g_pre, w_in_g)
    return sq, dx, dict(norm_pre=d_gpre, norm_post=d_gpost, w_in=d_win, ssm=d_ssm, w_glu=d_wglu, b_glu=d_bglu,
                        na_rpb=d_rpb, w_out=d_wout, w_ple=d_wple, ple_norm=d_gple, w_ple_gate=d_wpg)


def _place():
    return lax.axis_index("x"), lax.axis_index("y"), lax.axis_index("c")


def _all_gather(shard):
    m_per, n = shard.shape

    def body(x_ref, out_ref, send_sems, recv_sems, local_sem):
        x, y, c = _place()
        me, sibling = (x, y, c), (x, y, 1 - c)
        chips = [(1 - x, y), (x, 1 - y), (1 - x, 1 - y)]

        def rows(px, py, pc):
            return out_ref.at[pl.ds((4 * px + 2 * py + pc) * m_per, m_per), :]

        def copy(k, block, to, src=None):
            return pltpu.make_async_remote_copy(
                src_ref=rows(*block) if src is None else src, dst_ref=rows(*block),
                send_sem=send_sems.at[k], recv_sem=recv_sems.at[k], device_id=to, device_id_type=MESH)

        mine = pltpu.make_async_copy(x_ref, rows(*me), local_sem)
        mine.start()
        first = [copy(0, me, sibling, src=x_ref)]
        first += [copy(1 + j, me, (*chip, c), src=x_ref) for j, chip in enumerate(chips)]
        for cp in first:
            cp.start()
        passed = [copy(4 + j, (*chip, c), sibling) for j, chip in enumerate(chips)]
        for j, chip in enumerate(chips):
            copy(1 + j, (*chip, c), me).wait_recv()
            passed[j].start()
        copy(0, sibling, me).wait_recv()
        for j, chip in enumerate(chips):
            copy(4 + j, (*chip, 1 - c), me).wait_recv()
        for cp in first + passed:
            cp.wait_send()
        mine.wait()

    return pl.pallas_call(
        body, name="all_gather",
        out_shape=jax.ShapeDtypeStruct((N_DEV * m_per, n), shard.dtype),
        in_specs=[_whole_vmem()], out_specs=_whole_vmem(),
        scratch_shapes=[pltpu.SemaphoreType.DMA((7,)), pltpu.SemaphoreType.DMA((7,)), pltpu.SemaphoreType.DMA],
        compiler_params=pltpu.CompilerParams(vmem_limit_bytes=VMEM_LIMIT),
    )(shard)


def _exchange_partials(part):
    _, r, n = part.shape

    def body(part_ref, land_ref, send_sems, recv_sems, local_sem):
        x, y, c = _place()
        me = 4 * x + 2 * y + c
        local = pltpu.make_async_copy(part_ref.at[me], land_ref.at[me], local_sem)
        local.start()
        copies = []
        for k in range(1, N_DEV):
            px, py, pc = x ^ ((k >> 2) & 1), y ^ ((k >> 1) & 1), c ^ (k & 1)
            peer = 4 * px + 2 * py + pc
            cp = pltpu.make_async_remote_copy(
                src_ref=part_ref.at[peer], dst_ref=land_ref.at[me],
                send_sem=send_sems.at[k - 1], recv_sem=recv_sems.at[k - 1],
                device_id=(px, py, pc), device_id_type=MESH)
            cp.start()
            copies.append(cp)
        for cp in copies:
            cp.wait_recv()
        for cp in copies:
            cp.wait_send()
        local.wait()

    anyspace = pl.BlockSpec(memory_space=pl.ANY)
    return pl.pallas_call(
        body, name="exchange_partials",
        out_shape=jax.ShapeDtypeStruct(part.shape, part.dtype),
        in_specs=[anyspace], out_specs=anyspace,
        scratch_shapes=[pltpu.SemaphoreType.DMA((N_DEV - 1,)), pltpu.SemaphoreType.DMA((N_DEV - 1,)),
                        pltpu.SemaphoreType.DMA],
    )(part)


def _pair_exchange(part):
    _, r, n = part.shape

    def body(part_ref, land_ref, send_sems, recv_sems):
        x, y, c = _place()
        copies = []
        for q in range(4):
            cp = pltpu.make_async_remote_copy(
                src_ref=part_ref.at[2 * q + (1 - c)], dst_ref=land_ref.at[q],
                send_sem=send_sems.at[q], recv_sem=recv_sems.at[q],
                device_id=(x, y, 1 - c), device_id_type=MESH)
            cp.start()
            copies.append(cp)
        for cp in copies:
            cp.wait_recv()
        for cp in copies:
            cp.wait_send()

    anyspace = pl.BlockSpec(memory_space=pl.ANY)
    return pl.pallas_call(
        body, name="pair_exchange",
        out_shape=jax.ShapeDtypeStruct((4, r, n), part.dtype),
        in_specs=[anyspace], out_specs=anyspace,
        scratch_shapes=[pltpu.SemaphoreType.DMA((4,)), pltpu.SemaphoreType.DMA((4,))],
    )(part)


def _pair_sum(part, land):
    _, r, n = part.shape
    tile = 256
    assert r % tile == 0

    def body(part_ref, land_ref, out_ref):
        c = lax.axis_index("c")
        for q in range(4):
            out_ref[q] = (part_ref[q, c] + land_ref[q]).astype(BF16)

    return pl.pallas_call(
        body, name="pair_sum", grid=(r // tile,),
        in_specs=[pl.BlockSpec((4, 2, tile, n), lambda i: (0, 0, i, 0)), pl.BlockSpec((4, tile, n), lambda i: (0, i, 0))],
        out_specs=pl.BlockSpec((4, tile, n), lambda i: (0, i, 0)),
        out_shape=jax.ShapeDtypeStruct((4, r, n), BF16),
        compiler_params=_params("arbitrary"),
    )(part.reshape(4, 2, r, n), land)


def _chip_exchange(chip_part):
    _, r, n = chip_part.shape

    def body(part_ref, land_ref, send_sems, recv_sems, local_sem):
        x, y, c = _place()
        mine = 2 * x + y
        local = pltpu.make_async_copy(part_ref.at[mine], land_ref.at[mine], local_sem)
        local.start()
        copies = []
        for k in range(1, 4):
            px, py = x ^ (k >> 1), y ^ (k & 1)
            cp = pltpu.make_async_remote_copy(
                src_ref=part_ref.at[2 * px + py], dst_ref=land_ref.at[mine],
                send_sem=send_sems.at[k - 1], recv_sem=recv_sems.at[k - 1],
                device_id=(px, py, c), device_id_type=MESH)
            cp.start()
            copies.append(cp)
        for cp in copies:
            cp.wait_recv()
        for cp in copies:
            cp.wait_send()
        local.wait()

    anyspace = pl.BlockSpec(memory_space=pl.ANY)
    return pl.pallas_call(
        body, name="chip_exchange",
        out_shape=jax.ShapeDtypeStruct(chip_part.shape, chip_part.dtype),
        in_specs=[anyspace], out_specs=anyspace,
        scratch_shapes=[pltpu.SemaphoreType.DMA((3,)), pltpu.SemaphoreType.DMA((3,)), pltpu.SemaphoreType.DMA],
    )(chip_part)


def _sum_slots(land):
    slots, r, n = land.shape
    tile = next(t for t in (512, 256, 128) if r % t == 0)

    def body(land_ref, out_ref):
        acc = land_ref[0].astype(F32)
        for j in range(1, slots):
            acc = acc + land_ref[j].astype(F32)
        out_ref[...] = acc

    return pl.pallas_call(
        body, name="sum_slots", grid=(r // tile,),
        in_specs=[pl.BlockSpec((slots, tile, n), lambda i: (0, i, 0))],
        out_specs=pl.BlockSpec((tile, n), lambda i: (i, 0)),
        out_shape=jax.ShapeDtypeStruct((r, n), F32),
        compiler_params=_params("arbitrary"),
    )(land)


def _adamw(w, g, m, v):
    r, c = w.shape
    tile = 256 if r % 256 == 0 else r

    def body(w_ref, g_ref, m_ref, v_ref, d_ref, nm_ref, nv_ref):
        g = g_ref[...]
        nm = ADAM_B1 * m_ref[...] + (1.0 - ADAM_B1) * g
        nv = ADAM_B2 * v_ref[...] + (1.0 - ADAM_B2) * (g * g)
        m_hat = nm / (1.0 - ADAM_B1 ** ADAM_STEP)
        v_hat = nv / (1.0 - ADAM_B2 ** ADAM_STEP)
        d_ref[...] = -ADAM_LR * (m_hat / (jnp.sqrt(v_hat) + ADAM_EPS) + ADAM_WD * w_ref[...])
        nm_ref[...] = nm
        nv_ref[...] = nv

    spec = pl.BlockSpec((tile, c), lambda i: (i, 0))
    out = jax.ShapeDtypeStruct((r, c), F32)
    return pl.pallas_call(
        body, name="adamw", grid=(r // tile,), in_specs=[spec] * 4, out_specs=[spec] * 3,
        out_shape=[out, out, out], compiler_params=_params("arbitrary"),
    )(w, g, m, v)


_SLAB = 8 * LANES


def _flat_rows(a):
    flat = a.reshape(-1)
    pad = (-flat.shape[0]) % _SLAB
    if pad:
        flat = jnp.concatenate([flat, jnp.zeros((pad,), flat.dtype)])
    return flat.reshape(-1, LANES)


def _pack(arrays):
    slabs = [_flat_rows(a) for a in arrays]
    return jnp.concatenate(slabs, axis=0), [s.shape[0] for s in slabs]


def _unpack(packed, like, rows):
    out, at = [], 0
    for a, r in zip(like, rows):
        out.append(packed[at:at + r].reshape(-1)[:a.size].reshape(a.shape))
        at += r
    return out


SMALL = ('norm_pre', 'norm_post', 'ssm_a_re', 'ssm_a_im', 'ssm_log_dt', 'ssm_b_re', 'ssm_b_im', 'ssm_c_re',
         'ssm_c_im', 'ssm_d', 'b_glu', 'na_rpb', 'ple_norm')
BIG = ('w_in', 'w_glu', 'w_out', 'w_ple', 'w_ple_gate')
ORDER = ('norm_pre', 'norm_post', 'w_in', 'ssm_a_re', 'ssm_a_im', 'ssm_log_dt', 'ssm_b_re', 'ssm_b_im', 'ssm_c_re',
         'ssm_c_im', 'ssm_d', 'w_glu', 'b_glu', 'na_rpb', 'w_out', 'w_ple', 'ple_norm', 'w_ple_gate')


def kernel(x, p, norm_pre, norm_post, w_in, ssm_a_re, ssm_a_im, ssm_log_dt, ssm_b_re, ssm_b_im, ssm_c_re, ssm_c_im, ssm_d, w_glu, b_glu, na_rpb, w_out, w_ple, ple_norm, w_ple_gate, loss_target, m_norm_pre, m_norm_post, m_w_in, m_ssm_a_re, m_ssm_a_im, m_ssm_log_dt, m_ssm_b_re, m_ssm_b_im, m_ssm_c_re, m_ssm_c_im, m_ssm_d, m_w_glu, m_b_glu, m_na_rpb, m_w_out, m_w_ple, m_ple_norm, m_w_ple_gate, v_norm_pre, v_norm_post, v_w_in, v_ssm_a_re, v_ssm_a_im, v_ssm_log_dt, v_ssm_b_re, v_ssm_b_im, v_ssm_c_re, v_ssm_c_im, v_ssm_d, v_w_glu, v_b_glu, v_na_rpb, v_w_out, v_w_ple, v_ple_norm, v_w_ple_gate):
    args = dict(locals())
    weights = {n: args[n] for n in ORDER}
    mom_m = {n: args["m_" + n] for n in ORDER}
    mom_v = {n: args["v_" + n] for n in ORDER}

    shards = [weights[n][0].astype(BF16) for n in BIG]
    packed, shard_rows = _pack(shards)
    gathered = _all_gather(packed).reshape(N_DEV, -1, LANES)
    full = []
    at = 0
    for s, r in zip(shards, shard_rows):
        full.append(gathered[:, at:at + r].reshape((N_DEV,) + s.shape))
        at += r
    w_in_g, w_glu_g, w_out_g, w_ple_g, w_pg_g = full
    w_glu_f = w_glu_g.reshape(D_SSM, D_SSM)
    w_out_f = w_out_g.reshape(D_MODEL, D_MODEL)
    w_pg_f = w_pg_g.reshape(D_MODEL, D_MODEL)

    ssm = tuple(weights[n][0] for n in ('ssm_a_re', 'ssm_a_im', 'ssm_log_dt', 'ssm_b_re', 'ssm_b_im',
                                        'ssm_c_re', 'ssm_c_im', 'ssm_d'))
    sq, dx, grads = _local_step(x[0], p[0, 0], loss_target[0], norm_pre, norm_post, w_in_g, ssm, w_glu_f, b_glu,
                                na_rpb[0], w_out_f, w_ple_g, ple_norm, w_pg_f)
    loss = lax.psum(0.5 / D_MODEL * jnp.sum(sq), ("x", "y", "c"))

    local = dict(norm_pre=grads['norm_pre'], norm_post=grads['norm_post'], b_glu=grads['b_glu'],
                 na_rpb=grads['na_rpb'][None], ple_norm=grads['ple_norm'])
    for n, g in zip(('ssm_a_re', 'ssm_a_im', 'ssm_log_dt', 'ssm_b_re', 'ssm_b_im', 'ssm_c_re', 'ssm_c_im', 'ssm_d'),
                    grads['ssm']):
        local[n] = g[None]

    big_parts = [grads['w_in'],
                 grads['w_glu'].reshape(N_DEV, D_SSM // N_DEV, D_SSM),
                 grads['w_out'].reshape(N_DEV, D_MODEL // N_DEV, D_MODEL),
                 grads['w_ple'],
                 grads['w_ple_gate'].reshape(N_DEV, D_MODEL // N_DEV, D_MODEL)]
    small_flat, small_rows = _pack([local[n] for n in SMALL])
    pad = (-small_flat.shape[0]) % (128 * N_DEV)
    if pad:
        small_flat = jnp.concatenate([small_flat, jnp.zeros((pad, LANES), F32)], axis=0)
    small_per = small_flat.shape[0] // N_DEV
    part = jnp.concatenate([bp.reshape(N_DEV, -1, LANES) for bp in big_parts], axis=1)
    reduced = _sum_slots(_chip_exchange(_pair_sum(part, _pair_exchange(part))))
    big_grads = {}
    at = 0
    for n, bp in zip(BIG, big_parts):
        r = bp[0].size // LANES
        big_grads[n] = reduced[at:at + r].reshape(bp.shape[1:])
        at += r
    small_all = _all_gather(_sum_slots(_exchange_partials(small_flat.reshape(N_DEV, small_per, LANES))))

    outs = {}
    for n in BIG:
        w2 = weights[n][0]
        d, nm, nv = _adamw(w2, big_grads[n], mom_m[n][0], mom_v[n][0])
        outs[n] = (big_grads[n][None], d[None], nm[None], nv[None])
    like = [weights[n] for n in SMALL]
    w_s, rows_s = _pack(like)
    m_s, _ = _pack([mom_m[n] for n in SMALL])
    v_s, _ = _pack([mom_v[n] for n in SMALL])
    g_s = small_all[:w_s.shape[0]]
    d_s, nm_s, nv_s = _adamw(w_s, g_s, m_s, v_s)
    unpacked = [_unpack(a, like, rows_s) for a in (g_s, d_s, nm_s, nv_s)]
    for i, n in enumerate(SMALL):
        outs[n] = tuple(u[i] for u in unpacked)

    return (loss, dx[None], *[outs[n][0] for n in ORDER], *[outs[n][1] for n in ORDER],
            *[outs[n][2] for n in ORDER], *[outs[n][3] for n in ORDER])
```

```python
import functools

import jax
import jax.numpy as jnp
from jax import lax
from jax.experimental import pallas as pl
from jax.experimental.pallas import tpu as pltpu

F32 = jnp.float32
BF16 = jnp.bfloat16
HIGHEST = lax.Precision.HIGHEST

D_MODEL = 1024
D_PLE = 256
GRID_W = 64
D_SSM = 512
SSM_GROUP = 16
N_GROUPS = 32
N_PAIRS = 16
SSM_STATE = 64
D_NA = 512
NA_HEADS = 8
NA_HEAD_DIM = 64
NA_ROWS = 8
NA_COLS = 16
D_IN_PROJ = 3072
EPS = 1e-6
N_DEV = 8
SHARD_IN = D_IN_PROJ // N_DEV
LANES = 128
SSM_CHUNK = 16
TOK_TILE = 256
NA_QROWS = 4
NA_KROWS = 12
NEG = -1e30
VMEM_LIMIT = 56 * 1024 * 1024

ADAM_LR = 0.001
ADAM_B1 = 0.9
ADAM_B2 = 0.999
ADAM_EPS = 1e-08
ADAM_WD = 0.01
ADAM_STEP = 10

MESH = pl.DeviceIdType.MESH


def _params(*sem):
    return pltpu.CompilerParams(dimension_semantics=sem or None, vmem_limit_bytes=VMEM_LIMIT)


def _whole_vmem():
    return pl.BlockSpec(memory_space=pltpu.VMEM)


def _nt(a, b):
    return lax.dot_general(a, b, (((1,), (1,)), ((), ())), preferred_element_type=F32)


def _tn(a, b):
    return lax.dot_general(a, b, (((0,), (0,)), ((), ())), preferred_element_type=F32)


def _mm(a, b):
    return jnp.dot(a, b, preferred_element_type=F32)


def _sigmoid(x):
    return 1.0 / (1.0 + jnp.exp(-x))


_GELU_C = 0.7978845608028654


def _gelu(x):
    return 0.5 * x * (1.0 + jnp.tanh(_GELU_C * (x + 0.044715 * x * x * x)))


def _gelu_grad(x):
    th = jnp.tanh(_GELU_C * (x + 0.044715 * x * x * x))
    return 0.5 * (1.0 + th) + 0.5 * x * (1.0 - th * th) * _GELU_C * (1.0 + 3.0 * 0.044715 * x * x)


def _ssm_tables(a_re, a_im, log_dt, b_re, b_im, c_re, c_im, d):
    T, P, H = SSM_CHUNK, SSM_STATE, SSM_GROUP
    dt = jnp.exp(log_dt)[..., None]
    xr = dt * a_re
    xi = dt * a_im
    mag = jnp.exp(xr)
    lr = mag * jnp.cos(xi)
    li = mag * jnp.sin(xi)
    den = a_re * a_re + a_im * a_im
    cr = ((lr - 1.0) * a_re + li * a_im) / den
    ci = (li * a_re - (lr - 1.0) * a_im) / den
    bbr = cr[..., None] * b_re - ci[..., None] * b_im
    bbi = cr[..., None] * b_im + ci[..., None] * b_re
    kk = jnp.arange(T + 1, dtype=F32)[:, None, None, None]
    pm = jnp.exp(kk * xr)
    pw = jnp.stack([pm * jnp.cos(kk * xi), pm * jnp.sin(kk * xi)], axis=2)
    pw = pw.reshape(T + 1, 2, 2, N_PAIRS, 2 * P).transpose(3, 0, 1, 2, 4).reshape(N_PAIRS, T + 1, 8 * P)
    pw = jnp.concatenate([pw, jnp.zeros((N_PAIRS, 24 - (T + 1), 8 * P), F32)], axis=1)
    eye2 = jnp.eye(2, dtype=F32)

    def expand(t):
        t = t.transpose(2, 0, 1, 3, 4, 5)
        t = t[:, :, :, :, :, None, :] * eye2[None, None, None, :, None, :, None]
        return t.reshape(N_PAIRS, 4, 2 * H, 2 * P)

    bb = expand(jnp.stack([bbr, bbi], axis=1).reshape(2, 2, N_PAIRS, 2, P, H).transpose(0, 1, 2, 3, 5, 4))
    cc = expand(jnp.stack([c_re, c_im], axis=1).reshape(2, 2, N_PAIRS, 2, H, P))
    dd = d.reshape(N_PAIRS, 2 * H)[:, :, None] * jnp.eye(2 * H, dtype=F32)[None]
    dd = jnp.concatenate([dd, jnp.zeros((N_PAIRS, 2 * H, LANES - 2 * H), F32)], axis=2)
    return pw, bb, cc, dd


NA_TAB = 2 * NA_ROWS


def _na_table(rpb):
    qc = jnp.arange(GRID_W)[:, None, None]
    kc = (jnp.arange(2 * GRID_W) % GRID_W)[None, :, None]
    dc = jnp.arange(2 * NA_COLS - 1)[None, None, :]
    cstart = jnp.clip(qc - NA_COLS // 2, 0, GRID_W - NA_COLS)
    csel = ((kc >= cstart) & (kc < cstart + NA_COLS) & (kc - qc + NA_COLS - 1 == dc)).astype(F32)
    col_ok = jnp.sum(csel, axis=-1) > 0.5
    part = jnp.einsum('hrc,qmc->hrqm', rpb, csel, precision=HIGHEST)
    zero = jnp.zeros_like(part[:, :1])
    odd = jnp.arange(2 * GRID_W) >= GRID_W
    tab = jnp.where(odd, jnp.concatenate([part, zero], axis=1), jnp.concatenate([zero, part], axis=1))
    return jnp.where(col_ok, tab, NEG)


def _fwd_in(x2, g_pre, w_in_g):
    seq = x2.shape[0]

    def body(x_ref, g_ref, w_ref, u_ref, zs_ref, qkv_ref, zn_ref, hn_ref):
        x = x_ref[...]
        r = lax.rsqrt(jnp.mean(x * x, axis=-1, keepdims=True) + EPS)
        hn = (x * r * g_ref[...]).astype(BF16)
        hn_ref[...] = hn
        for j in range(N_DEV):
            pj = _mm(hn, w_ref[j])
            for i in range(SHARD_IN // LANES):
                blk = (SHARD_IN // LANES) * j + i
                piece = pj[:, i * LANES:(i + 1) * LANES]
                if blk < 4:
                    u_ref[:, blk * LANES:(blk + 1) * LANES] = piece
                elif blk < 8:
                    zs_ref[:, (blk - 4) * LANES:(blk - 3) * LANES] = piece
                elif blk < 20:
                    qkv_ref[:, (blk - 8) * LANES:(blk - 7) * LANES] = piece.astype(BF16)
                else:
                    zn_ref[:, (blk - 20) * LANES:(blk - 19) * LANES] = piece

    tok = lambda w: pl.BlockSpec((TOK_TILE, w), lambda i: (i, 0))
    return pl.pallas_call(
        body, name="fwd_in", grid=(seq // TOK_TILE,),
        in_specs=[tok(D_MODEL), pl.BlockSpec((1, D_MODEL), lambda i: (0, 0)), _whole_vmem()],
        out_specs=[tok(D_SSM), tok(D_SSM), tok(3 * D_NA), tok(D_NA), tok(D_MODEL)],
        out_shape=[jax.ShapeDtypeStruct((seq, D_SSM), F32), jax.ShapeDtypeStruct((seq, D_SSM), F32),
                   jax.ShapeDtypeStruct((seq, 3 * D_NA), BF16), jax.ShapeDtypeStruct((seq, D_NA), F32),
                   jax.ShapeDtypeStruct((seq, D_MODEL), BF16)],
        compiler_params=_params("arbitrary"),
    )(x2, g_pre, w_in_g)


def _mid(yssm, zs, o, zn, x2, p2, tgt, w_glu, b_glu, w_out, g_post, w_ple_g, g_ple, w_pg):
    seq = x2.shape[0]

    def body(yssm_ref, zs_ref, o_ref, zn_ref, x_ref, p_ref, tgt_ref, wglu_ref, bglu_ref, wout_ref, gpost_ref,
             wple_ref, gple_ref, wpg_ref,
             dyssm_ref, dzs_ref, do_ref, dzn_ref, dh1_ref, loss_ref, dwglu_ref, dbglu_ref, dwout_ref, dwple_ref,
             dwpg_ref, dgpost_ref, dgple_ref):
        @pl.when(pl.program_id(0) == 0)
        def _():
            for ref in (loss_ref, dwglu_ref, dbglu_ref, dwout_ref, dwple_ref, dwpg_ref, dgpost_ref, dgple_ref):
                ref[...] = jnp.zeros_like(ref)

        yv = yssm_ref[...]
        g1 = _gelu(yv)
        g1b = g1.astype(BF16)
        sg = _sigmoid(_mm(g1b, wglu_ref[...]) + bglu_ref[...])
        zs_v = zs_ref[...]
        s_zs = _sigmoid(zs_v)
        g2 = g1 * sg
        zn_v = zn_ref[...]
        s_zn = _sigmoid(zn_v)
        o_v = o_ref[...]
        cat = jnp.concatenate([g2 * (zs_v * s_zs), o_v * (zn_v * s_zn)], axis=1).astype(BF16)
        mix = _mm(cat, wout_ref[...])
        r2 = lax.rsqrt(jnp.mean(mix * mix, axis=-1, keepdims=True) + EPS)
        n2 = mix * r2
        gpost = gpost_ref[...]
        h1 = x_ref[...] + n2 * gpost
        pb = p_ref[...].astype(BF16)
        epre = jnp.concatenate([_mm(pb, wple_ref[j]) for j in range(N_DEV)], axis=1)
        r3 = lax.rsqrt(jnp.mean(epre * epre, axis=-1, keepdims=True) + EPS)
        n3 = epre * r3
        gple = gple_ref[...]
        e = n3 * gple
        h1b = h1.astype(BF16)
        gate = _sigmoid(_mm(h1b, wpg_ref[...]))
        diff = h1 + gate * e - tgt_ref[...]
        loss_ref[...] += jnp.sum(diff * diff, axis=0, keepdims=True)

        dy = diff * (1.0 / D_MODEL)
        dgp = (dy * e * gate * (1.0 - gate)).astype(BF16)
        de = dy * gate
        dh1 = dy + _nt(dgp, wpg_ref[...])
        dh1_ref[...] = dh1
        dwpg_ref[...] += _tn(h1b, dgp)
        dgple_ref[...] += jnp.sum(de * n3, axis=0, keepdims=True)
        dn3 = de * gple
        depre = (r3 * (dn3 - n3 * jnp.mean(dn3 * n3, axis=-1, keepdims=True))).astype(BF16)
        for j in range(N_DEV):
            dwple_ref[j] += _tn(pb, depre[:, j * LANES:(j + 1) * LANES])
        dgpost_ref[...] += jnp.sum(dh1 * n2, axis=0, keepdims=True)
        dn2 = dh1 * gpost
        dmix = (r2 * (dn2 - n2 * jnp.mean(dn2 * n2, axis=-1, keepdims=True))).astype(BF16)
        dcat = _nt(dmix, wout_ref[...])
        dwout_ref[...] += _tn(cat, dmix)

        dys = dcat[:, :D_SSM]
        dyn = dcat[:, D_SSM:]
        dg2 = dys * (zs_v * s_zs)
        dzs_ref[...] = dys * g2 * (s_zs * (1.0 + zs_v * (1.0 - s_zs)))
        dt = dg2 * g2 * (1.0 - sg)
        dtb = dt.astype(BF16)
        dg1 = dg2 * sg + _nt(dtb, wglu_ref[...])
        dwglu_ref[...] += _tn(g1b, dtb)
        dbglu_ref[...] += jnp.sum(dt, axis=0, keepdims=True)
        dyssm_ref[...] = dg1 * _gelu_grad(yv)
        do_ref[...] = dyn * (zn_v * s_zn)
        dzn_ref[...] = dyn * o_v * (s_zn * (1.0 + zn_v * (1.0 - s_zn)))

    tok = lambda w: pl.BlockSpec((TOK_TILE, w), lambda i: (i, 0))
    row = lambda w: pl.BlockSpec((1, w), lambda i: (0, 0))
    vm = _whole_vmem()
    half = jax.ShapeDtypeStruct((seq, D_SSM), F32)
    gain = jax.ShapeDtypeStruct((1, D_MODEL), F32)
    return pl.pallas_call(
        body, name="mid", grid=(seq // TOK_TILE,),
        in_specs=[tok(D_SSM), tok(D_SSM), tok(D_NA), tok(D_NA), tok(D_MODEL), tok(D_PLE), tok(D_MODEL),
                  vm, row(D_SSM), vm, row(D_MODEL), vm, row(D_MODEL), vm],
        out_specs=[tok(D_SSM), tok(D_SSM), tok(D_NA), tok(D_NA), tok(D_MODEL), vm, vm, vm, vm, vm, vm, vm, vm],
        out_shape=[half, half, half, half, jax.ShapeDtypeStruct((seq, D_MODEL), F32), gain,
                   jax.ShapeDtypeStruct((D_SSM, D_SSM), F32), jax.ShapeDtypeStruct((1, D_SSM), F32),
                   jax.ShapeDtypeStruct((D_MODEL, D_MODEL), F32),
                   jax.ShapeDtypeStruct((N_DEV, D_PLE, LANES), F32),
                   jax.ShapeDtypeStruct((D_MODEL, D_MODEL), F32), gain, gain],
        compiler_params=_params("arbitrary"),
    )(yssm, zs, o, zn, x2, p2, tgt, w_glu, b_glu, w_out, g_post, w_ple_g, g_ple, w_pg)


def _bwd_in(du, dzs, dq, dk, dv, dzn, hn, x2, dh1, g_pre, w_in_g):
    seq = x2.shape[0]

    def body(du_ref, dzs_ref, dq_ref, dk_ref, dv_ref, dzn_ref, hn_ref, x_ref, dh1_ref, g_ref, w_ref,
             dx_ref, dw_ref, dg_ref, dproj_ref):
        @pl.when(pl.program_id(0) == 0)
        def _():
            dw_ref[...] = jnp.zeros_like(dw_ref)
            dg_ref[...] = jnp.zeros_like(dg_ref)

        for k, ref in enumerate((du_ref, dzs_ref, dq_ref, dk_ref, dv_ref, dzn_ref)):
            dproj_ref[:, k * D_SSM:(k + 1) * D_SSM] = ref[...].astype(BF16)
        hn = hn_ref[...]
        dhn = jnp.zeros((TOK_TILE, D_MODEL), F32)
        for j in range(N_DEV):
            dpj = dproj_ref[:, j * SHARD_IN:(j + 1) * SHARD_IN]
            dhn += _nt(dpj, w_ref[j])
            dw_ref[j] += _tn(hn, dpj)
        x = x_ref[...]
        r = lax.rsqrt(jnp.mean(x * x, axis=-1, keepdims=True) + EPS)
        n1 = x * r
        dg_ref[...] += jnp.sum(dhn * n1, axis=0, keepdims=True)
        dn1 = dhn * g_ref[...]
        dx_ref[...] = dh1_ref[...] + r * (dn1 - n1 * jnp.mean(dn1 * n1, axis=-1, keepdims=True))

    tok = lambda w: pl.BlockSpec((TOK_TILE, w), lambda i: (i, 0))
    vm = _whole_vmem()
    return pl.pallas_call(
        body, name="bwd_in", grid=(seq // TOK_TILE,),
        in_specs=[tok(D_SSM)] * 6 + [tok(D_MODEL), tok(D_MODEL), tok(D_MODEL),
                                      pl.BlockSpec((1, D_MODEL), lambda i: (0, 0)), vm],
        out_specs=[tok(D_MODEL), vm, vm],
        out_shape=[jax.ShapeDtypeStruct((seq, D_MODEL), F32),
                   jax.ShapeDtypeStruct((N_DEV, D_MODEL, SHARD_IN), F32),
                   jax.ShapeDtypeStruct((1, D_MODEL), F32)],
        scratch_shapes=[pltpu.VMEM((TOK_TILE, D_IN_PROJ), BF16)],
        compiler_params=_params("arbitrary"),
    )(du, dzs, dq, dk, dv, dzn, hn, x2, dh1, g_pre, w_in_g)


RELAYOUT_CHUNKS = 128
PAIR_W = 2 * SSM_GROUP
PAIRS_PER_BLOCK = LANES // PAIR_W
CHUNK_W = SSM_CHUNK * PAIR_W
PW_ROWS = 24


def _lane_window(lo, width):
    lane = lax.broadcasted_iota(jnp.int32, (1, LANES), 1)
    return (lane >= lo) & (lane < lo + width)


def _to_pairs(a, nc):
    ncb = min(RELAYOUT_CHUNKS, nc)

    def body(x_ref, out_ref):
        xs = [x_ref[pl.ds(s, ncb, stride=SSM_CHUNK), :] for s in range(SSM_CHUNK)]
        for a_ in range(PAIRS_PER_BLOCK):
            for v in range(SSM_CHUNK // PAIRS_PER_BLOCK):
                acc = None
                for i in range(PAIRS_PER_BLOCK):
                    shift = (PAIR_W * (i - a_)) % LANES
                    piece = xs[PAIRS_PER_BLOCK * v + i]
                    piece = pltpu.roll(piece, shift, axis=1) if shift else piece
                    acc = piece if acc is None else jnp.where(_lane_window(PAIR_W * i, PAIR_W), piece, acc)
                out_ref[a_, :, LANES * v:LANES * (v + 1)] = acc.astype(BF16)

    return pl.pallas_call(
        body, name="to_pairs", grid=(D_SSM // LANES, nc // ncb),
        in_specs=[pl.BlockSpec((ncb * SSM_CHUNK, LANES), lambda cl, cb: (cb, cl))],
        out_specs=pl.BlockSpec((PAIRS_PER_BLOCK, ncb, CHUNK_W), lambda cl, cb: (cl, cb, 0)),
        out_shape=jax.ShapeDtypeStruct((N_PAIRS, nc, CHUNK_W), BF16),
        compiler_params=_params("arbitrary", "arbitrary"),
    )(a)


def _from_pairs(a, nc):
    ncb = min(RELAYOUT_CHUNKS, nc)

    def body(y_ref, out_ref):
        for s in range(SSM_CHUNK):
            v, i = divmod(s, PAIRS_PER_BLOCK)
            acc = None
            for a_ in range(PAIRS_PER_BLOCK):
                shift = (PAIR_W * (a_ - i)) % LANES
                piece = y_ref[a_, :, LANES * v:LANES * (v + 1)]
                piece = pltpu.roll(piece, shift, axis=1) if shift else piece
                acc = piece if acc is None else jnp.where(_lane_window(PAIR_W * a_, PAIR_W), piece, acc)
            out_ref[pl.ds(s, ncb, stride=SSM_CHUNK), :] = acc

    return pl.pallas_call(
        body, name="from_pairs", grid=(D_SSM // LANES, nc // ncb),
        in_specs=[pl.BlockSpec((PAIRS_PER_BLOCK, ncb, CHUNK_W), lambda cl, cb: (cl, cb, 0))],
        out_specs=pl.BlockSpec((ncb * SSM_CHUNK, LANES), lambda cl, cb: (cb, cl)),
        out_shape=jax.ShapeDtypeStruct((nc * SSM_CHUNK, D_SSM), F32),
        compiler_params=_params("arbitrary", "arbitrary"),
    )(a)


def _boundary_scan(nc, pw_ref, src_ref, dst_ref, conj):
    nblk = nc // 8
    lr0, li0, lr1, li1 = (pw_ref[0, SSM_CHUNK:SSM_CHUNK + 1, LANES * q:LANES * (q + 1)] for q in range(4))
    if conj:
        li0, li1 = -li0, -li1

    def step(i, carry):
        hr0, hi0, hr1, hi1 = carry
        up = pl.multiple_of(i * 8, 8)
        dn = pl.multiple_of((nblk - 1 - i) * 8, 8)
        a_rows, b_rows = (dn, up) if conj else (up, dn)
        s_r0 = src_ref[pl.ds(a_rows, 8), 0:128]
        s_i0 = src_ref[pl.ds(a_rows, 8), 128:256]
        s_r1 = src_ref[pl.ds(b_rows, 8), 256:384]
        s_i1 = src_ref[pl.ds(b_rows, 8), 384:512]
        o_r0, o_i0, o_r1, o_i1 = [], [], [], []
        for k in range(8):
            ka = 7 - k if conj else k
            kb = k if conj else 7 - k
            o_r0.append((ka, hr0))
            o_i0.append((ka, hi0))
            o_r1.append((kb, hr1))
            o_i1.append((kb, hi1))
            hr0, hi0 = (lr0 * hr0 - li0 * hi0 + s_r0[ka:ka + 1], lr0 * hi0 + li0 * hr0 + s_i0[ka:ka + 1])
            hr1, hi1 = (lr1 * hr1 - li1 * hi1 + s_r1[kb:kb + 1], lr1 * hi1 + li1 * hr1 + s_i1[kb:kb + 1])

        def rows(items):
            return jnp.concatenate([v for _, v in sorted(items, key=lambda kv: kv[0])], axis=0)

        dst_ref[pl.ds(a_rows, 8), 0:128] = rows(o_r0)
        dst_ref[pl.ds(a_rows, 8), 128:256] = rows(o_i0)
        dst_ref[pl.ds(b_rows, 8), 256:384] = rows(o_r1)
        dst_ref[pl.ds(b_rows, 8), 384:512] = rows(o_i1)
        return hr0, hi0, hr1, hi1

    z = jnp.zeros((1, LANES), F32)
    lax.fori_loop(0, nblk, step, (z, z, z, z))


def _pw_row(pw_ref, k, q):
    return pw_ref[0, k:k + 1, LANES * q:LANES * (q + 1)]


def _mm_f32(a, b, dims):
    return lax.dot_general(a, b, (dims, ((), ())), precision=HIGHEST, preferred_element_type=F32)


_POW_M = (lambda s: SSM_CHUNK - 1 - s, lambda s: s)
_POW_C = (lambda s: s + 1, lambda s: SSM_CHUNK - s)
_POW_K = (lambda s: s, lambda s: SSM_CHUNK - 1 - s)


def _chunk_matrices(pw_ref, bb_ref, cc_ref, dd_ref, m_scr, ct_scr, toep_scr, g_scr, kt_scr):
    blk = lambda s: slice(PAIR_W * s, PAIR_W * (s + 1))
    col = lambda q: slice(LANES * q, LANES * (q + 1))
    for d in range(2):
        bbr, bbi = bb_ref[0, 2 * d], bb_ref[0, 2 * d + 1]
        ccr, cci = cc_ref[0, 2 * d], cc_ref[0, 2 * d + 1]
        for s in range(SSM_CHUNK):
            pr, pi = _pw_row(pw_ref, _POW_M[d](s), 2 * d), _pw_row(pw_ref, _POW_M[d](s), 2 * d + 1)
            m_scr[blk(s), col(2 * d)] = (pr * bbr - pi * bbi).astype(m_scr.dtype)
            m_scr[blk(s), col(2 * d + 1)] = (pr * bbi + pi * bbr).astype(m_scr.dtype)
            pr, pi = _pw_row(pw_ref, _POW_C[d](s), 2 * d), _pw_row(pw_ref, _POW_C[d](s), 2 * d + 1)
            ct_scr[blk(s), col(2 * d)] = (ccr * pr - cci * pi).astype(ct_scr.dtype)
            ct_scr[blk(s), col(2 * d + 1)] = (-(ccr * pi + cci * pr)).astype(ct_scr.dtype)
            pr, pi = _pw_row(pw_ref, _POW_K[d](s), 2 * d), _pw_row(pw_ref, _POW_K[d](s), 2 * d + 1)
            g_scr[d, blk(s), 0:LANES] = ccr * pr - cci * pi
            g_scr[d, blk(s), LANES:2 * LANES] = -(ccr * pi + cci * pr)
        kt = _mm_f32(jnp.concatenate([bbr, bbi], axis=1), g_scr[d], ((1,), (1,)))
        if d == 0:
            kt = jnp.concatenate([kt[:, 0:LANES] + dd_ref[0], kt[:, LANES:]], axis=1)
        kt_scr[d] = kt
    lane = lax.broadcasted_iota(jnp.int32, (1, CHUNK_W), 1)
    for s in range(SSM_CHUNK):
        lo = PAIR_W * s
        hi = PAIR_W * (s + 1)
        fwd = kt_scr[0] if s == 0 else pltpu.roll(kt_scr[0], lo, axis=1)
        bwd = kt_scr[1] if hi == CHUNK_W else pltpu.roll(kt_scr[1], hi, axis=1)
        row = jnp.where(lane >= lo, fwd, 0.0) + jnp.where(lane < hi, bwd, 0.0)
        toep_scr[blk(s), :] = row.astype(toep_scr.dtype)


def _ssm_scratch(nc, mat_dtype):
    return [pltpu.VMEM((CHUNK_W, CHUNK_W), mat_dtype), pltpu.VMEM((CHUNK_W, CHUNK_W), mat_dtype),
            pltpu.VMEM((CHUNK_W, CHUNK_W), mat_dtype), pltpu.VMEM((2, CHUNK_W, 2 * LANES), F32),
            pltpu.VMEM((2, PAIR_W, CHUNK_W), F32), pltpu.VMEM((nc, CHUNK_W), F32), pltpu.VMEM((nc, CHUNK_W), F32)]


def _per_pair(*shape):
    return pl.BlockSpec((1,) + shape, lambda g: (g,) + (0,) * len(shape))


_TABLE_SPECS = lambda: [_per_pair(PW_ROWS, CHUNK_W), _per_pair(4, PAIR_W, LANES), _per_pair(4, PAIR_W, LANES),
                        _per_pair(PAIR_W, LANES)]


def _ssm_fwd(u, pw, bb, cc, dd):
    npair, nc, width = u.shape

    def body(u_ref, pw_ref, bb_ref, cc_ref, dd_ref, y_ref, hin_ref, m_scr, ct_scr, toep_scr, g_scr, kt_scr, s_scr, h_scr):
        _chunk_matrices(pw_ref, bb_ref, cc_ref, dd_ref, m_scr, ct_scr, toep_scr, g_scr, kt_scr)
        uv = u_ref[0]
        s_scr[...] = _mm(uv, m_scr[...])
        _boundary_scan(nc, pw_ref, s_scr, h_scr, conj=False)
        hin = h_scr[...]
        hin_ref[0] = hin
        y_ref[0] = _mm(uv, toep_scr[...]) + _nt(hin.astype(u.dtype), ct_scr[...])

    return pl.pallas_call(
        body, name="ssm_fwd", grid=(npair,),
        in_specs=[_per_pair(nc, width)] + _TABLE_SPECS(),
        out_specs=[_per_pair(nc, width), _per_pair(nc, width)],
        out_shape=[jax.ShapeDtypeStruct((npair, nc, width), F32), jax.ShapeDtypeStruct((npair, nc, width), F32)],
        scratch_shapes=_ssm_scratch(nc, u.dtype),
        compiler_params=_params("arbitrary"),
    )(u, pw, bb, cc, dd)


def _ssm_bwd(u, dy, hin, pw, bb, cc, dd):
    npair, nc, width = u.shape

    def body(u_ref, dy_ref, hin_ref, pw_ref, bb_ref, cc_ref, dd_ref, du_ref, dpw_ref, dbb_ref, dcc_ref, ddd_ref,
             m_scr, ct_scr, toep_scr, g_scr, kt_scr, dh_scr, ds_scr):
        _chunk_matrices(pw_ref, bb_ref, cc_ref, dd_ref, m_scr, ct_scr, toep_scr, g_scr, kt_scr)
        uv = u_ref[0]
        dyb = dy_ref[0]
        hin = hin_ref[0]
        dh_scr[...] = _mm(dyb, ct_scr[...])
        _boundary_scan(nc, pw_ref, dh_scr, ds_scr, conj=True)
        ds = ds_scr[...]
        dsb = ds.astype(u.dtype)
        du_ref[0] = _nt(dsb, m_scr[...]) + _nt(dyb, toep_scr[...])
        dm = _tn(uv, dsb)
        dct = _tn(dyb, hin.astype(u.dtype))
        dtoep = _tn(uv, dyb)

        dpw_ref[...] = jnp.zeros_like(dpw_ref)
        blk = lambda s: slice(PAIR_W * s, PAIR_W * (s + 1))
        col = lambda q: slice(LANES * q, LANES * (q + 1))

        def add_pw(k, q, val):
            dpw_ref[0, k:k + 1, col(q)] += jnp.sum(val, axis=0, keepdims=True)

        for d in range(2):
            g_r, g_i = ds[:, col(2 * d)], ds[:, col(2 * d + 1)]
            h_r, h_i = hin[:, col(2 * d)], hin[:, col(2 * d + 1)]
            add_pw(SSM_CHUNK, 2 * d, g_r * h_r + g_i * h_i)
            add_pw(SSM_CHUNK, 2 * d + 1, g_i * h_r - g_r * h_i)

        lane = lax.broadcasted_iota(jnp.int32, (1, CHUNK_W), 1)
        dkt0 = jnp.zeros((PAIR_W, CHUNK_W), F32)
        dkt1 = jnp.zeros((PAIR_W, CHUNK_W), F32)
        for s in range(SSM_CHUNK):
            lo = PAIR_W * s
            hi = PAIR_W * (s + 1)
            row = dtoep[blk(s), :]
            fwd = jnp.where(lane >= lo, row, 0.0)
            bwd = jnp.where(lane < hi, row, 0.0)
            dkt0 += fwd if s == 0 else pltpu.roll(fwd, CHUNK_W - lo, axis=1)
            dkt1 += bwd if hi == CHUNK_W else pltpu.roll(bwd, CHUNK_W - hi, axis=1)
        ddd_ref[0] = dkt0[:, 0:LANES]

        for d, dkt in enumerate((dkt0, dkt1)):
            bbr, bbi = bb_ref[0, 2 * d], bb_ref[0, 2 * d + 1]
            ccr, cci = cc_ref[0, 2 * d], cc_ref[0, 2 * d + 1]
            dbbcat = _mm_f32(dkt, g_scr[d], ((1,), (0,)))
            dg = _mm_f32(dkt, jnp.concatenate([bbr, bbi], axis=1), ((0,), (0,)))
            dbbr, dbbi = dbbcat[:, 0:LANES], dbbcat[:, LANES:]
            dccr = jnp.zeros((PAIR_W, LANES), F32)
            dcci = jnp.zeros((PAIR_W, LANES), F32)
            for s in range(SSM_CHUNK):
                k = _POW_M[d](s)
                pr, pi = _pw_row(pw_ref, k, 2 * d), _pw_row(pw_ref, k, 2 * d + 1)
                gr, gi = dm[blk(s), col(2 * d)], dm[blk(s), col(2 * d + 1)]
                dbbr += gr * pr + gi * pi
                dbbi += gi * pr - gr * pi
                add_pw(k, 2 * d, gr * bbr + gi * bbi)
                add_pw(k, 2 * d + 1, gi * bbr - gr * bbi)
                for k, gr, gi in ((_POW_C[d](s), dct[blk(s), col(2 * d)], dct[blk(s), col(2 * d + 1)]),
                                  (_POW_K[d](s), dg[blk(s), 0:LANES], dg[blk(s), LANES:])):
                    pr, pi = _pw_row(pw_ref, k, 2 * d), _pw_row(pw_ref, k, 2 * d + 1)
                    dccr += gr * pr - gi * pi
                    dcci += -(gr * pi + gi * pr)
                    add_pw(k, 2 * d, gr * ccr - gi * cci)
                    add_pw(k, 2 * d + 1, -(gr * cci + gi * ccr))
            dbb_ref[0, 2 * d] = dbbr
            dbb_ref[0, 2 * d + 1] = dbbi
            dcc_ref[0, 2 * d] = dccr
            dcc_ref[0, 2 * d + 1] = dcci

    return pl.pallas_call(
        body, name="ssm_bwd", grid=(npair,),
        in_specs=[_per_pair(nc, width), _per_pair(nc, width), _per_pair(nc, width)] + _TABLE_SPECS(),
        out_specs=[_per_pair(nc, width)] + _TABLE_SPECS(),
        out_shape=[jax.ShapeDtypeStruct((npair, nc, width), F32), jax.ShapeDtypeStruct(pw.shape, F32),
                   jax.ShapeDtypeStruct(bb.shape, F32), jax.ShapeDtypeStruct(cc.shape, F32),
                   jax.ShapeDtypeStruct(dd.shape, F32)],
        scratch_shapes=_ssm_scratch(nc, u.dtype),
        compiler_params=_params("arbitrary"),
    )(u, dy, hin, pw, bb, cc, dd)


NA_Q = NA_QROWS * GRID_W
NA_K = NA_KROWS * GRID_W
NA_SCALE = NA_HEAD_DIM ** -0.5


def _na_block(b, nb, rows):
    start = jnp.clip(NA_QROWS * b - NA_ROWS // 2, 0, rows - NA_KROWS) * GRID_W
    kind = jnp.where(b == 0, 0, jnp.where(b == nb - 1, 2, 1))
    return pl.multiple_of(start, GRID_W), kind


NA_CHUNK = 16


def _na_pieces(kind, i):
    ri, q0 = divmod(i * NA_CHUNK, GRID_W)
    off = (NA_ROWS - 1, NA_ROWS // 2 - 1, -1)[kind]
    lo = (0, ri, NA_KROWS - NA_ROWS)[kind]
    modes = {(True, True): 'both', (True, False): 'even', (False, True): 'odd', (False, False): None}
    out = []
    for k2 in range(NA_KROWS // 2):
        inside = tuple(lo <= kr < lo + NA_ROWS for kr in (2 * k2, 2 * k2 + 1))
        out.append((2 * k2 - ri + off + 1, slice(q0, q0 + NA_CHUNK), modes[inside]))
    return out


def _na_softmax_pieces(s_ref, tab_ref, hh, kind, i):
    rows = slice(i * NA_CHUNK, (i + 1) * NA_CHUNK)
    lane = lax.broadcasted_iota(jnp.int32, (1, LANES), 1)
    xs = []
    for k2, (t, q, mode) in enumerate(_na_pieces(kind, i)):
        if mode is None:
            xs.append(None)
            continue
        bias = tab_ref[hh, t, q, :]
        if mode == 'even':
            bias = jnp.where(lane < GRID_W, bias, NEG)
        elif mode == 'odd':
            bias = jnp.where(lane >= GRID_W, bias, NEG)
        xs.append(s_ref[hh, rows, k2 * LANES:(k2 + 1) * LANES] + bias)
    live = [x for x in xs if x is not None]
    m = jnp.max(functools.reduce(jnp.maximum, live), axis=-1, keepdims=True)
    es = [None if x is None else jnp.exp(x - m) for x in xs]
    total = jnp.sum(functools.reduce(jnp.add, [e for e in es if e is not None]), axis=-1, keepdims=True)
    inv = 1.0 / total
    return [None if e is None else e * inv for e in es]


def _na_heads():
    lane = lax.broadcasted_iota(jnp.int32, (1, LANES), 1)
    return [lane < NA_HEAD_DIM, lane >= NA_HEAD_DIM]


def _na_fwd(qkv, bias):
    seq = qkv.shape[0]
    rows = seq // GRID_W
    nb = rows // NA_QROWS

    def body(q_ref, k_ref, v_ref, bias_ref, o_ref, s_scr, p_scr):
        start, kind = _na_block(pl.program_id(1), nb, rows)

        def block(static_kind):
            q2 = q_ref[...] * NA_SCALE
            kw = k_ref[pl.ds(start, NA_K), :]
            vw = v_ref[pl.ds(start, NA_K), :]
            heads = _na_heads()
            for hh in range(2):
                s_scr[hh] = _nt(jnp.where(heads[hh], q2, jnp.zeros_like(q2)), kw)
            for hh in range(2):
                for i in range(NA_Q // NA_CHUNK):
                    r = slice(i * NA_CHUNK, (i + 1) * NA_CHUNK)
                    for k2, p in enumerate(_na_softmax_pieces(s_scr, bias_ref, hh, static_kind, i)):
                        p = jnp.zeros((NA_CHUNK, LANES), F32) if p is None else p
                        p_scr[hh, r, k2 * LANES:(k2 + 1) * LANES] = p.astype(p_scr.dtype)
            o_ref[...] = jnp.where(heads[0], _mm(p_scr[0], vw), _mm(p_scr[1], vw))

        for static_kind in range(3):
            pl.when(kind == static_kind)(functools.partial(block, static_kind))

    return pl.pallas_call(
        body, name="na_fwd", grid=(NA_HEADS // 2, nb),
        in_specs=[pl.BlockSpec((NA_Q, LANES), lambda hp, b: (b, hp)),
                  pl.BlockSpec((seq, LANES), lambda hp, b: (0, 4 + hp)),
                  pl.BlockSpec((seq, LANES), lambda hp, b: (0, 8 + hp)),
                  pl.BlockSpec((2, NA_TAB, GRID_W, LANES), lambda hp, b: (hp, 0, 0, 0))],
        out_specs=pl.BlockSpec((NA_Q, LANES), lambda hp, b: (b, hp)),
        out_shape=jax.ShapeDtypeStruct((seq, D_NA), F32),
        scratch_shapes=[pltpu.VMEM((2, NA_Q, NA_K), F32), pltpu.VMEM((2, NA_Q, NA_K), qkv.dtype)],
        compiler_params=_params("arbitrary", "arbitrary"),
    )(qkv, qkv, qkv, bias)


def _na_bwd(qkv, do, bias):
    seq = qkv.shape[0]
    rows = seq // GRID_W
    nb = rows // NA_QROWS

    def body(q_ref, k_ref, v_ref, do_ref, bias_ref, dq_ref, dk_ref, dv_ref, dbias_ref, s_scr, dp_scr, p_scr, ds_scr):
        b = pl.program_id(1)
        start, kind = _na_block(b, nb, rows)

        @pl.when(b == 0)
        def _():
            dk_ref[...] = jnp.zeros_like(dk_ref)
            dv_ref[...] = jnp.zeros_like(dv_ref)
            dbias_ref[...] = jnp.zeros_like(dbias_ref)

        def block(static_kind):
            q2 = q_ref[...] * NA_SCALE
            kw = k_ref[pl.ds(start, NA_K), :]
            vw = v_ref[pl.ds(start, NA_K), :]
            do2 = do_ref[...].astype(q2.dtype)
            heads = _na_heads()
            col = lambda k2: slice(k2 * LANES, (k2 + 1) * LANES)
            zero = jnp.zeros((NA_CHUNK, LANES), p_scr.dtype)
            for hh in range(2):
                s_scr[hh] = _nt(jnp.where(heads[hh], q2, jnp.zeros_like(q2)), kw)
                dp_scr[hh] = _nt(jnp.where(heads[hh], do2, jnp.zeros_like(do2)), vw)
            for hh in range(2):
                for i in range(NA_Q // NA_CHUNK):
                    r = slice(i * NA_CHUNK, (i + 1) * NA_CHUNK)
                    ps = _na_softmax_pieces(s_scr, bias_ref, hh, static_kind, i)
                    dps = [None if p is None else dp_scr[hh, r, col(k2)] for k2, p in enumerate(ps)]
                    pdp = functools.reduce(jnp.add, [p * dp for p, dp in zip(ps, dps) if p is not None])
                    rowsum = jnp.sum(pdp, axis=-1, keepdims=True)
                    for k2, (t, q, _) in enumerate(_na_pieces(static_kind, i)):
                        if ps[k2] is None:
                            p_scr[hh, r, col(k2)] = zero
                            ds_scr[hh, r, col(k2)] = zero
                            continue
                        ds = ps[k2] * (dps[k2] - rowsum)
                        dbias_ref[hh, t, q, :] += ds
                        p_scr[hh, r, col(k2)] = ps[k2].astype(p_scr.dtype)
                        ds_scr[hh, r, col(k2)] = ds.astype(ds_scr.dtype)
            dq_ref[...] = jnp.where(heads[0], _mm(ds_scr[0], kw), _mm(ds_scr[1], kw)) * NA_SCALE
            dk_ref[pl.ds(start, NA_K), :] += jnp.where(heads[0], _tn(ds_scr[0], q2), _tn(ds_scr[1], q2))
            dv_ref[pl.ds(start, NA_K), :] += jnp.where(heads[0], _tn(p_scr[0], do2), _tn(p_scr[1], do2))

        for static_kind in range(3):
            pl.when(kind == static_kind)(functools.partial(block, static_kind))

    return pl.pallas_call(
        body, name="na_bwd", grid=(NA_HEADS // 2, nb),
        in_specs=[pl.BlockSpec((NA_Q, LANES), lambda hp, b: (b, hp)),
                  pl.BlockSpec((seq, LANES), lambda hp, b: (0, 4 + hp)),
                  pl.BlockSpec((seq, LANES), lambda hp, b: (0, 8 + hp)),
                  pl.BlockSpec((NA_Q, LANES), lambda hp, b: (b, hp)),
                  pl.BlockSpec((2, NA_TAB, GRID_W, LANES), lambda hp, b: (hp, 0, 0, 0))],
        out_specs=[pl.BlockSpec((NA_Q, LANES), lambda hp, b: (b, hp)),
                   pl.BlockSpec((seq, LANES), lambda hp, b: (0, hp)),
                   pl.BlockSpec((seq, LANES), lambda hp, b: (0, hp)),
                   pl.BlockSpec((2, NA_TAB, GRID_W, LANES), lambda hp, b: (hp, 0, 0, 0))],
        out_shape=[jax.ShapeDtypeStruct((seq, D_NA), F32), jax.ShapeDtypeStruct((seq, D_NA), F32),
                   jax.ShapeDtypeStruct((seq, D_NA), F32),
                   jax.ShapeDtypeStruct((NA_HEADS, NA_TAB, GRID_W, LANES), F32)],
        scratch_shapes=[pltpu.VMEM((2, NA_Q, NA_K), F32), pltpu.VMEM((2, NA_Q, NA_K), F32),
                        pltpu.VMEM((2, NA_Q, NA_K), qkv.dtype), pltpu.VMEM((2, NA_Q, NA_K), qkv.dtype)],
        compiler_params=_params("arbitrary", "arbitrary"),
    )(qkv, qkv, qkv, do, bias)


def _local_step(x2, p2, tgt, g_pre, g_post, w_in_g, ssm, w_glu, b_glu, rpb, w_out, w_ple_g, g_ple, w_pg):
    seq = x2.shape[0]
    nc = seq // SSM_CHUNK

    (pw, bb, cc, dd), ssm_vjp = jax.vjp(_ssm_tables, *ssm)
    bias, bias_vjp = jax.vjp(_na_table, rpb)

    u, zs, qkv, zn, hn = _fwd_in(x2, g_pre, w_in_g)
    u_p = _to_pairs(u, nc)
    y_p, hin = _ssm_fwd(u_p, pw, bb, cc, dd)
    yssm = _from_pairs(y_p, nc)
    o = _na_fwd(qkv, bias)
    dyssm, dzs, do, dzn, dh1, sq, d_wglu, d_bglu, d_wout, d_wple, d_wpg, d_gpost, d_gple = _mid(
        yssm, zs, o, zn, x2, p2, tgt, w_glu, b_glu, w_out, g_post, w_ple_g, g_ple, w_pg)
    dq, dk, dv, dbias = _na_bwd(qkv, do, bias)
    (d_rpb,) = bias_vjp(dbias)
    du_p, dpw, dbb, dcc, ddd = _ssm_bwd(u_p, _to_pairs(dyssm, nc), hin, pw, bb, cc, dd)
    d_ssm = ssm_vjp((dpw, dbb, dcc, ddd))
    du = _from_pairs(du_p, nc)
    dx, d_win, d_gpre = _bwd_in(du, dzs, dq, dk, dv, dzn, hn, x2, dh1, g_pre, w_in_g)
    return sq, dx, dict(norm_pre=d_gpre, norm_post=d_gpost, w_in=d_win, ssm=d_ssm, w_glu=d_wglu, b_glu=d_bglu,
                        na_rpb=d_rpb, w_out=d_wout, w_ple=d_wple, ple_norm=d_gple, w_ple_gate=d_wpg)


def _place():
    return lax.axis_index("x"), lax.axis_index("y"), lax.axis_index("c")


def _all_gather(shard):
    m_per, n = shard.shape

    def body(x_ref, out_ref, send_sems, recv_sems, local_sem):
        x, y, c = _place()
        me, sibling = (x, y, c), (x, y, 1 - c)
        chips = [(1 - x, y), (x, 1 - y), (1 - x, 1 - y)]

        def rows(px, py, pc):
            return out_ref.at[pl.ds((4 * px + 2 * py + pc) * m_per, m_per), :]

        def copy(k, block, to, src=None):
            return pltpu.make_async_remote_copy(
                src_ref=rows(*block) if src is None else src, dst_ref=rows(*block),
                send_sem=send_sems.at[k], recv_sem=recv_sems.at[k], device_id=to, device_id_type=MESH)

        mine = pltpu.make_async_copy(x_ref, rows(*me), local_sem)
        mine.start()
        first = [copy(0, me, sibling, src=x_ref)]
        first += [copy(1 + j, me, (*chip, c), src=x_ref) for j, chip in enumerate(chips)]
        for cp in first:
            cp.start()
        passed = [copy(4 + j, (*chip, c), sibling) for j, chip in enumerate(chips)]
        for j, chip in enumerate(chips):
            copy(1 + j, (*chip, c), me).wait_recv()
            passed[j].start()
        copy(0, sibling, me).wait_recv()
        for j, chip in enumerate(chips):
            copy(4 + j, (*chip, 1 - c), me).wait_recv()
        for cp in first + passed:
            cp.wait_send()
        mine.wait()

    return pl.pallas_call(
        body, name="all_gather",
        out_shape=jax.ShapeDtypeStruct((N_DEV * m_per, n), shard.dtype),
        in_specs=[_whole_vmem()], out_specs=_whole_vmem(),
        scratch_shapes=[pltpu.SemaphoreType.DMA((7,)), pltpu.SemaphoreType.DMA((7,)), pltpu.SemaphoreType.DMA],
        compiler_params=pltpu.CompilerParams(vmem_limit_bytes=VMEM_LIMIT),
    )(shard)


def _exchange_partials(part):
    _, r, n = part.shape

    def body(part_ref, land_ref, send_sems, recv_sems, local_sem):
        x, y, c = _place()
        me = 4 * x + 2 * y + c
        local = pltpu.make_async_copy(part_ref.at[me], land_ref.at[me], local_sem)
        local.start()
        copies = []
        for k in range(1, N_DEV):
            px, py, pc = x ^ ((k >> 2) & 1), y ^ ((k >> 1) & 1), c ^ (k & 1)
            peer = 4 * px + 2 * py + pc
            cp = pltpu.make_async_remote_copy(
                src_ref=part_ref.at[peer], dst_ref=land_ref.at[me],
                send_sem=send_sems.at[k - 1], recv_sem=recv_sems.at[k - 1],
                device_id=(px, py, pc), device_id_type=MESH)
            cp.start()
            copies.append(cp)
        for cp in copies:
            cp.wait_recv()
        for cp in copies:
            cp.wait_send()
        local.wait()

    anyspace = pl.BlockSpec(memory_space=pl.ANY)
    return pl.pallas_call(
        body, name="exchange_partials",
        out_shape=jax.ShapeDtypeStruct(part.shape, part.dtype),
        in_specs=[anyspace], out_specs=anyspace,
        scratch_shapes=[pltpu.SemaphoreType.DMA((N_DEV - 1,)), pltpu.SemaphoreType.DMA((N_DEV - 1,)),
                        pltpu.SemaphoreType.DMA],
    )(part)


def _pair_exchange(part):
    _, r, n = part.shape

    def body(part_ref, land_ref, send_sems, recv_sems):
        x, y, c = _place()
        copies = []
        for q in range(4):
            cp = pltpu.make_async_remote_copy(
                src_ref=part_ref.at[2 * q + (1 - c)], dst_ref=land_ref.at[q],
                send_sem=send_sems.at[q], recv_sem=recv_sems.at[q],
                device_id=(x, y, 1 - c), device_id_type=MESH)
            cp.start()
            copies.append(cp)
        for cp in copies:
            cp.wait_recv()
        for cp in copies:
            cp.wait_send()

    anyspace = pl.BlockSpec(memory_space=pl.ANY)
    return pl.pallas_call(
        body, name="pair_exchange",
        out_shape=jax.ShapeDtypeStruct((4, r, n), part.dtype),
        in_specs=[anyspace], out_specs=anyspace,
        scratch_shapes=[pltpu.SemaphoreType.DMA((4,)), pltpu.SemaphoreType.DMA((4,))],
    )(part)


def _pair_sum(part, land):
    _, r, n = part.shape
    tile = 256
    assert r % tile == 0

    def body(part_ref, land_ref, out_ref):
        c = lax.axis_index("c")
        for q in range(4):
            out_ref[q] = (part_ref[q, c] + land_ref[q]).astype(BF16)

    return pl.pallas_call(
        body, name="pair_sum", grid=(r // tile,),
        in_specs=[pl.BlockSpec((4, 2, tile, n), lambda i: (0, 0, i, 0)), pl.BlockSpec((4, tile, n), lambda i: (0, i, 0))],
        out_specs=pl.BlockSpec((4, tile, n), lambda i: (0, i, 0)),
        out_shape=jax.ShapeDtypeStruct((4, r, n), BF16),
        compiler_params=_params("arbitrary"),
    )(part.reshape(4, 2, r, n), land)


def _chip_exchange(chip_part):
    _, r, n = chip_part.shape

    def body(part_ref, land_ref, send_sems, recv_sems, local_sem):
        x, y, c = _place()
        mine = 2 * x + y
        local = pltpu.make_async_copy(part_ref.at[mine], land_ref.at[mine], local_sem)
        local.start()
        copies = []
        for k in range(1, 4):
            px, py = x ^ (k >> 1), y ^ (k & 1)
            cp = pltpu.make_async_remote_copy(
                src_ref=part_ref.at[2 * px + py], dst_ref=land_ref.at[mine],
                send_sem=send_sems.at[k - 1], recv_sem=recv_sems.at[k - 1],
                device_id=(px, py, c), device_id_type=MESH)
            cp.start()
            copies.append(cp)
        for cp in copies:
            cp.wait_recv()
        for cp in copies:
            cp.wait_send()
        local.wait()

    anyspace = pl.BlockSpec(memory_space=pl.ANY)
    return pl.pallas_call(
        body, name="chip_exchange",
        out_shape=jax.ShapeDtypeStruct(chip_part.shape, chip_part.dtype),
        in_specs=[anyspace], out_specs=anyspace,
        scratch_shapes=[pltpu.SemaphoreType.DMA((3,)), pltpu.SemaphoreType.DMA((3,)), pltpu.SemaphoreType.DMA],
    )(chip_part)


def _sum_slots(land):
    slots, r, n = land.shape
    tile = next(t for t in (512, 256, 128) if r % t == 0)

    def body(land_ref, out_ref):
        acc = land_ref[0].astype(F32)
        for j in range(1, slots):
            acc = acc + land_ref[j].astype(F32)
        out_ref[...] = acc

    return pl.pallas_call(
        body, name="sum_slots", grid=(r // tile,),
        in_specs=[pl.BlockSpec((slots, tile, n), lambda i: (0, i, 0))],
        out_specs=pl.BlockSpec((tile, n), lambda i: (i, 0)),
        out_shape=jax.ShapeDtypeStruct((r, n), F32),
        compiler_params=_params("arbitrary"),
    )(land)


def _adamw(w, g, m, v):
    r, c = w.shape
    tile = 256 if r % 256 == 0 else r

    def body(w_ref, g_ref, m_ref, v_ref, d_ref, nm_ref, nv_ref):
        g = g_ref[...]
        nm = ADAM_B1 * m_ref[...] + (1.0 - ADAM_B1) * g
        nv = ADAM_B2 * v_ref[...] + (1.0 - ADAM_B2) * (g * g)
        m_hat = nm / (1.0 - ADAM_B1 ** ADAM_STEP)
        v_hat = nv / (1.0 - ADAM_B2 ** ADAM_STEP)
        d_ref[...] = -ADAM_LR * (m_hat / (jnp.sqrt(v_hat) + ADAM_EPS) + ADAM_WD * w_ref[...])
        nm_ref[...] = nm
        nv_ref[...] = nv

    spec = pl.BlockSpec((tile, c), lambda i: (i, 0))
    out = jax.ShapeDtypeStruct((r, c), F32)
    return pl.pallas_call(
        body, name="adamw", grid=(r // tile,), in_specs=[spec] * 4, out_specs=[spec] * 3,
        out_shape=[out, out, out], compiler_params=_params("arbitrary"),
    )(w, g, m, v)


_SLAB = 8 * LANES


def _flat_rows(a):
    flat = a.reshape(-1)
    pad = (-flat.shape[0]) % _SLAB
    if pad:
        flat = jnp.concatenate([flat, jnp.zeros((pad,), flat.dtype)])
    return flat.reshape(-1, LANES)


def _pack(arrays):
    slabs = [_flat_rows(a) for a in arrays]
    return jnp.concatenate(slabs, axis=0), [s.shape[0] for s in slabs]


def _unpack(packed, like, rows):
    out, at = [], 0
    for a, r in zip(like, rows):
        out.append(packed[at:at + r].reshape(-1)[:a.size].reshape(a.shape))
        at += r
    return out


SMALL = ('norm_pre', 'norm_post', 'ssm_a_re', 'ssm_a_im', 'ssm_log_dt', 'ssm_b_re', 'ssm_b_im', 'ssm_c_re',
         'ssm_c_im', 'ssm_d', 'b_glu', 'na_rpb', 'ple_norm')
BIG = ('w_in', 'w_glu', 'w_out', 'w_ple', 'w_ple_gate')
ORDER = ('norm_pre', 'norm_post', 'w_in', 'ssm_a_re', 'ssm_a_im', 'ssm_log_dt', 'ssm_b_re', 'ssm_b_im', 'ssm_c_re',
         'ssm_c_im', 'ssm_d', 'w_glu', 'b_glu', 'na_rpb', 'w_out', 'w_ple', 'ple_norm', 'w_ple_gate')


def kernel(x, p, norm_pre, norm_post, w_in, ssm_a_re, ssm_a_im, ssm_log_dt, ssm_b_re, ssm_b_im, ssm_c_re, ssm_c_im, ssm_d, w_glu, b_glu, na_rpb, w_out, w_ple, ple_norm, w_ple_gate, loss_target, m_norm_pre, m_norm_post, m_w_in, m_ssm_a_re, m_ssm_a_im, m_ssm_log_dt, m_ssm_b_re, m_ssm_b_im, m_ssm_c_re, m_ssm_c_im, m_ssm_d, m_w_glu, m_b_glu, m_na_rpb, m_w_out, m_w_ple, m_ple_norm, m_w_ple_gate, v_norm_pre, v_norm_post, v_w_in, v_ssm_a_re, v_ssm_a_im, v_ssm_log_dt, v_ssm_b_re, v_ssm_b_im, v_ssm_c_re, v_ssm_c_im, v_ssm_d, v_w_glu, v_b_glu, v_na_rpb, v_w_out, v_w_ple, v_ple_norm, v_w_ple_gate):
    args = dict(locals())
    weights = {n: args[n] for n in ORDER}
    mom_m = {n: args["m_" + n] for n in ORDER}
    mom_v = {n: args["v_" + n] for n in ORDER}

    shards = [weights[n][0].astype(BF16) for n in BIG]
    packed, shard_rows = _pack(shards)
    gathered = _all_gather(packed).reshape(N_DEV, -1, LANES)
    full = []
    at = 0
    for s, r in zip(shards, shard_rows):
        full.append(gathered[:, at:at + r].reshape((N_DEV,) + s.shape))
        at += r
    w_in_g, w_glu_g, w_out_g, w_ple_g, w_pg_g = full
    w_glu_f = w_glu_g.reshape(D_SSM, D_SSM)
    w_out_f = w_out_g.reshape(D_MODEL, D_MODEL)
    w_pg_f = w_pg_g.reshape(D_MODEL, D_MODEL)

    ssm = tuple(weights[n][0] for n in ('ssm_a_re', 'ssm_a_im', 'ssm_log_dt', 'ssm_b_re', 'ssm_b_im',
                                        'ssm_c_re', 'ssm_c_im', 'ssm_d'))
    sq, dx, grads = _local_step(x[0], p[0, 0], loss_target[0], norm_pre, norm_post, w_in_g, ssm, w_glu_f, b_glu,
                                na_rpb[0], w_out_f, w_ple_g, ple_norm, w_pg_f)
    loss_local = (0.5 / D_MODEL * jnp.sum(sq)).reshape(1)

    local = dict(norm_pre=grads['norm_pre'], norm_post=grads['norm_post'], b_glu=grads['b_glu'],
                 na_rpb=grads['na_rpb'][None], ple_norm=grads['ple_norm'])
    for n, g in zip(('ssm_a_re', 'ssm_a_im', 'ssm_log_dt', 'ssm_b_re', 'ssm_b_im', 'ssm_c_re', 'ssm_c_im', 'ssm_d'),
                    grads['ssm']):
        local[n] = g[None]

    big_parts = [grads['w_in'],
                 grads['w_glu'].reshape(N_DEV, D_SSM // N_DEV, D_SSM),
                 grads['w_out'].reshape(N_DEV, D_MODEL // N_DEV, D_MODEL),
                 grads['w_ple'],
                 grads['w_ple_gate'].reshape(N_DEV, D_MODEL // N_DEV, D_MODEL)]
    small_flat, small_rows = _pack([local[n] for n in SMALL] + [loss_local])
    pad = (-small_flat.shape[0]) % (128 * N_DEV)
    if pad:
        small_flat = jnp.concatenate([small_flat, jnp.zeros((pad, LANES), F32)], axis=0)
    small_per = small_flat.shape[0] // N_DEV
    part = jnp.concatenate([bp.reshape(N_DEV, -1, LANES) for bp in big_parts], axis=1)
    reduced = _sum_slots(_chip_exchange(_pair_sum(part, _pair_exchange(part))))
    big_grads = {}
    at = 0
    for n, bp in zip(BIG, big_parts):
        r = bp[0].size // LANES
        big_grads[n] = reduced[at:at + r].reshape(bp.shape[1:])
        at += r
    small_all = _all_gather(_sum_slots(_exchange_partials(small_flat.reshape(N_DEV, small_per, LANES))))

    outs = {}
    for n in BIG:
        w2 = weights[n][0]
        d, nm, nv = _adamw(w2, big_grads[n], mom_m[n][0], mom_v[n][0])
        outs[n] = (big_grads[n][None], d[None], nm[None], nv[None])
    like = [weights[n] for n in SMALL]
    w_s, rows_s = _pack(like)
    m_s, _ = _pack([mom_m[n] for n in SMALL])
    v_s, _ = _pack([mom_v[n] for n in SMALL])
    g_s = small_all[:w_s.shape[0]]
    loss = small_all[w_s.shape[0], 0]
    d_s, nm_s, nv_s = _adamw(w_s, g_s, m_s, v_s)
    unpacked = [_unpack(a, like, rows_s) for a in (g_s, d_s, nm_s, nv_s)]
    for i, n in enumerate(SMALL):
        outs[n] = tuple(u[i] for u in unpacked)

    return (loss, dx[None], *[outs[n][0] for n in ORDER], *[outs[n][1] for n in ORDER],
            *[outs[n][2] for n in ORDER], *[outs[n][3] for n in ORDER])
```

```python
import functools

import jax
import jax.numpy as jnp
from jax import lax
from jax.experimental import pallas as pl
from jax.experimental.pallas import tpu as pltpu

F32 = jnp.float32
BF16 = jnp.bfloat16
HIGHEST = lax.Precision.HIGHEST

D_MODEL = 1024
D_PLE = 256
GRID_W = 64
D_SSM = 512
SSM_GROUP = 16
N_GROUPS = 32
N_PAIRS = 16
SSM_STATE = 64
D_NA = 512
NA_HEADS = 8
NA_HEAD_DIM = 64
NA_ROWS = 8
NA_COLS = 16
D_IN_PROJ = 3072
EPS = 1e-6
N_DEV = 8
SHARD_IN = D_IN_PROJ // N_DEV
LANES = 128
SSM_CHUNK = 16
TOK_TILE = 256
NA_QROWS = 4
NA_KROWS = 12
NEG = -1e30
VMEM_LIMIT = 56 * 1024 * 1024

ADAM_LR = 0.001
ADAM_B1 = 0.9
ADAM_B2 = 0.999
ADAM_EPS = 1e-08
ADAM_WD = 0.01
ADAM_STEP = 10

MESH = pl.DeviceIdType.MESH


def _params(*sem):
    return pltpu.CompilerParams(dimension_semantics=sem or None, vmem_limit_bytes=VMEM_LIMIT)


def _whole_vmem():
    return pl.BlockSpec(memory_space=pltpu.VMEM)


def _nt(a, b):
    return lax.dot_general(a, b, (((1,), (1,)), ((), ())), preferred_element_type=F32)


def _tn(a, b):
    return lax.dot_general(a, b, (((0,), (0,)), ((), ())), preferred_element_type=F32)


def _mm(a, b):
    return jnp.dot(a, b, preferred_element_type=F32)


def _sigmoid(x):
    return 1.0 / (1.0 + jnp.exp(-x))


_GELU_C = 0.7978845608028654


def _gelu(x):
    return 0.5 * x * (1.0 + jnp.tanh(_GELU_C * (x + 0.044715 * x * x * x)))


def _gelu_grad(x):
    th = jnp.tanh(_GELU_C * (x + 0.044715 * x * x * x))
    return 0.5 * (1.0 + th) + 0.5 * x * (1.0 - th * th) * _GELU_C * (1.0 + 3.0 * 0.044715 * x * x)


def _ssm_tables(a_re, a_im, log_dt, b_re, b_im, c_re, c_im, d):
    T, P, H = SSM_CHUNK, SSM_STATE, SSM_GROUP
    dt = jnp.exp(log_dt)[..., None]
    xr = dt * a_re
    xi = dt * a_im
    mag = jnp.exp(xr)
    lr = mag * jnp.cos(xi)
    li = mag * jnp.sin(xi)
    den = a_re * a_re + a_im * a_im
    cr = ((lr - 1.0) * a_re + li * a_im) / den
    ci = (li * a_re - (lr - 1.0) * a_im) / den
    bbr = cr[..., None] * b_re - ci[..., None] * b_im
    bbi = cr[..., None] * b_im + ci[..., None] * b_re
    kk = jnp.arange(T + 1, dtype=F32)[:, None, None, None]
    pm = jnp.exp(kk * xr)
    pw = jnp.stack([pm * jnp.cos(kk * xi), pm * jnp.sin(kk * xi)], axis=2)
    pw = pw.reshape(T + 1, 2, 2, N_PAIRS, 2 * P).transpose(3, 0, 1, 2, 4).reshape(N_PAIRS, T + 1, 8 * P)
    pw = jnp.concatenate([pw, jnp.zeros((N_PAIRS, 24 - (T + 1), 8 * P), F32)], axis=1)
    eye2 = jnp.eye(2, dtype=F32)

    def expand(t):
        t = t.transpose(2, 0, 1, 3, 4, 5)
        t = t[:, :, :, :, :, None, :] * eye2[None, None, None, :, None, :, None]
        return t.reshape(N_PAIRS, 4, 2 * H, 2 * P)

    bb = expand(jnp.stack([bbr, bbi], axis=1).reshape(2, 2, N_PAIRS, 2, P, H).transpose(0, 1, 2, 3, 5, 4))
    cc = expand(jnp.stack([c_re, c_im], axis=1).reshape(2, 2, N_PAIRS, 2, H, P))
    dd = d.reshape(N_PAIRS, 2 * H)[:, :, None] * jnp.eye(2 * H, dtype=F32)[None]
    dd = jnp.concatenate([dd, jnp.zeros((N_PAIRS, 2 * H, LANES - 2 * H), F32)], axis=2)
    return pw, bb, cc, dd


NA_TAB = 2 * NA_ROWS


def _na_table(rpb):
    qc = jnp.arange(GRID_W)[:, None, None]
    kc = (jnp.arange(2 * GRID_W) % GRID_W)[None, :, None]
    dc = jnp.arange(2 * NA_COLS - 1)[None, None, :]
    cstart = jnp.clip(qc - NA_COLS // 2, 0, GRID_W - NA_COLS)
    csel = ((kc >= cstart) & (kc < cstart + NA_COLS) & (kc - qc + NA_COLS - 1 == dc)).astype(F32)
    col_ok = jnp.sum(csel, axis=-1) > 0.5
    part = jnp.einsum('hrc,qmc->hrqm', rpb, csel, precision=HIGHEST)
    zero = jnp.zeros_like(part[:, :1])
    odd = jnp.arange(2 * GRID_W) >= GRID_W
    tab = jnp.where(odd, jnp.concatenate([part, zero], axis=1), jnp.concatenate([zero, part], axis=1))
    return jnp.where(col_ok, tab, NEG)


def _fwd_in(x2, g_pre, w_in_g):
    seq = x2.shape[0]

    def body(x_ref, g_ref, w_ref, u_ref, zs_ref, qkv_ref, zn_ref, hn_ref):
        x = x_ref[...]
        r = lax.rsqrt(jnp.mean(x * x, axis=-1, keepdims=True) + EPS)
        hn = (x * r * g_ref[...]).astype(BF16)
        hn_ref[...] = hn
        for j in range(N_DEV):
            pj = _mm(hn, w_ref[j])
            for i in range(SHARD_IN // LANES):
                blk = (SHARD_IN // LANES) * j + i
                piece = pj[:, i * LANES:(i + 1) * LANES]
                if blk < 4:
                    u_ref[:, blk * LANES:(blk + 1) * LANES] = piece
                elif blk < 8:
                    zs_ref[:, (blk - 4) * LANES:(blk - 3) * LANES] = piece
                elif blk < 20:
                    qkv_ref[:, (blk - 8) * LANES:(blk - 7) * LANES] = piece.astype(BF16)
                else:
                    zn_ref[:, (blk - 20) * LANES:(blk - 19) * LANES] = piece

    tok = lambda w: pl.BlockSpec((TOK_TILE, w), lambda i: (i, 0))
    return pl.pallas_call(
        body, name="fwd_in", grid=(seq // TOK_TILE,),
        in_specs=[tok(D_MODEL), pl.BlockSpec((1, D_MODEL), lambda i: (0, 0)), _whole_vmem()],
        out_specs=[tok(D_SSM), tok(D_SSM), tok(3 * D_NA), tok(D_NA), tok(D_MODEL)],
        out_shape=[jax.ShapeDtypeStruct((seq, D_SSM), F32), jax.ShapeDtypeStruct((seq, D_SSM), F32),
                   jax.ShapeDtypeStruct((seq, 3 * D_NA), BF16), jax.ShapeDtypeStruct((seq, D_NA), F32),
                   jax.ShapeDtypeStruct((seq, D_MODEL), BF16)],
        compiler_params=_params("arbitrary"),
    )(x2, g_pre, w_in_g)


def _mid(yssm, zs, o, zn, x2, p2, tgt, w_glu, b_glu, w_out, g_post, w_ple_g, g_ple, w_pg):
    seq = x2.shape[0]

    def body(yssm_ref, zs_ref, o_ref, zn_ref, x_ref, p_ref, tgt_ref, wglu_ref, bglu_ref, wout_ref, gpost_ref,
             wple_ref, gple_ref, wpg_ref,
             dyssm_ref, dzs_ref, do_ref, dzn_ref, dh1_ref, loss_ref, dwglu_ref, dbglu_ref, dwout_ref, dwple_ref,
             dwpg_ref, dgpost_ref, dgple_ref):
        @pl.when(pl.program_id(0) == 0)
        def _():
            for ref in (loss_ref, dwglu_ref, dbglu_ref, dwout_ref, dwple_ref, dwpg_ref, dgpost_ref, dgple_ref):
                ref[...] = jnp.zeros_like(ref)

        yv = yssm_ref[...]
        g1 = _gelu(yv)
        g1b = g1.astype(BF16)
        sg = _sigmoid(_mm(g1b, wglu_ref[...]) + bglu_ref[...])
        zs_v = zs_ref[...]
        s_zs = _sigmoid(zs_v)
        g2 = g1 * sg
        zn_v = zn_ref[...]
        s_zn = _sigmoid(zn_v)
        o_v = o_ref[...]
        cat = jnp.concatenate([g2 * (zs_v * s_zs), o_v * (zn_v * s_zn)], axis=1).astype(BF16)
        mix = _mm(cat, wout_ref[...])
        r2 = lax.rsqrt(jnp.mean(mix * mix, axis=-1, keepdims=True) + EPS)
        n2 = mix * r2
        gpost = gpost_ref[...]
        h1 = x_ref[...] + n2 * gpost
        pb = p_ref[...].astype(BF16)
        epre = jnp.concatenate([_mm(pb, wple_ref[j]) for j in range(N_DEV)], axis=1)
        r3 = lax.rsqrt(jnp.mean(epre * epre, axis=-1, keepdims=True) + EPS)
        n3 = epre * r3
        gple = gple_ref[...]
        e = n3 * gple
        h1b = h1.astype(BF16)
        gate = _sigmoid(_mm(h1b, wpg_ref[...]))
        diff = h1 + gate * e - tgt_ref[...]
        loss_ref[...] += jnp.sum(diff * diff, axis=0, keepdims=True)

        dy = diff * (1.0 / D_MODEL)
        dgp = (dy * e * gate * (1.0 - gate)).astype(BF16)
        de = dy * gate
        dh1 = dy + _nt(dgp, wpg_ref[...])
        dh1_ref[...] = dh1
        dwpg_ref[...] += _tn(h1b, dgp)
        dgple_ref[...] += jnp.sum(de * n3, axis=0, keepdims=True)
        dn3 = de * gple
        depre = (r3 * (dn3 - n3 * jnp.mean(dn3 * n3, axis=-1, keepdims=True))).astype(BF16)
        for j in range(N_DEV):
            dwple_ref[j] += _tn(pb, depre[:, j * LANES:(j + 1) * LANES])
        dgpost_ref[...] += jnp.sum(dh1 * n2, axis=0, keepdims=True)
        dn2 = dh1 * gpost
        dmix = (r2 * (dn2 - n2 * jnp.mean(dn2 * n2, axis=-1, keepdims=True))).astype(BF16)
        dcat = _nt(dmix, wout_ref[...])
        dwout_ref[...] += _tn(cat, dmix)

        dys = dcat[:, :D_SSM]
        dyn = dcat[:, D_SSM:]
        dg2 = dys * (zs_v * s_zs)
        dzs_ref[...] = dys * g2 * (s_zs * (1.0 + zs_v * (1.0 - s_zs)))
        dt = dg2 * g2 * (1.0 - sg)
        dtb = dt.astype(BF16)
        dg1 = dg2 * sg + _nt(dtb, wglu_ref[...])
        dwglu_ref[...] += _tn(g1b, dtb)
        dbglu_ref[...] += jnp.sum(dt, axis=0, keepdims=True)
        dyssm_ref[...] = dg1 * _gelu_grad(yv)
        do_ref[...] = dyn * (zn_v * s_zn)
        dzn_ref[...] = dyn * o_v * (s_zn * (1.0 + zn_v * (1.0 - s_zn)))

    tok = lambda w: pl.BlockSpec((TOK_TILE, w), lambda i: (i, 0))
    row = lambda w: pl.BlockSpec((1, w), lambda i: (0, 0))
    vm = _whole_vmem()
    half = jax.ShapeDtypeStruct((seq, D_SSM), F32)
    gain = jax.ShapeDtypeStruct((1, D_MODEL), F32)
    return pl.pallas_call(
        body, name="mid", grid=(seq // TOK_TILE,),
        in_specs=[tok(D_SSM), tok(D_SSM), tok(D_NA), tok(D_NA), tok(D_MODEL), tok(D_PLE), tok(D_MODEL),
                  vm, row(D_SSM), vm, row(D_MODEL), vm, row(D_MODEL), vm],
        out_specs=[tok(D_SSM), tok(D_SSM), tok(D_NA), tok(D_NA), tok(D_MODEL), vm, vm, vm, vm, vm, vm, vm, vm],
        out_shape=[half, half, half, half, jax.ShapeDtypeStruct((seq, D_MODEL), F32), gain,
                   jax.ShapeDtypeStruct((D_SSM, D_SSM), F32), jax.ShapeDtypeStruct((1, D_SSM), F32),
                   jax.ShapeDtypeStruct((D_MODEL, D_MODEL), F32),
                   jax.ShapeDtypeStruct((N_DEV, D_PLE, LANES), F32),
                   jax.ShapeDtypeStruct((D_MODEL, D_MODEL), F32), gain, gain],
        compiler_params=_params("arbitrary"),
    )(yssm, zs, o, zn, x2, p2, tgt, w_glu, b_glu, w_out, g_post, w_ple_g, g_ple, w_pg)


def _bwd_in(du, dzs, dq, dk, dv, dzn, hn, x2, dh1, g_pre, w_in_g):
    seq = x2.shape[0]

    def body(du_ref, dzs_ref, dq_ref, dk_ref, dv_ref, dzn_ref, hn_ref, x_ref, dh1_ref, g_ref, w_ref,
             dx_ref, dw_ref, dg_ref, dproj_ref):
        @pl.when(pl.program_id(0) == 0)
        def _():
            dw_ref[...] = jnp.zeros_like(dw_ref)
            dg_ref[...] = jnp.zeros_like(dg_ref)

        for k, ref in enumerate((du_ref, dzs_ref, dq_ref, dk_ref, dv_ref, dzn_ref)):
            dproj_ref[:, k * D_SSM:(k + 1) * D_SSM] = ref[...].astype(BF16)
        hn = hn_ref[...]
        dhn = jnp.zeros((TOK_TILE, D_MODEL), F32)
        for j in range(N_DEV):
            dpj = dproj_ref[:, j * SHARD_IN:(j + 1) * SHARD_IN]
            dhn += _nt(dpj, w_ref[j])
            dw_ref[j] += _tn(hn, dpj)
        x = x_ref[...]
        r = lax.rsqrt(jnp.mean(x * x, axis=-1, keepdims=True) + EPS)
        n1 = x * r
        dg_ref[...] += jnp.sum(dhn * n1, axis=0, keepdims=True)
        dn1 = dhn * g_ref[...]
        dx_ref[...] = dh1_ref[...] + r * (dn1 - n1 * jnp.mean(dn1 * n1, axis=-1, keepdims=True))

    tok = lambda w: pl.BlockSpec((TOK_TILE, w), lambda i: (i, 0))
    vm = _whole_vmem()
    return pl.pallas_call(
        body, name="bwd_in", grid=(seq // TOK_TILE,),
        in_specs=[tok(D_SSM)] * 6 + [tok(D_MODEL), tok(D_MODEL), tok(D_MODEL),
                                      pl.BlockSpec((1, D_MODEL), lambda i: (0, 0)), vm],
        out_specs=[tok(D_MODEL), vm, vm],
        out_shape=[jax.ShapeDtypeStruct((seq, D_MODEL), F32),
                   jax.ShapeDtypeStruct((N_DEV, D_MODEL, SHARD_IN), F32),
                   jax.ShapeDtypeStruct((1, D_MODEL), F32)],
        scratch_shapes=[pltpu.VMEM((TOK_TILE, D_IN_PROJ), BF16)],
        compiler_params=_params("arbitrary"),
    )(du, dzs, dq, dk, dv, dzn, hn, x2, dh1, g_pre, w_in_g)


RELAYOUT_CHUNKS = 128
PAIR_W = 2 * SSM_GROUP
PAIRS_PER_BLOCK = LANES // PAIR_W
CHUNK_W = SSM_CHUNK * PAIR_W
PW_ROWS = 24


def _lane_window(lo, width):
    lane = lax.broadcasted_iota(jnp.int32, (1, LANES), 1)
    return (lane >= lo) & (lane < lo + width)


def _to_pairs(a, nc):
    ncb = min(RELAYOUT_CHUNKS, nc)

    def body(x_ref, out_ref):
        xs = [x_ref[pl.ds(s, ncb, stride=SSM_CHUNK), :] for s in range(SSM_CHUNK)]
        for a_ in range(PAIRS_PER_BLOCK):
            for v in range(SSM_CHUNK // PAIRS_PER_BLOCK):
                acc = None
                for i in range(PAIRS_PER_BLOCK):
                    shift = (PAIR_W * (i - a_)) % LANES
                    piece = xs[PAIRS_PER_BLOCK * v + i]
                    piece = pltpu.roll(piece, shift, axis=1) if shift else piece
                    acc = piece if acc is None else jnp.where(_lane_window(PAIR_W * i, PAIR_W), piece, acc)
                out_ref[a_, :, LANES * v:LANES * (v + 1)] = acc.astype(BF16)

    return pl.pallas_call(
        body, name="to_pairs", grid=(D_SSM // LANES, nc // ncb),
        in_specs=[pl.BlockSpec((ncb * SSM_CHUNK, LANES), lambda cl, cb: (cb, cl))],
        out_specs=pl.BlockSpec((PAIRS_PER_BLOCK, ncb, CHUNK_W), lambda cl, cb: (cl, cb, 0)),
        out_shape=jax.ShapeDtypeStruct((N_PAIRS, nc, CHUNK_W), BF16),
        compiler_params=_params("arbitrary", "arbitrary"),
    )(a)


def _from_pairs(a, nc):
    ncb = min(RELAYOUT_CHUNKS, nc)

    def body(y_ref, out_ref):
        for s in range(SSM_CHUNK):
            v, i = divmod(s, PAIRS_PER_BLOCK)
            acc = None
            for a_ in range(PAIRS_PER_BLOCK):
                shift = (PAIR_W * (a_ - i)) % LANES
                piece = y_ref[a_, :, LANES * v:LANES * (v + 1)]
                piece = pltpu.roll(piece, shift, axis=1) if shift else piece
                acc = piece if acc is None else jnp.where(_lane_window(PAIR_W * a_, PAIR_W), piece, acc)
            out_ref[pl.ds(s, ncb, stride=SSM_CHUNK), :] = acc

    return pl.pallas_call(
        body, name="from_pairs", grid=(D_SSM // LANES, nc // ncb),
        in_specs=[pl.BlockSpec((PAIRS_PER_BLOCK, ncb, CHUNK_W), lambda cl, cb: (cl, cb, 0))],
        out_specs=pl.BlockSpec((ncb * SSM_CHUNK, LANES), lambda cl, cb: (cb, cl)),
        out_shape=jax.ShapeDtypeStruct((nc * SSM_CHUNK, D_SSM), F32),
        compiler_params=_params("arbitrary", "arbitrary"),
    )(a)


def _boundary_scan(nc, pw_ref, src_ref, dst_ref, conj):
    nblk = nc // 8
    lr0, li0, lr1, li1 = (pw_ref[0, SSM_CHUNK:SSM_CHUNK + 1, LANES * q:LANES * (q + 1)] for q in range(4))
    if conj:
        li0, li1 = -li0, -li1

    def step(i, carry):
        hr0, hi0, hr1, hi1 = carry
        up = pl.multiple_of(i * 8, 8)
        dn = pl.multiple_of((nblk - 1 - i) * 8, 8)
        a_rows, b_rows = (dn, up) if conj else (up, dn)
        s_r0 = src_ref[pl.ds(a_rows, 8), 0:128]
        s_i0 = src_ref[pl.ds(a_rows, 8), 128:256]
        s_r1 = src_ref[pl.ds(b_rows, 8), 256:384]
        s_i1 = src_ref[pl.ds(b_rows, 8), 384:512]
        o_r0, o_i0, o_r1, o_i1 = [], [], [], []
        for k in range(8):
            ka = 7 - k if conj else k
            kb = k if conj else 7 - k
            o_r0.append((ka, hr0))
            o_i0.append((ka, hi0))
            o_r1.append((kb, hr1))
            o_i1.append((kb, hi1))
            hr0, hi0 = (lr0 * hr0 - li0 * hi0 + s_r0[ka:ka + 1], lr0 * hi0 + li0 * hr0 + s_i0[ka:ka + 1])
            hr1, hi1 = (lr1 * hr1 - li1 * hi1 + s_r1[kb:kb + 1], lr1 * hi1 + li1 * hr1 + s_i1[kb:kb + 1])

        def rows(items):
            return jnp.concatenate([v for _, v in sorted(items, key=lambda kv: kv[0])], axis=0)

        dst_ref[pl.ds(a_rows, 8), 0:128] = rows(o_r0)
        dst_ref[pl.ds(a_rows, 8), 128:256] = rows(o_i0)
        dst_ref[pl.ds(b_rows, 8), 256:384] = rows(o_r1)
        dst_ref[pl.ds(b_rows, 8), 384:512] = rows(o_i1)
        return hr0, hi0, hr1, hi1

    z = jnp.zeros((1, LANES), F32)
    lax.fori_loop(0, nblk, step, (z, z, z, z))


def _pw_row(pw_ref, k, q):
    return pw_ref[0, k:k + 1, LANES * q:LANES * (q + 1)]


def _mm_f32(a, b, dims):
    return lax.dot_general(a, b, (dims, ((), ())), precision=HIGHEST, preferred_element_type=F32)


_POW_M = (lambda s: SSM_CHUNK - 1 - s, lambda s: s)
_POW_C = (lambda s: s + 1, lambda s: SSM_CHUNK - s)
_POW_K = (lambda s: s, lambda s: SSM_CHUNK - 1 - s)


def _chunk_matrices(pw_ref, bb_ref, cc_ref, dd_ref, m_scr, ct_scr, toep_scr, g_scr, kt_scr):
    blk = lambda s: slice(PAIR_W * s, PAIR_W * (s + 1))
    col = lambda q: slice(LANES * q, LANES * (q + 1))
    for d in range(2):
        bbr, bbi = bb_ref[0, 2 * d], bb_ref[0, 2 * d + 1]
        ccr, cci = cc_ref[0, 2 * d], cc_ref[0, 2 * d + 1]
        for s in range(SSM_CHUNK):
            pr, pi = _pw_row(pw_ref, _POW_M[d](s), 2 * d), _pw_row(pw_ref, _POW_M[d](s), 2 * d + 1)
            m_scr[blk(s), col(2 * d)] = (pr * bbr - pi * bbi).astype(m_scr.dtype)
            m_scr[blk(s), col(2 * d + 1)] = (pr * bbi + pi * bbr).astype(m_scr.dtype)
            pr, pi = _pw_row(pw_ref, _POW_C[d](s), 2 * d), _pw_row(pw_ref, _POW_C[d](s), 2 * d + 1)
            ct_scr[blk(s), col(2 * d)] = (ccr * pr - cci * pi).astype(ct_scr.dtype)
            ct_scr[blk(s), col(2 * d + 1)] = (-(ccr * pi + cci * pr)).astype(ct_scr.dtype)
            pr, pi = _pw_row(pw_ref, _POW_K[d](s), 2 * d), _pw_row(pw_ref, _POW_K[d](s), 2 * d + 1)
            g_scr[d, blk(s), 0:LANES] = ccr * pr - cci * pi
            g_scr[d, blk(s), LANES:2 * LANES] = -(ccr * pi + cci * pr)
        kt = _mm_f32(jnp.concatenate([bbr, bbi], axis=1), g_scr[d], ((1,), (1,)))
        if d == 0:
            kt = jnp.concatenate([kt[:, 0:LANES] + dd_ref[0], kt[:, LANES:]], axis=1)
        kt_scr[d] = kt
    lane = lax.broadcasted_iota(jnp.int32, (1, CHUNK_W), 1)
    for s in range(SSM_CHUNK):
        lo = PAIR_W * s
        hi = PAIR_W * (s + 1)
        fwd = kt_scr[0] if s == 0 else pltpu.roll(kt_scr[0], lo, axis=1)
        bwd = kt_scr[1] if hi == CHUNK_W else pltpu.roll(kt_scr[1], hi, axis=1)
        row = jnp.where(lane >= lo, fwd, 0.0) + jnp.where(lane < hi, bwd, 0.0)
        toep_scr[blk(s), :] = row.astype(toep_scr.dtype)


def _ssm_scratch(nc, mat_dtype):
    return [pltpu.VMEM((CHUNK_W, CHUNK_W), mat_dtype), pltpu.VMEM((CHUNK_W, CHUNK_W), mat_dtype),
            pltpu.VMEM((CHUNK_W, CHUNK_W), mat_dtype), pltpu.VMEM((2, CHUNK_W, 2 * LANES), F32),
            pltpu.VMEM((2, PAIR_W, CHUNK_W), F32), pltpu.VMEM((nc, CHUNK_W), F32), pltpu.VMEM((nc, CHUNK_W), F32)]


def _per_pair(*shape):
    return pl.BlockSpec((1,) + shape, lambda g: (g,) + (0,) * len(shape))


_TABLE_SPECS = lambda: [_per_pair(PW_ROWS, CHUNK_W), _per_pair(4, PAIR_W, LANES), _per_pair(4, PAIR_W, LANES),
                        _per_pair(PAIR_W, LANES)]


def _ssm_fwd(u, pw, bb, cc, dd):
    npair, nc, width = u.shape

    def body(u_ref, pw_ref, bb_ref, cc_ref, dd_ref, y_ref, hin_ref, m_scr, ct_scr, toep_scr, g_scr, kt_scr, s_scr, h_scr):
        _chunk_matrices(pw_ref, bb_ref, cc_ref, dd_ref, m_scr, ct_scr, toep_scr, g_scr, kt_scr)
        uv = u_ref[0]
        s_scr[...] = _mm(uv, m_scr[...])
        _boundary_scan(nc, pw_ref, s_scr, h_scr, conj=False)
        hin = h_scr[...]
        hin_ref[0] = hin
        y_ref[0] = _mm(uv, toep_scr[...]) + _nt(hin.astype(u.dtype), ct_scr[...])

    return pl.pallas_call(
        body, name="ssm_fwd", grid=(npair,),
        in_specs=[_per_pair(nc, width)] + _TABLE_SPECS(),
        out_specs=[_per_pair(nc, width), _per_pair(nc, width)],
        out_shape=[jax.ShapeDtypeStruct((npair, nc, width), F32), jax.ShapeDtypeStruct((npair, nc, width), F32)],
        scratch_shapes=_ssm_scratch(nc, u.dtype),
        compiler_params=_params("arbitrary"),
    )(u, pw, bb, cc, dd)


def _ssm_bwd(u, dy, hin, pw, bb, cc, dd):
    npair, nc, width = u.shape

    def body(u_ref, dy_ref, hin_ref, pw_ref, bb_ref, cc_ref, dd_ref, du_ref, dpw_ref, dbb_ref, dcc_ref, ddd_ref,
             m_scr, ct_scr, toep_scr, g_scr, kt_scr, dh_scr, ds_scr):
        _chunk_matrices(pw_ref, bb_ref, cc_ref, dd_ref, m_scr, ct_scr, toep_scr, g_scr, kt_scr)
        uv = u_ref[0]
        dyb = dy_ref[0]
        hin = hin_ref[0]
        dh_scr[...] = _mm(dyb, ct_scr[...])
        _boundary_scan(nc, pw_ref, dh_scr, ds_scr, conj=True)
        ds = ds_scr[...]
        dsb = ds.astype(u.dtype)
        du_ref[0] = _nt(dsb, m_scr[...]) + _nt(dyb, toep_scr[...])
        dm = _tn(uv, dsb)
        dct = _tn(dyb, hin.astype(u.dtype))
        dtoep = _tn(uv, dyb)

        dpw_ref[...] = jnp.zeros_like(dpw_ref)
        blk = lambda s: slice(PAIR_W * s, PAIR_W * (s + 1))
        col = lambda q: slice(LANES * q, LANES * (q + 1))

        def add_pw(k, q, val):
            dpw_ref[0, k:k + 1, col(q)] += jnp.sum(val, axis=0, keepdims=True)

        for d in range(2):
            g_r, g_i = ds[:, col(2 * d)], ds[:, col(2 * d + 1)]
            h_r, h_i = hin[:, col(2 * d)], hin[:, col(2 * d + 1)]
            add_pw(SSM_CHUNK, 2 * d, g_r * h_r + g_i * h_i)
            add_pw(SSM_CHUNK, 2 * d + 1, g_i * h_r - g_r * h_i)

        lane = lax.broadcasted_iota(jnp.int32, (1, CHUNK_W), 1)
        dkt0 = jnp.zeros((PAIR_W, CHUNK_W), F32)
        dkt1 = jnp.zeros((PAIR_W, CHUNK_W), F32)
        for s in range(SSM_CHUNK):
            lo = PAIR_W * s
            hi = PAIR_W * (s + 1)
            row = dtoep[blk(s), :]
            fwd = jnp.where(lane >= lo, row, 0.0)
            bwd = jnp.where(lane < hi, row, 0.0)
            dkt0 += fwd if s == 0 else pltpu.roll(fwd, CHUNK_W - lo, axis=1)
            dkt1 += bwd if hi == CHUNK_W else pltpu.roll(bwd, CHUNK_W - hi, axis=1)
        ddd_ref[0] = dkt0[:, 0:LANES]

        for d, dkt in enumerate((dkt0, dkt1)):
            bbr, bbi = bb_ref[0, 2 * d], bb_ref[0, 2 * d + 1]
            ccr, cci = cc_ref[0, 2 * d], cc_ref[0, 2 * d + 1]
            dbbcat = _mm_f32(dkt, g_scr[d], ((1,), (0,)))
            dg = _mm_f32(dkt, jnp.concatenate([bbr, bbi], axis=1), ((0,), (0,)))
            dbbr, dbbi = dbbcat[:, 0:LANES], dbbcat[:, LANES:]
            dccr = jnp.zeros((PAIR_W, LANES), F32)
            dcci = jnp.zeros((PAIR_W, LANES), F32)
            for s in range(SSM_CHUNK):
                k = _POW_M[d](s)
                pr, pi = _pw_row(pw_ref, k, 2 * d), _pw_row(pw_ref, k, 2 * d + 1)
                gr, gi = dm[blk(s), col(2 * d)], dm[blk(s), col(2 * d + 1)]
                dbbr += gr * pr + gi * pi
                dbbi += gi * pr - gr * pi
                add_pw(k, 2 * d, gr * bbr + gi * bbi)
                add_pw(k, 2 * d + 1, gi * bbr - gr * bbi)
                for k, gr, gi in ((_POW_C[d](s), dct[blk(s), col(2 * d)], dct[blk(s), col(2 * d + 1)]),
                                  (_POW_K[d](s), dg[blk(s), 0:LANES], dg[blk(s), LANES:])):
                    pr, pi = _pw_row(pw_ref, k, 2 * d), _pw_row(pw_ref, k, 2 * d + 1)
                    dccr += gr * pr - gi * pi
                    dcci += -(gr * pi + gi * pr)
                    add_pw(k, 2 * d, gr * ccr - gi * cci)
                    add_pw(k, 2 * d + 1, -(gr * cci + gi * ccr))
            dbb_ref[0, 2 * d] = dbbr
            dbb_ref[0, 2 * d + 1] = dbbi
            dcc_ref[0, 2 * d] = dccr
            dcc_ref[0, 2 * d + 1] = dcci

    return pl.pallas_call(
        body, name="ssm_bwd", grid=(npair,),
        in_specs=[_per_pair(nc, width), _per_pair(nc, width), _per_pair(nc, width)] + _TABLE_SPECS(),
        out_specs=[_per_pair(nc, width)] + _TABLE_SPECS(),
        out_shape=[jax.ShapeDtypeStruct((npair, nc, width), F32), jax.ShapeDtypeStruct(pw.shape, F32),
                   jax.ShapeDtypeStruct(bb.shape, F32), jax.ShapeDtypeStruct(cc.shape, F32),
                   jax.ShapeDtypeStruct(dd.shape, F32)],
        scratch_shapes=_ssm_scratch(nc, u.dtype),
        compiler_params=_params("arbitrary"),
    )(u, dy, hin, pw, bb, cc, dd)


NA_Q = NA_QROWS * GRID_W
NA_K = NA_KROWS * GRID_W
NA_SCALE = NA_HEAD_DIM ** -0.5


def _na_block(b, nb, rows):
    start = jnp.clip(NA_QROWS * b - NA_ROWS // 2, 0, rows - NA_KROWS) * GRID_W
    kind = jnp.where(b == 0, 0, jnp.where(b == nb - 1, 2, 1))
    return pl.multiple_of(start, GRID_W), kind


NA_CHUNK = 16


def _na_pieces(kind, i):
    ri, q0 = divmod(i * NA_CHUNK, GRID_W)
    off = (NA_ROWS - 1, NA_ROWS // 2 - 1, -1)[kind]
    lo = (0, ri, NA_KROWS - NA_ROWS)[kind]
    modes = {(True, True): 'both', (True, False): 'even', (False, True): 'odd', (False, False): None}
    out = []
    for k2 in range(NA_KROWS // 2):
        inside = tuple(lo <= kr < lo + NA_ROWS for kr in (2 * k2, 2 * k2 + 1))
        out.append((2 * k2 - ri + off + 1, slice(q0, q0 + NA_CHUNK), modes[inside]))
    return out


def _na_softmax_pieces(s_ref, tab_ref, hh, kind, i):
    rows = slice(i * NA_CHUNK, (i + 1) * NA_CHUNK)
    lane = lax.broadcasted_iota(jnp.int32, (1, LANES), 1)
    xs = []
    for k2, (t, q, mode) in enumerate(_na_pieces(kind, i)):
        if mode is None:
            xs.append(None)
            continue
        bias = tab_ref[hh, t, q, :]
        if mode == 'even':
            bias = jnp.where(lane < GRID_W, bias, NEG)
        elif mode == 'odd':
            bias = jnp.where(lane >= GRID_W, bias, NEG)
        xs.append(s_ref[hh, rows, k2 * LANES:(k2 + 1) * LANES] + bias)
    live = [x for x in xs if x is not None]
    m = jnp.max(functools.reduce(jnp.maximum, live), axis=-1, keepdims=True)
    es = [None if x is None else jnp.exp(x - m) for x in xs]
    total = jnp.sum(functools.reduce(jnp.add, [e for e in es if e is not None]), axis=-1, keepdims=True)
    inv = 1.0 / total
    return [None if e is None else e * inv for e in es]


def _na_heads():
    lane = lax.broadcasted_iota(jnp.int32, (1, LANES), 1)
    return [lane < NA_HEAD_DIM, lane >= NA_HEAD_DIM]


def _na_fwd(qkv, bias):
    seq = qkv.shape[0]
    rows = seq // GRID_W
    nb = rows // NA_QROWS

    def body(q_ref, k_ref, v_ref, bias_ref, o_ref, p_ref, s_scr):
        start, kind = _na_block(pl.program_id(1), nb, rows)

        def block(static_kind):
            q2 = q_ref[...] * NA_SCALE
            kw = k_ref[pl.ds(start, NA_K), :]
            vw = v_ref[pl.ds(start, NA_K), :]
            heads = _na_heads()
            for hh in range(2):
                s_scr[hh] = _nt(jnp.where(heads[hh], q2, jnp.zeros_like(q2)), kw)
            for hh in range(2):
                for i in range(NA_Q // NA_CHUNK):
                    r = slice(i * NA_CHUNK, (i + 1) * NA_CHUNK)
                    for k2, p in enumerate(_na_softmax_pieces(s_scr, bias_ref, hh, static_kind, i)):
                        p = jnp.zeros((NA_CHUNK, LANES), F32) if p is None else p
                        p_ref[hh, r, k2 * LANES:(k2 + 1) * LANES] = p.astype(p_ref.dtype)
            o_ref[...] = jnp.where(heads[0], _mm(p_ref[0], vw), _mm(p_ref[1], vw))

        for static_kind in range(3):
            pl.when(kind == static_kind)(functools.partial(block, static_kind))

    return pl.pallas_call(
        body, name="na_fwd", grid=(NA_HEADS // 2, nb),
        in_specs=[pl.BlockSpec((NA_Q, LANES), lambda hp, b: (b, hp)),
                  pl.BlockSpec((seq, LANES), lambda hp, b: (0, 4 + hp)),
                  pl.BlockSpec((seq, LANES), lambda hp, b: (0, 8 + hp)),
                  pl.BlockSpec((2, NA_TAB, GRID_W, LANES), lambda hp, b: (hp, 0, 0, 0))],
        out_specs=[pl.BlockSpec((NA_Q, LANES), lambda hp, b: (b, hp)),
                   pl.BlockSpec((2, NA_Q, NA_K), lambda hp, b: (hp, b, 0))],
        out_shape=[jax.ShapeDtypeStruct((seq, D_NA), F32), jax.ShapeDtypeStruct((NA_HEADS, seq, NA_K), qkv.dtype)],
        scratch_shapes=[pltpu.VMEM((2, NA_Q, NA_K), F32)],
        compiler_params=_params("arbitrary", "arbitrary"),
    )(qkv, qkv, qkv, bias)


def _na_bwd(qkv, do, probs):
    seq = qkv.shape[0]
    rows = seq // GRID_W
    nb = rows // NA_QROWS

    def body(q_ref, k_ref, v_ref, do_ref, p_ref, dq_ref, dk_ref, dv_ref, dbias_ref, dp_scr, ds_scr):
        b = pl.program_id(1)
        start, kind = _na_block(b, nb, rows)

        @pl.when(b == 0)
        def _():
            dk_ref[...] = jnp.zeros_like(dk_ref)
            dv_ref[...] = jnp.zeros_like(dv_ref)
            dbias_ref[...] = jnp.zeros_like(dbias_ref)

        def block(static_kind):
            q2 = q_ref[...] * NA_SCALE
            kw = k_ref[pl.ds(start, NA_K), :]
            vw = v_ref[pl.ds(start, NA_K), :]
            do2 = do_ref[...].astype(q2.dtype)
            heads = _na_heads()
            col = lambda k2: slice(k2 * LANES, (k2 + 1) * LANES)
            zero = jnp.zeros((NA_CHUNK, LANES), ds_scr.dtype)
            for hh in range(2):
                dp_scr[hh] = _nt(jnp.where(heads[hh], do2, jnp.zeros_like(do2)), vw)
            for hh in range(2):
                for i in range(NA_Q // NA_CHUNK):
                    r = slice(i * NA_CHUNK, (i + 1) * NA_CHUNK)
                    pieces = _na_pieces(static_kind, i)
                    ps = [None if mode is None else p_ref[hh, r, col(k2)].astype(F32)
                          for k2, (_, _, mode) in enumerate(pieces)]
                    dps = [None if p is None else dp_scr[hh, r, col(k2)] for k2, p in enumerate(ps)]
                    pdp = functools.reduce(jnp.add, [p * dp for p, dp in zip(ps, dps) if p is not None])
                    rowsum = jnp.sum(pdp, axis=-1, keepdims=True)
                    for k2, (t, q, _) in enumerate(pieces):
                        if ps[k2] is None:
                            ds_scr[hh, r, col(k2)] = zero
                            continue
                        ds = ps[k2] * (dps[k2] - rowsum)
                        dbias_ref[hh, t, q, :] += ds
                        ds_scr[hh, r, col(k2)] = ds.astype(ds_scr.dtype)
            dq_ref[...] = jnp.where(heads[0], _mm(ds_scr[0], kw), _mm(ds_scr[1], kw)) * NA_SCALE
            dk_ref[pl.ds(start, NA_K), :] += jnp.where(heads[0], _tn(ds_scr[0], q2), _tn(ds_scr[1], q2))
            dv_ref[pl.ds(start, NA_K), :] += jnp.where(heads[0], _tn(p_ref[0], do2), _tn(p_ref[1], do2))

        for static_kind in range(3):
            pl.when(kind == static_kind)(functools.partial(block, static_kind))

    return pl.pallas_call(
        body, name="na_bwd", grid=(NA_HEADS // 2, nb),
        in_specs=[pl.BlockSpec((NA_Q, LANES), lambda hp, b: (b, hp)),
                  pl.BlockSpec((seq, LANES), lambda hp, b: (0, 4 + hp)),
                  pl.BlockSpec((seq, LANES), lambda hp, b: (0, 8 + hp)),
                  pl.BlockSpec((NA_Q, LANES), lambda hp, b: (b, hp)),
                  pl.BlockSpec((2, NA_Q, NA_K), lambda hp, b: (hp, b, 0))],
        out_specs=[pl.BlockSpec((NA_Q, LANES), lambda hp, b: (b, hp)),
                   pl.BlockSpec((seq, LANES), lambda hp, b: (0, hp)),
                   pl.BlockSpec((seq, LANES), lambda hp, b: (0, hp)),
                   pl.BlockSpec((2, NA_TAB, GRID_W, LANES), lambda hp, b: (hp, 0, 0, 0))],
        out_shape=[jax.ShapeDtypeStruct((seq, D_NA), F32), jax.ShapeDtypeStruct((seq, D_NA), F32),
                   jax.ShapeDtypeStruct((seq, D_NA), F32),
                   jax.ShapeDtypeStruct((NA_HEADS, NA_TAB, GRID_W, LANES), F32)],
        scratch_shapes=[pltpu.VMEM((2, NA_Q, NA_K), F32), pltpu.VMEM((2, NA_Q, NA_K), qkv.dtype)],
        compiler_params=_params("arbitrary", "arbitrary"),
    )(qkv, qkv, qkv, do, probs)


def _local_step(x2, p2, tgt, g_pre, g_post, w_in_g, ssm, w_glu, b_glu, rpb, w_out, w_ple_g, g_ple, w_pg):
    seq = x2.shape[0]
    nc = seq // SSM_CHUNK

    (pw, bb, cc, dd), ssm_vjp = jax.vjp(_ssm_tables, *ssm)
    bias, bias_vjp = jax.vjp(_na_table, rpb)

    u, zs, qkv, zn, hn = _fwd_in(x2, g_pre, w_in_g)
    u_p = _to_pairs(u, nc)
    y_p, hin = _ssm_fwd(u_p, pw, bb, cc, dd)
    yssm = _from_pairs(y_p, nc)
    o, probs = _na_fwd(qkv, bias)
    dyssm, dzs, do, dzn, dh1, sq, d_wglu, d_bglu, d_wout, d_wple, d_wpg, d_gpost, d_gple = _mid(
        yssm, zs, o, zn, x2, p2, tgt, w_glu, b_glu, w_out, g_post, w_ple_g, g_ple, w_pg)
    dq, dk, dv, dbias = _na_bwd(qkv, do, probs)
    (d_rpb,) = bias_vjp(dbias)
    du_p, dpw, dbb, dcc, ddd = _ssm_bwd(u_p, _to_pairs(dyssm, nc), hin, pw, bb, cc, dd)
    d_ssm = ssm_vjp((dpw, dbb, dcc, ddd))
    du = _from_pairs(du_p, nc)
    dx, d_win, d_gpre = _bwd_in(du, dzs, dq, dk, dv, dzn, hn, x2, dh1, g_pre, w_in_g)
    return sq, dx, dict(norm_pre=d_gpre, norm_post=d_gpost, w_in=d_win, ssm=d_ssm, w_glu=d_wglu, b_glu=d_bglu,
                        na_rpb=d_rpb, w_out=d_wout, w_ple=d_wple, ple_norm=d_gple, w_ple_gate=d_wpg)


def _place():
    return lax.axis_index("x"), lax.axis_index("y"), lax.axis_index("c")


def _all_gather(shard):
    m_per, n = shard.shape

    def body(x_ref, out_ref, send_sems, recv_sems, local_sem):
        x, y, c = _place()
        me, sibling = (x, y, c), (x, y, 1 - c)
        chips = [(1 - x, y), (x, 1 - y), (1 - x, 1 - y)]

        def rows(px, py, pc):
            return out_ref.at[pl.ds((4 * px + 2 * py + pc) * m_per, m_per), :]

        def copy(k, block, to, src=None):
            return pltpu.make_async_remote_copy(
                src_ref=rows(*block) if src is None else src, dst_ref=rows(*block),
                send_sem=send_sems.at[k], recv_sem=recv_sems.at[k], device_id=to, device_id_type=MESH)

        mine = pltpu.make_async_copy(x_ref, rows(*me), local_sem)
        mine.start()
        first = [copy(0, me, sibling, src=x_ref)]
        first += [copy(1 + j, me, (*chip, c), src=x_ref) for j, chip in enumerate(chips)]
        for cp in first:
            cp.start()
        passed = [copy(4 + j, (*chip, c), sibling) for j, chip in enumerate(chips)]
        for j, chip in enumerate(chips):
            copy(1 + j, (*chip, c), me).wait_recv()
            passed[j].start()
        copy(0, sibling, me).wait_recv()
        for j, chip in enumerate(chips):
            copy(4 + j, (*chip, 1 - c), me).wait_recv()
        for cp in first + passed:
            cp.wait_send()
        mine.wait()

    return pl.pallas_call(
        body, name="all_gather",
        out_shape=jax.ShapeDtypeStruct((N_DEV * m_per, n), shard.dtype),
        in_specs=[_whole_vmem()], out_specs=_whole_vmem(),
        scratch_shapes=[pltpu.SemaphoreType.DMA((7,)), pltpu.SemaphoreType.DMA((7,)), pltpu.SemaphoreType.DMA],
        compiler_params=pltpu.CompilerParams(vmem_limit_bytes=VMEM_LIMIT),
    )(shard)


def _exchange_partials(part):
    _, r, n = part.shape

    def body(part_ref, land_ref, send_sems, recv_sems, local_sem):
        x, y, c = _place()
        me = 4 * x + 2 * y + c
        local = pltpu.make_async_copy(part_ref.at[me], land_ref.at[me], local_sem)
        local.start()
        copies = []
        for k in range(1, N_DEV):
            px, py, pc = x ^ ((k >> 2) & 1), y ^ ((k >> 1) & 1), c ^ (k & 1)
            peer = 4 * px + 2 * py + pc
            cp = pltpu.make_async_remote_copy(
                src_ref=part_ref.at[peer], dst_ref=land_ref.at[me],
                send_sem=send_sems.at[k - 1], recv_sem=recv_sems.at[k - 1],
                device_id=(px, py, pc), device_id_type=MESH)
            cp.start()
            copies.append(cp)
        for cp in copies:
            cp.wait_recv()
        for cp in copies:
            cp.wait_send()
        local.wait()

    anyspace = pl.BlockSpec(memory_space=pl.ANY)
    return pl.pallas_call(
        body, name="exchange_partials",
        out_shape=jax.ShapeDtypeStruct(part.shape, part.dtype),
        in_specs=[anyspace], out_specs=anyspace,
        scratch_shapes=[pltpu.SemaphoreType.DMA((N_DEV - 1,)), pltpu.SemaphoreType.DMA((N_DEV - 1,)),
                        pltpu.SemaphoreType.DMA],
    )(part)


def _pair_exchange(part):
    _, r, n = part.shape

    def body(part_ref, land_ref, send_sems, recv_sems):
        x, y, c = _place()
        copies = []
        for q in range(4):
            cp = pltpu.make_async_remote_copy(
                src_ref=part_ref.at[2 * q + (1 - c)], dst_ref=land_ref.at[q],
                send_sem=send_sems.at[q], recv_sem=recv_sems.at[q],
                device_id=(x, y, 1 - c), device_id_type=MESH)
            cp.start()
            copies.append(cp)
        for cp in copies:
            cp.wait_recv()
        for cp in copies:
            cp.wait_send()

    anyspace = pl.BlockSpec(memory_space=pl.ANY)
    return pl.pallas_call(
        body, name="pair_exchange",
        out_shape=jax.ShapeDtypeStruct((4, r, n), part.dtype),
        in_specs=[anyspace], out_specs=anyspace,
        scratch_shapes=[pltpu.SemaphoreType.DMA((4,)), pltpu.SemaphoreType.DMA((4,))],
    )(part)


def _pair_sum(part, land):
    _, r, n = part.shape
    tile = 256
    assert r % tile == 0

    def body(part_ref, land_ref, out_ref):
        c = lax.axis_index("c")
        for q in range(4):
            out_ref[q] = (part_ref[q, c] + land_ref[q]).astype(BF16)

    return pl.pallas_call(
        body, name="pair_sum", grid=(r // tile,),
        in_specs=[pl.BlockSpec((4, 2, tile, n), lambda i: (0, 0, i, 0)), pl.BlockSpec((4, tile, n), lambda i: (0, i, 0))],
        out_specs=pl.BlockSpec((4, tile, n), lambda i: (0, i, 0)),
        out_shape=jax.ShapeDtypeStruct((4, r, n), BF16),
        compiler_params=_params("arbitrary"),
    )(part.reshape(4, 2, r, n), land)


def _chip_exchange(chip_part):
    _, r, n = chip_part.shape

    def body(part_ref, land_ref, send_sems, recv_sems, local_sem):
        x, y, c = _place()
        mine = 2 * x + y
        local = pltpu.make_async_copy(part_ref.at[mine], land_ref.at[mine], local_sem)
        local.start()
        copies = []
        for k in range(1, 4):
            px, py = x ^ (k >> 1), y ^ (k & 1)
            cp = pltpu.make_async_remote_copy(
                src_ref=part_ref.at[2 * px + py], dst_ref=land_ref.at[mine],
                send_sem=send_sems.at[k - 1], recv_sem=recv_sems.at[k - 1],
                device_id=(px, py, c), device_id_type=MESH)
            cp.start()
            copies.append(cp)
        for cp in copies:
            cp.wait_recv()
        for cp in copies:
            cp.wait_send()
        local.wait()

    anyspace = pl.BlockSpec(memory_space=pl.ANY)
    return pl.pallas_call(
        body, name="chip_exchange",
        out_shape=jax.ShapeDtypeStruct(chip_part.shape, chip_part.dtype),
        in_specs=[anyspace], out_specs=anyspace,
        scratch_shapes=[pltpu.SemaphoreType.DMA((3,)), pltpu.SemaphoreType.DMA((3,)), pltpu.SemaphoreType.DMA],
    )(chip_part)


def _sum_slots(land):
    slots, r, n = land.shape
    tile = next(t for t in (512, 256, 128) if r % t == 0)

    def body(land_ref, out_ref):
        acc = land_ref[0].astype(F32)
        for j in range(1, slots):
            acc = acc + land_ref[j].astype(F32)
        out_ref[...] = acc

    return pl.pallas_call(
        body, name="sum_slots", grid=(r // tile,),
        in_specs=[pl.BlockSpec((slots, tile, n), lambda i: (0, i, 0))],
        out_specs=pl.BlockSpec((tile, n), lambda i: (i, 0)),
        out_shape=jax.ShapeDtypeStruct((r, n), F32),
        compiler_params=_params("arbitrary"),
    )(land)


def _adamw(w, g, m, v):
    r, c = w.shape
    tile = 256 if r % 256 == 0 else r

    def body(w_ref, g_ref, m_ref, v_ref, d_ref, nm_ref, nv_ref):
        g = g_ref[...]
        nm = ADAM_B1 * m_ref[...] + (1.0 - ADAM_B1) * g
        nv = ADAM_B2 * v_ref[...] + (1.0 - ADAM_B2) * (g * g)
        m_hat = nm / (1.0 - ADAM_B1 ** ADAM_STEP)
        v_hat = nv / (1.0 - ADAM_B2 ** ADAM_STEP)
        d_ref[...] = -ADAM_LR * (m_hat / (jnp.sqrt(v_hat) + ADAM_EPS) + ADAM_WD * w_ref[...])
        nm_ref[...] = nm
        nv_ref[...] = nv

    spec = pl.BlockSpec((tile, c), lambda i: (i, 0))
    out = jax.ShapeDtypeStruct((r, c), F32)
    return pl.pallas_call(
        body, name="adamw", grid=(r // tile,), in_specs=[spec] * 4, out_specs=[spec] * 3,
        out_shape=[out, out, out], compiler_params=_params("arbitrary"),
    )(w, g, m, v)


_SLAB = 8 * LANES


def _flat_rows(a):
    flat = a.reshape(-1)
    pad = (-flat.shape[0]) % _SLAB
    if pad:
        flat = jnp.concatenate([flat, jnp.zeros((pad,), flat.dtype)])
    return flat.reshape(-1, LANES)


def _pack(arrays):
    slabs = [_flat_rows(a) for a in arrays]
    return jnp.concatenate(slabs, axis=0), [s.shape[0] for s in slabs]


def _unpack(packed, like, rows):
    out, at = [], 0
    for a, r in zip(like, rows):
        out.append(packed[at:at + r].reshape(-1)[:a.size].reshape(a.shape))
        at += r
    return out


SMALL = ('norm_pre', 'norm_post', 'ssm_a_re', 'ssm_a_im', 'ssm_log_dt', 'ssm_b_re', 'ssm_b_im', 'ssm_c_re',
         'ssm_c_im', 'ssm_d', 'b_glu', 'na_rpb', 'ple_norm')
BIG = ('w_in', 'w_glu', 'w_out', 'w_ple', 'w_ple_gate')
ORDER = ('norm_pre', 'norm_post', 'w_in', 'ssm_a_re', 'ssm_a_im', 'ssm_log_dt', 'ssm_b_re', 'ssm_b_im', 'ssm_c_re',
         'ssm_c_im', 'ssm_d', 'w_glu', 'b_glu', 'na_rpb', 'w_out', 'w_ple', 'ple_norm', 'w_ple_gate')


def kernel(x, p, norm_pre, norm_post, w_in, ssm_a_re, ssm_a_im, ssm_log_dt, ssm_b_re, ssm_b_im, ssm_c_re, ssm_c_im, ssm_d, w_glu, b_glu, na_rpb, w_out, w_ple, ple_norm, w_ple_gate, loss_target, m_norm_pre, m_norm_post, m_w_in, m_ssm_a_re, m_ssm_a_im, m_ssm_log_dt, m_ssm_b_re, m_ssm_b_im, m_ssm_c_re, m_ssm_c_im, m_ssm_d, m_w_glu, m_b_glu, m_na_rpb, m_w_out, m_w_ple, m_ple_norm, m_w_ple_gate, v_norm_pre, v_norm_post, v_w_in, v_ssm_a_re, v_ssm_a_im, v_ssm_log_dt, v_ssm_b_re, v_ssm_b_im, v_ssm_c_re, v_ssm_c_im, v_ssm_d, v_w_glu, v_b_glu, v_na_rpb, v_w_out, v_w_ple, v_ple_norm, v_w_ple_gate):
    args = dict(locals())
    weights = {n: args[n] for n in ORDER}
    mom_m = {n: args["m_" + n] for n in ORDER}
    mom_v = {n: args["v_" + n] for n in ORDER}

    shards = [weights[n][0].astype(BF16) for n in BIG]
    packed, shard_rows = _pack(shards)
    gathered = _all_gather(packed).reshape(N_DEV, -1, LANES)
    full = []
    at = 0
    for s, r in zip(shards, shard_rows):
        full.append(gathered[:, at:at + r].reshape((N_DEV,) + s.shape))
        at += r
    w_in_g, w_glu_g, w_out_g, w_ple_g, w_pg_g = full
    w_glu_f = w_glu_g.reshape(D_SSM, D_SSM)
    w_out_f = w_out_g.reshape(D_MODEL, D_MODEL)
    w_pg_f = w_pg_g.reshape(D_MODEL, D_MODEL)

    ssm = tuple(weights[n][0] for n in ('ssm_a_re', 'ssm_a_im', 'ssm_log_dt', 'ssm_b_re', 'ssm_b_im',
                                        'ssm_c_re', 'ssm_c_im', 'ssm_d'))
    sq, dx, grads = _local_step(x[0], p[0, 0], loss_target[0], norm_pre, norm_post, w_in_g, ssm, w_glu_f, b_glu,
                                na_rpb[0], w_out_f, w_ple_g, ple_norm, w_pg_f)
    loss_local = (0.5 / D_MODEL * jnp.sum(sq)).reshape(1)

    local = dict(norm_pre=grads['norm_pre'], norm_post=grads['norm_post'], b_glu=grads['b_glu'],
                 na_rpb=grads['na_rpb'][None], ple_norm=grads['ple_norm'])
    for n, g in zip(('ssm_a_re', 'ssm_a_im', 'ssm_log_dt', 'ssm_b_re', 'ssm_b_im', 'ssm_c_re', 'ssm_c_im', 'ssm_d'),
                    grads['ssm']):
        local[n] = g[None]

    big_parts = [grads['w_in'],
                 grads['w_glu'].reshape(N_DEV, D_SSM // N_DEV, D_SSM),
                 grads['w_out'].reshape(N_DEV, D_MODEL // N_DEV, D_MODEL),
                 grads['w_ple'],
                 grads['w_ple_gate'].reshape(N_DEV, D_MODEL // N_DEV, D_MODEL)]
    small_flat, small_rows = _pack([local[n] for n in SMALL] + [loss_local])
    pad = (-small_flat.shape[0]) % (128 * N_DEV)
    if pad:
        small_flat = jnp.concatenate([small_flat, jnp.zeros((pad, LANES), F32)], axis=0)
    small_per = small_flat.shape[0] // N_DEV
    part = jnp.concatenate([bp.reshape(N_DEV, -1, LANES) for bp in big_parts], axis=1)
    reduced = _sum_slots(_chip_exchange(_pair_sum(part, _pair_exchange(part))))
    big_grads = {}
    at = 0
    for n, bp in zip(BIG, big_parts):
        r = bp[0].size // LANES
        big_grads[n] = reduced[at:at + r].reshape(bp.shape[1:])
        at += r
    small_all = _all_gather(_sum_slots(_exchange_partials(small_flat.reshape(N_DEV, small_per, LANES))))

    outs = {}
    for n in BIG:
        w2 = weights[n][0]
        d, nm, nv = _adamw(w2, big_grads[n], mom_m[n][0], mom_v[n][0])
        outs[n] = (big_grads[n][None], d[None], nm[None], nv[None])
    like = [weights[n] for n in SMALL]
    w_s, rows_s = _pack(like)
    m_s, _ = _pack([mom_m[n] for n in SMALL])
    v_s, _ = _pack([mom_v[n] for n in SMALL])
    g_s = small_all[:w_s.shape[0]]
    loss = small_all[w_s.shape[0], 0]
    d_s, nm_s, nv_s = _adamw(w_s, g_s, m_s, v_s)
    unpacked = [_unpack(a, like, rows_s) for a in (g_s, d_s, nm_s, nv_s)]
    for i, n in enumerate(SMALL):
        outs[n] = tuple(u[i] for u in unpacked)

    return (loss, dx[None], *[outs[n][0] for n in ORDER], *[outs[n][1] for n in ORDER],
            *[outs[n][2] for n in ORDER], *[outs[n][3] for n in ORDER])
```

```python
import functools

import jax
import jax.numpy as jnp
from jax import lax
from jax.experimental import pallas as pl
from jax.experimental.pallas import tpu as pltpu

F32 = jnp.float32
BF16 = jnp.bfloat16
HIGHEST = lax.Precision.HIGHEST

D_MODEL = 1024
D_PLE = 256
GRID_W = 64
D_SSM = 512
SSM_GROUP = 16
N_GROUPS = 32
N_PAIRS = 16
SSM_STATE = 64
D_NA = 512
NA_HEADS = 8
NA_HEAD_DIM = 64
NA_ROWS = 8
NA_COLS = 16
D_IN_PROJ = 3072
EPS = 1e-6
N_DEV = 8
SHARD_IN = D_IN_PROJ // N_DEV
LANES = 128
SSM_CHUNK = 16
TOK_TILE = 256
NA_QROWS = 4
NA_KROWS = 12
NEG = -1e30
VMEM_LIMIT = 56 * 1024 * 1024

ADAM_LR = 0.001
ADAM_B1 = 0.9
ADAM_B2 = 0.999
ADAM_EPS = 1e-08
ADAM_WD = 0.01
ADAM_STEP = 10

MESH = pl.DeviceIdType.MESH


def _params(*sem):
    return pltpu.CompilerParams(dimension_semantics=sem or None, vmem_limit_bytes=VMEM_LIMIT)


def _whole_vmem():
    return pl.BlockSpec(memory_space=pltpu.VMEM)


def _nt(a, b):
    return lax.dot_general(a, b, (((1,), (1,)), ((), ())), preferred_element_type=F32)


def _tn(a, b):
    return lax.dot_general(a, b, (((0,), (0,)), ((), ())), preferred_element_type=F32)


def _mm(a, b):
    return jnp.dot(a, b, preferred_element_type=F32)


def _sigmoid(x):
    return 1.0 / (1.0 + jnp.exp(-x))


_GELU_C = 0.7978845608028654


def _gelu(x):
    return 0.5 * x * (1.0 + jnp.tanh(_GELU_C * (x + 0.044715 * x * x * x)))


def _gelu_grad(x):
    th = jnp.tanh(_GELU_C * (x + 0.044715 * x * x * x))
    return 0.5 * (1.0 + th) + 0.5 * x * (1.0 - th * th) * _GELU_C * (1.0 + 3.0 * 0.044715 * x * x)


def _ssm_tables(a_re, a_im, log_dt, b_re, b_im, c_re, c_im, d):
    T, P, H = SSM_CHUNK, SSM_STATE, SSM_GROUP
    dt = jnp.exp(log_dt)[..., None]
    xr = dt * a_re
    xi = dt * a_im
    mag = jnp.exp(xr)
    lr = mag * jnp.cos(xi)
    li = mag * jnp.sin(xi)
    den = a_re * a_re + a_im * a_im
    cr = ((lr - 1.0) * a_re + li * a_im) / den
    ci = (li * a_re - (lr - 1.0) * a_im) / den
    bbr = cr[..., None] * b_re - ci[..., None] * b_im
    bbi = cr[..., None] * b_im + ci[..., None] * b_re
    kk = jnp.arange(T + 1, dtype=F32)[:, None, None, None]
    pm = jnp.exp(kk * xr)
    pw = jnp.stack([pm * jnp.cos(kk * xi), pm * jnp.sin(kk * xi)], axis=2)
    pw = pw.reshape(T + 1, 2, 2, N_PAIRS, 2 * P).transpose(3, 0, 1, 2, 4).reshape(N_PAIRS, T + 1, 8 * P)
    pw = jnp.concatenate([pw, jnp.zeros((N_PAIRS, 24 - (T + 1), 8 * P), F32)], axis=1)
    eye2 = jnp.eye(2, dtype=F32)

    def expand(t):
        t = t.transpose(2, 0, 1, 3, 4, 5)
        t = t[:, :, :, :, :, None, :] * eye2[None, None, None, :, None, :, None]
        return t.reshape(N_PAIRS, 4, 2 * H, 2 * P)

    bb = expand(jnp.stack([bbr, bbi], axis=1).reshape(2, 2, N_PAIRS, 2, P, H).transpose(0, 1, 2, 3, 5, 4))
    cc = expand(jnp.stack([c_re, c_im], axis=1).reshape(2, 2, N_PAIRS, 2, H, P))
    dd = d.reshape(N_PAIRS, 2 * H)[:, :, None] * jnp.eye(2 * H, dtype=F32)[None]
    dd = jnp.concatenate([dd, jnp.zeros((N_PAIRS, 2 * H, LANES - 2 * H), F32)], axis=2)
    return pw, bb, cc, dd


NA_TAB = 2 * NA_ROWS


def _na_table(rpb):
    qc = jnp.arange(GRID_W)[:, None, None]
    kc = (jnp.arange(2 * GRID_W) % GRID_W)[None, :, None]
    dc = jnp.arange(2 * NA_COLS - 1)[None, None, :]
    cstart = jnp.clip(qc - NA_COLS // 2, 0, GRID_W - NA_COLS)
    csel = ((kc >= cstart) & (kc < cstart + NA_COLS) & (kc - qc + NA_COLS - 1 == dc)).astype(F32)
    col_ok = jnp.sum(csel, axis=-1) > 0.5
    part = jnp.einsum('hrc,qmc->hrqm', rpb, csel, precision=HIGHEST)
    zero = jnp.zeros_like(part[:, :1])
    odd = jnp.arange(2 * GRID_W) >= GRID_W
    tab = jnp.where(odd, jnp.concatenate([part, zero], axis=1), jnp.concatenate([zero, part], axis=1))
    return jnp.where(col_ok, tab, NEG)


def _place():
    return lax.axis_index("x"), lax.axis_index("y"), lax.axis_index("c")


def _remote(src, dst, send_sem, recv_sem, device):
    return pltpu.make_async_remote_copy(src_ref=src, dst_ref=dst, send_sem=send_sem, recv_sem=recv_sem,
                                        device_id=device, device_id_type=MESH)


def _start_all(copies):
    for cp in copies:
        cp.start()


def _wait_all(copies):
    for cp in copies:
        cp.wait()


def _gather_copies(shard_refs, full_refs, send_sems, recv_sems, local_sems):
    x, y, c = _place()
    me = 4 * x + 2 * y + c
    out = []
    for t, (shard, full) in enumerate(zip(shard_refs, full_refs)):
        out.append(pltpu.make_async_copy(shard, full.at[me], local_sems.at[t]))
        for k in range(1, N_DEV):
            peer = (x ^ ((k >> 2) & 1), y ^ ((k >> 1) & 1), c ^ (k & 1))
            out.append(_remote(shard, full.at[me], send_sems.at[t, k - 1], recv_sems.at[t, k - 1], peer))
    return out


def _pair_copies(part_refs, land_refs, send_sems, recv_sems):
    x, y, c = _place()
    out = []
    for t, (part, land) in enumerate(zip(part_refs, land_refs)):
        for q in range(4):
            out.append(_remote(part.at[2 * q + (1 - c)], land.at[q], send_sems.at[t, q], recv_sems.at[t, q],
                               (x, y, 1 - c)))
    return out


def _chip_copies(part_refs, land_refs, send_sems, recv_sems, local_sems):
    x, y, c = _place()
    mine = 2 * x + y
    out = []
    for t, (part, land) in enumerate(zip(part_refs, land_refs)):
        out.append(pltpu.make_async_copy(part.at[mine], land.at[mine], local_sems.at[t]))
        for k in range(1, 4):
            px, py = x ^ (k >> 1), y ^ (k & 1)
            out.append(_remote(part.at[2 * px + py], land.at[mine], send_sems.at[t, k - 1], recv_sems.at[t, k - 1],
                               (px, py, c)))
    return out


def _ride(copies, first, last):
    pl.when(first)(functools.partial(_start_all, copies))
    pl.when(last)(functools.partial(_wait_all, copies))


_ANY = lambda: pl.BlockSpec(memory_space=pl.ANY)


def _dma_sems(*shape):
    return pltpu.SemaphoreType.DMA(shape)


def _fwd_in(x2, g_pre, w_in_g, shards=()):
    seq = x2.shape[0]
    ns = len(shards)
    steps = seq // TOK_TILE

    def body(x_ref, g_ref, w_ref, *rest):
        shard_refs, rest = rest[:ns], rest[ns:]
        (u_ref, zs_ref, qkv_ref, zn_ref, hn_ref), rest = rest[:5], rest[5:]
        if ns:
            i = pl.program_id(0)
            _ride(_gather_copies(shard_refs, rest[:ns], *rest[ns:]), i == 0, i == steps - 1)
        x = x_ref[...]
        r = lax.rsqrt(jnp.mean(x * x, axis=-1, keepdims=True) + EPS)
        hn = (x * r * g_ref[...]).astype(BF16)
        hn_ref[...] = hn
        for j in range(N_DEV):
            pj = _mm(hn, w_ref[j])
            for i in range(SHARD_IN // LANES):
                blk = (SHARD_IN // LANES) * j + i
                piece = pj[:, i * LANES:(i + 1) * LANES]
                if blk < 4:
                    u_ref[:, blk * LANES:(blk + 1) * LANES] = piece
                elif blk < 8:
                    zs_ref[:, (blk - 4) * LANES:(blk - 3) * LANES] = piece
                elif blk < 20:
                    qkv_ref[:, (blk - 8) * LANES:(blk - 7) * LANES] = piece.astype(BF16)
                else:
                    zn_ref[:, (blk - 20) * LANES:(blk - 19) * LANES] = piece

    tok = lambda w: pl.BlockSpec((TOK_TILE, w), lambda i: (i, 0))
    out = pl.pallas_call(
        body, name="fwd_in", grid=(steps,),
        in_specs=[tok(D_MODEL), pl.BlockSpec((1, D_MODEL), lambda i: (0, 0)), _whole_vmem()] + [_ANY()] * ns,
        out_specs=[tok(D_SSM), tok(D_SSM), tok(3 * D_NA), tok(D_NA), tok(D_MODEL)] + [_ANY()] * ns,
        out_shape=[jax.ShapeDtypeStruct((seq, D_SSM), F32), jax.ShapeDtypeStruct((seq, D_SSM), F32),
                   jax.ShapeDtypeStruct((seq, 3 * D_NA), BF16), jax.ShapeDtypeStruct((seq, D_NA), F32),
                   jax.ShapeDtypeStruct((seq, D_MODEL), BF16)]
        + [jax.ShapeDtypeStruct((N_DEV,) + sh.shape, sh.dtype) for sh in shards],
        scratch_shapes=[_dma_sems(ns, N_DEV - 1), _dma_sems(ns, N_DEV - 1), _dma_sems(ns)] if ns else [],
        compiler_params=_params("arbitrary"),
    )(x2, g_pre, w_in_g, *shards)
    return out[:5], list(out[5:])


def _mid(yssm, zs, o, zn, x2, p2, tgt, w_glu, b_glu, w_out, g_post, w_ple_g, g_ple, w_pg):
    seq = x2.shape[0]

    def body(yssm_ref, zs_ref, o_ref, zn_ref, x_ref, p_ref, tgt_ref, wglu_ref, bglu_ref, wout_ref, gpost_ref,
             wple_ref, gple_ref, wpg_ref,
             dyssm_ref, dzs_ref, do_ref, dzn_ref, dh1_ref, loss_ref, dwglu_ref, dbglu_ref, dwout_ref, dwple_ref,
             dwpg_ref, dgpost_ref, dgple_ref):
        @pl.when(pl.program_id(0) == 0)
        def _():
            for ref in (loss_ref, dwglu_ref, dbglu_ref, dwout_ref, dwple_ref, dwpg_ref, dgpost_ref, dgple_ref):
                ref[...] = jnp.zeros_like(ref)

        yv = yssm_ref[...]
        g1 = _gelu(yv)
        g1b = g1.astype(BF16)
        sg = _sigmoid(_mm(g1b, wglu_ref[...]) + bglu_ref[...])
        zs_v = zs_ref[...]
        s_zs = _sigmoid(zs_v)
        g2 = g1 * sg
        zn_v = zn_ref[...]
        s_zn = _sigmoid(zn_v)
        o_v = o_ref[...]
        cat = jnp.concatenate([g2 * (zs_v * s_zs), o_v * (zn_v * s_zn)], axis=1).astype(BF16)
        mix = _mm(cat, wout_ref[...])
        r2 = lax.rsqrt(jnp.mean(mix * mix, axis=-1, keepdims=True) + EPS)
        n2 = mix * r2
        gpost = gpost_ref[...]
        h1 = x_ref[...] + n2 * gpost
        pb = p_ref[...].astype(BF16)
        epre = jnp.concatenate([_mm(pb, wple_ref[j]) for j in range(N_DEV)], axis=1)
        r3 = lax.rsqrt(jnp.mean(epre * epre, axis=-1, keepdims=True) + EPS)
        n3 = epre * r3
        gple = gple_ref[...]
        e = n3 * gple
        h1b = h1.astype(BF16)
        gate = _sigmoid(_mm(h1b, wpg_ref[...]))
        diff = h1 + gate * e - tgt_ref[...]
        loss_ref[...] += jnp.sum(diff * diff, axis=0, keepdims=True)

        dy = diff * (1.0 / D_MODEL)
        dgp = (dy * e * gate * (1.0 - gate)).astype(BF16)
        de = dy * gate
        dh1 = dy + _nt(dgp, wpg_ref[...])
        dh1_ref[...] = dh1
        dwpg_ref[...] += _tn(h1b, dgp)
        dgple_ref[...] += jnp.sum(de * n3, axis=0, keepdims=True)
        dn3 = de * gple
        depre = (r3 * (dn3 - n3 * jnp.mean(dn3 * n3, axis=-1, keepdims=True))).astype(BF16)
        for j in range(N_DEV):
            dwple_ref[j] += _tn(pb, depre[:, j * LANES:(j + 1) * LANES])
        dgpost_ref[...] += jnp.sum(dh1 * n2, axis=0, keepdims=True)
        dn2 = dh1 * gpost
        dmix = (r2 * (dn2 - n2 * jnp.mean(dn2 * n2, axis=-1, keepdims=True))).astype(BF16)
        dcat = _nt(dmix, wout_ref[...])
        dwout_ref[...] += _tn(cat, dmix)

        dys = dcat[:, :D_SSM]
        dyn = dcat[:, D_SSM:]
        dg2 = dys * (zs_v * s_zs)
        dzs_ref[...] = dys * g2 * (s_zs * (1.0 + zs_v * (1.0 - s_zs)))
        dt = dg2 * g2 * (1.0 - sg)
        dtb = dt.astype(BF16)
        dg1 = dg2 * sg + _nt(dtb, wglu_ref[...])
        dwglu_ref[...] += _tn(g1b, dtb)
        dbglu_ref[...] += jnp.sum(dt, axis=0, keepdims=True)
        dyssm_ref[...] = dg1 * _gelu_grad(yv)
        do_ref[...] = dyn * (zn_v * s_zn)
        dzn_ref[...] = dyn * o_v * (s_zn * (1.0 + zn_v * (1.0 - s_zn)))

    tok = lambda w: pl.BlockSpec((TOK_TILE, w), lambda i: (i, 0))
    row = lambda w: pl.BlockSpec((1, w), lambda i: (0, 0))
    vm = _whole_vmem()
    half = jax.ShapeDtypeStruct((seq, D_SSM), F32)
    gain = jax.ShapeDtypeStruct((1, D_MODEL), F32)
    return pl.pallas_call(
        body, name="mid", grid=(seq // TOK_TILE,),
        in_specs=[tok(D_SSM), tok(D_SSM), tok(D_NA), tok(D_NA), tok(D_MODEL), tok(D_PLE), tok(D_MODEL),
                  vm, row(D_SSM), vm, row(D_MODEL), vm, row(D_MODEL), vm],
        out_specs=[tok(D_SSM), tok(D_SSM), tok(D_NA), tok(D_NA), tok(D_MODEL), vm, vm, vm, vm, vm, vm, vm, vm],
        out_shape=[half, half, half, half, jax.ShapeDtypeStruct((seq, D_MODEL), F32), gain,
                   jax.ShapeDtypeStruct((D_SSM, D_SSM), F32), jax.ShapeDtypeStruct((1, D_SSM), F32),
                   jax.ShapeDtypeStruct((D_MODEL, D_MODEL), F32),
                   jax.ShapeDtypeStruct((N_DEV, D_PLE, LANES), F32),
                   jax.ShapeDtypeStruct((D_MODEL, D_MODEL), F32), gain, gain],
        compiler_params=_params("arbitrary"),
    )(yssm, zs, o, zn, x2, p2, tgt, w_glu, b_glu, w_out, g_post, w_ple_g, g_ple, w_pg)


def _bwd_in(du, dzs, dq, dk, dv, dzn, hn, x2, dh1, g_pre, w_in_g):
    seq = x2.shape[0]

    def body(du_ref, dzs_ref, dq_ref, dk_ref, dv_ref, dzn_ref, hn_ref, x_ref, dh1_ref, g_ref, w_ref,
             dx_ref, dw_ref, dg_ref, dproj_ref):
        @pl.when(pl.program_id(0) == 0)
        def _():
            dw_ref[...] = jnp.zeros_like(dw_ref)
            dg_ref[...] = jnp.zeros_like(dg_ref)

        for k, ref in enumerate((du_ref, dzs_ref, dq_ref, dk_ref, dv_ref, dzn_ref)):
            dproj_ref[:, k * D_SSM:(k + 1) * D_SSM] = ref[...].astype(BF16)
        hn = hn_ref[...]
        dhn = jnp.zeros((TOK_TILE, D_MODEL), F32)
        for j in range(N_DEV):
            dpj = dproj_ref[:, j * SHARD_IN:(j + 1) * SHARD_IN]
            dhn += _nt(dpj, w_ref[j])
            dw_ref[j] += _tn(hn, dpj)
        x = x_ref[...]
        r = lax.rsqrt(jnp.mean(x * x, axis=-1, keepdims=True) + EPS)
        n1 = x * r
        dg_ref[...] += jnp.sum(dhn * n1, axis=0, keepdims=True)
        dn1 = dhn * g_ref[...]
        dx_ref[...] = dh1_ref[...] + r * (dn1 - n1 * jnp.mean(dn1 * n1, axis=-1, keepdims=True))

    tok = lambda w: pl.BlockSpec((TOK_TILE, w), lambda i: (i, 0))
    vm = _whole_vmem()
    return pl.pallas_call(
        body, name="bwd_in", grid=(seq // TOK_TILE,),
        in_specs=[tok(D_SSM)] * 6 + [tok(D_MODEL), tok(D_MODEL), tok(D_MODEL),
                                      pl.BlockSpec((1, D_MODEL), lambda i: (0, 0)), vm],
        out_specs=[tok(D_MODEL), vm, vm],
        out_shape=[jax.ShapeDtypeStruct((seq, D_MODEL), F32),
                   jax.ShapeDtypeStruct((N_DEV, D_MODEL, SHARD_IN), F32),
                   jax.ShapeDtypeStruct((1, D_MODEL), F32)],
        scratch_shapes=[pltpu.VMEM((TOK_TILE, D_IN_PROJ), BF16)],
        compiler_params=_params("arbitrary"),
    )(du, dzs, dq, dk, dv, dzn, hn, x2, dh1, g_pre, w_in_g)


RELAYOUT_CHUNKS = 128
PAIR_W = 2 * SSM_GROUP
PAIRS_PER_BLOCK = LANES // PAIR_W
CHUNK_W = SSM_CHUNK * PAIR_W
PW_ROWS = 24


def _lane_window(lo, width):
    lane = lax.broadcasted_iota(jnp.int32, (1, LANES), 1)
    return (lane >= lo) & (lane < lo + width)


def _to_pairs(a, nc):
    ncb = min(RELAYOUT_CHUNKS, nc)

    def body(x_ref, out_ref):
        xs = [x_ref[pl.ds(s, ncb, stride=SSM_CHUNK), :] for s in range(SSM_CHUNK)]
        for a_ in range(PAIRS_PER_BLOCK):
            for v in range(SSM_CHUNK // PAIRS_PER_BLOCK):
                acc = None
                for i in range(PAIRS_PER_BLOCK):
                    shift = (PAIR_W * (i - a_)) % LANES
                    piece = xs[PAIRS_PER_BLOCK * v + i]
                    piece = pltpu.roll(piece, shift, axis=1) if shift else piece
                    acc = piece if acc is None else jnp.where(_lane_window(PAIR_W * i, PAIR_W), piece, acc)
                out_ref[a_, :, LANES * v:LANES * (v + 1)] = acc.astype(BF16)

    return pl.pallas_call(
        body, name="to_pairs", grid=(D_SSM // LANES, nc // ncb),
        in_specs=[pl.BlockSpec((ncb * SSM_CHUNK, LANES), lambda cl, cb: (cb, cl))],
        out_specs=pl.BlockSpec((PAIRS_PER_BLOCK, ncb, CHUNK_W), lambda cl, cb: (cl, cb, 0)),
        out_shape=jax.ShapeDtypeStruct((N_PAIRS, nc, CHUNK_W), BF16),
        compiler_params=_params("arbitrary", "arbitrary"),
    )(a)


def _from_pairs(a, nc):
    ncb = min(RELAYOUT_CHUNKS, nc)

    def body(y_ref, out_ref):
        for s in range(SSM_CHUNK):
            v, i = divmod(s, PAIRS_PER_BLOCK)
            acc = None
            for a_ in range(PAIRS_PER_BLOCK):
                shift = (PAIR_W * (a_ - i)) % LANES
                piece = y_ref[a_, :, LANES * v:LANES * (v + 1)]
                piece = pltpu.roll(piece, shift, axis=1) if shift else piece
                acc = piece if acc is None else jnp.where(_lane_window(PAIR_W * a_, PAIR_W), piece, acc)
            out_ref[pl.ds(s, ncb, stride=SSM_CHUNK), :] = acc

    return pl.pallas_call(
        body, name="from_pairs", grid=(D_SSM // LANES, nc // ncb),
        in_specs=[pl.BlockSpec((PAIRS_PER_BLOCK, ncb, CHUNK_W), lambda cl, cb: (cl, cb, 0))],
        out_specs=pl.BlockSpec((ncb * SSM_CHUNK, LANES), lambda cl, cb: (cb, cl)),
        out_shape=jax.ShapeDtypeStruct((nc * SSM_CHUNK, D_SSM), F32),
        compiler_params=_params("arbitrary", "arbitrary"),
    )(a)


def _boundary_scan(nc, pw_ref, src_ref, dst_ref, conj):
    nblk = nc // 8
    lr0, li0, lr1, li1 = (pw_ref[0, SSM_CHUNK:SSM_CHUNK + 1, LANES * q:LANES * (q + 1)] for q in range(4))
    if conj:
        li0, li1 = -li0, -li1

    def step(i, carry):
        hr0, hi0, hr1, hi1 = carry
        up = pl.multiple_of(i * 8, 8)
        dn = pl.multiple_of((nblk - 1 - i) * 8, 8)
        a_rows, b_rows = (dn, up) if conj else (up, dn)
        s_r0 = src_ref[pl.ds(a_rows, 8), 0:128]
        s_i0 = src_ref[pl.ds(a_rows, 8), 128:256]
        s_r1 = src_ref[pl.ds(b_rows, 8), 256:384]
        s_i1 = src_ref[pl.ds(b_rows, 8), 384:512]
        o_r0, o_i0, o_r1, o_i1 = [], [], [], []
        for k in range(8):
            ka = 7 - k if conj else k
            kb = k if conj else 7 - k
            o_r0.append((ka, hr0))
            o_i0.append((ka, hi0))
            o_r1.append((kb, hr1))
            o_i1.append((kb, hi1))
            hr0, hi0 = (lr0 * hr0 - li0 * hi0 + s_r0[ka:ka + 1], lr0 * hi0 + li0 * hr0 + s_i0[ka:ka + 1])
            hr1, hi1 = (lr1 * hr1 - li1 * hi1 + s_r1[kb:kb + 1], lr1 * hi1 + li1 * hr1 + s_i1[kb:kb + 1])

        def rows(items):
            return jnp.concatenate([v for _, v in sorted(items, key=lambda kv: kv[0])], axis=0)

        dst_ref[pl.ds(a_rows, 8), 0:128] = rows(o_r0)
        dst_ref[pl.ds(a_rows, 8), 128:256] = rows(o_i0)
        dst_ref[pl.ds(b_rows, 8), 256:384] = rows(o_r1)
        dst_ref[pl.ds(b_rows, 8), 384:512] = rows(o_i1)
        return hr0, hi0, hr1, hi1

    z = jnp.zeros((1, LANES), F32)
    lax.fori_loop(0, nblk, step, (z, z, z, z))


def _pw_row(pw_ref, k, q):
    return pw_ref[0, k:k + 1, LANES * q:LANES * (q + 1)]


def _mm_f32(a, b, dims):
    return lax.dot_general(a, b, (dims, ((), ())), precision=HIGHEST, preferred_element_type=F32)


_POW_M = (lambda s: SSM_CHUNK - 1 - s, lambda s: s)
_POW_C = (lambda s: s + 1, lambda s: SSM_CHUNK - s)
_POW_K = (lambda s: s, lambda s: SSM_CHUNK - 1 - s)


def _chunk_matrices(pw_ref, bb_ref, cc_ref, dd_ref, m_scr, ct_scr, toep_scr, g_scr, kt_scr):
    blk = lambda s: slice(PAIR_W * s, PAIR_W * (s + 1))
    col = lambda q: slice(LANES * q, LANES * (q + 1))
    for d in range(2):
        bbr, bbi = bb_ref[0, 2 * d], bb_ref[0, 2 * d + 1]
        ccr, cci = cc_ref[0, 2 * d], cc_ref[0, 2 * d + 1]
        for s in range(SSM_CHUNK):
            pr, pi = _pw_row(pw_ref, _POW_M[d](s), 2 * d), _pw_row(pw_ref, _POW_M[d](s), 2 * d + 1)
            m_scr[blk(s), col(2 * d)] = (pr * bbr - pi * bbi).astype(m_scr.dtype)
            m_scr[blk(s), col(2 * d + 1)] = (pr * bbi + pi * bbr).astype(m_scr.dtype)
            pr, pi = _pw_row(pw_ref, _POW_C[d](s), 2 * d), _pw_row(pw_ref, _POW_C[d](s), 2 * d + 1)
            ct_scr[blk(s), col(2 * d)] = (ccr * pr - cci * pi).astype(ct_scr.dtype)
            ct_scr[blk(s), col(2 * d + 1)] = (-(ccr * pi + cci * pr)).astype(ct_scr.dtype)
            pr, pi = _pw_row(pw_ref, _POW_K[d](s), 2 * d), _pw_row(pw_ref, _POW_K[d](s), 2 * d + 1)
            g_scr[d, blk(s), 0:LANES] = ccr * pr - cci * pi
            g_scr[d, blk(s), LANES:2 * LANES] = -(ccr * pi + cci * pr)
        kt = _mm_f32(jnp.concatenate([bbr, bbi], axis=1), g_scr[d], ((1,), (1,)))
        if d == 0:
            kt = jnp.concatenate([kt[:, 0:LANES] + dd_ref[0], kt[:, LANES:]], axis=1)
        kt_scr[d] = kt
    lane = lax.broadcasted_iota(jnp.int32, (1, CHUNK_W), 1)
    for s in range(SSM_CHUNK):
        lo = PAIR_W * s
        hi = PAIR_W * (s + 1)
        fwd = kt_scr[0] if s == 0 else pltpu.roll(kt_scr[0], lo, axis=1)
        bwd = kt_scr[1] if hi == CHUNK_W else pltpu.roll(kt_scr[1], hi, axis=1)
        row = jnp.where(lane >= lo, fwd, 0.0) + jnp.where(lane < hi, bwd, 0.0)
        toep_scr[blk(s), :] = row.astype(toep_scr.dtype)


def _ssm_scratch(nc, mat_dtype):
    return [pltpu.VMEM((CHUNK_W, CHUNK_W), mat_dtype), pltpu.VMEM((CHUNK_W, CHUNK_W), mat_dtype),
            pltpu.VMEM((CHUNK_W, CHUNK_W), mat_dtype), pltpu.VMEM((2, CHUNK_W, 2 * LANES), F32),
            pltpu.VMEM((2, PAIR_W, CHUNK_W), F32), pltpu.VMEM((nc, CHUNK_W), F32), pltpu.VMEM((nc, CHUNK_W), F32)]


def _per_pair(*shape):
    return pl.BlockSpec((1,) + shape, lambda g: (g,) + (0,) * len(shape))


_TABLE_SPECS = lambda: [_per_pair(PW_ROWS, CHUNK_W), _per_pair(4, PAIR_W, LANES), _per_pair(4, PAIR_W, LANES),
                        _per_pair(PAIR_W, LANES)]


def _ssm_fwd(u, pw, bb, cc, dd):
    npair, nc, width = u.shape

    def body(u_ref, pw_ref, bb_ref, cc_ref, dd_ref, y_ref, hin_ref, m_scr, ct_scr, toep_scr, g_scr, kt_scr, s_scr, h_scr):
        _chunk_matrices(pw_ref, bb_ref, cc_ref, dd_ref, m_scr, ct_scr, toep_scr, g_scr, kt_scr)
        uv = u_ref[0]
        s_scr[...] = _mm(uv, m_scr[...])
        _boundary_scan(nc, pw_ref, s_scr, h_scr, conj=False)
        hin = h_scr[...]
        hin_ref[0] = hin
        y_ref[0] = _mm(uv, toep_scr[...]) + _nt(hin.astype(u.dtype), ct_scr[...])

    return pl.pallas_call(
        body, name="ssm_fwd", grid=(npair,),
        in_specs=[_per_pair(nc, width)] + _TABLE_SPECS(),
        out_specs=[_per_pair(nc, width), _per_pair(nc, width)],
        out_shape=[jax.ShapeDtypeStruct((npair, nc, width), F32), jax.ShapeDtypeStruct((npair, nc, width), F32)],
        scratch_shapes=_ssm_scratch(nc, u.dtype),
        compiler_params=_params("arbitrary"),
    )(u, pw, bb, cc, dd)


def _ssm_bwd(u, dy, hin, pw, bb, cc, dd, chip_parts=()):
    npair, nc, width = u.shape
    ns = len(chip_parts)

    def body(u_ref, dy_ref, hin_ref, pw_ref, bb_ref, cc_ref, dd_ref, *rest):
        part_refs, rest = rest[:ns], rest[ns:]
        (du_ref, dpw_ref, dbb_ref, dcc_ref, ddd_ref), rest = rest[:5], rest[5:]
        land_refs, rest = rest[:ns], rest[ns:]
        m_scr, ct_scr, toep_scr, g_scr, kt_scr, dh_scr, ds_scr = rest[:7]
        if ns:
            g = pl.program_id(0)
            _ride(_chip_copies(part_refs, land_refs, *rest[7:]), g == 0, g == npair - 1)
        _chunk_matrices(pw_ref, bb_ref, cc_ref, dd_ref, m_scr, ct_scr, toep_scr, g_scr, kt_scr)
        uv = u_ref[0]
        dyb = dy_ref[0]
        hin = hin_ref[0]
        dh_scr[...] = _mm(dyb, ct_scr[...])
        _boundary_scan(nc, pw_ref, dh_scr, ds_scr, conj=True)
        ds = ds_scr[...]
        dsb = ds.astype(u.dtype)
        du_ref[0] = _nt(dsb, m_scr[...]) + _nt(dyb, toep_scr[...])
        dm = _tn(uv, dsb)
        dct = _tn(dyb, hin.astype(u.dtype))
        dtoep = _tn(uv, dyb)

        dpw_ref[...] = jnp.zeros_like(dpw_ref)
        blk = lambda s: slice(PAIR_W * s, PAIR_W * (s + 1))
        col = lambda q: slice(LANES * q, LANES * (q + 1))

        def add_pw(k, q, val):
            dpw_ref[0, k:k + 1, col(q)] += jnp.sum(val, axis=0, keepdims=True)

        for d in range(2):
            g_r, g_i = ds[:, col(2 * d)], ds[:, col(2 * d + 1)]
            h_r, h_i = hin[:, col(2 * d)], hin[:, col(2 * d + 1)]
            add_pw(SSM_CHUNK, 2 * d, g_r * h_r + g_i * h_i)
            add_pw(SSM_CHUNK, 2 * d + 1, g_i * h_r - g_r * h_i)

        lane = lax.broadcasted_iota(jnp.int32, (1, CHUNK_W), 1)
        dkt0 = jnp.zeros((PAIR_W, CHUNK_W), F32)
        dkt1 = jnp.zeros((PAIR_W, CHUNK_W), F32)
        for s in range(SSM_CHUNK):
            lo = PAIR_W * s
            hi = PAIR_W * (s + 1)
            row = dtoep[blk(s), :]
            fwd = jnp.where(lane >= lo, row, 0.0)
            bwd = jnp.where(lane < hi, row, 0.0)
            dkt0 += fwd if s == 0 else pltpu.roll(fwd, CHUNK_W - lo, axis=1)
            dkt1 += bwd if hi == CHUNK_W else pltpu.roll(bwd, CHUNK_W - hi, axis=1)
        ddd_ref[0] = dkt0[:, 0:LANES]

        for d, dkt in enumerate((dkt0, dkt1)):
            bbr, bbi = bb_ref[0, 2 * d], bb_ref[0, 2 * d + 1]
            ccr, cci = cc_ref[0, 2 * d], cc_ref[0, 2 * d + 1]
            dbbcat = _mm_f32(dkt, g_scr[d], ((1,), (0,)))
            dg = _mm_f32(dkt, jnp.concatenate([bbr, bbi], axis=1), ((0,), (0,)))
            dbbr, dbbi = dbbcat[:, 0:LANES], dbbcat[:, LANES:]
            dccr = jnp.zeros((PAIR_W, LANES), F32)
            dcci = jnp.zeros((PAIR_W, LANES), F32)
            for s in range(SSM_CHUNK):
                k = _POW_M[d](s)
                pr, pi = _pw_row(pw_ref, k, 2 * d), _pw_row(pw_ref, k, 2 * d + 1)
                gr, gi = dm[blk(s), col(2 * d)], dm[blk(s), col(2 * d + 1)]
                dbbr += gr * pr + gi * pi
                dbbi += gi * pr - gr * pi
                add_pw(k, 2 * d, gr * bbr + gi * bbi)
                add_pw(k, 2 * d + 1, gi * bbr - gr * bbi)
                for k, gr, gi in ((_POW_C[d](s), dct[blk(s), col(2 * d)], dct[blk(s), col(2 * d + 1)]),
                                  (_POW_K[d](s), dg[blk(s), 0:LANES], dg[blk(s), LANES:])):
                    pr, pi = _pw_row(pw_ref, k, 2 * d), _pw_row(pw_ref, k, 2 * d + 1)
                    dccr += gr * pr - gi * pi
                    dcci += -(gr * pi + gi * pr)
                    add_pw(k, 2 * d, gr * ccr - gi * cci)
                    add_pw(k, 2 * d + 1, -(gr * cci + gi * ccr))
            dbb_ref[0, 2 * d] = dbbr
            dbb_ref[0, 2 * d + 1] = dbbi
            dcc_ref[0, 2 * d] = dccr
            dcc_ref[0, 2 * d + 1] = dcci

    out = pl.pallas_call(
        body, name="ssm_bwd", grid=(npair,),
        in_specs=[_per_pair(nc, width), _per_pair(nc, width), _per_pair(nc, width)] + _TABLE_SPECS() + [_ANY()] * ns,
        out_specs=[_per_pair(nc, width)] + _TABLE_SPECS() + [_ANY()] * ns,
        out_shape=[jax.ShapeDtypeStruct((npair, nc, width), F32), jax.ShapeDtypeStruct(pw.shape, F32),
                   jax.ShapeDtypeStruct(bb.shape, F32), jax.ShapeDtypeStruct(cc.shape, F32),
                   jax.ShapeDtypeStruct(dd.shape, F32)]
        + [jax.ShapeDtypeStruct(pt.shape, pt.dtype) for pt in chip_parts],
        scratch_shapes=_ssm_scratch(nc, u.dtype) + ([_dma_sems(ns, 3), _dma_sems(ns, 3), _dma_sems(ns)] if ns else []),
        compiler_params=_params("arbitrary"),
    )(u, dy, hin, pw, bb, cc, dd, *chip_parts)
    return out[:5], list(out[5:])


NA_Q = NA_QROWS * GRID_W
NA_K = NA_KROWS * GRID_W
NA_SCALE = NA_HEAD_DIM ** -0.5


def _na_block(b, nb, rows):
    start = jnp.clip(NA_QROWS * b - NA_ROWS // 2, 0, rows - NA_KROWS) * GRID_W
    kind = jnp.where(b == 0, 0, jnp.where(b == nb - 1, 2, 1))
    return pl.multiple_of(start, GRID_W), kind


NA_CHUNK = 16


def _na_pieces(kind, i):
    ri, q0 = divmod(i * NA_CHUNK, GRID_W)
    off = (NA_ROWS - 1, NA_ROWS // 2 - 1, -1)[kind]
    lo = (0, ri, NA_KROWS - NA_ROWS)[kind]
    modes = {(True, True): 'both', (True, False): 'even', (False, True): 'odd', (False, False): None}
    out = []
    for k2 in range(NA_KROWS // 2):
        inside = tuple(lo <= kr < lo + NA_ROWS for kr in (2 * k2, 2 * k2 + 1))
        out.append((2 * k2 - ri + off + 1, slice(q0, q0 + NA_CHUNK), modes[inside]))
    return out


def _na_softmax_pieces(s_ref, tab_ref, hh, kind, i):
    rows = slice(i * NA_CHUNK, (i + 1) * NA_CHUNK)
    lane = lax.broadcasted_iota(jnp.int32, (1, LANES), 1)
    xs = []
    for k2, (t, q, mode) in enumerate(_na_pieces(kind, i)):
        if mode is None:
            xs.append(None)
            continue
        bias = tab_ref[hh, t, q, :]
        if mode == 'even':
            bias = jnp.where(lane < GRID_W, bias, NEG)
        elif mode == 'odd':
            bias = jnp.where(lane >= GRID_W, bias, NEG)
        xs.append(s_ref[hh, rows, k2 * LANES:(k2 + 1) * LANES] + bias)
    live = [x for x in xs if x is not None]
    m = jnp.max(functools.reduce(jnp.maximum, live), axis=-1, keepdims=True)
    es = [None if x is None else jnp.exp(x - m) for x in xs]
    total = jnp.sum(functools.reduce(jnp.add, [e for e in es if e is not None]), axis=-1, keepdims=True)
    inv = 1.0 / total
    return [None if e is None else e * inv for e in es]


def _na_heads():
    lane = lax.broadcasted_iota(jnp.int32, (1, LANES), 1)
    return [lane < NA_HEAD_DIM, lane >= NA_HEAD_DIM]


def _na_fwd(qkv, bias):
    seq = qkv.shape[0]
    rows = seq // GRID_W
    nb = rows // NA_QROWS

    def body(q_ref, k_ref, v_ref, bias_ref, o_ref, p_ref, s_scr):
        start, kind = _na_block(pl.program_id(1), nb, rows)

        def block(static_kind):
            q2 = q_ref[...] * NA_SCALE
            kw = k_ref[pl.ds(start, NA_K), :]
            vw = v_ref[pl.ds(start, NA_K), :]
            heads = _na_heads()
            for hh in range(2):
                s_scr[hh] = _nt(jnp.where(heads[hh], q2, jnp.zeros_like(q2)), kw)
            for hh in range(2):
                for i in range(NA_Q // NA_CHUNK):
                    r = slice(i * NA_CHUNK, (i + 1) * NA_CHUNK)
                    for k2, p in enumerate(_na_softmax_pieces(s_scr, bias_ref, hh, static_kind, i)):
                        p = jnp.zeros((NA_CHUNK, LANES), F32) if p is None else p
                        p_ref[hh, r, k2 * LANES:(k2 + 1) * LANES] = p.astype(p_ref.dtype)
            o_ref[...] = jnp.where(heads[0], _mm(p_ref[0], vw), _mm(p_ref[1], vw))

        for static_kind in range(3):
            pl.when(kind == static_kind)(functools.partial(block, static_kind))

    return pl.pallas_call(
        body, name="na_fwd", grid=(NA_HEADS // 2, nb),
        in_specs=[pl.BlockSpec((NA_Q, LANES), lambda hp, b: (b, hp)),
                  pl.BlockSpec((seq, LANES), lambda hp, b: (0, 4 + hp)),
                  pl.BlockSpec((seq, LANES), lambda hp, b: (0, 8 + hp)),
                  pl.BlockSpec((2, NA_TAB, GRID_W, LANES), lambda hp, b: (hp, 0, 0, 0))],
        out_specs=[pl.BlockSpec((NA_Q, LANES), lambda hp, b: (b, hp)),
                   pl.BlockSpec((2, NA_Q, NA_K), lambda hp, b: (hp, b, 0))],
        out_shape=[jax.ShapeDtypeStruct((seq, D_NA), F32), jax.ShapeDtypeStruct((NA_HEADS, seq, NA_K), qkv.dtype)],
        scratch_shapes=[pltpu.VMEM((2, NA_Q, NA_K), F32)],
        compiler_params=_params("arbitrary", "arbitrary"),
    )(qkv, qkv, qkv, bias)


def _na_bwd(qkv, do, probs, parts=()):
    seq = qkv.shape[0]
    rows = seq // GRID_W
    nb = rows // NA_QROWS
    ns = len(parts)

    def body(q_ref, k_ref, v_ref, do_ref, p_ref, *rest):
        part_refs, rest = rest[:ns], rest[ns:]
        (dq_ref, dk_ref, dv_ref, dbias_ref), rest = rest[:4], rest[4:]
        land_refs, rest = rest[:ns], rest[ns:]
        dp_scr, ds_scr = rest[:2]
        b = pl.program_id(1)
        start, kind = _na_block(b, nb, rows)
        if ns:
            hp = pl.program_id(0)
            _ride(_pair_copies(part_refs, land_refs, *rest[2:]), (hp == 0) & (b == 0),
                  (hp == NA_HEADS // 2 - 1) & (b == nb - 1))

        @pl.when(b == 0)
        def _():
            dk_ref[...] = jnp.zeros_like(dk_ref)
            dv_ref[...] = jnp.zeros_like(dv_ref)
            dbias_ref[...] = jnp.zeros_like(dbias_ref)

        def block(static_kind):
            q2 = q_ref[...] * NA_SCALE
            kw = k_ref[pl.ds(start, NA_K), :]
            vw = v_ref[pl.ds(start, NA_K), :]
            do2 = do_ref[...].astype(q2.dtype)
            heads = _na_heads()
            col = lambda k2: slice(k2 * LANES, (k2 + 1) * LANES)
            zero = jnp.zeros((NA_CHUNK, LANES), ds_scr.dtype)
            for hh in range(2):
                dp_scr[hh] = _nt(jnp.where(heads[hh], do2, jnp.zeros_like(do2)), vw)
            for hh in range(2):
                for i in range(NA_Q // NA_CHUNK):
                    r = slice(i * NA_CHUNK, (i + 1) * NA_CHUNK)
                    pieces = _na_pieces(static_kind, i)
                    ps = [None if mode is None else p_ref[hh, r, col(k2)].astype(F32)
                          for k2, (_, _, mode) in enumerate(pieces)]
                    dps = [None if p is None else dp_scr[hh, r, col(k2)] for k2, p in enumerate(ps)]
                    pdp = functools.reduce(jnp.add, [p * dp for p, dp in zip(ps, dps) if p is not None])
                    rowsum = jnp.sum(pdp, axis=-1, keepdims=True)
                    for k2, (t, q, _) in enumerate(pieces):
                        if ps[k2] is None:
                            ds_scr[hh, r, col(k2)] = zero
                            continue
                        ds = ps[k2] * (dps[k2] - rowsum)
                        dbias_ref[hh, t, q, :] += ds
                        ds_scr[hh, r, col(k2)] = ds.astype(ds_scr.dtype)
            dq_ref[...] = jnp.where(heads[0], _mm(ds_scr[0], kw), _mm(ds_scr[1], kw)) * NA_SCALE
            dk_ref[pl.ds(start, NA_K), :] += jnp.where(heads[0], _tn(ds_scr[0], q2), _tn(ds_scr[1], q2))
            dv_ref[pl.ds(start, NA_K), :] += jnp.where(heads[0], _tn(p_ref[0], do2), _tn(p_ref[1], do2))

        for static_kind in range(3):
            pl.when(kind == static_kind)(functools.partial(block, static_kind))

    out = pl.pallas_call(
        body, name="na_bwd", grid=(NA_HEADS // 2, nb),
        in_specs=[pl.BlockSpec((NA_Q, LANES), lambda hp, b: (b, hp)),
                  pl.BlockSpec((seq, LANES), lambda hp, b: (0, 4 + hp)),
                  pl.BlockSpec((seq, LANES), lambda hp, b: (0, 8 + hp)),
                  pl.BlockSpec((NA_Q, LANES), lambda hp, b: (b, hp)),
                  pl.BlockSpec((2, NA_Q, NA_K), lambda hp, b: (hp, b, 0))] + [_ANY()] * ns,
        out_specs=[pl.BlockSpec((NA_Q, LANES), lambda hp, b: (b, hp)),
                   pl.BlockSpec((seq, LANES), lambda hp, b: (0, hp)),
                   pl.BlockSpec((seq, LANES), lambda hp, b: (0, hp)),
                   pl.BlockSpec((2, NA_TAB, GRID_W, LANES), lambda hp, b: (hp, 0, 0, 0))] + [_ANY()] * ns,
        out_shape=[jax.ShapeDtypeStruct((seq, D_NA), F32), jax.ShapeDtypeStruct((seq, D_NA), F32),
                   jax.ShapeDtypeStruct((seq, D_NA), F32),
                   jax.ShapeDtypeStruct((NA_HEADS, NA_TAB, GRID_W, LANES), F32)]
        + [jax.ShapeDtypeStruct((4,) + pt.shape[1:], pt.dtype) for pt in parts],
        scratch_shapes=[pltpu.VMEM((2, NA_Q, NA_K), F32), pltpu.VMEM((2, NA_Q, NA_K), qkv.dtype)]
        + ([_dma_sems(ns, 4), _dma_sems(ns, 4)] if ns else []),
        compiler_params=_params("arbitrary", "arbitrary"),
    )(qkv, qkv, qkv, do, probs, *parts)
    return out[:4], list(out[4:])


def _local_step(x2, p2, tgt, g_pre, g_post, w_in_g, ssm, w_glu, b_glu, rpb, w_out, w_ple, g_ple, w_pg,
                distributed=False):
    seq = x2.shape[0]
    nc = seq // SSM_CHUNK

    (pw, bb, cc, dd), ssm_vjp = jax.vjp(_ssm_tables, *ssm)
    bias, bias_vjp = jax.vjp(_na_table, rpb)

    riders = (w_glu, w_out, w_ple, w_pg) if distributed else ()
    (u, zs, qkv, zn, hn), gathered = _fwd_in(x2, g_pre, w_in_g, riders)
    if distributed:
        w_glu, w_out, w_ple, w_pg = gathered
        w_glu = w_glu.reshape(D_SSM, D_SSM)
        w_out = w_out.reshape(D_MODEL, D_MODEL)
        w_pg = w_pg.reshape(D_MODEL, D_MODEL)
    u_p = _to_pairs(u, nc)
    y_p, hin = _ssm_fwd(u_p, pw, bb, cc, dd)
    yssm = _from_pairs(y_p, nc)
    o, probs = _na_fwd(qkv, bias)
    dyssm, dzs, do, dzn, dh1, sq, d_wglu, d_bglu, d_wout, d_wple, d_wpg, d_gpost, d_gple = _mid(
        yssm, zs, o, zn, x2, p2, tgt, w_glu, b_glu, w_out, g_post, w_ple, g_ple, w_pg)
    mid_grads = [d_wglu.reshape(N_DEV, D_SSM // N_DEV, D_SSM), d_wout.reshape(N_DEV, D_MODEL // N_DEV, D_MODEL),
                 d_wple, d_wpg.reshape(N_DEV, D_MODEL // N_DEV, D_MODEL)]
    (dq, dk, dv, dbias), lands = _na_bwd(qkv, do, probs, mid_grads if distributed else ())
    (d_rpb,) = bias_vjp(dbias)
    chip_parts = _pair_sum(mid_grads, lands) if distributed else ()
    (du_p, dpw, dbb, dcc, ddd), chip_lands = _ssm_bwd(u_p, _to_pairs(dyssm, nc), hin, pw, bb, cc, dd, chip_parts)
    d_ssm = ssm_vjp((dpw, dbb, dcc, ddd))
    du = _from_pairs(du_p, nc)
    dx, d_win, d_gpre = _bwd_in(du, dzs, dq, dk, dv, dzn, hn, x2, dh1, g_pre, w_in_g)
    if distributed:
        d_wglu, d_wout, d_wple, d_wpg = chip_lands
    return sq, dx, dict(norm_pre=d_gpre, norm_post=d_gpost, w_in=d_win, ssm=d_ssm, w_glu=d_wglu, b_glu=d_bglu,
                        na_rpb=d_rpb, w_out=d_wout, w_ple=d_wple, ple_norm=d_gple, w_ple_gate=d_wpg)


def _all_gather(shard):
    m_per, n = shard.shape

    def body(x_ref, out_ref, send_sems, recv_sems, local_sem):
        x, y, c = _place()
        me, sibling = (x, y, c), (x, y, 1 - c)
        chips = [(1 - x, y), (x, 1 - y), (1 - x, 1 - y)]

        def rows(px, py, pc):
            return out_ref.at[pl.ds((4 * px + 2 * py + pc) * m_per, m_per), :]

        def copy(k, block, to, src=None):
            return pltpu.make_async_remote_copy(
                src_ref=rows(*block) if src is None else src, dst_ref=rows(*block),
                send_sem=send_sems.at[k], recv_sem=recv_sems.at[k], device_id=to, device_id_type=MESH)

        mine = pltpu.make_async_copy(x_ref, rows(*me), local_sem)
        mine.start()
        first = [copy(0, me, sibling, src=x_ref)]
        first += [copy(1 + j, me, (*chip, c), src=x_ref) for j, chip in enumerate(chips)]
        for cp in first:
            cp.start()
        passed = [copy(4 + j, (*chip, c), sibling) for j, chip in enumerate(chips)]
        for j, chip in enumerate(chips):
            copy(1 + j, (*chip, c), me).wait_recv()
            passed[j].start()
        copy(0, sibling, me).wait_recv()
        for j, chip in enumerate(chips):
            copy(4 + j, (*chip, 1 - c), me).wait_recv()
        for cp in first + passed:
            cp.wait_send()
        mine.wait()

    return pl.pallas_call(
        body, name="all_gather",
        out_shape=jax.ShapeDtypeStruct((N_DEV * m_per, n), shard.dtype),
        in_specs=[_whole_vmem()], out_specs=_whole_vmem(),
        scratch_shapes=[pltpu.SemaphoreType.DMA((7,)), pltpu.SemaphoreType.DMA((7,)), pltpu.SemaphoreType.DMA],
        compiler_params=pltpu.CompilerParams(vmem_limit_bytes=VMEM_LIMIT),
    )(shard)


def _exchange_partials(part):
    def body(part_ref, land_ref, send_sems, recv_sems, local_sem):
        x, y, c = _place()
        me = 4 * x + 2 * y + c
        copies = [pltpu.make_async_copy(part_ref.at[me], land_ref.at[me], local_sem)]
        for k in range(1, N_DEV):
            px, py, pc = x ^ ((k >> 2) & 1), y ^ ((k >> 1) & 1), c ^ (k & 1)
            copies.append(_remote(part_ref.at[4 * px + 2 * py + pc], land_ref.at[me], send_sems.at[k - 1],
                                  recv_sems.at[k - 1], (px, py, pc)))
        _start_all(copies)
        _wait_all(copies)

    return pl.pallas_call(
        body, name="exchange_partials",
        out_shape=jax.ShapeDtypeStruct(part.shape, part.dtype),
        in_specs=[_ANY()], out_specs=_ANY(),
        scratch_shapes=[_dma_sems(N_DEV - 1), _dma_sems(N_DEV - 1), pltpu.SemaphoreType.DMA],
    )(part)


def _pair_exchange(parts):
    ns = len(parts)

    def body(*refs):
        copies = _pair_copies(refs[:ns], refs[ns:2 * ns], *refs[2 * ns:])
        _start_all(copies)
        _wait_all(copies)

    return pl.pallas_call(
        body, name="pair_exchange",
        out_shape=[jax.ShapeDtypeStruct((4,) + pt.shape[1:], pt.dtype) for pt in parts],
        in_specs=[_ANY()] * ns, out_specs=[_ANY()] * ns,
        scratch_shapes=[_dma_sems(ns, 4), _dma_sems(ns, 4)],
    )(*parts)


def _pair_sum(parts, lands):
    ns = len(parts)

    def body(*refs):
        c = lax.axis_index("c")
        for part_ref, land_ref, out_ref in zip(refs[:ns], refs[ns:2 * ns], refs[2 * ns:]):
            for q in range(4):
                out_ref[q] = (part_ref[q, c] + land_ref[q]).astype(BF16)

    return pl.pallas_call(
        body, name="pair_sum",
        in_specs=[_whole_vmem()] * (2 * ns), out_specs=[_whole_vmem()] * ns,
        out_shape=[jax.ShapeDtypeStruct(ld.shape, BF16) for ld in lands],
        compiler_params=pltpu.CompilerParams(vmem_limit_bytes=VMEM_LIMIT),
    )(*[pt.reshape((4, 2) + pt.shape[1:]) for pt in parts], *lands)


def _chip_exchange(chip_parts):
    ns = len(chip_parts)

    def body(*refs):
        copies = _chip_copies(refs[:ns], refs[ns:2 * ns], *refs[2 * ns:])
        _start_all(copies)
        _wait_all(copies)

    return pl.pallas_call(
        body, name="chip_exchange",
        out_shape=[jax.ShapeDtypeStruct(pt.shape, pt.dtype) for pt in chip_parts],
        in_specs=[_ANY()] * ns, out_specs=[_ANY()] * ns,
        scratch_shapes=[_dma_sems(ns, 3), _dma_sems(ns, 3), _dma_sems(ns)],
    )(*chip_parts)


def _sum_slots(land):
    slots, r, n = land.shape
    tile = next(t for t in (512, 256, 128) if r % t == 0)

    def body(land_ref, out_ref):
        acc = land_ref[0].astype(F32)
        for j in range(1, slots):
            acc = acc + land_ref[j].astype(F32)
        out_ref[...] = acc

    return pl.pallas_call(
        body, name="sum_slots", grid=(r // tile,),
        in_specs=[pl.BlockSpec((slots, tile, n), lambda i: (0, i, 0))],
        out_specs=pl.BlockSpec((tile, n), lambda i: (i, 0)),
        out_shape=jax.ShapeDtypeStruct((r, n), F32),
        compiler_params=_params("arbitrary"),
    )(land)


def _adamw(w, g, m, v):
    r, c = w.shape
    tile = 256 if r % 256 == 0 else r
    slots = g.shape[0] if g.ndim == 3 else 0

    def body(w_ref, g_ref, m_ref, v_ref, go_ref, d_ref, nm_ref, nv_ref):
        if slots:
            g = g_ref[0].astype(F32)
            for j in range(1, slots):
                g = g + g_ref[j].astype(F32)
        else:
            g = g_ref[...]
        go_ref[...] = g
        nm = ADAM_B1 * m_ref[...] + (1.0 - ADAM_B1) * g
        nv = ADAM_B2 * v_ref[...] + (1.0 - ADAM_B2) * (g * g)
        m_hat = nm / (1.0 - ADAM_B1 ** ADAM_STEP)
        v_hat = nv / (1.0 - ADAM_B2 ** ADAM_STEP)
        d_ref[...] = -ADAM_LR * (m_hat / (jnp.sqrt(v_hat) + ADAM_EPS) + ADAM_WD * w_ref[...])
        nm_ref[...] = nm
        nv_ref[...] = nv

    spec = pl.BlockSpec((tile, c), lambda i: (i, 0))
    g_spec = pl.BlockSpec((slots, tile, c), lambda i: (0, i, 0)) if slots else spec
    out = jax.ShapeDtypeStruct((r, c), F32)
    return pl.pallas_call(
        body, name="adamw", grid=(r // tile,), in_specs=[spec, g_spec, spec, spec], out_specs=[spec] * 4,
        out_shape=[out] * 4, compiler_params=_params("arbitrary"),
    )(w, g, m, v)


_SLAB = 8 * LANES


def _flat_rows(a):
    flat = a.reshape(-1)
    pad = (-flat.shape[0]) % _SLAB
    if pad:
        flat = jnp.concatenate([flat, jnp.zeros((pad,), flat.dtype)])
    return flat.reshape(-1, LANES)


def _pack(arrays):
    slabs = [_flat_rows(a) for a in arrays]
    return jnp.concatenate(slabs, axis=0), [s.shape[0] for s in slabs]


def _unpack(packed, like, rows):
    out, at = [], 0
    for a, r in zip(like, rows):
        out.append(packed[at:at + r].reshape(-1)[:a.size].reshape(a.shape))
        at += r
    return out


SMALL = ('norm_pre', 'norm_post', 'ssm_a_re', 'ssm_a_im', 'ssm_log_dt', 'ssm_b_re', 'ssm_b_im', 'ssm_c_re',
         'ssm_c_im', 'ssm_d', 'b_glu', 'na_rpb', 'ple_norm')
BIG = ('w_in', 'w_glu', 'w_out', 'w_ple', 'w_ple_gate')
ORDER = ('norm_pre', 'norm_post', 'w_in', 'ssm_a_re', 'ssm_a_im', 'ssm_log_dt', 'ssm_b_re', 'ssm_b_im', 'ssm_c_re',
         'ssm_c_im', 'ssm_d', 'w_glu', 'b_glu', 'na_rpb', 'w_out', 'w_ple', 'ple_norm', 'w_ple_gate')


def kernel(x, p, norm_pre, norm_post, w_in, ssm_a_re, ssm_a_im, ssm_log_dt, ssm_b_re, ssm_b_im, ssm_c_re, ssm_c_im, ssm_d, w_glu, b_glu, na_rpb, w_out, w_ple, ple_norm, w_ple_gate, loss_target, m_norm_pre, m_norm_post, m_w_in, m_ssm_a_re, m_ssm_a_im, m_ssm_log_dt, m_ssm_b_re, m_ssm_b_im, m_ssm_c_re, m_ssm_c_im, m_ssm_d, m_w_glu, m_b_glu, m_na_rpb, m_w_out, m_w_ple, m_ple_norm, m_w_ple_gate, v_norm_pre, v_norm_post, v_w_in, v_ssm_a_re, v_ssm_a_im, v_ssm_log_dt, v_ssm_b_re, v_ssm_b_im, v_ssm_c_re, v_ssm_c_im, v_ssm_d, v_w_glu, v_b_glu, v_na_rpb, v_w_out, v_w_ple, v_ple_norm, v_w_ple_gate):
    args = dict(locals())
    weights = {n: args[n] for n in ORDER}
    mom_m = {n: args["m_" + n] for n in ORDER}
    mom_v = {n: args["v_" + n] for n in ORDER}

    w_in_g = _all_gather(w_in[0].astype(BF16)).reshape(N_DEV, D_MODEL, SHARD_IN)
    blocks = [weights[n][0].astype(BF16) for n in ('w_glu', 'w_out', 'w_ple', 'w_ple_gate')]

    ssm = tuple(weights[n][0] for n in ('ssm_a_re', 'ssm_a_im', 'ssm_log_dt', 'ssm_b_re', 'ssm_b_im',
                                        'ssm_c_re', 'ssm_c_im', 'ssm_d'))
    sq, dx, grads = _local_step(x[0], p[0, 0], loss_target[0], norm_pre, norm_post, w_in_g, ssm, blocks[0], b_glu,
                                na_rpb[0], blocks[1], blocks[2], ple_norm, blocks[3], distributed=True)
    loss_local = (0.5 / D_MODEL * jnp.sum(sq)).reshape(1)

    local = dict(norm_pre=grads['norm_pre'], norm_post=grads['norm_post'], b_glu=grads['b_glu'],
                 na_rpb=grads['na_rpb'][None], ple_norm=grads['ple_norm'])
    for n, g in zip(('ssm_a_re', 'ssm_a_im', 'ssm_log_dt', 'ssm_b_re', 'ssm_b_im', 'ssm_c_re', 'ssm_c_im', 'ssm_d'),
                    grads['ssm']):
        local[n] = g[None]

    d_win = [grads['w_in']]
    grads['w_in'] = _chip_exchange(_pair_sum(d_win, _pair_exchange(d_win)))[0]
    small_flat, small_rows = _pack([local[n] for n in SMALL] + [loss_local])
    pad = (-small_flat.shape[0]) % (128 * N_DEV)
    if pad:
        small_flat = jnp.concatenate([small_flat, jnp.zeros((pad, LANES), F32)], axis=0)
    small_per = small_flat.shape[0] // N_DEV
    small_all = _all_gather(_sum_slots(_exchange_partials(small_flat.reshape(N_DEV, small_per, LANES))))

    outs = {}
    for n in BIG:
        g, d, nm, nv = _adamw(weights[n][0], grads[n], mom_m[n][0], mom_v[n][0])
        outs[n] = (g[None], d[None], nm[None], nv[None])
    like = [weights[n] for n in SMALL]
    w_s, rows_s = _pack(like)
    m_s, _ = _pack([mom_m[n] for n in SMALL])
    v_s, _ = _pack([mom_v[n] for n in SMALL])
    loss = small_all[w_s.shape[0], 0]
    unpacked = [_unpack(a, like, rows_s) for a in _adamw(w_s, small_all[:w_s.shape[0]], m_s, v_s)]
    for i, n in enumerate(SMALL):
        outs[n] = tuple(u[i] for u in unpacked)

    return (loss, dx[None], *[outs[n][0] for n in ORDER], *[outs[n][1] for n in ORDER],
            *[outs[n][2] for n in ORDER], *[outs[n][3] for n in ORDER])
```

```python
import functools

import jax
import jax.numpy as jnp
from jax import lax
from jax.experimental import pallas as pl
from jax.experimental.pallas import tpu as pltpu

F32 = jnp.float32
BF16 = jnp.bfloat16
HIGHEST = lax.Precision.HIGHEST

D_MODEL = 1024
D_PLE = 256
GRID_W = 64
D_SSM = 512
SSM_GROUP = 16
N_GROUPS = 32
N_PAIRS = 16
SSM_STATE = 64
D_NA = 512
NA_HEADS = 8
NA_HEAD_DIM = 64
NA_ROWS = 8
NA_COLS = 16
D_IN_PROJ = 3072
EPS = 1e-6
N_DEV = 8
SHARD_IN = D_IN_PROJ // N_DEV
LANES = 128
SSM_CHUNK = 16
TOK_TILE = 256
NA_QROWS = 4
NA_KROWS = 12
NEG = -1e30
VMEM_LIMIT = 56 * 1024 * 1024

ADAM_LR = 0.001
ADAM_B1 = 0.9
ADAM_B2 = 0.999
ADAM_EPS = 1e-08
ADAM_WD = 0.01
ADAM_STEP = 10

MESH = pl.DeviceIdType.MESH


def _params(*sem):
    return pltpu.CompilerParams(dimension_semantics=sem or None, vmem_limit_bytes=VMEM_LIMIT)


def _whole_vmem():
    return pl.BlockSpec(memory_space=pltpu.VMEM)


def _nt(a, b):
    return lax.dot_general(a, b, (((1,), (1,)), ((), ())), preferred_element_type=F32)


def _tn(a, b):
    return lax.dot_general(a, b, (((0,), (0,)), ((), ())), preferred_element_type=F32)


def _mm(a, b):
    return jnp.dot(a, b, preferred_element_type=F32)


def _sigmoid(x):
    return 1.0 / (1.0 + jnp.exp(-x))


_GELU_C = 0.7978845608028654


def _gelu(x):
    return 0.5 * x * (1.0 + jnp.tanh(_GELU_C * (x + 0.044715 * x * x * x)))


def _gelu_grad(x):
    th = jnp.tanh(_GELU_C * (x + 0.044715 * x * x * x))
    return 0.5 * (1.0 + th) + 0.5 * x * (1.0 - th * th) * _GELU_C * (1.0 + 3.0 * 0.044715 * x * x)


def _ssm_tables(a_re, a_im, log_dt, b_re, b_im, c_re, c_im, d):
    T, P, H = SSM_CHUNK, SSM_STATE, SSM_GROUP
    dt = jnp.exp(log_dt)[..., None]
    xr = dt * a_re
    xi = dt * a_im
    mag = jnp.exp(xr)
    lr = mag * jnp.cos(xi)
    li = mag * jnp.sin(xi)
    den = a_re * a_re + a_im * a_im
    cr = ((lr - 1.0) * a_re + li * a_im) / den
    ci = (li * a_re - (lr - 1.0) * a_im) / den
    bbr = cr[..., None] * b_re - ci[..., None] * b_im
    bbi = cr[..., None] * b_im + ci[..., None] * b_re
    kk = jnp.arange(T + 1, dtype=F32)[:, None, None, None]
    pm = jnp.exp(kk * xr)
    pw = jnp.stack([pm * jnp.cos(kk * xi), pm * jnp.sin(kk * xi)], axis=2)
    pw = pw.reshape(T + 1, 2, 2, N_PAIRS, 2 * P).transpose(3, 0, 1, 2, 4).reshape(N_PAIRS, T + 1, 8 * P)
    pw = jnp.concatenate([pw, jnp.zeros((N_PAIRS, 24 - (T + 1), 8 * P), F32)], axis=1)
    eye2 = jnp.eye(2, dtype=F32)

    def expand(t):
        t = t.transpose(2, 0, 1, 3, 4, 5)
        t = t[:, :, :, :, :, None, :] * eye2[None, None, None, :, None, :, None]
        return t.reshape(N_PAIRS, 4, 2 * H, 2 * P)

    bb = expand(jnp.stack([bbr, bbi], axis=1).reshape(2, 2, N_PAIRS, 2, P, H).transpose(0, 1, 2, 3, 5, 4))
    cc = expand(jnp.stack([c_re, c_im], axis=1).reshape(2, 2, N_PAIRS, 2, H, P))
    dd = d.reshape(N_PAIRS, 2 * H)[:, :, None] * jnp.eye(2 * H, dtype=F32)[None]
    dd = jnp.concatenate([dd, jnp.zeros((N_PAIRS, 2 * H, LANES - 2 * H), F32)], axis=2)
    return pw, bb, cc, dd


NA_TAB = 2 * NA_ROWS


def _na_table(rpb):
    qc = jnp.arange(GRID_W)[:, None, None]
    kc = (jnp.arange(2 * GRID_W) % GRID_W)[None, :, None]
    dc = jnp.arange(2 * NA_COLS - 1)[None, None, :]
    cstart = jnp.clip(qc - NA_COLS // 2, 0, GRID_W - NA_COLS)
    csel = ((kc >= cstart) & (kc < cstart + NA_COLS) & (kc - qc + NA_COLS - 1 == dc)).astype(F32)
    col_ok = jnp.sum(csel, axis=-1) > 0.5
    part = jnp.einsum('hrc,qmc->hrqm', rpb, csel, precision=HIGHEST)
    zero = jnp.zeros_like(part[:, :1])
    odd = jnp.arange(2 * GRID_W) >= GRID_W
    tab = jnp.where(odd, jnp.concatenate([part, zero], axis=1), jnp.concatenate([zero, part], axis=1))
    return jnp.where(col_ok, tab, NEG)


def _place():
    return lax.axis_index("x"), lax.axis_index("y"), lax.axis_index("c")


def _remote(src, dst, send_sem, recv_sem, device):
    return pltpu.make_async_remote_copy(src_ref=src, dst_ref=dst, send_sem=send_sem, recv_sem=recv_sem,
                                        device_id=device, device_id_type=MESH)


def _start_all(copies):
    for cp in copies:
        cp.start()


def _wait_all(copies):
    for cp in copies:
        cp.wait()


def _gather_copies(shard_refs, full_refs, send_sems, recv_sems, local_sems):
    x, y, c = _place()
    me = 4 * x + 2 * y + c
    out = []
    for t, (shard, full) in enumerate(zip(shard_refs, full_refs)):
        out.append(pltpu.make_async_copy(shard, full.at[me], local_sems.at[t]))
        for k in range(1, N_DEV):
            peer = (x ^ ((k >> 2) & 1), y ^ ((k >> 1) & 1), c ^ (k & 1))
            out.append(_remote(shard, full.at[me], send_sems.at[t, k - 1], recv_sems.at[t, k - 1], peer))
    return out


def _pair_copies(part_refs, land_refs, send_sems, recv_sems):
    x, y, c = _place()
    out = []
    for t, (part, land) in enumerate(zip(part_refs, land_refs)):
        for q in range(4):
            out.append(_remote(part.at[2 * q + (1 - c)], land.at[q], send_sems.at[t, q], recv_sems.at[t, q],
                               (x, y, 1 - c)))
    return out


def _chip_copies(part_refs, land_refs, send_sems, recv_sems, local_sems):
    x, y, c = _place()
    mine = 2 * x + y
    out = []
    for t, (part, land) in enumerate(zip(part_refs, land_refs)):
        out.append(pltpu.make_async_copy(part.at[mine], land.at[mine], local_sems.at[t]))
        for k in range(1, 4):
            px, py = x ^ (k >> 1), y ^ (k & 1)
            out.append(_remote(part.at[2 * px + py], land.at[mine], send_sems.at[t, k - 1], recv_sems.at[t, k - 1],
                               (px, py, c)))
    return out


def _alltoall_copies(part_refs, land_refs, send_sems, recv_sems, local_sems):
    x, y, c = _place()
    me = 4 * x + 2 * y + c
    out = []
    for t, (part, land) in enumerate(zip(part_refs, land_refs)):
        out.append(pltpu.make_async_copy(part.at[me], land.at[me], local_sems.at[t]))
        for k in range(1, N_DEV):
            px, py, pc = x ^ ((k >> 2) & 1), y ^ ((k >> 1) & 1), c ^ (k & 1)
            out.append(_remote(part.at[4 * px + 2 * py + pc], land.at[me], send_sems.at[t, k - 1],
                               recv_sems.at[t, k - 1], (px, py, pc)))
    return out


def _ride(copies, first, last):
    pl.when(first)(functools.partial(_start_all, copies))
    pl.when(last)(functools.partial(_wait_all, copies))


_ANY = lambda: pl.BlockSpec(memory_space=pl.ANY)


def _dma_sems(*shape):
    return pltpu.SemaphoreType.DMA(shape)


def _fwd_in(x2, g_pre, w_in_g, shards=()):
    seq = x2.shape[0]
    ns = len(shards)
    steps = seq // TOK_TILE

    def body(x_ref, g_ref, w_ref, *rest):
        shard_refs, rest = rest[:ns], rest[ns:]
        (u_ref, zs_ref, qkv_ref, zn_ref, hn_ref), rest = rest[:5], rest[5:]
        if ns:
            i = pl.program_id(0)
            _ride(_gather_copies(shard_refs, rest[:ns], *rest[ns:]), i == 0, i == steps - 1)
        x = x_ref[...]
        r = lax.rsqrt(jnp.mean(x * x, axis=-1, keepdims=True) + EPS)
        hn = (x * r * g_ref[...]).astype(BF16)
        hn_ref[...] = hn
        for j in range(N_DEV):
            pj = _mm(hn, w_ref[j])
            for i in range(SHARD_IN // LANES):
                blk = (SHARD_IN // LANES) * j + i
                piece = pj[:, i * LANES:(i + 1) * LANES]
                if blk < 4:
                    u_ref[:, blk * LANES:(blk + 1) * LANES] = piece
                elif blk < 8:
                    zs_ref[:, (blk - 4) * LANES:(blk - 3) * LANES] = piece
                elif blk < 20:
                    qkv_ref[:, (blk - 8) * LANES:(blk - 7) * LANES] = piece.astype(BF16)
                else:
                    zn_ref[:, (blk - 20) * LANES:(blk - 19) * LANES] = piece

    tok = lambda w: pl.BlockSpec((TOK_TILE, w), lambda i: (i, 0))
    out = pl.pallas_call(
        body, name="fwd_in", grid=(steps,),
        in_specs=[tok(D_MODEL), pl.BlockSpec((1, D_MODEL), lambda i: (0, 0)), _whole_vmem()] + [_ANY()] * ns,
        out_specs=[tok(D_SSM), tok(D_SSM), tok(3 * D_NA), tok(D_NA), tok(D_MODEL)] + [_ANY()] * ns,
        out_shape=[jax.ShapeDtypeStruct((seq, D_SSM), F32), jax.ShapeDtypeStruct((seq, D_SSM), F32),
                   jax.ShapeDtypeStruct((seq, 3 * D_NA), BF16), jax.ShapeDtypeStruct((seq, D_NA), F32),
                   jax.ShapeDtypeStruct((seq, D_MODEL), BF16)]
        + [jax.ShapeDtypeStruct((N_DEV,) + sh.shape, sh.dtype) for sh in shards],
        scratch_shapes=[_dma_sems(ns, N_DEV - 1), _dma_sems(ns, N_DEV - 1), _dma_sems(ns)] if ns else [],
        compiler_params=_params("arbitrary"),
    )(x2, g_pre, w_in_g, *shards)
    return out[:5], list(out[5:])


def _mid(yssm, zs, o, zn, x2, p2, tgt, w_glu, b_glu, w_out, g_post, w_ple_g, g_ple, w_pg):
    seq = x2.shape[0]

    def body(yssm_ref, zs_ref, o_ref, zn_ref, x_ref, p_ref, tgt_ref, wglu_ref, bglu_ref, wout_ref, gpost_ref,
             wple_ref, gple_ref, wpg_ref,
             dyssm_ref, dzs_ref, do_ref, dzn_ref, dh1_ref, loss_ref, dwglu_ref, dbglu_ref, dwout_ref, dwple_ref,
             dwpg_ref, dgpost_ref, dgple_ref):
        @pl.when(pl.program_id(0) == 0)
        def _():
            for ref in (loss_ref, dwglu_ref, dbglu_ref, dwout_ref, dwple_ref, dwpg_ref, dgpost_ref, dgple_ref):
                ref[...] = jnp.zeros_like(ref)

        yv = yssm_ref[...]
        g1 = _gelu(yv)
        g1b = g1.astype(BF16)
        sg = _sigmoid(_mm(g1b, wglu_ref[...]) + bglu_ref[...])
        zs_v = zs_ref[...]
        s_zs = _sigmoid(zs_v)
        g2 = g1 * sg
        zn_v = zn_ref[...]
        s_zn = _sigmoid(zn_v)
        o_v = o_ref[...]
        cat = jnp.concatenate([g2 * (zs_v * s_zs), o_v * (zn_v * s_zn)], axis=1).astype(BF16)
        mix = _mm(cat, wout_ref[...])
        r2 = lax.rsqrt(jnp.mean(mix * mix, axis=-1, keepdims=True) + EPS)
        n2 = mix * r2
        gpost = gpost_ref[...]
        h1 = x_ref[...] + n2 * gpost
        pb = p_ref[...].astype(BF16)
        epre = jnp.concatenate([_mm(pb, wple_ref[j]) for j in range(N_DEV)], axis=1)
        r3 = lax.rsqrt(jnp.mean(epre * epre, axis=-1, keepdims=True) + EPS)
        n3 = epre * r3
        gple = gple_ref[...]
        e = n3 * gple
        h1b = h1.astype(BF16)
        gate = _sigmoid(_mm(h1b, wpg_ref[...]))
        diff = h1 + gate * e - tgt_ref[...]
        loss_ref[...] += jnp.sum(diff * diff, axis=0, keepdims=True)

        dy = diff * (1.0 / D_MODEL)
        dgp = (dy * e * gate * (1.0 - gate)).astype(BF16)
        de = dy * gate
        dh1 = dy + _nt(dgp, wpg_ref[...])
        dh1_ref[...] = dh1
        dwpg_ref[...] += _tn(h1b, dgp)
        dgple_ref[...] += jnp.sum(de * n3, axis=0, keepdims=True)
        dn3 = de * gple
        depre = (r3 * (dn3 - n3 * jnp.mean(dn3 * n3, axis=-1, keepdims=True))).astype(BF16)
        for j in range(N_DEV):
            dwple_ref[j] += _tn(pb, depre[:, j * LANES:(j + 1) * LANES])
        dgpost_ref[...] += jnp.sum(dh1 * n2, axis=0, keepdims=True)
        dn2 = dh1 * gpost
        dmix = (r2 * (dn2 - n2 * jnp.mean(dn2 * n2, axis=-1, keepdims=True))).astype(BF16)
        dcat = _nt(dmix, wout_ref[...])
        dwout_ref[...] += _tn(cat, dmix)

        dys = dcat[:, :D_SSM]
        dyn = dcat[:, D_SSM:]
        dg2 = dys * (zs_v * s_zs)
        dzs_ref[...] = dys * g2 * (s_zs * (1.0 + zs_v * (1.0 - s_zs)))
        dt = dg2 * g2 * (1.0 - sg)
        dtb = dt.astype(BF16)
        dg1 = dg2 * sg + _nt(dtb, wglu_ref[...])
        dwglu_ref[...] += _tn(g1b, dtb)
        dbglu_ref[...] += jnp.sum(dt, axis=0, keepdims=True)
        dyssm_ref[...] = dg1 * _gelu_grad(yv)
        do_ref[...] = dyn * (zn_v * s_zn)
        dzn_ref[...] = dyn * o_v * (s_zn * (1.0 + zn_v * (1.0 - s_zn)))

    tok = lambda w: pl.BlockSpec((TOK_TILE, w), lambda i: (i, 0))
    row = lambda w: pl.BlockSpec((1, w), lambda i: (0, 0))
    vm = _whole_vmem()
    half = jax.ShapeDtypeStruct((seq, D_SSM), F32)
    gain = jax.ShapeDtypeStruct((1, D_MODEL), F32)
    return pl.pallas_call(
        body, name="mid", grid=(seq // TOK_TILE,),
        in_specs=[tok(D_SSM), tok(D_SSM), tok(D_NA), tok(D_NA), tok(D_MODEL), tok(D_PLE), tok(D_MODEL),
                  vm, row(D_SSM), vm, row(D_MODEL), vm, row(D_MODEL), vm],
        out_specs=[tok(D_SSM), tok(D_SSM), tok(D_NA), tok(D_NA), tok(D_MODEL), vm, vm, vm, vm, vm, vm, vm, vm],
        out_shape=[half, half, half, half, jax.ShapeDtypeStruct((seq, D_MODEL), F32), gain,
                   jax.ShapeDtypeStruct((D_SSM, D_SSM), F32), jax.ShapeDtypeStruct((1, D_SSM), F32),
                   jax.ShapeDtypeStruct((D_MODEL, D_MODEL), F32),
                   jax.ShapeDtypeStruct((N_DEV, D_PLE, LANES), F32),
                   jax.ShapeDtypeStruct((D_MODEL, D_MODEL), F32), gain, gain],
        compiler_params=_params("arbitrary"),
    )(yssm, zs, o, zn, x2, p2, tgt, w_glu, b_glu, w_out, g_post, w_ple_g, g_ple, w_pg)


def _bwd_in(du, dzs, dq, dk, dv, dzn, hn, x2, dh1, g_pre, w_in_g):
    seq = x2.shape[0]

    def body(du_ref, dzs_ref, dq_ref, dk_ref, dv_ref, dzn_ref, hn_ref, x_ref, dh1_ref, g_ref, w_ref,
             dx_ref, dw_ref, dg_ref, dproj_ref):
        @pl.when(pl.program_id(0) == 0)
        def _():
            dw_ref[...] = jnp.zeros_like(dw_ref)
            dg_ref[...] = jnp.zeros_like(dg_ref)

        for k, ref in enumerate((du_ref, dzs_ref, dq_ref, dk_ref, dv_ref, dzn_ref)):
            dproj_ref[:, k * D_SSM:(k + 1) * D_SSM] = ref[...].astype(BF16)
        hn = hn_ref[...]
        dhn = jnp.zeros((TOK_TILE, D_MODEL), F32)
        for j in range(N_DEV):
            dpj = dproj_ref[:, j * SHARD_IN:(j + 1) * SHARD_IN]
            dhn += _nt(dpj, w_ref[j])
            dw_ref[j] += _tn(hn, dpj)
        x = x_ref[...]
        r = lax.rsqrt(jnp.mean(x * x, axis=-1, keepdims=True) + EPS)
        n1 = x * r
        dg_ref[...] += jnp.sum(dhn * n1, axis=0, keepdims=True)
        dn1 = dhn * g_ref[...]
        dx_ref[...] = dh1_ref[...] + r * (dn1 - n1 * jnp.mean(dn1 * n1, axis=-1, keepdims=True))

    tok = lambda w: pl.BlockSpec((TOK_TILE, w), lambda i: (i, 0))
    vm = _whole_vmem()
    return pl.pallas_call(
        body, name="bwd_in", grid=(seq // TOK_TILE,),
        in_specs=[tok(D_SSM)] * 6 + [tok(D_MODEL), tok(D_MODEL), tok(D_MODEL),
                                      pl.BlockSpec((1, D_MODEL), lambda i: (0, 0)), vm],
        out_specs=[tok(D_MODEL), vm, vm],
        out_shape=[jax.ShapeDtypeStruct((seq, D_MODEL), F32),
                   jax.ShapeDtypeStruct((N_DEV, D_MODEL, SHARD_IN), F32),
                   jax.ShapeDtypeStruct((1, D_MODEL), F32)],
        scratch_shapes=[pltpu.VMEM((TOK_TILE, D_IN_PROJ), BF16)],
        compiler_params=_params("arbitrary"),
    )(du, dzs, dq, dk, dv, dzn, hn, x2, dh1, g_pre, w_in_g)


RELAYOUT_CHUNKS = 128
PAIR_W = 2 * SSM_GROUP
PAIRS_PER_BLOCK = LANES // PAIR_W
CHUNK_W = SSM_CHUNK * PAIR_W
PW_ROWS = 24


def _lane_window(lo, width):
    lane = lax.broadcasted_iota(jnp.int32, (1, LANES), 1)
    return (lane >= lo) & (lane < lo + width)


def _to_pairs(a, nc):
    ncb = min(RELAYOUT_CHUNKS, nc)

    def body(x_ref, out_ref):
        xs = [x_ref[pl.ds(s, ncb, stride=SSM_CHUNK), :] for s in range(SSM_CHUNK)]
        for a_ in range(PAIRS_PER_BLOCK):
            for v in range(SSM_CHUNK // PAIRS_PER_BLOCK):
                acc = None
                for i in range(PAIRS_PER_BLOCK):
                    shift = (PAIR_W * (i - a_)) % LANES
                    piece = xs[PAIRS_PER_BLOCK * v + i]
                    piece = pltpu.roll(piece, shift, axis=1) if shift else piece
                    acc = piece if acc is None else jnp.where(_lane_window(PAIR_W * i, PAIR_W), piece, acc)
                out_ref[a_, :, LANES * v:LANES * (v + 1)] = acc.astype(BF16)

    return pl.pallas_call(
        body, name="to_pairs", grid=(D_SSM // LANES, nc // ncb),
        in_specs=[pl.BlockSpec((ncb * SSM_CHUNK, LANES), lambda cl, cb: (cb, cl))],
        out_specs=pl.BlockSpec((PAIRS_PER_BLOCK, ncb, CHUNK_W), lambda cl, cb: (cl, cb, 0)),
        out_shape=jax.ShapeDtypeStruct((N_PAIRS, nc, CHUNK_W), BF16),
        compiler_params=_params("arbitrary", "arbitrary"),
    )(a)


def _from_pairs(a, nc):
    ncb = min(RELAYOUT_CHUNKS, nc)

    def body(y_ref, out_ref):
        for s in range(SSM_CHUNK):
            v, i = divmod(s, PAIRS_PER_BLOCK)
            acc = None
            for a_ in range(PAIRS_PER_BLOCK):
                shift = (PAIR_W * (a_ - i)) % LANES
                piece = y_ref[a_, :, LANES * v:LANES * (v + 1)]
                piece = pltpu.roll(piece, shift, axis=1) if shift else piece
                acc = piece if acc is None else jnp.where(_lane_window(PAIR_W * a_, PAIR_W), piece, acc)
            out_ref[pl.ds(s, ncb, stride=SSM_CHUNK), :] = acc

    return pl.pallas_call(
        body, name="from_pairs", grid=(D_SSM // LANES, nc // ncb),
        in_specs=[pl.BlockSpec((PAIRS_PER_BLOCK, ncb, CHUNK_W), lambda cl, cb: (cl, cb, 0))],
        out_specs=pl.BlockSpec((ncb * SSM_CHUNK, LANES), lambda cl, cb: (cb, cl)),
        out_shape=jax.ShapeDtypeStruct((nc * SSM_CHUNK, D_SSM), F32),
        compiler_params=_params("arbitrary", "arbitrary"),
    )(a)


def _boundary_scan(nc, pw_ref, src_ref, dst_ref, conj):
    nblk = nc // 8
    lr0, li0, lr1, li1 = (pw_ref[0, SSM_CHUNK:SSM_CHUNK + 1, LANES * q:LANES * (q + 1)] for q in range(4))
    if conj:
        li0, li1 = -li0, -li1

    def step(i, carry):
        hr0, hi0, hr1, hi1 = carry
        up = pl.multiple_of(i * 8, 8)
        dn = pl.multiple_of((nblk - 1 - i) * 8, 8)
        a_rows, b_rows = (dn, up) if conj else (up, dn)
        s_r0 = src_ref[pl.ds(a_rows, 8), 0:128]
        s_i0 = src_ref[pl.ds(a_rows, 8), 128:256]
        s_r1 = src_ref[pl.ds(b_rows, 8), 256:384]
        s_i1 = src_ref[pl.ds(b_rows, 8), 384:512]
        o_r0, o_i0, o_r1, o_i1 = [], [], [], []
        for k in range(8):
            ka = 7 - k if conj else k
            kb = k if conj else 7 - k
            o_r0.append((ka, hr0))
            o_i0.append((ka, hi0))
            o_r1.append((kb, hr1))
            o_i1.append((kb, hi1))
            hr0, hi0 = (lr0 * hr0 - li0 * hi0 + s_r0[ka:ka + 1], lr0 * hi0 + li0 * hr0 + s_i0[ka:ka + 1])
            hr1, hi1 = (lr1 * hr1 - li1 * hi1 + s_r1[kb:kb + 1], lr1 * hi1 + li1 * hr1 + s_i1[kb:kb + 1])

        def rows(items):
            return jnp.concatenate([v for _, v in sorted(items, key=lambda kv: kv[0])], axis=0)

        dst_ref[pl.ds(a_rows, 8), 0:128] = rows(o_r0)
        dst_ref[pl.ds(a_rows, 8), 128:256] = rows(o_i0)
        dst_ref[pl.ds(b_rows, 8), 256:384] = rows(o_r1)
        dst_ref[pl.ds(b_rows, 8), 384:512] = rows(o_i1)
        return hr0, hi0, hr1, hi1

    z = jnp.zeros((1, LANES), F32)
    lax.fori_loop(0, nblk, step, (z, z, z, z))


def _pw_row(pw_ref, k, q):
    return pw_ref[0, k:k + 1, LANES * q:LANES * (q + 1)]


def _mm_f32(a, b, dims):
    return lax.dot_general(a, b, (dims, ((), ())), precision=HIGHEST, preferred_element_type=F32)


_POW_M = (lambda s: SSM_CHUNK - 1 - s, lambda s: s)
_POW_C = (lambda s: s + 1, lambda s: SSM_CHUNK - s)
_POW_K = (lambda s: s, lambda s: SSM_CHUNK - 1 - s)


def _chunk_matrices(pw_ref, bb_ref, cc_ref, dd_ref, m_scr, ct_scr, toep_scr, g_scr, kt_scr):
    blk = lambda s: slice(PAIR_W * s, PAIR_W * (s + 1))
    col = lambda q: slice(LANES * q, LANES * (q + 1))
    for d in range(2):
        bbr, bbi = bb_ref[0, 2 * d], bb_ref[0, 2 * d + 1]
        ccr, cci = cc_ref[0, 2 * d], cc_ref[0, 2 * d + 1]
        for s in range(SSM_CHUNK):
            pr, pi = _pw_row(pw_ref, _POW_M[d](s), 2 * d), _pw_row(pw_ref, _POW_M[d](s), 2 * d + 1)
            m_scr[blk(s), col(2 * d)] = (pr * bbr - pi * bbi).astype(m_scr.dtype)
            m_scr[blk(s), col(2 * d + 1)] = (pr * bbi + pi * bbr).astype(m_scr.dtype)
            pr, pi = _pw_row(pw_ref, _POW_C[d](s), 2 * d), _pw_row(pw_ref, _POW_C[d](s), 2 * d + 1)
            ct_scr[blk(s), col(2 * d)] = (ccr * pr - cci * pi).astype(ct_scr.dtype)
            ct_scr[blk(s), col(2 * d + 1)] = (-(ccr * pi + cci * pr)).astype(ct_scr.dtype)
            pr, pi = _pw_row(pw_ref, _POW_K[d](s), 2 * d), _pw_row(pw_ref, _POW_K[d](s), 2 * d + 1)
            g_scr[d, blk(s), 0:LANES] = ccr * pr - cci * pi
            g_scr[d, blk(s), LANES:2 * LANES] = -(ccr * pi + cci * pr)
        kt = _mm_f32(jnp.concatenate([bbr, bbi], axis=1), g_scr[d], ((1,), (1,)))
        if d == 0:
            kt = jnp.concatenate([kt[:, 0:LANES] + dd_ref[0], kt[:, LANES:]], axis=1)
        kt_scr[d] = kt
    lane = lax.broadcasted_iota(jnp.int32, (1, CHUNK_W), 1)
    for s in range(SSM_CHUNK):
        lo = PAIR_W * s
        hi = PAIR_W * (s + 1)
        fwd = kt_scr[0] if s == 0 else pltpu.roll(kt_scr[0], lo, axis=1)
        bwd = kt_scr[1] if hi == CHUNK_W else pltpu.roll(kt_scr[1], hi, axis=1)
        row = jnp.where(lane >= lo, fwd, 0.0) + jnp.where(lane < hi, bwd, 0.0)
        toep_scr[blk(s), :] = row.astype(toep_scr.dtype)


def _ssm_scratch(nc, mat_dtype):
    return [pltpu.VMEM((CHUNK_W, CHUNK_W), mat_dtype), pltpu.VMEM((CHUNK_W, CHUNK_W), mat_dtype),
            pltpu.VMEM((CHUNK_W, CHUNK_W), mat_dtype), pltpu.VMEM((2, CHUNK_W, 2 * LANES), F32),
            pltpu.VMEM((2, PAIR_W, CHUNK_W), F32), pltpu.VMEM((nc, CHUNK_W), F32), pltpu.VMEM((nc, CHUNK_W), F32)]


def _per_pair(*shape):
    return pl.BlockSpec((1,) + shape, lambda g: (g,) + (0,) * len(shape))


_TABLE_SPECS = lambda: [_per_pair(PW_ROWS, CHUNK_W), _per_pair(4, PAIR_W, LANES), _per_pair(4, PAIR_W, LANES),
                        _per_pair(PAIR_W, LANES)]


def _ssm_fwd(u, pw, bb, cc, dd):
    npair, nc, width = u.shape

    def body(u_ref, pw_ref, bb_ref, cc_ref, dd_ref, y_ref, hin_ref, m_scr, ct_scr, toep_scr, g_scr, kt_scr, s_scr, h_scr):
        _chunk_matrices(pw_ref, bb_ref, cc_ref, dd_ref, m_scr, ct_scr, toep_scr, g_scr, kt_scr)
        uv = u_ref[0]
        s_scr[...] = _mm(uv, m_scr[...])
        _boundary_scan(nc, pw_ref, s_scr, h_scr, conj=False)
        hin = h_scr[...]
        hin_ref[0] = hin
        y_ref[0] = _mm(uv, toep_scr[...]) + _nt(hin.astype(u.dtype), ct_scr[...])

    return pl.pallas_call(
        body, name="ssm_fwd", grid=(npair,),
        in_specs=[_per_pair(nc, width)] + _TABLE_SPECS(),
        out_specs=[_per_pair(nc, width), _per_pair(nc, width)],
        out_shape=[jax.ShapeDtypeStruct((npair, nc, width), F32), jax.ShapeDtypeStruct((npair, nc, width), F32)],
        scratch_shapes=_ssm_scratch(nc, u.dtype),
        compiler_params=_params("arbitrary"),
    )(u, pw, bb, cc, dd)


def _ssm_bwd(u, dy, hin, pw, bb, cc, dd, chip_parts=()):
    npair, nc, width = u.shape
    ns = len(chip_parts)

    def body(u_ref, dy_ref, hin_ref, pw_ref, bb_ref, cc_ref, dd_ref, *rest):
        part_refs, rest = rest[:ns], rest[ns:]
        (du_ref, dpw_ref, dbb_ref, dcc_ref, ddd_ref), rest = rest[:5], rest[5:]
        land_refs, rest = rest[:ns], rest[ns:]
        m_scr, ct_scr, toep_scr, g_scr, kt_scr, dh_scr, ds_scr = rest[:7]
        if ns:
            g = pl.program_id(0)
            _ride(_chip_copies(part_refs, land_refs, *rest[7:]), g == 0, g == npair - 1)
        _chunk_matrices(pw_ref, bb_ref, cc_ref, dd_ref, m_scr, ct_scr, toep_scr, g_scr, kt_scr)
        uv = u_ref[0]
        dyb = dy_ref[0]
        hin = hin_ref[0]
        dh_scr[...] = _mm(dyb, ct_scr[...])
        _boundary_scan(nc, pw_ref, dh_scr, ds_scr, conj=True)
        ds = ds_scr[...]
        dsb = ds.astype(u.dtype)
        du_ref[0] = _nt(dsb, m_scr[...]) + _nt(dyb, toep_scr[...])
        dm = _tn(uv, dsb)
        dct = _tn(dyb, hin.astype(u.dtype))
        dtoep = _tn(uv, dyb)

        dpw_ref[...] = jnp.zeros_like(dpw_ref)
        blk = lambda s: slice(PAIR_W * s, PAIR_W * (s + 1))
        col = lambda q: slice(LANES * q, LANES * (q + 1))

        def add_pw(k, q, val):
            dpw_ref[0, k:k + 1, col(q)] += jnp.sum(val, axis=0, keepdims=True)

        for d in range(2):
            g_r, g_i = ds[:, col(2 * d)], ds[:, col(2 * d + 1)]
            h_r, h_i = hin[:, col(2 * d)], hin[:, col(2 * d + 1)]
            add_pw(SSM_CHUNK, 2 * d, g_r * h_r + g_i * h_i)
            add_pw(SSM_CHUNK, 2 * d + 1, g_i * h_r - g_r * h_i)

        lane = lax.broadcasted_iota(jnp.int32, (1, CHUNK_W), 1)
        dkt0 = jnp.zeros((PAIR_W, CHUNK_W), F32)
        dkt1 = jnp.zeros((PAIR_W, CHUNK_W), F32)
        for s in range(SSM_CHUNK):
            lo = PAIR_W * s
            hi = PAIR_W * (s + 1)
            row = dtoep[blk(s), :]
            fwd = jnp.where(lane >= lo, row, 0.0)
            bwd = jnp.where(lane < hi, row, 0.0)
            dkt0 += fwd if s == 0 else pltpu.roll(fwd, CHUNK_W - lo, axis=1)
            dkt1 += bwd if hi == CHUNK_W else pltpu.roll(bwd, CHUNK_W - hi, axis=1)
        ddd_ref[0] = dkt0[:, 0:LANES]

        for d, dkt in enumerate((dkt0, dkt1)):
            bbr, bbi = bb_ref[0, 2 * d], bb_ref[0, 2 * d + 1]
            ccr, cci = cc_ref[0, 2 * d], cc_ref[0, 2 * d + 1]
            dbbcat = _mm_f32(dkt, g_scr[d], ((1,), (0,)))
            dg = _mm_f32(dkt, jnp.concatenate([bbr, bbi], axis=1), ((0,), (0,)))
            dbbr, dbbi = dbbcat[:, 0:LANES], dbbcat[:, LANES:]
            dccr = jnp.zeros((PAIR_W, LANES), F32)
            dcci = jnp.zeros((PAIR_W, LANES), F32)
            for s in range(SSM_CHUNK):
                k = _POW_M[d](s)
                pr, pi = _pw_row(pw_ref, k, 2 * d), _pw_row(pw_ref, k, 2 * d + 1)
                gr, gi = dm[blk(s), col(2 * d)], dm[blk(s), col(2 * d + 1)]
                dbbr += gr * pr + gi * pi
                dbbi += gi * pr - gr * pi
                add_pw(k, 2 * d, gr * bbr + gi * bbi)
                add_pw(k, 2 * d + 1, gi * bbr - gr * bbi)
                for k, gr, gi in ((_POW_C[d](s), dct[blk(s), col(2 * d)], dct[blk(s), col(2 * d + 1)]),
                                  (_POW_K[d](s), dg[blk(s), 0:LANES], dg[blk(s), LANES:])):
                    pr, pi = _pw_row(pw_ref, k, 2 * d), _pw_row(pw_ref, k, 2 * d + 1)
                    dccr += gr * pr - gi * pi
                    dcci += -(gr * pi + gi * pr)
                    add_pw(k, 2 * d, gr * ccr - gi * cci)
                    add_pw(k, 2 * d + 1, -(gr * cci + gi * ccr))
            dbb_ref[0, 2 * d] = dbbr
            dbb_ref[0, 2 * d + 1] = dbbi
            dcc_ref[0, 2 * d] = dccr
            dcc_ref[0, 2 * d + 1] = dcci

    out = pl.pallas_call(
        body, name="ssm_bwd", grid=(npair,),
        in_specs=[_per_pair(nc, width), _per_pair(nc, width), _per_pair(nc, width)] + _TABLE_SPECS() + [_ANY()] * ns,
        out_specs=[_per_pair(nc, width)] + _TABLE_SPECS() + [_ANY()] * ns,
        out_shape=[jax.ShapeDtypeStruct((npair, nc, width), F32), jax.ShapeDtypeStruct(pw.shape, F32),
                   jax.ShapeDtypeStruct(bb.shape, F32), jax.ShapeDtypeStruct(cc.shape, F32),
                   jax.ShapeDtypeStruct(dd.shape, F32)]
        + [jax.ShapeDtypeStruct(pt.shape, pt.dtype) for pt in chip_parts],
        scratch_shapes=_ssm_scratch(nc, u.dtype) + ([_dma_sems(ns, 3), _dma_sems(ns, 3), _dma_sems(ns)] if ns else []),
        compiler_params=_params("arbitrary"),
    )(u, dy, hin, pw, bb, cc, dd, *chip_parts)
    return out[:5], list(out[5:])


NA_Q = NA_QROWS * GRID_W
NA_K = NA_KROWS * GRID_W
NA_SCALE = NA_HEAD_DIM ** -0.5


def _na_block(b, nb, rows):
    start = jnp.clip(NA_QROWS * b - NA_ROWS // 2, 0, rows - NA_KROWS) * GRID_W
    kind = jnp.where(b == 0, 0, jnp.where(b == nb - 1, 2, 1))
    return pl.multiple_of(start, GRID_W), kind


NA_CHUNK = 16


def _na_pieces(kind, i):
    ri, q0 = divmod(i * NA_CHUNK, GRID_W)
    off = (NA_ROWS - 1, NA_ROWS // 2 - 1, -1)[kind]
    lo = (0, ri, NA_KROWS - NA_ROWS)[kind]
    modes = {(True, True): 'both', (True, False): 'even', (False, True): 'odd', (False, False): None}
    out = []
    for k2 in range(NA_KROWS // 2):
        inside = tuple(lo <= kr < lo + NA_ROWS for kr in (2 * k2, 2 * k2 + 1))
        out.append((2 * k2 - ri + off + 1, slice(q0, q0 + NA_CHUNK), modes[inside]))
    return out


def _na_softmax_pieces(s_ref, tab_ref, hh, kind, i):
    rows = slice(i * NA_CHUNK, (i + 1) * NA_CHUNK)
    lane = lax.broadcasted_iota(jnp.int32, (1, LANES), 1)
    xs = []
    for k2, (t, q, mode) in enumerate(_na_pieces(kind, i)):
        if mode is None:
            xs.append(None)
            continue
        bias = tab_ref[hh, t, q, :]
        if mode == 'even':
            bias = jnp.where(lane < GRID_W, bias, NEG)
        elif mode == 'odd':
            bias = jnp.where(lane >= GRID_W, bias, NEG)
        xs.append(s_ref[hh, rows, k2 * LANES:(k2 + 1) * LANES] + bias)
    live = [x for x in xs if x is not None]
    m = jnp.max(functools.reduce(jnp.maximum, live), axis=-1, keepdims=True)
    es = [None if x is None else jnp.exp(x - m) for x in xs]
    total = jnp.sum(functools.reduce(jnp.add, [e for e in es if e is not None]), axis=-1, keepdims=True)
    inv = 1.0 / total
    return [None if e is None else e * inv for e in es]


def _na_heads():
    lane = lax.broadcasted_iota(jnp.int32, (1, LANES), 1)
    return [lane < NA_HEAD_DIM, lane >= NA_HEAD_DIM]


def _na_fwd(qkv, bias):
    seq = qkv.shape[0]
    rows = seq // GRID_W
    nb = rows // NA_QROWS

    def body(q_ref, k_ref, v_ref, bias_ref, o_ref, p_ref, s_scr):
        start, kind = _na_block(pl.program_id(1), nb, rows)

        def block(static_kind):
            q2 = q_ref[...] * NA_SCALE
            kw = k_ref[pl.ds(start, NA_K), :]
            vw = v_ref[pl.ds(start, NA_K), :]
            heads = _na_heads()
            for hh in range(2):
                s_scr[hh] = _nt(jnp.where(heads[hh], q2, jnp.zeros_like(q2)), kw)
            for hh in range(2):
                for i in range(NA_Q // NA_CHUNK):
                    r = slice(i * NA_CHUNK, (i + 1) * NA_CHUNK)
                    for k2, p in enumerate(_na_softmax_pieces(s_scr, bias_ref, hh, static_kind, i)):
                        p = jnp.zeros((NA_CHUNK, LANES), F32) if p is None else p
                        p_ref[hh, r, k2 * LANES:(k2 + 1) * LANES] = p.astype(p_ref.dtype)
            o_ref[...] = jnp.where(heads[0], _mm(p_ref[0], vw), _mm(p_ref[1], vw))

        for static_kind in range(3):
            pl.when(kind == static_kind)(functools.partial(block, static_kind))

    return pl.pallas_call(
        body, name="na_fwd", grid=(NA_HEADS // 2, nb),
        in_specs=[pl.BlockSpec((NA_Q, LANES), lambda hp, b: (b, hp)),
                  pl.BlockSpec((seq, LANES), lambda hp, b: (0, 4 + hp)),
                  pl.BlockSpec((seq, LANES), lambda hp, b: (0, 8 + hp)),
                  pl.BlockSpec((2, NA_TAB, GRID_W, LANES), lambda hp, b: (hp, 0, 0, 0))],
        out_specs=[pl.BlockSpec((NA_Q, LANES), lambda hp, b: (b, hp)),
                   pl.BlockSpec((2, NA_Q, NA_K), lambda hp, b: (hp, b, 0))],
        out_shape=[jax.ShapeDtypeStruct((seq, D_NA), F32), jax.ShapeDtypeStruct((NA_HEADS, seq, NA_K), qkv.dtype)],
        scratch_shapes=[pltpu.VMEM((2, NA_Q, NA_K), F32)],
        compiler_params=_params("arbitrary", "arbitrary"),
    )(qkv, qkv, qkv, bias)


def _na_bwd(qkv, do, probs, parts=()):
    seq = qkv.shape[0]
    rows = seq // GRID_W
    nb = rows // NA_QROWS
    ns = len(parts)

    def body(q_ref, k_ref, v_ref, do_ref, p_ref, *rest):
        part_refs, rest = rest[:ns], rest[ns:]
        (dq_ref, dk_ref, dv_ref, dbias_ref), rest = rest[:4], rest[4:]
        land_refs, rest = rest[:ns], rest[ns:]
        dp_scr, ds_scr = rest[:2]
        b = pl.program_id(1)
        start, kind = _na_block(b, nb, rows)
        if ns:
            hp = pl.program_id(0)
            _ride(_pair_copies(part_refs, land_refs, *rest[2:]), (hp == 0) & (b == 0),
                  (hp == NA_HEADS // 2 - 1) & (b == nb - 1))

        @pl.when(b == 0)
        def _():
            dk_ref[...] = jnp.zeros_like(dk_ref)
            dv_ref[...] = jnp.zeros_like(dv_ref)
            dbias_ref[...] = jnp.zeros_like(dbias_ref)

        def block(static_kind):
            q2 = q_ref[...] * NA_SCALE
            kw = k_ref[pl.ds(start, NA_K), :]
            vw = v_ref[pl.ds(start, NA_K), :]
            do2 = do_ref[...].astype(q2.dtype)
            heads = _na_heads()
            col = lambda k2: slice(k2 * LANES, (k2 + 1) * LANES)
            zero = jnp.zeros((NA_CHUNK, LANES), ds_scr.dtype)
            for hh in range(2):
                dp_scr[hh] = _nt(jnp.where(heads[hh], do2, jnp.zeros_like(do2)), vw)
            for hh in range(2):
                for i in range(NA_Q // NA_CHUNK):
                    r = slice(i * NA_CHUNK, (i + 1) * NA_CHUNK)
                    pieces = _na_pieces(static_kind, i)
                    ps = [None if mode is None else p_ref[hh, r, col(k2)].astype(F32)
                          for k2, (_, _, mode) in enumerate(pieces)]
                    dps = [None if p is None else dp_scr[hh, r, col(k2)] for k2, p in enumerate(ps)]
                    pdp = functools.reduce(jnp.add, [p * dp for p, dp in zip(ps, dps) if p is not None])
                    rowsum = jnp.sum(pdp, axis=-1, keepdims=True)
                    for k2, (t, q, _) in enumerate(pieces):
                        if ps[k2] is None:
                            ds_scr[hh, r, col(k2)] = zero
                            continue
                        ds = ps[k2] * (dps[k2] - rowsum)
                        dbias_ref[hh, t, q, :] += ds
                        ds_scr[hh, r, col(k2)] = ds.astype(ds_scr.dtype)
            dq_ref[...] = jnp.where(heads[0], _mm(ds_scr[0], kw), _mm(ds_scr[1], kw)) * NA_SCALE
            dk_ref[pl.ds(start, NA_K), :] += jnp.where(heads[0], _tn(ds_scr[0], q2), _tn(ds_scr[1], q2))
            dv_ref[pl.ds(start, NA_K), :] += jnp.where(heads[0], _tn(p_ref[0], do2), _tn(p_ref[1], do2))

        for static_kind in range(3):
            pl.when(kind == static_kind)(functools.partial(block, static_kind))

    out = pl.pallas_call(
        body, name="na_bwd", grid=(NA_HEADS // 2, nb),
        in_specs=[pl.BlockSpec((NA_Q, LANES), lambda hp, b: (b, hp)),
                  pl.BlockSpec((seq, LANES), lambda hp, b: (0, 4 + hp)),
                  pl.BlockSpec((seq, LANES), lambda hp, b: (0, 8 + hp)),
                  pl.BlockSpec((NA_Q, LANES), lambda hp, b: (b, hp)),
                  pl.BlockSpec((2, NA_Q, NA_K), lambda hp, b: (hp, b, 0))] + [_ANY()] * ns,
        out_specs=[pl.BlockSpec((NA_Q, LANES), lambda hp, b: (b, hp)),
                   pl.BlockSpec((seq, LANES), lambda hp, b: (0, hp)),
                   pl.BlockSpec((seq, LANES), lambda hp, b: (0, hp)),
                   pl.BlockSpec((2, NA_TAB, GRID_W, LANES), lambda hp, b: (hp, 0, 0, 0))] + [_ANY()] * ns,
        out_shape=[jax.ShapeDtypeStruct((seq, D_NA), F32), jax.ShapeDtypeStruct((seq, D_NA), F32),
                   jax.ShapeDtypeStruct((seq, D_NA), F32),
                   jax.ShapeDtypeStruct((NA_HEADS, NA_TAB, GRID_W, LANES), F32)]
        + [jax.ShapeDtypeStruct((4,) + pt.shape[1:], pt.dtype) for pt in parts],
        scratch_shapes=[pltpu.VMEM((2, NA_Q, NA_K), F32), pltpu.VMEM((2, NA_Q, NA_K), qkv.dtype)]
        + ([_dma_sems(ns, 4), _dma_sems(ns, 4)] if ns else []),
        compiler_params=_params("arbitrary", "arbitrary"),
    )(qkv, qkv, qkv, do, probs, *parts)
    return out[:4], list(out[4:])


def _local_step(x2, p2, tgt, g_pre, g_post, w_in_g, ssm, w_glu, b_glu, rpb, w_out, w_ple, g_ple, w_pg,
                distributed=False):
    seq = x2.shape[0]
    nc = seq // SSM_CHUNK

    (pw, bb, cc, dd), ssm_vjp = jax.vjp(_ssm_tables, *ssm)
    bias, bias_vjp = jax.vjp(_na_table, rpb)

    riders = (w_glu, w_out, w_ple, w_pg) if distributed else ()
    (u, zs, qkv, zn, hn), gathered = _fwd_in(x2, g_pre, w_in_g, riders)
    if distributed:
        w_glu, w_out, w_ple, w_pg = gathered
        w_glu = w_glu.reshape(D_SSM, D_SSM)
        w_out = w_out.reshape(D_MODEL, D_MODEL)
        w_pg = w_pg.reshape(D_MODEL, D_MODEL)
    u_p = _to_pairs(u, nc)
    y_p, hin = _ssm_fwd(u_p, pw, bb, cc, dd)
    yssm = _from_pairs(y_p, nc)
    o, probs = _na_fwd(qkv, bias)
    dyssm, dzs, do, dzn, dh1, sq, d_wglu, d_bglu, d_wout, d_wple, d_wpg, d_gpost, d_gple = _mid(
        yssm, zs, o, zn, x2, p2, tgt, w_glu, b_glu, w_out, g_post, w_ple, g_ple, w_pg)
    mid_grads = [d_wglu.reshape(N_DEV, D_SSM // N_DEV, D_SSM), d_wout.reshape(N_DEV, D_MODEL // N_DEV, D_MODEL),
                 d_wple, d_wpg.reshape(N_DEV, D_MODEL // N_DEV, D_MODEL)]
    (dq, dk, dv, dbias), lands = _na_bwd(qkv, do, probs, mid_grads if distributed else ())
    (d_rpb,) = bias_vjp(dbias)
    chip_parts = _pair_sum(mid_grads, lands)[0] if distributed else ()
    (du_p, dpw, dbb, dcc, ddd), chip_lands = _ssm_bwd(u_p, _to_pairs(dyssm, nc), hin, pw, bb, cc, dd, chip_parts)
    d_ssm = ssm_vjp((dpw, dbb, dcc, ddd))
    du = _from_pairs(du_p, nc)
    dx, d_win, d_gpre = _bwd_in(du, dzs, dq, dk, dv, dzn, hn, x2, dh1, g_pre, w_in_g)
    if distributed:
        d_wglu, d_wout, d_wple, d_wpg = chip_lands
    return sq, dx, dict(norm_pre=d_gpre, norm_post=d_gpost, w_in=d_win, ssm=d_ssm, w_glu=d_wglu, b_glu=d_bglu,
                        na_rpb=d_rpb, w_out=d_wout, w_ple=d_wple, ple_norm=d_gple, w_ple_gate=d_wpg)


def _all_gather(shard):
    m_per, n = shard.shape

    def body(x_ref, out_ref, send_sems, recv_sems, local_sem):
        x, y, c = _place()
        me, sibling = (x, y, c), (x, y, 1 - c)
        chips = [(1 - x, y), (x, 1 - y), (1 - x, 1 - y)]

        def rows(px, py, pc):
            return out_ref.at[pl.ds((4 * px + 2 * py + pc) * m_per, m_per), :]

        def copy(k, block, to, src=None):
            return pltpu.make_async_remote_copy(
                src_ref=rows(*block) if src is None else src, dst_ref=rows(*block),
                send_sem=send_sems.at[k], recv_sem=recv_sems.at[k], device_id=to, device_id_type=MESH)

        mine = pltpu.make_async_copy(x_ref, rows(*me), local_sem)
        mine.start()
        first = [copy(0, me, sibling, src=x_ref)]
        first += [copy(1 + j, me, (*chip, c), src=x_ref) for j, chip in enumerate(chips)]
        for cp in first:
            cp.start()
        passed = [copy(4 + j, (*chip, c), sibling) for j, chip in enumerate(chips)]
        for j, chip in enumerate(chips):
            copy(1 + j, (*chip, c), me).wait_recv()
            passed[j].start()
        copy(0, sibling, me).wait_recv()
        for j, chip in enumerate(chips):
            copy(4 + j, (*chip, 1 - c), me).wait_recv()
        for cp in first + passed:
            cp.wait_send()
        mine.wait()

    return pl.pallas_call(
        body, name="all_gather",
        out_shape=jax.ShapeDtypeStruct((N_DEV * m_per, n), shard.dtype),
        in_specs=[_whole_vmem()], out_specs=_whole_vmem(),
        scratch_shapes=[pltpu.SemaphoreType.DMA((7,)), pltpu.SemaphoreType.DMA((7,)), pltpu.SemaphoreType.DMA],
        compiler_params=pltpu.CompilerParams(vmem_limit_bytes=VMEM_LIMIT),
    )(shard)


def _exchange_first(pair_parts, all_parts):
    n1, n2 = len(pair_parts), len(all_parts)

    def body(*refs):
        ins, refs = refs[:n1 + n2], refs[n1 + n2:]
        outs, sems = refs[:n1 + n2], refs[n1 + n2:]
        copies = (_pair_copies(ins[:n1], outs[:n1], *sems[:2])
                  + _alltoall_copies(ins[n1:], outs[n1:], *sems[2:]))
        _start_all(copies)
        _wait_all(copies)

    out = pl.pallas_call(
        body, name="exchange_first",
        out_shape=[jax.ShapeDtypeStruct((4,) + pt.shape[1:], pt.dtype) for pt in pair_parts]
        + [jax.ShapeDtypeStruct(pt.shape, pt.dtype) for pt in all_parts],
        in_specs=[_ANY()] * (n1 + n2), out_specs=[_ANY()] * (n1 + n2),
        scratch_shapes=[_dma_sems(n1, 4), _dma_sems(n1, 4),
                        _dma_sems(n2, N_DEV - 1), _dma_sems(n2, N_DEV - 1), _dma_sems(n2)],
    )(*pair_parts, *all_parts)
    return list(out[:n1]), list(out[n1:])


def _pair_sum(parts, lands, all_lands=()):
    ns, na = len(parts), len(all_lands)

    def body(*refs):
        c = lax.axis_index("c")
        ins, outs = refs[:2 * ns + na], refs[2 * ns + na:]
        for part_ref, land_ref, out_ref in zip(ins[:ns], ins[ns:2 * ns], outs[:ns]):
            for q in range(4):
                out_ref[q] = (part_ref[q, c] + land_ref[q]).astype(BF16)
        for land_ref, out_ref in zip(ins[2 * ns:], outs[ns:]):
            acc = land_ref[0]
            for j in range(1, N_DEV):
                acc = acc + land_ref[j]
            out_ref[...] = acc

    out = pl.pallas_call(
        body, name="pair_sum",
        in_specs=[_whole_vmem()] * (2 * ns + na), out_specs=[_whole_vmem()] * (ns + na),
        out_shape=[jax.ShapeDtypeStruct(ld.shape, BF16) for ld in lands]
        + [jax.ShapeDtypeStruct(ld.shape[1:], ld.dtype) for ld in all_lands],
        compiler_params=pltpu.CompilerParams(vmem_limit_bytes=VMEM_LIMIT),
    )(*[pt.reshape((4, 2) + pt.shape[1:]) for pt in parts], *lands, *all_lands)
    return list(out[:ns]), list(out[ns:])


def _exchange_second(chip_parts, shards):
    n1, n2 = len(chip_parts), len(shards)

    def body(*refs):
        ins, refs = refs[:n1 + n2], refs[n1 + n2:]
        outs, sems = refs[:n1 + n2], refs[n1 + n2:]
        copies = (_chip_copies(ins[:n1], outs[:n1], *sems[:3])
                  + _gather_copies(ins[n1:], outs[n1:], *sems[3:]))
        _start_all(copies)
        _wait_all(copies)

    out = pl.pallas_call(
        body, name="exchange_second",
        out_shape=[jax.ShapeDtypeStruct(pt.shape, pt.dtype) for pt in chip_parts]
        + [jax.ShapeDtypeStruct((N_DEV,) + sh.shape, sh.dtype) for sh in shards],
        in_specs=[_ANY()] * (n1 + n2), out_specs=[_ANY()] * (n1 + n2),
        scratch_shapes=[_dma_sems(n1, 3), _dma_sems(n1, 3), _dma_sems(n1),
                        _dma_sems(n2, N_DEV - 1), _dma_sems(n2, N_DEV - 1), _dma_sems(n2)],
    )(*chip_parts, *shards)
    return list(out[:n1]), list(out[n1:])


def _adamw(w, g, m, v):
    r, c = w.shape
    tile = 256 if r % 256 == 0 else r
    slots = g.shape[0] if g.ndim == 3 else 0

    def body(w_ref, g_ref, m_ref, v_ref, go_ref, d_ref, nm_ref, nv_ref):
        if slots:
            g = g_ref[0].astype(F32)
            for j in range(1, slots):
                g = g + g_ref[j].astype(F32)
        else:
            g = g_ref[...]
        go_ref[...] = g
        nm = ADAM_B1 * m_ref[...] + (1.0 - ADAM_B1) * g
        nv = ADAM_B2 * v_ref[...] + (1.0 - ADAM_B2) * (g * g)
        m_hat = nm / (1.0 - ADAM_B1 ** ADAM_STEP)
        v_hat = nv / (1.0 - ADAM_B2 ** ADAM_STEP)
        d_ref[...] = -ADAM_LR * (m_hat / (jnp.sqrt(v_hat) + ADAM_EPS) + ADAM_WD * w_ref[...])
        nm_ref[...] = nm
        nv_ref[...] = nv

    spec = pl.BlockSpec((tile, c), lambda i: (i, 0))
    g_spec = pl.BlockSpec((slots, tile, c), lambda i: (0, i, 0)) if slots else spec
    out = jax.ShapeDtypeStruct((r, c), F32)
    return pl.pallas_call(
        body, name="adamw", grid=(r // tile,), in_specs=[spec, g_spec, spec, spec], out_specs=[spec] * 4,
        out_shape=[out] * 4, compiler_params=_params("arbitrary"),
    )(w, g, m, v)


_SLAB = 8 * LANES


def _flat_rows(a):
    flat = a.reshape(-1)
    pad = (-flat.shape[0]) % _SLAB
    if pad:
        flat = jnp.concatenate([flat, jnp.zeros((pad,), flat.dtype)])
    return flat.reshape(-1, LANES)


def _pack(arrays):
    slabs = [_flat_rows(a) for a in arrays]
    return jnp.concatenate(slabs, axis=0), [s.shape[0] for s in slabs]


def _unpack(packed, like, rows):
    out, at = [], 0
    for a, r in zip(like, rows):
        out.append(packed[at:at + r].reshape(-1)[:a.size].reshape(a.shape))
        at += r
    return out


SMALL = ('norm_pre', 'norm_post', 'ssm_a_re', 'ssm_a_im', 'ssm_log_dt', 'ssm_b_re', 'ssm_b_im', 'ssm_c_re',
         'ssm_c_im', 'ssm_d', 'b_glu', 'na_rpb', 'ple_norm')
BIG = ('w_in', 'w_glu', 'w_out', 'w_ple', 'w_ple_gate')
ORDER = ('norm_pre', 'norm_post', 'w_in', 'ssm_a_re', 'ssm_a_im', 'ssm_log_dt', 'ssm_b_re', 'ssm_b_im', 'ssm_c_re',
         'ssm_c_im', 'ssm_d', 'w_glu', 'b_glu', 'na_rpb', 'w_out', 'w_ple', 'ple_norm', 'w_ple_gate')


def kernel(x, p, norm_pre, norm_post, w_in, ssm_a_re, ssm_a_im, ssm_log_dt, ssm_b_re, ssm_b_im, ssm_c_re, ssm_c_im, ssm_d, w_glu, b_glu, na_rpb, w_out, w_ple, ple_norm, w_ple_gate, loss_target, m_norm_pre, m_norm_post, m_w_in, m_ssm_a_re, m_ssm_a_im, m_ssm_log_dt, m_ssm_b_re, m_ssm_b_im, m_ssm_c_re, m_ssm_c_im, m_ssm_d, m_w_glu, m_b_glu, m_na_rpb, m_w_out, m_w_ple, m_ple_norm, m_w_ple_gate, v_norm_pre, v_norm_post, v_w_in, v_ssm_a_re, v_ssm_a_im, v_ssm_log_dt, v_ssm_b_re, v_ssm_b_im, v_ssm_c_re, v_ssm_c_im, v_ssm_d, v_w_glu, v_b_glu, v_na_rpb, v_w_out, v_w_ple, v_ple_norm, v_w_ple_gate):
    args = dict(locals())
    weights = {n: args[n] for n in ORDER}
    mom_m = {n: args["m_" + n] for n in ORDER}
    mom_v = {n: args["v_" + n] for n in ORDER}

    w_in_g = _all_gather(w_in[0].astype(BF16)).reshape(N_DEV, D_MODEL, SHARD_IN)
    blocks = [weights[n][0].astype(BF16) for n in ('w_glu', 'w_out', 'w_ple', 'w_ple_gate')]

    ssm = tuple(weights[n][0] for n in ('ssm_a_re', 'ssm_a_im', 'ssm_log_dt', 'ssm_b_re', 'ssm_b_im',
                                        'ssm_c_re', 'ssm_c_im', 'ssm_d'))
    sq, dx, grads = _local_step(x[0], p[0, 0], loss_target[0], norm_pre, norm_post, w_in_g, ssm, blocks[0], b_glu,
                                na_rpb[0], blocks[1], blocks[2], ple_norm, blocks[3], distributed=True)
    loss_local = (0.5 / D_MODEL * jnp.sum(sq)).reshape(1)

    local = dict(norm_pre=grads['norm_pre'], norm_post=grads['norm_post'], b_glu=grads['b_glu'],
                 na_rpb=grads['na_rpb'][None], ple_norm=grads['ple_norm'])
    for n, g in zip(('ssm_a_re', 'ssm_a_im', 'ssm_log_dt', 'ssm_b_re', 'ssm_b_im', 'ssm_c_re', 'ssm_c_im', 'ssm_d'),
                    grads['ssm']):
        local[n] = g[None]

    small_flat, small_rows = _pack([local[n] for n in SMALL] + [loss_local])
    pad = (-small_flat.shape[0]) % (8 * N_DEV)
    if pad:
        small_flat = jnp.concatenate([small_flat, jnp.zeros((pad, LANES), F32)], axis=0)
    small_per = small_flat.shape[0] // N_DEV
    d_win = [grads['w_in']]
    lands, small_lands = _exchange_first(d_win, [small_flat.reshape(N_DEV, small_per, LANES)])
    chip_parts, small_sums = _pair_sum(d_win, lands, small_lands)
    chip_lands, small_full = _exchange_second(chip_parts, small_sums)
    grads['w_in'] = chip_lands[0]
    small_all = small_full[0].reshape(N_DEV * small_per, LANES)

    outs = {}
    for n in BIG:
        g, d, nm, nv = _adamw(weights[n][0], grads[n], mom_m[n][0], mom_v[n][0])
        outs[n] = (g[None], d[None], nm[None], nv[None])
    like = [weights[n] for n in SMALL]
    w_s, rows_s = _pack(like)
    m_s, _ = _pack([mom_m[n] for n in SMALL])
    v_s, _ = _pack([mom_v[n] for n in SMALL])
    loss = small_all[w_s.shape[0], 0]
    unpacked = [_unpack(a, like, rows_s) for a in _adamw(w_s, small_all[:w_s.shape[0]], m_s, v_s)]
    for i, n in enumerate(SMALL):
        outs[n] = tuple(u[i] for u in unpacked)

    return (loss, dx[None], *[outs[n][0] for n in ORDER], *[outs[n][1] for n in ORDER],
            *[outs[n][2] for n in ORDER], *[outs[n][3] for n in ORDER])
```

```python
import functools

import jax
import jax.numpy as jnp
from jax import lax
from jax.experimental import pallas as pl
from jax.experimental.pallas import tpu as pltpu

F32 = jnp.float32
BF16 = jnp.bfloat16
HIGHEST = lax.Precision.HIGHEST

D_MODEL = 1024
D_PLE = 256
GRID_W = 64
D_SSM = 512
SSM_GROUP = 16
N_GROUPS = 32
N_PAIRS = 16
SSM_STATE = 64
D_NA = 512
NA_HEADS = 8
NA_HEAD_DIM = 64
NA_ROWS = 8
NA_COLS = 16
D_IN_PROJ = 3072
EPS = 1e-6
N_DEV = 8
SHARD_IN = D_IN_PROJ // N_DEV
LANES = 128
SSM_CHUNK = 16
TOK_TILE = 256
IN_TILE = 512
NA_QROWS = 4
NA_KROWS = 12
NEG = -1e30
VMEM_LIMIT = 56 * 1024 * 1024

ADAM_LR = 0.001
ADAM_B1 = 0.9
ADAM_B2 = 0.999
ADAM_EPS = 1e-08
ADAM_WD = 0.01
ADAM_STEP = 10

MESH = pl.DeviceIdType.MESH


def _params(*sem):
    return pltpu.CompilerParams(dimension_semantics=sem or None, vmem_limit_bytes=VMEM_LIMIT)


def _whole_vmem():
    return pl.BlockSpec(memory_space=pltpu.VMEM)


def _nt(a, b):
    return lax.dot_general(a, b, (((1,), (1,)), ((), ())), preferred_element_type=F32)


def _tn(a, b):
    return lax.dot_general(a, b, (((0,), (0,)), ((), ())), preferred_element_type=F32)


def _mm(a, b):
    return jnp.dot(a, b, preferred_element_type=F32)


def _sigmoid(x):
    return 1.0 / (1.0 + jnp.exp(-x))


_GELU_C = 0.7978845608028654


def _gelu(x):
    return 0.5 * x * (1.0 + jnp.tanh(_GELU_C * (x + 0.044715 * x * x * x)))


def _gelu_grad(x):
    th = jnp.tanh(_GELU_C * (x + 0.044715 * x * x * x))
    return 0.5 * (1.0 + th) + 0.5 * x * (1.0 - th * th) * _GELU_C * (1.0 + 3.0 * 0.044715 * x * x)


def _ssm_tables(a_re, a_im, log_dt, b_re, b_im, c_re, c_im, d):
    T, P, H = SSM_CHUNK, SSM_STATE, SSM_GROUP
    dt = jnp.exp(log_dt)[..., None]
    xr = dt * a_re
    xi = dt * a_im
    mag = jnp.exp(xr)
    lr = mag * jnp.cos(xi)
    li = mag * jnp.sin(xi)
    den = a_re * a_re + a_im * a_im
    cr = ((lr - 1.0) * a_re + li * a_im) / den
    ci = (li * a_re - (lr - 1.0) * a_im) / den
    bbr = cr[..., None] * b_re - ci[..., None] * b_im
    bbi = cr[..., None] * b_im + ci[..., None] * b_re
    kk = jnp.arange(T + 1, dtype=F32)[:, None, None, None]
    pm = jnp.exp(kk * xr)
    pw = jnp.stack([pm * jnp.cos(kk * xi), pm * jnp.sin(kk * xi)], axis=2)
    pw = pw.reshape(T + 1, 2, 2, N_PAIRS, 2 * P).transpose(3, 0, 1, 2, 4).reshape(N_PAIRS, T + 1, 8 * P)
    pw = jnp.concatenate([pw, jnp.zeros((N_PAIRS, 24 - (T + 1), 8 * P), F32)], axis=1)
    eye2 = jnp.eye(2, dtype=F32)

    def expand(t):
        t = t.transpose(2, 0, 1, 3, 4, 5)
        t = t[:, :, :, :, :, None, :] * eye2[None, None, None, :, None, :, None]
        return t.reshape(N_PAIRS, 4, 2 * H, 2 * P)

    bb = expand(jnp.stack([bbr, bbi], axis=1).reshape(2, 2, N_PAIRS, 2, P, H).transpose(0, 1, 2, 3, 5, 4))
    cc = expand(jnp.stack([c_re, c_im], axis=1).reshape(2, 2, N_PAIRS, 2, H, P))
    dd = d.reshape(N_PAIRS, 2 * H)[:, :, None] * jnp.eye(2 * H, dtype=F32)[None]
    dd = jnp.concatenate([dd, jnp.zeros((N_PAIRS, 2 * H, LANES - 2 * H), F32)], axis=2)
    return pw, bb, cc, dd


NA_TAB = 2 * NA_ROWS


def _na_table(rpb):
    qc = jnp.arange(GRID_W)[:, None, None]
    kc = (jnp.arange(2 * GRID_W) % GRID_W)[None, :, None]
    dc = jnp.arange(2 * NA_COLS - 1)[None, None, :]
    cstart = jnp.clip(qc - NA_COLS // 2, 0, GRID_W - NA_COLS)
    csel = ((kc >= cstart) & (kc < cstart + NA_COLS) & (kc - qc + NA_COLS - 1 == dc)).astype(F32)
    col_ok = jnp.sum(csel, axis=-1) > 0.5
    part = jnp.einsum('hrc,qmc->hrqm', rpb, csel, precision=HIGHEST)
    zero = jnp.zeros_like(part[:, :1])
    odd = jnp.arange(2 * GRID_W) >= GRID_W
    tab = jnp.where(odd, jnp.concatenate([part, zero], axis=1), jnp.concatenate([zero, part], axis=1))
    return jnp.where(col_ok, tab, NEG)


def _place():
    return lax.axis_index("x"), lax.axis_index("y"), lax.axis_index("c")


def _remote(src, dst, send_sem, recv_sem, device):
    return pltpu.make_async_remote_copy(src_ref=src, dst_ref=dst, send_sem=send_sem, recv_sem=recv_sem,
                                        device_id=device, device_id_type=MESH)


def _start_all(copies):
    for cp in copies:
        cp.start()


def _wait_all(copies):
    for cp in copies:
        cp.wait()


def _gather_copies(shard_refs, full_refs, send_sems, recv_sems, local_sems):
    x, y, c = _place()
    me = 4 * x + 2 * y + c
    out = []
    for t, (shard, full) in enumerate(zip(shard_refs, full_refs)):
        out.append(pltpu.make_async_copy(shard, full.at[me], local_sems.at[t]))
        for k in range(1, N_DEV):
            peer = (x ^ ((k >> 2) & 1), y ^ ((k >> 1) & 1), c ^ (k & 1))
            out.append(_remote(shard, full.at[me], send_sems.at[t, k - 1], recv_sems.at[t, k - 1], peer))
    return out


def _pair_copies(part_refs, land_refs, send_sems, recv_sems):
    x, y, c = _place()
    out = []
    for t, (part, land) in enumerate(zip(part_refs, land_refs)):
        for q in range(4):
            out.append(_remote(part.at[2 * q + (1 - c)], land.at[q], send_sems.at[t, q], recv_sems.at[t, q],
                               (x, y, 1 - c)))
    return out


def _chip_copies(part_refs, land_refs, send_sems, recv_sems, local_sems):
    x, y, c = _place()
    mine = 2 * x + y
    out = []
    for t, (part, land) in enumerate(zip(part_refs, land_refs)):
        out.append(pltpu.make_async_copy(part.at[mine], land.at[mine], local_sems.at[t]))
        for k in range(1, 4):
            px, py = x ^ (k >> 1), y ^ (k & 1)
            out.append(_remote(part.at[2 * px + py], land.at[mine], send_sems.at[t, k - 1], recv_sems.at[t, k - 1],
                               (px, py, c)))
    return out


def _alltoall_copies(part_refs, land_refs, send_sems, recv_sems, local_sems):
    x, y, c = _place()
    me = 4 * x + 2 * y + c
    out = []
    for t, (part, land) in enumerate(zip(part_refs, land_refs)):
        out.append(pltpu.make_async_copy(part.at[me], land.at[me], local_sems.at[t]))
        for k in range(1, N_DEV):
            px, py, pc = x ^ ((k >> 2) & 1), y ^ ((k >> 1) & 1), c ^ (k & 1)
            out.append(_remote(part.at[4 * px + 2 * py + pc], land.at[me], send_sems.at[t, k - 1],
                               recv_sems.at[t, k - 1], (px, py, pc)))
    return out


def _ride(copies, first, last):
    pl.when(first)(functools.partial(_start_all, copies))
    pl.when(last)(functools.partial(_wait_all, copies))


_ANY = lambda: pl.BlockSpec(memory_space=pl.ANY)


def _dma_sems(*shape):
    return pltpu.SemaphoreType.DMA(shape)


def _fwd_in(x2, g_pre, w_in_g, shards=()):
    seq = x2.shape[0]
    ns = len(shards)
    steps = seq // TOK_TILE

    def body(x_ref, g_ref, w_ref, *rest):
        shard_refs, rest = rest[:ns], rest[ns:]
        (u_ref, zs_ref, qkv_ref, zn_ref, hn_ref), rest = rest[:5], rest[5:]
        if ns:
            i = pl.program_id(0)
            _ride(_gather_copies(shard_refs, rest[:ns], *rest[ns:]), i == 0, i == steps - 1)
        x = x_ref[...]
        r = lax.rsqrt(jnp.mean(x * x, axis=-1, keepdims=True) + EPS)
        hn = (x * r * g_ref[...]).astype(BF16)
        hn_ref[...] = hn
        for j in range(N_DEV):
            pj = _mm(hn, w_ref[j])
            for i in range(SHARD_IN // LANES):
                blk = (SHARD_IN // LANES) * j + i
                piece = pj[:, i * LANES:(i + 1) * LANES]
                if blk < 4:
                    u_ref[:, blk * LANES:(blk + 1) * LANES] = piece
                elif blk < 8:
                    zs_ref[:, (blk - 4) * LANES:(blk - 3) * LANES] = piece
                elif blk < 20:
                    qkv_ref[:, (blk - 8) * LANES:(blk - 7) * LANES] = piece.astype(BF16)
                else:
                    zn_ref[:, (blk - 20) * LANES:(blk - 19) * LANES] = piece

    tok = lambda w: pl.BlockSpec((TOK_TILE, w), lambda i: (i, 0))
    out = pl.pallas_call(
        body, name="fwd_in", grid=(steps,),
        in_specs=[tok(D_MODEL), pl.BlockSpec((1, D_MODEL), lambda i: (0, 0)), _whole_vmem()] + [_ANY()] * ns,
        out_specs=[tok(D_SSM), tok(D_SSM), tok(3 * D_NA), tok(D_NA), tok(D_MODEL)] + [_ANY()] * ns,
        out_shape=[jax.ShapeDtypeStruct((seq, D_SSM), F32), jax.ShapeDtypeStruct((seq, D_SSM), F32),
                   jax.ShapeDtypeStruct((seq, 3 * D_NA), BF16), jax.ShapeDtypeStruct((seq, D_NA), F32),
                   jax.ShapeDtypeStruct((seq, D_MODEL), BF16)]
        + [jax.ShapeDtypeStruct((N_DEV,) + sh.shape, sh.dtype) for sh in shards],
        scratch_shapes=[_dma_sems(ns, N_DEV - 1), _dma_sems(ns, N_DEV - 1), _dma_sems(ns)] if ns else [],
        compiler_params=_params("arbitrary"),
    )(x2, g_pre, w_in_g, *shards)
    return out[:5], list(out[5:])


def _mid(yssm, zs, o, zn, x2, p2, tgt, w_glu, b_glu, w_out, g_post, w_ple_g, g_ple, w_pg):
    seq = x2.shape[0]

    def body(yssm_ref, zs_ref, o_ref, zn_ref, x_ref, p_ref, tgt_ref, wglu_ref, bglu_ref, wout_ref, gpost_ref,
             wple_ref, gple_ref, wpg_ref,
             dyssm_ref, dzs_ref, do_ref, dzn_ref, dh1_ref, loss_ref, dwglu_ref, dbglu_ref, dwout_ref, dwple_ref,
             dwpg_ref, dgpost_ref, dgple_ref):
        @pl.when(pl.program_id(0) == 0)
        def _():
            for ref in (loss_ref, dwglu_ref, dbglu_ref, dwout_ref, dwple_ref, dwpg_ref, dgpost_ref, dgple_ref):
                ref[...] = jnp.zeros_like(ref)

        yv = yssm_ref[...]
        g1 = _gelu(yv)
        g1b = g1.astype(BF16)
        sg = _sigmoid(_mm(g1b, wglu_ref[...]) + bglu_ref[...])
        zs_v = zs_ref[...]
        s_zs = _sigmoid(zs_v)
        g2 = g1 * sg
        zn_v = zn_ref[...]
        s_zn = _sigmoid(zn_v)
        o_v = o_ref[...]
        cat = jnp.concatenate([g2 * (zs_v * s_zs), o_v * (zn_v * s_zn)], axis=1).astype(BF16)
        mix = _mm(cat, wout_ref[...])
        r2 = lax.rsqrt(jnp.mean(mix * mix, axis=-1, keepdims=True) + EPS)
        n2 = mix * r2
        gpost = gpost_ref[...]
        h1 = x_ref[...] + n2 * gpost
        pb = p_ref[...].astype(BF16)
        epre = jnp.concatenate([_mm(pb, wple_ref[j]) for j in range(N_DEV)], axis=1)
        r3 = lax.rsqrt(jnp.mean(epre * epre, axis=-1, keepdims=True) + EPS)
        n3 = epre * r3
        gple = gple_ref[...]
        e = n3 * gple
        h1b = h1.astype(BF16)
        gate = _sigmoid(_mm(h1b, wpg_ref[...]))
        diff = h1 + gate * e - tgt_ref[...]
        loss_ref[...] += jnp.sum(diff * diff, axis=0, keepdims=True)

        dy = diff * (1.0 / D_MODEL)
        dgp = (dy * e * gate * (1.0 - gate)).astype(BF16)
        de = dy * gate
        dh1 = dy + _nt(dgp, wpg_ref[...])
        dh1_ref[...] = dh1
        dwpg_ref[...] += _tn(h1b, dgp)
        dgple_ref[...] += jnp.sum(de * n3, axis=0, keepdims=True)
        dn3 = de * gple
        depre = (r3 * (dn3 - n3 * jnp.mean(dn3 * n3, axis=-1, keepdims=True))).astype(BF16)
        for j in range(N_DEV):
            dwple_ref[j] += _tn(pb, depre[:, j * LANES:(j + 1) * LANES])
        dgpost_ref[...] += jnp.sum(dh1 * n2, axis=0, keepdims=True)
        dn2 = dh1 * gpost
        dmix = (r2 * (dn2 - n2 * jnp.mean(dn2 * n2, axis=-1, keepdims=True))).astype(BF16)
        dcat = _nt(dmix, wout_ref[...])
        dwout_ref[...] += _tn(cat, dmix)

        dys = dcat[:, :D_SSM]
        dyn = dcat[:, D_SSM:]
        dg2 = dys * (zs_v * s_zs)
        dzs_ref[...] = dys * g2 * (s_zs * (1.0 + zs_v * (1.0 - s_zs)))
        dt = dg2 * g2 * (1.0 - sg)
        dtb = dt.astype(BF16)
        dg1 = dg2 * sg + _nt(dtb, wglu_ref[...])
        dwglu_ref[...] += _tn(g1b, dtb)
        dbglu_ref[...] += jnp.sum(dt, axis=0, keepdims=True)
        dyssm_ref[...] = dg1 * _gelu_grad(yv)
        do_ref[...] = dyn * (zn_v * s_zn)
        dzn_ref[...] = dyn * o_v * (s_zn * (1.0 + zn_v * (1.0 - s_zn)))

    tok = lambda w: pl.BlockSpec((TOK_TILE, w), lambda i: (i, 0))
    row = lambda w: pl.BlockSpec((1, w), lambda i: (0, 0))
    vm = _whole_vmem()
    half = jax.ShapeDtypeStruct((seq, D_SSM), F32)
    gain = jax.ShapeDtypeStruct((1, D_MODEL), F32)
    return pl.pallas_call(
        body, name="mid", grid=(seq // TOK_TILE,),
        in_specs=[tok(D_SSM), tok(D_SSM), tok(D_NA), tok(D_NA), tok(D_MODEL), tok(D_PLE), tok(D_MODEL),
                  vm, row(D_SSM), vm, row(D_MODEL), vm, row(D_MODEL), vm],
        out_specs=[tok(D_SSM), tok(D_SSM), tok(D_NA), tok(D_NA), tok(D_MODEL), vm, vm, vm, vm, vm, vm, vm, vm],
        out_shape=[half, half, half, half, jax.ShapeDtypeStruct((seq, D_MODEL), F32), gain,
                   jax.ShapeDtypeStruct((D_SSM, D_SSM), F32), jax.ShapeDtypeStruct((1, D_SSM), F32),
                   jax.ShapeDtypeStruct((D_MODEL, D_MODEL), F32),
                   jax.ShapeDtypeStruct((N_DEV, D_PLE, LANES), F32),
                   jax.ShapeDtypeStruct((D_MODEL, D_MODEL), F32), gain, gain],
        compiler_params=_params("arbitrary"),
    )(yssm, zs, o, zn, x2, p2, tgt, w_glu, b_glu, w_out, g_post, w_ple_g, g_ple, w_pg)


def _bwd_in(du, dzs, dq, dk, dv, dzn, hn, x2, dh1, g_pre, w_in_g):
    seq = x2.shape[0]

    def body(du_ref, dzs_ref, dq_ref, dk_ref, dv_ref, dzn_ref, hn_ref, x_ref, dh1_ref, g_ref, w_ref,
             dx_ref, dw_ref, dg_ref, dproj_ref):
        @pl.when(pl.program_id(0) == 0)
        def _():
            dw_ref[...] = jnp.zeros_like(dw_ref)
            dg_ref[...] = jnp.zeros_like(dg_ref)

        for k, ref in enumerate((du_ref, dzs_ref, dq_ref, dk_ref, dv_ref, dzn_ref)):
            dproj_ref[:, k * D_SSM:(k + 1) * D_SSM] = ref[...].astype(BF16)
        hn = hn_ref[...]
        dhn = jnp.zeros((IN_TILE, D_MODEL), F32)
        for j in range(N_DEV):
            dpj = dproj_ref[:, j * SHARD_IN:(j + 1) * SHARD_IN]
            dhn += _nt(dpj, w_ref[j])
            dw_ref[j] += _tn(hn, dpj)
        x = x_ref[...]
        r = lax.rsqrt(jnp.mean(x * x, axis=-1, keepdims=True) + EPS)
        n1 = x * r
        dg_ref[...] += jnp.sum(dhn * n1, axis=0, keepdims=True)
        dn1 = dhn * g_ref[...]
        dx_ref[...] = dh1_ref[...] + r * (dn1 - n1 * jnp.mean(dn1 * n1, axis=-1, keepdims=True))

    tok = lambda w: pl.BlockSpec((IN_TILE, w), lambda i: (i, 0))
    vm = _whole_vmem()
    return pl.pallas_call(
        body, name="bwd_in", grid=(seq // IN_TILE,),
        in_specs=[tok(D_SSM)] * 6 + [tok(D_MODEL), tok(D_MODEL), tok(D_MODEL),
                                      pl.BlockSpec((1, D_MODEL), lambda i: (0, 0)), vm],
        out_specs=[tok(D_MODEL), vm, vm],
        out_shape=[jax.ShapeDtypeStruct((seq, D_MODEL), F32),
                   jax.ShapeDtypeStruct((N_DEV, D_MODEL, SHARD_IN), F32),
                   jax.ShapeDtypeStruct((1, D_MODEL), F32)],
        scratch_shapes=[pltpu.VMEM((IN_TILE, D_IN_PROJ), BF16)],
        compiler_params=_params("arbitrary"),
    )(du, dzs, dq, dk, dv, dzn, hn, x2, dh1, g_pre, w_in_g)


RELAYOUT_CHUNKS = 128
PAIR_W = 2 * SSM_GROUP
PAIRS_PER_BLOCK = LANES // PAIR_W
CHUNK_W = SSM_CHUNK * PAIR_W
PW_ROWS = 24


def _lane_window(lo, width):
    lane = lax.broadcasted_iota(jnp.int32, (1, LANES), 1)
    return (lane >= lo) & (lane < lo + width)


def _to_pairs(a, nc):
    ncb = min(RELAYOUT_CHUNKS, nc)

    def body(x_ref, out_ref):
        xs = [x_ref[pl.ds(s, ncb, stride=SSM_CHUNK), :] for s in range(SSM_CHUNK)]
        for a_ in range(PAIRS_PER_BLOCK):
            for v in range(SSM_CHUNK // PAIRS_PER_BLOCK):
                acc = None
                for i in range(PAIRS_PER_BLOCK):
                    shift = (PAIR_W * (i - a_)) % LANES
                    piece = xs[PAIRS_PER_BLOCK * v + i]
                    piece = pltpu.roll(piece, shift, axis=1) if shift else piece
                    acc = piece if acc is None else jnp.where(_lane_window(PAIR_W * i, PAIR_W), piece, acc)
                out_ref[a_, :, LANES * v:LANES * (v + 1)] = acc.astype(BF16)

    return pl.pallas_call(
        body, name="to_pairs", grid=(D_SSM // LANES, nc // ncb),
        in_specs=[pl.BlockSpec((ncb * SSM_CHUNK, LANES), lambda cl, cb: (cb, cl))],
        out_specs=pl.BlockSpec((PAIRS_PER_BLOCK, ncb, CHUNK_W), lambda cl, cb: (cl, cb, 0)),
        out_shape=jax.ShapeDtypeStruct((N_PAIRS, nc, CHUNK_W), BF16),
        compiler_params=_params("arbitrary", "arbitrary"),
    )(a)


def _from_pairs(a, nc):
    ncb = min(RELAYOUT_CHUNKS, nc)

    def body(y_ref, out_ref):
        for s in range(SSM_CHUNK):
            v, i = divmod(s, PAIRS_PER_BLOCK)
            acc = None
            for a_ in range(PAIRS_PER_BLOCK):
                shift = (PAIR_W * (a_ - i)) % LANES
                piece = y_ref[a_, :, LANES * v:LANES * (v + 1)]
                piece = pltpu.roll(piece, shift, axis=1) if shift else piece
                acc = piece if acc is None else jnp.where(_lane_window(PAIR_W * a_, PAIR_W), piece, acc)
            out_ref[pl.ds(s, ncb, stride=SSM_CHUNK), :] = acc

    return pl.pallas_call(
        body, name="from_pairs", grid=(D_SSM // LANES, nc // ncb),
        in_specs=[pl.BlockSpec((PAIRS_PER_BLOCK, ncb, CHUNK_W), lambda cl, cb: (cl, cb, 0))],
        out_specs=pl.BlockSpec((ncb * SSM_CHUNK, LANES), lambda cl, cb: (cb, cl)),
        out_shape=jax.ShapeDtypeStruct((nc * SSM_CHUNK, D_SSM), F32),
        compiler_params=_params("arbitrary", "arbitrary"),
    )(a)


def _boundary_scan(nc, pw_ref, src_ref, dst_ref, conj):
    nblk = nc // 8
    lr0, li0, lr1, li1 = (pw_ref[0, SSM_CHUNK:SSM_CHUNK + 1, LANES * q:LANES * (q + 1)] for q in range(4))
    if conj:
        li0, li1 = -li0, -li1

    def step(i, carry):
        hr0, hi0, hr1, hi1 = carry
        up = pl.multiple_of(i * 8, 8)
        dn = pl.multiple_of((nblk - 1 - i) * 8, 8)
        a_rows, b_rows = (dn, up) if conj else (up, dn)
        s_r0 = src_ref[pl.ds(a_rows, 8), 0:128]
        s_i0 = src_ref[pl.ds(a_rows, 8), 128:256]
        s_r1 = src_ref[pl.ds(b_rows, 8), 256:384]
        s_i1 = src_ref[pl.ds(b_rows, 8), 384:512]
        o_r0, o_i0, o_r1, o_i1 = [], [], [], []
        for k in range(8):
            ka = 7 - k if conj else k
            kb = k if conj else 7 - k
            o_r0.append((ka, hr0))
            o_i0.append((ka, hi0))
            o_r1.append((kb, hr1))
            o_i1.append((kb, hi1))
            hr0, hi0 = (lr0 * hr0 - li0 * hi0 + s_r0[ka:ka + 1], lr0 * hi0 + li0 * hr0 + s_i0[ka:ka + 1])
            hr1, hi1 = (lr1 * hr1 - li1 * hi1 + s_r1[kb:kb + 1], lr1 * hi1 + li1 * hr1 + s_i1[kb:kb + 1])

        def rows(items):
            return jnp.concatenate([v for _, v in sorted(items, key=lambda kv: kv[0])], axis=0)

        dst_ref[pl.ds(a_rows, 8), 0:128] = rows(o_r0)
        dst_ref[pl.ds(a_rows, 8), 128:256] = rows(o_i0)
        dst_ref[pl.ds(b_rows, 8), 256:384] = rows(o_r1)
        dst_ref[pl.ds(b_rows, 8), 384:512] = rows(o_i1)
        return hr0, hi0, hr1, hi1

    z = jnp.zeros((1, LANES), F32)
    lax.fori_loop(0, nblk, step, (z, z, z, z))


def _pw_row(pw_ref, k, q):
    return pw_ref[0, k:k + 1, LANES * q:LANES * (q + 1)]


def _mm_f32(a, b, dims):
    return lax.dot_general(a, b, (dims, ((), ())), precision=HIGHEST, preferred_element_type=F32)


_POW_M = (lambda s: SSM_CHUNK - 1 - s, lambda s: s)
_POW_C = (lambda s: s + 1, lambda s: SSM_CHUNK - s)
_POW_K = (lambda s: s, lambda s: SSM_CHUNK - 1 - s)


def _chunk_matrices(pw_ref, bb_ref, cc_ref, dd_ref, m_scr, ct_scr, toep_scr, g_scr, kt_scr):
    blk = lambda s: slice(PAIR_W * s, PAIR_W * (s + 1))
    col = lambda q: slice(LANES * q, LANES * (q + 1))
    for d in range(2):
        bbr, bbi = bb_ref[0, 2 * d], bb_ref[0, 2 * d + 1]
        ccr, cci = cc_ref[0, 2 * d], cc_ref[0, 2 * d + 1]
        for s in range(SSM_CHUNK):
            pr, pi = _pw_row(pw_ref, _POW_M[d](s), 2 * d), _pw_row(pw_ref, _POW_M[d](s), 2 * d + 1)
            m_scr[blk(s), col(2 * d)] = (pr * bbr - pi * bbi).astype(m_scr.dtype)
            m_scr[blk(s), col(2 * d + 1)] = (pr * bbi + pi * bbr).astype(m_scr.dtype)
            pr, pi = _pw_row(pw_ref, _POW_C[d](s), 2 * d), _pw_row(pw_ref, _POW_C[d](s), 2 * d + 1)
            ct_scr[blk(s), col(2 * d)] = (ccr * pr - cci * pi).astype(ct_scr.dtype)
            ct_scr[blk(s), col(2 * d + 1)] = (-(ccr * pi + cci * pr)).astype(ct_scr.dtype)
            pr, pi = _pw_row(pw_ref, _POW_K[d](s), 2 * d), _pw_row(pw_ref, _POW_K[d](s), 2 * d + 1)
            g_scr[d, blk(s), 0:LANES] = ccr * pr - cci * pi
            g_scr[d, blk(s), LANES:2 * LANES] = -(ccr * pi + cci * pr)
        kt = _mm_f32(jnp.concatenate([bbr, bbi], axis=1), g_scr[d], ((1,), (1,)))
        if d == 0:
            kt = jnp.concatenate([kt[:, 0:LANES] + dd_ref[0], kt[:, LANES:]], axis=1)
        kt_scr[d] = kt
    lane = lax.broadcasted_iota(jnp.int32, (1, CHUNK_W), 1)
    for s in range(SSM_CHUNK):
        lo = PAIR_W * s
        hi = PAIR_W * (s + 1)
        fwd = kt_scr[0] if s == 0 else pltpu.roll(kt_scr[0], lo, axis=1)
        bwd = kt_scr[1] if hi == CHUNK_W else pltpu.roll(kt_scr[1], hi, axis=1)
        row = jnp.where(lane >= lo, fwd, 0.0) + jnp.where(lane < hi, bwd, 0.0)
        toep_scr[blk(s), :] = row.astype(toep_scr.dtype)


def _ssm_scratch(nc, mat_dtype):
    return [pltpu.VMEM((CHUNK_W, CHUNK_W), mat_dtype), pltpu.VMEM((CHUNK_W, CHUNK_W), mat_dtype),
            pltpu.VMEM((CHUNK_W, CHUNK_W), mat_dtype), pltpu.VMEM((2, CHUNK_W, 2 * LANES), F32),
            pltpu.VMEM((2, PAIR_W, CHUNK_W), F32), pltpu.VMEM((nc, CHUNK_W), F32), pltpu.VMEM((nc, CHUNK_W), F32)]


def _per_pair(*shape):
    return pl.BlockSpec((1,) + shape, lambda g: (g,) + (0,) * len(shape))


_TABLE_SPECS = lambda: [_per_pair(PW_ROWS, CHUNK_W), _per_pair(4, PAIR_W, LANES), _per_pair(4, PAIR_W, LANES),
                        _per_pair(PAIR_W, LANES)]


def _ssm_fwd(u, pw, bb, cc, dd):
    npair, nc, width = u.shape

    def body(u_ref, pw_ref, bb_ref, cc_ref, dd_ref, y_ref, hin_ref, m_scr, ct_scr, toep_scr, g_scr, kt_scr, s_scr, h_scr):
        _chunk_matrices(pw_ref, bb_ref, cc_ref, dd_ref, m_scr, ct_scr, toep_scr, g_scr, kt_scr)
        uv = u_ref[0]
        s_scr[...] = _mm(uv, m_scr[...])
        _boundary_scan(nc, pw_ref, s_scr, h_scr, conj=False)
        hin = h_scr[...]
        hin_ref[0] = hin
        y_ref[0] = _mm(uv, toep_scr[...]) + _nt(hin.astype(u.dtype), ct_scr[...])

    return pl.pallas_call(
        body, name="ssm_fwd", grid=(npair,),
        in_specs=[_per_pair(nc, width)] + _TABLE_SPECS(),
        out_specs=[_per_pair(nc, width), _per_pair(nc, width)],
        out_shape=[jax.ShapeDtypeStruct((npair, nc, width), F32), jax.ShapeDtypeStruct((npair, nc, width), F32)],
        scratch_shapes=_ssm_scratch(nc, u.dtype),
        compiler_params=_params("arbitrary"),
    )(u, pw, bb, cc, dd)


def _ssm_bwd(u, dy, hin, pw, bb, cc, dd, chip_parts=()):
    npair, nc, width = u.shape
    ns = len(chip_parts)

    def body(u_ref, dy_ref, hin_ref, pw_ref, bb_ref, cc_ref, dd_ref, *rest):
        part_refs, rest = rest[:ns], rest[ns:]
        (du_ref, dpw_ref, dbb_ref, dcc_ref, ddd_ref), rest = rest[:5], rest[5:]
        land_refs, rest = rest[:ns], rest[ns:]
        m_scr, ct_scr, toep_scr, g_scr, kt_scr, dh_scr, ds_scr = rest[:7]
        if ns:
            g = pl.program_id(0)
            _ride(_chip_copies(part_refs, land_refs, *rest[7:]), g == 0, g == npair - 1)
        _chunk_matrices(pw_ref, bb_ref, cc_ref, dd_ref, m_scr, ct_scr, toep_scr, g_scr, kt_scr)
        uv = u_ref[0]
        dyb = dy_ref[0]
        hin = hin_ref[0]
        dh_scr[...] = _mm(dyb, ct_scr[...])
        _boundary_scan(nc, pw_ref, dh_scr, ds_scr, conj=True)
        ds = ds_scr[...]
        dsb = ds.astype(u.dtype)
        du_ref[0] = _nt(dsb, m_scr[...]) + _nt(dyb, toep_scr[...])
        dm = _tn(uv, dsb)
        dct = _tn(dyb, hin.astype(u.dtype))
        dtoep = _tn(uv, dyb)

        dpw_ref[...] = jnp.zeros_like(dpw_ref)
        blk = lambda s: slice(PAIR_W * s, PAIR_W * (s + 1))
        col = lambda q: slice(LANES * q, LANES * (q + 1))

        def add_pw(k, q, val):
            dpw_ref[0, k:k + 1, col(q)] += jnp.sum(val, axis=0, keepdims=True)

        for d in range(2):
            g_r, g_i = ds[:, col(2 * d)], ds[:, col(2 * d + 1)]
            h_r, h_i = hin[:, col(2 * d)], hin[:, col(2 * d + 1)]
            add_pw(SSM_CHUNK, 2 * d, g_r * h_r + g_i * h_i)
            add_pw(SSM_CHUNK, 2 * d + 1, g_i * h_r - g_r * h_i)

        lane = lax.broadcasted_iota(jnp.int32, (1, CHUNK_W), 1)
        dkt0 = jnp.zeros((PAIR_W, CHUNK_W), F32)
        dkt1 = jnp.zeros((PAIR_W, CHUNK_W), F32)
        for s in range(SSM_CHUNK):
            lo = PAIR_W * s
            hi = PAIR_W * (s + 1)
            row = dtoep[blk(s), :]
            fwd = jnp.where(lane >= lo, row, 0.0)
            bwd = jnp.where(lane < hi, row, 0.0)
            dkt0 += fwd if s == 0 else pltpu.roll(fwd, CHUNK_W - lo, axis=1)
            dkt1 += bwd if hi == CHUNK_W else pltpu.roll(bwd, CHUNK_W - hi, axis=1)
        ddd_ref[0] = dkt0[:, 0:LANES]

        for d, dkt in enumerate((dkt0, dkt1)):
            bbr, bbi = bb_ref[0, 2 * d], bb_ref[0, 2 * d + 1]
            ccr, cci = cc_ref[0, 2 * d], cc_ref[0, 2 * d + 1]
            dbbcat = _mm_f32(dkt, g_scr[d], ((1,), (0,)))
            dg = _mm_f32(dkt, jnp.concatenate([bbr, bbi], axis=1), ((0,), (0,)))
            dbbr, dbbi = dbbcat[:, 0:LANES], dbbcat[:, LANES:]
            dccr = jnp.zeros((PAIR_W, LANES), F32)
            dcci = jnp.zeros((PAIR_W, LANES), F32)
            for s in range(SSM_CHUNK):
                k = _POW_M[d](s)
                pr, pi = _pw_row(pw_ref, k, 2 * d), _pw_row(pw_ref, k, 2 * d + 1)
                gr, gi = dm[blk(s), col(2 * d)], dm[blk(s), col(2 * d + 1)]
                dbbr += gr * pr + gi * pi
                dbbi += gi * pr - gr * pi
                add_pw(k, 2 * d, gr * bbr + gi * bbi)
                add_pw(k, 2 * d + 1, gi * bbr - gr * bbi)
                for k, gr, gi in ((_POW_C[d](s), dct[blk(s), col(2 * d)], dct[blk(s), col(2 * d + 1)]),
                                  (_POW_K[d](s), dg[blk(s), 0:LANES], dg[blk(s), LANES:])):
                    pr, pi = _pw_row(pw_ref, k, 2 * d), _pw_row(pw_ref, k, 2 * d + 1)
                    dccr += gr * pr - gi * pi
                    dcci += -(gr * pi + gi * pr)
                    add_pw(k, 2 * d, gr * ccr - gi * cci)
                    add_pw(k, 2 * d + 1, -(gr * cci + gi * ccr))
            dbb_ref[0, 2 * d] = dbbr
            dbb_ref[0, 2 * d + 1] = dbbi
            dcc_ref[0, 2 * d] = dccr
            dcc_ref[0, 2 * d + 1] = dcci

    out = pl.pallas_call(
        body, name="ssm_bwd", grid=(npair,),
        in_specs=[_per_pair(nc, width), _per_pair(nc, width), _per_pair(nc, width)] + _TABLE_SPECS() + [_ANY()] * ns,
        out_specs=[_per_pair(nc, width)] + _TABLE_SPECS() + [_ANY()] * ns,
        out_shape=[jax.ShapeDtypeStruct((npair, nc, width), F32), jax.ShapeDtypeStruct(pw.shape, F32),
                   jax.ShapeDtypeStruct(bb.shape, F32), jax.ShapeDtypeStruct(cc.shape, F32),
                   jax.ShapeDtypeStruct(dd.shape, F32)]
        + [jax.ShapeDtypeStruct(pt.shape, pt.dtype) for pt in chip_parts],
        scratch_shapes=_ssm_scratch(nc, u.dtype) + ([_dma_sems(ns, 3), _dma_sems(ns, 3), _dma_sems(ns)] if ns else []),
        compiler_params=_params("arbitrary"),
    )(u, dy, hin, pw, bb, cc, dd, *chip_parts)
    return out[:5], list(out[5:])


NA_Q = NA_QROWS * GRID_W
NA_K = NA_KROWS * GRID_W
NA_SCALE = NA_HEAD_DIM ** -0.5


def _na_block(b, nb, rows):
    start = jnp.clip(NA_QROWS * b - NA_ROWS // 2, 0, rows - NA_KROWS) * GRID_W
    kind = jnp.where(b == 0, 0, jnp.where(b == nb - 1, 2, 1))
    return pl.multiple_of(start, GRID_W), kind


NA_CHUNK = 16


def _na_pieces(kind, i):
    ri, q0 = divmod(i * NA_CHUNK, GRID_W)
    off = (NA_ROWS - 1, NA_ROWS // 2 - 1, -1)[kind]
    lo = (0, ri, NA_KROWS - NA_ROWS)[kind]
    modes = {(True, True): 'both', (True, False): 'even', (False, True): 'odd', (False, False): None}
    out = []
    for k2 in range(NA_KROWS // 2):
        inside = tuple(lo <= kr < lo + NA_ROWS for kr in (2 * k2, 2 * k2 + 1))
        out.append((2 * k2 - ri + off + 1, slice(q0, q0 + NA_CHUNK), modes[inside]))
    return out


def _na_softmax_pieces(s_ref, tab_ref, hh, kind, i):
    rows = slice(i * NA_CHUNK, (i + 1) * NA_CHUNK)
    lane = lax.broadcasted_iota(jnp.int32, (1, LANES), 1)
    xs = []
    for k2, (t, q, mode) in enumerate(_na_pieces(kind, i)):
        if mode is None:
            xs.append(None)
            continue
        bias = tab_ref[hh, t, q, :]
        if mode == 'even':
            bias = jnp.where(lane < GRID_W, bias, NEG)
        elif mode == 'odd':
            bias = jnp.where(lane >= GRID_W, bias, NEG)
        xs.append(s_ref[hh, rows, k2 * LANES:(k2 + 1) * LANES] + bias)
    live = [x for x in xs if x is not None]
    m = jnp.max(functools.reduce(jnp.maximum, live), axis=-1, keepdims=True)
    es = [None if x is None else jnp.exp(x - m) for x in xs]
    total = jnp.sum(functools.reduce(jnp.add, [e for e in es if e is not None]), axis=-1, keepdims=True)
    inv = 1.0 / total
    return [None if e is None else e * inv for e in es]


def _na_heads():
    lane = lax.broadcasted_iota(jnp.int32, (1, LANES), 1)
    return [lane < NA_HEAD_DIM, lane >= NA_HEAD_DIM]


def _na_fwd(qkv, bias):
    seq = qkv.shape[0]
    rows = seq // GRID_W
    nb = rows // NA_QROWS

    def body(q_ref, k_ref, v_ref, bias_ref, o_ref, p_ref, s_scr):
        start, kind = _na_block(pl.program_id(1), nb, rows)

        def block(static_kind):
            q2 = q_ref[...] * NA_SCALE
            kw = k_ref[pl.ds(start, NA_K), :]
            vw = v_ref[pl.ds(start, NA_K), :]
            heads = _na_heads()
            for hh in range(2):
                s_scr[hh] = _nt(jnp.where(heads[hh], q2, jnp.zeros_like(q2)), kw)
            for hh in range(2):
                for i in range(NA_Q // NA_CHUNK):
                    r = slice(i * NA_CHUNK, (i + 1) * NA_CHUNK)
                    for k2, p in enumerate(_na_softmax_pieces(s_scr, bias_ref, hh, static_kind, i)):
                        p = jnp.zeros((NA_CHUNK, LANES), F32) if p is None else p
                        p_ref[hh, r, k2 * LANES:(k2 + 1) * LANES] = p.astype(p_ref.dtype)
            o_ref[...] = jnp.where(heads[0], _mm(p_ref[0], vw), _mm(p_ref[1], vw))

        for static_kind in range(3):
            pl.when(kind == static_kind)(functools.partial(block, static_kind))

    return pl.pallas_call(
        body, name="na_fwd", grid=(NA_HEADS // 2, nb),
        in_specs=[pl.BlockSpec((NA_Q, LANES), lambda hp, b: (b, hp)),
                  pl.BlockSpec((seq, LANES), lambda hp, b: (0, 4 + hp)),
                  pl.BlockSpec((seq, LANES), lambda hp, b: (0, 8 + hp)),
                  pl.BlockSpec((2, NA_TAB, GRID_W, LANES), lambda hp, b: (hp, 0, 0, 0))],
        out_specs=[pl.BlockSpec((NA_Q, LANES), lambda hp, b: (b, hp)),
                   pl.BlockSpec((2, NA_Q, NA_K), lambda hp, b: (hp, b, 0))],
        out_shape=[jax.ShapeDtypeStruct((seq, D_NA), F32), jax.ShapeDtypeStruct((NA_HEADS, seq, NA_K), qkv.dtype)],
        scratch_shapes=[pltpu.VMEM((2, NA_Q, NA_K), F32)],
        compiler_params=_params("arbitrary", "arbitrary"),
    )(qkv, qkv, qkv, bias)


def _na_bwd(qkv, do, probs, parts=()):
    seq = qkv.shape[0]
    rows = seq // GRID_W
    nb = rows // NA_QROWS
    ns = len(parts)

    def body(q_ref, k_ref, v_ref, do_ref, p_ref, *rest):
        part_refs, rest = rest[:ns], rest[ns:]
        (dq_ref, dk_ref, dv_ref, dbias_ref), rest = rest[:4], rest[4:]
        land_refs, rest = rest[:ns], rest[ns:]
        dp_scr, ds_scr = rest[:2]
        b = pl.program_id(1)
        start, kind = _na_block(b, nb, rows)
        if ns:
            hp = pl.program_id(0)
            _ride(_pair_copies(part_refs, land_refs, *rest[2:]), (hp == 0) & (b == 0),
                  (hp == NA_HEADS // 2 - 1) & (b == nb - 1))

        @pl.when(b == 0)
        def _():
            dk_ref[...] = jnp.zeros_like(dk_ref)
            dv_ref[...] = jnp.zeros_like(dv_ref)
            dbias_ref[...] = jnp.zeros_like(dbias_ref)

        def block(static_kind):
            q2 = q_ref[...] * NA_SCALE
            kw = k_ref[pl.ds(start, NA_K), :]
            vw = v_ref[pl.ds(start, NA_K), :]
            do2 = do_ref[...].astype(q2.dtype)
            heads = _na_heads()
            col = lambda k2: slice(k2 * LANES, (k2 + 1) * LANES)
            zero = jnp.zeros((NA_CHUNK, LANES), ds_scr.dtype)
            for hh in range(2):
                dp_scr[hh] = _nt(jnp.where(heads[hh], do2, jnp.zeros_like(do2)), vw)
            for hh in range(2):
                for i in range(NA_Q // NA_CHUNK):
                    r = slice(i * NA_CHUNK, (i + 1) * NA_CHUNK)
                    pieces = _na_pieces(static_kind, i)
                    ps = [None if mode is None else p_ref[hh, r, col(k2)].astype(F32)
                          for k2, (_, _, mode) in enumerate(pieces)]
                    dps = [None if p is None else dp_scr[hh, r, col(k2)] for k2, p in enumerate(ps)]
                    pdp = functools.reduce(jnp.add, [p * dp for p, dp in zip(ps, dps) if p is not None])
                    rowsum = jnp.sum(pdp, axis=-1, keepdims=True)
                    for k2, (t, q, _) in enumerate(pieces):
                        if ps[k2] is None:
                            ds_scr[hh, r, col(k2)] = zero
                            continue
                        ds = ps[k2] * (dps[k2] - rowsum)
                        dbias_ref[hh, t, q, :] += ds
                        ds_scr[hh, r, col(k2)] = ds.astype(ds_scr.dtype)
            dq_ref[...] = jnp.where(heads[0], _mm(ds_scr[0], kw), _mm(ds_scr[1], kw)) * NA_SCALE
            dk_ref[pl.ds(start, NA_K), :] += jnp.where(heads[0], _tn(ds_scr[0], q2), _tn(ds_scr[1], q2))
            dv_ref[pl.ds(start, NA_K), :] += jnp.where(heads[0], _tn(p_ref[0], do2), _tn(p_ref[1], do2))

        for static_kind in range(3):
            pl.when(kind == static_kind)(functools.partial(block, static_kind))

    out = pl.pallas_call(
        body, name="na_bwd", grid=(NA_HEADS // 2, nb),
        in_specs=[pl.BlockSpec((NA_Q, LANES), lambda hp, b: (b, hp)),
                  pl.BlockSpec((seq, LANES), lambda hp, b: (0, 4 + hp)),
                  pl.BlockSpec((seq, LANES), lambda hp, b: (0, 8 + hp)),
                  pl.BlockSpec((NA_Q, LANES), lambda hp, b: (b, hp)),
                  pl.BlockSpec((2, NA_Q, NA_K), lambda hp, b: (hp, b, 0))] + [_ANY()] * ns,
        out_specs=[pl.BlockSpec((NA_Q, LANES), lambda hp, b: (b, hp)),
                   pl.BlockSpec((seq, LANES), lambda hp, b: (0, hp)),
                   pl.BlockSpec((seq, LANES), lambda hp, b: (0, hp)),
                   pl.BlockSpec((2, NA_TAB, GRID_W, LANES), lambda hp, b: (hp, 0, 0, 0))] + [_ANY()] * ns,
        out_shape=[jax.ShapeDtypeStruct((seq, D_NA), F32), jax.ShapeDtypeStruct((seq, D_NA), F32),
                   jax.ShapeDtypeStruct((seq, D_NA), F32),
                   jax.ShapeDtypeStruct((NA_HEADS, NA_TAB, GRID_W, LANES), F32)]
        + [jax.ShapeDtypeStruct((4,) + pt.shape[1:], pt.dtype) for pt in parts],
        scratch_shapes=[pltpu.VMEM((2, NA_Q, NA_K), F32), pltpu.VMEM((2, NA_Q, NA_K), qkv.dtype)]
        + ([_dma_sems(ns, 4), _dma_sems(ns, 4)] if ns else []),
        compiler_params=_params("arbitrary", "arbitrary"),
    )(qkv, qkv, qkv, do, probs, *parts)
    return out[:4], list(out[4:])


def _local_step(x2, p2, tgt, g_pre, g_post, w_in_g, ssm, w_glu, b_glu, rpb, w_out, w_ple, g_ple, w_pg,
                distributed=False):
    seq = x2.shape[0]
    nc = seq // SSM_CHUNK

    (pw, bb, cc, dd), ssm_vjp = jax.vjp(_ssm_tables, *ssm)
    bias, bias_vjp = jax.vjp(_na_table, rpb)

    riders = (w_glu, w_out, w_ple, w_pg) if distributed else ()
    (u, zs, qkv, zn, hn), gathered = _fwd_in(x2, g_pre, w_in_g, riders)
    if distributed:
        w_glu, w_out, w_ple, w_pg = gathered
        w_glu = w_glu.reshape(D_SSM, D_SSM)
        w_out = w_out.reshape(D_MODEL, D_MODEL)
        w_pg = w_pg.reshape(D_MODEL, D_MODEL)
    u_p = _to_pairs(u, nc)
    y_p, hin = _ssm_fwd(u_p, pw, bb, cc, dd)
    yssm = _from_pairs(y_p, nc)
    o, probs = _na_fwd(qkv, bias)
    dyssm, dzs, do, dzn, dh1, sq, d_wglu, d_bglu, d_wout, d_wple, d_wpg, d_gpost, d_gple = _mid(
        yssm, zs, o, zn, x2, p2, tgt, w_glu, b_glu, w_out, g_post, w_ple, g_ple, w_pg)
    mid_grads = [d_wglu.reshape(N_DEV, D_SSM // N_DEV, D_SSM), d_wout.reshape(N_DEV, D_MODEL // N_DEV, D_MODEL),
                 d_wple, d_wpg.reshape(N_DEV, D_MODEL // N_DEV, D_MODEL)]
    (dq, dk, dv, dbias), lands = _na_bwd(qkv, do, probs, mid_grads if distributed else ())
    (d_rpb,) = bias_vjp(dbias)
    chip_parts = _pair_sum(mid_grads, lands)[0] if distributed else ()
    (du_p, dpw, dbb, dcc, ddd), chip_lands = _ssm_bwd(u_p, _to_pairs(dyssm, nc), hin, pw, bb, cc, dd, chip_parts)
    d_ssm = ssm_vjp((dpw, dbb, dcc, ddd))
    du = _from_pairs(du_p, nc)
    dx, d_win, d_gpre = _bwd_in(du, dzs, dq, dk, dv, dzn, hn, x2, dh1, g_pre, w_in_g)
    if distributed:
        d_wglu, d_wout, d_wple, d_wpg = chip_lands
    return sq, dx, dict(norm_pre=d_gpre, norm_post=d_gpost, w_in=d_win, ssm=d_ssm, w_glu=d_wglu, b_glu=d_bglu,
                        na_rpb=d_rpb, w_out=d_wout, w_ple=d_wple, ple_norm=d_gple, w_ple_gate=d_wpg)


def _all_gather(shard):
    m_per, n = shard.shape

    def body(x_ref, out_ref, send_sems, recv_sems, local_sem):
        x, y, c = _place()
        me, sibling = (x, y, c), (x, y, 1 - c)
        chips = [(1 - x, y), (x, 1 - y), (1 - x, 1 - y)]

        def rows(px, py, pc):
            return out_ref.at[pl.ds((4 * px + 2 * py + pc) * m_per, m_per), :]

        def copy(k, block, to, src=None):
            return pltpu.make_async_remote_copy(
                src_ref=rows(*block) if src is None else src, dst_ref=rows(*block),
                send_sem=send_sems.at[k], recv_sem=recv_sems.at[k], device_id=to, device_id_type=MESH)

        mine = pltpu.make_async_copy(x_ref, rows(*me), local_sem)
        mine.start()
        first = [copy(0, me, sibling, src=x_ref)]
        first += [copy(1 + j, me, (*chip, c), src=x_ref) for j, chip in enumerate(chips)]
        for cp in first:
            cp.start()
        passed = [copy(4 + j, (*chip, c), sibling) for j, chip in enumerate(chips)]
        for j, chip in enumerate(chips):
            copy(1 + j, (*chip, c), me).wait_recv()
            passed[j].start()
        copy(0, sibling, me).wait_recv()
        for j, chip in enumerate(chips):
            copy(4 + j, (*chip, 1 - c), me).wait_recv()
        for cp in first + passed:
            cp.wait_send()
        mine.wait()

    return pl.pallas_call(
        body, name="all_gather",
        out_shape=jax.ShapeDtypeStruct((N_DEV * m_per, n), shard.dtype),
        in_specs=[_whole_vmem()], out_specs=_whole_vmem(),
        scratch_shapes=[pltpu.SemaphoreType.DMA((7,)), pltpu.SemaphoreType.DMA((7,)), pltpu.SemaphoreType.DMA],
        compiler_params=pltpu.CompilerParams(vmem_limit_bytes=VMEM_LIMIT),
    )(shard)


def _exchange_first(pair_parts, all_parts):
    n1, n2 = len(pair_parts), len(all_parts)

    def body(*refs):
        ins, refs = refs[:n1 + n2], refs[n1 + n2:]
        outs, sems = refs[:n1 + n2], refs[n1 + n2:]
        copies = (_pair_copies(ins[:n1], outs[:n1], *sems[:2])
                  + _alltoall_copies(ins[n1:], outs[n1:], *sems[2:]))
        _start_all(copies)
        _wait_all(copies)

    out = pl.pallas_call(
        body, name="exchange_first",
        out_shape=[jax.ShapeDtypeStruct((4,) + pt.shape[1:], pt.dtype) for pt in pair_parts]
        + [jax.ShapeDtypeStruct(pt.shape, pt.dtype) for pt in all_parts],
        in_specs=[_ANY()] * (n1 + n2), out_specs=[_ANY()] * (n1 + n2),
        scratch_shapes=[_dma_sems(n1, 4), _dma_sems(n1, 4),
                        _dma_sems(n2, N_DEV - 1), _dma_sems(n2, N_DEV - 1), _dma_sems(n2)],
    )(*pair_parts, *all_parts)
    return list(out[:n1]), list(out[n1:])


def _pair_sum(parts, lands, all_lands=()):
    ns, na = len(parts), len(all_lands)

    def body(*refs):
        c = lax.axis_index("c")
        ins, outs = refs[:2 * ns + na], refs[2 * ns + na:]
        for part_ref, land_ref, out_ref in zip(ins[:ns], ins[ns:2 * ns], outs[:ns]):
            for q in range(4):
                out_ref[q] = (part_ref[q, c] + land_ref[q]).astype(BF16)
        for land_ref, out_ref in zip(ins[2 * ns:], outs[ns:]):
            acc = land_ref[0]
            for j in range(1, N_DEV):
                acc = acc + land_ref[j]
            out_ref[...] = acc

    out = pl.pallas_call(
        body, name="pair_sum",
        in_specs=[_whole_vmem()] * (2 * ns + na), out_specs=[_whole_vmem()] * (ns + na),
        out_shape=[jax.ShapeDtypeStruct(ld.shape, BF16) for ld in lands]
        + [jax.ShapeDtypeStruct(ld.shape[1:], ld.dtype) for ld in all_lands],
        compiler_params=pltpu.CompilerParams(vmem_limit_bytes=VMEM_LIMIT),
    )(*[pt.reshape((4, 2) + pt.shape[1:]) for pt in parts], *lands, *all_lands)
    return list(out[:ns]), list(out[ns:])


def _exchange_second(chip_parts, shards):
    n1, n2 = len(chip_parts), len(shards)

    def body(*refs):
        ins, refs = refs[:n1 + n2], refs[n1 + n2:]
        outs, sems = refs[:n1 + n2], refs[n1 + n2:]
        copies = (_chip_copies(ins[:n1], outs[:n1], *sems[:3])
                  + _gather_copies(ins[n1:], outs[n1:], *sems[3:]))
        _start_all(copies)
        _wait_all(copies)

    out = pl.pallas_call(
        body, name="exchange_second",
        out_shape=[jax.ShapeDtypeStruct(pt.shape, pt.dtype) for pt in chip_parts]
        + [jax.ShapeDtypeStruct((N_DEV,) + sh.shape, sh.dtype) for sh in shards],
        in_specs=[_ANY()] * (n1 + n2), out_specs=[_ANY()] * (n1 + n2),
        scratch_shapes=[_dma_sems(n1, 3), _dma_sems(n1, 3), _dma_sems(n1),
                        _dma_sems(n2, N_DEV - 1), _dma_sems(n2, N_DEV - 1), _dma_sems(n2)],
    )(*chip_parts, *shards)
    return list(out[:n1]), list(out[n1:])


def _adamw(w, g, m, v):
    r, c = w.shape
    tile = 256 if r % 256 == 0 else r
    slots = g.shape[0] if g.ndim == 3 else 0

    def body(w_ref, g_ref, m_ref, v_ref, go_ref, d_ref, nm_ref, nv_ref):
        if slots:
            g = g_ref[0].astype(F32)
            for j in range(1, slots):
                g = g + g_ref[j].astype(F32)
        else:
            g = g_ref[...]
        go_ref[...] = g
        nm = ADAM_B1 * m_ref[...] + (1.0 - ADAM_B1) * g
        nv = ADAM_B2 * v_ref[...] + (1.0 - ADAM_B2) * (g * g)
        m_hat = nm / (1.0 - ADAM_B1 ** ADAM_STEP)
        v_hat = nv / (1.0 - ADAM_B2 ** ADAM_STEP)
        d_ref[...] = -ADAM_LR * (m_hat / (jnp.sqrt(v_hat) + ADAM_EPS) + ADAM_WD * w_ref[...])
        nm_ref[...] = nm
        nv_ref[...] = nv

    spec = pl.BlockSpec((tile, c), lambda i: (i, 0))
    g_spec = pl.BlockSpec((slots, tile, c), lambda i: (0, i, 0)) if slots else spec
    out = jax.ShapeDtypeStruct((r, c), F32)
    return pl.pallas_call(
        body, name="adamw", grid=(r // tile,), in_specs=[spec, g_spec, spec, spec], out_specs=[spec] * 4,
        out_shape=[out] * 4, compiler_params=_params("arbitrary"),
    )(w, g, m, v)


_SLAB = 8 * LANES


def _flat_rows(a):
    flat = a.reshape(-1)
    pad = (-flat.shape[0]) % _SLAB
    if pad:
        flat = jnp.concatenate([flat, jnp.zeros((pad,), flat.dtype)])
    return flat.reshape(-1, LANES)


def _pack(arrays):
    slabs = [_flat_rows(a) for a in arrays]
    return jnp.concatenate(slabs, axis=0), [s.shape[0] for s in slabs]


def _unpack(packed, like, rows):
    out, at = [], 0
    for a, r in zip(like, rows):
        out.append(packed[at:at + r].reshape(-1)[:a.size].reshape(a.shape))
        at += r
    return out


SMALL = ('norm_pre', 'norm_post', 'ssm_a_re', 'ssm_a_im', 'ssm_log_dt', 'ssm_b_re', 'ssm_b_im', 'ssm_c_re',
         'ssm_c_im', 'ssm_d', 'b_glu', 'na_rpb', 'ple_norm')
BIG = ('w_in', 'w_glu', 'w_out', 'w_ple', 'w_ple_gate')
ORDER = ('norm_pre', 'norm_post', 'w_in', 'ssm_a_re', 'ssm_a_im', 'ssm_log_dt', 'ssm_b_re', 'ssm_b_im', 'ssm_c_re',
         'ssm_c_im', 'ssm_d', 'w_glu', 'b_glu', 'na_rpb', 'w_out', 'w_ple', 'ple_norm', 'w_ple_gate')


def kernel(x, p, norm_pre, norm_post, w_in, ssm_a_re, ssm_a_im, ssm_log_dt, ssm_b_re, ssm_b_im, ssm_c_re, ssm_c_im, ssm_d, w_glu, b_glu, na_rpb, w_out, w_ple, ple_norm, w_ple_gate, loss_target, m_norm_pre, m_norm_post, m_w_in, m_ssm_a_re, m_ssm_a_im, m_ssm_log_dt, m_ssm_b_re, m_ssm_b_im, m_ssm_c_re, m_ssm_c_im, m_ssm_d, m_w_glu, m_b_glu, m_na_rpb, m_w_out, m_w_ple, m_ple_norm, m_w_ple_gate, v_norm_pre, v_norm_post, v_w_in, v_ssm_a_re, v_ssm_a_im, v_ssm_log_dt, v_ssm_b_re, v_ssm_b_im, v_ssm_c_re, v_ssm_c_im, v_ssm_d, v_w_glu, v_b_glu, v_na_rpb, v_w_out, v_w_ple, v_ple_norm, v_w_ple_gate):
    args = dict(locals())
    weights = {n: args[n] for n in ORDER}
    mom_m = {n: args["m_" + n] for n in ORDER}
    mom_v = {n: args["v_" + n] for n in ORDER}

    w_in_g = _all_gather(w_in[0].astype(BF16)).reshape(N_DEV, D_MODEL, SHARD_IN)
    blocks = [weights[n][0].astype(BF16) for n in ('w_glu', 'w_out', 'w_ple', 'w_ple_gate')]

    ssm = tuple(weights[n][0] for n in ('ssm_a_re', 'ssm_a_im', 'ssm_log_dt', 'ssm_b_re', 'ssm_b_im',
                                        'ssm_c_re', 'ssm_c_im', 'ssm_d'))
    sq, dx, grads = _local_step(x[0], p[0, 0], loss_target[0], norm_pre, norm_post, w_in_g, ssm, blocks[0], b_glu,
                                na_rpb[0], blocks[1], blocks[2], ple_norm, blocks[3], distributed=True)
    loss_local = (0.5 / D_MODEL * jnp.sum(sq)).reshape(1)

    local = dict(norm_pre=grads['norm_pre'], norm_post=grads['norm_post'], b_glu=grads['b_glu'],
                 na_rpb=grads['na_rpb'][None], ple_norm=grads['ple_norm'])
    for n, g in zip(('ssm_a_re', 'ssm_a_im', 'ssm_log_dt', 'ssm_b_re', 'ssm_b_im', 'ssm_c_re', 'ssm_c_im', 'ssm_d'),
                    grads['ssm']):
        local[n] = g[None]

    small_flat, small_rows = _pack([local[n] for n in SMALL] + [loss_local])
    pad = (-small_flat.shape[0]) % (8 * N_DEV)
    if pad:
        small_flat = jnp.concatenate([small_flat, jnp.zeros((pad, LANES), F32)], axis=0)
    small_per = small_flat.shape[0] // N_DEV
    d_win = [grads['w_in']]
    lands, small_lands = _exchange_first(d_win, [small_flat.reshape(N_DEV, small_per, LANES)])
    chip_parts, small_sums = _pair_sum(d_win, lands, small_lands)
    chip_lands, small_full = _exchange_second(chip_parts, small_sums)
    grads['w_in'] = chip_lands[0]
    small_all = small_full[0].reshape(N_DEV * small_per, LANES)

    outs = {}
    for n in BIG:
        g, d, nm, nv = _adamw(weights[n][0], grads[n], mom_m[n][0], mom_v[n][0])
        outs[n] = (g[None], d[None], nm[None], nv[None])
    like = [weights[n] for n in SMALL]
    w_s, rows_s = _pack(like)
    m_s, _ = _pack([mom_m[n] for n in SMALL])
    v_s, _ = _pack([mom_v[n] for n in SMALL])
    loss = small_all[w_s.shape[0], 0]
    unpacked = [_unpack(a, like, rows_s) for a in _adamw(w_s, small_all[:w_s.shape[0]], m_s, v_s)]
    for i, n in enumerate(SMALL):
        outs[n] = tuple(u[i] for u in unpacked)

    return (loss, dx[None], *[outs[n][0] for n in ORDER], *[outs[n][1] for n in ORDER],
            *[outs[n][2] for n in ORDER], *[outs[n][3] for n in ORDER])
```

```python
import functools

import jax
import jax.numpy as jnp
from jax import lax
from jax.experimental import pallas as pl
from jax.experimental.pallas import tpu as pltpu

F32 = jnp.float32
BF16 = jnp.bfloat16
HIGHEST = lax.Precision.HIGHEST

D_MODEL = 1024
D_PLE = 256
GRID_W = 64
D_SSM = 512
SSM_GROUP = 16
N_GROUPS = 32
N_PAIRS = 16
SSM_STATE = 64
D_NA = 512
NA_HEADS = 8
NA_HEAD_DIM = 64
NA_ROWS = 8
NA_COLS = 16
D_IN_PROJ = 3072
EPS = 1e-6
N_DEV = 8
SHARD_IN = D_IN_PROJ // N_DEV
LANES = 128
SSM_CHUNK = 16
TOK_TILE = 256
IN_TILE = 512
NA_QROWS = 4
NA_KROWS = 12
NEG = -1e30
VMEM_LIMIT = 56 * 1024 * 1024

ADAM_LR = 0.001
ADAM_B1 = 0.9
ADAM_B2 = 0.999
ADAM_EPS = 1e-08
ADAM_WD = 0.01
ADAM_STEP = 10

MESH = pl.DeviceIdType.MESH


def _params(*sem):
    return pltpu.CompilerParams(dimension_semantics=sem or None, vmem_limit_bytes=VMEM_LIMIT)


def _whole_vmem():
    return pl.BlockSpec(memory_space=pltpu.VMEM)


def _nt(a, b):
    return lax.dot_general(a, b, (((1,), (1,)), ((), ())), preferred_element_type=F32)


def _tn(a, b):
    return lax.dot_general(a, b, (((0,), (0,)), ((), ())), preferred_element_type=F32)


def _mm(a, b):
    return jnp.dot(a, b, preferred_element_type=F32)


def _sigmoid(x):
    return 1.0 / (1.0 + jnp.exp(-x))


_GELU_C = 0.7978845608028654


def _gelu(x):
    return 0.5 * x * (1.0 + jnp.tanh(_GELU_C * (x + 0.044715 * x * x * x)))


def _gelu_grad(x):
    th = jnp.tanh(_GELU_C * (x + 0.044715 * x * x * x))
    return 0.5 * (1.0 + th) + 0.5 * x * (1.0 - th * th) * _GELU_C * (1.0 + 3.0 * 0.044715 * x * x)


def _ssm_tables(a_re, a_im, log_dt, b_re, b_im, c_re, c_im, d):
    T, P, H = SSM_CHUNK, SSM_STATE, SSM_GROUP
    dt = jnp.exp(log_dt)[..., None]
    xr = dt * a_re
    xi = dt * a_im
    mag = jnp.exp(xr)
    lr = mag * jnp.cos(xi)
    li = mag * jnp.sin(xi)
    den = a_re * a_re + a_im * a_im
    cr = ((lr - 1.0) * a_re + li * a_im) / den
    ci = (li * a_re - (lr - 1.0) * a_im) / den
    bbr = cr[..., None] * b_re - ci[..., None] * b_im
    bbi = cr[..., None] * b_im + ci[..., None] * b_re
    kk = jnp.arange(T + 1, dtype=F32)[:, None, None, None]
    pm = jnp.exp(kk * xr)
    pw = jnp.stack([pm * jnp.cos(kk * xi), pm * jnp.sin(kk * xi)], axis=2)
    pw = pw.reshape(T + 1, 2, 2, N_PAIRS, 2 * P).transpose(3, 0, 1, 2, 4).reshape(N_PAIRS, T + 1, 8 * P)
    pw = jnp.concatenate([pw, jnp.zeros((N_PAIRS, 24 - (T + 1), 8 * P), F32)], axis=1)
    eye2 = jnp.eye(2, dtype=F32)

    def expand(t):
        t = t.transpose(2, 0, 1, 3, 4, 5)
        t = t[:, :, :, :, :, None, :] * eye2[None, None, None, :, None, :, None]
        return t.reshape(N_PAIRS, 4, 2 * H, 2 * P)

    bb = expand(jnp.stack([bbr, bbi], axis=1).reshape(2, 2, N_PAIRS, 2, P, H).transpose(0, 1, 2, 3, 5, 4))
    cc = expand(jnp.stack([c_re, c_im], axis=1).reshape(2, 2, N_PAIRS, 2, H, P))
    dd = d.reshape(N_PAIRS, 2 * H)[:, :, None] * jnp.eye(2 * H, dtype=F32)[None]
    dd = jnp.concatenate([dd, jnp.zeros((N_PAIRS, 2 * H, LANES - 2 * H), F32)], axis=2)
    return pw, bb, cc, dd


NA_TAB = 2 * NA_ROWS


def _na_table(rpb):
    qc = jnp.arange(GRID_W)[:, None, None]
    kc = (jnp.arange(2 * GRID_W) % GRID_W)[None, :, None]
    dc = jnp.arange(2 * NA_COLS - 1)[None, None, :]
    cstart = jnp.clip(qc - NA_COLS // 2, 0, GRID_W - NA_COLS)
    csel = ((kc >= cstart) & (kc < cstart + NA_COLS) & (kc - qc + NA_COLS - 1 == dc)).astype(F32)
    col_ok = jnp.sum(csel, axis=-1) > 0.5
    part = jnp.einsum('hrc,qmc->hrqm', rpb, csel, precision=HIGHEST)
    zero = jnp.zeros_like(part[:, :1])
    odd = jnp.arange(2 * GRID_W) >= GRID_W
    tab = jnp.where(odd, jnp.concatenate([part, zero], axis=1), jnp.concatenate([zero, part], axis=1))
    return jnp.where(col_ok, tab, NEG)


def _place():
    return lax.axis_index("x"), lax.axis_index("y"), lax.axis_index("c")


def _remote(src, dst, send_sem, recv_sem, device):
    return pltpu.make_async_remote_copy(src_ref=src, dst_ref=dst, send_sem=send_sem, recv_sem=recv_sem,
                                        device_id=device, device_id_type=MESH)


def _start_all(copies):
    for cp in copies:
        cp.start()


def _wait_all(copies):
    for cp in copies:
        cp.wait()


def _gather_copies(shard_refs, full_refs, send_sems, recv_sems, local_sems):
    x, y, c = _place()
    me = 4 * x + 2 * y + c
    out = []
    for t, (shard, full) in enumerate(zip(shard_refs, full_refs)):
        out.append(pltpu.make_async_copy(shard, full.at[me], local_sems.at[t]))
        for k in range(1, N_DEV):
            peer = (x ^ ((k >> 2) & 1), y ^ ((k >> 1) & 1), c ^ (k & 1))
            out.append(_remote(shard, full.at[me], send_sems.at[t, k - 1], recv_sems.at[t, k - 1], peer))
    return out


def _pair_copies(part_refs, land_refs, send_sems, recv_sems):
    x, y, c = _place()
    out = []
    for t, (part, land) in enumerate(zip(part_refs, land_refs)):
        for q in range(4):
            out.append(_remote(part.at[2 * q + (1 - c)], land.at[q], send_sems.at[t, q], recv_sems.at[t, q],
                               (x, y, 1 - c)))
    return out


def _chip_copies(part_refs, land_refs, send_sems, recv_sems, local_sems):
    x, y, c = _place()
    mine = 2 * x + y
    out = []
    for t, (part, land) in enumerate(zip(part_refs, land_refs)):
        out.append(pltpu.make_async_copy(part.at[mine], land.at[mine], local_sems.at[t]))
        for k in range(1, 4):
            px, py = x ^ (k >> 1), y ^ (k & 1)
            out.append(_remote(part.at[2 * px + py], land.at[mine], send_sems.at[t, k - 1], recv_sems.at[t, k - 1],
                               (px, py, c)))
    return out


def _alltoall_copies(part_refs, land_refs, send_sems, recv_sems, local_sems):
    x, y, c = _place()
    me = 4 * x + 2 * y + c
    out = []
    for t, (part, land) in enumerate(zip(part_refs, land_refs)):
        out.append(pltpu.make_async_copy(part.at[me], land.at[me], local_sems.at[t]))
        for k in range(1, N_DEV):
            px, py, pc = x ^ ((k >> 2) & 1), y ^ ((k >> 1) & 1), c ^ (k & 1)
            out.append(_remote(part.at[4 * px + 2 * py + pc], land.at[me], send_sems.at[t, k - 1],
                               recv_sems.at[t, k - 1], (px, py, pc)))
    return out


def _ride(copies, first, last):
    pl.when(first)(functools.partial(_start_all, copies))
    pl.when(last)(functools.partial(_wait_all, copies))


_ANY = lambda: pl.BlockSpec(memory_space=pl.ANY)


def _dma_sems(*shape):
    return pltpu.SemaphoreType.DMA(shape)


def _fwd_in(x2, g_pre, w_in_g, shards=()):
    seq = x2.shape[0]
    ns = len(shards)
    steps = seq // TOK_TILE

    def body(x_ref, g_ref, w_ref, *rest):
        shard_refs, rest = rest[:ns], rest[ns:]
        (u_ref, zs_ref, qkv_ref, zn_ref, hn_ref), rest = rest[:5], rest[5:]
        if ns:
            i = pl.program_id(0)
            _ride(_gather_copies(shard_refs, rest[:ns], *rest[ns:]), i == 0, i == steps - 1)
        x = x_ref[...]
        r = lax.rsqrt(jnp.mean(x * x, axis=-1, keepdims=True) + EPS)
        hn = (x * r * g_ref[...]).astype(BF16)
        hn_ref[...] = hn
        for j in range(N_DEV):
            pj = _mm(hn, w_ref[j])
            for i in range(SHARD_IN // LANES):
                blk = (SHARD_IN // LANES) * j + i
                piece = pj[:, i * LANES:(i + 1) * LANES]
                if blk < 4:
                    u_ref[:, blk * LANES:(blk + 1) * LANES] = piece
                elif blk < 8:
                    zs_ref[:, (blk - 4) * LANES:(blk - 3) * LANES] = piece
                elif blk < 20:
                    qkv_ref[:, (blk - 8) * LANES:(blk - 7) * LANES] = piece.astype(BF16)
                else:
                    zn_ref[:, (blk - 20) * LANES:(blk - 19) * LANES] = piece

    tok = lambda w: pl.BlockSpec((TOK_TILE, w), lambda i: (i, 0))
    out = pl.pallas_call(
        body, name="fwd_in", grid=(steps,),
        in_specs=[tok(D_MODEL), pl.BlockSpec((1, D_MODEL), lambda i: (0, 0)), _whole_vmem()] + [_ANY()] * ns,
        out_specs=[tok(D_SSM), tok(D_SSM), tok(3 * D_NA), tok(D_NA), tok(D_MODEL)] + [_ANY()] * ns,
        out_shape=[jax.ShapeDtypeStruct((seq, D_SSM), F32), jax.ShapeDtypeStruct((seq, D_SSM), F32),
                   jax.ShapeDtypeStruct((seq, 3 * D_NA), BF16), jax.ShapeDtypeStruct((seq, D_NA), F32),
                   jax.ShapeDtypeStruct((seq, D_MODEL), BF16)]
        + [jax.ShapeDtypeStruct((N_DEV,) + sh.shape, sh.dtype) for sh in shards],
        scratch_shapes=[_dma_sems(ns, N_DEV - 1), _dma_sems(ns, N_DEV - 1), _dma_sems(ns)] if ns else [],
        compiler_params=_params("arbitrary"),
    )(x2, g_pre, w_in_g, *shards)
    return out[:5], list(out[5:])


def _mid(yssm, zs, o, zn, x2, p2, tgt, w_glu, b_glu, w_out, g_post, w_ple_g, g_ple, w_pg):
    seq = x2.shape[0]

    def body(yssm_ref, zs_ref, o_ref, zn_ref, x_ref, p_ref, tgt_ref, wglu_ref, bglu_ref, wout_ref, gpost_ref,
             wple_ref, gple_ref, wpg_ref,
             dyssm_ref, dzs_ref, do_ref, dzn_ref, dh1_ref, loss_ref, dwglu_ref, dbglu_ref, dwout_ref, dwple_ref,
             dwpg_ref, dgpost_ref, dgple_ref):
        @pl.when(pl.program_id(0) == 0)
        def _():
            for ref in (loss_ref, dwglu_ref, dbglu_ref, dwout_ref, dwple_ref, dwpg_ref, dgpost_ref, dgple_ref):
                ref[...] = jnp.zeros_like(ref)

        yv = yssm_ref[...]
        g1 = _gelu(yv)
        g1b = g1.astype(BF16)
        sg = _sigmoid(_mm(g1b, wglu_ref[...]) + bglu_ref[...])
        zs_v = zs_ref[...]
        s_zs = _sigmoid(zs_v)
        g2 = g1 * sg
        zn_v = zn_ref[...]
        s_zn = _sigmoid(zn_v)
        o_v = o_ref[...]
        cat = jnp.concatenate([g2 * (zs_v * s_zs), o_v * (zn_v * s_zn)], axis=1).astype(BF16)
        mix = _mm(cat, wout_ref[...])
        r2 = lax.rsqrt(jnp.mean(mix * mix, axis=-1, keepdims=True) + EPS)
        n2 = mix * r2
        gpost = gpost_ref[...]
        h1 = x_ref[...] + n2 * gpost
        pb = p_ref[...].astype(BF16)
        epre = jnp.concatenate([_mm(pb, wple_ref[j]) for j in range(N_DEV)], axis=1)
        r3 = lax.rsqrt(jnp.mean(epre * epre, axis=-1, keepdims=True) + EPS)
        n3 = epre * r3
        gple = gple_ref[...]
        e = n3 * gple
        h1b = h1.astype(BF16)
        gate = _sigmoid(_mm(h1b, wpg_ref[...]))
        diff = h1 + gate * e - tgt_ref[...]
        loss_ref[...] += jnp.sum(diff * diff, axis=0, keepdims=True)

        dy = diff * (1.0 / D_MODEL)
        dgp = (dy * e * gate * (1.0 - gate)).astype(BF16)
        de = dy * gate
        dh1 = dy + _nt(dgp, wpg_ref[...])
        dh1_ref[...] = dh1
        dwpg_ref[...] += _tn(h1b, dgp)
        dgple_ref[...] += jnp.sum(de * n3, axis=0, keepdims=True)
        dn3 = de * gple
        depre = (r3 * (dn3 - n3 * jnp.mean(dn3 * n3, axis=-1, keepdims=True))).astype(BF16)
        for j in range(N_DEV):
            dwple_ref[j] += _tn(pb, depre[:, j * LANES:(j + 1) * LANES])
        dgpost_ref[...] += jnp.sum(dh1 * n2, axis=0, keepdims=True)
        dn2 = dh1 * gpost
        dmix = (r2 * (dn2 - n2 * jnp.mean(dn2 * n2, axis=-1, keepdims=True))).astype(BF16)
        dcat = _nt(dmix, wout_ref[...])
        dwout_ref[...] += _tn(cat, dmix)

        dys = dcat[:, :D_SSM]
        dyn = dcat[:, D_SSM:]
        dg2 = dys * (zs_v * s_zs)
        dzs_ref[...] = dys * g2 * (s_zs * (1.0 + zs_v * (1.0 - s_zs)))
        dt = dg2 * g2 * (1.0 - sg)
        dtb = dt.astype(BF16)
        dg1 = dg2 * sg + _nt(dtb, wglu_ref[...])
        dwglu_ref[...] += _tn(g1b, dtb)
        dbglu_ref[...] += jnp.sum(dt, axis=0, keepdims=True)
        dyssm_ref[...] = dg1 * _gelu_grad(yv)
        do_ref[...] = dyn * (zn_v * s_zn)
        dzn_ref[...] = dyn * o_v * (s_zn * (1.0 + zn_v * (1.0 - s_zn)))

    tok = lambda w: pl.BlockSpec((TOK_TILE, w), lambda i: (i, 0))
    row = lambda w: pl.BlockSpec((1, w), lambda i: (0, 0))
    vm = _whole_vmem()
    half = jax.ShapeDtypeStruct((seq, D_SSM), F32)
    gain = jax.ShapeDtypeStruct((1, D_MODEL), F32)
    return pl.pallas_call(
        body, name="mid", grid=(seq // TOK_TILE,),
        in_specs=[tok(D_SSM), tok(D_SSM), tok(D_NA), tok(D_NA), tok(D_MODEL), tok(D_PLE), tok(D_MODEL),
                  vm, row(D_SSM), vm, row(D_MODEL), vm, row(D_MODEL), vm],
        out_specs=[tok(D_SSM), tok(D_SSM), tok(D_NA), tok(D_NA), tok(D_MODEL), vm, vm, vm, vm, vm, vm, vm, vm],
        out_shape=[half, half, half, half, jax.ShapeDtypeStruct((seq, D_MODEL), F32), gain,
                   jax.ShapeDtypeStruct((D_SSM, D_SSM), F32), jax.ShapeDtypeStruct((1, D_SSM), F32),
                   jax.ShapeDtypeStruct((D_MODEL, D_MODEL), F32),
                   jax.ShapeDtypeStruct((N_DEV, D_PLE, LANES), F32),
                   jax.ShapeDtypeStruct((D_MODEL, D_MODEL), F32), gain, gain],
        compiler_params=_params("arbitrary"),
    )(yssm, zs, o, zn, x2, p2, tgt, w_glu, b_glu, w_out, g_post, w_ple_g, g_ple, w_pg)


def _bwd_in(du, dzs, dq, dk, dv, dzn, hn, x2, dh1, g_pre, w_in_g):
    seq = x2.shape[0]

    def body(du_ref, dzs_ref, dq_ref, dk_ref, dv_ref, dzn_ref, hn_ref, x_ref, dh1_ref, g_ref, w_ref,
             dx_ref, dw_ref, dg_ref, dproj_ref):
        @pl.when(pl.program_id(0) == 0)
        def _():
            dw_ref[...] = jnp.zeros_like(dw_ref)
            dg_ref[...] = jnp.zeros_like(dg_ref)

        for k, ref in enumerate((du_ref, dzs_ref, dq_ref, dk_ref, dv_ref, dzn_ref)):
            dproj_ref[:, k * D_SSM:(k + 1) * D_SSM] = ref[...].astype(BF16)
        hn = hn_ref[...]
        dhn = jnp.zeros((IN_TILE, D_MODEL), F32)
        for j in range(N_DEV):
            dpj = dproj_ref[:, j * SHARD_IN:(j + 1) * SHARD_IN]
            dhn += _nt(dpj, w_ref[j])
            dw_ref[j] += _tn(hn, dpj)
        x = x_ref[...]
        r = lax.rsqrt(jnp.mean(x * x, axis=-1, keepdims=True) + EPS)
        n1 = x * r
        dg_ref[...] += jnp.sum(dhn * n1, axis=0, keepdims=True)
        dn1 = dhn * g_ref[...]
        dx_ref[...] = dh1_ref[...] + r * (dn1 - n1 * jnp.mean(dn1 * n1, axis=-1, keepdims=True))

    tok = lambda w: pl.BlockSpec((IN_TILE, w), lambda i: (i, 0))
    vm = _whole_vmem()
    return pl.pallas_call(
        body, name="bwd_in", grid=(seq // IN_TILE,),
        in_specs=[tok(D_SSM)] * 6 + [tok(D_MODEL), tok(D_MODEL), tok(D_MODEL),
                                      pl.BlockSpec((1, D_MODEL), lambda i: (0, 0)), vm],
        out_specs=[tok(D_MODEL), vm, vm],
        out_shape=[jax.ShapeDtypeStruct((seq, D_MODEL), F32),
                   jax.ShapeDtypeStruct((N_DEV, D_MODEL, SHARD_IN), F32),
                   jax.ShapeDtypeStruct((1, D_MODEL), F32)],
        scratch_shapes=[pltpu.VMEM((IN_TILE, D_IN_PROJ), BF16)],
        compiler_params=_params("arbitrary"),
    )(du, dzs, dq, dk, dv, dzn, hn, x2, dh1, g_pre, w_in_g)


RELAYOUT_CHUNKS = 128
PAIR_W = 2 * SSM_GROUP
PAIRS_PER_BLOCK = LANES // PAIR_W
CHUNK_W = SSM_CHUNK * PAIR_W
PW_ROWS = 24


def _lane_window(lo, width):
    lane = lax.broadcasted_iota(jnp.int32, (1, LANES), 1)
    return (lane >= lo) & (lane < lo + width)


def _to_pairs(a, nc):
    ncb = min(RELAYOUT_CHUNKS, nc)

    def body(x_ref, out_ref):
        xs = [x_ref[pl.ds(s, ncb, stride=SSM_CHUNK), :] for s in range(SSM_CHUNK)]
        for a_ in range(PAIRS_PER_BLOCK):
            for v in range(SSM_CHUNK // PAIRS_PER_BLOCK):
                acc = None
                for i in range(PAIRS_PER_BLOCK):
                    shift = (PAIR_W * (i - a_)) % LANES
                    piece = xs[PAIRS_PER_BLOCK * v + i]
                    piece = pltpu.roll(piece, shift, axis=1) if shift else piece
                    acc = piece if acc is None else jnp.where(_lane_window(PAIR_W * i, PAIR_W), piece, acc)
                out_ref[a_, :, LANES * v:LANES * (v + 1)] = acc.astype(BF16)

    return pl.pallas_call(
        body, name="to_pairs", grid=(D_SSM // LANES, nc // ncb),
        in_specs=[pl.BlockSpec((ncb * SSM_CHUNK, LANES), lambda cl, cb: (cb, cl))],
        out_specs=pl.BlockSpec((PAIRS_PER_BLOCK, ncb, CHUNK_W), lambda cl, cb: (cl, cb, 0)),
        out_shape=jax.ShapeDtypeStruct((N_PAIRS, nc, CHUNK_W), BF16),
        compiler_params=_params("arbitrary", "arbitrary"),
    )(a)


def _from_pairs(a, nc):
    ncb = min(RELAYOUT_CHUNKS, nc)

    def body(y_ref, out_ref):
        for s in range(SSM_CHUNK):
            v, i = divmod(s, PAIRS_PER_BLOCK)
            acc = None
            for a_ in range(PAIRS_PER_BLOCK):
                shift = (PAIR_W * (a_ - i)) % LANES
                piece = y_ref[a_, :, LANES * v:LANES * (v + 1)]
                piece = pltpu.roll(piece, shift, axis=1) if shift else piece
                acc = piece if acc is None else jnp.where(_lane_window(PAIR_W * a_, PAIR_W), piece, acc)
            out_ref[pl.ds(s, ncb, stride=SSM_CHUNK), :] = acc

    return pl.pallas_call(
        body, name="from_pairs", grid=(D_SSM // LANES, nc // ncb),
        in_specs=[pl.BlockSpec((PAIRS_PER_BLOCK, ncb, CHUNK_W), lambda cl, cb: (cl, cb, 0))],
        out_specs=pl.BlockSpec((ncb * SSM_CHUNK, LANES), lambda cl, cb: (cb, cl)),
        out_shape=jax.ShapeDtypeStruct((nc * SSM_CHUNK, D_SSM), F32),
        compiler_params=_params("arbitrary", "arbitrary"),
    )(a)


SSM_PAIRS = 4
SLOT = 8


def _boundary_scan(nc, pw_ref, buf_ref, conj):
    pad = [jnp.zeros((SLOT - SSM_PAIRS, LANES), F32)]
    lr0, li0, lr1, li1 = (jnp.concatenate([_pw_row(pw_ref, j, SSM_CHUNK, q) for j in range(SSM_PAIRS)] + pad, axis=0)
                          for q in range(4))
    if conj:
        li0, li1 = -li0, -li1

    def step(c, carry):
        hr0, hi0, hr1, hi1 = carry
        up = pl.ds(pl.multiple_of(c * SLOT, SLOT), SLOT)
        dn = pl.ds(pl.multiple_of((nc - 1 - c) * SLOT, SLOT), SLOT)
        ra, rb = (dn, up) if conj else (up, dn)
        s_r0, s_i0 = buf_ref[0, ra, :], buf_ref[1, ra, :]
        s_r1, s_i1 = buf_ref[2, rb, :], buf_ref[3, rb, :]
        buf_ref[0, ra, :] = hr0
        buf_ref[1, ra, :] = hi0
        buf_ref[2, rb, :] = hr1
        buf_ref[3, rb, :] = hi1
        return (lr0 * hr0 - li0 * hi0 + s_r0, lr0 * hi0 + li0 * hr0 + s_i0,
                lr1 * hr1 - li1 * hi1 + s_r1, lr1 * hi1 + li1 * hr1 + s_i1)

    z = jnp.zeros((SLOT, LANES), F32)
    lax.fori_loop(0, nc, step, (z, z, z, z), unroll=8)


def _put_pair(buf_ref, j, nc, val):
    for q in range(4):
        buf_ref[q, pl.ds(j, nc, stride=SLOT), :] = val[:, LANES * q:LANES * (q + 1)]


def _get_pair(buf_ref, j, nc):
    return jnp.concatenate([buf_ref[q, pl.ds(j, nc, stride=SLOT), :] for q in range(4)], axis=1)


def _pw_row(pw_ref, j, k, q):
    return pw_ref[j, k:k + 1, LANES * q:LANES * (q + 1)]


def _mm_f32(a, b, dims):
    return lax.dot_general(a, b, (dims, ((), ())), precision=HIGHEST, preferred_element_type=F32)


_POW_M = (lambda s: SSM_CHUNK - 1 - s, lambda s: s)
_POW_C = (lambda s: s + 1, lambda s: SSM_CHUNK - s)
_POW_K = (lambda s: s, lambda s: SSM_CHUNK - 1 - s)


def _chunk_matrices(j, pw_ref, bb_ref, cc_ref, dd_ref, m_scr, ct_scr, toep_scr, g_scr, kt_scr):
    blk = lambda s: slice(PAIR_W * s, PAIR_W * (s + 1))
    col = lambda q: slice(LANES * q, LANES * (q + 1))
    for d in range(2):
        bbr, bbi = bb_ref[j, 2 * d], bb_ref[j, 2 * d + 1]
        ccr, cci = cc_ref[j, 2 * d], cc_ref[j, 2 * d + 1]
        for s in range(SSM_CHUNK):
            pr, pi = _pw_row(pw_ref, j, _POW_M[d](s), 2 * d), _pw_row(pw_ref, j, _POW_M[d](s), 2 * d + 1)
            m_scr[j, blk(s), col(2 * d)] = (pr * bbr - pi * bbi).astype(m_scr.dtype)
            m_scr[j, blk(s), col(2 * d + 1)] = (pr * bbi + pi * bbr).astype(m_scr.dtype)
            pr, pi = _pw_row(pw_ref, j, _POW_C[d](s), 2 * d), _pw_row(pw_ref, j, _POW_C[d](s), 2 * d + 1)
            ct_scr[j, blk(s), col(2 * d)] = (ccr * pr - cci * pi).astype(ct_scr.dtype)
            ct_scr[j, blk(s), col(2 * d + 1)] = (-(ccr * pi + cci * pr)).astype(ct_scr.dtype)
            pr, pi = _pw_row(pw_ref, j, _POW_K[d](s), 2 * d), _pw_row(pw_ref, j, _POW_K[d](s), 2 * d + 1)
            g_scr[j, d, blk(s), 0:LANES] = ccr * pr - cci * pi
            g_scr[j, d, blk(s), LANES:2 * LANES] = -(ccr * pi + cci * pr)
        kt = _mm_f32(jnp.concatenate([bbr, bbi], axis=1), g_scr[j, d], ((1,), (1,)))
        if d == 0:
            kt = jnp.concatenate([kt[:, 0:LANES] + dd_ref[j], kt[:, LANES:]], axis=1)
        kt_scr[d] = kt
    lane = lax.broadcasted_iota(jnp.int32, (1, CHUNK_W), 1)
    for s in range(SSM_CHUNK):
        lo = PAIR_W * s
        hi = PAIR_W * (s + 1)
        fwd = kt_scr[0] if s == 0 else pltpu.roll(kt_scr[0], lo, axis=1)
        bwd = kt_scr[1] if hi == CHUNK_W else pltpu.roll(kt_scr[1], hi, axis=1)
        row = jnp.where(lane >= lo, fwd, 0.0) + jnp.where(lane < hi, bwd, 0.0)
        toep_scr[j, blk(s), :] = row.astype(toep_scr.dtype)


def _ssm_scratch(nc, mat_dtype, buffers):
    mats = [pltpu.VMEM((SSM_PAIRS, CHUNK_W, CHUNK_W), mat_dtype) for _ in range(3)]
    return mats + [pltpu.VMEM((SSM_PAIRS, 2, CHUNK_W, 2 * LANES), F32), pltpu.VMEM((2, PAIR_W, CHUNK_W), F32)] + [
        pltpu.VMEM((4, nc * SLOT, LANES), F32) for _ in range(buffers)]


def _per_step(*shape):
    return pl.BlockSpec((SSM_PAIRS,) + shape, lambda g: (g,) + (0,) * len(shape))


_TABLE_SPECS = lambda: [_per_step(PW_ROWS, CHUNK_W), _per_step(4, PAIR_W, LANES), _per_step(4, PAIR_W, LANES),
                        _per_step(PAIR_W, LANES)]


def _ssm_fwd(u, pw, bb, cc, dd):
    npair, nc, width = u.shape

    def body(u_ref, pw_ref, bb_ref, cc_ref, dd_ref, y_ref, m_scr, ct_scr, toep_scr, g_scr, kt_scr, h_scr):
        @pl.when(pl.program_id(0) == 0)
        def _():
            h_scr[...] = jnp.zeros_like(h_scr)

        for j in range(SSM_PAIRS):
            _chunk_matrices(j, pw_ref, bb_ref, cc_ref, dd_ref, m_scr, ct_scr, toep_scr, g_scr, kt_scr)
            _put_pair(h_scr, j, nc, _mm(u_ref[j], m_scr[j]))
        _boundary_scan(nc, pw_ref, h_scr, conj=False)
        for j in range(SSM_PAIRS):
            hin = _get_pair(h_scr, j, nc)
            y_ref[j] = _mm(u_ref[j], toep_scr[j]) + _nt(hin.astype(u.dtype), ct_scr[j])

    return pl.pallas_call(
        body, name="ssm_fwd", grid=(npair // SSM_PAIRS,),
        in_specs=[_per_step(nc, width)] + _TABLE_SPECS(),
        out_specs=_per_step(nc, width),
        out_shape=jax.ShapeDtypeStruct((npair, nc, width), F32),
        scratch_shapes=_ssm_scratch(nc, u.dtype, 1),
        compiler_params=_params("arbitrary"),
    )(u, pw, bb, cc, dd)


def _ssm_bwd(u, dy, pw, bb, cc, dd, chip_parts=()):
    npair, nc, width = u.shape
    ns = len(chip_parts)
    steps = npair // SSM_PAIRS

    def body(u_ref, dy_ref, pw_ref, bb_ref, cc_ref, dd_ref, *rest):
        part_refs, rest = rest[:ns], rest[ns:]
        (du_ref, dpw_ref, dbb_ref, dcc_ref, ddd_ref), rest = rest[:5], rest[5:]
        land_refs, rest = rest[:ns], rest[ns:]
        m_scr, ct_scr, toep_scr, g_scr, kt_scr, h_scr, d_scr = rest[:7]
        g = pl.program_id(0)
        if ns:
            _ride(_chip_copies(part_refs, land_refs, *rest[7:]), g == 0, g == steps - 1)

        @pl.when(g == 0)
        def _():
            h_scr[...] = jnp.zeros_like(h_scr)
            d_scr[...] = jnp.zeros_like(d_scr)

        for j in range(SSM_PAIRS):
            _chunk_matrices(j, pw_ref, bb_ref, cc_ref, dd_ref, m_scr, ct_scr, toep_scr, g_scr, kt_scr)
            _put_pair(h_scr, j, nc, _mm(u_ref[j], m_scr[j]))
            _put_pair(d_scr, j, nc, _mm(dy_ref[j], ct_scr[j]))
        _boundary_scan(nc, pw_ref, h_scr, conj=False)
        _boundary_scan(nc, pw_ref, d_scr, conj=True)

        dpw_ref[...] = jnp.zeros_like(dpw_ref)
        blk = lambda s: slice(PAIR_W * s, PAIR_W * (s + 1))
        col = lambda q: slice(LANES * q, LANES * (q + 1))
        lane = lax.broadcasted_iota(jnp.int32, (1, CHUNK_W), 1)
        for j in range(SSM_PAIRS):
            uv = u_ref[j]
            dyb = dy_ref[j]
            hin = _get_pair(h_scr, j, nc)
            ds = _get_pair(d_scr, j, nc)
            dsb = ds.astype(u.dtype)
            du_ref[j] = _nt(dsb, m_scr[j]) + _nt(dyb, toep_scr[j])
            dm = _tn(uv, dsb)
            dct = _tn(dyb, hin.astype(u.dtype))
            dtoep = _tn(uv, dyb)

            def add_pw(k, q, val):
                dpw_ref[j, k:k + 1, col(q)] += jnp.sum(val, axis=0, keepdims=True)

            for d in range(2):
                g_r, g_i = ds[:, col(2 * d)], ds[:, col(2 * d + 1)]
                h_r, h_i = hin[:, col(2 * d)], hin[:, col(2 * d + 1)]
                add_pw(SSM_CHUNK, 2 * d, g_r * h_r + g_i * h_i)
                add_pw(SSM_CHUNK, 2 * d + 1, g_i * h_r - g_r * h_i)

            dkt0 = jnp.zeros((PAIR_W, CHUNK_W), F32)
            dkt1 = jnp.zeros((PAIR_W, CHUNK_W), F32)
            for s in range(SSM_CHUNK):
                lo = PAIR_W * s
                hi = PAIR_W * (s + 1)
                row = dtoep[blk(s), :]
                fwd = jnp.where(lane >= lo, row, 0.0)
                bwd = jnp.where(lane < hi, row, 0.0)
                dkt0 += fwd if s == 0 else pltpu.roll(fwd, CHUNK_W - lo, axis=1)
                dkt1 += bwd if hi == CHUNK_W else pltpu.roll(bwd, CHUNK_W - hi, axis=1)
            ddd_ref[j] = dkt0[:, 0:LANES]

            for d, dkt in enumerate((dkt0, dkt1)):
                bbr, bbi = bb_ref[j, 2 * d], bb_ref[j, 2 * d + 1]
                ccr, cci = cc_ref[j, 2 * d], cc_ref[j, 2 * d + 1]
                dbbcat = _mm_f32(dkt, g_scr[j, d], ((1,), (0,)))
                dg = _mm_f32(dkt, jnp.concatenate([bbr, bbi], axis=1), ((0,), (0,)))
                dbbr, dbbi = dbbcat[:, 0:LANES], dbbcat[:, LANES:]
                dccr = jnp.zeros((PAIR_W, LANES), F32)
                dcci = jnp.zeros((PAIR_W, LANES), F32)
                for s in range(SSM_CHUNK):
                    k = _POW_M[d](s)
                    pr, pi = _pw_row(pw_ref, j, k, 2 * d), _pw_row(pw_ref, j, k, 2 * d + 1)
                    gr, gi = dm[blk(s), col(2 * d)], dm[blk(s), col(2 * d + 1)]
                    dbbr += gr * pr + gi * pi
                    dbbi += gi * pr - gr * pi
                    add_pw(k, 2 * d, gr * bbr + gi * bbi)
                    add_pw(k, 2 * d + 1, gi * bbr - gr * bbi)
                    for k, gr, gi in ((_POW_C[d](s), dct[blk(s), col(2 * d)], dct[blk(s), col(2 * d + 1)]),
                                      (_POW_K[d](s), dg[blk(s), 0:LANES], dg[blk(s), LANES:])):
                        pr, pi = _pw_row(pw_ref, j, k, 2 * d), _pw_row(pw_ref, j, k, 2 * d + 1)
                        dccr += gr * pr - gi * pi
                        dcci += -(gr * pi + gi * pr)
                        add_pw(k, 2 * d, gr * ccr - gi * cci)
                        add_pw(k, 2 * d + 1, -(gr * cci + gi * ccr))
                dbb_ref[j, 2 * d] = dbbr
                dbb_ref[j, 2 * d + 1] = dbbi
                dcc_ref[j, 2 * d] = dccr
                dcc_ref[j, 2 * d + 1] = dcci

    out = pl.pallas_call(
        body, name="ssm_bwd", grid=(steps,),
        in_specs=[_per_step(nc, width), _per_step(nc, width)] + _TABLE_SPECS() + [_ANY()] * ns,
        out_specs=[_per_step(nc, width)] + _TABLE_SPECS() + [_ANY()] * ns,
        out_shape=[jax.ShapeDtypeStruct((npair, nc, width), F32), jax.ShapeDtypeStruct(pw.shape, F32),
                   jax.ShapeDtypeStruct(bb.shape, F32), jax.ShapeDtypeStruct(cc.shape, F32),
                   jax.ShapeDtypeStruct(dd.shape, F32)]
        + [jax.ShapeDtypeStruct(pt.shape, pt.dtype) for pt in chip_parts],
        scratch_shapes=_ssm_scratch(nc, u.dtype, 2) + ([_dma_sems(ns, 3), _dma_sems(ns, 3), _dma_sems(ns)] if ns else []),
        compiler_params=_params("arbitrary"),
    )(u, dy, pw, bb, cc, dd, *chip_parts)
    return out[:5], list(out[5:])


NA_Q = NA_QROWS * GRID_W
NA_K = NA_KROWS * GRID_W
NA_SCALE = NA_HEAD_DIM ** -0.5


def _na_block(b, nb, rows):
    start = jnp.clip(NA_QROWS * b - NA_ROWS // 2, 0, rows - NA_KROWS) * GRID_W
    kind = jnp.where(b == 0, 0, jnp.where(b == nb - 1, 2, 1))
    return pl.multiple_of(start, GRID_W), kind


NA_CHUNK = 16


def _na_pieces(kind, i):
    ri, q0 = divmod(i * NA_CHUNK, GRID_W)
    off = (NA_ROWS - 1, NA_ROWS // 2 - 1, -1)[kind]
    lo = (0, ri, NA_KROWS - NA_ROWS)[kind]
    modes = {(True, True): 'both', (True, False): 'even', (False, True): 'odd', (False, False): None}
    out = []
    for k2 in range(NA_KROWS // 2):
        inside = tuple(lo <= kr < lo + NA_ROWS for kr in (2 * k2, 2 * k2 + 1))
        out.append((2 * k2 - ri + off + 1, slice(q0, q0 + NA_CHUNK), modes[inside]))
    return out


def _na_softmax_pieces(s_ref, tab_ref, hh, kind, i):
    rows = slice(i * NA_CHUNK, (i + 1) * NA_CHUNK)
    lane = lax.broadcasted_iota(jnp.int32, (1, LANES), 1)
    xs = []
    for k2, (t, q, mode) in enumerate(_na_pieces(kind, i)):
        if mode is None:
            xs.append(None)
            continue
        bias = tab_ref[hh, t, q, :]
        if mode == 'even':
            bias = jnp.where(lane < GRID_W, bias, NEG)
        elif mode == 'odd':
            bias = jnp.where(lane >= GRID_W, bias, NEG)
        xs.append(s_ref[hh, rows, k2 * LANES:(k2 + 1) * LANES] + bias)
    live = [x for x in xs if x is not None]
    m = jnp.max(functools.reduce(jnp.maximum, live), axis=-1, keepdims=True)
    es = [None if x is None else jnp.exp(x - m) for x in xs]
    total = jnp.sum(functools.reduce(jnp.add, [e for e in es if e is not None]), axis=-1, keepdims=True)
    inv = 1.0 / total
    return [None if e is None else e * inv for e in es]


def _na_heads():
    lane = lax.broadcasted_iota(jnp.int32, (1, LANES), 1)
    return [lane < NA_HEAD_DIM, lane >= NA_HEAD_DIM]


def _na_fwd(qkv, bias):
    seq = qkv.shape[0]
    rows = seq // GRID_W
    nb = rows // NA_QROWS

    def body(q_ref, k_ref, v_ref, bias_ref, o_ref, p_ref, s_scr):
        start, kind = _na_block(pl.program_id(1), nb, rows)

        def block(static_kind):
            q2 = q_ref[...] * NA_SCALE
            kw = k_ref[pl.ds(start, NA_K), :]
            vw = v_ref[pl.ds(start, NA_K), :]
            heads = _na_heads()
            for hh in range(2):
                s_scr[hh] = _nt(jnp.where(heads[hh], q2, jnp.zeros_like(q2)), kw)
            for hh in range(2):
                for i in range(NA_Q // NA_CHUNK):
                    r = slice(i * NA_CHUNK, (i + 1) * NA_CHUNK)
                    for k2, p in enumerate(_na_softmax_pieces(s_scr, bias_ref, hh, static_kind, i)):
                        p = jnp.zeros((NA_CHUNK, LANES), F32) if p is None else p
                        p_ref[hh, r, k2 * LANES:(k2 + 1) * LANES] = p.astype(p_ref.dtype)
            o_ref[...] = jnp.where(heads[0], _mm(p_ref[0], vw), _mm(p_ref[1], vw))

        for static_kind in range(3):
            pl.when(kind == static_kind)(functools.partial(block, static_kind))

    return pl.pallas_call(
        body, name="na_fwd", grid=(NA_HEADS // 2, nb),
        in_specs=[pl.BlockSpec((NA_Q, LANES), lambda hp, b: (b, hp)),
                  pl.BlockSpec((seq, LANES), lambda hp, b: (0, 4 + hp)),
                  pl.BlockSpec((seq, LANES), lambda hp, b: (0, 8 + hp)),
                  pl.BlockSpec((2, NA_TAB, GRID_W, LANES), lambda hp, b: (hp, 0, 0, 0))],
        out_specs=[pl.BlockSpec((NA_Q, LANES), lambda hp, b: (b, hp)),
                   pl.BlockSpec((2, NA_Q, NA_K), lambda hp, b: (hp, b, 0))],
        out_shape=[jax.ShapeDtypeStruct((seq, D_NA), F32), jax.ShapeDtypeStruct((NA_HEADS, seq, NA_K), qkv.dtype)],
        scratch_shapes=[pltpu.VMEM((2, NA_Q, NA_K), F32)],
        compiler_params=_params("arbitrary", "arbitrary"),
    )(qkv, qkv, qkv, bias)


def _na_bwd(qkv, do, probs, parts=()):
    seq = qkv.shape[0]
    rows = seq // GRID_W
    nb = rows // NA_QROWS
    ns = len(parts)

    def body(q_ref, k_ref, v_ref, do_ref, p_ref, *rest):
        part_refs, rest = rest[:ns], rest[ns:]
        (dq_ref, dk_ref, dv_ref, dbias_ref), rest = rest[:4], rest[4:]
        land_refs, rest = rest[:ns], rest[ns:]
        dp_scr, ds_scr = rest[:2]
        b = pl.program_id(1)
        start, kind = _na_block(b, nb, rows)
        if ns:
            hp = pl.program_id(0)
            _ride(_pair_copies(part_refs, land_refs, *rest[2:]), (hp == 0) & (b == 0),
                  (hp == NA_HEADS // 2 - 1) & (b == nb - 1))

        @pl.when(b == 0)
        def _():
            dk_ref[...] = jnp.zeros_like(dk_ref)
            dv_ref[...] = jnp.zeros_like(dv_ref)
            dbias_ref[...] = jnp.zeros_like(dbias_ref)

        def block(static_kind):
            q2 = q_ref[...] * NA_SCALE
            kw = k_ref[pl.ds(start, NA_K), :]
            vw = v_ref[pl.ds(start, NA_K), :]
            do2 = do_ref[...].astype(q2.dtype)
            heads = _na_heads()
            col = lambda k2: slice(k2 * LANES, (k2 + 1) * LANES)
            zero = jnp.zeros((NA_CHUNK, LANES), ds_scr.dtype)
            for hh in range(2):
                dp_scr[hh] = _nt(jnp.where(heads[hh], do2, jnp.zeros_like(do2)), vw)
            for hh in range(2):
                for i in range(NA_Q // NA_CHUNK):
                    r = slice(i * NA_CHUNK, (i + 1) * NA_CHUNK)
                    pieces = _na_pieces(static_kind, i)
                    ps = [None if mode is None else p_ref[hh, r, col(k2)].astype(F32)
                          for k2, (_, _, mode) in enumerate(pieces)]
                    dps = [None if p is None else dp_scr[hh, r, col(k2)] for k2, p in enumerate(ps)]
                    pdp = functools.reduce(jnp.add, [p * dp for p, dp in zip(ps, dps) if p is not None])
                    rowsum = jnp.sum(pdp, axis=-1, keepdims=True)
                    for k2, (t, q, _) in enumerate(pieces):
                        if ps[k2] is None:
                            ds_scr[hh, r, col(k2)] = zero
                            continue
                        ds = ps[k2] * (dps[k2] - rowsum)
                        dbias_ref[hh, t, q, :] += ds
                        ds_scr[hh, r, col(k2)] = ds.astype(ds_scr.dtype)
            dq_ref[...] = jnp.where(heads[0], _mm(ds_scr[0], kw), _mm(ds_scr[1], kw)) * NA_SCALE
            dk_ref[pl.ds(start, NA_K), :] += jnp.where(heads[0], _tn(ds_scr[0], q2), _tn(ds_scr[1], q2))
            dv_ref[pl.ds(start, NA_K), :] += jnp.where(heads[0], _tn(p_ref[0], do2), _tn(p_ref[1], do2))

        for static_kind in range(3):
            pl.when(kind == static_kind)(functools.partial(block, static_kind))

    out = pl.pallas_call(
        body, name="na_bwd", grid=(NA_HEADS // 2, nb),
        in_specs=[pl.BlockSpec((NA_Q, LANES), lambda hp, b: (b, hp)),
                  pl.BlockSpec((seq, LANES), lambda hp, b: (0, 4 + hp)),
                  pl.BlockSpec((seq, LANES), lambda hp, b: (0, 8 + hp)),
                  pl.BlockSpec((NA_Q, LANES), lambda hp, b: (b, hp)),
                  pl.BlockSpec((2, NA_Q, NA_K), lambda hp, b: (hp, b, 0))] + [_ANY()] * ns,
        out_specs=[pl.BlockSpec((NA_Q, LANES), lambda hp, b: (b, hp)),
                   pl.BlockSpec((seq, LANES), lambda hp, b: (0, hp)),
                   pl.BlockSpec((seq, LANES), lambda hp, b: (0, hp)),
                   pl.BlockSpec((2, NA_TAB, GRID_W, LANES), lambda hp, b: (hp, 0, 0, 0))] + [_ANY()] * ns,
        out_shape=[jax.ShapeDtypeStruct((seq, D_NA), F32), jax.ShapeDtypeStruct((seq, D_NA), F32),
                   jax.ShapeDtypeStruct((seq, D_NA), F32),
                   jax.ShapeDtypeStruct((NA_HEADS, NA_TAB, GRID_W, LANES), F32)]
        + [jax.ShapeDtypeStruct((4,) + pt.shape[1:], pt.dtype) for pt in parts],
        scratch_shapes=[pltpu.VMEM((2, NA_Q, NA_K), F32), pltpu.VMEM((2, NA_Q, NA_K), qkv.dtype)]
        + ([_dma_sems(ns, 4), _dma_sems(ns, 4)] if ns else []),
        compiler_params=_params("arbitrary", "arbitrary"),
    )(qkv, qkv, qkv, do, probs, *parts)
    return out[:4], list(out[4:])


def _local_step(x2, p2, tgt, g_pre, g_post, w_in_g, ssm, w_glu, b_glu, rpb, w_out, w_ple, g_ple, w_pg,
                distributed=False):
    seq = x2.shape[0]
    nc = seq // SSM_CHUNK

    (pw, bb, cc, dd), ssm_vjp = jax.vjp(_ssm_tables, *ssm)
    bias, bias_vjp = jax.vjp(_na_table, rpb)

    riders = (w_glu, w_out, w_ple, w_pg) if distributed else ()
    (u, zs, qkv, zn, hn), gathered = _fwd_in(x2, g_pre, w_in_g, riders)
    if distributed:
        w_glu, w_out, w_ple, w_pg = gathered
        w_glu = w_glu.reshape(D_SSM, D_SSM)
        w_out = w_out.reshape(D_MODEL, D_MODEL)
        w_pg = w_pg.reshape(D_MODEL, D_MODEL)
    u_p = _to_pairs(u, nc)
    y_p = _ssm_fwd(u_p, pw, bb, cc, dd)
    yssm = _from_pairs(y_p, nc)
    o, probs = _na_fwd(qkv, bias)
    dyssm, dzs, do, dzn, dh1, sq, d_wglu, d_bglu, d_wout, d_wple, d_wpg, d_gpost, d_gple = _mid(
        yssm, zs, o, zn, x2, p2, tgt, w_glu, b_glu, w_out, g_post, w_ple, g_ple, w_pg)
    mid_grads = [d_wglu.reshape(N_DEV, D_SSM // N_DEV, D_SSM), d_wout.reshape(N_DEV, D_MODEL // N_DEV, D_MODEL),
                 d_wple, d_wpg.reshape(N_DEV, D_MODEL // N_DEV, D_MODEL)]
    (dq, dk, dv, dbias), lands = _na_bwd(qkv, do, probs, mid_grads if distributed else ())
    (d_rpb,) = bias_vjp(dbias)
    chip_parts = _pair_sum(mid_grads, lands)[0] if distributed else ()
    (du_p, dpw, dbb, dcc, ddd), chip_lands = _ssm_bwd(u_p, _to_pairs(dyssm, nc), pw, bb, cc, dd, chip_parts)
    d_ssm = ssm_vjp((dpw, dbb, dcc, ddd))
    du = _from_pairs(du_p, nc)
    dx, d_win, d_gpre = _bwd_in(du, dzs, dq, dk, dv, dzn, hn, x2, dh1, g_pre, w_in_g)
    if distributed:
        d_wglu, d_wout, d_wple, d_wpg = chip_lands
    return sq, dx, dict(norm_pre=d_gpre, norm_post=d_gpost, w_in=d_win, ssm=d_ssm, w_glu=d_wglu, b_glu=d_bglu,
                        na_rpb=d_rpb, w_out=d_wout, w_ple=d_wple, ple_norm=d_gple, w_ple_gate=d_wpg)


def _all_gather(shard):
    m_per, n = shard.shape

    def body(x_ref, out_ref, send_sems, recv_sems, local_sem):
        x, y, c = _place()
        me, sibling = (x, y, c), (x, y, 1 - c)
        chips = [(1 - x, y), (x, 1 - y), (1 - x, 1 - y)]

        def rows(px, py, pc):
            return out_ref.at[pl.ds((4 * px + 2 * py + pc) * m_per, m_per), :]

        def copy(k, block, to, src=None):
            return pltpu.make_async_remote_copy(
                src_ref=rows(*block) if src is None else src, dst_ref=rows(*block),
                send_sem=send_sems.at[k], recv_sem=recv_sems.at[k], device_id=to, device_id_type=MESH)

        mine = pltpu.make_async_copy(x_ref, rows(*me), local_sem)
        mine.start()
        first = [copy(0, me, sibling, src=x_ref)]
        first += [copy(1 + j, me, (*chip, c), src=x_ref) for j, chip in enumerate(chips)]
        for cp in first:
            cp.start()
        passed = [copy(4 + j, (*chip, c), sibling) for j, chip in enumerate(chips)]
        for j, chip in enumerate(chips):
            copy(1 + j, (*chip, c), me).wait_recv()
            passed[j].start()
        copy(0, sibling, me).wait_recv()
        for j, chip in enumerate(chips):
            copy(4 + j, (*chip, 1 - c), me).wait_recv()
        for cp in first + passed:
            cp.wait_send()
        mine.wait()

    return pl.pallas_call(
        body, name="all_gather",
        out_shape=jax.ShapeDtypeStruct((N_DEV * m_per, n), shard.dtype),
        in_specs=[_whole_vmem()], out_specs=_whole_vmem(),
        scratch_shapes=[pltpu.SemaphoreType.DMA((7,)), pltpu.SemaphoreType.DMA((7,)), pltpu.SemaphoreType.DMA],
        compiler_params=pltpu.CompilerParams(vmem_limit_bytes=VMEM_LIMIT),
    )(shard)


def _exchange_first(pair_parts, all_parts):
    n1, n2 = len(pair_parts), len(all_parts)

    def body(*refs):
        ins, refs = refs[:n1 + n2], refs[n1 + n2:]
        outs, sems = refs[:n1 + n2], refs[n1 + n2:]
        copies = (_pair_copies(ins[:n1], outs[:n1], *sems[:2])
                  + _alltoall_copies(ins[n1:], outs[n1:], *sems[2:]))
        _start_all(copies)
        _wait_all(copies)

    out = pl.pallas_call(
        body, name="exchange_first",
        out_shape=[jax.ShapeDtypeStruct((4,) + pt.shape[1:], pt.dtype) for pt in pair_parts]
        + [jax.ShapeDtypeStruct(pt.shape, pt.dtype) for pt in all_parts],
        in_specs=[_ANY()] * (n1 + n2), out_specs=[_ANY()] * (n1 + n2),
        scratch_shapes=[_dma_sems(n1, 4), _dma_sems(n1, 4),
                        _dma_sems(n2, N_DEV - 1), _dma_sems(n2, N_DEV - 1), _dma_sems(n2)],
    )(*pair_parts, *all_parts)
    return list(out[:n1]), list(out[n1:])


def _pair_sum(parts, lands, all_lands=()):
    ns, na = len(parts), len(all_lands)

    def body(*refs):
        c = lax.axis_index("c")
        ins, outs = refs[:2 * ns + na], refs[2 * ns + na:]
        for part_ref, land_ref, out_ref in zip(ins[:ns], ins[ns:2 * ns], outs[:ns]):
            for q in range(4):
                out_ref[q] = (part_ref[q, c] + land_ref[q]).astype(BF16)
        for land_ref, out_ref in zip(ins[2 * ns:], outs[ns:]):
            acc = land_ref[0]
            for j in range(1, N_DEV):
                acc = acc + land_ref[j]
            out_ref[...] = acc

    out = pl.pallas_call(
        body, name="pair_sum",
        in_specs=[_whole_vmem()] * (2 * ns + na), out_specs=[_whole_vmem()] * (ns + na),
        out_shape=[jax.ShapeDtypeStruct(ld.shape, BF16) for ld in lands]
        + [jax.ShapeDtypeStruct(ld.shape[1:], ld.dtype) for ld in all_lands],
        compiler_params=pltpu.CompilerParams(vmem_limit_bytes=VMEM_LIMIT),
    )(*[pt.reshape((4, 2) + pt.shape[1:]) for pt in parts], *lands, *all_lands)
    return list(out[:ns]), list(out[ns:])


def _exchange_second(chip_parts, shards):
    n1, n2 = len(chip_parts), len(shards)

    def body(*refs):
        ins, refs = refs[:n1 + n2], refs[n1 + n2:]
        outs, sems = refs[:n1 + n2], refs[n1 + n2:]
        copies = (_chip_copies(ins[:n1], outs[:n1], *sems[:3])
                  + _gather_copies(ins[n1:], outs[n1:], *sems[3:]))
        _start_all(copies)
        _wait_all(copies)

    out = pl.pallas_call(
        body, name="exchange_second",
        out_shape=[jax.ShapeDtypeStruct(pt.shape, pt.dtype) for pt in chip_parts]
        + [jax.ShapeDtypeStruct((N_DEV,) + sh.shape, sh.dtype) for sh in shards],
        in_specs=[_ANY()] * (n1 + n2), out_specs=[_ANY()] * (n1 + n2),
        scratch_shapes=[_dma_sems(n1, 3), _dma_sems(n1, 3), _dma_sems(n1),
                        _dma_sems(n2, N_DEV - 1), _dma_sems(n2, N_DEV - 1), _dma_sems(n2)],
    )(*chip_parts, *shards)
    return list(out[:n1]), list(out[n1:])


def _adamw(w, g, m, v):
    r, c = w.shape
    tile = 256 if r % 256 == 0 else r
    slots = g.shape[0] if g.ndim == 3 else 0

    def body(w_ref, g_ref, m_ref, v_ref, go_ref, d_ref, nm_ref, nv_ref):
        if slots:
            g = g_ref[0].astype(F32)
            for j in range(1, slots):
                g = g + g_ref[j].astype(F32)
        else:
            g = g_ref[...]
        go_ref[...] = g
        nm = ADAM_B1 * m_ref[...] + (1.0 - ADAM_B1) * g
        nv = ADAM_B2 * v_ref[...] + (1.0 - ADAM_B2) * (g * g)
        m_hat = nm / (1.0 - ADAM_B1 ** ADAM_STEP)
        v_hat = nv / (1.0 - ADAM_B2 ** ADAM_STEP)
        d_ref[...] = -ADAM_LR * (m_hat / (jnp.sqrt(v_hat) + ADAM_EPS) + ADAM_WD * w_ref[...])
        nm_ref[...] = nm
        nv_ref[...] = nv

    spec = pl.BlockSpec((tile, c), lambda i: (i, 0))
    g_spec = pl.BlockSpec((slots, tile, c), lambda i: (0, i, 0)) if slots else spec
    out = jax.ShapeDtypeStruct((r, c), F32)
    return pl.pallas_call(
        body, name="adamw", grid=(r // tile,), in_specs=[spec, g_spec, spec, spec], out_specs=[spec] * 4,
        out_shape=[out] * 4, compiler_params=_params("arbitrary"),
    )(w, g, m, v)


_SLAB = 8 * LANES


def _flat_rows(a):
    flat = a.reshape(-1)
    pad = (-flat.shape[0]) % _SLAB
    if pad:
        flat = jnp.concatenate([flat, jnp.zeros((pad,), flat.dtype)])
    return flat.reshape(-1, LANES)


def _pack(arrays):
    slabs = [_flat_rows(a) for a in arrays]
    return jnp.concatenate(slabs, axis=0), [s.shape[0] for s in slabs]


def _unpack(packed, like, rows):
    out, at = [], 0
    for a, r in zip(like, rows):
        out.append(packed[at:at + r].reshape(-1)[:a.size].reshape(a.shape))
        at += r
    return out


SMALL = ('norm_pre', 'norm_post', 'ssm_a_re', 'ssm_a_im', 'ssm_log_dt', 'ssm_b_re', 'ssm_b_im', 'ssm_c_re',
         'ssm_c_im', 'ssm_d', 'b_glu', 'na_rpb', 'ple_norm')
BIG = ('w_in', 'w_glu', 'w_out', 'w_ple', 'w_ple_gate')
ORDER = ('norm_pre', 'norm_post', 'w_in', 'ssm_a_re', 'ssm_a_im', 'ssm_log_dt', 'ssm_b_re', 'ssm_b_im', 'ssm_c_re',
         'ssm_c_im', 'ssm_d', 'w_glu', 'b_glu', 'na_rpb', 'w_out', 'w_ple', 'ple_norm', 'w_ple_gate')


def kernel(x, p, norm_pre, norm_post, w_in, ssm_a_re, ssm_a_im, ssm_log_dt, ssm_b_re, ssm_b_im, ssm_c_re, ssm_c_im, ssm_d, w_glu, b_glu, na_rpb, w_out, w_ple, ple_norm, w_ple_gate, loss_target, m_norm_pre, m_norm_post, m_w_in, m_ssm_a_re, m_ssm_a_im, m_ssm_log_dt, m_ssm_b_re, m_ssm_b_im, m_ssm_c_re, m_ssm_c_im, m_ssm_d, m_w_glu, m_b_glu, m_na_rpb, m_w_out, m_w_ple, m_ple_norm, m_w_ple_gate, v_norm_pre, v_norm_post, v_w_in, v_ssm_a_re, v_ssm_a_im, v_ssm_log_dt, v_ssm_b_re, v_ssm_b_im, v_ssm_c_re, v_ssm_c_im, v_ssm_d, v_w_glu, v_b_glu, v_na_rpb, v_w_out, v_w_ple, v_ple_norm, v_w_ple_gate):
    args = dict(locals())
    weights = {n: args[n] for n in ORDER}
    mom_m = {n: args["m_" + n] for n in ORDER}
    mom_v = {n: args["v_" + n] for n in ORDER}

    w_in_g = _all_gather(w_in[0].astype(BF16)).reshape(N_DEV, D_MODEL, SHARD_IN)
    blocks = [weights[n][0].astype(BF16) for n in ('w_glu', 'w_out', 'w_ple', 'w_ple_gate')]

    ssm = tuple(weights[n][0] for n in ('ssm_a_re', 'ssm_a_im', 'ssm_log_dt', 'ssm_b_re', 'ssm_b_im',
                                        'ssm_c_re', 'ssm_c_im', 'ssm_d'))
    sq, dx, grads = _local_step(x[0], p[0, 0], loss_target[0], norm_pre, norm_post, w_in_g, ssm, blocks[0], b_glu,
                                na_rpb[0], blocks[1], blocks[2], ple_norm, blocks[3], distributed=True)
    loss_local = (0.5 / D_MODEL * jnp.sum(sq)).reshape(1)

    local = dict(norm_pre=grads['norm_pre'], norm_post=grads['norm_post'], b_glu=grads['b_glu'],
                 na_rpb=grads['na_rpb'][None], ple_norm=grads['ple_norm'])
    for n, g in zip(('ssm_a_re', 'ssm_a_im', 'ssm_log_dt', 'ssm_b_re', 'ssm_b_im', 'ssm_c_re', 'ssm_c_im', 'ssm_d'),
                    grads['ssm']):
        local[n] = g[None]

    small_flat, small_rows = _pack([local[n] for n in SMALL] + [loss_local])
    pad = (-small_flat.shape[0]) % (8 * N_DEV)
    if pad:
        small_flat = jnp.concatenate([small_flat, jnp.zeros((pad, LANES), F32)], axis=0)
    small_per = small_flat.shape[0] // N_DEV
    d_win = [grads['w_in']]
    lands, small_lands = _exchange_first(d_win, [small_flat.reshape(N_DEV, small_per, LANES)])
    chip_parts, small_sums = _pair_sum(d_win, lands, small_lands)
    chip_lands, small_full = _exchange_second(chip_parts, small_sums)
    grads['w_in'] = chip_lands[0]
    small_all = small_full[0].reshape(N_DEV * small_per, LANES)

    outs = {}
    for n in BIG:
        g, d, nm, nv = _adamw(weights[n][0], grads[n], mom_m[n][0], mom_v[n][0])
        outs[n] = (g[None], d[None], nm[None], nv[None])
    like = [weights[n] for n in SMALL]
    w_s, rows_s = _pack(like)
    m_s, _ = _pack([mom_m[n] for n in SMALL])
    v_s, _ = _pack([mom_v[n] for n in SMALL])
    loss = small_all[w_s.shape[0], 0]
    unpacked = [_unpack(a, like, rows_s) for a in _adamw(w_s, small_all[:w_s.shape[0]], m_s, v_s)]
    for i, n in enumerate(SMALL):
        outs[n] = tuple(u[i] for u in unpacked)

    return (loss, dx[None], *[outs[n][0] for n in ORDER], *[outs[n][1] for n in ORDER],
            *[outs[n][2] for n in ORDER], *[outs[n][3] for n in ORDER])
```

```python
import functools

import jax
import jax.numpy as jnp
from jax import lax
from jax.experimental import pallas as pl
from jax.experimental.pallas import tpu as pltpu

F32 = jnp.float32
BF16 = jnp.bfloat16
HIGHEST = lax.Precision.HIGHEST

D_MODEL = 1024
D_PLE = 256
GRID_W = 64
D_SSM = 512
SSM_GROUP = 16
N_PAIRS = 16
SSM_STATE = 64
D_NA = 512
NA_HEADS = 8
NA_HEAD_DIM = 64
NA_ROWS = 8
NA_COLS = 16
D_IN_PROJ = 3072
EPS = 1e-6
N_DEV = 8
SHARD_IN = D_IN_PROJ // N_DEV
LANES = 128
SSM_CHUNK = 16
PW_ROWS = 24
TOK_TILE = 256
NA_QROWS = 4
NA_KROWS = 12
NEG = -1e30
VMEM_LIMIT = 56 * 1024 * 1024

ADAM_LR = 0.001
ADAM_B1 = 0.9
ADAM_B2 = 0.999
ADAM_EPS = 1e-08
ADAM_WD = 0.01
ADAM_STEP = 10

MESH = pl.DeviceIdType.MESH


def _params(*sem):
    return pltpu.CompilerParams(dimension_semantics=sem or None, vmem_limit_bytes=VMEM_LIMIT)


def _whole_vmem():
    return pl.BlockSpec(memory_space=pltpu.VMEM)


def _nt(a, b):
    return lax.dot_general(a, b, (((1,), (1,)), ((), ())), preferred_element_type=F32)


def _tn(a, b):
    return lax.dot_general(a, b, (((0,), (0,)), ((), ())), preferred_element_type=F32)


def _mm(a, b):
    return jnp.dot(a, b, preferred_element_type=F32)


def _sigmoid(x):
    return 1.0 / (1.0 + jnp.exp(-x))


_GELU_C = 0.7978845608028654


def _gelu(x):
    return 0.5 * x * (1.0 + jnp.tanh(_GELU_C * (x + 0.044715 * x * x * x)))


def _gelu_grad(x):
    th = jnp.tanh(_GELU_C * (x + 0.044715 * x * x * x))
    return 0.5 * (1.0 + th) + 0.5 * x * (1.0 - th * th) * _GELU_C * (1.0 + 3.0 * 0.044715 * x * x)


def _ssm_tables(a_re, a_im, log_dt, b_re, b_im, c_re, c_im, d):
    T, P, H = SSM_CHUNK, SSM_STATE, SSM_GROUP
    dt = jnp.exp(log_dt)[..., None]
    xr = dt * a_re
    xi = dt * a_im
    mag = jnp.exp(xr)
    lr = mag * jnp.cos(xi)
    li = mag * jnp.sin(xi)
    den = a_re * a_re + a_im * a_im
    cr = ((lr - 1.0) * a_re + li * a_im) / den
    ci = (li * a_re - (lr - 1.0) * a_im) / den
    bbr = cr[..., None] * b_re - ci[..., None] * b_im
    bbi = cr[..., None] * b_im + ci[..., None] * b_re
    kk = jnp.arange(T + 1, dtype=F32)[:, None, None, None]
    pm = jnp.exp(kk * xr)
    pw = jnp.stack([pm * jnp.cos(kk * xi), pm * jnp.sin(kk * xi)], axis=2)
    pw = pw.reshape(T + 1, 2, 2, N_PAIRS, 2 * P).transpose(3, 0, 1, 2, 4).reshape(N_PAIRS, T + 1, 8 * P)
    pw = jnp.concatenate([pw, jnp.zeros((N_PAIRS, PW_ROWS - (T + 1), 8 * P), F32)], axis=1)
    eye2 = jnp.eye(2, dtype=F32)

    def expand(t):
        t = t.transpose(2, 0, 1, 3, 4, 5)
        t = t[:, :, :, :, :, None, :] * eye2[None, None, None, :, None, :, None]
        return t.reshape(N_PAIRS, 4, 2 * H, 2 * P)

    bb = expand(jnp.stack([bbr, bbi], axis=1).reshape(2, 2, N_PAIRS, 2, P, H).transpose(0, 1, 2, 3, 5, 4))
    cc = expand(jnp.stack([c_re, c_im], axis=1).reshape(2, 2, N_PAIRS, 2, H, P))
    dd = d.reshape(N_PAIRS, 2 * H)[:, :, None] * jnp.eye(2 * H, dtype=F32)[None]
    dd = jnp.concatenate([dd, jnp.zeros((N_PAIRS, 2 * H, LANES - 2 * H), F32)], axis=2)
    return pw, bb, cc, dd


NA_TAB = 2 * NA_ROWS


def _na_table(rpb):
    qc = jnp.arange(GRID_W)[:, None, None]
    kc = (jnp.arange(2 * GRID_W) % GRID_W)[None, :, None]
    dc = jnp.arange(2 * NA_COLS - 1)[None, None, :]
    cstart = jnp.clip(qc - NA_COLS // 2, 0, GRID_W - NA_COLS)
    csel = ((kc >= cstart) & (kc < cstart + NA_COLS) & (kc - qc + NA_COLS - 1 == dc)).astype(F32)
    col_ok = jnp.sum(csel, axis=-1) > 0.5
    part = jnp.einsum('hrc,qmc->hrqm', rpb, csel, precision=HIGHEST)
    zero = jnp.zeros_like(part[:, :1])
    odd = jnp.arange(2 * GRID_W) >= GRID_W
    tab = jnp.where(odd, jnp.concatenate([part, zero], axis=1), jnp.concatenate([zero, part], axis=1))
    return jnp.where(col_ok, tab, NEG)


def _place():
    return lax.axis_index("x"), lax.axis_index("y"), lax.axis_index("c")


def _remote(src, dst, send_sem, recv_sem, device):
    return pltpu.make_async_remote_copy(src_ref=src, dst_ref=dst, send_sem=send_sem, recv_sem=recv_sem,
                                        device_id=device, device_id_type=MESH)


def _start_all(copies):
    for cp in copies:
        cp.start()


def _wait_all(copies):
    for cp in copies:
        cp.wait()


def _gather_copies(shard_refs, full_refs, send_sems, recv_sems, local_sems):
    x, y, c = _place()
    me = 4 * x + 2 * y + c
    out = []
    for t, (shard, full) in enumerate(zip(shard_refs, full_refs)):
        out.append(pltpu.make_async_copy(shard, full.at[me], local_sems.at[t]))
        for k in range(1, N_DEV):
            peer = (x ^ ((k >> 2) & 1), y ^ ((k >> 1) & 1), c ^ (k & 1))
            out.append(_remote(shard, full.at[me], send_sems.at[t, k - 1], recv_sems.at[t, k - 1], peer))
    return out


def _pair_copies(part_refs, land_refs, send_sems, recv_sems):
    x, y, c = _place()
    out = []
    for t, (part, land) in enumerate(zip(part_refs, land_refs)):
        for q in range(4):
            out.append(_remote(part.at[2 * q + (1 - c)], land.at[q], send_sems.at[t, q], recv_sems.at[t, q],
                               (x, y, 1 - c)))
    return out


def _chip_copies(part_refs, land_refs, send_sems, recv_sems, local_sems):
    x, y, c = _place()
    mine = 2 * x + y
    out = []
    for t, (part, land) in enumerate(zip(part_refs, land_refs)):
        out.append(pltpu.make_async_copy(part.at[mine], land.at[mine], local_sems.at[t]))
        for k in range(1, 4):
            px, py = x ^ (k >> 1), y ^ (k & 1)
            out.append(_remote(part.at[2 * px + py], land.at[mine], send_sems.at[t, k - 1], recv_sems.at[t, k - 1],
                               (px, py, c)))
    return out


def _alltoall_copies(part_refs, land_refs, send_sems, recv_sems, local_sems):
    x, y, c = _place()
    me = 4 * x + 2 * y + c
    out = []
    for t, (part, land) in enumerate(zip(part_refs, land_refs)):
        out.append(pltpu.make_async_copy(part.at[me], land.at[me], local_sems.at[t]))
        for k in range(1, N_DEV):
            px, py, pc = x ^ ((k >> 2) & 1), y ^ ((k >> 1) & 1), c ^ (k & 1)
            out.append(_remote(part.at[4 * px + 2 * py + pc], land.at[me], send_sems.at[t, k - 1],
                               recv_sems.at[t, k - 1], (px, py, pc)))
    return out


def _ride(copies, first, last):
    pl.when(first)(functools.partial(_start_all, copies))
    pl.when(last)(functools.partial(_wait_all, copies))


_ANY = lambda: pl.BlockSpec(memory_space=pl.ANY)


def _dma_sems(*shape):
    return pltpu.SemaphoreType.DMA(shape)


def _fwd_in(x2, g_pre, w_in_g, shards=()):
    seq = x2.shape[0]
    ns = len(shards)
    steps = seq // TOK_TILE

    def body(x_ref, g_ref, w_ref, *rest):
        shard_refs, rest = rest[:ns], rest[ns:]
        (u_ref, zs_ref, qkv_ref, zn_ref, hn_ref), rest = rest[:5], rest[5:]
        if ns:
            i = pl.program_id(0)
            _ride(_gather_copies(shard_refs, rest[:ns], *rest[ns:]), i == 0, i == steps - 1)
        x = x_ref[...]
        r = lax.rsqrt(jnp.mean(x * x, axis=-1, keepdims=True) + EPS)
        hn = (x * r * g_ref[...]).astype(BF16)
        hn_ref[...] = hn
        for j in range(N_DEV):
            pj = _mm(hn, w_ref[j])
            for i in range(SHARD_IN // LANES):
                blk = (SHARD_IN // LANES) * j + i
                piece = pj[:, i * LANES:(i + 1) * LANES]
                if blk < 4:
                    u_ref[:, blk * LANES:(blk + 1) * LANES] = piece
                elif blk < 8:
                    zs_ref[:, (blk - 4) * LANES:(blk - 3) * LANES] = piece
                elif blk < 20:
                    qkv_ref[:, (blk - 8) * LANES:(blk - 7) * LANES] = piece.astype(BF16)
                else:
                    zn_ref[:, (blk - 20) * LANES:(blk - 19) * LANES] = piece

    tok = lambda w: pl.BlockSpec((TOK_TILE, w), lambda i: (i, 0))
    out = pl.pallas_call(
        body, name="fwd_in", grid=(steps,),
        in_specs=[tok(D_MODEL), pl.BlockSpec((1, D_MODEL), lambda i: (0, 0)), _whole_vmem()] + [_ANY()] * ns,
        out_specs=[tok(D_SSM), tok(D_SSM), tok(3 * D_NA), tok(D_NA), tok(D_MODEL)] + [_ANY()] * ns,
        out_shape=[jax.ShapeDtypeStruct((seq, D_SSM), F32), jax.ShapeDtypeStruct((seq, D_SSM), F32),
                   jax.ShapeDtypeStruct((seq, 3 * D_NA), BF16), jax.ShapeDtypeStruct((seq, D_NA), F32),
                   jax.ShapeDtypeStruct((seq, D_MODEL), BF16)]
        + [jax.ShapeDtypeStruct((N_DEV,) + sh.shape, sh.dtype) for sh in shards],
        scratch_shapes=[_dma_sems(ns, N_DEV - 1), _dma_sems(ns, N_DEV - 1), _dma_sems(ns)] if ns else [],
        compiler_params=_params("arbitrary"),
    )(x2, g_pre, w_in_g, *shards)
    return out[:5], list(out[5:])


def _mid(yssm, zs, o, zn, x2, p2, tgt, w_glu, b_glu, w_out, g_post, w_ple_g, g_ple, w_pg):
    seq = x2.shape[0]

    def body(yssm_ref, zs_ref, o_ref, zn_ref, x_ref, p_ref, tgt_ref, wglu_ref, bglu_ref, wout_ref, gpost_ref,
             wple_ref, gple_ref, wpg_ref,
             dyssm_ref, dzs_ref, do_ref, dzn_ref, dh1_ref, loss_ref, dwglu_ref, dbglu_ref, dwout_ref, dwple_ref,
             dwpg_ref, dgpost_ref, dgple_ref):
        @pl.when(pl.program_id(0) == 0)
        def _():
            for ref in (loss_ref, dwglu_ref, dbglu_ref, dwout_ref, dwple_ref, dwpg_ref, dgpost_ref, dgple_ref):
                ref[...] = jnp.zeros_like(ref)

        yv = yssm_ref[...]
        g1 = _gelu(yv)
        g1b = g1.astype(BF16)
        sg = _sigmoid(_mm(g1b, wglu_ref[...]) + bglu_ref[...])
        zs_v = zs_ref[...]
        s_zs = _sigmoid(zs_v)
        g2 = g1 * sg
        zn_v = zn_ref[...]
        s_zn = _sigmoid(zn_v)
        o_v = o_ref[...]
        cat = jnp.concatenate([g2 * (zs_v * s_zs), o_v * (zn_v * s_zn)], axis=1).astype(BF16)
        mix = _mm(cat, wout_ref[...])
        r2 = lax.rsqrt(jnp.mean(mix * mix, axis=-1, keepdims=True) + EPS)
        n2 = mix * r2
        gpost = gpost_ref[...]
        h1 = x_ref[...] + n2 * gpost
        pb = p_ref[...].astype(BF16)
        epre = jnp.concatenate([_mm(pb, wple_ref[j]) for j in range(N_DEV)], axis=1)
        r3 = lax.rsqrt(jnp.mean(epre * epre, axis=-1, keepdims=True) + EPS)
        n3 = epre * r3
        gple = gple_ref[...]
        e = n3 * gple
        h1b = h1.astype(BF16)
        gate = _sigmoid(_mm(h1b, wpg_ref[...]))
        diff = h1 + gate * e - tgt_ref[...]
        loss_ref[...] += jnp.sum(diff * diff, axis=0, keepdims=True)

        dy = diff * (1.0 / D_MODEL)
        dgp = (dy * e * gate * (1.0 - gate)).astype(BF16)
        de = dy * gate
        dh1 = dy + _nt(dgp, wpg_ref[...])
        dh1_ref[...] = dh1
        dwpg_ref[...] += _tn(h1b, dgp)
        dgple_ref[...] += jnp.sum(de * n3, axis=0, keepdims=True)
        dn3 = de * gple
        depre = (r3 * (dn3 - n3 * jnp.mean(dn3 * n3, axis=-1, keepdims=True))).astype(BF16)
        for j in range(N_DEV):
            dwple_ref[j] += _tn(pb, depre[:, j * LANES:(j + 1) * LANES])
        dgpost_ref[...] += jnp.sum(dh1 * n2, axis=0, keepdims=True)
        dn2 = dh1 * gpost
        dmix = (r2 * (dn2 - n2 * jnp.mean(dn2 * n2, axis=-1, keepdims=True))).astype(BF16)
        dcat = _nt(dmix, wout_ref[...])
        dwout_ref[...] += _tn(cat, dmix)

        dys = dcat[:, :D_SSM]
        dyn = dcat[:, D_SSM:]
        dg2 = dys * (zs_v * s_zs)
        dzs_ref[...] = (dys * g2 * (s_zs * (1.0 + zs_v * (1.0 - s_zs)))).astype(BF16)
        dt = dg2 * g2 * (1.0 - sg)
        dtb = dt.astype(BF16)
        dg1 = dg2 * sg + _nt(dtb, wglu_ref[...])
        dwglu_ref[...] += _tn(g1b, dtb)
        dbglu_ref[...] += jnp.sum(dt, axis=0, keepdims=True)
        dyssm_ref[...] = dg1 * _gelu_grad(yv)
        do_ref[...] = (dyn * (zn_v * s_zn)).astype(BF16)
        dzn_ref[...] = (dyn * o_v * (s_zn * (1.0 + zn_v * (1.0 - s_zn)))).astype(BF16)

    tok = lambda w: pl.BlockSpec((TOK_TILE, w), lambda i: (i, 0))
    row = lambda w: pl.BlockSpec((1, w), lambda i: (0, 0))
    vm = _whole_vmem()
    half = jax.ShapeDtypeStruct((seq, D_SSM), F32)
    half_b = jax.ShapeDtypeStruct((seq, D_SSM), BF16)
    gain = jax.ShapeDtypeStruct((1, D_MODEL), F32)
    return pl.pallas_call(
        body, name="mid", grid=(seq // TOK_TILE,),
        in_specs=[tok(D_SSM), tok(D_SSM), tok(D_NA), tok(D_NA), tok(D_MODEL), tok(D_PLE), tok(D_MODEL),
                  vm, row(D_SSM), vm, row(D_MODEL), vm, row(D_MODEL), vm],
        out_specs=[tok(D_SSM), tok(D_SSM), tok(D_NA), tok(D_NA), tok(D_MODEL), vm, vm, vm, vm, vm, vm, vm, vm],
        out_shape=[half, half_b, half_b, half_b, jax.ShapeDtypeStruct((seq, D_MODEL), F32), gain,
                   jax.ShapeDtypeStruct((D_SSM, D_SSM), F32), jax.ShapeDtypeStruct((1, D_SSM), F32),
                   jax.ShapeDtypeStruct((D_MODEL, D_MODEL), F32),
                   jax.ShapeDtypeStruct((N_DEV, D_PLE, LANES), F32),
                   jax.ShapeDtypeStruct((D_MODEL, D_MODEL), F32), gain, gain],
        compiler_params=_params("arbitrary"),
    )(yssm, zs, o, zn, x2, p2, tgt, w_glu, b_glu, w_out, g_post, w_ple_g, g_ple, w_pg)


def _bwd_in(du, dzs, dq, dk, dv, dzn, hn, x2, dh1, g_pre, w_in_g):
    seq = x2.shape[0]

    def body(du_ref, dzs_ref, dq_ref, dk_ref, dv_ref, dzn_ref, hn_ref, x_ref, dh1_ref, g_ref, w_ref,
             dx_ref, dw_ref, dg_ref, dproj_ref):
        @pl.when(pl.program_id(0) == 0)
        def _():
            dw_ref[...] = jnp.zeros_like(dw_ref)
            dg_ref[...] = jnp.zeros_like(dg_ref)

        for k, ref in enumerate((du_ref, dzs_ref, dq_ref, dk_ref, dv_ref, dzn_ref)):
            dproj_ref[:, k * D_SSM:(k + 1) * D_SSM] = ref[...].astype(BF16)
        hn = hn_ref[...]
        dhn = jnp.zeros((TOK_TILE, D_MODEL), F32)
        for j in range(N_DEV):
            dpj = dproj_ref[:, j * SHARD_IN:(j + 1) * SHARD_IN]
            dhn += _nt(dpj, w_ref[j])
            dw_ref[j] += _tn(hn, dpj)
        x = x_ref[...]
        r = lax.rsqrt(jnp.mean(x * x, axis=-1, keepdims=True) + EPS)
        n1 = x * r
        dg_ref[...] += jnp.sum(dhn * n1, axis=0, keepdims=True)
        dn1 = dhn * g_ref[...]
        dx_ref[...] = dh1_ref[...] + r * (dn1 - n1 * jnp.mean(dn1 * n1, axis=-1, keepdims=True))

    tok = lambda w: pl.BlockSpec((TOK_TILE, w), lambda i: (i, 0))
    vm = _whole_vmem()
    return pl.pallas_call(
        body, name="bwd_in", grid=(seq // TOK_TILE,),
        in_specs=[tok(D_SSM)] * 6 + [tok(D_MODEL), tok(D_MODEL), tok(D_MODEL),
                                      pl.BlockSpec((1, D_MODEL), lambda i: (0, 0)), vm],
        out_specs=[tok(D_MODEL), vm, vm],
        out_shape=[jax.ShapeDtypeStruct((seq, D_MODEL), F32),
                   jax.ShapeDtypeStruct((N_DEV, D_MODEL, SHARD_IN), F32),
                   jax.ShapeDtypeStruct((1, D_MODEL), F32)],
        scratch_shapes=[pltpu.VMEM((TOK_TILE, D_IN_PROJ), BF16)],
        compiler_params=_params("arbitrary"),
    )(du, dzs, dq, dk, dv, dzn, hn, x2, dh1, g_pre, w_in_g)


RELAYOUT_CHUNKS = 128
PAIR_W = 2 * SSM_GROUP
PAIRS_PER_BLOCK = LANES // PAIR_W
CHUNK_W = SSM_CHUNK * PAIR_W


def _lane_window(lo, width):
    lane = lax.broadcasted_iota(jnp.int32, (1, LANES), 1)
    return (lane >= lo) & (lane < lo + width)


def _to_pairs(a, nc):
    ncb = min(RELAYOUT_CHUNKS, nc)

    def body(x_ref, out_ref):
        xs = [x_ref[pl.ds(s, ncb, stride=SSM_CHUNK), :] for s in range(SSM_CHUNK)]
        for a_ in range(PAIRS_PER_BLOCK):
            for v in range(SSM_CHUNK // PAIRS_PER_BLOCK):
                acc = None
                for i in range(PAIRS_PER_BLOCK):
                    shift = (PAIR_W * (i - a_)) % LANES
                    piece = xs[PAIRS_PER_BLOCK * v + i]
                    piece = pltpu.roll(piece, shift, axis=1) if shift else piece
                    acc = piece if acc is None else jnp.where(_lane_window(PAIR_W * i, PAIR_W), piece, acc)
                out_ref[a_, :, LANES * v:LANES * (v + 1)] = acc.astype(BF16)

    return pl.pallas_call(
        body, name="to_pairs", grid=(D_SSM // LANES, nc // ncb),
        in_specs=[pl.BlockSpec((ncb * SSM_CHUNK, LANES), lambda cl, cb: (cb, cl))],
        out_specs=pl.BlockSpec((PAIRS_PER_BLOCK, ncb, CHUNK_W), lambda cl, cb: (cl, cb, 0)),
        out_shape=jax.ShapeDtypeStruct((N_PAIRS, nc, CHUNK_W), BF16),
        compiler_params=_params("arbitrary", "arbitrary"),
    )(a)


def _from_pairs(a, nc):
    ncb = min(RELAYOUT_CHUNKS, nc)

    def body(y_ref, out_ref):
        for s in range(SSM_CHUNK):
            v, i = divmod(s, PAIRS_PER_BLOCK)
            acc = None
            for a_ in range(PAIRS_PER_BLOCK):
                shift = (PAIR_W * (a_ - i)) % LANES
                piece = y_ref[a_, :, LANES * v:LANES * (v + 1)]
                piece = pltpu.roll(piece, shift, axis=1) if shift else piece
                acc = piece if acc is None else jnp.where(_lane_window(PAIR_W * a_, PAIR_W), piece, acc)
            out_ref[pl.ds(s, ncb, stride=SSM_CHUNK), :] = acc

    return pl.pallas_call(
        body, name="from_pairs", grid=(D_SSM // LANES, nc // ncb),
        in_specs=[pl.BlockSpec((PAIRS_PER_BLOCK, ncb, CHUNK_W), lambda cl, cb: (cl, cb, 0))],
        out_specs=pl.BlockSpec((ncb * SSM_CHUNK, LANES), lambda cl, cb: (cb, cl)),
        out_shape=jax.ShapeDtypeStruct((nc * SSM_CHUNK, D_SSM), F32),
        compiler_params=_params("arbitrary", "arbitrary"),
    )(a)


SSM_PAIRS = 4
SLOT = 8


def _boundary_scan(nc, pw_ref, buf_ref, conj):
    pad = [jnp.zeros((SLOT - SSM_PAIRS, LANES), F32)]
    lr0, li0, lr1, li1 = (jnp.concatenate([_pw_row(pw_ref, j, SSM_CHUNK, q) for j in range(SSM_PAIRS)] + pad, axis=0)
                          for q in range(4))
    if conj:
        li0, li1 = -li0, -li1

    def step(c, carry):
        hr0, hi0, hr1, hi1 = carry
        up = pl.ds(pl.multiple_of(c * SLOT, SLOT), SLOT)
        dn = pl.ds(pl.multiple_of((nc - 1 - c) * SLOT, SLOT), SLOT)
        ra, rb = (dn, up) if conj else (up, dn)
        s_r0, s_i0 = buf_ref[0, ra, :], buf_ref[1, ra, :]
        s_r1, s_i1 = buf_ref[2, rb, :], buf_ref[3, rb, :]
        buf_ref[0, ra, :] = hr0
        buf_ref[1, ra, :] = hi0
        buf_ref[2, rb, :] = hr1
        buf_ref[3, rb, :] = hi1
        return (lr0 * hr0 - li0 * hi0 + s_r0, lr0 * hi0 + li0 * hr0 + s_i0,
                lr1 * hr1 - li1 * hi1 + s_r1, lr1 * hi1 + li1 * hr1 + s_i1)

    z = jnp.zeros((SLOT, LANES), F32)
    lax.fori_loop(0, nc, step, (z, z, z, z), unroll=8)


def _put_pair(buf_ref, j, nc, val):
    for q in range(4):
        buf_ref[q, pl.ds(j, nc, stride=SLOT), :] = val[:, LANES * q:LANES * (q + 1)]


def _get_pair(buf_ref, j, nc):
    return jnp.concatenate([buf_ref[q, pl.ds(j, nc, stride=SLOT), :] for q in range(4)], axis=1)


def _pw_row(pw_ref, j, k, q):
    return pw_ref[j, k:k + 1, LANES * q:LANES * (q + 1)]


def _mm_f32(a, b, dims):
    return lax.dot_general(a, b, (dims, ((), ())), precision=HIGHEST, preferred_element_type=F32)


_POW_M = (lambda s: SSM_CHUNK - 1 - s, lambda s: s)
_POW_C = (lambda s: s + 1, lambda s: SSM_CHUNK - s)
_POW_K = (lambda s: s, lambda s: SSM_CHUNK - 1 - s)


def _chunk_matrices(j, pw_ref, bb_ref, cc_ref, dd_ref, m_scr, ct_scr, toep_scr, g_scr, kt_scr):
    blk = lambda s: slice(PAIR_W * s, PAIR_W * (s + 1))
    col = lambda q: slice(LANES * q, LANES * (q + 1))
    for d in range(2):
        bbr, bbi = bb_ref[j, 2 * d], bb_ref[j, 2 * d + 1]
        ccr, cci = cc_ref[j, 2 * d], cc_ref[j, 2 * d + 1]
        for s in range(SSM_CHUNK):
            pr, pi = _pw_row(pw_ref, j, _POW_M[d](s), 2 * d), _pw_row(pw_ref, j, _POW_M[d](s), 2 * d + 1)
            m_scr[j, blk(s), col(2 * d)] = (pr * bbr - pi * bbi).astype(m_scr.dtype)
            m_scr[j, blk(s), col(2 * d + 1)] = (pr * bbi + pi * bbr).astype(m_scr.dtype)
            pr, pi = _pw_row(pw_ref, j, _POW_C[d](s), 2 * d), _pw_row(pw_ref, j, _POW_C[d](s), 2 * d + 1)
            ct_scr[j, blk(s), col(2 * d)] = (ccr * pr - cci * pi).astype(ct_scr.dtype)
            ct_scr[j, blk(s), col(2 * d + 1)] = (-(ccr * pi + cci * pr)).astype(ct_scr.dtype)
            pr, pi = _pw_row(pw_ref, j, _POW_K[d](s), 2 * d), _pw_row(pw_ref, j, _POW_K[d](s), 2 * d + 1)
            g_scr[j, d, blk(s), 0:LANES] = ccr * pr - cci * pi
            g_scr[j, d, blk(s), LANES:2 * LANES] = -(ccr * pi + cci * pr)
        kt = _mm_f32(jnp.concatenate([bbr, bbi], axis=1), g_scr[j, d], ((1,), (1,)))
        if d == 0:
            kt = jnp.concatenate([kt[:, 0:LANES] + dd_ref[j], kt[:, LANES:]], axis=1)
        kt_scr[d] = kt
    lane = lax.broadcasted_iota(jnp.int32, (1, CHUNK_W), 1)
    for s in range(SSM_CHUNK):
        lo = PAIR_W * s
        hi = PAIR_W * (s + 1)
        fwd = kt_scr[0] if s == 0 else pltpu.roll(kt_scr[0], lo, axis=1)
        bwd = kt_scr[1] if hi == CHUNK_W else pltpu.roll(kt_scr[1], hi, axis=1)
        row = jnp.where(lane >= lo, fwd, 0.0) + jnp.where(lane < hi, bwd, 0.0)
        toep_scr[j, blk(s), :] = row.astype(toep_scr.dtype)


def _ssm_scratch(nc, mat_dtype, buffers):
    mats = [pltpu.VMEM((SSM_PAIRS, CHUNK_W, CHUNK_W), mat_dtype) for _ in range(3)]
    return mats + [pltpu.VMEM((SSM_PAIRS, 2, CHUNK_W, 2 * LANES), F32), pltpu.VMEM((2, PAIR_W, CHUNK_W), F32)] + [
        pltpu.VMEM((4, nc * SLOT, LANES), F32) for _ in range(buffers)]


def _per_step(*shape):
    return pl.BlockSpec((SSM_PAIRS,) + shape, lambda g: (g,) + (0,) * len(shape))


_TABLE_SPECS = lambda: [_per_step(PW_ROWS, CHUNK_W), _per_step(4, PAIR_W, LANES), _per_step(4, PAIR_W, LANES),
                        _per_step(PAIR_W, LANES)]


def _ssm_fwd(u, pw, bb, cc, dd):
    npair, nc, width = u.shape

    def body(u_ref, pw_ref, bb_ref, cc_ref, dd_ref, y_ref, m_scr, ct_scr, toep_scr, g_scr, kt_scr, h_scr):
        @pl.when(pl.program_id(0) == 0)
        def _():
            h_scr[...] = jnp.zeros_like(h_scr)

        for j in range(SSM_PAIRS):
            _chunk_matrices(j, pw_ref, bb_ref, cc_ref, dd_ref, m_scr, ct_scr, toep_scr, g_scr, kt_scr)
            _put_pair(h_scr, j, nc, _mm(u_ref[j], m_scr[j]))
        _boundary_scan(nc, pw_ref, h_scr, conj=False)
        for j in range(SSM_PAIRS):
            hin = _get_pair(h_scr, j, nc)
            y_ref[j] = _mm(u_ref[j], toep_scr[j]) + _nt(hin.astype(u.dtype), ct_scr[j])

    return pl.pallas_call(
        body, name="ssm_fwd", grid=(npair // SSM_PAIRS,),
        in_specs=[_per_step(nc, width)] + _TABLE_SPECS(),
        out_specs=_per_step(nc, width),
        out_shape=jax.ShapeDtypeStruct((npair, nc, width), F32),
        scratch_shapes=_ssm_scratch(nc, u.dtype, 1),
        compiler_params=_params("arbitrary"),
    )(u, pw, bb, cc, dd)


def _ssm_bwd(u, dy, pw, bb, cc, dd, chip_parts=()):
    npair, nc, width = u.shape
    ns = len(chip_parts)
    steps = npair // SSM_PAIRS

    def body(u_ref, dy_ref, pw_ref, bb_ref, cc_ref, dd_ref, *rest):
        part_refs, rest = rest[:ns], rest[ns:]
        (du_ref, dpw_ref, dbb_ref, dcc_ref, ddd_ref), rest = rest[:5], rest[5:]
        land_refs, rest = rest[:ns], rest[ns:]
        m_scr, ct_scr, toep_scr, g_scr, kt_scr, h_scr, d_scr = rest[:7]
        g = pl.program_id(0)
        if ns:
            _ride(_chip_copies(part_refs, land_refs, *rest[7:]), g == 0, g == steps - 1)

        @pl.when(g == 0)
        def _():
            h_scr[...] = jnp.zeros_like(h_scr)
            d_scr[...] = jnp.zeros_like(d_scr)

        for j in range(SSM_PAIRS):
            _chunk_matrices(j, pw_ref, bb_ref, cc_ref, dd_ref, m_scr, ct_scr, toep_scr, g_scr, kt_scr)
            _put_pair(h_scr, j, nc, _mm(u_ref[j], m_scr[j]))
            _put_pair(d_scr, j, nc, _mm(dy_ref[j], ct_scr[j]))
        _boundary_scan(nc, pw_ref, h_scr, conj=False)
        _boundary_scan(nc, pw_ref, d_scr, conj=True)

        dpw_ref[...] = jnp.zeros_like(dpw_ref)
        blk = lambda s: slice(PAIR_W * s, PAIR_W * (s + 1))
        col = lambda q: slice(LANES * q, LANES * (q + 1))
        lane = lax.broadcasted_iota(jnp.int32, (1, CHUNK_W), 1)
        for j in range(SSM_PAIRS):
            uv = u_ref[j]
            dyb = dy_ref[j]
            hin = _get_pair(h_scr, j, nc)
            ds = _get_pair(d_scr, j, nc)
            dsb = ds.astype(u.dtype)
            du_ref[j] = _nt(dsb, m_scr[j]) + _nt(dyb, toep_scr[j])
            dm = _tn(uv, dsb)
            dct = _tn(dyb, hin.astype(u.dtype))
            dtoep = _tn(uv, dyb)

            def add_pw(k, q, val):
                dpw_ref[j, k:k + 1, col(q)] += jnp.sum(val, axis=0, keepdims=True)

            for d in range(2):
                g_r, g_i = ds[:, col(2 * d)], ds[:, col(2 * d + 1)]
                h_r, h_i = hin[:, col(2 * d)], hin[:, col(2 * d + 1)]
                add_pw(SSM_CHUNK, 2 * d, g_r * h_r + g_i * h_i)
                add_pw(SSM_CHUNK, 2 * d + 1, g_i * h_r - g_r * h_i)

            dkt0 = jnp.zeros((PAIR_W, CHUNK_W), F32)
            dkt1 = jnp.zeros((PAIR_W, CHUNK_W), F32)
            for s in range(SSM_CHUNK):
                lo = PAIR_W * s
                hi = PAIR_W * (s + 1)
                row = dtoep[blk(s), :]
                fwd = jnp.where(lane >= lo, row, 0.0)
                bwd = jnp.where(lane < hi, row, 0.0)
                dkt0 += fwd if s == 0 else pltpu.roll(fwd, CHUNK_W - lo, axis=1)
                dkt1 += bwd if hi == CHUNK_W else pltpu.roll(bwd, CHUNK_W - hi, axis=1)
            ddd_ref[j] = dkt0[:, 0:LANES]

            for d, dkt in enumerate((dkt0, dkt1)):
                bbr, bbi = bb_ref[j, 2 * d], bb_ref[j, 2 * d + 1]
                ccr, cci = cc_ref[j, 2 * d], cc_ref[j, 2 * d + 1]
                dbbcat = _mm_f32(dkt, g_scr[j, d], ((1,), (0,)))
                dg = _mm_f32(dkt, jnp.concatenate([bbr, bbi], axis=1), ((0,), (0,)))
                dbbr, dbbi = dbbcat[:, 0:LANES], dbbcat[:, LANES:]
                dccr = jnp.zeros((PAIR_W, LANES), F32)
                dcci = jnp.zeros((PAIR_W, LANES), F32)
                for s in range(SSM_CHUNK):
                    k = _POW_M[d](s)
                    pr, pi = _pw_row(pw_ref, j, k, 2 * d), _pw_row(pw_ref, j, k, 2 * d + 1)
                    gr, gi = dm[blk(s), col(2 * d)], dm[blk(s), col(2 * d + 1)]
                    dbbr += gr * pr + gi * pi
                    dbbi += gi * pr - gr * pi
                    add_pw(k, 2 * d, gr * bbr + gi * bbi)
                    add_pw(k, 2 * d + 1, gi * bbr - gr * bbi)
                    for k, gr, gi in ((_POW_C[d](s), dct[blk(s), col(2 * d)], dct[blk(s), col(2 * d + 1)]),
                                      (_POW_K[d](s), dg[blk(s), 0:LANES], dg[blk(s), LANES:])):
                        pr, pi = _pw_row(pw_ref, j, k, 2 * d), _pw_row(pw_ref, j, k, 2 * d + 1)
                        dccr += gr * pr - gi * pi
                        dcci += -(gr * pi + gi * pr)
                        add_pw(k, 2 * d, gr * ccr - gi * cci)
                        add_pw(k, 2 * d + 1, -(gr * cci + gi * ccr))
                dbb_ref[j, 2 * d] = dbbr
                dbb_ref[j, 2 * d + 1] = dbbi
                dcc_ref[j, 2 * d] = dccr
                dcc_ref[j, 2 * d + 1] = dcci

    out = pl.pallas_call(
        body, name="ssm_bwd", grid=(steps,),
        in_specs=[_per_step(nc, width), _per_step(nc, width)] + _TABLE_SPECS() + [_ANY()] * ns,
        out_specs=[_per_step(nc, width)] + _TABLE_SPECS() + [_ANY()] * ns,
        out_shape=[jax.ShapeDtypeStruct((npair, nc, width), F32), jax.ShapeDtypeStruct(pw.shape, F32),
                   jax.ShapeDtypeStruct(bb.shape, F32), jax.ShapeDtypeStruct(cc.shape, F32),
                   jax.ShapeDtypeStruct(dd.shape, F32)]
        + [jax.ShapeDtypeStruct(pt.shape, pt.dtype) for pt in chip_parts],
        scratch_shapes=_ssm_scratch(nc, u.dtype, 2) + ([_dma_sems(ns, 3), _dma_sems(ns, 3), _dma_sems(ns)] if ns else []),
        compiler_params=_params("arbitrary"),
    )(u, dy, pw, bb, cc, dd, *chip_parts)
    return out[:5], list(out[5:])


NA_Q = NA_QROWS * GRID_W
NA_K = NA_KROWS * GRID_W
NA_SCALE = NA_HEAD_DIM ** -0.5


def _na_block(b, nb, rows):
    start = jnp.clip(NA_QROWS * b - NA_ROWS // 2, 0, rows - NA_KROWS) * GRID_W
    kind = jnp.where(b == 0, 0, jnp.where(b == nb - 1, 2, 1))
    return pl.multiple_of(start, GRID_W), kind


NA_CHUNK = 16


def _na_pieces(kind, i):
    ri, q0 = divmod(i * NA_CHUNK, GRID_W)
    off = (NA_ROWS - 1, NA_ROWS // 2 - 1, -1)[kind]
    lo = (0, ri, NA_KROWS - NA_ROWS)[kind]
    modes = {(True, True): 'both', (True, False): 'even', (False, True): 'odd', (False, False): None}
    out = []
    for k2 in range(NA_KROWS // 2):
        inside = tuple(lo <= kr < lo + NA_ROWS for kr in (2 * k2, 2 * k2 + 1))
        out.append((2 * k2 - ri + off + 1, slice(q0, q0 + NA_CHUNK), modes[inside]))
    return out


def _na_softmax_pieces(s_ref, tab_ref, hh, kind, i):
    rows = slice(i * NA_CHUNK, (i + 1) * NA_CHUNK)
    lane = lax.broadcasted_iota(jnp.int32, (1, LANES), 1)
    xs = []
    for k2, (t, q, mode) in enumerate(_na_pieces(kind, i)):
        if mode is None:
            xs.append(None)
            continue
        bias = tab_ref[hh, t, q, :]
        if mode == 'even':
            bias = jnp.where(lane < GRID_W, bias, NEG)
        elif mode == 'odd':
            bias = jnp.where(lane >= GRID_W, bias, NEG)
        xs.append(s_ref[hh, rows, k2 * LANES:(k2 + 1) * LANES] + bias)
    live = [x for x in xs if x is not None]
    m = jnp.max(functools.reduce(jnp.maximum, live), axis=-1, keepdims=True)
    es = [None if x is None else jnp.exp(x - m) for x in xs]
    total = jnp.sum(functools.reduce(jnp.add, [e for e in es if e is not None]), axis=-1, keepdims=True)
    inv = 1.0 / total
    return [None if e is None else e * inv for e in es]


def _na_heads():
    lane = lax.broadcasted_iota(jnp.int32, (1, LANES), 1)
    return [lane < NA_HEAD_DIM, lane >= NA_HEAD_DIM]


def _na_fwd(qkv, bias):
    seq = qkv.shape[0]
    rows = seq // GRID_W
    nb = rows // NA_QROWS

    def body(q_ref, k_ref, v_ref, bias_ref, o_ref, p_ref, s_scr):
        start, kind = _na_block(pl.program_id(1), nb, rows)

        def block(static_kind):
            q2 = q_ref[...] * NA_SCALE
            kw = k_ref[pl.ds(start, NA_K), :]
            vw = v_ref[pl.ds(start, NA_K), :]
            heads = _na_heads()
            for hh in range(2):
                s_scr[hh] = _nt(jnp.where(heads[hh], q2, jnp.zeros_like(q2)), kw)
            for hh in range(2):
                for i in range(NA_Q // NA_CHUNK):
                    r = slice(i * NA_CHUNK, (i + 1) * NA_CHUNK)
                    for k2, p in enumerate(_na_softmax_pieces(s_scr, bias_ref, hh, static_kind, i)):
                        p = jnp.zeros((NA_CHUNK, LANES), F32) if p is None else p
                        p_ref[hh, r, k2 * LANES:(k2 + 1) * LANES] = p.astype(p_ref.dtype)
            o_ref[...] = jnp.where(heads[0], _mm(p_ref[0], vw), _mm(p_ref[1], vw))

        for static_kind in range(3):
            pl.when(kind == static_kind)(functools.partial(block, static_kind))

    return pl.pallas_call(
        body, name="na_fwd", grid=(NA_HEADS // 2, nb),
        in_specs=[pl.BlockSpec((NA_Q, LANES), lambda hp, b: (b, hp)),
                  pl.BlockSpec((seq, LANES), lambda hp, b: (0, 4 + hp)),
                  pl.BlockSpec((seq, LANES), lambda hp, b: (0, 8 + hp)),
                  pl.BlockSpec((2, NA_TAB, GRID_W, LANES), lambda hp, b: (hp, 0, 0, 0))],
        out_specs=[pl.BlockSpec((NA_Q, LANES), lambda hp, b: (b, hp)),
                   pl.BlockSpec((2, NA_Q, NA_K), lambda hp, b: (hp, b, 0))],
        out_shape=[jax.ShapeDtypeStruct((seq, D_NA), F32), jax.ShapeDtypeStruct((NA_HEADS, seq, NA_K), qkv.dtype)],
        scratch_shapes=[pltpu.VMEM((2, NA_Q, NA_K), F32)],
        compiler_params=_params("arbitrary", "arbitrary"),
    )(qkv, qkv, qkv, bias)


def _na_bwd(qkv, do, probs, parts=()):
    seq = qkv.shape[0]
    rows = seq // GRID_W
    nb = rows // NA_QROWS
    ns = len(parts)

    def body(q_ref, k_ref, v_ref, do_ref, p_ref, *rest):
        part_refs, rest = rest[:ns], rest[ns:]
        (dq_ref, dk_ref, dv_ref, dbias_ref), rest = rest[:4], rest[4:]
        land_refs, rest = rest[:ns], rest[ns:]
        dp_scr, ds_scr = rest[:2]
        b = pl.program_id(1)
        start, kind = _na_block(b, nb, rows)
        if ns:
            hp = pl.program_id(0)
            _ride(_pair_copies(part_refs, land_refs, *rest[2:]), (hp == 0) & (b == 0),
                  (hp == NA_HEADS // 2 - 1) & (b == nb - 1))

        @pl.when(b == 0)
        def _():
            dk_ref[...] = jnp.zeros_like(dk_ref)
            dv_ref[...] = jnp.zeros_like(dv_ref)
            dbias_ref[...] = jnp.zeros_like(dbias_ref)

        def block(static_kind):
            q2 = q_ref[...] * NA_SCALE
            kw = k_ref[pl.ds(start, NA_K), :]
            vw = v_ref[pl.ds(start, NA_K), :]
            do2 = do_ref[...].astype(q2.dtype)
            heads = _na_heads()
            col = lambda k2: slice(k2 * LANES, (k2 + 1) * LANES)
            zero = jnp.zeros((NA_CHUNK, LANES), ds_scr.dtype)
            for hh in range(2):
                dp_scr[hh] = _nt(jnp.where(heads[hh], do2, jnp.zeros_like(do2)), vw)
            for hh in range(2):
                for i in range(NA_Q // NA_CHUNK):
                    r = slice(i * NA_CHUNK, (i + 1) * NA_CHUNK)
                    pieces = _na_pieces(static_kind, i)
                    ps = [None if mode is None else p_ref[hh, r, col(k2)].astype(F32)
                          for k2, (_, _, mode) in enumerate(pieces)]
                    dps = [None if p is None else dp_scr[hh, r, col(k2)] for k2, p in enumerate(ps)]
                    pdp = functools.reduce(jnp.add, [p * dp for p, dp in zip(ps, dps) if p is not None])
                    rowsum = jnp.sum(pdp, axis=-1, keepdims=True)
                    for k2, (t, q, _) in enumerate(pieces):
                        if ps[k2] is None:
                            ds_scr[hh, r, col(k2)] = zero
                            continue
                        ds = ps[k2] * (dps[k2] - rowsum)
                        dbias_ref[hh, t, q, :] += ds
                        ds_scr[hh, r, col(k2)] = ds.astype(ds_scr.dtype)
            dq_ref[...] = jnp.where(heads[0], _mm(ds_scr[0], kw), _mm(ds_scr[1], kw)) * NA_SCALE
            dk_ref[pl.ds(start, NA_K), :] += jnp.where(heads[0], _tn(ds_scr[0], q2), _tn(ds_scr[1], q2))
            dv_ref[pl.ds(start, NA_K), :] += jnp.where(heads[0], _tn(p_ref[0], do2), _tn(p_ref[1], do2))

        for static_kind in range(3):
            pl.when(kind == static_kind)(functools.partial(block, static_kind))

    out = pl.pallas_call(
        body, name="na_bwd", grid=(NA_HEADS // 2, nb),
        in_specs=[pl.BlockSpec((NA_Q, LANES), lambda hp, b: (b, hp)),
                  pl.BlockSpec((seq, LANES), lambda hp, b: (0, 4 + hp)),
                  pl.BlockSpec((seq, LANES), lambda hp, b: (0, 8 + hp)),
                  pl.BlockSpec((NA_Q, LANES), lambda hp, b: (b, hp)),
                  pl.BlockSpec((2, NA_Q, NA_K), lambda hp, b: (hp, b, 0))] + [_ANY()] * ns,
        out_specs=[pl.BlockSpec((NA_Q, LANES), lambda hp, b: (b, hp)),
                   pl.BlockSpec((seq, LANES), lambda hp, b: (0, hp)),
                   pl.BlockSpec((seq, LANES), lambda hp, b: (0, hp)),
                   pl.BlockSpec((2, NA_TAB, GRID_W, LANES), lambda hp, b: (hp, 0, 0, 0))] + [_ANY()] * ns,
        out_shape=[jax.ShapeDtypeStruct((seq, D_NA), F32), jax.ShapeDtypeStruct((seq, D_NA), F32),
                   jax.ShapeDtypeStruct((seq, D_NA), F32),
                   jax.ShapeDtypeStruct((NA_HEADS, NA_TAB, GRID_W, LANES), F32)]
        + [jax.ShapeDtypeStruct((4,) + pt.shape[1:], pt.dtype) for pt in parts],
        scratch_shapes=[pltpu.VMEM((2, NA_Q, NA_K), F32), pltpu.VMEM((2, NA_Q, NA_K), qkv.dtype)]
        + ([_dma_sems(ns, 4), _dma_sems(ns, 4)] if ns else []),
        compiler_params=_params("arbitrary", "arbitrary"),
    )(qkv, qkv, qkv, do, probs, *parts)
    return out[:4], list(out[4:])


def _local_step(x2, p2, tgt, g_pre, g_post, w_in_g, ssm, w_glu, b_glu, rpb, w_out, w_ple, g_ple, w_pg,
                distributed=False):
    seq = x2.shape[0]
    nc = seq // SSM_CHUNK

    (pw, bb, cc, dd), ssm_vjp = jax.vjp(_ssm_tables, *ssm)
    bias, bias_vjp = jax.vjp(_na_table, rpb)

    riders = (w_glu, w_out, w_ple, w_pg) if distributed else ()
    (u, zs, qkv, zn, hn), gathered = _fwd_in(x2, g_pre, w_in_g, riders)
    if distributed:
        w_glu, w_out, w_ple, w_pg = gathered
        w_glu = w_glu.reshape(D_SSM, D_SSM)
        w_out = w_out.reshape(D_MODEL, D_MODEL)
        w_pg = w_pg.reshape(D_MODEL, D_MODEL)
    u_p = _to_pairs(u, nc)
    y_p = _ssm_fwd(u_p, pw, bb, cc, dd)
    yssm = _from_pairs(y_p, nc)
    o, probs = _na_fwd(qkv, bias)
    dyssm, dzs, do, dzn, dh1, sq, d_wglu, d_bglu, d_wout, d_wple, d_wpg, d_gpost, d_gple = _mid(
        yssm, zs, o, zn, x2, p2, tgt, w_glu, b_glu, w_out, g_post, w_ple, g_ple, w_pg)
    mid_grads = [d_wglu.reshape(N_DEV, D_SSM // N_DEV, D_SSM), d_wout.reshape(N_DEV, D_MODEL // N_DEV, D_MODEL),
                 d_wple, d_wpg.reshape(N_DEV, D_MODEL // N_DEV, D_MODEL)]
    (dq, dk, dv, dbias), lands = _na_bwd(qkv, do, probs, mid_grads if distributed else ())
    (d_rpb,) = bias_vjp(dbias)
    chip_parts = _pair_sum(mid_grads, lands)[0] if distributed else ()
    (du_p, dpw, dbb, dcc, ddd), chip_lands = _ssm_bwd(u_p, _to_pairs(dyssm, nc), pw, bb, cc, dd, chip_parts)
    d_ssm = ssm_vjp((dpw, dbb, dcc, ddd))
    du = _from_pairs(du_p, nc)
    dx, d_win, d_gpre = _bwd_in(du, dzs, dq, dk, dv, dzn, hn, x2, dh1, g_pre, w_in_g)
    if distributed:
        d_wglu, d_wout, d_wple, d_wpg = chip_lands
    return sq, dx, dict(norm_pre=d_gpre, norm_post=d_gpost, w_in=d_win, ssm=d_ssm, w_glu=d_wglu, b_glu=d_bglu,
                        na_rpb=d_rpb, w_out=d_wout, w_ple=d_wple, ple_norm=d_gple, w_ple_gate=d_wpg)


def _all_gather(shard):
    m_per, n = shard.shape

    def body(x_ref, out_ref, send_sems, recv_sems, local_sem):
        x, y, c = _place()
        me, sibling = (x, y, c), (x, y, 1 - c)
        chips = [(1 - x, y), (x, 1 - y), (1 - x, 1 - y)]

        def rows(px, py, pc):
            return out_ref.at[pl.ds((4 * px + 2 * py + pc) * m_per, m_per), :]

        def copy(k, block, to, src=None):
            return pltpu.make_async_remote_copy(
                src_ref=rows(*block) if src is None else src, dst_ref=rows(*block),
                send_sem=send_sems.at[k], recv_sem=recv_sems.at[k], device_id=to, device_id_type=MESH)

        mine = pltpu.make_async_copy(x_ref, rows(*me), local_sem)
        mine.start()
        first = [copy(0, me, sibling, src=x_ref)]
        first += [copy(1 + j, me, (*chip, c), src=x_ref) for j, chip in enumerate(chips)]
        for cp in first:
            cp.start()
        passed = [copy(4 + j, (*chip, c), sibling) for j, chip in enumerate(chips)]
        for j, chip in enumerate(chips):
            copy(1 + j, (*chip, c), me).wait_recv()
            passed[j].start()
        copy(0, sibling, me).wait_recv()
        for j, chip in enumerate(chips):
            copy(4 + j, (*chip, 1 - c), me).wait_recv()
        for cp in first + passed:
            cp.wait_send()
        mine.wait()

    return pl.pallas_call(
        body, name="all_gather",
        out_shape=jax.ShapeDtypeStruct((N_DEV * m_per, n), shard.dtype),
        in_specs=[_whole_vmem()], out_specs=_whole_vmem(),
        scratch_shapes=[pltpu.SemaphoreType.DMA((7,)), pltpu.SemaphoreType.DMA((7,)), pltpu.SemaphoreType.DMA],
        compiler_params=pltpu.CompilerParams(vmem_limit_bytes=VMEM_LIMIT),
    )(shard)


def _exchange_first(pair_parts, all_parts):
    n1, n2 = len(pair_parts), len(all_parts)

    def body(*refs):
        ins, refs = refs[:n1 + n2], refs[n1 + n2:]
        outs, sems = refs[:n1 + n2], refs[n1 + n2:]
        copies = (_pair_copies(ins[:n1], outs[:n1], *sems[:2])
                  + _alltoall_copies(ins[n1:], outs[n1:], *sems[2:]))
        _start_all(copies)
        _wait_all(copies)

    out = pl.pallas_call(
        body, name="exchange_first",
        out_shape=[jax.ShapeDtypeStruct((4,) + pt.shape[1:], pt.dtype) for pt in pair_parts]
        + [jax.ShapeDtypeStruct(pt.shape, pt.dtype) for pt in all_parts],
        in_specs=[_ANY()] * (n1 + n2), out_specs=[_ANY()] * (n1 + n2),
        scratch_shapes=[_dma_sems(n1, 4), _dma_sems(n1, 4),
                        _dma_sems(n2, N_DEV - 1), _dma_sems(n2, N_DEV - 1), _dma_sems(n2)],
    )(*pair_parts, *all_parts)
    return list(out[:n1]), list(out[n1:])


def _pair_sum(parts, lands, all_lands=()):
    ns, na = len(parts), len(all_lands)

    def body(*refs):
        c = lax.axis_index("c")
        ins, outs = refs[:2 * ns + na], refs[2 * ns + na:]
        for part_ref, land_ref, out_ref in zip(ins[:ns], ins[ns:2 * ns], outs[:ns]):
            for q in range(4):
                out_ref[q] = (part_ref[q, c] + land_ref[q]).astype(BF16)
        for land_ref, out_ref in zip(ins[2 * ns:], outs[ns:]):
            acc = land_ref[0]
            for j in range(1, N_DEV):
                acc = acc + land_ref[j]
            out_ref[...] = acc

    out = pl.pallas_call(
        body, name="pair_sum",
        in_specs=[_whole_vmem()] * (2 * ns + na), out_specs=[_whole_vmem()] * (ns + na),
        out_shape=[jax.ShapeDtypeStruct(ld.shape, BF16) for ld in lands]
        + [jax.ShapeDtypeStruct(ld.shape[1:], ld.dtype) for ld in all_lands],
        compiler_params=pltpu.CompilerParams(vmem_limit_bytes=VMEM_LIMIT),
    )(*[pt.reshape((4, 2) + pt.shape[1:]) for pt in parts], *lands, *all_lands)
    return list(out[:ns]), list(out[ns:])


def _exchange_second(chip_parts, shards):
    n1, n2 = len(chip_parts), len(shards)

    def body(*refs):
        ins, refs = refs[:n1 + n2], refs[n1 + n2:]
        outs, sems = refs[:n1 + n2], refs[n1 + n2:]
        copies = (_chip_copies(ins[:n1], outs[:n1], *sems[:3])
                  + _gather_copies(ins[n1:], outs[n1:], *sems[3:]))
        _start_all(copies)
        _wait_all(copies)

    out = pl.pallas_call(
        body, name="exchange_second",
        out_shape=[jax.ShapeDtypeStruct(pt.shape, pt.dtype) for pt in chip_parts]
        + [jax.ShapeDtypeStruct((N_DEV,) + sh.shape, sh.dtype) for sh in shards],
        in_specs=[_ANY()] * (n1 + n2), out_specs=[_ANY()] * (n1 + n2),
        scratch_shapes=[_dma_sems(n1, 3), _dma_sems(n1, 3), _dma_sems(n1),
                        _dma_sems(n2, N_DEV - 1), _dma_sems(n2, N_DEV - 1), _dma_sems(n2)],
    )(*chip_parts, *shards)
    return list(out[:n1]), list(out[n1:])


ADAM_ROWS = 256


def _adamw(ws, gs, ms, vs):
    n = len(ws)

    def body(*refs):
        ins, outs = refs[:4 * n], refs[4 * n:]
        for t in range(n):
            w_ref, g_ref, m_ref, v_ref = ins[t], ins[n + t], ins[2 * n + t], ins[3 * n + t]
            go_ref, d_ref, nm_ref, nv_ref = outs[4 * t:4 * t + 4]
            rows = w_ref.shape[0]
            for lo in range(0, rows, ADAM_ROWS):
                r = slice(lo, min(lo + ADAM_ROWS, rows))
                if len(g_ref.shape) == 3:
                    g = g_ref[0, r, :].astype(F32)
                    for j in range(1, g_ref.shape[0]):
                        g = g + g_ref[j, r, :].astype(F32)
                else:
                    g = g_ref[r, :]
                go_ref[r, :] = g
                nm = ADAM_B1 * m_ref[r, :] + (1.0 - ADAM_B1) * g
                nv = ADAM_B2 * v_ref[r, :] + (1.0 - ADAM_B2) * (g * g)
                m_hat = nm / (1.0 - ADAM_B1 ** ADAM_STEP)
                v_hat = nv / (1.0 - ADAM_B2 ** ADAM_STEP)
                d_ref[r, :] = -ADAM_LR * (m_hat / (jnp.sqrt(v_hat) + ADAM_EPS) + ADAM_WD * w_ref[r, :])
                nm_ref[r, :] = nm
                nv_ref[r, :] = nv

    out = pl.pallas_call(
        body, name="adamw",
        in_specs=[_whole_vmem()] * (4 * n), out_specs=[_whole_vmem()] * (4 * n),
        out_shape=[jax.ShapeDtypeStruct(w.shape, F32) for w in ws for _ in range(4)],
        compiler_params=pltpu.CompilerParams(vmem_limit_bytes=VMEM_LIMIT),
    )(*ws, *gs, *ms, *vs)
    return [tuple(out[4 * t:4 * t + 4]) for t in range(n)]


_SLAB = 8 * LANES


def _flat_rows(a):
    flat = a.reshape(-1)
    pad = (-flat.shape[0]) % _SLAB
    if pad:
        flat = jnp.concatenate([flat, jnp.zeros((pad,), flat.dtype)])
    return flat.reshape(-1, LANES)


def _pack(arrays):
    slabs = [_flat_rows(a) for a in arrays]
    return jnp.concatenate(slabs, axis=0), [s.shape[0] for s in slabs]


def _unpack(packed, like, rows):
    out, at = [], 0
    for a, r in zip(like, rows):
        out.append(packed[at:at + r].reshape(-1)[:a.size].reshape(a.shape))
        at += r
    return out


SMALL = ('norm_pre', 'norm_post', 'ssm_a_re', 'ssm_a_im', 'ssm_log_dt', 'ssm_b_re', 'ssm_b_im', 'ssm_c_re',
         'ssm_c_im', 'ssm_d', 'b_glu', 'na_rpb', 'ple_norm')
BIG = ('w_in', 'w_glu', 'w_out', 'w_ple', 'w_ple_gate')
ORDER = ('norm_pre', 'norm_post', 'w_in', 'ssm_a_re', 'ssm_a_im', 'ssm_log_dt', 'ssm_b_re', 'ssm_b_im', 'ssm_c_re',
         'ssm_c_im', 'ssm_d', 'w_glu', 'b_glu', 'na_rpb', 'w_out', 'w_ple', 'ple_norm', 'w_ple_gate')


def kernel(x, p, norm_pre, norm_post, w_in, ssm_a_re, ssm_a_im, ssm_log_dt, ssm_b_re, ssm_b_im, ssm_c_re, ssm_c_im, ssm_d, w_glu, b_glu, na_rpb, w_out, w_ple, ple_norm, w_ple_gate, loss_target, m_norm_pre, m_norm_post, m_w_in, m_ssm_a_re, m_ssm_a_im, m_ssm_log_dt, m_ssm_b_re, m_ssm_b_im, m_ssm_c_re, m_ssm_c_im, m_ssm_d, m_w_glu, m_b_glu, m_na_rpb, m_w_out, m_w_ple, m_ple_norm, m_w_ple_gate, v_norm_pre, v_norm_post, v_w_in, v_ssm_a_re, v_ssm_a_im, v_ssm_log_dt, v_ssm_b_re, v_ssm_b_im, v_ssm_c_re, v_ssm_c_im, v_ssm_d, v_w_glu, v_b_glu, v_na_rpb, v_w_out, v_w_ple, v_ple_norm, v_w_ple_gate):
    args = dict(locals())
    weights = {n: args[n] for n in ORDER}
    mom_m = {n: args["m_" + n] for n in ORDER}
    mom_v = {n: args["v_" + n] for n in ORDER}

    w_in_g = _all_gather(w_in[0].astype(BF16)).reshape(N_DEV, D_MODEL, SHARD_IN)
    blocks = [weights[n][0].astype(BF16) for n in ('w_glu', 'w_out', 'w_ple', 'w_ple_gate')]

    ssm = tuple(weights[n][0] for n in ('ssm_a_re', 'ssm_a_im', 'ssm_log_dt', 'ssm_b_re', 'ssm_b_im',
                                        'ssm_c_re', 'ssm_c_im', 'ssm_d'))
    sq, dx, grads = _local_step(x[0], p[0, 0], loss_target[0], norm_pre, norm_post, w_in_g, ssm, blocks[0], b_glu,
                                na_rpb[0], blocks[1], blocks[2], ple_norm, blocks[3], distributed=True)
    loss_local = (0.5 / D_MODEL * jnp.sum(sq)).reshape(1)

    local = dict(norm_pre=grads['norm_pre'], norm_post=grads['norm_post'], b_glu=grads['b_glu'],
                 na_rpb=grads['na_rpb'][None], ple_norm=grads['ple_norm'])
    for n, g in zip(('ssm_a_re', 'ssm_a_im', 'ssm_log_dt', 'ssm_b_re', 'ssm_b_im', 'ssm_c_re', 'ssm_c_im', 'ssm_d'),
                    grads['ssm']):
        local[n] = g[None]

    small_flat, _ = _pack([local[n] for n in SMALL] + [loss_local])
    pad = (-small_flat.shape[0]) % (8 * N_DEV)
    if pad:
        small_flat = jnp.concatenate([small_flat, jnp.zeros((pad, LANES), F32)], axis=0)
    small_per = small_flat.shape[0] // N_DEV
    d_win = [grads['w_in']]
    lands, small_lands = _exchange_first(d_win, [small_flat.reshape(N_DEV, small_per, LANES)])
    chip_parts, small_sums = _pair_sum(d_win, lands, small_lands)
    chip_lands, small_full = _exchange_second(chip_parts, small_sums)
    grads['w_in'] = chip_lands[0]
    small_all = small_full[0].reshape(N_DEV * small_per, LANES)

    like = [weights[n] for n in SMALL]
    w_s, rows_s = _pack(like)
    m_s, _ = _pack([mom_m[n] for n in SMALL])
    v_s, _ = _pack([mom_v[n] for n in SMALL])
    loss = small_all[w_s.shape[0], 0]
    results = _adamw([weights[n][0] for n in BIG] + [w_s], [grads[n] for n in BIG] + [small_all[:w_s.shape[0]]],
                     [mom_m[n][0] for n in BIG] + [m_s], [mom_v[n][0] for n in BIG] + [v_s])
    outs = {n: tuple(a[None] for a in res) for n, res in zip(BIG, results)}
    unpacked = [_unpack(a, like, rows_s) for a in results[-1]]
    for i, n in enumerate(SMALL):
        outs[n] = tuple(u[i] for u in unpacked)

    return (loss, dx[None], *[outs[n][0] for n in ORDER], *[outs[n][1] for n in ORDER],
            *[outs[n][2] for n in ORDER], *[outs[n][3] for n in ORDER])
```

```python
import functools

import jax
import jax.numpy as jnp
from jax import lax
from jax.experimental import pallas as pl
from jax.experimental.pallas import tpu as pltpu

F32 = jnp.float32
BF16 = jnp.bfloat16
HIGHEST = lax.Precision.HIGHEST

D_MODEL = 1024
D_PLE = 256
GRID_W = 64
D_SSM = 512
SSM_GROUP = 16
N_PAIRS = 16
SSM_STATE = 64
D_NA = 512
NA_HEADS = 8
NA_HEAD_DIM = 64
NA_ROWS = 8
NA_COLS = 16
D_IN_PROJ = 3072
EPS = 1e-6
N_DEV = 8
SHARD_IN = D_IN_PROJ // N_DEV
LANES = 128
SSM_CHUNK = 16
PW_ROWS = 24
TOK_TILE = 256
NA_QROWS = 4
NA_KROWS = 12
NEG = -1e30
VMEM_LIMIT = 56 * 1024 * 1024

ADAM_LR = 0.001
ADAM_B1 = 0.9
ADAM_B2 = 0.999
ADAM_EPS = 1e-08
ADAM_WD = 0.01
ADAM_STEP = 10

MESH = pl.DeviceIdType.MESH


def _params(*sem):
    return pltpu.CompilerParams(dimension_semantics=sem or None, vmem_limit_bytes=VMEM_LIMIT)


def _whole_vmem():
    return pl.BlockSpec(memory_space=pltpu.VMEM)


def _nt(a, b):
    return lax.dot_general(a, b, (((1,), (1,)), ((), ())), preferred_element_type=F32)


def _tn(a, b):
    return lax.dot_general(a, b, (((0,), (0,)), ((), ())), preferred_element_type=F32)


def _mm(a, b):
    return jnp.dot(a, b, preferred_element_type=F32)


def _sigmoid(x):
    return 1.0 / (1.0 + jnp.exp(-x))


_GELU_C = 0.7978845608028654


def _gelu(x):
    return 0.5 * x * (1.0 + jnp.tanh(_GELU_C * (x + 0.044715 * x * x * x)))


def _gelu_grad(x):
    th = jnp.tanh(_GELU_C * (x + 0.044715 * x * x * x))
    return 0.5 * (1.0 + th) + 0.5 * x * (1.0 - th * th) * _GELU_C * (1.0 + 3.0 * 0.044715 * x * x)


def _ssm_tables(a_re, a_im, log_dt, b_re, b_im, c_re, c_im, d):
    T, P, H = SSM_CHUNK, SSM_STATE, SSM_GROUP
    dt = jnp.exp(log_dt)[..., None]
    xr = dt * a_re
    xi = dt * a_im
    mag = jnp.exp(xr)
    lr = mag * jnp.cos(xi)
    li = mag * jnp.sin(xi)
    den = a_re * a_re + a_im * a_im
    cr = ((lr - 1.0) * a_re + li * a_im) / den
    ci = (li * a_re - (lr - 1.0) * a_im) / den
    bbr = cr[..., None] * b_re - ci[..., None] * b_im
    bbi = cr[..., None] * b_im + ci[..., None] * b_re
    kk = jnp.arange(T + 1, dtype=F32)[:, None, None, None]
    pm = jnp.exp(kk * xr)
    pw = jnp.stack([pm * jnp.cos(kk * xi), pm * jnp.sin(kk * xi)], axis=2)
    pw = pw.reshape(T + 1, 2, 2, N_PAIRS, 2 * P).transpose(3, 0, 1, 2, 4).reshape(N_PAIRS, T + 1, 8 * P)
    pw = jnp.concatenate([pw, jnp.zeros((N_PAIRS, PW_ROWS - (T + 1), 8 * P), F32)], axis=1)
    eye2 = jnp.eye(2, dtype=F32)

    def expand(t):
        t = t.transpose(2, 0, 1, 3, 4, 5)
        t = t[:, :, :, :, :, None, :] * eye2[None, None, None, :, None, :, None]
        return t.reshape(N_PAIRS, 4, 2 * H, 2 * P)

    bb = expand(jnp.stack([bbr, bbi], axis=1).reshape(2, 2, N_PAIRS, 2, P, H).transpose(0, 1, 2, 3, 5, 4))
    cc = expand(jnp.stack([c_re, c_im], axis=1).reshape(2, 2, N_PAIRS, 2, H, P))
    dd = d.reshape(N_PAIRS, 2 * H)[:, :, None] * jnp.eye(2 * H, dtype=F32)[None]
    dd = jnp.concatenate([dd, jnp.zeros((N_PAIRS, 2 * H, LANES - 2 * H), F32)], axis=2)
    return pw, bb, cc, dd


NA_TAB = 2 * NA_ROWS


def _na_table(rpb):
    qc = jnp.arange(GRID_W)[:, None, None]
    kc = (jnp.arange(2 * GRID_W) % GRID_W)[None, :, None]
    dc = jnp.arange(2 * NA_COLS - 1)[None, None, :]
    cstart = jnp.clip(qc - NA_COLS // 2, 0, GRID_W - NA_COLS)
    csel = ((kc >= cstart) & (kc < cstart + NA_COLS) & (kc - qc + NA_COLS - 1 == dc)).astype(F32)
    col_ok = jnp.sum(csel, axis=-1) > 0.5
    part = jnp.einsum('hrc,qmc->hrqm', rpb, csel, precision=HIGHEST)
    zero = jnp.zeros_like(part[:, :1])
    odd = jnp.arange(2 * GRID_W) >= GRID_W
    tab = jnp.where(odd, jnp.concatenate([part, zero], axis=1), jnp.concatenate([zero, part], axis=1))
    return jnp.where(col_ok, tab, NEG)


def _place():
    return lax.axis_index("x"), lax.axis_index("y"), lax.axis_index("c")


def _remote(src, dst, send_sem, recv_sem, device):
    return pltpu.make_async_remote_copy(src_ref=src, dst_ref=dst, send_sem=send_sem, recv_sem=recv_sem,
                                        device_id=device, device_id_type=MESH)


def _start_all(copies):
    for cp in copies:
        cp.start()


def _wait_all(copies):
    for cp in copies:
        cp.wait()


def _gather_copies(shard_refs, full_refs, send_sems, recv_sems, local_sems):
    x, y, c = _place()
    me = 4 * x + 2 * y + c
    out = []
    for t, (shard, full) in enumerate(zip(shard_refs, full_refs)):
        out.append(pltpu.make_async_copy(shard, full.at[me], local_sems.at[t]))
        for k in range(1, N_DEV):
            peer = (x ^ ((k >> 2) & 1), y ^ ((k >> 1) & 1), c ^ (k & 1))
            out.append(_remote(shard, full.at[me], send_sems.at[t, k - 1], recv_sems.at[t, k - 1], peer))
    return out


def _pair_copies(part_refs, land_refs, send_sems, recv_sems):
    x, y, c = _place()
    out = []
    for t, (part, land) in enumerate(zip(part_refs, land_refs)):
        for q in range(4):
            out.append(_remote(part.at[2 * q + (1 - c)], land.at[q], send_sems.at[t, q], recv_sems.at[t, q],
                               (x, y, 1 - c)))
    return out


def _chip_copies(part_refs, land_refs, send_sems, recv_sems, local_sems):
    x, y, c = _place()
    mine = 2 * x + y
    out = []
    for t, (part, land) in enumerate(zip(part_refs, land_refs)):
        out.append(pltpu.make_async_copy(part.at[mine], land.at[mine], local_sems.at[t]))
        for k in range(1, 4):
            px, py = x ^ (k >> 1), y ^ (k & 1)
            out.append(_remote(part.at[2 * px + py], land.at[mine], send_sems.at[t, k - 1], recv_sems.at[t, k - 1],
                               (px, py, c)))
    return out


def _alltoall_copies(part_refs, land_refs, send_sems, recv_sems, local_sems):
    x, y, c = _place()
    me = 4 * x + 2 * y + c
    out = []
    for t, (part, land) in enumerate(zip(part_refs, land_refs)):
        out.append(pltpu.make_async_copy(part.at[me], land.at[me], local_sems.at[t]))
        for k in range(1, N_DEV):
            px, py, pc = x ^ ((k >> 2) & 1), y ^ ((k >> 1) & 1), c ^ (k & 1)
            out.append(_remote(part.at[4 * px + 2 * py + pc], land.at[me], send_sems.at[t, k - 1],
                               recv_sems.at[t, k - 1], (px, py, pc)))
    return out


def _ride(copies, first, last):
    pl.when(first)(functools.partial(_start_all, copies))
    pl.when(last)(functools.partial(_wait_all, copies))


_ANY = lambda: pl.BlockSpec(memory_space=pl.ANY)


def _dma_sems(*shape):
    return pltpu.SemaphoreType.DMA(shape)


PAIR_W = 2 * SSM_GROUP
PAIRS_PER_BLOCK = LANES // PAIR_W
CHUNK_W = SSM_CHUNK * PAIR_W
TILE_CHUNKS = TOK_TILE // SSM_CHUNK


def _lane_window(lo, width):
    lane = lax.broadcasted_iota(jnp.int32, (1, LANES), 1)
    return (lane >= lo) & (lane < lo + width)


def _tile_to_pairs(src_ref, out_ref):
    for cl in range(D_SSM // LANES):
        xs = [src_ref[cl, pl.ds(s, TILE_CHUNKS, stride=SSM_CHUNK), :] for s in range(SSM_CHUNK)]
        for a_ in range(PAIRS_PER_BLOCK):
            for v in range(SSM_CHUNK // PAIRS_PER_BLOCK):
                acc = None
                for i in range(PAIRS_PER_BLOCK):
                    shift = (PAIR_W * (i - a_)) % LANES
                    piece = xs[PAIRS_PER_BLOCK * v + i]
                    piece = pltpu.roll(piece, shift, axis=1) if shift else piece
                    acc = piece if acc is None else jnp.where(_lane_window(PAIR_W * i, PAIR_W), piece, acc)
                out_ref[PAIRS_PER_BLOCK * cl + a_, :, LANES * v:LANES * (v + 1)] = acc.astype(out_ref.dtype)


def _tile_from_pairs(in_ref, dst_ref):
    for cl in range(D_SSM // LANES):
        for s in range(SSM_CHUNK):
            v, i = divmod(s, PAIRS_PER_BLOCK)
            acc = None
            for a_ in range(PAIRS_PER_BLOCK):
                shift = (PAIR_W * (a_ - i)) % LANES
                piece = in_ref[PAIRS_PER_BLOCK * cl + a_, :, LANES * v:LANES * (v + 1)].astype(F32)
                piece = pltpu.roll(piece, shift, axis=1) if shift else piece
                acc = piece if acc is None else jnp.where(_lane_window(PAIR_W * a_, PAIR_W), piece, acc)
            dst_ref[cl, pl.ds(s, TILE_CHUNKS, stride=SSM_CHUNK), :] = acc


def _pair_tile():
    return pl.BlockSpec((N_PAIRS, TILE_CHUNKS, CHUNK_W), lambda i: (0, i, 0))


def _tile_scratch():
    return pltpu.VMEM((D_SSM // LANES, TOK_TILE, LANES), F32)


def _fwd_in(x2, g_pre, w_in_g, shards=()):
    seq = x2.shape[0]
    ns = len(shards)
    steps = seq // TOK_TILE

    def body(x_ref, g_ref, w_ref, *rest):
        shard_refs, rest = rest[:ns], rest[ns:]
        (u_ref, zs_ref, qkv_ref, zn_ref, hn_ref), rest = rest[:5], rest[5:]
        u_scr, rest = rest[-1], rest[:-1]
        if ns:
            i = pl.program_id(0)
            _ride(_gather_copies(shard_refs, rest[:ns], *rest[ns:]), i == 0, i == steps - 1)
        x = x_ref[...]
        r = lax.rsqrt(jnp.mean(x * x, axis=-1, keepdims=True) + EPS)
        hn = (x * r * g_ref[...]).astype(BF16)
        hn_ref[...] = hn
        for j in range(N_DEV):
            pj = _mm(hn, w_ref[j])
            for i in range(SHARD_IN // LANES):
                blk = (SHARD_IN // LANES) * j + i
                piece = pj[:, i * LANES:(i + 1) * LANES]
                if blk < 4:
                    u_scr[blk] = piece
                elif blk < 8:
                    zs_ref[:, (blk - 4) * LANES:(blk - 3) * LANES] = piece
                elif blk < 20:
                    qkv_ref[:, (blk - 8) * LANES:(blk - 7) * LANES] = piece.astype(BF16)
                else:
                    zn_ref[:, (blk - 20) * LANES:(blk - 19) * LANES] = piece
        _tile_to_pairs(u_scr, u_ref)

    tok = lambda w: pl.BlockSpec((TOK_TILE, w), lambda i: (i, 0))
    out = pl.pallas_call(
        body, name="fwd_in", grid=(steps,),
        in_specs=[tok(D_MODEL), pl.BlockSpec((1, D_MODEL), lambda i: (0, 0)), _whole_vmem()] + [_ANY()] * ns,
        out_specs=[_pair_tile(), tok(D_SSM), tok(3 * D_NA), tok(D_NA), tok(D_MODEL)] + [_ANY()] * ns,
        out_shape=[jax.ShapeDtypeStruct((N_PAIRS, seq // SSM_CHUNK, CHUNK_W), BF16), jax.ShapeDtypeStruct((seq, D_SSM), F32),
                   jax.ShapeDtypeStruct((seq, 3 * D_NA), BF16), jax.ShapeDtypeStruct((seq, D_NA), F32),
                   jax.ShapeDtypeStruct((seq, D_MODEL), BF16)]
        + [jax.ShapeDtypeStruct((N_DEV,) + sh.shape, sh.dtype) for sh in shards],
        scratch_shapes=([_dma_sems(ns, N_DEV - 1), _dma_sems(ns, N_DEV - 1), _dma_sems(ns)] if ns else [])
        + [_tile_scratch()],
        compiler_params=_params("arbitrary"),
    )(x2, g_pre, w_in_g, *shards)
    return out[:5], list(out[5:])


def _mid(yssm, zs, o, zn, x2, p2, tgt, w_glu, b_glu, w_out, g_post, w_ple_g, g_ple, w_pg):
    seq = x2.shape[0]

    def body(yssm_ref, zs_ref, o_ref, zn_ref, x_ref, p_ref, tgt_ref, wglu_ref, bglu_ref, wout_ref, gpost_ref,
             wple_ref, gple_ref, wpg_ref,
             dyssm_ref, dzs_ref, do_ref, dzn_ref, dh1_ref, loss_ref, dwglu_ref, dbglu_ref, dwout_ref, dwple_ref,
             dwpg_ref, dgpost_ref, dgple_ref, y_scr, dy_scr):
        @pl.when(pl.program_id(0) == 0)
        def _():
            for ref in (loss_ref, dwglu_ref, dbglu_ref, dwout_ref, dwple_ref, dwpg_ref, dgpost_ref, dgple_ref):
                ref[...] = jnp.zeros_like(ref)

        _tile_from_pairs(yssm_ref, y_scr)
        yv = jnp.concatenate([y_scr[cl] for cl in range(D_SSM // LANES)], axis=1)
        g1 = _gelu(yv)
        g1b = g1.astype(BF16)
        sg = _sigmoid(_mm(g1b, wglu_ref[...]) + bglu_ref[...])
        zs_v = zs_ref[...]
        s_zs = _sigmoid(zs_v)
        g2 = g1 * sg
        zn_v = zn_ref[...]
        s_zn = _sigmoid(zn_v)
        o_v = o_ref[...]
        cat = jnp.concatenate([g2 * (zs_v * s_zs), o_v * (zn_v * s_zn)], axis=1).astype(BF16)
        mix = _mm(cat, wout_ref[...])
        r2 = lax.rsqrt(jnp.mean(mix * mix, axis=-1, keepdims=True) + EPS)
        n2 = mix * r2
        gpost = gpost_ref[...]
        h1 = x_ref[...] + n2 * gpost
        pb = p_ref[...].astype(BF16)
        epre = jnp.concatenate([_mm(pb, wple_ref[j]) for j in range(N_DEV)], axis=1)
        r3 = lax.rsqrt(jnp.mean(epre * epre, axis=-1, keepdims=True) + EPS)
        n3 = epre * r3
        gple = gple_ref[...]
        e = n3 * gple
        h1b = h1.astype(BF16)
        gate = _sigmoid(_mm(h1b, wpg_ref[...]))
        diff = h1 + gate * e - tgt_ref[...]
        loss_ref[...] += jnp.sum(diff * diff, axis=0, keepdims=True)

        dy = diff * (1.0 / D_MODEL)
        dgp = (dy * e * gate * (1.0 - gate)).astype(BF16)
        de = dy * gate
        dh1 = dy + _nt(dgp, wpg_ref[...])
        dh1_ref[...] = dh1
        dwpg_ref[...] += _tn(h1b, dgp)
        dgple_ref[...] += jnp.sum(de * n3, axis=0, keepdims=True)
        dn3 = de * gple
        depre = (r3 * (dn3 - n3 * jnp.mean(dn3 * n3, axis=-1, keepdims=True))).astype(BF16)
        for j in range(N_DEV):
            dwple_ref[j] += _tn(pb, depre[:, j * LANES:(j + 1) * LANES])
        dgpost_ref[...] += jnp.sum(dh1 * n2, axis=0, keepdims=True)
        dn2 = dh1 * gpost
        dmix = (r2 * (dn2 - n2 * jnp.mean(dn2 * n2, axis=-1, keepdims=True))).astype(BF16)
        dcat = _nt(dmix, wout_ref[...])
        dwout_ref[...] += _tn(cat, dmix)

        dys = dcat[:, :D_SSM]
        dyn = dcat[:, D_SSM:]
        dg2 = dys * (zs_v * s_zs)
        dzs_ref[...] = (dys * g2 * (s_zs * (1.0 + zs_v * (1.0 - s_zs)))).astype(BF16)
        dt = dg2 * g2 * (1.0 - sg)
        dtb = dt.astype(BF16)
        dg1 = dg2 * sg + _nt(dtb, wglu_ref[...])
        dwglu_ref[...] += _tn(g1b, dtb)
        dbglu_ref[...] += jnp.sum(dt, axis=0, keepdims=True)
        dyssm = dg1 * _gelu_grad(yv)
        for cl in range(D_SSM // LANES):
            dy_scr[cl] = dyssm[:, cl * LANES:(cl + 1) * LANES]
        _tile_to_pairs(dy_scr, dyssm_ref)
        do_ref[...] = (dyn * (zn_v * s_zn)).astype(BF16)
        dzn_ref[...] = (dyn * o_v * (s_zn * (1.0 + zn_v * (1.0 - s_zn)))).astype(BF16)

    tok = lambda w: pl.BlockSpec((TOK_TILE, w), lambda i: (i, 0))
    row = lambda w: pl.BlockSpec((1, w), lambda i: (0, 0))
    vm = _whole_vmem()
    half_b = jax.ShapeDtypeStruct((seq, D_SSM), BF16)
    gain = jax.ShapeDtypeStruct((1, D_MODEL), F32)
    return pl.pallas_call(
        body, name="mid", grid=(seq // TOK_TILE,),
        in_specs=[_pair_tile(), tok(D_SSM), tok(D_NA), tok(D_NA), tok(D_MODEL), tok(D_PLE), tok(D_MODEL),
                  vm, row(D_SSM), vm, row(D_MODEL), vm, row(D_MODEL), vm],
        out_specs=[_pair_tile(), tok(D_SSM), tok(D_NA), tok(D_NA), tok(D_MODEL), vm, vm, vm, vm, vm, vm, vm, vm],
        out_shape=[jax.ShapeDtypeStruct((N_PAIRS, seq // SSM_CHUNK, CHUNK_W), BF16), half_b, half_b, half_b, jax.ShapeDtypeStruct((seq, D_MODEL), F32), gain,
                   jax.ShapeDtypeStruct((D_SSM, D_SSM), F32), jax.ShapeDtypeStruct((1, D_SSM), F32),
                   jax.ShapeDtypeStruct((D_MODEL, D_MODEL), F32),
                   jax.ShapeDtypeStruct((N_DEV, D_PLE, LANES), F32),
                   jax.ShapeDtypeStruct((D_MODEL, D_MODEL), F32), gain, gain],
        scratch_shapes=[_tile_scratch(), _tile_scratch()],
        compiler_params=_params("arbitrary"),
    )(yssm, zs, o, zn, x2, p2, tgt, w_glu, b_glu, w_out, g_post, w_ple_g, g_ple, w_pg)


def _bwd_in(du, dzs, dq, dk, dv, dzn, hn, x2, dh1, g_pre, w_in_g):
    seq = x2.shape[0]

    def body(du_ref, dzs_ref, dq_ref, dk_ref, dv_ref, dzn_ref, hn_ref, x_ref, dh1_ref, g_ref, w_ref,
             dx_ref, dw_ref, dg_ref, dproj_ref, du_scr):
        @pl.when(pl.program_id(0) == 0)
        def _():
            dw_ref[...] = jnp.zeros_like(dw_ref)
            dg_ref[...] = jnp.zeros_like(dg_ref)

        _tile_from_pairs(du_ref, du_scr)
        for cl in range(D_SSM // LANES):
            dproj_ref[:, cl * LANES:(cl + 1) * LANES] = du_scr[cl].astype(BF16)
        for k, ref in enumerate((dzs_ref, dq_ref, dk_ref, dv_ref, dzn_ref), start=1):
            dproj_ref[:, k * D_SSM:(k + 1) * D_SSM] = ref[...].astype(BF16)
        hn = hn_ref[...]
        dhn = jnp.zeros((TOK_TILE, D_MODEL), F32)
        for j in range(N_DEV):
            dpj = dproj_ref[:, j * SHARD_IN:(j + 1) * SHARD_IN]
            dhn += _nt(dpj, w_ref[j])
            dw_ref[j] += _tn(hn, dpj)
        x = x_ref[...]
        r = lax.rsqrt(jnp.mean(x * x, axis=-1, keepdims=True) + EPS)
        n1 = x * r
        dg_ref[...] += jnp.sum(dhn * n1, axis=0, keepdims=True)
        dn1 = dhn * g_ref[...]
        dx_ref[...] = dh1_ref[...] + r * (dn1 - n1 * jnp.mean(dn1 * n1, axis=-1, keepdims=True))

    tok = lambda w: pl.BlockSpec((TOK_TILE, w), lambda i: (i, 0))
    vm = _whole_vmem()
    return pl.pallas_call(
        body, name="bwd_in", grid=(seq // TOK_TILE,),
        in_specs=[_pair_tile()] + [tok(D_SSM)] * 5 + [tok(D_MODEL), tok(D_MODEL), tok(D_MODEL),
                                      pl.BlockSpec((1, D_MODEL), lambda i: (0, 0)), vm],
        out_specs=[tok(D_MODEL), vm, vm],
        out_shape=[jax.ShapeDtypeStruct((seq, D_MODEL), F32),
                   jax.ShapeDtypeStruct((N_DEV, D_MODEL, SHARD_IN), F32),
                   jax.ShapeDtypeStruct((1, D_MODEL), F32)],
        scratch_shapes=[pltpu.VMEM((TOK_TILE, D_IN_PROJ), BF16), _tile_scratch()],
        compiler_params=_params("arbitrary"),
    )(du, dzs, dq, dk, dv, dzn, hn, x2, dh1, g_pre, w_in_g)


SSM_PAIRS = 4
SLOT = 8


def _boundary_scan(nc, pw_ref, buf_ref, conj):
    pad = [jnp.zeros((SLOT - SSM_PAIRS, LANES), F32)]
    lr0, li0, lr1, li1 = (jnp.concatenate([_pw_row(pw_ref, j, SSM_CHUNK, q) for j in range(SSM_PAIRS)] + pad, axis=0)
                          for q in range(4))
    if conj:
        li0, li1 = -li0, -li1

    def step(c, carry):
        hr0, hi0, hr1, hi1 = carry
        up = pl.ds(pl.multiple_of(c * SLOT, SLOT), SLOT)
        dn = pl.ds(pl.multiple_of((nc - 1 - c) * SLOT, SLOT), SLOT)
        ra, rb = (dn, up) if conj else (up, dn)
        s_r0, s_i0 = buf_ref[0, ra, :], buf_ref[1, ra, :]
        s_r1, s_i1 = buf_ref[2, rb, :], buf_ref[3, rb, :]
        buf_ref[0, ra, :] = hr0
        buf_ref[1, ra, :] = hi0
        buf_ref[2, rb, :] = hr1
        buf_ref[3, rb, :] = hi1
        return (lr0 * hr0 - li0 * hi0 + s_r0, lr0 * hi0 + li0 * hr0 + s_i0,
                lr1 * hr1 - li1 * hi1 + s_r1, lr1 * hi1 + li1 * hr1 + s_i1)

    z = jnp.zeros((SLOT, LANES), F32)
    lax.fori_loop(0, nc, step, (z, z, z, z), unroll=8)


def _put_pair(buf_ref, j, nc, val):
    for q in range(4):
        buf_ref[q, pl.ds(j, nc, stride=SLOT), :] = val[:, LANES * q:LANES * (q + 1)]


def _get_pair(buf_ref, j, nc):
    return jnp.concatenate([buf_ref[q, pl.ds(j, nc, stride=SLOT), :] for q in range(4)], axis=1)


def _pw_row(pw_ref, j, k, q):
    return pw_ref[j, k:k + 1, LANES * q:LANES * (q + 1)]


def _mm_f32(a, b, dims):
    return lax.dot_general(a, b, (dims, ((), ())), precision=HIGHEST, preferred_element_type=F32)


_POW_M = (lambda s: SSM_CHUNK - 1 - s, lambda s: s)
_POW_C = (lambda s: s + 1, lambda s: SSM_CHUNK - s)
_POW_K = (lambda s: s, lambda s: SSM_CHUNK - 1 - s)


def _chunk_matrices(j, pw_ref, bb_ref, cc_ref, dd_ref, m_scr, ct_scr, toep_scr, g_scr, kt_scr):
    blk = lambda s: slice(PAIR_W * s, PAIR_W * (s + 1))
    col = lambda q: slice(LANES * q, LANES * (q + 1))
    for d in range(2):
        bbr, bbi = bb_ref[j, 2 * d], bb_ref[j, 2 * d + 1]
        ccr, cci = cc_ref[j, 2 * d], cc_ref[j, 2 * d + 1]
        for s in range(SSM_CHUNK):
            pr, pi = _pw_row(pw_ref, j, _POW_M[d](s), 2 * d), _pw_row(pw_ref, j, _POW_M[d](s), 2 * d + 1)
            m_scr[j, blk(s), col(2 * d)] = (pr * bbr - pi * bbi).astype(m_scr.dtype)
            m_scr[j, blk(s), col(2 * d + 1)] = (pr * bbi + pi * bbr).astype(m_scr.dtype)
            pr, pi = _pw_row(pw_ref, j, _POW_C[d](s), 2 * d), _pw_row(pw_ref, j, _POW_C[d](s), 2 * d + 1)
            ct_scr[j, blk(s), col(2 * d)] = (ccr * pr - cci * pi).astype(ct_scr.dtype)
            ct_scr[j, blk(s), col(2 * d + 1)] = (-(ccr * pi + cci * pr)).astype(ct_scr.dtype)
            pr, pi = _pw_row(pw_ref, j, _POW_K[d](s), 2 * d), _pw_row(pw_ref, j, _POW_K[d](s), 2 * d + 1)
            g_scr[j, d, blk(s), 0:LANES] = ccr * pr - cci * pi
            g_scr[j, d, blk(s), LANES:2 * LANES] = -(ccr * pi + cci * pr)
        kt = _mm_f32(jnp.concatenate([bbr, bbi], axis=1), g_scr[j, d], ((1,), (1,)))
        if d == 0:
            kt = jnp.concatenate([kt[:, 0:LANES] + dd_ref[j], kt[:, LANES:]], axis=1)
        kt_scr[d] = kt
    lane = lax.broadcasted_iota(jnp.int32, (1, CHUNK_W), 1)
    for s in range(SSM_CHUNK):
        lo = PAIR_W * s
        hi = PAIR_W * (s + 1)
        fwd = kt_scr[0] if s == 0 else pltpu.roll(kt_scr[0], lo, axis=1)
        bwd = kt_scr[1] if hi == CHUNK_W else pltpu.roll(kt_scr[1], hi, axis=1)
        row = jnp.where(lane >= lo, fwd, 0.0) + jnp.where(lane < hi, bwd, 0.0)
        toep_scr[j, blk(s), :] = row.astype(toep_scr.dtype)


def _ssm_scratch(nc, mat_dtype, buffers):
    mats = [pltpu.VMEM((SSM_PAIRS, CHUNK_W, CHUNK_W), mat_dtype) for _ in range(3)]
    return mats + [pltpu.VMEM((SSM_PAIRS, 2, CHUNK_W, 2 * LANES), F32), pltpu.VMEM((2, PAIR_W, CHUNK_W), F32)] + [
        pltpu.VMEM((4, nc * SLOT, LANES), F32) for _ in range(buffers)]


def _per_step(*shape):
    return pl.BlockSpec((SSM_PAIRS,) + shape, lambda g: (g,) + (0,) * len(shape))


_TABLE_SPECS = lambda: [_per_step(PW_ROWS, CHUNK_W), _per_step(4, PAIR_W, LANES), _per_step(4, PAIR_W, LANES),
                        _per_step(PAIR_W, LANES)]


def _ssm_fwd(u, pw, bb, cc, dd):
    npair, nc, width = u.shape

    def body(u_ref, pw_ref, bb_ref, cc_ref, dd_ref, y_ref, m_scr, ct_scr, toep_scr, g_scr, kt_scr, h_scr):
        @pl.when(pl.program_id(0) == 0)
        def _():
            h_scr[...] = jnp.zeros_like(h_scr)

        for j in range(SSM_PAIRS):
            _chunk_matrices(j, pw_ref, bb_ref, cc_ref, dd_ref, m_scr, ct_scr, toep_scr, g_scr, kt_scr)
            _put_pair(h_scr, j, nc, _mm(u_ref[j], m_scr[j]))
        _boundary_scan(nc, pw_ref, h_scr, conj=False)
        for j in range(SSM_PAIRS):
            hin = _get_pair(h_scr, j, nc)
            y_ref[j] = _mm(u_ref[j], toep_scr[j]) + _nt(hin.astype(u.dtype), ct_scr[j])

    return pl.pallas_call(
        body, name="ssm_fwd", grid=(npair // SSM_PAIRS,),
        in_specs=[_per_step(nc, width)] + _TABLE_SPECS(),
        out_specs=_per_step(nc, width),
        out_shape=jax.ShapeDtypeStruct((npair, nc, width), F32),
        scratch_shapes=_ssm_scratch(nc, u.dtype, 1),
        compiler_params=_params("arbitrary"),
    )(u, pw, bb, cc, dd)


def _ssm_bwd(u, dy, pw, bb, cc, dd, chip_parts=()):
    npair, nc, width = u.shape
    ns = len(chip_parts)
    steps = npair // SSM_PAIRS

    def body(u_ref, dy_ref, pw_ref, bb_ref, cc_ref, dd_ref, *rest):
        part_refs, rest = rest[:ns], rest[ns:]
        (du_ref, dpw_ref, dbb_ref, dcc_ref, ddd_ref), rest = rest[:5], rest[5:]
        land_refs, rest = rest[:ns], rest[ns:]
        m_scr, ct_scr, toep_scr, g_scr, kt_scr, h_scr, d_scr = rest[:7]
        g = pl.program_id(0)
        if ns:
            _ride(_chip_copies(part_refs, land_refs, *rest[7:]), g == 0, g == steps - 1)

        @pl.when(g == 0)
        def _():
            h_scr[...] = jnp.zeros_like(h_scr)
            d_scr[...] = jnp.zeros_like(d_scr)

        for j in range(SSM_PAIRS):
            _chunk_matrices(j, pw_ref, bb_ref, cc_ref, dd_ref, m_scr, ct_scr, toep_scr, g_scr, kt_scr)
            _put_pair(h_scr, j, nc, _mm(u_ref[j], m_scr[j]))
            _put_pair(d_scr, j, nc, _mm(dy_ref[j], ct_scr[j]))
        _boundary_scan(nc, pw_ref, h_scr, conj=False)
        _boundary_scan(nc, pw_ref, d_scr, conj=True)

        dpw_ref[...] = jnp.zeros_like(dpw_ref)
        blk = lambda s: slice(PAIR_W * s, PAIR_W * (s + 1))
        col = lambda q: slice(LANES * q, LANES * (q + 1))
        lane = lax.broadcasted_iota(jnp.int32, (1, CHUNK_W), 1)
        for j in range(SSM_PAIRS):
            uv = u_ref[j]
            dyb = dy_ref[j]
            hin = _get_pair(h_scr, j, nc)
            ds = _get_pair(d_scr, j, nc)
            dsb = ds.astype(u.dtype)
            du_ref[j] = _nt(dsb, m_scr[j]) + _nt(dyb, toep_scr[j])
            dm = _tn(uv, dsb)
            dct = _tn(dyb, hin.astype(u.dtype))
            dtoep = _tn(uv, dyb)

            def add_pw(k, q, val):
                dpw_ref[j, k:k + 1, col(q)] += jnp.sum(val, axis=0, keepdims=True)

            for d in range(2):
                g_r, g_i = ds[:, col(2 * d)], ds[:, col(2 * d + 1)]
                h_r, h_i = hin[:, col(2 * d)], hin[:, col(2 * d + 1)]
                add_pw(SSM_CHUNK, 2 * d, g_r * h_r + g_i * h_i)
                add_pw(SSM_CHUNK, 2 * d + 1, g_i * h_r - g_r * h_i)

            dkt0 = jnp.zeros((PAIR_W, CHUNK_W), F32)
            dkt1 = jnp.zeros((PAIR_W, CHUNK_W), F32)
            for s in range(SSM_CHUNK):
                lo = PAIR_W * s
                hi = PAIR_W * (s + 1)
                row = dtoep[blk(s), :]
                fwd = jnp.where(lane >= lo, row, 0.0)
                bwd = jnp.where(lane < hi, row, 0.0)
                dkt0 += fwd if s == 0 else pltpu.roll(fwd, CHUNK_W - lo, axis=1)
                dkt1 += bwd if hi == CHUNK_W else pltpu.roll(bwd, CHUNK_W - hi, axis=1)
            ddd_ref[j] = dkt0[:, 0:LANES]

            for d, dkt in enumerate((dkt0, dkt1)):
                bbr, bbi = bb_ref[j, 2 * d], bb_ref[j, 2 * d + 1]
                ccr, cci = cc_ref[j, 2 * d], cc_ref[j, 2 * d + 1]
                dbbcat = _mm_f32(dkt, g_scr[j, d], ((1,), (0,)))
                dg = _mm_f32(dkt, jnp.concatenate([bbr, bbi], axis=1), ((0,), (0,)))
                dbbr, dbbi = dbbcat[:, 0:LANES], dbbcat[:, LANES:]
                dccr = jnp.zeros((PAIR_W, LANES), F32)
                dcci = jnp.zeros((PAIR_W, LANES), F32)
                for s in range(SSM_CHUNK):
                    k = _POW_M[d](s)
                    pr, pi = _pw_row(pw_ref, j, k, 2 * d), _pw_row(pw_ref, j, k, 2 * d + 1)
                    gr, gi = dm[blk(s), col(2 * d)], dm[blk(s), col(2 * d + 1)]
                    dbbr += gr * pr + gi * pi
                    dbbi += gi * pr - gr * pi
                    add_pw(k, 2 * d, gr * bbr + gi * bbi)
                    add_pw(k, 2 * d + 1, gi * bbr - gr * bbi)
                    for k, gr, gi in ((_POW_C[d](s), dct[blk(s), col(2 * d)], dct[blk(s), col(2 * d + 1)]),
                                      (_POW_K[d](s), dg[blk(s), 0:LANES], dg[blk(s), LANES:])):
                        pr, pi = _pw_row(pw_ref, j, k, 2 * d), _pw_row(pw_ref, j, k, 2 * d + 1)
                        dccr += gr * pr - gi * pi
                        dcci += -(gr * pi + gi * pr)
                        add_pw(k, 2 * d, gr * ccr - gi * cci)
                        add_pw(k, 2 * d + 1, -(gr * cci + gi * ccr))
                dbb_ref[j, 2 * d] = dbbr
                dbb_ref[j, 2 * d + 1] = dbbi
                dcc_ref[j, 2 * d] = dccr
                dcc_ref[j, 2 * d + 1] = dcci

    out = pl.pallas_call(
        body, name="ssm_bwd", grid=(steps,),
        in_specs=[_per_step(nc, width), _per_step(nc, width)] + _TABLE_SPECS() + [_ANY()] * ns,
        out_specs=[_per_step(nc, width)] + _TABLE_SPECS() + [_ANY()] * ns,
        out_shape=[jax.ShapeDtypeStruct((npair, nc, width), F32), jax.ShapeDtypeStruct(pw.shape, F32),
                   jax.ShapeDtypeStruct(bb.shape, F32), jax.ShapeDtypeStruct(cc.shape, F32),
                   jax.ShapeDtypeStruct(dd.shape, F32)]
        + [jax.ShapeDtypeStruct(pt.shape, pt.dtype) for pt in chip_parts],
        scratch_shapes=_ssm_scratch(nc, u.dtype, 2) + ([_dma_sems(ns, 3), _dma_sems(ns, 3), _dma_sems(ns)] if ns else []),
        compiler_params=_params("arbitrary"),
    )(u, dy, pw, bb, cc, dd, *chip_parts)
    return out[:5], list(out[5:])


NA_Q = NA_QROWS * GRID_W
NA_K = NA_KROWS * GRID_W
NA_SCALE = NA_HEAD_DIM ** -0.5


def _na_block(b, nb, rows):
    start = jnp.clip(NA_QROWS * b - NA_ROWS // 2, 0, rows - NA_KROWS) * GRID_W
    kind = jnp.where(b == 0, 0, jnp.where(b == nb - 1, 2, 1))
    return pl.multiple_of(start, GRID_W), kind


NA_CHUNK = 16


def _na_pieces(kind, i):
    ri, q0 = divmod(i * NA_CHUNK, GRID_W)
    off = (NA_ROWS - 1, NA_ROWS // 2 - 1, -1)[kind]
    lo = (0, ri, NA_KROWS - NA_ROWS)[kind]
    modes = {(True, True): 'both', (True, False): 'even', (False, True): 'odd', (False, False): None}
    out = []
    for k2 in range(NA_KROWS // 2):
        inside = tuple(lo <= kr < lo + NA_ROWS for kr in (2 * k2, 2 * k2 + 1))
        out.append((2 * k2 - ri + off + 1, slice(q0, q0 + NA_CHUNK), modes[inside]))
    return out


def _na_softmax_pieces(s_ref, tab_ref, hh, kind, i):
    rows = slice(i * NA_CHUNK, (i + 1) * NA_CHUNK)
    lane = lax.broadcasted_iota(jnp.int32, (1, LANES), 1)
    xs = []
    for k2, (t, q, mode) in enumerate(_na_pieces(kind, i)):
        if mode is None:
            xs.append(None)
            continue
        bias = tab_ref[hh, t, q, :]
        if mode == 'even':
            bias = jnp.where(lane < GRID_W, bias, NEG)
        elif mode == 'odd':
            bias = jnp.where(lane >= GRID_W, bias, NEG)
        xs.append(s_ref[hh, rows, k2 * LANES:(k2 + 1) * LANES] + bias)
    live = [x for x in xs if x is not None]
    m = jnp.max(functools.reduce(jnp.maximum, live), axis=-1, keepdims=True)
    es = [None if x is None else jnp.exp(x - m) for x in xs]
    total = jnp.sum(functools.reduce(jnp.add, [e for e in es if e is not None]), axis=-1, keepdims=True)
    inv = 1.0 / total
    return [None if e is None else e * inv for e in es]


def _na_heads():
    lane = lax.broadcasted_iota(jnp.int32, (1, LANES), 1)
    return [lane < NA_HEAD_DIM, lane >= NA_HEAD_DIM]


def _na_fwd(qkv, bias):
    seq = qkv.shape[0]
    rows = seq // GRID_W
    nb = rows // NA_QROWS

    def body(q_ref, k_ref, v_ref, bias_ref, o_ref, p_ref, s_scr):
        start, kind = _na_block(pl.program_id(1), nb, rows)

        def block(static_kind):
            q2 = q_ref[...] * NA_SCALE
            kw = k_ref[pl.ds(start, NA_K), :]
            vw = v_ref[pl.ds(start, NA_K), :]
            heads = _na_heads()
            for hh in range(2):
                s_scr[hh] = _nt(jnp.where(heads[hh], q2, jnp.zeros_like(q2)), kw)
            for hh in range(2):
                for i in range(NA_Q // NA_CHUNK):
                    r = slice(i * NA_CHUNK, (i + 1) * NA_CHUNK)
                    for k2, p in enumerate(_na_softmax_pieces(s_scr, bias_ref, hh, static_kind, i)):
                        p = jnp.zeros((NA_CHUNK, LANES), F32) if p is None else p
                        p_ref[hh, r, k2 * LANES:(k2 + 1) * LANES] = p.astype(p_ref.dtype)
            o_ref[...] = jnp.where(heads[0], _mm(p_ref[0], vw), _mm(p_ref[1], vw))

        for static_kind in range(3):
            pl.when(kind == static_kind)(functools.partial(block, static_kind))

    return pl.pallas_call(
        body, name="na_fwd", grid=(NA_HEADS // 2, nb),
        in_specs=[pl.BlockSpec((NA_Q, LANES), lambda hp, b: (b, hp)),
                  pl.BlockSpec((seq, LANES), lambda hp, b: (0, 4 + hp)),
                  pl.BlockSpec((seq, LANES), lambda hp, b: (0, 8 + hp)),
                  pl.BlockSpec((2, NA_TAB, GRID_W, LANES), lambda hp, b: (hp, 0, 0, 0))],
        out_specs=[pl.BlockSpec((NA_Q, LANES), lambda hp, b: (b, hp)),
                   pl.BlockSpec((2, NA_Q, NA_K), lambda hp, b: (hp, b, 0))],
        out_shape=[jax.ShapeDtypeStruct((seq, D_NA), F32), jax.ShapeDtypeStruct((NA_HEADS, seq, NA_K), qkv.dtype)],
        scratch_shapes=[pltpu.VMEM((2, NA_Q, NA_K), F32)],
        compiler_params=_params("arbitrary", "arbitrary"),
    )(qkv, qkv, qkv, bias)


def _na_bwd(qkv, do, probs, parts=()):
    seq = qkv.shape[0]
    rows = seq // GRID_W
    nb = rows // NA_QROWS
    ns = len(parts)

    def body(q_ref, k_ref, v_ref, do_ref, p_ref, *rest):
        part_refs, rest = rest[:ns], rest[ns:]
        (dq_ref, dk_ref, dv_ref, dbias_ref), rest = rest[:4], rest[4:]
        land_refs, rest = rest[:ns], rest[ns:]
        dp_scr, ds_scr = rest[:2]
        b = pl.program_id(1)
        start, kind = _na_block(b, nb, rows)
        if ns:
            hp = pl.program_id(0)
            _ride(_pair_copies(part_refs, land_refs, *rest[2:]), (hp == 0) & (b == 0),
                  (hp == NA_HEADS // 2 - 1) & (b == nb - 1))

        @pl.when(b == 0)
        def _():
            dk_ref[...] = jnp.zeros_like(dk_ref)
            dv_ref[...] = jnp.zeros_like(dv_ref)
            dbias_ref[...] = jnp.zeros_like(dbias_ref)

        def block(static_kind):
            q2 = q_ref[...] * NA_SCALE
            kw = k_ref[pl.ds(start, NA_K), :]
            vw = v_ref[pl.ds(start, NA_K), :]
            do2 = do_ref[...].astype(q2.dtype)
            heads = _na_heads()
            col = lambda k2: slice(k2 * LANES, (k2 + 1) * LANES)
            zero = jnp.zeros((NA_CHUNK, LANES), ds_scr.dtype)
            for hh in range(2):
                dp_scr[hh] = _nt(jnp.where(heads[hh], do2, jnp.zeros_like(do2)), vw)
            for hh in range(2):
                for i in range(NA_Q // NA_CHUNK):
                    r = slice(i * NA_CHUNK, (i + 1) * NA_CHUNK)
                    pieces = _na_pieces(static_kind, i)
                    ps = [None if mode is None else p_ref[hh, r, col(k2)].astype(F32)
                          for k2, (_, _, mode) in enumerate(pieces)]
                    dps = [None if p is None else dp_scr[hh, r, col(k2)] for k2, p in enumerate(ps)]
                    pdp = functools.reduce(jnp.add, [p * dp for p, dp in zip(ps, dps) if p is not None])
                    rowsum = jnp.sum(pdp, axis=-1, keepdims=True)
                    for k2, (t, q, _) in enumerate(pieces):
                        if ps[k2] is None:
                            ds_scr[hh, r, col(k2)] = zero
                            continue
                        ds = ps[k2] * (dps[k2] - rowsum)
                        dbias_ref[hh, t, q, :] += ds
                        ds_scr[hh, r, col(k2)] = ds.astype(ds_scr.dtype)
            dq_ref[...] = jnp.where(heads[0], _mm(ds_scr[0], kw), _mm(ds_scr[1], kw)) * NA_SCALE
            dk_ref[pl.ds(start, NA_K), :] += jnp.where(heads[0], _tn(ds_scr[0], q2), _tn(ds_scr[1], q2))
            dv_ref[pl.ds(start, NA_K), :] += jnp.where(heads[0], _tn(p_ref[0], do2), _tn(p_ref[1], do2))

        for static_kind in range(3):
            pl.when(kind == static_kind)(functools.partial(block, static_kind))

    out = pl.pallas_call(
        body, name="na_bwd", grid=(NA_HEADS // 2, nb),
        in_specs=[pl.BlockSpec((NA_Q, LANES), lambda hp, b: (b, hp)),
                  pl.BlockSpec((seq, LANES), lambda hp, b: (0, 4 + hp)),
                  pl.BlockSpec((seq, LANES), lambda hp, b: (0, 8 + hp)),
                  pl.BlockSpec((NA_Q, LANES), lambda hp, b: (b, hp)),
                  pl.BlockSpec((2, NA_Q, NA_K), lambda hp, b: (hp, b, 0))] + [_ANY()] * ns,
        out_specs=[pl.BlockSpec((NA_Q, LANES), lambda hp, b: (b, hp)),
                   pl.BlockSpec((seq, LANES), lambda hp, b: (0, hp)),
                   pl.BlockSpec((seq, LANES), lambda hp, b: (0, hp)),
                   pl.BlockSpec((2, NA_TAB, GRID_W, LANES), lambda hp, b: (hp, 0, 0, 0))] + [_ANY()] * ns,
        out_shape=[jax.ShapeDtypeStruct((seq, D_NA), F32), jax.ShapeDtypeStruct((seq, D_NA), F32),
                   jax.ShapeDtypeStruct((seq, D_NA), F32),
                   jax.ShapeDtypeStruct((NA_HEADS, NA_TAB, GRID_W, LANES), F32)]
        + [jax.ShapeDtypeStruct((4,) + pt.shape[1:], pt.dtype) for pt in parts],
        scratch_shapes=[pltpu.VMEM((2, NA_Q, NA_K), F32), pltpu.VMEM((2, NA_Q, NA_K), qkv.dtype)]
        + ([_dma_sems(ns, 4), _dma_sems(ns, 4)] if ns else []),
        compiler_params=_params("arbitrary", "arbitrary"),
    )(qkv, qkv, qkv, do, probs, *parts)
    return out[:4], list(out[4:])


def _local_step(x2, p2, tgt, g_pre, g_post, w_in_g, ssm, w_glu, b_glu, rpb, w_out, w_ple, g_ple, w_pg,
                distributed=False):
    (pw, bb, cc, dd), ssm_vjp = jax.vjp(_ssm_tables, *ssm)
    bias, bias_vjp = jax.vjp(_na_table, rpb)

    riders = (w_glu, w_out, w_ple, w_pg) if distributed else ()
    (u_p, zs, qkv, zn, hn), gathered = _fwd_in(x2, g_pre, w_in_g, riders)
    if distributed:
        w_glu, w_out, w_ple, w_pg = gathered
        w_glu = w_glu.reshape(D_SSM, D_SSM)
        w_out = w_out.reshape(D_MODEL, D_MODEL)
        w_pg = w_pg.reshape(D_MODEL, D_MODEL)
    yssm = _ssm_fwd(u_p, pw, bb, cc, dd)
    o, probs = _na_fwd(qkv, bias)
    dyssm, dzs, do, dzn, dh1, sq, d_wglu, d_bglu, d_wout, d_wple, d_wpg, d_gpost, d_gple = _mid(
        yssm, zs, o, zn, x2, p2, tgt, w_glu, b_glu, w_out, g_post, w_ple, g_ple, w_pg)
    mid_grads = [d_wglu.reshape(N_DEV, D_SSM // N_DEV, D_SSM), d_wout.reshape(N_DEV, D_MODEL // N_DEV, D_MODEL),
                 d_wple, d_wpg.reshape(N_DEV, D_MODEL // N_DEV, D_MODEL)]
    (dq, dk, dv, dbias), lands = _na_bwd(qkv, do, probs, mid_grads if distributed else ())
    (d_rpb,) = bias_vjp(dbias)
    chip_parts = _pair_sum(mid_grads, lands)[0] if distributed else ()
    (du_p, dpw, dbb, dcc, ddd), chip_lands = _ssm_bwd(u_p, dyssm, pw, bb, cc, dd, chip_parts)
    d_ssm = ssm_vjp((dpw, dbb, dcc, ddd))
    dx, d_win, d_gpre = _bwd_in(du_p, dzs, dq, dk, dv, dzn, hn, x2, dh1, g_pre, w_in_g)
    if distributed:
        d_wglu, d_wout, d_wple, d_wpg = chip_lands
    return sq, dx, dict(norm_pre=d_gpre, norm_post=d_gpost, w_in=d_win, ssm=d_ssm, w_glu=d_wglu, b_glu=d_bglu,
                        na_rpb=d_rpb, w_out=d_wout, w_ple=d_wple, ple_norm=d_gple, w_ple_gate=d_wpg)


def _all_gather(shard):
    m_per, n = shard.shape

    def body(x_ref, out_ref, send_sems, recv_sems, local_sem):
        x, y, c = _place()
        me, sibling = (x, y, c), (x, y, 1 - c)
        chips = [(1 - x, y), (x, 1 - y), (1 - x, 1 - y)]

        def rows(px, py, pc):
            return out_ref.at[pl.ds((4 * px + 2 * py + pc) * m_per, m_per), :]

        def copy(k, block, to, src=None):
            return pltpu.make_async_remote_copy(
                src_ref=rows(*block) if src is None else src, dst_ref=rows(*block),
                send_sem=send_sems.at[k], recv_sem=recv_sems.at[k], device_id=to, device_id_type=MESH)

        mine = pltpu.make_async_copy(x_ref, rows(*me), local_sem)
        mine.start()
        first = [copy(0, me, sibling, src=x_ref)]
        first += [copy(1 + j, me, (*chip, c), src=x_ref) for j, chip in enumerate(chips)]
        for cp in first:
            cp.start()
        passed = [copy(4 + j, (*chip, c), sibling) for j, chip in enumerate(chips)]
        for j, chip in enumerate(chips):
            copy(1 + j, (*chip, c), me).wait_recv()
            passed[j].start()
        copy(0, sibling, me).wait_recv()
        for j, chip in enumerate(chips):
            copy(4 + j, (*chip, 1 - c), me).wait_recv()
        for cp in first + passed:
            cp.wait_send()
        mine.wait()

    return pl.pallas_call(
        body, name="all_gather",
        out_shape=jax.ShapeDtypeStruct((N_DEV * m_per, n), shard.dtype),
        in_specs=[_whole_vmem()], out_specs=_whole_vmem(),
        scratch_shapes=[pltpu.SemaphoreType.DMA((7,)), pltpu.SemaphoreType.DMA((7,)), pltpu.SemaphoreType.DMA],
        compiler_params=pltpu.CompilerParams(vmem_limit_bytes=VMEM_LIMIT),
    )(shard)


def _exchange_first(pair_parts, all_parts):
    n1, n2 = len(pair_parts), len(all_parts)

    def body(*refs):
        ins, refs = refs[:n1 + n2], refs[n1 + n2:]
        outs, sems = refs[:n1 + n2], refs[n1 + n2:]
        copies = (_pair_copies(ins[:n1], outs[:n1], *sems[:2])
                  + _alltoall_copies(ins[n1:], outs[n1:], *sems[2:]))
        _start_all(copies)
        _wait_all(copies)

    out = pl.pallas_call(
        body, name="exchange_first",
        out_shape=[jax.ShapeDtypeStruct((4,) + pt.shape[1:], pt.dtype) for pt in pair_parts]
        + [jax.ShapeDtypeStruct(pt.shape, pt.dtype) for pt in all_parts],
        in_specs=[_ANY()] * (n1 + n2), out_specs=[_ANY()] * (n1 + n2),
        scratch_shapes=[_dma_sems(n1, 4), _dma_sems(n1, 4),
                        _dma_sems(n2, N_DEV - 1), _dma_sems(n2, N_DEV - 1), _dma_sems(n2)],
    )(*pair_parts, *all_parts)
    return list(out[:n1]), list(out[n1:])


def _pair_sum(parts, lands, all_lands=()):
    ns, na = len(parts), len(all_lands)

    def body(*refs):
        c = lax.axis_index("c")
        ins, outs = refs[:2 * ns + na], refs[2 * ns + na:]
        for part_ref, land_ref, out_ref in zip(ins[:ns], ins[ns:2 * ns], outs[:ns]):
            for q in range(4):
                out_ref[q] = (part_ref[q, c] + land_ref[q]).astype(BF16)
        for land_ref, out_ref in zip(ins[2 * ns:], outs[ns:]):
            acc = land_ref[0]
            for j in range(1, N_DEV):
                acc = acc + land_ref[j]
            out_ref[...] = acc

    out = pl.pallas_call(
        body, name="pair_sum",
        in_specs=[_whole_vmem()] * (2 * ns + na), out_specs=[_whole_vmem()] * (ns + na),
        out_shape=[jax.ShapeDtypeStruct(ld.shape, BF16) for ld in lands]
        + [jax.ShapeDtypeStruct(ld.shape[1:], ld.dtype) for ld in all_lands],
        compiler_params=pltpu.CompilerParams(vmem_limit_bytes=VMEM_LIMIT),
    )(*[pt.reshape((4, 2) + pt.shape[1:]) for pt in parts], *lands, *all_lands)
    return list(out[:ns]), list(out[ns:])


def _exchange_second(chip_parts, shards):
    n1, n2 = len(chip_parts), len(shards)

    def body(*refs):
        ins, refs = refs[:n1 + n2], refs[n1 + n2:]
        outs, sems = refs[:n1 + n2], refs[n1 + n2:]
        copies = (_chip_copies(ins[:n1], outs[:n1], *sems[:3])
                  + _gather_copies(ins[n1:], outs[n1:], *sems[3:]))
        _start_all(copies)
        _wait_all(copies)

    out = pl.pallas_call(
        body, name="exchange_second",
        out_shape=[jax.ShapeDtypeStruct(pt.shape, pt.dtype) for pt in chip_parts]
        + [jax.ShapeDtypeStruct((N_DEV,) + sh.shape, sh.dtype) for sh in shards],
        in_specs=[_ANY()] * (n1 + n2), out_specs=[_ANY()] * (n1 + n2),
        scratch_shapes=[_dma_sems(n1, 3), _dma_sems(n1, 3), _dma_sems(n1),
                        _dma_sems(n2, N_DEV - 1), _dma_sems(n2, N_DEV - 1), _dma_sems(n2)],
    )(*chip_parts, *shards)
    return list(out[:n1]), list(out[n1:])


ADAM_ROWS = 256


def _adamw(ws, gs, ms, vs):
    n = len(ws)

    def body(*refs):
        ins, outs = refs[:4 * n], refs[4 * n:]
        for t in range(n):
            w_ref, g_ref, m_ref, v_ref = ins[t], ins[n + t], ins[2 * n + t], ins[3 * n + t]
            go_ref, d_ref, nm_ref, nv_ref = outs[4 * t:4 * t + 4]
            rows = w_ref.shape[0]
            for lo in range(0, rows, ADAM_ROWS):
                r = slice(lo, min(lo + ADAM_ROWS, rows))
                if len(g_ref.shape) == 3:
                    g = g_ref[0, r, :].astype(F32)
                    for j in range(1, g_ref.shape[0]):
                        g = g + g_ref[j, r, :].astype(F32)
                else:
                    g = g_ref[r, :]
                go_ref[r, :] = g
                nm = ADAM_B1 * m_ref[r, :] + (1.0 - ADAM_B1) * g
                nv = ADAM_B2 * v_ref[r, :] + (1.0 - ADAM_B2) * (g * g)
                m_hat = nm / (1.0 - ADAM_B1 ** ADAM_STEP)
                v_hat = nv / (1.0 - ADAM_B2 ** ADAM_STEP)
                d_ref[r, :] = -ADAM_LR * (m_hat / (jnp.sqrt(v_hat) + ADAM_EPS) + ADAM_WD * w_ref[r, :])
                nm_ref[r, :] = nm
                nv_ref[r, :] = nv

    out = pl.pallas_call(
        body, name="adamw",
        in_specs=[_whole_vmem()] * (4 * n), out_specs=[_whole_vmem()] * (4 * n),
        out_shape=[jax.ShapeDtypeStruct(w.shape, F32) for w in ws for _ in range(4)],
        compiler_params=pltpu.CompilerParams(vmem_limit_bytes=VMEM_LIMIT),
    )(*ws, *gs, *ms, *vs)
    return [tuple(out[4 * t:4 * t + 4]) for t in range(n)]


_SLAB = 8 * LANES


def _flat_rows(a):
    flat = a.reshape(-1)
    pad = (-flat.shape[0]) % _SLAB
    if pad:
        flat = jnp.concatenate([flat, jnp.zeros((pad,), flat.dtype)])
    return flat.reshape(-1, LANES)


def _pack(arrays):
    slabs = [_flat_rows(a) for a in arrays]
    return jnp.concatenate(slabs, axis=0), [s.shape[0] for s in slabs]


def _unpack(packed, like, rows):
    out, at = [], 0
    for a, r in zip(like, rows):
        out.append(packed[at:at + r].reshape(-1)[:a.size].reshape(a.shape))
        at += r
    return out


SMALL = ('norm_pre', 'norm_post', 'ssm_a_re', 'ssm_a_im', 'ssm_log_dt', 'ssm_b_re', 'ssm_b_im', 'ssm_c_re',
         'ssm_c_im', 'ssm_d', 'b_glu', 'na_rpb', 'ple_norm')
BIG = ('w_in', 'w_glu', 'w_out', 'w_ple', 'w_ple_gate')
ORDER = ('norm_pre', 'norm_post', 'w_in', 'ssm_a_re', 'ssm_a_im', 'ssm_log_dt', 'ssm_b_re', 'ssm_b_im', 'ssm_c_re',
         'ssm_c_im', 'ssm_d', 'w_glu', 'b_glu', 'na_rpb', 'w_out', 'w_ple', 'ple_norm', 'w_ple_gate')


def kernel(x, p, norm_pre, norm_post, w_in, ssm_a_re, ssm_a_im, ssm_log_dt, ssm_b_re, ssm_b_im, ssm_c_re, ssm_c_im, ssm_d, w_glu, b_glu, na_rpb, w_out, w_ple, ple_norm, w_ple_gate, loss_target, m_norm_pre, m_norm_post, m_w_in, m_ssm_a_re, m_ssm_a_im, m_ssm_log_dt, m_ssm_b_re, m_ssm_b_im, m_ssm_c_re, m_ssm_c_im, m_ssm_d, m_w_glu, m_b_glu, m_na_rpb, m_w_out, m_w_ple, m_ple_norm, m_w_ple_gate, v_norm_pre, v_norm_post, v_w_in, v_ssm_a_re, v_ssm_a_im, v_ssm_log_dt, v_ssm_b_re, v_ssm_b_im, v_ssm_c_re, v_ssm_c_im, v_ssm_d, v_w_glu, v_b_glu, v_na_rpb, v_w_out, v_w_ple, v_ple_norm, v_w_ple_gate):
    args = dict(locals())
    weights = {n: args[n] for n in ORDER}
    mom_m = {n: args["m_" + n] for n in ORDER}
    mom_v = {n: args["v_" + n] for n in ORDER}

    w_in_g = _all_gather(w_in[0].astype(BF16)).reshape(N_DEV, D_MODEL, SHARD_IN)
    blocks = [weights[n][0].astype(BF16) for n in ('w_glu', 'w_out', 'w_ple', 'w_ple_gate')]

    ssm = tuple(weights[n][0] for n in ('ssm_a_re', 'ssm_a_im', 'ssm_log_dt', 'ssm_b_re', 'ssm_b_im',
                                        'ssm_c_re', 'ssm_c_im', 'ssm_d'))
    sq, dx, grads = _local_step(x[0], p[0, 0], loss_target[0], norm_pre, norm_post, w_in_g, ssm, blocks[0], b_glu,
                                na_rpb[0], blocks[1], blocks[2], ple_norm, blocks[3], distributed=True)
    loss_local = (0.5 / D_MODEL * jnp.sum(sq)).reshape(1)

    local = dict(norm_pre=grads['norm_pre'], norm_post=grads['norm_post'], b_glu=grads['b_glu'],
                 na_rpb=grads['na_rpb'][None], ple_norm=grads['ple_norm'])
    for n, g in zip(('ssm_a_re', 'ssm_a_im', 'ssm_log_dt', 'ssm_b_re', 'ssm_b_im', 'ssm_c_re', 'ssm_c_im', 'ssm_d'),
                    grads['ssm']):
        local[n] = g[None]

    small_flat, _ = _pack([local[n] for n in SMALL] + [loss_local])
    pad = (-small_flat.shape[0]) % (8 * N_DEV)
    if pad:
        small_flat = jnp.concatenate([small_flat, jnp.zeros((pad, LANES), F32)], axis=0)
    small_per = small_flat.shape[0] // N_DEV
    d_win = [grads['w_in']]
    lands, small_lands = _exchange_first(d_win, [small_flat.reshape(N_DEV, small_per, LANES)])
    chip_parts, small_sums = _pair_sum(d_win, lands, small_lands)
    chip_lands, small_full = _exchange_second(chip_parts, small_sums)
    grads['w_in'] = chip_lands[0]
    small_all = small_full[0].reshape(N_DEV * small_per, LANES)

    like = [weights[n] for n in SMALL]
    w_s, rows_s = _pack(like)
    m_s, _ = _pack([mom_m[n] for n in SMALL])
    v_s, _ = _pack([mom_v[n] for n in SMALL])
    loss = small_all[w_s.shape[0], 0]
    results = _adamw([weights[n][0] for n in BIG] + [w_s], [grads[n] for n in BIG] + [small_all[:w_s.shape[0]]],
                     [mom_m[n][0] for n in BIG] + [m_s], [mom_v[n][0] for n in BIG] + [v_s])
    outs = {n: tuple(a[None] for a in res) for n, res in zip(BIG, results)}
    unpacked = [_unpack(a, like, rows_s) for a in results[-1]]
    for i, n in enumerate(SMALL):
        outs[n] = tuple(u[i] for u in unpacked)

    return (loss, dx[None], *[outs[n][0] for n in ORDER], *[outs[n][1] for n in ORDER],
            *[outs[n][2] for n in ORDER], *[outs[n][3] for n in ORDER])
```

```python
import functools

import jax
import jax.numpy as jnp
from jax import lax
from jax.experimental import pallas as pl
from jax.experimental.pallas import tpu as pltpu

F32 = jnp.float32
BF16 = jnp.bfloat16
HIGHEST = lax.Precision.HIGHEST

D_MODEL = 1024
D_PLE = 256
GRID_W = 64
D_SSM = 512
SSM_GROUP = 16
N_PAIRS = 16
SSM_STATE = 64
D_NA = 512
NA_HEADS = 8
NA_HEAD_DIM = 64
NA_ROWS = 8
NA_COLS = 16
D_IN_PROJ = 3072
EPS = 1e-6
N_DEV = 8
SHARD_IN = D_IN_PROJ // N_DEV
LANES = 128
SSM_CHUNK = 16
PW_ROWS = 24
TOK_TILE = 256
NA_QROWS = 4
NA_KROWS = 12
NEG = -1e30
VMEM_LIMIT = 56 * 1024 * 1024

ADAM_LR = 0.001
ADAM_B1 = 0.9
ADAM_B2 = 0.999
ADAM_EPS = 1e-08
ADAM_WD = 0.01
ADAM_STEP = 10

MESH = pl.DeviceIdType.MESH


def _params(*sem):
    return pltpu.CompilerParams(dimension_semantics=sem or None, vmem_limit_bytes=VMEM_LIMIT)


def _whole_vmem():
    return pl.BlockSpec(memory_space=pltpu.VMEM)


def _nt(a, b):
    return lax.dot_general(a, b, (((1,), (1,)), ((), ())), preferred_element_type=F32)


def _tn(a, b):
    return lax.dot_general(a, b, (((0,), (0,)), ((), ())), preferred_element_type=F32)


def _mm(a, b):
    return jnp.dot(a, b, preferred_element_type=F32)


def _sigmoid(x):
    return 1.0 / (1.0 + jnp.exp(-x))


_GELU_C = 0.7978845608028654


def _gelu(x):
    return 0.5 * x * (1.0 + jnp.tanh(_GELU_C * (x + 0.044715 * x * x * x)))


def _gelu_grad(x):
    th = jnp.tanh(_GELU_C * (x + 0.044715 * x * x * x))
    return 0.5 * (1.0 + th) + 0.5 * x * (1.0 - th * th) * _GELU_C * (1.0 + 3.0 * 0.044715 * x * x)


def _ssm_tables(a_re, a_im, log_dt, b_re, b_im, c_re, c_im, d):
    T, P, H = SSM_CHUNK, SSM_STATE, SSM_GROUP
    dt = jnp.exp(log_dt)[..., None]
    xr = dt * a_re
    xi = dt * a_im
    mag = jnp.exp(xr)
    lr = mag * jnp.cos(xi)
    li = mag * jnp.sin(xi)
    den = a_re * a_re + a_im * a_im
    cr = ((lr - 1.0) * a_re + li * a_im) / den
    ci = (li * a_re - (lr - 1.0) * a_im) / den
    bbr = cr[..., None] * b_re - ci[..., None] * b_im
    bbi = cr[..., None] * b_im + ci[..., None] * b_re
    kk = jnp.arange(T + 1, dtype=F32)[:, None, None, None]
    pm = jnp.exp(kk * xr)
    pw = jnp.stack([pm * jnp.cos(kk * xi), pm * jnp.sin(kk * xi)], axis=2)
    pw = pw.reshape(T + 1, 2, 2, N_PAIRS, 2 * P).transpose(3, 0, 1, 2, 4).reshape(N_PAIRS, T + 1, 8 * P)
    pw = jnp.concatenate([pw, jnp.zeros((N_PAIRS, PW_ROWS - (T + 1), 8 * P), F32)], axis=1)
    eye2 = jnp.eye(2, dtype=F32)

    def expand(t):
        t = t.transpose(2, 0, 1, 3, 4, 5)
        t = t[:, :, :, :, :, None, :] * eye2[None, None, None, :, None, :, None]
        return t.reshape(N_PAIRS, 4, 2 * H, 2 * P)

    bb = expand(jnp.stack([bbr, bbi], axis=1).reshape(2, 2, N_PAIRS, 2, P, H).transpose(0, 1, 2, 3, 5, 4))
    cc = expand(jnp.stack([c_re, c_im], axis=1).reshape(2, 2, N_PAIRS, 2, H, P))
    dd = d.reshape(N_PAIRS, 2 * H)[:, :, None] * jnp.eye(2 * H, dtype=F32)[None]
    dd = jnp.concatenate([dd, jnp.zeros((N_PAIRS, 2 * H, LANES - 2 * H), F32)], axis=2)
    return pw, bb, cc, dd


NA_TAB = 2 * NA_ROWS


def _na_table(rpb):
    qc = jnp.arange(GRID_W)[:, None, None]
    kc = (jnp.arange(2 * GRID_W) % GRID_W)[None, :, None]
    dc = jnp.arange(2 * NA_COLS - 1)[None, None, :]
    cstart = jnp.clip(qc - NA_COLS // 2, 0, GRID_W - NA_COLS)
    csel = ((kc >= cstart) & (kc < cstart + NA_COLS) & (kc - qc + NA_COLS - 1 == dc)).astype(F32)
    col_ok = jnp.sum(csel, axis=-1) > 0.5
    part = jnp.einsum('hrc,qmc->hrqm', rpb, csel, precision=HIGHEST)
    zero = jnp.zeros_like(part[:, :1])
    odd = jnp.arange(2 * GRID_W) >= GRID_W
    tab = jnp.where(odd, jnp.concatenate([part, zero], axis=1), jnp.concatenate([zero, part], axis=1))
    return jnp.where(col_ok, tab, NEG)


def _place():
    return lax.axis_index("x"), lax.axis_index("y"), lax.axis_index("c")


def _remote(src, dst, send_sem, recv_sem, device):
    return pltpu.make_async_remote_copy(src_ref=src, dst_ref=dst, send_sem=send_sem, recv_sem=recv_sem,
                                        device_id=device, device_id_type=MESH)


def _start_all(copies):
    for cp in copies:
        cp.start()


def _wait_all(copies):
    for cp in copies:
        cp.wait()


def _gather_copies(shard_refs, full_refs, send_sems, recv_sems, local_sems):
    x, y, c = _place()
    me = 4 * x + 2 * y + c
    out = []
    for t, (shard, full) in enumerate(zip(shard_refs, full_refs)):
        out.append(pltpu.make_async_copy(shard, full.at[me], local_sems.at[t]))
        for k in range(1, N_DEV):
            peer = (x ^ ((k >> 2) & 1), y ^ ((k >> 1) & 1), c ^ (k & 1))
            out.append(_remote(shard, full.at[me], send_sems.at[t, k - 1], recv_sems.at[t, k - 1], peer))
    return out


def _pair_copies(part_refs, land_refs, send_sems, recv_sems):
    x, y, c = _place()
    out = []
    for t, (part, land) in enumerate(zip(part_refs, land_refs)):
        for q in range(4):
            out.append(_remote(part.at[2 * q + (1 - c)], land.at[q], send_sems.at[t, q], recv_sems.at[t, q],
                               (x, y, 1 - c)))
    return out


def _chip_copies(part_refs, land_refs, send_sems, recv_sems, local_sems):
    x, y, c = _place()
    mine = 2 * x + y
    out = []
    for t, (part, land) in enumerate(zip(part_refs, land_refs)):
        out.append(pltpu.make_async_copy(part.at[mine], land.at[mine], local_sems.at[t]))
        for k in range(1, 4):
            px, py = x ^ (k >> 1), y ^ (k & 1)
            out.append(_remote(part.at[2 * px + py], land.at[mine], send_sems.at[t, k - 1], recv_sems.at[t, k - 1],
                               (px, py, c)))
    return out


def _alltoall_copies(part_refs, land_refs, send_sems, recv_sems, local_sems):
    x, y, c = _place()
    me = 4 * x + 2 * y + c
    out = []
    for t, (part, land) in enumerate(zip(part_refs, land_refs)):
        out.append(pltpu.make_async_copy(part.at[me], land.at[me], local_sems.at[t]))
        for k in range(1, N_DEV):
            px, py, pc = x ^ ((k >> 2) & 1), y ^ ((k >> 1) & 1), c ^ (k & 1)
            out.append(_remote(part.at[4 * px + 2 * py + pc], land.at[me], send_sems.at[t, k - 1],
                               recv_sems.at[t, k - 1], (px, py, pc)))
    return out


def _ride(copies, first, last):
    pl.when(first)(functools.partial(_start_all, copies))
    pl.when(last)(functools.partial(_wait_all, copies))


_ANY = lambda: pl.BlockSpec(memory_space=pl.ANY)


def _dma_sems(*shape):
    return pltpu.SemaphoreType.DMA(shape)


PAIR_W = 2 * SSM_GROUP
PAIRS_PER_BLOCK = LANES // PAIR_W
CHUNK_W = SSM_CHUNK * PAIR_W
TILE_CHUNKS = TOK_TILE // SSM_CHUNK


def _lane_window(lo, width):
    lane = lax.broadcasted_iota(jnp.int32, (1, LANES), 1)
    return (lane >= lo) & (lane < lo + width)


def _tile_to_pairs(src_ref, out_ref):
    for cl in range(D_SSM // LANES):
        xs = [src_ref[cl, pl.ds(s, TILE_CHUNKS, stride=SSM_CHUNK), :] for s in range(SSM_CHUNK)]
        for a_ in range(PAIRS_PER_BLOCK):
            for v in range(SSM_CHUNK // PAIRS_PER_BLOCK):
                acc = None
                for i in range(PAIRS_PER_BLOCK):
                    shift = (PAIR_W * (i - a_)) % LANES
                    piece = xs[PAIRS_PER_BLOCK * v + i]
                    piece = pltpu.roll(piece, shift, axis=1) if shift else piece
                    acc = piece if acc is None else jnp.where(_lane_window(PAIR_W * i, PAIR_W), piece, acc)
                out_ref[PAIRS_PER_BLOCK * cl + a_, :, LANES * v:LANES * (v + 1)] = acc.astype(out_ref.dtype)


def _tile_from_pairs(in_ref, dst_ref):
    for cl in range(D_SSM // LANES):
        for s in range(SSM_CHUNK):
            v, i = divmod(s, PAIRS_PER_BLOCK)
            acc = None
            for a_ in range(PAIRS_PER_BLOCK):
                shift = (PAIR_W * (a_ - i)) % LANES
                piece = in_ref[PAIRS_PER_BLOCK * cl + a_, :, LANES * v:LANES * (v + 1)].astype(F32)
                piece = pltpu.roll(piece, shift, axis=1) if shift else piece
                acc = piece if acc is None else jnp.where(_lane_window(PAIR_W * a_, PAIR_W), piece, acc)
            dst_ref[cl, pl.ds(s, TILE_CHUNKS, stride=SSM_CHUNK), :] = acc


def _pair_tile():
    return pl.BlockSpec((N_PAIRS, TILE_CHUNKS, CHUNK_W), lambda i: (0, i, 0))


def _tile_scratch():
    return pltpu.VMEM((D_SSM // LANES, TOK_TILE, LANES), F32)


def _fwd_in(x2, g_pre, w_in_g, shards=()):
    seq = x2.shape[0]
    ns = len(shards)
    steps = seq // TOK_TILE

    def body(x_ref, g_ref, w_ref, *rest):
        shard_refs, rest = rest[:ns], rest[ns:]
        (u_ref, zs_ref, qkv_ref, zn_ref, hn_ref), rest = rest[:5], rest[5:]
        u_scr, rest = rest[-1], rest[:-1]
        if ns:
            i = pl.program_id(0)
            _ride(_gather_copies(shard_refs, rest[:ns], *rest[ns:]), i == 0, i == steps - 1)
        x = x_ref[...]
        r = lax.rsqrt(jnp.mean(x * x, axis=-1, keepdims=True) + EPS)
        hn = (x * r * g_ref[...]).astype(BF16)
        hn_ref[...] = hn
        for j in range(N_DEV):
            pj = _mm(hn, w_ref[j])
            for i in range(SHARD_IN // LANES):
                blk = (SHARD_IN // LANES) * j + i
                piece = pj[:, i * LANES:(i + 1) * LANES]
                if blk < 4:
                    u_scr[blk] = piece
                elif blk < 8:
                    zs_ref[:, (blk - 4) * LANES:(blk - 3) * LANES] = piece
                elif blk < 20:
                    qkv_ref[:, (blk - 8) * LANES:(blk - 7) * LANES] = piece.astype(BF16)
                else:
                    zn_ref[:, (blk - 20) * LANES:(blk - 19) * LANES] = piece
        _tile_to_pairs(u_scr, u_ref)

    tok = lambda w: pl.BlockSpec((TOK_TILE, w), lambda i: (i, 0))
    out = pl.pallas_call(
        body, name="fwd_in", grid=(steps,),
        in_specs=[tok(D_MODEL), pl.BlockSpec((1, D_MODEL), lambda i: (0, 0)), _whole_vmem()] + [_ANY()] * ns,
        out_specs=[_pair_tile(), tok(D_SSM), tok(3 * D_NA), tok(D_NA), tok(D_MODEL)] + [_ANY()] * ns,
        out_shape=[jax.ShapeDtypeStruct((N_PAIRS, seq // SSM_CHUNK, CHUNK_W), BF16), jax.ShapeDtypeStruct((seq, D_SSM), F32),
                   jax.ShapeDtypeStruct((seq, 3 * D_NA), BF16), jax.ShapeDtypeStruct((seq, D_NA), F32),
                   jax.ShapeDtypeStruct((seq, D_MODEL), BF16)]
        + [jax.ShapeDtypeStruct((N_DEV,) + sh.shape, sh.dtype) for sh in shards],
        scratch_shapes=([_dma_sems(ns, N_DEV - 1), _dma_sems(ns, N_DEV - 1), _dma_sems(ns)] if ns else [])
        + [_tile_scratch()],
        compiler_params=_params("arbitrary"),
    )(x2, g_pre, w_in_g, *shards)
    return out[:5], list(out[5:])


def _mid(yssm, zs, o, zn, x2, p2, tgt, w_glu, b_glu, w_out, g_post, w_ple_g, g_ple, w_pg):
    seq = x2.shape[0]

    def body(yssm_ref, zs_ref, o_ref, zn_ref, x_ref, p_ref, tgt_ref, wglu_ref, bglu_ref, wout_ref, gpost_ref,
             wple_ref, gple_ref, wpg_ref,
             dyssm_ref, dzs_ref, do_ref, dzn_ref, dh1_ref, loss_ref, dwglu_ref, dbglu_ref, dwout_ref, dwple_ref,
             dwpg_ref, dgpost_ref, dgple_ref, y_scr, dy_scr):
        @pl.when(pl.program_id(0) == 0)
        def _():
            for ref in (loss_ref, dwglu_ref, dbglu_ref, dwout_ref, dwple_ref, dwpg_ref, dgpost_ref, dgple_ref):
                ref[...] = jnp.zeros_like(ref)

        _tile_from_pairs(yssm_ref, y_scr)
        yv = jnp.concatenate([y_scr[cl] for cl in range(D_SSM // LANES)], axis=1)
        g1 = _gelu(yv)
        g1b = g1.astype(BF16)
        sg = _sigmoid(_mm(g1b, wglu_ref[...]) + bglu_ref[...])
        zs_v = zs_ref[...]
        s_zs = _sigmoid(zs_v)
        g2 = g1 * sg
        zn_v = zn_ref[...]
        s_zn = _sigmoid(zn_v)
        o_v = o_ref[...]
        cat = jnp.concatenate([g2 * (zs_v * s_zs), o_v * (zn_v * s_zn)], axis=1).astype(BF16)
        mix = _mm(cat, wout_ref[...])
        r2 = lax.rsqrt(jnp.mean(mix * mix, axis=-1, keepdims=True) + EPS)
        n2 = mix * r2
        gpost = gpost_ref[...]
        h1 = x_ref[...] + n2 * gpost
        pb = p_ref[...].astype(BF16)
        epre = jnp.concatenate([_mm(pb, wple_ref[j]) for j in range(N_DEV)], axis=1)
        r3 = lax.rsqrt(jnp.mean(epre * epre, axis=-1, keepdims=True) + EPS)
        n3 = epre * r3
        gple = gple_ref[...]
        e = n3 * gple
        h1b = h1.astype(BF16)
        gate = _sigmoid(_mm(h1b, wpg_ref[...]))
        diff = h1 + gate * e - tgt_ref[...]
        loss_ref[...] += jnp.sum(diff * diff, axis=0, keepdims=True)

        dy = diff * (1.0 / D_MODEL)
        dgp = (dy * e * gate * (1.0 - gate)).astype(BF16)
        de = dy * gate
        dh1 = dy + _nt(dgp, wpg_ref[...])
        dh1_ref[...] = dh1
        dwpg_ref[...] += _tn(h1b, dgp)
        dgple_ref[...] += jnp.sum(de * n3, axis=0, keepdims=True)
        dn3 = de * gple
        depre = (r3 * (dn3 - n3 * jnp.mean(dn3 * n3, axis=-1, keepdims=True))).astype(BF16)
        for j in range(N_DEV):
            dwple_ref[j] += _tn(pb, depre[:, j * LANES:(j + 1) * LANES])
        dgpost_ref[...] += jnp.sum(dh1 * n2, axis=0, keepdims=True)
        dn2 = dh1 * gpost
        dmix = (r2 * (dn2 - n2 * jnp.mean(dn2 * n2, axis=-1, keepdims=True))).astype(BF16)
        dcat = _nt(dmix, wout_ref[...])
        dwout_ref[...] += _tn(cat, dmix)

        dys = dcat[:, :D_SSM]
        dyn = dcat[:, D_SSM:]
        dg2 = dys * (zs_v * s_zs)
        dzs_ref[...] = (dys * g2 * (s_zs * (1.0 + zs_v * (1.0 - s_zs)))).astype(BF16)
        dt = dg2 * g2 * (1.0 - sg)
        dtb = dt.astype(BF16)
        dg1 = dg2 * sg + _nt(dtb, wglu_ref[...])
        dwglu_ref[...] += _tn(g1b, dtb)
        dbglu_ref[...] += jnp.sum(dt, axis=0, keepdims=True)
        dyssm = dg1 * _gelu_grad(yv)
        for cl in range(D_SSM // LANES):
            dy_scr[cl] = dyssm[:, cl * LANES:(cl + 1) * LANES]
        _tile_to_pairs(dy_scr, dyssm_ref)
        do_ref[...] = (dyn * (zn_v * s_zn)).astype(BF16)
        dzn_ref[...] = (dyn * o_v * (s_zn * (1.0 + zn_v * (1.0 - s_zn)))).astype(BF16)

    tok = lambda w: pl.BlockSpec((TOK_TILE, w), lambda i: (i, 0))
    row = lambda w: pl.BlockSpec((1, w), lambda i: (0, 0))
    vm = _whole_vmem()
    half_b = jax.ShapeDtypeStruct((seq, D_SSM), BF16)
    gain = jax.ShapeDtypeStruct((1, D_MODEL), F32)
    return pl.pallas_call(
        body, name="mid", grid=(seq // TOK_TILE,),
        in_specs=[_pair_tile(), tok(D_SSM), tok(D_NA), tok(D_NA), tok(D_MODEL), tok(D_PLE), tok(D_MODEL),
                  vm, row(D_SSM), vm, row(D_MODEL), vm, row(D_MODEL), vm],
        out_specs=[_pair_tile(), tok(D_SSM), tok(D_NA), tok(D_NA), tok(D_MODEL), vm, vm, vm, vm, vm, vm, vm, vm],
        out_shape=[jax.ShapeDtypeStruct((N_PAIRS, seq // SSM_CHUNK, CHUNK_W), BF16), half_b, half_b, half_b, jax.ShapeDtypeStruct((seq, D_MODEL), F32), gain,
                   jax.ShapeDtypeStruct((D_SSM, D_SSM), F32), jax.ShapeDtypeStruct((1, D_SSM), F32),
                   jax.ShapeDtypeStruct((D_MODEL, D_MODEL), F32),
                   jax.ShapeDtypeStruct((N_DEV, D_PLE, LANES), F32),
                   jax.ShapeDtypeStruct((D_MODEL, D_MODEL), F32), gain, gain],
        scratch_shapes=[_tile_scratch(), _tile_scratch()],
        compiler_params=_params("arbitrary"),
    )(yssm, zs, o, zn, x2, p2, tgt, w_glu, b_glu, w_out, g_post, w_ple_g, g_ple, w_pg)


def _bwd_in(du, dzs, dq, dk, dv, dzn, hn, x2, dh1, g_pre, w_in_g):
    seq = x2.shape[0]

    def body(du_ref, dzs_ref, dq_ref, dk_ref, dv_ref, dzn_ref, hn_ref, x_ref, dh1_ref, g_ref, w_ref,
             dx_ref, dw_ref, dg_ref, dproj_ref, du_scr):
        @pl.when(pl.program_id(0) == 0)
        def _():
            dw_ref[...] = jnp.zeros_like(dw_ref)
            dg_ref[...] = jnp.zeros_like(dg_ref)

        _tile_from_pairs(du_ref, du_scr)
        for cl in range(D_SSM // LANES):
            dproj_ref[:, cl * LANES:(cl + 1) * LANES] = du_scr[cl].astype(BF16)
        for k, ref in enumerate((dzs_ref, dq_ref, dk_ref, dv_ref, dzn_ref), start=1):
            dproj_ref[:, k * D_SSM:(k + 1) * D_SSM] = ref[...].astype(BF16)
        hn = hn_ref[...]
        dhn = jnp.zeros((TOK_TILE, D_MODEL), F32)
        for j in range(N_DEV):
            dpj = dproj_ref[:, j * SHARD_IN:(j + 1) * SHARD_IN]
            dhn += _nt(dpj, w_ref[j])
            dw_ref[j] += _tn(hn, dpj)
        x = x_ref[...]
        r = lax.rsqrt(jnp.mean(x * x, axis=-1, keepdims=True) + EPS)
        n1 = x * r
        dg_ref[...] += jnp.sum(dhn * n1, axis=0, keepdims=True)
        dn1 = dhn * g_ref[...]
        dx_ref[...] = dh1_ref[...] + r * (dn1 - n1 * jnp.mean(dn1 * n1, axis=-1, keepdims=True))

    tok = lambda w: pl.BlockSpec((TOK_TILE, w), lambda i: (i, 0))
    vm = _whole_vmem()
    return pl.pallas_call(
        body, name="bwd_in", grid=(seq // TOK_TILE,),
        in_specs=[_pair_tile()] + [tok(D_SSM)] * 5 + [tok(D_MODEL), tok(D_MODEL), tok(D_MODEL),
                                      pl.BlockSpec((1, D_MODEL), lambda i: (0, 0)), vm],
        out_specs=[tok(D_MODEL), vm, vm],
        out_shape=[jax.ShapeDtypeStruct((seq, D_MODEL), F32),
                   jax.ShapeDtypeStruct((N_DEV, D_MODEL, SHARD_IN), F32),
                   jax.ShapeDtypeStruct((1, D_MODEL), F32)],
        scratch_shapes=[pltpu.VMEM((TOK_TILE, D_IN_PROJ), BF16), _tile_scratch()],
        compiler_params=_params("arbitrary"),
    )(du, dzs, dq, dk, dv, dzn, hn, x2, dh1, g_pre, w_in_g)


SSM_PAIRS = 4
SLOT = 8


def _boundary_scan(nc, pw_ref, buf_ref, conj):
    pad = [jnp.zeros((SLOT - SSM_PAIRS, LANES), F32)]
    lr0, li0, lr1, li1 = (jnp.concatenate([_pw_row(pw_ref, j, SSM_CHUNK, q) for j in range(SSM_PAIRS)] + pad, axis=0)
                          for q in range(4))
    if conj:
        li0, li1 = -li0, -li1

    def step(c, carry):
        hr0, hi0, hr1, hi1 = carry
        up = pl.ds(pl.multiple_of(c * SLOT, SLOT), SLOT)
        dn = pl.ds(pl.multiple_of((nc - 1 - c) * SLOT, SLOT), SLOT)
        ra, rb = (dn, up) if conj else (up, dn)
        s_r0, s_i0 = buf_ref[0, ra, :], buf_ref[1, ra, :]
        s_r1, s_i1 = buf_ref[2, rb, :], buf_ref[3, rb, :]
        buf_ref[0, ra, :] = hr0
        buf_ref[1, ra, :] = hi0
        buf_ref[2, rb, :] = hr1
        buf_ref[3, rb, :] = hi1
        return (lr0 * hr0 - li0 * hi0 + s_r0, lr0 * hi0 + li0 * hr0 + s_i0,
                lr1 * hr1 - li1 * hi1 + s_r1, lr1 * hi1 + li1 * hr1 + s_i1)

    z = jnp.zeros((SLOT, LANES), F32)
    lax.fori_loop(0, nc, step, (z, z, z, z), unroll=8)


def _put_pair(buf_ref, j, nc, val):
    for q in range(4):
        buf_ref[q, pl.ds(j, nc, stride=SLOT), :] = val[:, LANES * q:LANES * (q + 1)]


def _get_pair(buf_ref, j, nc):
    return jnp.concatenate([buf_ref[q, pl.ds(j, nc, stride=SLOT), :] for q in range(4)], axis=1)


def _pw_row(pw_ref, j, k, q):
    return pw_ref[j, k:k + 1, LANES * q:LANES * (q + 1)]


def _mm_f32(a, b, dims):
    return lax.dot_general(a, b, (dims, ((), ())), precision=HIGHEST, preferred_element_type=F32)


_POW_M = (lambda s: SSM_CHUNK - 1 - s, lambda s: s)
_POW_C = (lambda s: s + 1, lambda s: SSM_CHUNK - s)
_POW_K = (lambda s: s, lambda s: SSM_CHUNK - 1 - s)


def _chunk_matrices(j, pw_ref, bb_ref, cc_ref, dd_ref, m_scr, ct_scr, toep_scr, g_scr, kt_scr):
    blk = lambda s: slice(PAIR_W * s, PAIR_W * (s + 1))
    col = lambda q: slice(LANES * q, LANES * (q + 1))
    for d in range(2):
        bbr, bbi = bb_ref[j, 2 * d], bb_ref[j, 2 * d + 1]
        ccr, cci = cc_ref[j, 2 * d], cc_ref[j, 2 * d + 1]
        for s in range(SSM_CHUNK):
            pr, pi = _pw_row(pw_ref, j, _POW_M[d](s), 2 * d), _pw_row(pw_ref, j, _POW_M[d](s), 2 * d + 1)
            m_scr[j, blk(s), col(2 * d)] = (pr * bbr - pi * bbi).astype(m_scr.dtype)
            m_scr[j, blk(s), col(2 * d + 1)] = (pr * bbi + pi * bbr).astype(m_scr.dtype)
            pr, pi = _pw_row(pw_ref, j, _POW_C[d](s), 2 * d), _pw_row(pw_ref, j, _POW_C[d](s), 2 * d + 1)
            ct_scr[j, blk(s), col(2 * d)] = (ccr * pr - cci * pi).astype(ct_scr.dtype)
            ct_scr[j, blk(s), col(2 * d + 1)] = (-(ccr * pi + cci * pr)).astype(ct_scr.dtype)
            pr, pi = _pw_row(pw_ref, j, _POW_K[d](s), 2 * d), _pw_row(pw_ref, j, _POW_K[d](s), 2 * d + 1)
            g_scr[j, d, blk(s), 0:LANES] = ccr * pr - cci * pi
            g_scr[j, d, blk(s), LANES:2 * LANES] = -(ccr * pi + cci * pr)
        kt = _mm_f32(jnp.concatenate([bbr, bbi], axis=1), g_scr[j, d], ((1,), (1,)))
        if d == 0:
            kt = jnp.concatenate([kt[:, 0:LANES] + dd_ref[j], kt[:, LANES:]], axis=1)
        kt_scr[d] = kt
    lane = lax.broadcasted_iota(jnp.int32, (1, CHUNK_W), 1)
    for s in range(SSM_CHUNK):
        lo = PAIR_W * s
        hi = PAIR_W * (s + 1)
        fwd = kt_scr[0] if s == 0 else pltpu.roll(kt_scr[0], lo, axis=1)
        bwd = kt_scr[1] if hi == CHUNK_W else pltpu.roll(kt_scr[1], hi, axis=1)
        row = jnp.where(lane >= lo, fwd, 0.0) + jnp.where(lane < hi, bwd, 0.0)
        toep_scr[j, blk(s), :] = row.astype(toep_scr.dtype)


def _ssm_scratch(nc, mat_dtype, buffers):
    mats = [pltpu.VMEM((SSM_PAIRS, CHUNK_W, CHUNK_W), mat_dtype) for _ in range(3)]
    return mats + [pltpu.VMEM((SSM_PAIRS, 2, CHUNK_W, 2 * LANES), F32), pltpu.VMEM((2, PAIR_W, CHUNK_W), F32)] + [
        pltpu.VMEM((4, nc * SLOT, LANES), F32) for _ in range(buffers)]


def _per_step(*shape):
    return pl.BlockSpec((SSM_PAIRS,) + shape, lambda g: (g,) + (0,) * len(shape))


_TABLE_SPECS = lambda: [_per_step(PW_ROWS, CHUNK_W), _per_step(4, PAIR_W, LANES), _per_step(4, PAIR_W, LANES),
                        _per_step(PAIR_W, LANES)]


def _ssm_fwd(u, pw, bb, cc, dd):
    npair, nc, width = u.shape

    def body(u_ref, pw_ref, bb_ref, cc_ref, dd_ref, y_ref, m_scr, ct_scr, toep_scr, g_scr, kt_scr, h_scr):
        @pl.when(pl.program_id(0) == 0)
        def _():
            h_scr[...] = jnp.zeros_like(h_scr)

        for j in range(SSM_PAIRS):
            _chunk_matrices(j, pw_ref, bb_ref, cc_ref, dd_ref, m_scr, ct_scr, toep_scr, g_scr, kt_scr)
            _put_pair(h_scr, j, nc, _mm(u_ref[j], m_scr[j]))
        _boundary_scan(nc, pw_ref, h_scr, conj=False)
        for j in range(SSM_PAIRS):
            hin = _get_pair(h_scr, j, nc)
            y_ref[j] = _mm(u_ref[j], toep_scr[j]) + _nt(hin.astype(u.dtype), ct_scr[j])

    return pl.pallas_call(
        body, name="ssm_fwd", grid=(npair // SSM_PAIRS,),
        in_specs=[_per_step(nc, width)] + _TABLE_SPECS(),
        out_specs=_per_step(nc, width),
        out_shape=jax.ShapeDtypeStruct((npair, nc, width), F32),
        scratch_shapes=_ssm_scratch(nc, u.dtype, 1),
        compiler_params=_params("arbitrary"),
    )(u, pw, bb, cc, dd)


def _ssm_bwd(u, dy, pw, bb, cc, dd, chip_parts=()):
    npair, nc, width = u.shape
    ns = len(chip_parts)
    steps = npair // SSM_PAIRS

    def body(u_ref, dy_ref, pw_ref, bb_ref, cc_ref, dd_ref, *rest):
        part_refs, rest = rest[:ns], rest[ns:]
        (du_ref, dpw_ref, dbb_ref, dcc_ref, ddd_ref), rest = rest[:5], rest[5:]
        land_refs, rest = rest[:ns], rest[ns:]
        m_scr, ct_scr, toep_scr, g_scr, kt_scr, h_scr, d_scr = rest[:7]
        g = pl.program_id(0)
        if ns:
            _ride(_chip_copies(part_refs, land_refs, *rest[7:]), g == 0, g == steps - 1)

        @pl.when(g == 0)
        def _():
            h_scr[...] = jnp.zeros_like(h_scr)
            d_scr[...] = jnp.zeros_like(d_scr)

        for j in range(SSM_PAIRS):
            _chunk_matrices(j, pw_ref, bb_ref, cc_ref, dd_ref, m_scr, ct_scr, toep_scr, g_scr, kt_scr)
            _put_pair(h_scr, j, nc, _mm(u_ref[j], m_scr[j]))
            _put_pair(d_scr, j, nc, _mm(dy_ref[j], ct_scr[j]))
        _boundary_scan(nc, pw_ref, h_scr, conj=False)
        _boundary_scan(nc, pw_ref, d_scr, conj=True)

        dpw_ref[...] = jnp.zeros_like(dpw_ref)
        blk = lambda s: slice(PAIR_W * s, PAIR_W * (s + 1))
        col = lambda q: slice(LANES * q, LANES * (q + 1))
        lane = lax.broadcasted_iota(jnp.int32, (1, CHUNK_W), 1)
        for j in range(SSM_PAIRS):
            uv = u_ref[j]
            dyb = dy_ref[j]
            hin = _get_pair(h_scr, j, nc)
            ds = _get_pair(d_scr, j, nc)
            dsb = ds.astype(u.dtype)
            du_ref[j] = _nt(dsb, m_scr[j]) + _nt(dyb, toep_scr[j])
            dm = _tn(uv, dsb)
            dct = _tn(dyb, hin.astype(u.dtype))
            dtoep = _tn(uv, dyb)

            def add_pw(k, q, val):
                dpw_ref[j, k:k + 1, col(q)] += jnp.sum(val, axis=0, keepdims=True)

            for d in range(2):
                g_r, g_i = ds[:, col(2 * d)], ds[:, col(2 * d + 1)]
                h_r, h_i = hin[:, col(2 * d)], hin[:, col(2 * d + 1)]
                add_pw(SSM_CHUNK, 2 * d, g_r * h_r + g_i * h_i)
                add_pw(SSM_CHUNK, 2 * d + 1, g_i * h_r - g_r * h_i)

            dkt0 = jnp.zeros((PAIR_W, CHUNK_W), F32)
            dkt1 = jnp.zeros((PAIR_W, CHUNK_W), F32)
            for s in range(SSM_CHUNK):
                lo = PAIR_W * s
                hi = PAIR_W * (s + 1)
                row = dtoep[blk(s), :]
                fwd = jnp.where(lane >= lo, row, 0.0)
                bwd = jnp.where(lane < hi, row, 0.0)
                dkt0 += fwd if s == 0 else pltpu.roll(fwd, CHUNK_W - lo, axis=1)
                dkt1 += bwd if hi == CHUNK_W else pltpu.roll(bwd, CHUNK_W - hi, axis=1)
            ddd_ref[j] = dkt0[:, 0:LANES]

            for d, dkt in enumerate((dkt0, dkt1)):
                bbr, bbi = bb_ref[j, 2 * d], bb_ref[j, 2 * d + 1]
                ccr, cci = cc_ref[j, 2 * d], cc_ref[j, 2 * d + 1]
                dbbcat = _mm_f32(dkt, g_scr[j, d], ((1,), (0,)))
                dg = _mm_f32(dkt, jnp.concatenate([bbr, bbi], axis=1), ((0,), (0,)))
                dbbr, dbbi = dbbcat[:, 0:LANES], dbbcat[:, LANES:]
                dccr = jnp.zeros((PAIR_W, LANES), F32)
                dcci = jnp.zeros((PAIR_W, LANES), F32)
                for s in range(SSM_CHUNK):
                    k = _POW_M[d](s)
                    pr, pi = _pw_row(pw_ref, j, k, 2 * d), _pw_row(pw_ref, j, k, 2 * d + 1)
                    gr, gi = dm[blk(s), col(2 * d)], dm[blk(s), col(2 * d + 1)]
                    dbbr += gr * pr + gi * pi
                    dbbi += gi * pr - gr * pi
                    add_pw(k, 2 * d, gr * bbr + gi * bbi)
                    add_pw(k, 2 * d + 1, gi * bbr - gr * bbi)
                    for k, gr, gi in ((_POW_C[d](s), dct[blk(s), col(2 * d)], dct[blk(s), col(2 * d + 1)]),
                                      (_POW_K[d](s), dg[blk(s), 0:LANES], dg[blk(s), LANES:])):
                        pr, pi = _pw_row(pw_ref, j, k, 2 * d), _pw_row(pw_ref, j, k, 2 * d + 1)
                        dccr += gr * pr - gi * pi
                        dcci += -(gr * pi + gi * pr)
                        add_pw(k, 2 * d, gr * ccr - gi * cci)
                        add_pw(k, 2 * d + 1, -(gr * cci + gi * ccr))
                dbb_ref[j, 2 * d] = dbbr
                dbb_ref[j, 2 * d + 1] = dbbi
                dcc_ref[j, 2 * d] = dccr
                dcc_ref[j, 2 * d + 1] = dcci

    out = pl.pallas_call(
        body, name="ssm_bwd", grid=(steps,),
        in_specs=[_per_step(nc, width), _per_step(nc, width)] + _TABLE_SPECS() + [_ANY()] * ns,
        out_specs=[_per_step(nc, width)] + _TABLE_SPECS() + [_ANY()] * ns,
        out_shape=[jax.ShapeDtypeStruct((npair, nc, width), F32), jax.ShapeDtypeStruct(pw.shape, F32),
                   jax.ShapeDtypeStruct(bb.shape, F32), jax.ShapeDtypeStruct(cc.shape, F32),
                   jax.ShapeDtypeStruct(dd.shape, F32)]
        + [jax.ShapeDtypeStruct(pt.shape, pt.dtype) for pt in chip_parts],
        scratch_shapes=_ssm_scratch(nc, u.dtype, 2) + ([_dma_sems(ns, 3), _dma_sems(ns, 3), _dma_sems(ns)] if ns else []),
        compiler_params=_params("arbitrary"),
    )(u, dy, pw, bb, cc, dd, *chip_parts)
    return out[:5], list(out[5:])


NA_Q = NA_QROWS * GRID_W
NA_K = NA_KROWS * GRID_W
NA_SCALE = NA_HEAD_DIM ** -0.5


def _na_block(b, nb, rows):
    start = jnp.clip(NA_QROWS * b - NA_ROWS // 2, 0, rows - NA_KROWS) * GRID_W
    kind = jnp.where(b == 0, 0, jnp.where(b == nb - 1, 2, 1))
    return pl.multiple_of(start, GRID_W), kind


NA_CHUNK = 16


def _na_pieces(kind, i):
    ri, q0 = divmod(i * NA_CHUNK, GRID_W)
    off = (NA_ROWS - 1, NA_ROWS // 2 - 1, -1)[kind]
    lo = (0, ri, NA_KROWS - NA_ROWS)[kind]
    modes = {(True, True): 'both', (True, False): 'even', (False, True): 'odd', (False, False): None}
    out = []
    for k2 in range(NA_KROWS // 2):
        inside = tuple(lo <= kr < lo + NA_ROWS for kr in (2 * k2, 2 * k2 + 1))
        out.append((2 * k2 - ri + off + 1, slice(q0, q0 + NA_CHUNK), modes[inside]))
    return out


def _na_softmax_pieces(s_ref, tab_ref, hh, kind, i):
    rows = slice(i * NA_CHUNK, (i + 1) * NA_CHUNK)
    lane = lax.broadcasted_iota(jnp.int32, (1, LANES), 1)
    xs = []
    for k2, (t, q, mode) in enumerate(_na_pieces(kind, i)):
        if mode is None:
            xs.append(None)
            continue
        bias = tab_ref[hh, t, q, :]
        if mode == 'even':
            bias = jnp.where(lane < GRID_W, bias, NEG)
        elif mode == 'odd':
            bias = jnp.where(lane >= GRID_W, bias, NEG)
        xs.append(s_ref[hh, rows, k2 * LANES:(k2 + 1) * LANES] + bias)
    live = [x for x in xs if x is not None]
    m = jnp.max(functools.reduce(jnp.maximum, live), axis=-1, keepdims=True)
    es = [None if x is None else jnp.exp(x - m) for x in xs]
    total = jnp.sum(functools.reduce(jnp.add, [e for e in es if e is not None]), axis=-1, keepdims=True)
    inv = 1.0 / total
    return [None if e is None else e * inv for e in es]


def _na_heads():
    lane = lax.broadcasted_iota(jnp.int32, (1, LANES), 1)
    return [lane < NA_HEAD_DIM, lane >= NA_HEAD_DIM]


def _na_fwd(qkv, bias):
    seq = qkv.shape[0]
    rows = seq // GRID_W
    nb = rows // NA_QROWS

    def body(q_ref, k_ref, v_ref, bias_ref, o_ref, p_ref, s_scr):
        start, kind = _na_block(pl.program_id(1), nb, rows)

        def block(static_kind):
            q2 = q_ref[...] * NA_SCALE
            kw = k_ref[pl.ds(start, NA_K), :]
            vw = v_ref[pl.ds(start, NA_K), :]
            heads = _na_heads()
            for hh in range(2):
                s_scr[hh] = _nt(jnp.where(heads[hh], q2, jnp.zeros_like(q2)), kw)
            for hh in range(2):
                for i in range(NA_Q // NA_CHUNK):
                    r = slice(i * NA_CHUNK, (i + 1) * NA_CHUNK)
                    for k2, p in enumerate(_na_softmax_pieces(s_scr, bias_ref, hh, static_kind, i)):
                        p = jnp.zeros((NA_CHUNK, LANES), F32) if p is None else p
                        p_ref[hh, r, k2 * LANES:(k2 + 1) * LANES] = p.astype(p_ref.dtype)
            o_ref[...] = jnp.where(heads[0], _mm(p_ref[0], vw), _mm(p_ref[1], vw))

        for static_kind in range(3):
            pl.when(kind == static_kind)(functools.partial(block, static_kind))

    return pl.pallas_call(
        body, name="na_fwd", grid=(NA_HEADS // 2, nb),
        in_specs=[pl.BlockSpec((NA_Q, LANES), lambda hp, b: (b, hp)),
                  pl.BlockSpec((seq, LANES), lambda hp, b: (0, 4 + hp)),
                  pl.BlockSpec((seq, LANES), lambda hp, b: (0, 8 + hp)),
                  pl.BlockSpec((2, NA_TAB, GRID_W, LANES), lambda hp, b: (hp, 0, 0, 0))],
        out_specs=[pl.BlockSpec((NA_Q, LANES), lambda hp, b: (b, hp)),
                   pl.BlockSpec((2, NA_Q, NA_K), lambda hp, b: (hp, b, 0))],
        out_shape=[jax.ShapeDtypeStruct((seq, D_NA), F32), jax.ShapeDtypeStruct((NA_HEADS, seq, NA_K), qkv.dtype)],
        scratch_shapes=[pltpu.VMEM((2, NA_Q, NA_K), F32)],
        compiler_params=_params("arbitrary", "arbitrary"),
    )(qkv, qkv, qkv, bias)


def _na_bwd(qkv, do, probs, parts=()):
    seq = qkv.shape[0]
    rows = seq // GRID_W
    nb = rows // NA_QROWS
    ns = len(parts)

    def body(q_ref, k_ref, v_ref, do_ref, p_ref, *rest):
        part_refs, rest = rest[:ns], rest[ns:]
        (dq_ref, dk_ref, dv_ref, dbias_ref), rest = rest[:4], rest[4:]
        land_refs, rest = rest[:ns], rest[ns:]
        dp_scr, ds_scr = rest[:2]
        b = pl.program_id(1)
        start, kind = _na_block(b, nb, rows)
        if ns:
            hp = pl.program_id(0)
            _ride(_pair_copies(part_refs, land_refs, *rest[2:]), (hp == 0) & (b == 0),
                  (hp == NA_HEADS // 2 - 1) & (b == nb - 1))

        @pl.when(b == 0)
        def _():
            dk_ref[...] = jnp.zeros_like(dk_ref)
            dv_ref[...] = jnp.zeros_like(dv_ref)
            dbias_ref[...] = jnp.zeros_like(dbias_ref)

        def block(static_kind):
            q2 = q_ref[...] * NA_SCALE
            kw = k_ref[pl.ds(start, NA_K), :]
            vw = v_ref[pl.ds(start, NA_K), :]
            do2 = do_ref[...].astype(q2.dtype)
            heads = _na_heads()
            col = lambda k2: slice(k2 * LANES, (k2 + 1) * LANES)
            zero = jnp.zeros((NA_CHUNK, LANES), ds_scr.dtype)
            for hh in range(2):
                dp_scr[hh] = _nt(jnp.where(heads[hh], do2, jnp.zeros_like(do2)), vw)
            for hh in range(2):
                for i in range(NA_Q // NA_CHUNK):
                    r = slice(i * NA_CHUNK, (i + 1) * NA_CHUNK)
                    pieces = _na_pieces(static_kind, i)
                    ps = [None if mode is None else p_ref[hh, r, col(k2)].astype(F32)
                          for k2, (_, _, mode) in enumerate(pieces)]
                    dps = [None if p is None else dp_scr[hh, r, col(k2)] for k2, p in enumerate(ps)]
                    pdp = functools.reduce(jnp.add, [p * dp for p, dp in zip(ps, dps) if p is not None])
                    rowsum = jnp.sum(pdp, axis=-1, keepdims=True)
                    for k2, (t, q, _) in enumerate(pieces):
                        if ps[k2] is None:
                            ds_scr[hh, r, col(k2)] = zero
                            continue
                        ds = ps[k2] * (dps[k2] - rowsum)
                        dbias_ref[hh, t, q, :] += ds
                        ds_scr[hh, r, col(k2)] = ds.astype(ds_scr.dtype)
            dq_ref[...] = jnp.where(heads[0], _mm(ds_scr[0], kw), _mm(ds_scr[1], kw)) * NA_SCALE
            dk_ref[pl.ds(start, NA_K), :] += jnp.where(heads[0], _tn(ds_scr[0], q2), _tn(ds_scr[1], q2))
            dv_ref[pl.ds(start, NA_K), :] += jnp.where(heads[0], _tn(p_ref[0], do2), _tn(p_ref[1], do2))

        for static_kind in range(3):
            pl.when(kind == static_kind)(functools.partial(block, static_kind))

    out = pl.pallas_call(
        body, name="na_bwd", grid=(NA_HEADS // 2, nb),
        in_specs=[pl.BlockSpec((NA_Q, LANES), lambda hp, b: (b, hp)),
                  pl.BlockSpec((seq, LANES), lambda hp, b: (0, 4 + hp)),
                  pl.BlockSpec((seq, LANES), lambda hp, b: (0, 8 + hp)),
                  pl.BlockSpec((NA_Q, LANES), lambda hp, b: (b, hp)),
                  pl.BlockSpec((2, NA_Q, NA_K), lambda hp, b: (hp, b, 0))] + [_ANY()] * ns,
        out_specs=[pl.BlockSpec((NA_Q, LANES), lambda hp, b: (b, hp)),
                   pl.BlockSpec((seq, LANES), lambda hp, b: (0, hp)),
                   pl.BlockSpec((seq, LANES), lambda hp, b: (0, hp)),
                   pl.BlockSpec((2, NA_TAB, GRID_W, LANES), lambda hp, b: (hp, 0, 0, 0))] + [_ANY()] * ns,
        out_shape=[jax.ShapeDtypeStruct((seq, D_NA), F32), jax.ShapeDtypeStruct((seq, D_NA), F32),
                   jax.ShapeDtypeStruct((seq, D_NA), F32),
                   jax.ShapeDtypeStruct((NA_HEADS, NA_TAB, GRID_W, LANES), F32)]
        + [jax.ShapeDtypeStruct((4,) + pt.shape[1:], pt.dtype) for pt in parts],
        scratch_shapes=[pltpu.VMEM((2, NA_Q, NA_K), F32), pltpu.VMEM((2, NA_Q, NA_K), qkv.dtype)]
        + ([_dma_sems(ns, 4), _dma_sems(ns, 4)] if ns else []),
        compiler_params=_params("arbitrary", "arbitrary"),
    )(qkv, qkv, qkv, do, probs, *parts)
    return out[:4], list(out[4:])


def _local_step(x2, p2, tgt, g_pre, g_post, w_in_g, ssm, w_glu, b_glu, rpb, w_out, w_ple, g_ple, w_pg,
                distributed=False):
    (pw, bb, cc, dd), ssm_vjp = jax.vjp(_ssm_tables, *ssm)
    bias, bias_vjp = jax.vjp(_na_table, rpb)

    riders = (w_glu, w_out, w_ple, w_pg) if distributed else ()
    (u_p, zs, qkv, zn, hn), gathered = _fwd_in(x2, g_pre, w_in_g, riders)
    if distributed:
        w_glu, w_out, w_ple, w_pg = gathered
        w_glu = w_glu.reshape(D_SSM, D_SSM)
        w_out = w_out.reshape(D_MODEL, D_MODEL)
        w_pg = w_pg.reshape(D_MODEL, D_MODEL)
    yssm = _ssm_fwd(u_p, pw, bb, cc, dd)
    o, probs = _na_fwd(qkv, bias)
    dyssm, dzs, do, dzn, dh1, sq, d_wglu, d_bglu, d_wout, d_wple, d_wpg, d_gpost, d_gple = _mid(
        yssm, zs, o, zn, x2, p2, tgt, w_glu, b_glu, w_out, g_post, w_ple, g_ple, w_pg)
    mid_grads = [d_wglu.reshape(N_DEV, D_SSM // N_DEV, D_SSM), d_wout.reshape(N_DEV, D_MODEL // N_DEV, D_MODEL),
                 d_wple, d_wpg.reshape(N_DEV, D_MODEL // N_DEV, D_MODEL)]
    (dq, dk, dv, dbias), lands = _na_bwd(qkv, do, probs, mid_grads if distributed else ())
    (d_rpb,) = bias_vjp(dbias)
    chip_parts = _pair_sum(mid_grads, lands)[0] if distributed else ()
    (du_p, dpw, dbb, dcc, ddd), chip_lands = _ssm_bwd(u_p, dyssm, pw, bb, cc, dd, chip_parts)
    d_ssm = ssm_vjp((dpw, dbb, dcc, ddd))
    dx, d_win, d_gpre = _bwd_in(du_p, dzs, dq, dk, dv, dzn, hn, x2, dh1, g_pre, w_in_g)
    if distributed:
        d_wglu, d_wout, d_wple, d_wpg = chip_lands
    return sq, dx, dict(norm_pre=d_gpre, norm_post=d_gpost, w_in=d_win, ssm=d_ssm, w_glu=d_wglu, b_glu=d_bglu,
                        na_rpb=d_rpb, w_out=d_wout, w_ple=d_wple, ple_norm=d_gple, w_ple_gate=d_wpg)


def _all_gather(shard):
    m_per, n = shard.shape

    def body(x_ref, out_ref, send_sems, recv_sems, local_sem):
        x, y, c = _place()
        me, sibling = (x, y, c), (x, y, 1 - c)
        chips = [(1 - x, y), (x, 1 - y), (1 - x, 1 - y)]

        def rows(px, py, pc):
            return out_ref.at[pl.ds((4 * px + 2 * py + pc) * m_per, m_per), :]

        def copy(k, block, to, src=None):
            return pltpu.make_async_remote_copy(
                src_ref=rows(*block) if src is None else src, dst_ref=rows(*block),
                send_sem=send_sems.at[k], recv_sem=recv_sems.at[k], device_id=to, device_id_type=MESH)

        mine = pltpu.make_async_copy(x_ref, rows(*me), local_sem)
        mine.start()
        first = [copy(0, me, sibling, src=x_ref)]
        first += [copy(1 + j, me, (*chip, c), src=x_ref) for j, chip in enumerate(chips)]
        for cp in first:
            cp.start()
        passed = [copy(4 + j, (*chip, c), sibling) for j, chip in enumerate(chips)]
        for j, chip in enumerate(chips):
            copy(1 + j, (*chip, c), me).wait_recv()
            passed[j].start()
        copy(0, sibling, me).wait_recv()
        for j, chip in enumerate(chips):
            copy(4 + j, (*chip, 1 - c), me).wait_recv()
        for cp in first + passed:
            cp.wait_send()
        mine.wait()

    return pl.pallas_call(
        body, name="all_gather",
        out_shape=jax.ShapeDtypeStruct((N_DEV * m_per, n), shard.dtype),
        in_specs=[_whole_vmem()], out_specs=_whole_vmem(),
        scratch_shapes=[pltpu.SemaphoreType.DMA((7,)), pltpu.SemaphoreType.DMA((7,)), pltpu.SemaphoreType.DMA],
        compiler_params=pltpu.CompilerParams(vmem_limit_bytes=VMEM_LIMIT),
    )(shard)


def _exchange_first(pair_parts, all_parts):
    n1, n2 = len(pair_parts), len(all_parts)

    def body(*refs):
        ins, refs = refs[:n1 + n2], refs[n1 + n2:]
        outs, sems = refs[:n1 + n2], refs[n1 + n2:]
        copies = (_pair_copies(ins[:n1], outs[:n1], *sems[:2])
                  + _alltoall_copies(ins[n1:], outs[n1:], *sems[2:]))
        _start_all(copies)
        _wait_all(copies)

    out = pl.pallas_call(
        body, name="exchange_first",
        out_shape=[jax.ShapeDtypeStruct((4,) + pt.shape[1:], pt.dtype) for pt in pair_parts]
        + [jax.ShapeDtypeStruct(pt.shape, pt.dtype) for pt in all_parts],
        in_specs=[_ANY()] * (n1 + n2), out_specs=[_ANY()] * (n1 + n2),
        scratch_shapes=[_dma_sems(n1, 4), _dma_sems(n1, 4),
                        _dma_sems(n2, N_DEV - 1), _dma_sems(n2, N_DEV - 1), _dma_sems(n2)],
    )(*pair_parts, *all_parts)
    return list(out[:n1]), list(out[n1:])


def _pair_sum(parts, lands, all_lands=()):
    ns, na = len(parts), len(all_lands)

    def body(*refs):
        c = lax.axis_index("c")
        ins, outs = refs[:2 * ns + na], refs[2 * ns + na:]
        for part_ref, land_ref, out_ref in zip(ins[:ns], ins[ns:2 * ns], outs[:ns]):
            for q in range(4):
                out_ref[q] = (part_ref[q, c] + land_ref[q]).astype(BF16)
        for land_ref, out_ref in zip(ins[2 * ns:], outs[ns:]):
            acc = land_ref[0]
            for j in range(1, N_DEV):
                acc = acc + land_ref[j]
            out_ref[...] = acc

    out = pl.pallas_call(
        body, name="pair_sum",
        in_specs=[_whole_vmem()] * (2 * ns + na), out_specs=[_whole_vmem()] * (ns + na),
        out_shape=[jax.ShapeDtypeStruct(ld.shape, BF16) for ld in lands]
        + [jax.ShapeDtypeStruct(ld.shape[1:], ld.dtype) for ld in all_lands],
        compiler_params=pltpu.CompilerParams(vmem_limit_bytes=VMEM_LIMIT),
    )(*[pt.reshape((4, 2) + pt.shape[1:]) for pt in parts], *lands, *all_lands)
    return list(out[:ns]), list(out[ns:])


def _exchange_second(chip_parts, shards):
    n1, n2 = len(chip_parts), len(shards)

    def body(*refs):
        ins, refs = refs[:n1 + n2], refs[n1 + n2:]
        outs, sems = refs[:n1 + n2], refs[n1 + n2:]
        copies = (_chip_copies(ins[:n1], outs[:n1], *sems[:3])
                  + _gather_copies(ins[n1:], outs[n1:], *sems[3:]))
        _start_all(copies)
        _wait_all(copies)

    out = pl.pallas_call(
        body, name="exchange_second",
        out_shape=[jax.ShapeDtypeStruct(pt.shape, pt.dtype) for pt in chip_parts]
        + [jax.ShapeDtypeStruct((N_DEV,) + sh.shape, sh.dtype) for sh in shards],
        in_specs=[_ANY()] * (n1 + n2), out_specs=[_ANY()] * (n1 + n2),
        scratch_shapes=[_dma_sems(n1, 3), _dma_sems(n1, 3), _dma_sems(n1),
                        _dma_sems(n2, N_DEV - 1), _dma_sems(n2, N_DEV - 1), _dma_sems(n2)],
    )(*chip_parts, *shards)
    return list(out[:n1]), list(out[n1:])


ADAM_ROWS = 256


def _adamw(ws, gs, ms, vs, slab=None):
    n = len(ws)
    arrays = [g for g in gs if not isinstance(g, int)]
    n_in = 3 * n + len(arrays) + (slab is not None)

    def body(*refs):
        ins, outs = refs[:n_in], refs[n_in:]
        w_refs, m_refs, v_refs = ins[:n], ins[n:2 * n], ins[2 * n:3 * n]
        g_refs = iter(ins[3 * n:3 * n + len(arrays)])
        for t in range(n):
            w_ref, m_ref, v_ref = w_refs[t], m_refs[t], v_refs[t]
            go_ref, d_ref, nm_ref, nv_ref = outs[4 * t:4 * t + 4]
            g_ref = None if isinstance(gs[t], int) else next(g_refs)
            rows = w_ref.shape[0]
            for lo in range(0, rows, ADAM_ROWS):
                hi = min(lo + ADAM_ROWS, rows)
                r = slice(lo, hi)
                if g_ref is None:
                    g = ins[-1][gs[t] + lo:gs[t] + hi, :]
                elif len(g_ref.shape) == 3:
                    g = g_ref[0, r, :].astype(F32)
                    for j in range(1, g_ref.shape[0]):
                        g = g + g_ref[j, r, :].astype(F32)
                else:
                    g = g_ref[r, :]
                go_ref[r, :] = g
                nm = ADAM_B1 * m_ref[r, :] + (1.0 - ADAM_B1) * g
                nv = ADAM_B2 * v_ref[r, :] + (1.0 - ADAM_B2) * (g * g)
                m_hat = nm / (1.0 - ADAM_B1 ** ADAM_STEP)
                v_hat = nv / (1.0 - ADAM_B2 ** ADAM_STEP)
                d_ref[r, :] = -ADAM_LR * (m_hat / (jnp.sqrt(v_hat) + ADAM_EPS) + ADAM_WD * w_ref[r, :])
                nm_ref[r, :] = nm
                nv_ref[r, :] = nv

    out = pl.pallas_call(
        body, name="adamw",
        in_specs=[_whole_vmem()] * n_in, out_specs=[_whole_vmem()] * (4 * n),
        out_shape=[jax.ShapeDtypeStruct(w.shape, F32) for w in ws for _ in range(4)],
        compiler_params=pltpu.CompilerParams(vmem_limit_bytes=VMEM_LIMIT),
    )(*ws, *ms, *vs, *arrays, *([slab] if slab is not None else []))
    return [tuple(out[4 * t:4 * t + 4]) for t in range(n)]


_SLAB = 8 * LANES


def _flat_rows(a):
    flat = a.reshape(-1)
    pad = (-flat.shape[0]) % _SLAB
    if pad:
        flat = jnp.concatenate([flat, jnp.zeros((pad,), flat.dtype)])
    return flat.reshape(-1, LANES)


def _pack(arrays):
    slabs = [_flat_rows(a) for a in arrays]
    return jnp.concatenate(slabs, axis=0), [s.shape[0] for s in slabs]


SMALL = ('norm_pre', 'norm_post', 'ssm_a_re', 'ssm_a_im', 'ssm_log_dt', 'ssm_b_re', 'ssm_b_im', 'ssm_c_re',
         'ssm_c_im', 'ssm_d', 'b_glu', 'na_rpb', 'ple_norm')
BIG = ('w_in', 'w_glu', 'w_out', 'w_ple', 'w_ple_gate')
ORDER = ('norm_pre', 'norm_post', 'w_in', 'ssm_a_re', 'ssm_a_im', 'ssm_log_dt', 'ssm_b_re', 'ssm_b_im', 'ssm_c_re',
         'ssm_c_im', 'ssm_d', 'w_glu', 'b_glu', 'na_rpb', 'w_out', 'w_ple', 'ple_norm', 'w_ple_gate')


def kernel(x, p, norm_pre, norm_post, w_in, ssm_a_re, ssm_a_im, ssm_log_dt, ssm_b_re, ssm_b_im, ssm_c_re, ssm_c_im, ssm_d, w_glu, b_glu, na_rpb, w_out, w_ple, ple_norm, w_ple_gate, loss_target, m_norm_pre, m_norm_post, m_w_in, m_ssm_a_re, m_ssm_a_im, m_ssm_log_dt, m_ssm_b_re, m_ssm_b_im, m_ssm_c_re, m_ssm_c_im, m_ssm_d, m_w_glu, m_b_glu, m_na_rpb, m_w_out, m_w_ple, m_ple_norm, m_w_ple_gate, v_norm_pre, v_norm_post, v_w_in, v_ssm_a_re, v_ssm_a_im, v_ssm_log_dt, v_ssm_b_re, v_ssm_b_im, v_ssm_c_re, v_ssm_c_im, v_ssm_d, v_w_glu, v_b_glu, v_na_rpb, v_w_out, v_w_ple, v_ple_norm, v_w_ple_gate):
    args = dict(locals())
    weights = {n: args[n] for n in ORDER}
    mom_m = {n: args["m_" + n] for n in ORDER}
    mom_v = {n: args["v_" + n] for n in ORDER}

    w_in_g = _all_gather(w_in[0].astype(BF16)).reshape(N_DEV, D_MODEL, SHARD_IN)
    blocks = [weights[n][0].astype(BF16) for n in ('w_glu', 'w_out', 'w_ple', 'w_ple_gate')]

    ssm = tuple(weights[n][0] for n in ('ssm_a_re', 'ssm_a_im', 'ssm_log_dt', 'ssm_b_re', 'ssm_b_im',
                                        'ssm_c_re', 'ssm_c_im', 'ssm_d'))
    sq, dx, grads = _local_step(x[0], p[0, 0], loss_target[0], norm_pre, norm_post, w_in_g, ssm, blocks[0], b_glu,
                                na_rpb[0], blocks[1], blocks[2], ple_norm, blocks[3], distributed=True)
    loss_local = (0.5 / D_MODEL * jnp.sum(sq)).reshape(1)

    local = dict(norm_pre=grads['norm_pre'], norm_post=grads['norm_post'], b_glu=grads['b_glu'],
                 na_rpb=grads['na_rpb'][None], ple_norm=grads['ple_norm'])
    for n, g in zip(('ssm_a_re', 'ssm_a_im', 'ssm_log_dt', 'ssm_b_re', 'ssm_b_im', 'ssm_c_re', 'ssm_c_im', 'ssm_d'),
                    grads['ssm']):
        local[n] = g[None]

    small_flat, _ = _pack([local[n] for n in SMALL] + [loss_local])
    pad = (-small_flat.shape[0]) % (8 * N_DEV)
    if pad:
        small_flat = jnp.concatenate([small_flat, jnp.zeros((pad, LANES), F32)], axis=0)
    small_per = small_flat.shape[0] // N_DEV
    d_win = [grads['w_in']]
    lands, small_lands = _exchange_first(d_win, [small_flat.reshape(N_DEV, small_per, LANES)])
    chip_parts, small_sums = _pair_sum(d_win, lands, small_lands)
    chip_lands, small_full = _exchange_second(chip_parts, small_sums)
    grads['w_in'] = chip_lands[0]
    small_all = small_full[0].reshape(N_DEV * small_per, LANES)

    like = [weights[n] for n in SMALL]
    slabs = [[_flat_rows(t[n]) for n in SMALL] for t in (weights, mom_m, mom_v)]
    starts, at = [], 0
    for w_rows in slabs[0]:
        starts.append(at)
        at += w_rows.shape[0]
    loss = small_all[at, 0]
    results = _adamw([weights[n][0] for n in BIG] + slabs[0], [grads[n] for n in BIG] + starts,
                     [mom_m[n][0] for n in BIG] + slabs[1], [mom_v[n][0] for n in BIG] + slabs[2], slab=small_all)
    outs = {n: tuple(a[None] for a in res) for n, res in zip(BIG, results)}
    for n, a, res in zip(SMALL, like, results[len(BIG):]):
        outs[n] = tuple(r.reshape(-1)[:a.size].reshape(a.shape) for r in res)

    return (loss, dx[None], *[outs[n][0] for n in ORDER], *[outs[n][1] for n in ORDER],
            *[outs[n][2] for n in ORDER], *[outs[n][3] for n in ORDER])
```

```python
import functools

import jax
import jax.numpy as jnp
from jax import lax
from jax.experimental import pallas as pl
from jax.experimental.pallas import tpu as pltpu

F32 = jnp.float32
BF16 = jnp.bfloat16
HIGHEST = lax.Precision.HIGHEST

D_MODEL = 1024
D_PLE = 256
GRID_W = 64
D_SSM = 512
SSM_GROUP = 16
N_PAIRS = 16
SSM_STATE = 64
D_NA = 512
NA_HEADS = 8
NA_HEAD_DIM = 64
NA_ROWS = 8
NA_COLS = 16
D_IN_PROJ = 3072
EPS = 1e-6
N_DEV = 8
SHARD_IN = D_IN_PROJ // N_DEV
IN_GROUPS = N_DEV // 2
GROUP_W = 2 * SHARD_IN
LANES = 128
SSM_CHUNK = 16
PW_ROWS = 24
TOK_TILE = 256
NA_QROWS = 4
NA_KROWS = 12
NEG = -1e30
VMEM_LIMIT = 56 * 1024 * 1024

ADAM_LR = 0.001
ADAM_B1 = 0.9
ADAM_B2 = 0.999
ADAM_EPS = 1e-08
ADAM_WD = 0.01
ADAM_STEP = 10

MESH = pl.DeviceIdType.MESH


def _params(*sem):
    return pltpu.CompilerParams(dimension_semantics=sem or None, vmem_limit_bytes=VMEM_LIMIT)


def _whole_vmem():
    return pl.BlockSpec(memory_space=pltpu.VMEM)


def _nt(a, b):
    return lax.dot_general(a, b, (((1,), (1,)), ((), ())), preferred_element_type=F32)


def _tn(a, b):
    return lax.dot_general(a, b, (((0,), (0,)), ((), ())), preferred_element_type=F32)


def _mm(a, b):
    return jnp.dot(a, b, preferred_element_type=F32)


def _sigmoid(x):
    return 1.0 / (1.0 + jnp.exp(-x))


_GELU_C = 0.7978845608028654


def _gelu(x):
    return 0.5 * x * (1.0 + jnp.tanh(_GELU_C * (x + 0.044715 * x * x * x)))


def _gelu_grad(x):
    th = jnp.tanh(_GELU_C * (x + 0.044715 * x * x * x))
    return 0.5 * (1.0 + th) + 0.5 * x * (1.0 - th * th) * _GELU_C * (1.0 + 3.0 * 0.044715 * x * x)


def _ssm_tables(a_re, a_im, log_dt, b_re, b_im, c_re, c_im, d):
    T, P, H = SSM_CHUNK, SSM_STATE, SSM_GROUP
    dt = jnp.exp(log_dt)[..., None]
    xr = dt * a_re
    xi = dt * a_im
    mag = jnp.exp(xr)
    lr = mag * jnp.cos(xi)
    li = mag * jnp.sin(xi)
    den = a_re * a_re + a_im * a_im
    cr = ((lr - 1.0) * a_re + li * a_im) / den
    ci = (li * a_re - (lr - 1.0) * a_im) / den
    bbr = cr[..., None] * b_re - ci[..., None] * b_im
    bbi = cr[..., None] * b_im + ci[..., None] * b_re
    kk = jnp.arange(T + 1, dtype=F32)[:, None, None, None]
    pm = jnp.exp(kk * xr)
    pw = jnp.stack([pm * jnp.cos(kk * xi), pm * jnp.sin(kk * xi)], axis=2)
    pw = pw.reshape(T + 1, 2, 2, N_PAIRS, 2 * P).transpose(3, 0, 1, 2, 4).reshape(N_PAIRS, T + 1, 8 * P)
    pw = jnp.concatenate([pw, jnp.zeros((N_PAIRS, PW_ROWS - (T + 1), 8 * P), F32)], axis=1)
    eye2 = jnp.eye(2, dtype=F32)

    def expand(t):
        t = t.transpose(2, 0, 1, 3, 4, 5)
        t = t[:, :, :, :, :, None, :] * eye2[None, None, None, :, None, :, None]
        return t.reshape(N_PAIRS, 4, 2 * H, 2 * P)

    bb = expand(jnp.stack([bbr, bbi], axis=1).reshape(2, 2, N_PAIRS, 2, P, H).transpose(0, 1, 2, 3, 5, 4))
    cc = expand(jnp.stack([c_re, c_im], axis=1).reshape(2, 2, N_PAIRS, 2, H, P))
    dd = d.reshape(N_PAIRS, 2 * H)[:, :, None] * jnp.eye(2 * H, dtype=F32)[None]
    dd = jnp.concatenate([dd, jnp.zeros((N_PAIRS, 2 * H, LANES - 2 * H), F32)], axis=2)
    return pw, bb, cc, dd


NA_TAB = 2 * NA_ROWS


def _na_table(rpb):
    qc = jnp.arange(GRID_W)[:, None, None]
    kc = (jnp.arange(2 * GRID_W) % GRID_W)[None, :, None]
    dc = jnp.arange(2 * NA_COLS - 1)[None, None, :]
    cstart = jnp.clip(qc - NA_COLS // 2, 0, GRID_W - NA_COLS)
    csel = ((kc >= cstart) & (kc < cstart + NA_COLS) & (kc - qc + NA_COLS - 1 == dc)).astype(F32)
    col_ok = jnp.sum(csel, axis=-1) > 0.5
    part = jnp.einsum('hrc,qmc->hrqm', rpb, csel, precision=HIGHEST)
    zero = jnp.zeros_like(part[:, :1])
    odd = jnp.arange(2 * GRID_W) >= GRID_W
    tab = jnp.where(odd, jnp.concatenate([part, zero], axis=1), jnp.concatenate([zero, part], axis=1))
    return jnp.where(col_ok, tab, NEG)


def _place():
    return lax.axis_index("x"), lax.axis_index("y"), lax.axis_index("c")


def _remote(src, dst, send_sem, recv_sem, device):
    return pltpu.make_async_remote_copy(src_ref=src, dst_ref=dst, send_sem=send_sem, recv_sem=recv_sem,
                                        device_id=device, device_id_type=MESH)


def _start_all(copies):
    for cp in copies:
        cp.start()


def _wait_all(copies):
    for cp in copies:
        cp.wait()


def _gather_copies(shard_refs, full_refs, send_sems, recv_sems, local_sems):
    x, y, c = _place()
    me = 4 * x + 2 * y + c
    out = []
    for t, (shard, full) in enumerate(zip(shard_refs, full_refs)):
        out.append(pltpu.make_async_copy(shard, full.at[me], local_sems.at[t]))
        for k in range(1, N_DEV):
            peer = (x ^ ((k >> 2) & 1), y ^ ((k >> 1) & 1), c ^ (k & 1))
            out.append(_remote(shard, full.at[me], send_sems.at[t, k - 1], recv_sems.at[t, k - 1], peer))
    return out


def _pair_copies(part_refs, land_refs, send_sems, recv_sems):
    x, y, c = _place()
    out = []
    for t, (part, land) in enumerate(zip(part_refs, land_refs)):
        for q in range(4):
            out.append(_remote(part.at[2 * q + (1 - c)], land.at[q], send_sems.at[t, q], recv_sems.at[t, q],
                               (x, y, 1 - c)))
    return out


def _chip_copies(part_refs, land_refs, send_sems, recv_sems, local_sems):
    x, y, c = _place()
    mine = 2 * x + y
    out = []
    for t, (part, land) in enumerate(zip(part_refs, land_refs)):
        out.append(pltpu.make_async_copy(part.at[mine], land.at[mine], local_sems.at[t]))
        for k in range(1, 4):
            px, py = x ^ (k >> 1), y ^ (k & 1)
            out.append(_remote(part.at[2 * px + py], land.at[mine], send_sems.at[t, k - 1], recv_sems.at[t, k - 1],
                               (px, py, c)))
    return out


def _alltoall_copies(part_refs, land_refs, send_sems, recv_sems, local_sems):
    x, y, c = _place()
    me = 4 * x + 2 * y + c
    out = []
    for t, (part, land) in enumerate(zip(part_refs, land_refs)):
        out.append(pltpu.make_async_copy(part.at[me], land.at[me], local_sems.at[t]))
        for k in range(1, N_DEV):
            px, py, pc = x ^ ((k >> 2) & 1), y ^ ((k >> 1) & 1), c ^ (k & 1)
            out.append(_remote(part.at[4 * px + 2 * py + pc], land.at[me], send_sems.at[t, k - 1],
                               recv_sems.at[t, k - 1], (px, py, pc)))
    return out


def _ride(copies, first, last):
    pl.when(first)(functools.partial(_start_all, copies))
    pl.when(last)(functools.partial(_wait_all, copies))


_ANY = lambda: pl.BlockSpec(memory_space=pl.ANY)


def _dma_sems(*shape):
    return pltpu.SemaphoreType.DMA(shape)


PAIR_W = 2 * SSM_GROUP
PAIRS_PER_BLOCK = LANES // PAIR_W
CHUNK_W = SSM_CHUNK * PAIR_W
TILE_CHUNKS = TOK_TILE // SSM_CHUNK


def _lane_window(lo, width):
    lane = lax.broadcasted_iota(jnp.int32, (1, LANES), 1)
    return (lane >= lo) & (lane < lo + width)


def _tile_to_pairs(src_ref, out_ref):
    for cl in range(D_SSM // LANES):
        xs = [src_ref[cl, pl.ds(s, TILE_CHUNKS, stride=SSM_CHUNK), :] for s in range(SSM_CHUNK)]
        for a_ in range(PAIRS_PER_BLOCK):
            for v in range(SSM_CHUNK // PAIRS_PER_BLOCK):
                acc = None
                for i in range(PAIRS_PER_BLOCK):
                    shift = (PAIR_W * (i - a_)) % LANES
                    piece = xs[PAIRS_PER_BLOCK * v + i]
                    piece = pltpu.roll(piece, shift, axis=1) if shift else piece
                    acc = piece if acc is None else jnp.where(_lane_window(PAIR_W * i, PAIR_W), piece, acc)
                out_ref[PAIRS_PER_BLOCK * cl + a_, :, LANES * v:LANES * (v + 1)] = acc.astype(out_ref.dtype)


def _tile_from_pairs(in_ref, dst_ref):
    for cl in range(D_SSM // LANES):
        for s in range(SSM_CHUNK):
            v, i = divmod(s, PAIRS_PER_BLOCK)
            acc = None
            for a_ in range(PAIRS_PER_BLOCK):
                shift = (PAIR_W * (a_ - i)) % LANES
                piece = in_ref[PAIRS_PER_BLOCK * cl + a_, :, LANES * v:LANES * (v + 1)].astype(F32)
                piece = pltpu.roll(piece, shift, axis=1) if shift else piece
                acc = piece if acc is None else jnp.where(_lane_window(PAIR_W * a_, PAIR_W), piece, acc)
            dst_ref[cl, pl.ds(s, TILE_CHUNKS, stride=SSM_CHUNK), :] = acc


def _pair_tile():
    return pl.BlockSpec((N_PAIRS, TILE_CHUNKS, CHUNK_W), lambda i: (0, i, 0))


def _tile_scratch():
    return pltpu.VMEM((D_SSM // LANES, TOK_TILE, LANES), F32)


def _fwd_in(x2, g_pre, w_in_g, shards=()):
    seq = x2.shape[0]
    ns = len(shards)
    steps = seq // TOK_TILE

    def body(x_ref, g_ref, w_ref, *rest):
        shard_refs, rest = rest[:ns], rest[ns:]
        (u_ref, zs_ref, qkv_ref, zn_ref, hn_ref), rest = rest[:5], rest[5:]
        u_scr, rest = rest[-1], rest[:-1]
        if ns:
            i = pl.program_id(0)
            _ride(_gather_copies(shard_refs, rest[:ns], *rest[ns:]), i == 0, i == steps - 1)
        x = x_ref[...]
        r = lax.rsqrt(jnp.mean(x * x, axis=-1, keepdims=True) + EPS)
        hn = (x * r * g_ref[...]).astype(BF16)
        hn_ref[...] = hn
        for j in range(IN_GROUPS):
            pj = _mm(hn, w_ref[j])
            for i in range(GROUP_W // LANES):
                blk = (GROUP_W // LANES) * j + i
                piece = pj[:, i * LANES:(i + 1) * LANES]
                if blk < 4:
                    u_scr[blk] = piece
                elif blk < 8:
                    zs_ref[:, (blk - 4) * LANES:(blk - 3) * LANES] = piece
                elif blk < 20:
                    qkv_ref[:, (blk - 8) * LANES:(blk - 7) * LANES] = piece.astype(BF16)
                else:
                    zn_ref[:, (blk - 20) * LANES:(blk - 19) * LANES] = piece
        _tile_to_pairs(u_scr, u_ref)

    tok = lambda w: pl.BlockSpec((TOK_TILE, w), lambda i: (i, 0))
    out = pl.pallas_call(
        body, name="fwd_in", grid=(steps,),
        in_specs=[tok(D_MODEL), pl.BlockSpec((1, D_MODEL), lambda i: (0, 0)), _whole_vmem()] + [_ANY()] * ns,
        out_specs=[_pair_tile(), tok(D_SSM), tok(3 * D_NA), tok(D_NA), tok(D_MODEL)] + [_ANY()] * ns,
        out_shape=[jax.ShapeDtypeStruct((N_PAIRS, seq // SSM_CHUNK, CHUNK_W), BF16), jax.ShapeDtypeStruct((seq, D_SSM), F32),
                   jax.ShapeDtypeStruct((seq, 3 * D_NA), BF16), jax.ShapeDtypeStruct((seq, D_NA), F32),
                   jax.ShapeDtypeStruct((seq, D_MODEL), BF16)]
        + [jax.ShapeDtypeStruct((N_DEV,) + sh.shape, sh.dtype) for sh in shards],
        scratch_shapes=([_dma_sems(ns, N_DEV - 1), _dma_sems(ns, N_DEV - 1), _dma_sems(ns)] if ns else [])
        + [_tile_scratch()],
        compiler_params=_params("arbitrary"),
    )(x2, g_pre, w_in_g, *shards)
    return out[:5], list(out[5:])


def _mid(yssm, zs, o, zn, x2, p2, tgt, w_glu, b_glu, w_out, g_post, w_ple_g, g_ple, w_pg):
    seq = x2.shape[0]

    def body(yssm_ref, zs_ref, o_ref, zn_ref, x_ref, p_ref, tgt_ref, wglu_ref, bglu_ref, wout_ref, gpost_ref,
             wple_ref, gple_ref, wpg_ref,
             dyssm_ref, dzs_ref, do_ref, dzn_ref, dh1_ref, loss_ref, dwglu_ref, dbglu_ref, dwout_ref, dwple_ref,
             dwpg_ref, dgpost_ref, dgple_ref, y_scr, dy_scr):
        @pl.when(pl.program_id(0) == 0)
        def _():
            for ref in (loss_ref, dwglu_ref, dbglu_ref, dwout_ref, dwple_ref, dwpg_ref, dgpost_ref, dgple_ref):
                ref[...] = jnp.zeros_like(ref)

        _tile_from_pairs(yssm_ref, y_scr)
        yv = jnp.concatenate([y_scr[cl] for cl in range(D_SSM // LANES)], axis=1)
        g1 = _gelu(yv)
        g1b = g1.astype(BF16)
        sg = _sigmoid(_mm(g1b, wglu_ref[...]) + bglu_ref[...])
        zs_v = zs_ref[...]
        s_zs = _sigmoid(zs_v)
        g2 = g1 * sg
        zn_v = zn_ref[...]
        s_zn = _sigmoid(zn_v)
        o_v = o_ref[...]
        cat = jnp.concatenate([g2 * (zs_v * s_zs), o_v * (zn_v * s_zn)], axis=1).astype(BF16)
        mix = _mm(cat, wout_ref[...])
        r2 = lax.rsqrt(jnp.mean(mix * mix, axis=-1, keepdims=True) + EPS)
        n2 = mix * r2
        gpost = gpost_ref[...]
        h1 = x_ref[...] + n2 * gpost
        pb = p_ref[...].astype(BF16)
        epre = jnp.concatenate([_mm(pb, wple_ref[j]) for j in range(N_DEV)], axis=1)
        r3 = lax.rsqrt(jnp.mean(epre * epre, axis=-1, keepdims=True) + EPS)
        n3 = epre * r3
        gple = gple_ref[...]
        e = n3 * gple
        h1b = h1.astype(BF16)
        gate = _sigmoid(_mm(h1b, wpg_ref[...]))
        diff = h1 + gate * e - tgt_ref[...]
        loss_ref[...] += jnp.sum(diff * diff, axis=0, keepdims=True)

        dy = diff * (1.0 / D_MODEL)
        dgp = (dy * e * gate * (1.0 - gate)).astype(BF16)
        de = dy * gate
        dh1 = dy + _nt(dgp, wpg_ref[...])
        dh1_ref[...] = dh1
        dwpg_ref[...] += _tn(h1b, dgp)
        dgple_ref[...] += jnp.sum(de * n3, axis=0, keepdims=True)
        dn3 = de * gple
        depre = (r3 * (dn3 - n3 * jnp.mean(dn3 * n3, axis=-1, keepdims=True))).astype(BF16)
        for j in range(N_DEV):
            dwple_ref[j] += _tn(pb, depre[:, j * LANES:(j + 1) * LANES])
        dgpost_ref[...] += jnp.sum(dh1 * n2, axis=0, keepdims=True)
        dn2 = dh1 * gpost
        dmix = (r2 * (dn2 - n2 * jnp.mean(dn2 * n2, axis=-1, keepdims=True))).astype(BF16)
        dcat = _nt(dmix, wout_ref[...])
        dwout_ref[...] += _tn(cat, dmix)

        dys = dcat[:, :D_SSM]
        dyn = dcat[:, D_SSM:]
        dg2 = dys * (zs_v * s_zs)
        dzs_ref[...] = (dys * g2 * (s_zs * (1.0 + zs_v * (1.0 - s_zs)))).astype(BF16)
        dt = dg2 * g2 * (1.0 - sg)
        dtb = dt.astype(BF16)
        dg1 = dg2 * sg + _nt(dtb, wglu_ref[...])
        dwglu_ref[...] += _tn(g1b, dtb)
        dbglu_ref[...] += jnp.sum(dt, axis=0, keepdims=True)
        dyssm = dg1 * _gelu_grad(yv)
        for cl in range(D_SSM // LANES):
            dy_scr[cl] = dyssm[:, cl * LANES:(cl + 1) * LANES]
        _tile_to_pairs(dy_scr, dyssm_ref)
        do_ref[...] = (dyn * (zn_v * s_zn)).astype(BF16)
        dzn_ref[...] = (dyn * o_v * (s_zn * (1.0 + zn_v * (1.0 - s_zn)))).astype(BF16)

    tok = lambda w: pl.BlockSpec((TOK_TILE, w), lambda i: (i, 0))
    row = lambda w: pl.BlockSpec((1, w), lambda i: (0, 0))
    vm = _whole_vmem()
    half_b = jax.ShapeDtypeStruct((seq, D_SSM), BF16)
    gain = jax.ShapeDtypeStruct((1, D_MODEL), F32)
    return pl.pallas_call(
        body, name="mid", grid=(seq // TOK_TILE,),
        in_specs=[_pair_tile(), tok(D_SSM), tok(D_NA), tok(D_NA), tok(D_MODEL), tok(D_PLE), tok(D_MODEL),
                  vm, row(D_SSM), vm, row(D_MODEL), vm, row(D_MODEL), vm],
        out_specs=[_pair_tile(), tok(D_SSM), tok(D_NA), tok(D_NA), tok(D_MODEL), vm, vm, vm, vm, vm, vm, vm, vm],
        out_shape=[jax.ShapeDtypeStruct((N_PAIRS, seq // SSM_CHUNK, CHUNK_W), BF16), half_b, half_b, half_b, jax.ShapeDtypeStruct((seq, D_MODEL), F32), gain,
                   jax.ShapeDtypeStruct((D_SSM, D_SSM), F32), jax.ShapeDtypeStruct((1, D_SSM), F32),
                   jax.ShapeDtypeStruct((D_MODEL, D_MODEL), F32),
                   jax.ShapeDtypeStruct((N_DEV, D_PLE, LANES), F32),
                   jax.ShapeDtypeStruct((D_MODEL, D_MODEL), F32), gain, gain],
        scratch_shapes=[_tile_scratch(), _tile_scratch()],
        compiler_params=_params("arbitrary"),
    )(yssm, zs, o, zn, x2, p2, tgt, w_glu, b_glu, w_out, g_post, w_ple_g, g_ple, w_pg)


def _bwd_in(du, dzs, dq, dk, dv, dzn, hn, x2, dh1, g_pre, w_in_g):
    seq = x2.shape[0]

    def body(du_ref, dzs_ref, dq_ref, dk_ref, dv_ref, dzn_ref, hn_ref, x_ref, dh1_ref, g_ref, w_ref,
             dx_ref, dw_ref, dg_ref, dproj_ref, du_scr):
        @pl.when(pl.program_id(0) == 0)
        def _():
            dw_ref[...] = jnp.zeros_like(dw_ref)
            dg_ref[...] = jnp.zeros_like(dg_ref)

        _tile_from_pairs(du_ref, du_scr)
        for cl in range(D_SSM // LANES):
            dproj_ref[:, cl * LANES:(cl + 1) * LANES] = du_scr[cl].astype(BF16)
        for k, ref in enumerate((dzs_ref, dq_ref, dk_ref, dv_ref, dzn_ref), start=1):
            dproj_ref[:, k * D_SSM:(k + 1) * D_SSM] = ref[...].astype(BF16)
        hn = hn_ref[...]
        dhn = jnp.zeros((TOK_TILE, D_MODEL), F32)
        for j in range(IN_GROUPS):
            dpj = dproj_ref[:, j * GROUP_W:(j + 1) * GROUP_W]
            dhn += _nt(dpj, w_ref[j])
            dw_ref[j] += _tn(hn, dpj)
        x = x_ref[...]
        r = lax.rsqrt(jnp.mean(x * x, axis=-1, keepdims=True) + EPS)
        n1 = x * r
        dg_ref[...] += jnp.sum(dhn * n1, axis=0, keepdims=True)
        dn1 = dhn * g_ref[...]
        dx_ref[...] = dh1_ref[...] + r * (dn1 - n1 * jnp.mean(dn1 * n1, axis=-1, keepdims=True))

    tok = lambda w: pl.BlockSpec((TOK_TILE, w), lambda i: (i, 0))
    vm = _whole_vmem()
    return pl.pallas_call(
        body, name="bwd_in", grid=(seq // TOK_TILE,),
        in_specs=[_pair_tile()] + [tok(D_SSM)] * 5 + [tok(D_MODEL), tok(D_MODEL), tok(D_MODEL),
                                      pl.BlockSpec((1, D_MODEL), lambda i: (0, 0)), vm],
        out_specs=[tok(D_MODEL), vm, vm],
        out_shape=[jax.ShapeDtypeStruct((seq, D_MODEL), F32),
                   jax.ShapeDtypeStruct((IN_GROUPS, D_MODEL, GROUP_W), F32),
                   jax.ShapeDtypeStruct((1, D_MODEL), F32)],
        scratch_shapes=[pltpu.VMEM((TOK_TILE, D_IN_PROJ), BF16), _tile_scratch()],
        compiler_params=_params("arbitrary"),
    )(du, dzs, dq, dk, dv, dzn, hn, x2, dh1, g_pre, w_in_g)


SSM_PAIRS = 4
SLOT = 8


def _boundary_scan(nc, pw_ref, buf_ref, conj):
    pad = [jnp.zeros((SLOT - SSM_PAIRS, LANES), F32)]
    lr0, li0, lr1, li1 = (jnp.concatenate([_pw_row(pw_ref, j, SSM_CHUNK, q) for j in range(SSM_PAIRS)] + pad, axis=0)
                          for q in range(4))
    if conj:
        li0, li1 = -li0, -li1

    def step(c, carry):
        hr0, hi0, hr1, hi1 = carry
        up = pl.ds(pl.multiple_of(c * SLOT, SLOT), SLOT)
        dn = pl.ds(pl.multiple_of((nc - 1 - c) * SLOT, SLOT), SLOT)
        ra, rb = (dn, up) if conj else (up, dn)
        s_r0, s_i0 = buf_ref[0, ra, :], buf_ref[1, ra, :]
        s_r1, s_i1 = buf_ref[2, rb, :], buf_ref[3, rb, :]
        buf_ref[0, ra, :] = hr0
        buf_ref[1, ra, :] = hi0
        buf_ref[2, rb, :] = hr1
        buf_ref[3, rb, :] = hi1
        return (lr0 * hr0 - li0 * hi0 + s_r0, lr0 * hi0 + li0 * hr0 + s_i0,
                lr1 * hr1 - li1 * hi1 + s_r1, lr1 * hi1 + li1 * hr1 + s_i1)

    z = jnp.zeros((SLOT, LANES), F32)
    lax.fori_loop(0, nc, step, (z, z, z, z), unroll=8)


def _put_pair(buf_ref, j, nc, val):
    for q in range(4):
        buf_ref[q, pl.ds(j, nc, stride=SLOT), :] = val[:, LANES * q:LANES * (q + 1)]


def _get_pair(buf_ref, j, nc):
    return jnp.concatenate([buf_ref[q, pl.ds(j, nc, stride=SLOT), :] for q in range(4)], axis=1)


def _pw_row(pw_ref, j, k, q):
    return pw_ref[j, k:k + 1, LANES * q:LANES * (q + 1)]


def _mm_f32(a, b, dims):
    return lax.dot_general(a, b, (dims, ((), ())), precision=HIGHEST, preferred_element_type=F32)


_POW_M = (lambda s: SSM_CHUNK - 1 - s, lambda s: s)
_POW_C = (lambda s: s + 1, lambda s: SSM_CHUNK - s)
_POW_K = (lambda s: s, lambda s: SSM_CHUNK - 1 - s)


def _chunk_matrices(j, pw_ref, bb_ref, cc_ref, dd_ref, m_scr, ct_scr, toep_scr, g_scr, kt_scr):
    blk = lambda s: slice(PAIR_W * s, PAIR_W * (s + 1))
    col = lambda q: slice(LANES * q, LANES * (q + 1))
    for d in range(2):
        bbr, bbi = bb_ref[j, 2 * d], bb_ref[j, 2 * d + 1]
        ccr, cci = cc_ref[j, 2 * d], cc_ref[j, 2 * d + 1]
        for s in range(SSM_CHUNK):
            pr, pi = _pw_row(pw_ref, j, _POW_M[d](s), 2 * d), _pw_row(pw_ref, j, _POW_M[d](s), 2 * d + 1)
            m_scr[j, blk(s), col(2 * d)] = (pr * bbr - pi * bbi).astype(m_scr.dtype)
            m_scr[j, blk(s), col(2 * d + 1)] = (pr * bbi + pi * bbr).astype(m_scr.dtype)
            pr, pi = _pw_row(pw_ref, j, _POW_C[d](s), 2 * d), _pw_row(pw_ref, j, _POW_C[d](s), 2 * d + 1)
            ct_scr[j, blk(s), col(2 * d)] = (ccr * pr - cci * pi).astype(ct_scr.dtype)
            ct_scr[j, blk(s), col(2 * d + 1)] = (-(ccr * pi + cci * pr)).astype(ct_scr.dtype)
            pr, pi = _pw_row(pw_ref, j, _POW_K[d](s), 2 * d), _pw_row(pw_ref, j, _POW_K[d](s), 2 * d + 1)
            g_scr[j, d, blk(s), 0:LANES] = ccr * pr - cci * pi
            g_scr[j, d, blk(s), LANES:2 * LANES] = -(ccr * pi + cci * pr)
        kt = _mm_f32(jnp.concatenate([bbr, bbi], axis=1), g_scr[j, d], ((1,), (1,)))
        if d == 0:
            kt = jnp.concatenate([kt[:, 0:LANES] + dd_ref[j], kt[:, LANES:]], axis=1)
        kt_scr[d] = kt
    lane = lax.broadcasted_iota(jnp.int32, (1, CHUNK_W), 1)
    for s in range(SSM_CHUNK):
        lo = PAIR_W * s
        hi = PAIR_W * (s + 1)
        fwd = kt_scr[0] if s == 0 else pltpu.roll(kt_scr[0], lo, axis=1)
        bwd = kt_scr[1] if hi == CHUNK_W else pltpu.roll(kt_scr[1], hi, axis=1)
        row = jnp.where(lane >= lo, fwd, 0.0) + jnp.where(lane < hi, bwd, 0.0)
        toep_scr[j, blk(s), :] = row.astype(toep_scr.dtype)


def _ssm_scratch(nc, mat_dtype, buffers):
    mats = [pltpu.VMEM((SSM_PAIRS, CHUNK_W, CHUNK_W), mat_dtype) for _ in range(3)]
    return mats + [pltpu.VMEM((SSM_PAIRS, 2, CHUNK_W, 2 * LANES), F32), pltpu.VMEM((2, PAIR_W, CHUNK_W), F32)] + [
        pltpu.VMEM((4, nc * SLOT, LANES), F32) for _ in range(buffers)]


def _per_step(*shape):
    return pl.BlockSpec((SSM_PAIRS,) + shape, lambda g: (g,) + (0,) * len(shape))


_TABLE_SPECS = lambda: [_per_step(PW_ROWS, CHUNK_W), _per_step(4, PAIR_W, LANES), _per_step(4, PAIR_W, LANES),
                        _per_step(PAIR_W, LANES)]


def _ssm_fwd(u, pw, bb, cc, dd):
    npair, nc, width = u.shape

    def body(u_ref, pw_ref, bb_ref, cc_ref, dd_ref, y_ref, m_scr, ct_scr, toep_scr, g_scr, kt_scr, h_scr):
        @pl.when(pl.program_id(0) == 0)
        def _():
            h_scr[...] = jnp.zeros_like(h_scr)

        for j in range(SSM_PAIRS):
            _chunk_matrices(j, pw_ref, bb_ref, cc_ref, dd_ref, m_scr, ct_scr, toep_scr, g_scr, kt_scr)
            _put_pair(h_scr, j, nc, _mm(u_ref[j], m_scr[j]))
        _boundary_scan(nc, pw_ref, h_scr, conj=False)
        for j in range(SSM_PAIRS):
            hin = _get_pair(h_scr, j, nc)
            y_ref[j] = _mm(u_ref[j], toep_scr[j]) + _nt(hin.astype(u.dtype), ct_scr[j])

    return pl.pallas_call(
        body, name="ssm_fwd", grid=(npair // SSM_PAIRS,),
        in_specs=[_per_step(nc, width)] + _TABLE_SPECS(),
        out_specs=_per_step(nc, width),
        out_shape=jax.ShapeDtypeStruct((npair, nc, width), F32),
        scratch_shapes=_ssm_scratch(nc, u.dtype, 1),
        compiler_params=_params("arbitrary"),
    )(u, pw, bb, cc, dd)


def _ssm_bwd(u, dy, pw, bb, cc, dd, chip_parts=()):
    npair, nc, width = u.shape
    ns = len(chip_parts)
    steps = npair // SSM_PAIRS

    def body(u_ref, dy_ref, pw_ref, bb_ref, cc_ref, dd_ref, *rest):
        part_refs, rest = rest[:ns], rest[ns:]
        (du_ref, dpw_ref, dbb_ref, dcc_ref, ddd_ref), rest = rest[:5], rest[5:]
        land_refs, rest = rest[:ns], rest[ns:]
        m_scr, ct_scr, toep_scr, g_scr, kt_scr, h_scr, d_scr = rest[:7]
        g = pl.program_id(0)
        if ns:
            _ride(_chip_copies(part_refs, land_refs, *rest[7:]), g == 0, g == steps - 1)

        @pl.when(g == 0)
        def _():
            h_scr[...] = jnp.zeros_like(h_scr)
            d_scr[...] = jnp.zeros_like(d_scr)

        for j in range(SSM_PAIRS):
            _chunk_matrices(j, pw_ref, bb_ref, cc_ref, dd_ref, m_scr, ct_scr, toep_scr, g_scr, kt_scr)
            _put_pair(h_scr, j, nc, _mm(u_ref[j], m_scr[j]))
            _put_pair(d_scr, j, nc, _mm(dy_ref[j], ct_scr[j]))
        _boundary_scan(nc, pw_ref, h_scr, conj=False)
        _boundary_scan(nc, pw_ref, d_scr, conj=True)

        dpw_ref[...] = jnp.zeros_like(dpw_ref)
        blk = lambda s: slice(PAIR_W * s, PAIR_W * (s + 1))
        col = lambda q: slice(LANES * q, LANES * (q + 1))
        lane = lax.broadcasted_iota(jnp.int32, (1, CHUNK_W), 1)
        for j in range(SSM_PAIRS):
            uv = u_ref[j]
            dyb = dy_ref[j]
            hin = _get_pair(h_scr, j, nc)
            ds = _get_pair(d_scr, j, nc)
            dsb = ds.astype(u.dtype)
            du_ref[j] = _nt(dsb, m_scr[j]) + _nt(dyb, toep_scr[j])
            dm = _tn(uv, dsb)
            dct = _tn(dyb, hin.astype(u.dtype))
            dtoep = _tn(uv, dyb)

            def add_pw(k, q, val):
                dpw_ref[j, k:k + 1, col(q)] += jnp.sum(val, axis=0, keepdims=True)

            for d in range(2):
                g_r, g_i = ds[:, col(2 * d)], ds[:, col(2 * d + 1)]
                h_r, h_i = hin[:, col(2 * d)], hin[:, col(2 * d + 1)]
                add_pw(SSM_CHUNK, 2 * d, g_r * h_r + g_i * h_i)
                add_pw(SSM_CHUNK, 2 * d + 1, g_i * h_r - g_r * h_i)

            dkt0 = jnp.zeros((PAIR_W, CHUNK_W), F32)
            dkt1 = jnp.zeros((PAIR_W, CHUNK_W), F32)
            for s in range(SSM_CHUNK):
                lo = PAIR_W * s
                hi = PAIR_W * (s + 1)
                row = dtoep[blk(s), :]
                fwd = jnp.where(lane >= lo, row, 0.0)
                bwd = jnp.where(lane < hi, row, 0.0)
                dkt0 += fwd if s == 0 else pltpu.roll(fwd, CHUNK_W - lo, axis=1)
                dkt1 += bwd if hi == CHUNK_W else pltpu.roll(bwd, CHUNK_W - hi, axis=1)
            ddd_ref[j] = dkt0[:, 0:LANES]

            for d, dkt in enumerate((dkt0, dkt1)):
                bbr, bbi = bb_ref[j, 2 * d], bb_ref[j, 2 * d + 1]
                ccr, cci = cc_ref[j, 2 * d], cc_ref[j, 2 * d + 1]
                dbbcat = _mm_f32(dkt, g_scr[j, d], ((1,), (0,)))
                dg = _mm_f32(dkt, jnp.concatenate([bbr, bbi], axis=1), ((0,), (0,)))
                dbbr, dbbi = dbbcat[:, 0:LANES], dbbcat[:, LANES:]
                dccr = jnp.zeros((PAIR_W, LANES), F32)
                dcci = jnp.zeros((PAIR_W, LANES), F32)
                for s in range(SSM_CHUNK):
                    k = _POW_M[d](s)
                    pr, pi = _pw_row(pw_ref, j, k, 2 * d), _pw_row(pw_ref, j, k, 2 * d + 1)
                    gr, gi = dm[blk(s), col(2 * d)], dm[blk(s), col(2 * d + 1)]
                    dbbr += gr * pr + gi * pi
                    dbbi += gi * pr - gr * pi
                    add_pw(k, 2 * d, gr * bbr + gi * bbi)
                    add_pw(k, 2 * d + 1, gi * bbr - gr * bbi)
                    for k, gr, gi in ((_POW_C[d](s), dct[blk(s), col(2 * d)], dct[blk(s), col(2 * d + 1)]),
                                      (_POW_K[d](s), dg[blk(s), 0:LANES], dg[blk(s), LANES:])):
                        pr, pi = _pw_row(pw_ref, j, k, 2 * d), _pw_row(pw_ref, j, k, 2 * d + 1)
                        dccr += gr * pr - gi * pi
                        dcci += -(gr * pi + gi * pr)
                        add_pw(k, 2 * d, gr * ccr - gi * cci)
                        add_pw(k, 2 * d + 1, -(gr * cci + gi * ccr))
                dbb_ref[j, 2 * d] = dbbr
                dbb_ref[j, 2 * d + 1] = dbbi
                dcc_ref[j, 2 * d] = dccr
                dcc_ref[j, 2 * d + 1] = dcci

    out = pl.pallas_call(
        body, name="ssm_bwd", grid=(steps,),
        in_specs=[_per_step(nc, width), _per_step(nc, width)] + _TABLE_SPECS() + [_ANY()] * ns,
        out_specs=[_per_step(nc, width)] + _TABLE_SPECS() + [_ANY()] * ns,
        out_shape=[jax.ShapeDtypeStruct((npair, nc, width), F32), jax.ShapeDtypeStruct(pw.shape, F32),
                   jax.ShapeDtypeStruct(bb.shape, F32), jax.ShapeDtypeStruct(cc.shape, F32),
                   jax.ShapeDtypeStruct(dd.shape, F32)]
        + [jax.ShapeDtypeStruct(pt.shape, pt.dtype) for pt in chip_parts],
        scratch_shapes=_ssm_scratch(nc, u.dtype, 2) + ([_dma_sems(ns, 3), _dma_sems(ns, 3), _dma_sems(ns)] if ns else []),
        compiler_params=_params("arbitrary"),
    )(u, dy, pw, bb, cc, dd, *chip_parts)
    return out[:5], list(out[5:])


NA_Q = NA_QROWS * GRID_W
NA_K = NA_KROWS * GRID_W
NA_SCALE = NA_HEAD_DIM ** -0.5


def _na_block(b, nb, rows):
    start = jnp.clip(NA_QROWS * b - NA_ROWS // 2, 0, rows - NA_KROWS) * GRID_W
    kind = jnp.where(b == 0, 0, jnp.where(b == nb - 1, 2, 1))
    return pl.multiple_of(start, GRID_W), kind


NA_CHUNK = 16


def _na_pieces(kind, i):
    ri, q0 = divmod(i * NA_CHUNK, GRID_W)
    off = (NA_ROWS - 1, NA_ROWS // 2 - 1, -1)[kind]
    lo = (0, ri, NA_KROWS - NA_ROWS)[kind]
    modes = {(True, True): 'both', (True, False): 'even', (False, True): 'odd', (False, False): None}
    out = []
    for k2 in range(NA_KROWS // 2):
        inside = tuple(lo <= kr < lo + NA_ROWS for kr in (2 * k2, 2 * k2 + 1))
        out.append((2 * k2 - ri + off + 1, slice(q0, q0 + NA_CHUNK), modes[inside]))
    return out


def _na_softmax_pieces(s_ref, tab_ref, hh, kind, i):
    rows = slice(i * NA_CHUNK, (i + 1) * NA_CHUNK)
    lane = lax.broadcasted_iota(jnp.int32, (1, LANES), 1)
    xs = []
    for k2, (t, q, mode) in enumerate(_na_pieces(kind, i)):
        if mode is None:
            xs.append(None)
            continue
        bias = tab_ref[hh, t, q, :]
        if mode == 'even':
            bias = jnp.where(lane < GRID_W, bias, NEG)
        elif mode == 'odd':
            bias = jnp.where(lane >= GRID_W, bias, NEG)
        xs.append(s_ref[hh, rows, k2 * LANES:(k2 + 1) * LANES] + bias)
    live = [x for x in xs if x is not None]
    m = jnp.max(functools.reduce(jnp.maximum, live), axis=-1, keepdims=True)
    es = [None if x is None else jnp.exp(x - m) for x in xs]
    total = jnp.sum(functools.reduce(jnp.add, [e for e in es if e is not None]), axis=-1, keepdims=True)
    inv = 1.0 / total
    return [None if e is None else e * inv for e in es]


def _na_heads():
    lane = lax.broadcasted_iota(jnp.int32, (1, LANES), 1)
    return [lane < NA_HEAD_DIM, lane >= NA_HEAD_DIM]


def _na_fwd(qkv, bias):
    seq = qkv.shape[0]
    rows = seq // GRID_W
    nb = rows // NA_QROWS

    def body(q_ref, k_ref, v_ref, bias_ref, o_ref, p_ref, s_scr):
        start, kind = _na_block(pl.program_id(1), nb, rows)

        def block(static_kind):
            q2 = q_ref[...] * NA_SCALE
            kw = k_ref[pl.ds(start, NA_K), :]
            vw = v_ref[pl.ds(start, NA_K), :]
            heads = _na_heads()
            for hh in range(2):
                s_scr[hh] = _nt(jnp.where(heads[hh], q2, jnp.zeros_like(q2)), kw)
            for hh in range(2):
                for i in range(NA_Q // NA_CHUNK):
                    r = slice(i * NA_CHUNK, (i + 1) * NA_CHUNK)
                    for k2, p in enumerate(_na_softmax_pieces(s_scr, bias_ref, hh, static_kind, i)):
                        p = jnp.zeros((NA_CHUNK, LANES), F32) if p is None else p
                        p_ref[hh, r, k2 * LANES:(k2 + 1) * LANES] = p.astype(p_ref.dtype)
            o_ref[...] = jnp.where(heads[0], _mm(p_ref[0], vw), _mm(p_ref[1], vw))

        for static_kind in range(3):
            pl.when(kind == static_kind)(functools.partial(block, static_kind))

    return pl.pallas_call(
        body, name="na_fwd", grid=(NA_HEADS // 2, nb),
        in_specs=[pl.BlockSpec((NA_Q, LANES), lambda hp, b: (b, hp)),
                  pl.BlockSpec((seq, LANES), lambda hp, b: (0, 4 + hp)),
                  pl.BlockSpec((seq, LANES), lambda hp, b: (0, 8 + hp)),
                  pl.BlockSpec((2, NA_TAB, GRID_W, LANES), lambda hp, b: (hp, 0, 0, 0))],
        out_specs=[pl.BlockSpec((NA_Q, LANES), lambda hp, b: (b, hp)),
                   pl.BlockSpec((2, NA_Q, NA_K), lambda hp, b: (hp, b, 0))],
        out_shape=[jax.ShapeDtypeStruct((seq, D_NA), F32), jax.ShapeDtypeStruct((NA_HEADS, seq, NA_K), qkv.dtype)],
        scratch_shapes=[pltpu.VMEM((2, NA_Q, NA_K), F32)],
        compiler_params=_params("arbitrary", "arbitrary"),
    )(qkv, qkv, qkv, bias)


def _na_bwd(qkv, do, probs, parts=()):
    seq = qkv.shape[0]
    rows = seq // GRID_W
    nb = rows // NA_QROWS
    ns = len(parts)

    def body(q_ref, k_ref, v_ref, do_ref, p_ref, *rest):
        part_refs, rest = rest[:ns], rest[ns:]
        (dq_ref, dk_ref, dv_ref, dbias_ref), rest = rest[:4], rest[4:]
        land_refs, rest = rest[:ns], rest[ns:]
        dp_scr, ds_scr = rest[:2]
        b = pl.program_id(1)
        start, kind = _na_block(b, nb, rows)
        if ns:
            hp = pl.program_id(0)
            _ride(_pair_copies(part_refs, land_refs, *rest[2:]), (hp == 0) & (b == 0),
                  (hp == NA_HEADS // 2 - 1) & (b == nb - 1))

        @pl.when(b == 0)
        def _():
            dk_ref[...] = jnp.zeros_like(dk_ref)
            dv_ref[...] = jnp.zeros_like(dv_ref)
            dbias_ref[...] = jnp.zeros_like(dbias_ref)

        def block(static_kind):
            q2 = q_ref[...] * NA_SCALE
            kw = k_ref[pl.ds(start, NA_K), :]
            vw = v_ref[pl.ds(start, NA_K), :]
            do2 = do_ref[...].astype(q2.dtype)
            heads = _na_heads()
            col = lambda k2: slice(k2 * LANES, (k2 + 1) * LANES)
            zero = jnp.zeros((NA_CHUNK, LANES), ds_scr.dtype)
            for hh in range(2):
                dp_scr[hh] = _nt(jnp.where(heads[hh], do2, jnp.zeros_like(do2)), vw)
            for hh in range(2):
                for i in range(NA_Q // NA_CHUNK):
                    r = slice(i * NA_CHUNK, (i + 1) * NA_CHUNK)
                    pieces = _na_pieces(static_kind, i)
                    ps = [None if mode is None else p_ref[hh, r, col(k2)].astype(F32)
                          for k2, (_, _, mode) in enumerate(pieces)]
                    dps = [None if p is None else dp_scr[hh, r, col(k2)] for k2, p in enumerate(ps)]
                    pdp = functools.reduce(jnp.add, [p * dp for p, dp in zip(ps, dps) if p is not None])
                    rowsum = jnp.sum(pdp, axis=-1, keepdims=True)
                    for k2, (t, q, _) in enumerate(pieces):
                        if ps[k2] is None:
                            ds_scr[hh, r, col(k2)] = zero
                            continue
                        ds = ps[k2] * (dps[k2] - rowsum)
                        dbias_ref[hh, t, q, :] += ds
                        ds_scr[hh, r, col(k2)] = ds.astype(ds_scr.dtype)
            dq_ref[...] = jnp.where(heads[0], _mm(ds_scr[0], kw), _mm(ds_scr[1], kw)) * NA_SCALE
            dk_ref[pl.ds(start, NA_K), :] += jnp.where(heads[0], _tn(ds_scr[0], q2), _tn(ds_scr[1], q2))
            dv_ref[pl.ds(start, NA_K), :] += jnp.where(heads[0], _tn(p_ref[0], do2), _tn(p_ref[1], do2))

        for static_kind in range(3):
            pl.when(kind == static_kind)(functools.partial(block, static_kind))

    out = pl.pallas_call(
        body, name="na_bwd", grid=(NA_HEADS // 2, nb),
        in_specs=[pl.BlockSpec((NA_Q, LANES), lambda hp, b: (b, hp)),
                  pl.BlockSpec((seq, LANES), lambda hp, b: (0, 4 + hp)),
                  pl.BlockSpec((seq, LANES), lambda hp, b: (0, 8 + hp)),
                  pl.BlockSpec((NA_Q, LANES), lambda hp, b: (b, hp)),
                  pl.BlockSpec((2, NA_Q, NA_K), lambda hp, b: (hp, b, 0))] + [_ANY()] * ns,
        out_specs=[pl.BlockSpec((NA_Q, LANES), lambda hp, b: (b, hp)),
                   pl.BlockSpec((seq, LANES), lambda hp, b: (0, hp)),
                   pl.BlockSpec((seq, LANES), lambda hp, b: (0, hp)),
                   pl.BlockSpec((2, NA_TAB, GRID_W, LANES), lambda hp, b: (hp, 0, 0, 0))] + [_ANY()] * ns,
        out_shape=[jax.ShapeDtypeStruct((seq, D_NA), F32), jax.ShapeDtypeStruct((seq, D_NA), F32),
                   jax.ShapeDtypeStruct((seq, D_NA), F32),
                   jax.ShapeDtypeStruct((NA_HEADS, NA_TAB, GRID_W, LANES), F32)]
        + [jax.ShapeDtypeStruct((4,) + pt.shape[1:], pt.dtype) for pt in parts],
        scratch_shapes=[pltpu.VMEM((2, NA_Q, NA_K), F32), pltpu.VMEM((2, NA_Q, NA_K), qkv.dtype)]
        + ([_dma_sems(ns, 4), _dma_sems(ns, 4)] if ns else []),
        compiler_params=_params("arbitrary", "arbitrary"),
    )(qkv, qkv, qkv, do, probs, *parts)
    return out[:4], list(out[4:])


def _local_step(x2, p2, tgt, g_pre, g_post, w_in_g, ssm, w_glu, b_glu, rpb, w_out, w_ple, g_ple, w_pg,
                distributed=False):
    (pw, bb, cc, dd), ssm_vjp = jax.vjp(_ssm_tables, *ssm)
    bias, bias_vjp = jax.vjp(_na_table, rpb)

    riders = (w_glu, w_out, w_ple, w_pg) if distributed else ()
    w_in_g = w_in_g.reshape(IN_GROUPS, 2, D_MODEL, SHARD_IN).transpose(0, 2, 1, 3).reshape(IN_GROUPS, D_MODEL, GROUP_W)
    (u_p, zs, qkv, zn, hn), gathered = _fwd_in(x2, g_pre, w_in_g, riders)
    if distributed:
        w_glu, w_out, w_ple, w_pg = gathered
        w_glu = w_glu.reshape(D_SSM, D_SSM)
        w_out = w_out.reshape(D_MODEL, D_MODEL)
        w_pg = w_pg.reshape(D_MODEL, D_MODEL)
    yssm = _ssm_fwd(u_p, pw, bb, cc, dd)
    o, probs = _na_fwd(qkv, bias)
    dyssm, dzs, do, dzn, dh1, sq, d_wglu, d_bglu, d_wout, d_wple, d_wpg, d_gpost, d_gple = _mid(
        yssm, zs, o, zn, x2, p2, tgt, w_glu, b_glu, w_out, g_post, w_ple, g_ple, w_pg)
    mid_grads = [d_wglu.reshape(N_DEV, D_SSM // N_DEV, D_SSM), d_wout.reshape(N_DEV, D_MODEL // N_DEV, D_MODEL),
                 d_wple, d_wpg.reshape(N_DEV, D_MODEL // N_DEV, D_MODEL)]
    (dq, dk, dv, dbias), lands = _na_bwd(qkv, do, probs, mid_grads if distributed else ())
    (d_rpb,) = bias_vjp(dbias)
    chip_parts = _pair_sum(mid_grads, lands)[0] if distributed else ()
    (du_p, dpw, dbb, dcc, ddd), chip_lands = _ssm_bwd(u_p, dyssm, pw, bb, cc, dd, chip_parts)
    d_ssm = ssm_vjp((dpw, dbb, dcc, ddd))
    dx, d_win, d_gpre = _bwd_in(du_p, dzs, dq, dk, dv, dzn, hn, x2, dh1, g_pre, w_in_g)
    d_win = d_win.reshape(IN_GROUPS, D_MODEL, 2, SHARD_IN).transpose(0, 2, 1, 3).reshape(N_DEV, D_MODEL, SHARD_IN)
    if distributed:
        d_wglu, d_wout, d_wple, d_wpg = chip_lands
    return sq, dx, dict(norm_pre=d_gpre, norm_post=d_gpost, w_in=d_win, ssm=d_ssm, w_glu=d_wglu, b_glu=d_bglu,
                        na_rpb=d_rpb, w_out=d_wout, w_ple=d_wple, ple_norm=d_gple, w_ple_gate=d_wpg)


def _all_gather(shard):
    m_per, n = shard.shape

    def body(x_ref, out_ref, send_sems, recv_sems, local_sem):
        x, y, c = _place()
        me, sibling = (x, y, c), (x, y, 1 - c)
        chips = [(1 - x, y), (x, 1 - y), (1 - x, 1 - y)]

        def rows(px, py, pc):
            return out_ref.at[pl.ds((4 * px + 2 * py + pc) * m_per, m_per), :]

        def copy(k, block, to, src=None):
            return pltpu.make_async_remote_copy(
                src_ref=rows(*block) if src is None else src, dst_ref=rows(*block),
                send_sem=send_sems.at[k], recv_sem=recv_sems.at[k], device_id=to, device_id_type=MESH)

        mine = pltpu.make_async_copy(x_ref, rows(*me), local_sem)
        mine.start()
        first = [copy(0, me, sibling, src=x_ref)]
        first += [copy(1 + j, me, (*chip, c), src=x_ref) for j, chip in enumerate(chips)]
        for cp in first:
            cp.start()
        passed = [copy(4 + j, (*chip, c), sibling) for j, chip in enumerate(chips)]
        for j, chip in enumerate(chips):
            copy(1 + j, (*chip, c), me).wait_recv()
            passed[j].start()
        copy(0, sibling, me).wait_recv()
        for j, chip in enumerate(chips):
            copy(4 + j, (*chip, 1 - c), me).wait_recv()
        for cp in first + passed:
            cp.wait_send()
        mine.wait()

    return pl.pallas_call(
        body, name="all_gather",
        out_shape=jax.ShapeDtypeStruct((N_DEV * m_per, n), shard.dtype),
        in_specs=[_whole_vmem()], out_specs=_whole_vmem(),
        scratch_shapes=[pltpu.SemaphoreType.DMA((7,)), pltpu.SemaphoreType.DMA((7,)), pltpu.SemaphoreType.DMA],
        compiler_params=pltpu.CompilerParams(vmem_limit_bytes=VMEM_LIMIT),
    )(shard)


def _exchange_first(pair_parts, all_parts):
    n1, n2 = len(pair_parts), len(all_parts)

    def body(*refs):
        ins, refs = refs[:n1 + n2], refs[n1 + n2:]
        outs, sems = refs[:n1 + n2], refs[n1 + n2:]
        copies = (_pair_copies(ins[:n1], outs[:n1], *sems[:2])
                  + _alltoall_copies(ins[n1:], outs[n1:], *sems[2:]))
        _start_all(copies)
        _wait_all(copies)

    out = pl.pallas_call(
        body, name="exchange_first",
        out_shape=[jax.ShapeDtypeStruct((4,) + pt.shape[1:], pt.dtype) for pt in pair_parts]
        + [jax.ShapeDtypeStruct(pt.shape, pt.dtype) for pt in all_parts],
        in_specs=[_ANY()] * (n1 + n2), out_specs=[_ANY()] * (n1 + n2),
        scratch_shapes=[_dma_sems(n1, 4), _dma_sems(n1, 4),
                        _dma_sems(n2, N_DEV - 1), _dma_sems(n2, N_DEV - 1), _dma_sems(n2)],
    )(*pair_parts, *all_parts)
    return list(out[:n1]), list(out[n1:])


def _pair_sum(parts, lands, all_lands=()):
    ns, na = len(parts), len(all_lands)

    def body(*refs):
        c = lax.axis_index("c")
        ins, outs = refs[:2 * ns + na], refs[2 * ns + na:]
        for part_ref, land_ref, out_ref in zip(ins[:ns], ins[ns:2 * ns], outs[:ns]):
            for q in range(4):
                out_ref[q] = (part_ref[q, c] + land_ref[q]).astype(BF16)
        for land_ref, out_ref in zip(ins[2 * ns:], outs[ns:]):
            acc = land_ref[0]
            for j in range(1, N_DEV):
                acc = acc + land_ref[j]
            out_ref[...] = acc

    out = pl.pallas_call(
        body, name="pair_sum",
        in_specs=[_whole_vmem()] * (2 * ns + na), out_specs=[_whole_vmem()] * (ns + na),
        out_shape=[jax.ShapeDtypeStruct(ld.shape, BF16) for ld in lands]
        + [jax.ShapeDtypeStruct(ld.shape[1:], ld.dtype) for ld in all_lands],
        compiler_params=pltpu.CompilerParams(vmem_limit_bytes=VMEM_LIMIT),
    )(*[pt.reshape((4, 2) + pt.shape[1:]) for pt in parts], *lands, *all_lands)
    return list(out[:ns]), list(out[ns:])


def _exchange_second(chip_parts, shards):
    n1, n2 = len(chip_parts), len(shards)

    def body(*refs):
        ins, refs = refs[:n1 + n2], refs[n1 + n2:]
        outs, sems = refs[:n1 + n2], refs[n1 + n2:]
        copies = (_chip_copies(ins[:n1], outs[:n1], *sems[:3])
                  + _gather_copies(ins[n1:], outs[n1:], *sems[3:]))
        _start_all(copies)
        _wait_all(copies)

    out = pl.pallas_call(
        body, name="exchange_second",
        out_shape=[jax.ShapeDtypeStruct(pt.shape, pt.dtype) for pt in chip_parts]
        + [jax.ShapeDtypeStruct((N_DEV,) + sh.shape, sh.dtype) for sh in shards],
        in_specs=[_ANY()] * (n1 + n2), out_specs=[_ANY()] * (n1 + n2),
        scratch_shapes=[_dma_sems(n1, 3), _dma_sems(n1, 3), _dma_sems(n1),
                        _dma_sems(n2, N_DEV - 1), _dma_sems(n2, N_DEV - 1), _dma_sems(n2)],
    )(*chip_parts, *shards)
    return list(out[:n1]), list(out[n1:])


ADAM_ROWS = 256


def _adamw(ws, gs, ms, vs, slab=None):
    n = len(ws)
    arrays = [g for g in gs if not isinstance(g, int)]
    n_in = 3 * n + len(arrays) + (slab is not None)

    def body(*refs):
        ins, outs = refs[:n_in], refs[n_in:]
        w_refs, m_refs, v_refs = ins[:n], ins[n:2 * n], ins[2 * n:3 * n]
        g_refs = iter(ins[3 * n:3 * n + len(arrays)])
        for t in range(n):
            w_ref, m_ref, v_ref = w_refs[t], m_refs[t], v_refs[t]
            go_ref, d_ref, nm_ref, nv_ref = outs[4 * t:4 * t + 4]
            g_ref = None if isinstance(gs[t], int) else next(g_refs)
            rows = w_ref.shape[0]
            for lo in range(0, rows, ADAM_ROWS):
                hi = min(lo + ADAM_ROWS, rows)
                r = slice(lo, hi)
                if g_ref is None:
                    g = ins[-1][gs[t] + lo:gs[t] + hi, :]
                elif len(g_ref.shape) == 3:
                    g = g_ref[0, r, :].astype(F32)
                    for j in range(1, g_ref.shape[0]):
                        g = g + g_ref[j, r, :].astype(F32)
                else:
                    g = g_ref[r, :]
                go_ref[r, :] = g
                nm = ADAM_B1 * m_ref[r, :] + (1.0 - ADAM_B1) * g
                nv = ADAM_B2 * v_ref[r, :] + (1.0 - ADAM_B2) * (g * g)
                m_hat = nm / (1.0 - ADAM_B1 ** ADAM_STEP)
                v_hat = nv / (1.0 - ADAM_B2 ** ADAM_STEP)
                d_ref[r, :] = -ADAM_LR * (m_hat / (jnp.sqrt(v_hat) + ADAM_EPS) + ADAM_WD * w_ref[r, :])
                nm_ref[r, :] = nm
                nv_ref[r, :] = nv

    out = pl.pallas_call(
        body, name="adamw",
        in_specs=[_whole_vmem()] * n_in, out_specs=[_whole_vmem()] * (4 * n),
        out_shape=[jax.ShapeDtypeStruct(w.shape, F32) for w in ws for _ in range(4)],
        compiler_params=pltpu.CompilerParams(vmem_limit_bytes=VMEM_LIMIT),
    )(*ws, *ms, *vs, *arrays, *([slab] if slab is not None else []))
    return [tuple(out[4 * t:4 * t + 4]) for t in range(n)]


_SLAB = 8 * LANES


def _flat_rows(a):
    flat = a.reshape(-1)
    pad = (-flat.shape[0]) % _SLAB
    if pad:
        flat = jnp.concatenate([flat, jnp.zeros((pad,), flat.dtype)])
    return flat.reshape(-1, LANES)


def _pack(arrays):
    slabs = [_flat_rows(a) for a in arrays]
    return jnp.concatenate(slabs, axis=0), [s.shape[0] for s in slabs]


SMALL = ('norm_pre', 'norm_post', 'ssm_a_re', 'ssm_a_im', 'ssm_log_dt', 'ssm_b_re', 'ssm_b_im', 'ssm_c_re',
         'ssm_c_im', 'ssm_d', 'b_glu', 'na_rpb', 'ple_norm')
BIG = ('w_in', 'w_glu', 'w_out', 'w_ple', 'w_ple_gate')
ORDER = ('norm_pre', 'norm_post', 'w_in', 'ssm_a_re', 'ssm_a_im', 'ssm_log_dt', 'ssm_b_re', 'ssm_b_im', 'ssm_c_re',
         'ssm_c_im', 'ssm_d', 'w_glu', 'b_glu', 'na_rpb', 'w_out', 'w_ple', 'ple_norm', 'w_ple_gate')


def kernel(x, p, norm_pre, norm_post, w_in, ssm_a_re, ssm_a_im, ssm_log_dt, ssm_b_re, ssm_b_im, ssm_c_re, ssm_c_im, ssm_d, w_glu, b_glu, na_rpb, w_out, w_ple, ple_norm, w_ple_gate, loss_target, m_norm_pre, m_norm_post, m_w_in, m_ssm_a_re, m_ssm_a_im, m_ssm_log_dt, m_ssm_b_re, m_ssm_b_im, m_ssm_c_re, m_ssm_c_im, m_ssm_d, m_w_glu, m_b_glu, m_na_rpb, m_w_out, m_w_ple, m_ple_norm, m_w_ple_gate, v_norm_pre, v_norm_post, v_w_in, v_ssm_a_re, v_ssm_a_im, v_ssm_log_dt, v_ssm_b_re, v_ssm_b_im, v_ssm_c_re, v_ssm_c_im, v_ssm_d, v_w_glu, v_b_glu, v_na_rpb, v_w_out, v_w_ple, v_ple_norm, v_w_ple_gate):
    args = dict(locals())
    weights = {n: args[n] for n in ORDER}
    mom_m = {n: args["m_" + n] for n in ORDER}
    mom_v = {n: args["v_" + n] for n in ORDER}

    w_in_g = _all_gather(w_in[0].astype(BF16)).reshape(N_DEV, D_MODEL, SHARD_IN)
    blocks = [weights[n][0].astype(BF16) for n in ('w_glu', 'w_out', 'w_ple', 'w_ple_gate')]

    ssm = tuple(weights[n][0] for n in ('ssm_a_re', 'ssm_a_im', 'ssm_log_dt', 'ssm_b_re', 'ssm_b_im',
                                        'ssm_c_re', 'ssm_c_im', 'ssm_d'))
    sq, dx, grads = _local_step(x[0], p[0, 0], loss_target[0], norm_pre, norm_post, w_in_g, ssm, blocks[0], b_glu,
                                na_rpb[0], blocks[1], blocks[2], ple_norm, blocks[3], distributed=True)
    loss_local = (0.5 / D_MODEL * jnp.sum(sq)).reshape(1)

    local = dict(norm_pre=grads['norm_pre'], norm_post=grads['norm_post'], b_glu=grads['b_glu'],
                 na_rpb=grads['na_rpb'][None], ple_norm=grads['ple_norm'])
    for n, g in zip(('ssm_a_re', 'ssm_a_im', 'ssm_log_dt', 'ssm_b_re', 'ssm_b_im', 'ssm_c_re', 'ssm_c_im', 'ssm_d'),
                    grads['ssm']):
        local[n] = g[None]

    small_flat, _ = _pack([local[n] for n in SMALL] + [loss_local])
    pad = (-small_flat.shape[0]) % (8 * N_DEV)
    if pad:
        small_flat = jnp.concatenate([small_flat, jnp.zeros((pad, LANES), F32)], axis=0)
    small_per = small_flat.shape[0] // N_DEV
    d_win = [grads['w_in']]
    lands, small_lands = _exchange_first(d_win, [small_flat.reshape(N_DEV, small_per, LANES)])
    chip_parts, small_sums = _pair_sum(d_win, lands, small_lands)
    chip_lands, small_full = _exchange_second(chip_parts, small_sums)
    grads['w_in'] = chip_lands[0]
    small_all = small_full[0].reshape(N_DEV * small_per, LANES)

    like = [weights[n] for n in SMALL]
    slabs = [[_flat_rows(t[n]) for n in SMALL] for t in (weights, mom_m, mom_v)]
    starts, at = [], 0
    for w_rows in slabs[0]:
        starts.append(at)
        at += w_rows.shape[0]
    loss = small_all[at, 0]
    results = _adamw([weights[n][0] for n in BIG] + slabs[0], [grads[n] for n in BIG] + starts,
                     [mom_m[n][0] for n in BIG] + slabs[1], [mom_v[n][0] for n in BIG] + slabs[2], slab=small_all)
    outs = {n: tuple(a[None] for a in res) for n, res in zip(BIG, results)}
    for n, a, res in zip(SMALL, like, results[len(BIG):]):
        outs[n] = tuple(r.reshape(-1)[:a.size].reshape(a.shape) for r in res)

    return (loss, dx[None], *[outs[n][0] for n in ORDER], *[outs[n][1] for n in ORDER],
            *[outs[n][2] for n in ORDER], *[outs[n][3] for n in ORDER])
```

```python
import functools

import jax
import jax.numpy as jnp
from jax import lax
from jax.experimental import pallas as pl
from jax.experimental.pallas import tpu as pltpu

F32 = jnp.float32
BF16 = jnp.bfloat16
HIGHEST = lax.Precision.HIGHEST

D_MODEL = 1024
D_PLE = 256
GRID_W = 64
D_SSM = 512
SSM_GROUP = 16
N_PAIRS = 16
SSM_STATE = 64
D_NA = 512
NA_HEADS = 8
NA_HEAD_DIM = 64
NA_ROWS = 8
NA_COLS = 16
D_IN_PROJ = 3072
EPS = 1e-6
N_DEV = 8
SHARD_IN = D_IN_PROJ // N_DEV
IN_GROUPS = N_DEV // 2
GROUP_W = 2 * SHARD_IN
LANES = 128
SSM_CHUNK = 16
PW_ROWS = 24
TOK_TILE = 256
NA_QROWS = 4
NA_KROWS = 12
NEG = -1e30
VMEM_LIMIT = 56 * 1024 * 1024

ADAM_LR = 0.001
ADAM_B1 = 0.9
ADAM_B2 = 0.999
ADAM_EPS = 1e-08
ADAM_WD = 0.01
ADAM_STEP = 10

MESH = pl.DeviceIdType.MESH


def _params(*sem):
    return pltpu.CompilerParams(dimension_semantics=sem or None, vmem_limit_bytes=VMEM_LIMIT)


def _whole_vmem():
    return pl.BlockSpec(memory_space=pltpu.VMEM)


def _nt(a, b):
    return lax.dot_general(a, b, (((1,), (1,)), ((), ())), preferred_element_type=F32)


def _tn(a, b):
    return lax.dot_general(a, b, (((0,), (0,)), ((), ())), preferred_element_type=F32)


def _mm(a, b):
    return jnp.dot(a, b, preferred_element_type=F32)


def _sigmoid(x):
    return 1.0 / (1.0 + jnp.exp(-x))


_GELU_C = 0.7978845608028654


def _gelu(x):
    return 0.5 * x * (1.0 + jnp.tanh(_GELU_C * (x + 0.044715 * x * x * x)))


def _gelu_grad(x):
    th = jnp.tanh(_GELU_C * (x + 0.044715 * x * x * x))
    return 0.5 * (1.0 + th) + 0.5 * x * (1.0 - th * th) * _GELU_C * (1.0 + 3.0 * 0.044715 * x * x)


def _ssm_tables(a_re, a_im, log_dt, b_re, b_im, c_re, c_im, d):
    T, P, H = SSM_CHUNK, SSM_STATE, SSM_GROUP
    dt = jnp.exp(log_dt)[..., None]
    xr = dt * a_re
    xi = dt * a_im
    mag = jnp.exp(xr)
    lr = mag * jnp.cos(xi)
    li = mag * jnp.sin(xi)
    den = a_re * a_re + a_im * a_im
    cr = ((lr - 1.0) * a_re + li * a_im) / den
    ci = (li * a_re - (lr - 1.0) * a_im) / den
    bbr = cr[..., None] * b_re - ci[..., None] * b_im
    bbi = cr[..., None] * b_im + ci[..., None] * b_re
    kk = jnp.arange(T + 1, dtype=F32)[:, None, None, None]
    pm = jnp.exp(kk * xr)
    pw = jnp.stack([pm * jnp.cos(kk * xi), pm * jnp.sin(kk * xi)], axis=2)
    pw = pw.reshape(T + 1, 2, 2, N_PAIRS, 2 * P).transpose(3, 0, 1, 2, 4).reshape(N_PAIRS, T + 1, 8 * P)
    pw = jnp.concatenate([pw, jnp.zeros((N_PAIRS, PW_ROWS - (T + 1), 8 * P), F32)], axis=1)
    eye2 = jnp.eye(2, dtype=F32)

    def expand(t):
        t = t.transpose(2, 0, 1, 3, 4, 5)
        t = t[:, :, :, :, :, None, :] * eye2[None, None, None, :, None, :, None]
        return t.reshape(N_PAIRS, 4, 2 * H, 2 * P)

    bb = expand(jnp.stack([bbr, bbi], axis=1).reshape(2, 2, N_PAIRS, 2, P, H).transpose(0, 1, 2, 3, 5, 4))
    cc = expand(jnp.stack([c_re, c_im], axis=1).reshape(2, 2, N_PAIRS, 2, H, P))
    dd = d.reshape(N_PAIRS, 2 * H)[:, :, None] * jnp.eye(2 * H, dtype=F32)[None]
    dd = jnp.concatenate([dd, jnp.zeros((N_PAIRS, 2 * H, LANES - 2 * H), F32)], axis=2)
    return pw, bb, cc, dd


NA_TAB = 2 * NA_ROWS


def _na_table(rpb):
    qc = jnp.arange(GRID_W)[:, None, None]
    kc = (jnp.arange(2 * GRID_W) % GRID_W)[None, :, None]
    dc = jnp.arange(2 * NA_COLS - 1)[None, None, :]
    cstart = jnp.clip(qc - NA_COLS // 2, 0, GRID_W - NA_COLS)
    csel = ((kc >= cstart) & (kc < cstart + NA_COLS) & (kc - qc + NA_COLS - 1 == dc)).astype(F32)
    col_ok = jnp.sum(csel, axis=-1) > 0.5
    part = jnp.einsum('hrc,qmc->hrqm', rpb, csel, precision=HIGHEST)
    zero = jnp.zeros_like(part[:, :1])
    odd = jnp.arange(2 * GRID_W) >= GRID_W
    tab = jnp.where(odd, jnp.concatenate([part, zero], axis=1), jnp.concatenate([zero, part], axis=1))
    return jnp.where(col_ok, tab, NEG)


def _place():
    return lax.axis_index("x"), lax.axis_index("y"), lax.axis_index("c")


def _remote(src, dst, send_sem, recv_sem, device):
    return pltpu.make_async_remote_copy(src_ref=src, dst_ref=dst, send_sem=send_sem, recv_sem=recv_sem,
                                        device_id=device, device_id_type=MESH)


def _start_all(copies):
    for cp in copies:
        cp.start()


def _wait_all(copies):
    for cp in copies:
        cp.wait()


def _gather_copies(shard_refs, full_refs, send_sems, recv_sems, local_sems):
    x, y, c = _place()
    me = 4 * x + 2 * y + c
    out = []
    for t, (shard, full) in enumerate(zip(shard_refs, full_refs)):
        out.append(pltpu.make_async_copy(shard, full.at[me], local_sems.at[t]))
        for k in range(1, N_DEV):
            peer = (x ^ ((k >> 2) & 1), y ^ ((k >> 1) & 1), c ^ (k & 1))
            out.append(_remote(shard, full.at[me], send_sems.at[t, k - 1], recv_sems.at[t, k - 1], peer))
    return out


def _pair_copies(part_refs, land_refs, send_sems, recv_sems):
    x, y, c = _place()
    out = []
    for t, (part, land) in enumerate(zip(part_refs, land_refs)):
        for q in range(4):
            out.append(_remote(part.at[2 * q + (1 - c)], land.at[q], send_sems.at[t, q], recv_sems.at[t, q],
                               (x, y, 1 - c)))
    return out


def _chip_copies(part_refs, land_refs, send_sems, recv_sems, local_sems):
    x, y, c = _place()
    mine = 2 * x + y
    out = []
    for t, (part, land) in enumerate(zip(part_refs, land_refs)):
        out.append(pltpu.make_async_copy(part.at[mine], land.at[mine], local_sems.at[t]))
        for k in range(1, 4):
            px, py = x ^ (k >> 1), y ^ (k & 1)
            out.append(_remote(part.at[2 * px + py], land.at[mine], send_sems.at[t, k - 1], recv_sems.at[t, k - 1],
                               (px, py, c)))
    return out


def _alltoall_copies(part_refs, land_refs, send_sems, recv_sems, local_sems):
    x, y, c = _place()
    me = 4 * x + 2 * y + c
    out = []
    for t, (part, land) in enumerate(zip(part_refs, land_refs)):
        out.append(pltpu.make_async_copy(part.at[me], land.at[me], local_sems.at[t]))
        for k in range(1, N_DEV):
            px, py, pc = x ^ ((k >> 2) & 1), y ^ ((k >> 1) & 1), c ^ (k & 1)
            out.append(_remote(part.at[4 * px + 2 * py + pc], land.at[me], send_sems.at[t, k - 1],
                               recv_sems.at[t, k - 1], (px, py, pc)))
    return out


def _ride(copies, first, last):
    pl.when(first)(functools.partial(_start_all, copies))
    pl.when(last)(functools.partial(_wait_all, copies))


_ANY = lambda: pl.BlockSpec(memory_space=pl.ANY)


def _dma_sems(*shape):
    return pltpu.SemaphoreType.DMA(shape)


PAIR_W = 2 * SSM_GROUP
PAIRS_PER_BLOCK = LANES // PAIR_W
CHUNK_W = SSM_CHUNK * PAIR_W
TILE_CHUNKS = TOK_TILE // SSM_CHUNK


def _lane_window(lo, width):
    lane = lax.broadcasted_iota(jnp.int32, (1, LANES), 1)
    return (lane >= lo) & (lane < lo + width)


def _tile_to_pairs(src_ref, out_ref):
    for cl in range(D_SSM // LANES):
        xs = [src_ref[cl, pl.ds(s, TILE_CHUNKS, stride=SSM_CHUNK), :] for s in range(SSM_CHUNK)]
        for a_ in range(PAIRS_PER_BLOCK):
            for v in range(SSM_CHUNK // PAIRS_PER_BLOCK):
                acc = None
                for i in range(PAIRS_PER_BLOCK):
                    shift = (PAIR_W * (i - a_)) % LANES
                    piece = xs[PAIRS_PER_BLOCK * v + i]
                    piece = pltpu.roll(piece, shift, axis=1) if shift else piece
                    acc = piece if acc is None else jnp.where(_lane_window(PAIR_W * i, PAIR_W), piece, acc)
                out_ref[PAIRS_PER_BLOCK * cl + a_, :, LANES * v:LANES * (v + 1)] = acc.astype(out_ref.dtype)


def _tile_from_pairs(in_ref, dst_ref):
    for cl in range(D_SSM // LANES):
        for s in range(SSM_CHUNK):
            v, i = divmod(s, PAIRS_PER_BLOCK)
            acc = None
            for a_ in range(PAIRS_PER_BLOCK):
                shift = (PAIR_W * (a_ - i)) % LANES
                piece = in_ref[PAIRS_PER_BLOCK * cl + a_, :, LANES * v:LANES * (v + 1)].astype(F32)
                piece = pltpu.roll(piece, shift, axis=1) if shift else piece
                acc = piece if acc is None else jnp.where(_lane_window(PAIR_W * a_, PAIR_W), piece, acc)
            dst_ref[cl, pl.ds(s, TILE_CHUNKS, stride=SSM_CHUNK), :] = acc


def _pair_shards(w_ref, w_scr):
    for g in range(IN_GROUPS):
        w_scr[g, :, 0:SHARD_IN] = w_ref[2 * g]
        w_scr[g, :, SHARD_IN:GROUP_W] = w_ref[2 * g + 1]


def _pair_tile():
    return pl.BlockSpec((N_PAIRS, TILE_CHUNKS, CHUNK_W), lambda i: (0, i, 0))


def _tile_scratch():
    return pltpu.VMEM((D_SSM // LANES, TOK_TILE, LANES), F32)


def _fwd_in(x2, g_pre, w_in_g, shards=()):
    seq = x2.shape[0]
    ns = len(shards)
    steps = seq // TOK_TILE

    def body(x_ref, g_ref, w_ref, *rest):
        shard_refs, rest = rest[:ns], rest[ns:]
        (u_ref, zs_ref, qkv_ref, zn_ref, hn_ref), rest = rest[:5], rest[5:]
        (w_scr, u_scr), rest = rest[-2:], rest[:-2]
        i = pl.program_id(0)
        if ns:
            _ride(_gather_copies(shard_refs, rest[:ns], *rest[ns:]), i == 0, i == steps - 1)
        pl.when(i == 0)(functools.partial(_pair_shards, w_ref, w_scr))
        x = x_ref[...]
        r = lax.rsqrt(jnp.mean(x * x, axis=-1, keepdims=True) + EPS)
        hn = (x * r * g_ref[...]).astype(BF16)
        hn_ref[...] = hn
        for j in range(IN_GROUPS):
            pj = _mm(hn, w_scr[j])
            for i in range(GROUP_W // LANES):
                blk = (GROUP_W // LANES) * j + i
                piece = pj[:, i * LANES:(i + 1) * LANES]
                if blk < 4:
                    u_scr[blk] = piece
                elif blk < 8:
                    zs_ref[:, (blk - 4) * LANES:(blk - 3) * LANES] = piece
                elif blk < 20:
                    qkv_ref[:, (blk - 8) * LANES:(blk - 7) * LANES] = piece.astype(BF16)
                else:
                    zn_ref[:, (blk - 20) * LANES:(blk - 19) * LANES] = piece
        _tile_to_pairs(u_scr, u_ref)

    tok = lambda w: pl.BlockSpec((TOK_TILE, w), lambda i: (i, 0))
    out = pl.pallas_call(
        body, name="fwd_in", grid=(steps,),
        in_specs=[tok(D_MODEL), pl.BlockSpec((1, D_MODEL), lambda i: (0, 0)), _whole_vmem()] + [_ANY()] * ns,
        out_specs=[_pair_tile(), tok(D_SSM), tok(3 * D_NA), tok(D_NA), tok(D_MODEL)] + [_ANY()] * ns,
        out_shape=[jax.ShapeDtypeStruct((N_PAIRS, seq // SSM_CHUNK, CHUNK_W), BF16), jax.ShapeDtypeStruct((seq, D_SSM), F32),
                   jax.ShapeDtypeStruct((seq, 3 * D_NA), BF16), jax.ShapeDtypeStruct((seq, D_NA), F32),
                   jax.ShapeDtypeStruct((seq, D_MODEL), BF16)]
        + [jax.ShapeDtypeStruct((N_DEV,) + sh.shape, sh.dtype) for sh in shards],
        scratch_shapes=([_dma_sems(ns, N_DEV - 1), _dma_sems(ns, N_DEV - 1), _dma_sems(ns)] if ns else [])
        + [pltpu.VMEM((IN_GROUPS, D_MODEL, GROUP_W), BF16), _tile_scratch()],
        compiler_params=_params("arbitrary"),
    )(x2, g_pre, w_in_g, *shards)
    return out[:5], list(out[5:])


def _mid(yssm, zs, o, zn, x2, p2, tgt, w_glu, b_glu, w_out, g_post, w_ple_g, g_ple, w_pg):
    seq = x2.shape[0]

    def body(yssm_ref, zs_ref, o_ref, zn_ref, x_ref, p_ref, tgt_ref, wglu_ref, bglu_ref, wout_ref, gpost_ref,
             wple_ref, gple_ref, wpg_ref,
             dyssm_ref, dzs_ref, do_ref, dzn_ref, dh1_ref, loss_ref, dwglu_ref, dbglu_ref, dwout_ref, dwple_ref,
             dwpg_ref, dgpost_ref, dgple_ref, y_scr, dy_scr):
        @pl.when(pl.program_id(0) == 0)
        def _():
            for ref in (loss_ref, dwglu_ref, dbglu_ref, dwout_ref, dwple_ref, dwpg_ref, dgpost_ref, dgple_ref):
                ref[...] = jnp.zeros_like(ref)

        _tile_from_pairs(yssm_ref, y_scr)
        yv = jnp.concatenate([y_scr[cl] for cl in range(D_SSM // LANES)], axis=1)
        g1 = _gelu(yv)
        g1b = g1.astype(BF16)
        sg = _sigmoid(_mm(g1b, wglu_ref[...]) + bglu_ref[...])
        zs_v = zs_ref[...]
        s_zs = _sigmoid(zs_v)
        g2 = g1 * sg
        zn_v = zn_ref[...]
        s_zn = _sigmoid(zn_v)
        o_v = o_ref[...]
        cat = jnp.concatenate([g2 * (zs_v * s_zs), o_v * (zn_v * s_zn)], axis=1).astype(BF16)
        mix = _mm(cat, wout_ref[...])
        r2 = lax.rsqrt(jnp.mean(mix * mix, axis=-1, keepdims=True) + EPS)
        n2 = mix * r2
        gpost = gpost_ref[...]
        h1 = x_ref[...] + n2 * gpost
        pb = p_ref[...].astype(BF16)
        epre = jnp.concatenate([_mm(pb, wple_ref[j]) for j in range(N_DEV)], axis=1)
        r3 = lax.rsqrt(jnp.mean(epre * epre, axis=-1, keepdims=True) + EPS)
        n3 = epre * r3
        gple = gple_ref[...]
        e = n3 * gple
        h1b = h1.astype(BF16)
        gate = _sigmoid(_mm(h1b, wpg_ref[...]))
        diff = h1 + gate * e - tgt_ref[...]
        loss_ref[...] += jnp.sum(diff * diff, axis=0, keepdims=True)

        dy = diff * (1.0 / D_MODEL)
        dgp = (dy * e * gate * (1.0 - gate)).astype(BF16)
        de = dy * gate
        dh1 = dy + _nt(dgp, wpg_ref[...])
        dh1_ref[...] = dh1
        dwpg_ref[...] += _tn(h1b, dgp)
        dgple_ref[...] += jnp.sum(de * n3, axis=0, keepdims=True)
        dn3 = de * gple
        depre = (r3 * (dn3 - n3 * jnp.mean(dn3 * n3, axis=-1, keepdims=True))).astype(BF16)
        for j in range(N_DEV):
            dwple_ref[j] += _tn(pb, depre[:, j * LANES:(j + 1) * LANES])
        dgpost_ref[...] += jnp.sum(dh1 * n2, axis=0, keepdims=True)
        dn2 = dh1 * gpost
        dmix = (r2 * (dn2 - n2 * jnp.mean(dn2 * n2, axis=-1, keepdims=True))).astype(BF16)
        dcat = _nt(dmix, wout_ref[...])
        dwout_ref[...] += _tn(cat, dmix)

        dys = dcat[:, :D_SSM]
        dyn = dcat[:, D_SSM:]
        dg2 = dys * (zs_v * s_zs)
        dzs_ref[...] = (dys * g2 * (s_zs * (1.0 + zs_v * (1.0 - s_zs)))).astype(BF16)
        dt = dg2 * g2 * (1.0 - sg)
        dtb = dt.astype(BF16)
        dg1 = dg2 * sg + _nt(dtb, wglu_ref[...])
        dwglu_ref[...] += _tn(g1b, dtb)
        dbglu_ref[...] += jnp.sum(dt, axis=0, keepdims=True)
        dyssm = dg1 * _gelu_grad(yv)
        for cl in range(D_SSM // LANES):
            dy_scr[cl] = dyssm[:, cl * LANES:(cl + 1) * LANES]
        _tile_to_pairs(dy_scr, dyssm_ref)
        do_ref[...] = (dyn * (zn_v * s_zn)).astype(BF16)
        dzn_ref[...] = (dyn * o_v * (s_zn * (1.0 + zn_v * (1.0 - s_zn)))).astype(BF16)

    tok = lambda w: pl.BlockSpec((TOK_TILE, w), lambda i: (i, 0))
    row = lambda w: pl.BlockSpec((1, w), lambda i: (0, 0))
    vm = _whole_vmem()
    half_b = jax.ShapeDtypeStruct((seq, D_SSM), BF16)
    gain = jax.ShapeDtypeStruct((1, D_MODEL), F32)
    return pl.pallas_call(
        body, name="mid", grid=(seq // TOK_TILE,),
        in_specs=[_pair_tile(), tok(D_SSM), tok(D_NA), tok(D_NA), tok(D_MODEL), tok(D_PLE), tok(D_MODEL),
                  vm, row(D_SSM), vm, row(D_MODEL), vm, row(D_MODEL), vm],
        out_specs=[_pair_tile(), tok(D_SSM), tok(D_NA), tok(D_NA), tok(D_MODEL), vm, vm, vm, vm, vm, vm, vm, vm],
        out_shape=[jax.ShapeDtypeStruct((N_PAIRS, seq // SSM_CHUNK, CHUNK_W), BF16), half_b, half_b, half_b, jax.ShapeDtypeStruct((seq, D_MODEL), F32), gain,
                   jax.ShapeDtypeStruct((D_SSM, D_SSM), F32), jax.ShapeDtypeStruct((1, D_SSM), F32),
                   jax.ShapeDtypeStruct((D_MODEL, D_MODEL), F32),
                   jax.ShapeDtypeStruct((N_DEV, D_PLE, LANES), F32),
                   jax.ShapeDtypeStruct((D_MODEL, D_MODEL), F32), gain, gain],
        scratch_shapes=[_tile_scratch(), _tile_scratch()],
        compiler_params=_params("arbitrary"),
    )(yssm, zs, o, zn, x2, p2, tgt, w_glu, b_glu, w_out, g_post, w_ple_g, g_ple, w_pg)


def _bwd_in(du, dzs, dq, dk, dv, dzn, hn, x2, dh1, g_pre, w_in_g):
    seq = x2.shape[0]

    def body(du_ref, dzs_ref, dq_ref, dk_ref, dv_ref, dzn_ref, hn_ref, x_ref, dh1_ref, g_ref, w_ref,
             dx_ref, dw_ref, dg_ref, dproj_ref, du_scr, w_scr):
        @pl.when(pl.program_id(0) == 0)
        def _():
            dw_ref[...] = jnp.zeros_like(dw_ref)
            dg_ref[...] = jnp.zeros_like(dg_ref)
            _pair_shards(w_ref, w_scr)

        _tile_from_pairs(du_ref, du_scr)
        for cl in range(D_SSM // LANES):
            dproj_ref[:, cl * LANES:(cl + 1) * LANES] = du_scr[cl].astype(BF16)
        for k, ref in enumerate((dzs_ref, dq_ref, dk_ref, dv_ref, dzn_ref), start=1):
            dproj_ref[:, k * D_SSM:(k + 1) * D_SSM] = ref[...].astype(BF16)
        hn = hn_ref[...]
        dhn = jnp.zeros((TOK_TILE, D_MODEL), F32)
        for j in range(IN_GROUPS):
            dpj = dproj_ref[:, j * GROUP_W:(j + 1) * GROUP_W]
            dhn += _nt(dpj, w_scr[j])
            dw = _tn(hn, dpj)
            dw_ref[2 * j] += dw[:, 0:SHARD_IN]
            dw_ref[2 * j + 1] += dw[:, SHARD_IN:GROUP_W]
        x = x_ref[...]
        r = lax.rsqrt(jnp.mean(x * x, axis=-1, keepdims=True) + EPS)
        n1 = x * r
        dg_ref[...] += jnp.sum(dhn * n1, axis=0, keepdims=True)
        dn1 = dhn * g_ref[...]
        dx_ref[...] = dh1_ref[...] + r * (dn1 - n1 * jnp.mean(dn1 * n1, axis=-1, keepdims=True))

    tok = lambda w: pl.BlockSpec((TOK_TILE, w), lambda i: (i, 0))
    vm = _whole_vmem()
    return pl.pallas_call(
        body, name="bwd_in", grid=(seq // TOK_TILE,),
        in_specs=[_pair_tile()] + [tok(D_SSM)] * 5 + [tok(D_MODEL), tok(D_MODEL), tok(D_MODEL),
                                      pl.BlockSpec((1, D_MODEL), lambda i: (0, 0)), vm],
        out_specs=[tok(D_MODEL), vm, vm],
        out_shape=[jax.ShapeDtypeStruct((seq, D_MODEL), F32),
                   jax.ShapeDtypeStruct((N_DEV, D_MODEL, SHARD_IN), F32),
                   jax.ShapeDtypeStruct((1, D_MODEL), F32)],
        scratch_shapes=[pltpu.VMEM((TOK_TILE, D_IN_PROJ), BF16), _tile_scratch(),
                        pltpu.VMEM((IN_GROUPS, D_MODEL, GROUP_W), BF16)],
        compiler_params=_params("arbitrary"),
    )(du, dzs, dq, dk, dv, dzn, hn, x2, dh1, g_pre, w_in_g)


SSM_PAIRS = 4
SLOT = 8


def _boundary_scan(nc, pw_ref, buf_ref, conj):
    pad = [jnp.zeros((SLOT - SSM_PAIRS, LANES), F32)]
    lr0, li0, lr1, li1 = (jnp.concatenate([_pw_row(pw_ref, j, SSM_CHUNK, q) for j in range(SSM_PAIRS)] + pad, axis=0)
                          for q in range(4))
    if conj:
        li0, li1 = -li0, -li1

    def step(c, carry):
        hr0, hi0, hr1, hi1 = carry
        up = pl.ds(pl.multiple_of(c * SLOT, SLOT), SLOT)
        dn = pl.ds(pl.multiple_of((nc - 1 - c) * SLOT, SLOT), SLOT)
        ra, rb = (dn, up) if conj else (up, dn)
        s_r0, s_i0 = buf_ref[0, ra, :], buf_ref[1, ra, :]
        s_r1, s_i1 = buf_ref[2, rb, :], buf_ref[3, rb, :]
        buf_ref[0, ra, :] = hr0
        buf_ref[1, ra, :] = hi0
        buf_ref[2, rb, :] = hr1
        buf_ref[3, rb, :] = hi1
        return (lr0 * hr0 - li0 * hi0 + s_r0, lr0 * hi0 + li0 * hr0 + s_i0,
                lr1 * hr1 - li1 * hi1 + s_r1, lr1 * hi1 + li1 * hr1 + s_i1)

    z = jnp.zeros((SLOT, LANES), F32)
    lax.fori_loop(0, nc, step, (z, z, z, z), unroll=8)


def _put_pair(buf_ref, j, nc, val):
    for q in range(4):
        buf_ref[q, pl.ds(j, nc, stride=SLOT), :] = val[:, LANES * q:LANES * (q + 1)]


def _get_pair(buf_ref, j, nc):
    return jnp.concatenate([buf_ref[q, pl.ds(j, nc, stride=SLOT), :] for q in range(4)], axis=1)


def _pw_row(pw_ref, j, k, q):
    return pw_ref[j, k:k + 1, LANES * q:LANES * (q + 1)]


def _mm_f32(a, b, dims):
    return lax.dot_general(a, b, (dims, ((), ())), precision=HIGHEST, preferred_element_type=F32)


_POW_M = (lambda s: SSM_CHUNK - 1 - s, lambda s: s)
_POW_C = (lambda s: s + 1, lambda s: SSM_CHUNK - s)
_POW_K = (lambda s: s, lambda s: SSM_CHUNK - 1 - s)


def _chunk_matrices(j, pw_ref, bb_ref, cc_ref, dd_ref, m_scr, ct_scr, toep_scr, g_scr, kt_scr):
    blk = lambda s: slice(PAIR_W * s, PAIR_W * (s + 1))
    col = lambda q: slice(LANES * q, LANES * (q + 1))
    for d in range(2):
        bbr, bbi = bb_ref[j, 2 * d], bb_ref[j, 2 * d + 1]
        ccr, cci = cc_ref[j, 2 * d], cc_ref[j, 2 * d + 1]
        for s in range(SSM_CHUNK):
            pr, pi = _pw_row(pw_ref, j, _POW_M[d](s), 2 * d), _pw_row(pw_ref, j, _POW_M[d](s), 2 * d + 1)
            m_scr[j, blk(s), col(2 * d)] = (pr * bbr - pi * bbi).astype(m_scr.dtype)
            m_scr[j, blk(s), col(2 * d + 1)] = (pr * bbi + pi * bbr).astype(m_scr.dtype)
            pr, pi = _pw_row(pw_ref, j, _POW_C[d](s), 2 * d), _pw_row(pw_ref, j, _POW_C[d](s), 2 * d + 1)
            ct_scr[j, blk(s), col(2 * d)] = (ccr * pr - cci * pi).astype(ct_scr.dtype)
            ct_scr[j, blk(s), col(2 * d + 1)] = (-(ccr * pi + cci * pr)).astype(ct_scr.dtype)
            pr, pi = _pw_row(pw_ref, j, _POW_K[d](s), 2 * d), _pw_row(pw_ref, j, _POW_K[d](s), 2 * d + 1)
            g_scr[j, d, blk(s), 0:LANES] = ccr * pr - cci * pi
            g_scr[j, d, blk(s), LANES:2 * LANES] = -(ccr * pi + cci * pr)
        kt = _mm_f32(jnp.concatenate([bbr, bbi], axis=1), g_scr[j, d], ((1,), (1,)))
        if d == 0:
            kt = jnp.concatenate([kt[:, 0:LANES] + dd_ref[j], kt[:, LANES:]], axis=1)
        kt_scr[d] = kt
    lane = lax.broadcasted_iota(jnp.int32, (1, CHUNK_W), 1)
    for s in range(SSM_CHUNK):
        lo = PAIR_W * s
        hi = PAIR_W * (s + 1)
        fwd = kt_scr[0] if s == 0 else pltpu.roll(kt_scr[0], lo, axis=1)
        bwd = kt_scr[1] if hi == CHUNK_W else pltpu.roll(kt_scr[1], hi, axis=1)
        row = jnp.where(lane >= lo, fwd, 0.0) + jnp.where(lane < hi, bwd, 0.0)
        toep_scr[j, blk(s), :] = row.astype(toep_scr.dtype)


def _ssm_scratch(nc, mat_dtype, buffers):
    mats = [pltpu.VMEM((SSM_PAIRS, CHUNK_W, CHUNK_W), mat_dtype) for _ in range(3)]
    return mats + [pltpu.VMEM((SSM_PAIRS, 2, CHUNK_W, 2 * LANES), F32), pltpu.VMEM((2, PAIR_W, CHUNK_W), F32)] + [
        pltpu.VMEM((4, nc * SLOT, LANES), F32) for _ in range(buffers)]


def _per_step(*shape):
    return pl.BlockSpec((SSM_PAIRS,) + shape, lambda g: (g,) + (0,) * len(shape))


_TABLE_SPECS = lambda: [_per_step(PW_ROWS, CHUNK_W), _per_step(4, PAIR_W, LANES), _per_step(4, PAIR_W, LANES),
                        _per_step(PAIR_W, LANES)]


def _ssm_fwd(u, pw, bb, cc, dd):
    npair, nc, width = u.shape

    def body(u_ref, pw_ref, bb_ref, cc_ref, dd_ref, y_ref, m_scr, ct_scr, toep_scr, g_scr, kt_scr, h_scr):
        @pl.when(pl.program_id(0) == 0)
        def _():
            h_scr[...] = jnp.zeros_like(h_scr)

        for j in range(SSM_PAIRS):
            _chunk_matrices(j, pw_ref, bb_ref, cc_ref, dd_ref, m_scr, ct_scr, toep_scr, g_scr, kt_scr)
            _put_pair(h_scr, j, nc, _mm(u_ref[j], m_scr[j]))
        _boundary_scan(nc, pw_ref, h_scr, conj=False)
        for j in range(SSM_PAIRS):
            hin = _get_pair(h_scr, j, nc)
            y_ref[j] = _mm(u_ref[j], toep_scr[j]) + _nt(hin.astype(u.dtype), ct_scr[j])

    return pl.pallas_call(
        body, name="ssm_fwd", grid=(npair // SSM_PAIRS,),
        in_specs=[_per_step(nc, width)] + _TABLE_SPECS(),
        out_specs=_per_step(nc, width),
        out_shape=jax.ShapeDtypeStruct((npair, nc, width), F32),
        scratch_shapes=_ssm_scratch(nc, u.dtype, 1),
        compiler_params=_params("arbitrary"),
    )(u, pw, bb, cc, dd)


def _ssm_bwd(u, dy, pw, bb, cc, dd, chip_parts=()):
    npair, nc, width = u.shape
    ns = len(chip_parts)
    steps = npair // SSM_PAIRS

    def body(u_ref, dy_ref, pw_ref, bb_ref, cc_ref, dd_ref, *rest):
        part_refs, rest = rest[:ns], rest[ns:]
        (du_ref, dpw_ref, dbb_ref, dcc_ref, ddd_ref), rest = rest[:5], rest[5:]
        land_refs, rest = rest[:ns], rest[ns:]
        m_scr, ct_scr, toep_scr, g_scr, kt_scr, h_scr, d_scr = rest[:7]
        g = pl.program_id(0)
        if ns:
            _ride(_chip_copies(part_refs, land_refs, *rest[7:]), g == 0, g == steps - 1)

        @pl.when(g == 0)
        def _():
            h_scr[...] = jnp.zeros_like(h_scr)
            d_scr[...] = jnp.zeros_like(d_scr)

        for j in range(SSM_PAIRS):
            _chunk_matrices(j, pw_ref, bb_ref, cc_ref, dd_ref, m_scr, ct_scr, toep_scr, g_scr, kt_scr)
            _put_pair(h_scr, j, nc, _mm(u_ref[j], m_scr[j]))
            _put_pair(d_scr, j, nc, _mm(dy_ref[j], ct_scr[j]))
        _boundary_scan(nc, pw_ref, h_scr, conj=False)
        _boundary_scan(nc, pw_ref, d_scr, conj=True)

        dpw_ref[...] = jnp.zeros_like(dpw_ref)
        blk = lambda s: slice(PAIR_W * s, PAIR_W * (s + 1))
        col = lambda q: slice(LANES * q, LANES * (q + 1))
        lane = lax.broadcasted_iota(jnp.int32, (1, CHUNK_W), 1)
        for j in range(SSM_PAIRS):
            uv = u_ref[j]
            dyb = dy_ref[j]
            hin = _get_pair(h_scr, j, nc)
            ds = _get_pair(d_scr, j, nc)
            dsb = ds.astype(u.dtype)
            du_ref[j] = _nt(dsb, m_scr[j]) + _nt(dyb, toep_scr[j])
            dm = _tn(uv, dsb)
            dct = _tn(dyb, hin.astype(u.dtype))
            dtoep = _tn(uv, dyb)

            def add_pw(k, q, val):
                dpw_ref[j, k:k + 1, col(q)] += jnp.sum(val, axis=0, keepdims=True)

            for d in range(2):
                g_r, g_i = ds[:, col(2 * d)], ds[:, col(2 * d + 1)]
                h_r, h_i = hin[:, col(2 * d)], hin[:, col(2 * d + 1)]
                add_pw(SSM_CHUNK, 2 * d, g_r * h_r + g_i * h_i)
                add_pw(SSM_CHUNK, 2 * d + 1, g_i * h_r - g_r * h_i)

            dkt0 = jnp.zeros((PAIR_W, CHUNK_W), F32)
            dkt1 = jnp.zeros((PAIR_W, CHUNK_W), F32)
            for s in range(SSM_CHUNK):
                lo = PAIR_W * s
                hi = PAIR_W * (s + 1)
                row = dtoep[blk(s), :]
                fwd = jnp.where(lane >= lo, row, 0.0)
                bwd = jnp.where(lane < hi, row, 0.0)
                dkt0 += fwd if s == 0 else pltpu.roll(fwd, CHUNK_W - lo, axis=1)
                dkt1 += bwd if hi == CHUNK_W else pltpu.roll(bwd, CHUNK_W - hi, axis=1)
            ddd_ref[j] = dkt0[:, 0:LANES]

            for d, dkt in enumerate((dkt0, dkt1)):
                bbr, bbi = bb_ref[j, 2 * d], bb_ref[j, 2 * d + 1]
                ccr, cci = cc_ref[j, 2 * d], cc_ref[j, 2 * d + 1]
                dbbcat = _mm_f32(dkt, g_scr[j, d], ((1,), (0,)))
                dg = _mm_f32(dkt, jnp.concatenate([bbr, bbi], axis=1), ((0,), (0,)))
                dbbr, dbbi = dbbcat[:, 0:LANES], dbbcat[:, LANES:]
                dccr = jnp.zeros((PAIR_W, LANES), F32)
                dcci = jnp.zeros((PAIR_W, LANES), F32)
                for s in range(SSM_CHUNK):
                    k = _POW_M[d](s)
                    pr, pi = _pw_row(pw_ref, j, k, 2 * d), _pw_row(pw_ref, j, k, 2 * d + 1)
                    gr, gi = dm[blk(s), col(2 * d)], dm[blk(s), col(2 * d + 1)]
                    dbbr += gr * pr + gi * pi
                    dbbi += gi * pr - gr * pi
                    add_pw(k, 2 * d, gr * bbr + gi * bbi)
                    add_pw(k, 2 * d + 1, gi * bbr - gr * bbi)
                    for k, gr, gi in ((_POW_C[d](s), dct[blk(s), col(2 * d)], dct[blk(s), col(2 * d + 1)]),
                                      (_POW_K[d](s), dg[blk(s), 0:LANES], dg[blk(s), LANES:])):
                        pr, pi = _pw_row(pw_ref, j, k, 2 * d), _pw_row(pw_ref, j, k, 2 * d + 1)
                        dccr += gr * pr - gi * pi
                        dcci += -(gr * pi + gi * pr)
                        add_pw(k, 2 * d, gr * ccr - gi * cci)
                        add_pw(k, 2 * d + 1, -(gr * cci + gi * ccr))
                dbb_ref[j, 2 * d] = dbbr
                dbb_ref[j, 2 * d + 1] = dbbi
                dcc_ref[j, 2 * d] = dccr
                dcc_ref[j, 2 * d + 1] = dcci

    out = pl.pallas_call(
        body, name="ssm_bwd", grid=(steps,),
        in_specs=[_per_step(nc, width), _per_step(nc, width)] + _TABLE_SPECS() + [_ANY()] * ns,
        out_specs=[_per_step(nc, width)] + _TABLE_SPECS() + [_ANY()] * ns,
        out_shape=[jax.ShapeDtypeStruct((npair, nc, width), F32), jax.ShapeDtypeStruct(pw.shape, F32),
                   jax.ShapeDtypeStruct(bb.shape, F32), jax.ShapeDtypeStruct(cc.shape, F32),
                   jax.ShapeDtypeStruct(dd.shape, F32)]
        + [jax.ShapeDtypeStruct(pt.shape, pt.dtype) for pt in chip_parts],
        scratch_shapes=_ssm_scratch(nc, u.dtype, 2) + ([_dma_sems(ns, 3), _dma_sems(ns, 3), _dma_sems(ns)] if ns else []),
        compiler_params=_params("arbitrary"),
    )(u, dy, pw, bb, cc, dd, *chip_parts)
    return out[:5], list(out[5:])


NA_Q = NA_QROWS * GRID_W
NA_K = NA_KROWS * GRID_W
NA_SCALE = NA_HEAD_DIM ** -0.5


def _na_block(b, nb, rows):
    start = jnp.clip(NA_QROWS * b - NA_ROWS // 2, 0, rows - NA_KROWS) * GRID_W
    kind = jnp.where(b == 0, 0, jnp.where(b == nb - 1, 2, 1))
    return pl.multiple_of(start, GRID_W), kind


NA_CHUNK = 16


def _na_pieces(kind, i):
    ri, q0 = divmod(i * NA_CHUNK, GRID_W)
    off = (NA_ROWS - 1, NA_ROWS // 2 - 1, -1)[kind]
    lo = (0, ri, NA_KROWS - NA_ROWS)[kind]
    modes = {(True, True): 'both', (True, False): 'even', (False, True): 'odd', (False, False): None}
    out = []
    for k2 in range(NA_KROWS // 2):
        inside = tuple(lo <= kr < lo + NA_ROWS for kr in (2 * k2, 2 * k2 + 1))
        out.append((2 * k2 - ri + off + 1, slice(q0, q0 + NA_CHUNK), modes[inside]))
    return out


def _na_softmax_pieces(s_ref, tab_ref, hh, kind, i):
    rows = slice(i * NA_CHUNK, (i + 1) * NA_CHUNK)
    lane = lax.broadcasted_iota(jnp.int32, (1, LANES), 1)
    xs = []
    for k2, (t, q, mode) in enumerate(_na_pieces(kind, i)):
        if mode is None:
            xs.append(None)
            continue
        bias = tab_ref[hh, t, q, :]
        if mode == 'even':
            bias = jnp.where(lane < GRID_W, bias, NEG)
        elif mode == 'odd':
            bias = jnp.where(lane >= GRID_W, bias, NEG)
        xs.append(s_ref[hh, rows, k2 * LANES:(k2 + 1) * LANES] + bias)
    live = [x for x in xs if x is not None]
    m = jnp.max(functools.reduce(jnp.maximum, live), axis=-1, keepdims=True)
    es = [None if x is None else jnp.exp(x - m) for x in xs]
    total = jnp.sum(functools.reduce(jnp.add, [e for e in es if e is not None]), axis=-1, keepdims=True)
    inv = 1.0 / total
    return [None if e is None else e * inv for e in es]


def _na_heads():
    lane = lax.broadcasted_iota(jnp.int32, (1, LANES), 1)
    return [lane < NA_HEAD_DIM, lane >= NA_HEAD_DIM]


def _na_fwd(qkv, bias):
    seq = qkv.shape[0]
    rows = seq // GRID_W
    nb = rows // NA_QROWS

    def body(q_ref, k_ref, v_ref, bias_ref, o_ref, p_ref, s_scr):
        start, kind = _na_block(pl.program_id(1), nb, rows)

        def block(static_kind):
            q2 = q_ref[...] * NA_SCALE
            kw = k_ref[pl.ds(start, NA_K), :]
            vw = v_ref[pl.ds(start, NA_K), :]
            heads = _na_heads()
            for hh in range(2):
                s_scr[hh] = _nt(jnp.where(heads[hh], q2, jnp.zeros_like(q2)), kw)
            for hh in range(2):
                for i in range(NA_Q // NA_CHUNK):
                    r = slice(i * NA_CHUNK, (i + 1) * NA_CHUNK)
                    for k2, p in enumerate(_na_softmax_pieces(s_scr, bias_ref, hh, static_kind, i)):
                        p = jnp.zeros((NA_CHUNK, LANES), F32) if p is None else p
                        p_ref[hh, r, k2 * LANES:(k2 + 1) * LANES] = p.astype(p_ref.dtype)
            o_ref[...] = jnp.where(heads[0], _mm(p_ref[0], vw), _mm(p_ref[1], vw))

        for static_kind in range(3):
            pl.when(kind == static_kind)(functools.partial(block, static_kind))

    return pl.pallas_call(
        body, name="na_fwd", grid=(NA_HEADS // 2, nb),
        in_specs=[pl.BlockSpec((NA_Q, LANES), lambda hp, b: (b, hp)),
                  pl.BlockSpec((seq, LANES), lambda hp, b: (0, 4 + hp)),
                  pl.BlockSpec((seq, LANES), lambda hp, b: (0, 8 + hp)),
                  pl.BlockSpec((2, NA_TAB, GRID_W, LANES), lambda hp, b: (hp, 0, 0, 0))],
        out_specs=[pl.BlockSpec((NA_Q, LANES), lambda hp, b: (b, hp)),
                   pl.BlockSpec((2, NA_Q, NA_K), lambda hp, b: (hp, b, 0))],
        out_shape=[jax.ShapeDtypeStruct((seq, D_NA), F32), jax.ShapeDtypeStruct((NA_HEADS, seq, NA_K), qkv.dtype)],
        scratch_shapes=[pltpu.VMEM((2, NA_Q, NA_K), F32)],
        compiler_params=_params("arbitrary", "arbitrary"),
    )(qkv, qkv, qkv, bias)


def _na_bwd(qkv, do, probs, parts=()):
    seq = qkv.shape[0]
    rows = seq // GRID_W
    nb = rows // NA_QROWS
    ns = len(parts)

    def body(q_ref, k_ref, v_ref, do_ref, p_ref, *rest):
        part_refs, rest = rest[:ns], rest[ns:]
        (dq_ref, dk_ref, dv_ref, dbias_ref), rest = rest[:4], rest[4:]
        land_refs, rest = rest[:ns], rest[ns:]
        dp_scr, ds_scr = rest[:2]
        b = pl.program_id(1)
        start, kind = _na_block(b, nb, rows)
        if ns:
            hp = pl.program_id(0)
            _ride(_pair_copies(part_refs, land_refs, *rest[2:]), (hp == 0) & (b == 0),
                  (hp == NA_HEADS // 2 - 1) & (b == nb - 1))

        @pl.when(b == 0)
        def _():
            dk_ref[...] = jnp.zeros_like(dk_ref)
            dv_ref[...] = jnp.zeros_like(dv_ref)
            dbias_ref[...] = jnp.zeros_like(dbias_ref)

        def block(static_kind):
            q2 = q_ref[...] * NA_SCALE
            kw = k_ref[pl.ds(start, NA_K), :]
            vw = v_ref[pl.ds(start, NA_K), :]
            do2 = do_ref[...].astype(q2.dtype)
            heads = _na_heads()
            col = lambda k2: slice(k2 * LANES, (k2 + 1) * LANES)
            zero = jnp.zeros((NA_CHUNK, LANES), ds_scr.dtype)
            for hh in range(2):
                dp_scr[hh] = _nt(jnp.where(heads[hh], do2, jnp.zeros_like(do2)), vw)
            for hh in range(2):
                for i in range(NA_Q // NA_CHUNK):
                    r = slice(i * NA_CHUNK, (i + 1) * NA_CHUNK)
                    pieces = _na_pieces(static_kind, i)
                    ps = [None if mode is None else p_ref[hh, r, col(k2)].astype(F32)
                          for k2, (_, _, mode) in enumerate(pieces)]
                    dps = [None if p is None else dp_scr[hh, r, col(k2)] for k2, p in enumerate(ps)]
                    pdp = functools.reduce(jnp.add, [p * dp for p, dp in zip(ps, dps) if p is not None])
                    rowsum = jnp.sum(pdp, axis=-1, keepdims=True)
                    for k2, (t, q, _) in enumerate(pieces):
                        if ps[k2] is None:
                            ds_scr[hh, r, col(k2)] = zero
                            continue
                        ds = ps[k2] * (dps[k2] - rowsum)
                        dbias_ref[hh, t, q, :] += ds
                        ds_scr[hh, r, col(k2)] = ds.astype(ds_scr.dtype)
            dq_ref[...] = jnp.where(heads[0], _mm(ds_scr[0], kw), _mm(ds_scr[1], kw)) * NA_SCALE
            dk_ref[pl.ds(start, NA_K), :] += jnp.where(heads[0], _tn(ds_scr[0], q2), _tn(ds_scr[1], q2))
            dv_ref[pl.ds(start, NA_K), :] += jnp.where(heads[0], _tn(p_ref[0], do2), _tn(p_ref[1], do2))

        for static_kind in range(3):
            pl.when(kind == static_kind)(functools.partial(block, static_kind))

    out = pl.pallas_call(
        body, name="na_bwd", grid=(NA_HEADS // 2, nb),
        in_specs=[pl.BlockSpec((NA_Q, LANES), lambda hp, b: (b, hp)),
                  pl.BlockSpec((seq, LANES), lambda hp, b: (0, 4 + hp)),
                  pl.BlockSpec((seq, LANES), lambda hp, b: (0, 8 + hp)),
                  pl.BlockSpec((NA_Q, LANES), lambda hp, b: (b, hp)),
                  pl.BlockSpec((2, NA_Q, NA_K), lambda hp, b: (hp, b, 0))] + [_ANY()] * ns,
        out_specs=[pl.BlockSpec((NA_Q, LANES), lambda hp, b: (b, hp)),
                   pl.BlockSpec((seq, LANES), lambda hp, b: (0, hp)),
                   pl.BlockSpec((seq, LANES), lambda hp, b: (0, hp)),
                   pl.BlockSpec((2, NA_TAB, GRID_W, LANES), lambda hp, b: (hp, 0, 0, 0))] + [_ANY()] * ns,
        out_shape=[jax.ShapeDtypeStruct((seq, D_NA), F32), jax.ShapeDtypeStruct((seq, D_NA), F32),
                   jax.ShapeDtypeStruct((seq, D_NA), F32),
                   jax.ShapeDtypeStruct((NA_HEADS, NA_TAB, GRID_W, LANES), F32)]
        + [jax.ShapeDtypeStruct((4,) + pt.shape[1:], pt.dtype) for pt in parts],
        scratch_shapes=[pltpu.VMEM((2, NA_Q, NA_K), F32), pltpu.VMEM((2, NA_Q, NA_K), qkv.dtype)]
        + ([_dma_sems(ns, 4), _dma_sems(ns, 4)] if ns else []),
        compiler_params=_params("arbitrary", "arbitrary"),
    )(qkv, qkv, qkv, do, probs, *parts)
    return out[:4], list(out[4:])


def _local_step(x2, p2, tgt, g_pre, g_post, w_in_g, ssm, w_glu, b_glu, rpb, w_out, w_ple, g_ple, w_pg,
                distributed=False):
    (pw, bb, cc, dd), ssm_vjp = jax.vjp(_ssm_tables, *ssm)
    bias, bias_vjp = jax.vjp(_na_table, rpb)

    riders = (w_glu, w_out, w_ple, w_pg) if distributed else ()
    (u_p, zs, qkv, zn, hn), gathered = _fwd_in(x2, g_pre, w_in_g, riders)
    if distributed:
        w_glu, w_out, w_ple, w_pg = gathered
        w_glu = w_glu.reshape(D_SSM, D_SSM)
        w_out = w_out.reshape(D_MODEL, D_MODEL)
        w_pg = w_pg.reshape(D_MODEL, D_MODEL)
    yssm = _ssm_fwd(u_p, pw, bb, cc, dd)
    o, probs = _na_fwd(qkv, bias)
    dyssm, dzs, do, dzn, dh1, sq, d_wglu, d_bglu, d_wout, d_wple, d_wpg, d_gpost, d_gple = _mid(
        yssm, zs, o, zn, x2, p2, tgt, w_glu, b_glu, w_out, g_post, w_ple, g_ple, w_pg)
    mid_grads = [d_wglu.reshape(N_DEV, D_SSM // N_DEV, D_SSM), d_wout.reshape(N_DEV, D_MODEL // N_DEV, D_MODEL),
                 d_wple, d_wpg.reshape(N_DEV, D_MODEL // N_DEV, D_MODEL)]
    (dq, dk, dv, dbias), lands = _na_bwd(qkv, do, probs, mid_grads if distributed else ())
    (d_rpb,) = bias_vjp(dbias)
    chip_parts = _pair_sum(mid_grads, lands)[0] if distributed else ()
    (du_p, dpw, dbb, dcc, ddd), chip_lands = _ssm_bwd(u_p, dyssm, pw, bb, cc, dd, chip_parts)
    d_ssm = ssm_vjp((dpw, dbb, dcc, ddd))
    dx, d_win, d_gpre = _bwd_in(du_p, dzs, dq, dk, dv, dzn, hn, x2, dh1, g_pre, w_in_g)
    if distributed:
        d_wglu, d_wout, d_wple, d_wpg = chip_lands
    return sq, dx, dict(norm_pre=d_gpre, norm_post=d_gpost, w_in=d_win, ssm=d_ssm, w_glu=d_wglu, b_glu=d_bglu,
                        na_rpb=d_rpb, w_out=d_wout, w_ple=d_wple, ple_norm=d_gple, w_ple_gate=d_wpg)


def _all_gather(shard):
    m_per, n = shard.shape

    def body(x_ref, out_ref, send_sems, recv_sems, local_sem):
        x, y, c = _place()
        me, sibling = (x, y, c), (x, y, 1 - c)
        chips = [(1 - x, y), (x, 1 - y), (1 - x, 1 - y)]

        def rows(px, py, pc):
            return out_ref.at[pl.ds((4 * px + 2 * py + pc) * m_per, m_per), :]

        def copy(k, block, to, src=None):
            return pltpu.make_async_remote_copy(
                src_ref=rows(*block) if src is None else src, dst_ref=rows(*block),
                send_sem=send_sems.at[k], recv_sem=recv_sems.at[k], device_id=to, device_id_type=MESH)

        mine = pltpu.make_async_copy(x_ref, rows(*me), local_sem)
        mine.start()
        first = [copy(0, me, sibling, src=x_ref)]
        first += [copy(1 + j, me, (*chip, c), src=x_ref) for j, chip in enumerate(chips)]
        for cp in first:
            cp.start()
        passed = [copy(4 + j, (*chip, c), sibling) for j, chip in enumerate(chips)]
        for j, chip in enumerate(chips):
            copy(1 + j, (*chip, c), me).wait_recv()
            passed[j].start()
        copy(0, sibling, me).wait_recv()
        for j, chip in enumerate(chips):
            copy(4 + j, (*chip, 1 - c), me).wait_recv()
        for cp in first + passed:
            cp.wait_send()
        mine.wait()

    return pl.pallas_call(
        body, name="all_gather",
        out_shape=jax.ShapeDtypeStruct((N_DEV * m_per, n), shard.dtype),
        in_specs=[_whole_vmem()], out_specs=_whole_vmem(),
        scratch_shapes=[pltpu.SemaphoreType.DMA((7,)), pltpu.SemaphoreType.DMA((7,)), pltpu.SemaphoreType.DMA],
        compiler_params=pltpu.CompilerParams(vmem_limit_bytes=VMEM_LIMIT),
    )(shard)


def _exchange_first(pair_parts, all_parts):
    n1, n2 = len(pair_parts), len(all_parts)

    def body(*refs):
        ins, refs = refs[:n1 + n2], refs[n1 + n2:]
        outs, sems = refs[:n1 + n2], refs[n1 + n2:]
        copies = (_pair_copies(ins[:n1], outs[:n1], *sems[:2])
                  + _alltoall_copies(ins[n1:], outs[n1:], *sems[2:]))
        _start_all(copies)
        _wait_all(copies)

    out = pl.pallas_call(
        body, name="exchange_first",
        out_shape=[jax.ShapeDtypeStruct((4,) + pt.shape[1:], pt.dtype) for pt in pair_parts]
        + [jax.ShapeDtypeStruct(pt.shape, pt.dtype) for pt in all_parts],
        in_specs=[_ANY()] * (n1 + n2), out_specs=[_ANY()] * (n1 + n2),
        scratch_shapes=[_dma_sems(n1, 4), _dma_sems(n1, 4),
                        _dma_sems(n2, N_DEV - 1), _dma_sems(n2, N_DEV - 1), _dma_sems(n2)],
    )(*pair_parts, *all_parts)
    return list(out[:n1]), list(out[n1:])


def _pair_sum(parts, lands, all_lands=()):
    ns, na = len(parts), len(all_lands)

    def body(*refs):
        c = lax.axis_index("c")
        ins, outs = refs[:2 * ns + na], refs[2 * ns + na:]
        for part_ref, land_ref, out_ref in zip(ins[:ns], ins[ns:2 * ns], outs[:ns]):
            for q in range(4):
                out_ref[q] = (part_ref[q, c] + land_ref[q]).astype(BF16)
        for land_ref, out_ref in zip(ins[2 * ns:], outs[ns:]):
            acc = land_ref[0]
            for j in range(1, N_DEV):
                acc = acc + land_ref[j]
            out_ref[...] = acc

    out = pl.pallas_call(
        body, name="pair_sum",
        in_specs=[_whole_vmem()] * (2 * ns + na), out_specs=[_whole_vmem()] * (ns + na),
        out_shape=[jax.ShapeDtypeStruct(ld.shape, BF16) for ld in lands]
        + [jax.ShapeDtypeStruct(ld.shape[1:], ld.dtype) for ld in all_lands],
        compiler_params=pltpu.CompilerParams(vmem_limit_bytes=VMEM_LIMIT),
    )(*[pt.reshape((4, 2) + pt.shape[1:]) for pt in parts], *lands, *all_lands)
    return list(out[:ns]), list(out[ns:])


def _exchange_second(chip_parts, shards):
    n1, n2 = len(chip_parts), len(shards)

    def body(*refs):
        ins, refs = refs[:n1 + n2], refs[n1 + n2:]
        outs, sems = refs[:n1 + n2], refs[n1 + n2:]
        copies = (_chip_copies(ins[:n1], outs[:n1], *sems[:3])
                  + _gather_copies(ins[n1:], outs[n1:], *sems[3:]))
        _start_all(copies)
        _wait_all(copies)

    out = pl.pallas_call(
        body, name="exchange_second",
        out_shape=[jax.ShapeDtypeStruct(pt.shape, pt.dtype) for pt in chip_parts]
        + [jax.ShapeDtypeStruct((N_DEV,) + sh.shape, sh.dtype) for sh in shards],
        in_specs=[_ANY()] * (n1 + n2), out_specs=[_ANY()] * (n1 + n2),
        scratch_shapes=[_dma_sems(n1, 3), _dma_sems(n1, 3), _dma_sems(n1),
                        _dma_sems(n2, N_DEV - 1), _dma_sems(n2, N_DEV - 1), _dma_sems(n2)],
    )(*chip_parts, *shards)
    return list(out[:n1]), list(out[n1:])


ADAM_ROWS = 256


def _adamw(ws, gs, ms, vs, slab=None):
    n = len(ws)
    arrays = [g for g in gs if not isinstance(g, int)]
    n_in = 3 * n + len(arrays) + (slab is not None)

    def body(*refs):
        ins, outs = refs[:n_in], refs[n_in:]
        w_refs, m_refs, v_refs = ins[:n], ins[n:2 * n], ins[2 * n:3 * n]
        g_refs = iter(ins[3 * n:3 * n + len(arrays)])
        for t in range(n):
            w_ref, m_ref, v_ref = w_refs[t], m_refs[t], v_refs[t]
            go_ref, d_ref, nm_ref, nv_ref = outs[4 * t:4 * t + 4]
            g_ref = None if isinstance(gs[t], int) else next(g_refs)
            rows = w_ref.shape[0]
            for lo in range(0, rows, ADAM_ROWS):
                hi = min(lo + ADAM_ROWS, rows)
                r = slice(lo, hi)
                if g_ref is None:
                    g = ins[-1][gs[t] + lo:gs[t] + hi, :]
                elif len(g_ref.shape) == 3:
                    g = g_ref[0, r, :].astype(F32)
                    for j in range(1, g_ref.shape[0]):
                        g = g + g_ref[j, r, :].astype(F32)
                else:
                    g = g_ref[r, :]
                go_ref[r, :] = g
                nm = ADAM_B1 * m_ref[r, :] + (1.0 - ADAM_B1) * g
                nv = ADAM_B2 * v_ref[r, :] + (1.0 - ADAM_B2) * (g * g)
                m_hat = nm / (1.0 - ADAM_B1 ** ADAM_STEP)
                v_hat = nv / (1.0 - ADAM_B2 ** ADAM_STEP)
                d_ref[r, :] = -ADAM_LR * (m_hat / (jnp.sqrt(v_hat) + ADAM_EPS) + ADAM_WD * w_ref[r, :])
                nm_ref[r, :] = nm
                nv_ref[r, :] = nv

    out = pl.pallas_call(
        body, name="adamw",
        in_specs=[_whole_vmem()] * n_in, out_specs=[_whole_vmem()] * (4 * n),
        out_shape=[jax.ShapeDtypeStruct(w.shape, F32) for w in ws for _ in range(4)],
        compiler_params=pltpu.CompilerParams(vmem_limit_bytes=VMEM_LIMIT),
    )(*ws, *ms, *vs, *arrays, *([slab] if slab is not None else []))
    return [tuple(out[4 * t:4 * t + 4]) for t in range(n)]


_SLAB = 8 * LANES


def _flat_rows(a):
    flat = a.reshape(-1)
    pad = (-flat.shape[0]) % _SLAB
    if pad:
        flat = jnp.concatenate([flat, jnp.zeros((pad,), flat.dtype)])
    return flat.reshape(-1, LANES)


def _pack(arrays):
    slabs = [_flat_rows(a) for a in arrays]
    return jnp.concatenate(slabs, axis=0), [s.shape[0] for s in slabs]


SMALL = ('norm_pre', 'norm_post', 'ssm_a_re', 'ssm_a_im', 'ssm_log_dt', 'ssm_b_re', 'ssm_b_im', 'ssm_c_re',
         'ssm_c_im', 'ssm_d', 'b_glu', 'na_rpb', 'ple_norm')
BIG = ('w_in', 'w_glu', 'w_out', 'w_ple', 'w_ple_gate')
ORDER = ('norm_pre', 'norm_post', 'w_in', 'ssm_a_re', 'ssm_a_im', 'ssm_log_dt', 'ssm_b_re', 'ssm_b_im', 'ssm_c_re',
         'ssm_c_im', 'ssm_d', 'w_glu', 'b_glu', 'na_rpb', 'w_out', 'w_ple', 'ple_norm', 'w_ple_gate')


def kernel(x, p, norm_pre, norm_post, w_in, ssm_a_re, ssm_a_im, ssm_log_dt, ssm_b_re, ssm_b_im, ssm_c_re, ssm_c_im, ssm_d, w_glu, b_glu, na_rpb, w_out, w_ple, ple_norm, w_ple_gate, loss_target, m_norm_pre, m_norm_post, m_w_in, m_ssm_a_re, m_ssm_a_im, m_ssm_log_dt, m_ssm_b_re, m_ssm_b_im, m_ssm_c_re, m_ssm_c_im, m_ssm_d, m_w_glu, m_b_glu, m_na_rpb, m_w_out, m_w_ple, m_ple_norm, m_w_ple_gate, v_norm_pre, v_norm_post, v_w_in, v_ssm_a_re, v_ssm_a_im, v_ssm_log_dt, v_ssm_b_re, v_ssm_b_im, v_ssm_c_re, v_ssm_c_im, v_ssm_d, v_w_glu, v_b_glu, v_na_rpb, v_w_out, v_w_ple, v_ple_norm, v_w_ple_gate):
    args = dict(locals())
    weights = {n: args[n] for n in ORDER}
    mom_m = {n: args["m_" + n] for n in ORDER}
    mom_v = {n: args["v_" + n] for n in ORDER}

    w_in_g = _all_gather(w_in[0].astype(BF16)).reshape(N_DEV, D_MODEL, SHARD_IN)
    blocks = [weights[n][0].astype(BF16) for n in ('w_glu', 'w_out', 'w_ple', 'w_ple_gate')]

    ssm = tuple(weights[n][0] for n in ('ssm_a_re', 'ssm_a_im', 'ssm_log_dt', 'ssm_b_re', 'ssm_b_im',
                                        'ssm_c_re', 'ssm_c_im', 'ssm_d'))
    sq, dx, grads = _local_step(x[0], p[0, 0], loss_target[0], norm_pre, norm_post, w_in_g, ssm, blocks[0], b_glu,
                                na_rpb[0], blocks[1], blocks[2], ple_norm, blocks[3], distributed=True)
    loss_local = (0.5 / D_MODEL * jnp.sum(sq)).reshape(1)

    local = dict(norm_pre=grads['norm_pre'], norm_post=grads['norm_post'], b_glu=grads['b_glu'],
                 na_rpb=grads['na_rpb'][None], ple_norm=grads['ple_norm'])
    for n, g in zip(('ssm_a_re', 'ssm_a_im', 'ssm_log_dt', 'ssm_b_re', 'ssm_b_im', 'ssm_c_re', 'ssm_c_im', 'ssm_d'),
                    grads['ssm']):
        local[n] = g[None]

    small_flat, _ = _pack([local[n] for n in SMALL] + [loss_local])
    pad = (-small_flat.shape[0]) % (8 * N_DEV)
    if pad:
        small_flat = jnp.concatenate([small_flat, jnp.zeros((pad, LANES), F32)], axis=0)
    small_per = small_flat.shape[0] // N_DEV
    d_win = [grads['w_in']]
    lands, small_lands = _exchange_first(d_win, [small_flat.reshape(N_DEV, small_per, LANES)])
    chip_parts, small_sums = _pair_sum(d_win, lands, small_lands)
    chip_lands, small_full = _exchange_second(chip_parts, small_sums)
    grads['w_in'] = chip_lands[0]
    small_all = small_full[0].reshape(N_DEV * small_per, LANES)

    like = [weights[n] for n in SMALL]
    slabs = [[_flat_rows(t[n]) for n in SMALL] for t in (weights, mom_m, mom_v)]
    starts, at = [], 0
    for w_rows in slabs[0]:
        starts.append(at)
        at += w_rows.shape[0]
    loss = small_all[at, 0]
    results = _adamw([weights[n][0] for n in BIG] + slabs[0], [grads[n] for n in BIG] + starts,
                     [mom_m[n][0] for n in BIG] + slabs[1], [mom_v[n][0] for n in BIG] + slabs[2], slab=small_all)
    outs = {n: tuple(a[None] for a in res) for n, res in zip(BIG, results)}
    for n, a, res in zip(SMALL, like, results[len(BIG):]):
        outs[n] = tuple(r.reshape(-1)[:a.size].reshape(a.shape) for r in res)

    return (loss, dx[None], *[outs[n][0] for n in ORDER], *[outs[n][1] for n in ORDER],
            *[outs[n][2] for n in ORDER], *[outs[n][3] for n in ORDER])
```

```python
import functools

import jax
import jax.numpy as jnp
from jax import lax
from jax.experimental import pallas as pl
from jax.experimental.pallas import tpu as pltpu

F32 = jnp.float32
BF16 = jnp.bfloat16
HIGHEST = lax.Precision.HIGHEST

D_MODEL = 1024
D_PLE = 256
GRID_W = 64
D_SSM = 512
SSM_GROUP = 16
N_PAIRS = 16
SSM_STATE = 64
D_NA = 512
NA_HEADS = 8
NA_HEAD_DIM = 64
NA_ROWS = 8
NA_COLS = 16
D_IN_PROJ = 3072
EPS = 1e-6
N_DEV = 8
SHARD_IN = D_IN_PROJ // N_DEV
IN_GROUPS = N_DEV // 2
GROUP_W = 2 * SHARD_IN
LANES = 128
SSM_CHUNK = 16
PW_ROWS = 24
TOK_TILE = 256
NA_QROWS = 4
NA_KROWS = 12
NEG = -1e30
VMEM_LIMIT = 56 * 1024 * 1024

ADAM_LR = 0.001
ADAM_B1 = 0.9
ADAM_B2 = 0.999
ADAM_EPS = 1e-08
ADAM_WD = 0.01
ADAM_STEP = 10

MESH = pl.DeviceIdType.MESH


def _params(*sem):
    return pltpu.CompilerParams(dimension_semantics=sem or None, vmem_limit_bytes=VMEM_LIMIT)


def _whole_vmem():
    return pl.BlockSpec(memory_space=pltpu.VMEM)


def _nt(a, b):
    return lax.dot_general(a, b, (((1,), (1,)), ((), ())), preferred_element_type=F32)


def _tn(a, b):
    return lax.dot_general(a, b, (((0,), (0,)), ((), ())), preferred_element_type=F32)


def _mm(a, b):
    return jnp.dot(a, b, preferred_element_type=F32)


def _sigmoid(x):
    return 1.0 / (1.0 + jnp.exp(-x))


_GELU_C = 0.7978845608028654


def _gelu(x):
    return 0.5 * x * (1.0 + jnp.tanh(_GELU_C * (x + 0.044715 * x * x * x)))


def _gelu_grad(x):
    th = jnp.tanh(_GELU_C * (x + 0.044715 * x * x * x))
    return 0.5 * (1.0 + th) + 0.5 * x * (1.0 - th * th) * _GELU_C * (1.0 + 3.0 * 0.044715 * x * x)


def _ssm_tables(a_re, a_im, log_dt, b_re, b_im, c_re, c_im, d):
    T, P, H = SSM_CHUNK, SSM_STATE, SSM_GROUP
    dt = jnp.exp(log_dt)[..., None]
    xr = dt * a_re
    xi = dt * a_im
    mag = jnp.exp(xr)
    lr = mag * jnp.cos(xi)
    li = mag * jnp.sin(xi)
    den = a_re * a_re + a_im * a_im
    cr = ((lr - 1.0) * a_re + li * a_im) / den
    ci = (li * a_re - (lr - 1.0) * a_im) / den
    bbr = cr[..., None] * b_re - ci[..., None] * b_im
    bbi = cr[..., None] * b_im + ci[..., None] * b_re
    kk = jnp.arange(T + 1, dtype=F32)[:, None, None, None]
    pm = jnp.exp(kk * xr)
    pw = jnp.stack([pm * jnp.cos(kk * xi), pm * jnp.sin(kk * xi)], axis=2)
    pw = pw.reshape(T + 1, 2, 2, N_PAIRS, 2 * P).transpose(3, 0, 1, 2, 4).reshape(N_PAIRS, T + 1, 8 * P)
    pw = jnp.concatenate([pw, jnp.zeros((N_PAIRS, PW_ROWS - (T + 1), 8 * P), F32)], axis=1)
    eye2 = jnp.eye(2, dtype=F32)

    def expand(t):
        t = t.transpose(2, 0, 1, 3, 4, 5)
        t = t[:, :, :, :, :, None, :] * eye2[None, None, None, :, None, :, None]
        return t.reshape(N_PAIRS, 4, 2 * H, 2 * P)

    bb = expand(jnp.stack([bbr, bbi], axis=1).reshape(2, 2, N_PAIRS, 2, P, H).transpose(0, 1, 2, 3, 5, 4))
    cc = expand(jnp.stack([c_re, c_im], axis=1).reshape(2, 2, N_PAIRS, 2, H, P))
    dd = d.reshape(N_PAIRS, 2 * H)[:, :, None] * jnp.eye(2 * H, dtype=F32)[None]
    dd = jnp.concatenate([dd, jnp.zeros((N_PAIRS, 2 * H, LANES - 2 * H), F32)], axis=2)
    return pw, bb, cc, dd


NA_TAB = 2 * NA_ROWS


def _na_table(rpb):
    qc = jnp.arange(GRID_W)[:, None, None]
    kc = (jnp.arange(2 * GRID_W) % GRID_W)[None, :, None]
    dc = jnp.arange(2 * NA_COLS - 1)[None, None, :]
    cstart = jnp.clip(qc - NA_COLS // 2, 0, GRID_W - NA_COLS)
    csel = ((kc >= cstart) & (kc < cstart + NA_COLS) & (kc - qc + NA_COLS - 1 == dc)).astype(F32)
    col_ok = jnp.sum(csel, axis=-1) > 0.5
    part = jnp.einsum('hrc,qmc->hrqm', rpb, csel, precision=HIGHEST)
    zero = jnp.zeros_like(part[:, :1])
    odd = jnp.arange(2 * GRID_W) >= GRID_W
    tab = jnp.where(odd, jnp.concatenate([part, zero], axis=1), jnp.concatenate([zero, part], axis=1))
    return jnp.where(col_ok, tab, NEG)


def _place():
    return lax.axis_index("x"), lax.axis_index("y"), lax.axis_index("c")


def _remote(src, dst, send_sem, recv_sem, device):
    return pltpu.make_async_remote_copy(src_ref=src, dst_ref=dst, send_sem=send_sem, recv_sem=recv_sem,
                                        device_id=device, device_id_type=MESH)


def _start_all(copies):
    for cp in copies:
        cp.start()


def _wait_all(copies):
    for cp in copies:
        cp.wait()


def _gather_copies(shard_refs, full_refs, send_sems, recv_sems, local_sems):
    x, y, c = _place()
    me = 4 * x + 2 * y + c
    out = []
    for t, (shard, full) in enumerate(zip(shard_refs, full_refs)):
        out.append(pltpu.make_async_copy(shard, full.at[me], local_sems.at[t]))
        for k in range(1, N_DEV):
            peer = (x ^ ((k >> 2) & 1), y ^ ((k >> 1) & 1), c ^ (k & 1))
            out.append(_remote(shard, full.at[me], send_sems.at[t, k - 1], recv_sems.at[t, k - 1], peer))
    return out


def _pair_copies(part_refs, land_refs, send_sems, recv_sems):
    x, y, c = _place()
    out = []
    for t, (part, land) in enumerate(zip(part_refs, land_refs)):
        for q in range(4):
            out.append(_remote(part.at[2 * q + (1 - c)], land.at[q], send_sems.at[t, q], recv_sems.at[t, q],
                               (x, y, 1 - c)))
    return out


def _chip_copies(part_refs, land_refs, send_sems, recv_sems, local_sems):
    x, y, c = _place()
    mine = 2 * x + y
    out = []
    for t, (part, land) in enumerate(zip(part_refs, land_refs)):
        out.append(pltpu.make_async_copy(part.at[mine], land.at[mine], local_sems.at[t]))
        for k in range(1, 4):
            px, py = x ^ (k >> 1), y ^ (k & 1)
            out.append(_remote(part.at[2 * px + py], land.at[mine], send_sems.at[t, k - 1], recv_sems.at[t, k - 1],
                               (px, py, c)))
    return out


def _alltoall_copies(part_refs, land_refs, send_sems, recv_sems, local_sems):
    x, y, c = _place()
    me = 4 * x + 2 * y + c
    out = []
    for t, (part, land) in enumerate(zip(part_refs, land_refs)):
        out.append(pltpu.make_async_copy(part.at[me], land.at[me], local_sems.at[t]))
        for k in range(1, N_DEV):
            px, py, pc = x ^ ((k >> 2) & 1), y ^ ((k >> 1) & 1), c ^ (k & 1)
            out.append(_remote(part.at[4 * px + 2 * py + pc], land.at[me], send_sems.at[t, k - 1],
                               recv_sems.at[t, k - 1], (px, py, pc)))
    return out


def _ride(copies, first, last):
    pl.when(first)(functools.partial(_start_all, copies))
    pl.when(last)(functools.partial(_wait_all, copies))


_ANY = lambda: pl.BlockSpec(memory_space=pl.ANY)


def _dma_sems(*shape):
    return pltpu.SemaphoreType.DMA(shape)


PAIR_W = 2 * SSM_GROUP
PAIRS_PER_BLOCK = LANES // PAIR_W
CHUNK_W = SSM_CHUNK * PAIR_W
TILE_CHUNKS = TOK_TILE // SSM_CHUNK


def _lane_window(lo, width):
    lane = lax.broadcasted_iota(jnp.int32, (1, LANES), 1)
    return (lane >= lo) & (lane < lo + width)


def _tile_to_pairs(src_ref, out_ref):
    for cl in range(D_SSM // LANES):
        xs = [src_ref[cl, pl.ds(s, TILE_CHUNKS, stride=SSM_CHUNK), :] for s in range(SSM_CHUNK)]
        for a_ in range(PAIRS_PER_BLOCK):
            for v in range(SSM_CHUNK // PAIRS_PER_BLOCK):
                acc = None
                for i in range(PAIRS_PER_BLOCK):
                    shift = (PAIR_W * (i - a_)) % LANES
                    piece = xs[PAIRS_PER_BLOCK * v + i]
                    piece = pltpu.roll(piece, shift, axis=1) if shift else piece
                    acc = piece if acc is None else jnp.where(_lane_window(PAIR_W * i, PAIR_W), piece, acc)
                out_ref[PAIRS_PER_BLOCK * cl + a_, :, LANES * v:LANES * (v + 1)] = acc.astype(out_ref.dtype)


def _tile_from_pairs(in_ref, dst_ref):
    for cl in range(D_SSM // LANES):
        for s in range(SSM_CHUNK):
            v, i = divmod(s, PAIRS_PER_BLOCK)
            acc = None
            for a_ in range(PAIRS_PER_BLOCK):
                shift = (PAIR_W * (a_ - i)) % LANES
                piece = in_ref[PAIRS_PER_BLOCK * cl + a_, :, LANES * v:LANES * (v + 1)].astype(F32)
                piece = pltpu.roll(piece, shift, axis=1) if shift else piece
                acc = piece if acc is None else jnp.where(_lane_window(PAIR_W * a_, PAIR_W), piece, acc)
            dst_ref[cl, pl.ds(s, TILE_CHUNKS, stride=SSM_CHUNK), :] = acc


def _pair_shards(w_ref, w_scr):
    for g in range(IN_GROUPS):
        w_scr[g, :, 0:SHARD_IN] = w_ref[2 * g]
        w_scr[g, :, SHARD_IN:GROUP_W] = w_ref[2 * g + 1]


def _pair_tile():
    return pl.BlockSpec((N_PAIRS, TILE_CHUNKS, CHUNK_W), lambda i: (0, i, 0))


def _tile_scratch():
    return pltpu.VMEM((D_SSM // LANES, TOK_TILE, LANES), F32)


def _fwd_in(x2, g_pre, w_in_g, shards=()):
    seq = x2.shape[0]
    ns = len(shards)
    steps = seq // TOK_TILE

    def body(x_ref, g_ref, w_ref, *rest):
        shard_refs, rest = rest[:ns], rest[ns:]
        (u_ref, zs_ref, qkv_ref, zn_ref, hn_ref), rest = rest[:5], rest[5:]
        (w_scr, u_scr), rest = rest[-2:], rest[:-2]
        i = pl.program_id(0)
        if ns:
            _ride(_gather_copies(shard_refs, rest[:ns], *rest[ns:]), i == 0, i == steps - 1)
        pl.when(i == 0)(functools.partial(_pair_shards, w_ref, w_scr))
        x = x_ref[...]
        r = lax.rsqrt(jnp.mean(x * x, axis=-1, keepdims=True) + EPS)
        hn = (x * r * g_ref[...]).astype(BF16)
        hn_ref[...] = hn
        for j in range(IN_GROUPS):
            pj = _mm(hn, w_scr[j])
            for i in range(GROUP_W // LANES):
                blk = (GROUP_W // LANES) * j + i
                piece = pj[:, i * LANES:(i + 1) * LANES]
                if blk < 4:
                    u_scr[blk] = piece
                elif blk < 8:
                    zs_ref[:, (blk - 4) * LANES:(blk - 3) * LANES] = piece
                elif blk < 20:
                    qkv_ref[:, (blk - 8) * LANES:(blk - 7) * LANES] = piece.astype(BF16)
                else:
                    zn_ref[:, (blk - 20) * LANES:(blk - 19) * LANES] = piece
        _tile_to_pairs(u_scr, u_ref)

    tok = lambda w: pl.BlockSpec((TOK_TILE, w), lambda i: (i, 0))
    out = pl.pallas_call(
        body, name="fwd_in", grid=(steps,),
        in_specs=[tok(D_MODEL), pl.BlockSpec((1, D_MODEL), lambda i: (0, 0)), _whole_vmem()] + [_ANY()] * ns,
        out_specs=[_pair_tile(), tok(D_SSM), tok(3 * D_NA), tok(D_NA), tok(D_MODEL)] + [_ANY()] * ns,
        out_shape=[jax.ShapeDtypeStruct((N_PAIRS, seq // SSM_CHUNK, CHUNK_W), BF16), jax.ShapeDtypeStruct((seq, D_SSM), F32),
                   jax.ShapeDtypeStruct((seq, 3 * D_NA), BF16), jax.ShapeDtypeStruct((seq, D_NA), F32),
                   jax.ShapeDtypeStruct((seq, D_MODEL), BF16)]
        + [jax.ShapeDtypeStruct((N_DEV,) + sh.shape, sh.dtype) for sh in shards],
        scratch_shapes=([_dma_sems(ns, N_DEV - 1), _dma_sems(ns, N_DEV - 1), _dma_sems(ns)] if ns else [])
        + [pltpu.VMEM((IN_GROUPS, D_MODEL, GROUP_W), BF16), _tile_scratch()],
        compiler_params=_params("arbitrary"),
    )(x2, g_pre, w_in_g, *shards)
    return out[:5], list(out[5:])


def _mid(yssm, zs, o, zn, x2, p2, tgt, w_glu, b_glu, w_out, g_post, w_ple_g, g_ple, w_pg):
    seq = x2.shape[0]

    def body(yssm_ref, zs_ref, o_ref, zn_ref, x_ref, p_ref, tgt_ref, wglu_ref, bglu_ref, wout_ref, gpost_ref,
             wple_ref, gple_ref, wpg_ref,
             dyssm_ref, dzs_ref, do_ref, dzn_ref, dh1_ref, loss_ref, dwglu_ref, dbglu_ref, dwout_ref, dwple_ref,
             dwpg_ref, dgpost_ref, dgple_ref, y_scr, dy_scr, wple_scr):
        @pl.when(pl.program_id(0) == 0)
        def _():
            for ref in (loss_ref, dwglu_ref, dbglu_ref, dwout_ref, dwple_ref, dwpg_ref, dgpost_ref, dgple_ref):
                ref[...] = jnp.zeros_like(ref)
            for g in range(N_DEV // 2):
                wple_scr[g, :, 0:LANES] = wple_ref[2 * g]
                wple_scr[g, :, LANES:2 * LANES] = wple_ref[2 * g + 1]

        _tile_from_pairs(yssm_ref, y_scr)
        yv = jnp.concatenate([y_scr[cl] for cl in range(D_SSM // LANES)], axis=1)
        g1 = _gelu(yv)
        g1b = g1.astype(BF16)
        sg = _sigmoid(_mm(g1b, wglu_ref[...]) + bglu_ref[...])
        zs_v = zs_ref[...]
        s_zs = _sigmoid(zs_v)
        g2 = g1 * sg
        zn_v = zn_ref[...]
        s_zn = _sigmoid(zn_v)
        o_v = o_ref[...]
        cat = jnp.concatenate([g2 * (zs_v * s_zs), o_v * (zn_v * s_zn)], axis=1).astype(BF16)
        mix = _mm(cat, wout_ref[...])
        r2 = lax.rsqrt(jnp.mean(mix * mix, axis=-1, keepdims=True) + EPS)
        n2 = mix * r2
        gpost = gpost_ref[...]
        h1 = x_ref[...] + n2 * gpost
        pb = p_ref[...].astype(BF16)
        epre = jnp.concatenate([_mm(pb, wple_scr[g]) for g in range(N_DEV // 2)], axis=1)
        r3 = lax.rsqrt(jnp.mean(epre * epre, axis=-1, keepdims=True) + EPS)
        n3 = epre * r3
        gple = gple_ref[...]
        e = n3 * gple
        h1b = h1.astype(BF16)
        gate = _sigmoid(_mm(h1b, wpg_ref[...]))
        diff = h1 + gate * e - tgt_ref[...]
        loss_ref[...] += jnp.sum(diff * diff, axis=0, keepdims=True)

        dy = diff * (1.0 / D_MODEL)
        dgp = (dy * e * gate * (1.0 - gate)).astype(BF16)
        de = dy * gate
        dh1 = dy + _nt(dgp, wpg_ref[...])
        dh1_ref[...] = dh1
        dwpg_ref[...] += _tn(h1b, dgp)
        dgple_ref[...] += jnp.sum(de * n3, axis=0, keepdims=True)
        dn3 = de * gple
        depre = (r3 * (dn3 - n3 * jnp.mean(dn3 * n3, axis=-1, keepdims=True))).astype(BF16)
        for g in range(N_DEV // 2):
            dwp = _tn(pb, depre[:, 2 * g * LANES:2 * (g + 1) * LANES])
            dwple_ref[2 * g] += dwp[:, 0:LANES]
            dwple_ref[2 * g + 1] += dwp[:, LANES:2 * LANES]
        dgpost_ref[...] += jnp.sum(dh1 * n2, axis=0, keepdims=True)
        dn2 = dh1 * gpost
        dmix = (r2 * (dn2 - n2 * jnp.mean(dn2 * n2, axis=-1, keepdims=True))).astype(BF16)
        dcat = _nt(dmix, wout_ref[...])
        dwout_ref[...] += _tn(cat, dmix)

        dys = dcat[:, :D_SSM]
        dyn = dcat[:, D_SSM:]
        dg2 = dys * (zs_v * s_zs)
        dzs_ref[...] = (dys * g2 * (s_zs * (1.0 + zs_v * (1.0 - s_zs)))).astype(BF16)
        dt = dg2 * g2 * (1.0 - sg)
        dtb = dt.astype(BF16)
        dg1 = dg2 * sg + _nt(dtb, wglu_ref[...])
        dwglu_ref[...] += _tn(g1b, dtb)
        dbglu_ref[...] += jnp.sum(dt, axis=0, keepdims=True)
        dyssm = dg1 * _gelu_grad(yv)
        for cl in range(D_SSM // LANES):
            dy_scr[cl] = dyssm[:, cl * LANES:(cl + 1) * LANES]
        _tile_to_pairs(dy_scr, dyssm_ref)
        do_ref[...] = (dyn * (zn_v * s_zn)).astype(BF16)
        dzn_ref[...] = (dyn * o_v * (s_zn * (1.0 + zn_v * (1.0 - s_zn)))).astype(BF16)

    tok = lambda w: pl.BlockSpec((TOK_TILE, w), lambda i: (i, 0))
    row = lambda w: pl.BlockSpec((1, w), lambda i: (0, 0))
    vm = _whole_vmem()
    half_b = jax.ShapeDtypeStruct((seq, D_SSM), BF16)
    gain = jax.ShapeDtypeStruct((1, D_MODEL), F32)
    return pl.pallas_call(
        body, name="mid", grid=(seq // TOK_TILE,),
        in_specs=[_pair_tile(), tok(D_SSM), tok(D_NA), tok(D_NA), tok(D_MODEL), tok(D_PLE), tok(D_MODEL),
                  vm, row(D_SSM), vm, row(D_MODEL), vm, row(D_MODEL), vm],
        out_specs=[_pair_tile(), tok(D_SSM), tok(D_NA), tok(D_NA), tok(D_MODEL), vm, vm, vm, vm, vm, vm, vm, vm],
        out_shape=[jax.ShapeDtypeStruct((N_PAIRS, seq // SSM_CHUNK, CHUNK_W), BF16), half_b, half_b, half_b, jax.ShapeDtypeStruct((seq, D_MODEL), F32), gain,
                   jax.ShapeDtypeStruct((D_SSM, D_SSM), F32), jax.ShapeDtypeStruct((1, D_SSM), F32),
                   jax.ShapeDtypeStruct((D_MODEL, D_MODEL), F32),
                   jax.ShapeDtypeStruct((N_DEV, D_PLE, LANES), F32),
                   jax.ShapeDtypeStruct((D_MODEL, D_MODEL), F32), gain, gain],
        scratch_shapes=[_tile_scratch(), _tile_scratch(), pltpu.VMEM((N_DEV // 2, D_PLE, 2 * LANES), BF16)],
        compiler_params=_params("arbitrary"),
    )(yssm, zs, o, zn, x2, p2, tgt, w_glu, b_glu, w_out, g_post, w_ple_g, g_ple, w_pg)


def _bwd_in(du, dzs, dq, dk, dv, dzn, hn, x2, dh1, g_pre, w_in_g):
    seq = x2.shape[0]

    def body(du_ref, dzs_ref, dq_ref, dk_ref, dv_ref, dzn_ref, hn_ref, x_ref, dh1_ref, g_ref, w_ref,
             dx_ref, dw_ref, dg_ref, dproj_ref, du_scr, w_scr):
        @pl.when(pl.program_id(0) == 0)
        def _():
            dw_ref[...] = jnp.zeros_like(dw_ref)
            dg_ref[...] = jnp.zeros_like(dg_ref)
            _pair_shards(w_ref, w_scr)

        _tile_from_pairs(du_ref, du_scr)
        for cl in range(D_SSM // LANES):
            dproj_ref[:, cl * LANES:(cl + 1) * LANES] = du_scr[cl].astype(BF16)
        for k, ref in enumerate((dzs_ref, dq_ref, dk_ref, dv_ref, dzn_ref), start=1):
            dproj_ref[:, k * D_SSM:(k + 1) * D_SSM] = ref[...].astype(BF16)
        hn = hn_ref[...]
        dhn = jnp.zeros((TOK_TILE, D_MODEL), F32)
        for j in range(IN_GROUPS):
            dpj = dproj_ref[:, j * GROUP_W:(j + 1) * GROUP_W]
            dhn += _nt(dpj, w_scr[j])
            dw = _tn(hn, dpj)
            dw_ref[2 * j] += dw[:, 0:SHARD_IN]
            dw_ref[2 * j + 1] += dw[:, SHARD_IN:GROUP_W]
        x = x_ref[...]
        r = lax.rsqrt(jnp.mean(x * x, axis=-1, keepdims=True) + EPS)
        n1 = x * r
        dg_ref[...] += jnp.sum(dhn * n1, axis=0, keepdims=True)
        dn1 = dhn * g_ref[...]
        dx_ref[...] = dh1_ref[...] + r * (dn1 - n1 * jnp.mean(dn1 * n1, axis=-1, keepdims=True))

    tok = lambda w: pl.BlockSpec((TOK_TILE, w), lambda i: (i, 0))
    vm = _whole_vmem()
    return pl.pallas_call(
        body, name="bwd_in", grid=(seq // TOK_TILE,),
        in_specs=[_pair_tile()] + [tok(D_SSM)] * 5 + [tok(D_MODEL), tok(D_MODEL), tok(D_MODEL),
                                      pl.BlockSpec((1, D_MODEL), lambda i: (0, 0)), vm],
        out_specs=[tok(D_MODEL), vm, vm],
        out_shape=[jax.ShapeDtypeStruct((seq, D_MODEL), F32),
                   jax.ShapeDtypeStruct((N_DEV, D_MODEL, SHARD_IN), F32),
                   jax.ShapeDtypeStruct((1, D_MODEL), F32)],
        scratch_shapes=[pltpu.VMEM((TOK_TILE, D_IN_PROJ), BF16), _tile_scratch(),
                        pltpu.VMEM((IN_GROUPS, D_MODEL, GROUP_W), BF16)],
        compiler_params=_params("arbitrary"),
    )(du, dzs, dq, dk, dv, dzn, hn, x2, dh1, g_pre, w_in_g)


SSM_PAIRS = 4
SLOT = 8


def _boundary_scan(nc, pw_ref, buf_ref, conj):
    pad = [jnp.zeros((SLOT - SSM_PAIRS, LANES), F32)]
    lr0, li0, lr1, li1 = (jnp.concatenate([_pw_row(pw_ref, j, SSM_CHUNK, q) for j in range(SSM_PAIRS)] + pad, axis=0)
                          for q in range(4))
    if conj:
        li0, li1 = -li0, -li1

    def step(c, carry):
        hr0, hi0, hr1, hi1 = carry
        up = pl.ds(pl.multiple_of(c * SLOT, SLOT), SLOT)
        dn = pl.ds(pl.multiple_of((nc - 1 - c) * SLOT, SLOT), SLOT)
        ra, rb = (dn, up) if conj else (up, dn)
        s_r0, s_i0 = buf_ref[0, ra, :], buf_ref[1, ra, :]
        s_r1, s_i1 = buf_ref[2, rb, :], buf_ref[3, rb, :]
        buf_ref[0, ra, :] = hr0
        buf_ref[1, ra, :] = hi0
        buf_ref[2, rb, :] = hr1
        buf_ref[3, rb, :] = hi1
        return (lr0 * hr0 - li0 * hi0 + s_r0, lr0 * hi0 + li0 * hr0 + s_i0,
                lr1 * hr1 - li1 * hi1 + s_r1, lr1 * hi1 + li1 * hr1 + s_i1)

    z = jnp.zeros((SLOT, LANES), F32)
    lax.fori_loop(0, nc, step, (z, z, z, z), unroll=8)


def _put_pair(buf_ref, j, nc, val):
    for q in range(4):
        buf_ref[q, pl.ds(j, nc, stride=SLOT), :] = val[:, LANES * q:LANES * (q + 1)]


def _get_pair(buf_ref, j, nc):
    return jnp.concatenate([buf_ref[q, pl.ds(j, nc, stride=SLOT), :] for q in range(4)], axis=1)


def _pw_row(pw_ref, j, k, q):
    return pw_ref[j, k:k + 1, LANES * q:LANES * (q + 1)]


def _mm_f32(a, b, dims):
    return lax.dot_general(a, b, (dims, ((), ())), precision=HIGHEST, preferred_element_type=F32)


_POW_M = (lambda s: SSM_CHUNK - 1 - s, lambda s: s)
_POW_C = (lambda s: s + 1, lambda s: SSM_CHUNK - s)
_POW_K = (lambda s: s, lambda s: SSM_CHUNK - 1 - s)


def _chunk_matrices(j, pw_ref, bb_ref, cc_ref, dd_ref, m_scr, ct_scr, toep_scr, g_scr, kt_scr):
    blk = lambda s: slice(PAIR_W * s, PAIR_W * (s + 1))
    col = lambda q: slice(LANES * q, LANES * (q + 1))
    for d in range(2):
        bbr, bbi = bb_ref[j, 2 * d], bb_ref[j, 2 * d + 1]
        ccr, cci = cc_ref[j, 2 * d], cc_ref[j, 2 * d + 1]
        for s in range(SSM_CHUNK):
            pr, pi = _pw_row(pw_ref, j, _POW_M[d](s), 2 * d), _pw_row(pw_ref, j, _POW_M[d](s), 2 * d + 1)
            m_scr[j, blk(s), col(2 * d)] = (pr * bbr - pi * bbi).astype(m_scr.dtype)
            m_scr[j, blk(s), col(2 * d + 1)] = (pr * bbi + pi * bbr).astype(m_scr.dtype)
            pr, pi = _pw_row(pw_ref, j, _POW_C[d](s), 2 * d), _pw_row(pw_ref, j, _POW_C[d](s), 2 * d + 1)
            ct_scr[j, blk(s), col(2 * d)] = (ccr * pr - cci * pi).astype(ct_scr.dtype)
            ct_scr[j, blk(s), col(2 * d + 1)] = (-(ccr * pi + cci * pr)).astype(ct_scr.dtype)
            pr, pi = _pw_row(pw_ref, j, _POW_K[d](s), 2 * d), _pw_row(pw_ref, j, _POW_K[d](s), 2 * d + 1)
            g_scr[j, d, blk(s), 0:LANES] = ccr * pr - cci * pi
            g_scr[j, d, blk(s), LANES:2 * LANES] = -(ccr * pi + cci * pr)
        kt = _mm_f32(jnp.concatenate([bbr, bbi], axis=1), g_scr[j, d], ((1,), (1,)))
        if d == 0:
            kt = jnp.concatenate([kt[:, 0:LANES] + dd_ref[j], kt[:, LANES:]], axis=1)
        kt_scr[d] = kt
    lane = lax.broadcasted_iota(jnp.int32, (1, CHUNK_W), 1)
    for s in range(SSM_CHUNK):
        lo = PAIR_W * s
        hi = PAIR_W * (s + 1)
        fwd = kt_scr[0] if s == 0 else pltpu.roll(kt_scr[0], lo, axis=1)
        bwd = kt_scr[1] if hi == CHUNK_W else pltpu.roll(kt_scr[1], hi, axis=1)
        row = jnp.where(lane >= lo, fwd, 0.0) + jnp.where(lane < hi, bwd, 0.0)
        toep_scr[j, blk(s), :] = row.astype(toep_scr.dtype)


def _ssm_scratch(nc, mat_dtype, buffers):
    mats = [pltpu.VMEM((SSM_PAIRS, CHUNK_W, CHUNK_W), mat_dtype) for _ in range(3)]
    return mats + [pltpu.VMEM((SSM_PAIRS, 2, CHUNK_W, 2 * LANES), F32), pltpu.VMEM((2, PAIR_W, CHUNK_W), F32)] + [
        pltpu.VMEM((4, nc * SLOT, LANES), F32) for _ in range(buffers)]


def _per_step(*shape):
    return pl.BlockSpec((SSM_PAIRS,) + shape, lambda g: (g,) + (0,) * len(shape))


_TABLE_SPECS = lambda: [_per_step(PW_ROWS, CHUNK_W), _per_step(4, PAIR_W, LANES), _per_step(4, PAIR_W, LANES),
                        _per_step(PAIR_W, LANES)]


def _ssm_fwd(u, pw, bb, cc, dd):
    npair, nc, width = u.shape

    def body(u_ref, pw_ref, bb_ref, cc_ref, dd_ref, y_ref, m_scr, ct_scr, toep_scr, g_scr, kt_scr, h_scr):
        @pl.when(pl.program_id(0) == 0)
        def _():
            h_scr[...] = jnp.zeros_like(h_scr)

        for j in range(SSM_PAIRS):
            _chunk_matrices(j, pw_ref, bb_ref, cc_ref, dd_ref, m_scr, ct_scr, toep_scr, g_scr, kt_scr)
            _put_pair(h_scr, j, nc, _mm(u_ref[j], m_scr[j]))
        _boundary_scan(nc, pw_ref, h_scr, conj=False)
        for j in range(SSM_PAIRS):
            hin = _get_pair(h_scr, j, nc)
            y_ref[j] = _mm(u_ref[j], toep_scr[j]) + _nt(hin.astype(u.dtype), ct_scr[j])

    return pl.pallas_call(
        body, name="ssm_fwd", grid=(npair // SSM_PAIRS,),
        in_specs=[_per_step(nc, width)] + _TABLE_SPECS(),
        out_specs=_per_step(nc, width),
        out_shape=jax.ShapeDtypeStruct((npair, nc, width), F32),
        scratch_shapes=_ssm_scratch(nc, u.dtype, 1),
        compiler_params=_params("arbitrary"),
    )(u, pw, bb, cc, dd)


def _ssm_bwd(u, dy, pw, bb, cc, dd, chip_parts=()):
    npair, nc, width = u.shape
    ns = len(chip_parts)
    steps = npair // SSM_PAIRS

    def body(u_ref, dy_ref, pw_ref, bb_ref, cc_ref, dd_ref, *rest):
        part_refs, rest = rest[:ns], rest[ns:]
        (du_ref, dpw_ref, dbb_ref, dcc_ref, ddd_ref), rest = rest[:5], rest[5:]
        land_refs, rest = rest[:ns], rest[ns:]
        m_scr, ct_scr, toep_scr, g_scr, kt_scr, h_scr, d_scr = rest[:7]
        g = pl.program_id(0)
        if ns:
            _ride(_chip_copies(part_refs, land_refs, *rest[7:]), g == 0, g == steps - 1)

        @pl.when(g == 0)
        def _():
            h_scr[...] = jnp.zeros_like(h_scr)
            d_scr[...] = jnp.zeros_like(d_scr)

        for j in range(SSM_PAIRS):
            _chunk_matrices(j, pw_ref, bb_ref, cc_ref, dd_ref, m_scr, ct_scr, toep_scr, g_scr, kt_scr)
            _put_pair(h_scr, j, nc, _mm(u_ref[j], m_scr[j]))
            _put_pair(d_scr, j, nc, _mm(dy_ref[j], ct_scr[j]))
        _boundary_scan(nc, pw_ref, h_scr, conj=False)
        _boundary_scan(nc, pw_ref, d_scr, conj=True)

        dpw_ref[...] = jnp.zeros_like(dpw_ref)
        blk = lambda s: slice(PAIR_W * s, PAIR_W * (s + 1))
        col = lambda q: slice(LANES * q, LANES * (q + 1))
        lane = lax.broadcasted_iota(jnp.int32, (1, CHUNK_W), 1)
        for j in range(SSM_PAIRS):
            uv = u_ref[j]
            dyb = dy_ref[j]
            hin = _get_pair(h_scr, j, nc)
            ds = _get_pair(d_scr, j, nc)
            dsb = ds.astype(u.dtype)
            du_ref[j] = _nt(dsb, m_scr[j]) + _nt(dyb, toep_scr[j])
            dm = _tn(uv, dsb)
            dct = _tn(dyb, hin.astype(u.dtype))
            dtoep = _tn(uv, dyb)

            def add_pw(k, q, val):
                dpw_ref[j, k:k + 1, col(q)] += jnp.sum(val, axis=0, keepdims=True)

            for d in range(2):
                g_r, g_i = ds[:, col(2 * d)], ds[:, col(2 * d + 1)]
                h_r, h_i = hin[:, col(2 * d)], hin[:, col(2 * d + 1)]
                add_pw(SSM_CHUNK, 2 * d, g_r * h_r + g_i * h_i)
                add_pw(SSM_CHUNK, 2 * d + 1, g_i * h_r - g_r * h_i)

            dkt0 = jnp.zeros((PAIR_W, CHUNK_W), F32)
            dkt1 = jnp.zeros((PAIR_W, CHUNK_W), F32)
            for s in range(SSM_CHUNK):
                lo = PAIR_W * s
                hi = PAIR_W * (s + 1)
                row = dtoep[blk(s), :]
                fwd = jnp.where(lane >= lo, row, 0.0)
                bwd = jnp.where(lane < hi, row, 0.0)
                dkt0 += fwd if s == 0 else pltpu.roll(fwd, CHUNK_W - lo, axis=1)
                dkt1 += bwd if hi == CHUNK_W else pltpu.roll(bwd, CHUNK_W - hi, axis=1)
            ddd_ref[j] = dkt0[:, 0:LANES]

            for d, dkt in enumerate((dkt0, dkt1)):
                bbr, bbi = bb_ref[j, 2 * d], bb_ref[j, 2 * d + 1]
                ccr, cci = cc_ref[j, 2 * d], cc_ref[j, 2 * d + 1]
                dbbcat = _mm_f32(dkt, g_scr[j, d], ((1,), (0,)))
                dg = _mm_f32(dkt, jnp.concatenate([bbr, bbi], axis=1), ((0,), (0,)))
                dbbr, dbbi = dbbcat[:, 0:LANES], dbbcat[:, LANES:]
                dccr = jnp.zeros((PAIR_W, LANES), F32)
                dcci = jnp.zeros((PAIR_W, LANES), F32)
                for s in range(SSM_CHUNK):
                    k = _POW_M[d](s)
                    pr, pi = _pw_row(pw_ref, j, k, 2 * d), _pw_row(pw_ref, j, k, 2 * d + 1)
                    gr, gi = dm[blk(s), col(2 * d)], dm[blk(s), col(2 * d + 1)]
                    dbbr += gr * pr + gi * pi
                    dbbi += gi * pr - gr * pi
                    add_pw(k, 2 * d, gr * bbr + gi * bbi)
                    add_pw(k, 2 * d + 1, gi * bbr - gr * bbi)
                    for k, gr, gi in ((_POW_C[d](s), dct[blk(s), col(2 * d)], dct[blk(s), col(2 * d + 1)]),
                                      (_POW_K[d](s), dg[blk(s), 0:LANES], dg[blk(s), LANES:])):
                        pr, pi = _pw_row(pw_ref, j, k, 2 * d), _pw_row(pw_ref, j, k, 2 * d + 1)
                        dccr += gr * pr - gi * pi
                        dcci += -(gr * pi + gi * pr)
                        add_pw(k, 2 * d, gr * ccr - gi * cci)
                        add_pw(k, 2 * d + 1, -(gr * cci + gi * ccr))
                dbb_ref[j, 2 * d] = dbbr
                dbb_ref[j, 2 * d + 1] = dbbi
                dcc_ref[j, 2 * d] = dccr
                dcc_ref[j, 2 * d + 1] = dcci

    out = pl.pallas_call(
        body, name="ssm_bwd", grid=(steps,),
        in_specs=[_per_step(nc, width), _per_step(nc, width)] + _TABLE_SPECS() + [_ANY()] * ns,
        out_specs=[_per_step(nc, width)] + _TABLE_SPECS() + [_ANY()] * ns,
        out_shape=[jax.ShapeDtypeStruct((npair, nc, width), F32), jax.ShapeDtypeStruct(pw.shape, F32),
                   jax.ShapeDtypeStruct(bb.shape, F32), jax.ShapeDtypeStruct(cc.shape, F32),
                   jax.ShapeDtypeStruct(dd.shape, F32)]
        + [jax.ShapeDtypeStruct(pt.shape, pt.dtype) for pt in chip_parts],
        scratch_shapes=_ssm_scratch(nc, u.dtype, 2) + ([_dma_sems(ns, 3), _dma_sems(ns, 3), _dma_sems(ns)] if ns else []),
        compiler_params=_params("arbitrary"),
    )(u, dy, pw, bb, cc, dd, *chip_parts)
    return out[:5], list(out[5:])


NA_Q = NA_QROWS * GRID_W
NA_K = NA_KROWS * GRID_W
NA_SCALE = NA_HEAD_DIM ** -0.5


def _na_block(b, nb, rows):
    start = jnp.clip(NA_QROWS * b - NA_ROWS // 2, 0, rows - NA_KROWS) * GRID_W
    kind = jnp.where(b == 0, 0, jnp.where(b == nb - 1, 2, 1))
    return pl.multiple_of(start, GRID_W), kind


NA_CHUNK = 16


def _na_pieces(kind, i):
    ri, q0 = divmod(i * NA_CHUNK, GRID_W)
    off = (NA_ROWS - 1, NA_ROWS // 2 - 1, -1)[kind]
    lo = (0, ri, NA_KROWS - NA_ROWS)[kind]
    modes = {(True, True): 'both', (True, False): 'even', (False, True): 'odd', (False, False): None}
    out = []
    for k2 in range(NA_KROWS // 2):
        inside = tuple(lo <= kr < lo + NA_ROWS for kr in (2 * k2, 2 * k2 + 1))
        out.append((2 * k2 - ri + off + 1, slice(q0, q0 + NA_CHUNK), modes[inside]))
    return out


def _na_softmax_pieces(s_ref, tab_ref, hh, kind, i):
    rows = slice(i * NA_CHUNK, (i + 1) * NA_CHUNK)
    lane = lax.broadcasted_iota(jnp.int32, (1, LANES), 1)
    xs = []
    for k2, (t, q, mode) in enumerate(_na_pieces(kind, i)):
        if mode is None:
            xs.append(None)
            continue
        bias = tab_ref[hh, t, q, :]
        if mode == 'even':
            bias = jnp.where(lane < GRID_W, bias, NEG)
        elif mode == 'odd':
            bias = jnp.where(lane >= GRID_W, bias, NEG)
        xs.append(s_ref[hh, rows, k2 * LANES:(k2 + 1) * LANES] + bias)
    live = [x for x in xs if x is not None]
    m = jnp.max(functools.reduce(jnp.maximum, live), axis=-1, keepdims=True)
    es = [None if x is None else jnp.exp(x - m) for x in xs]
    total = jnp.sum(functools.reduce(jnp.add, [e for e in es if e is not None]), axis=-1, keepdims=True)
    inv = 1.0 / total
    return [None if e is None else e * inv for e in es]


def _na_heads():
    lane = lax.broadcasted_iota(jnp.int32, (1, LANES), 1)
    return [lane < NA_HEAD_DIM, lane >= NA_HEAD_DIM]


def _na_fwd(qkv, bias):
    seq = qkv.shape[0]
    rows = seq // GRID_W
    nb = rows // NA_QROWS

    def body(q_ref, k_ref, v_ref, bias_ref, o_ref, p_ref, s_scr):
        start, kind = _na_block(pl.program_id(1), nb, rows)

        def block(static_kind):
            q2 = q_ref[...] * NA_SCALE
            kw = k_ref[pl.ds(start, NA_K), :]
            vw = v_ref[pl.ds(start, NA_K), :]
            heads = _na_heads()
            for hh in range(2):
                s_scr[hh] = _nt(jnp.where(heads[hh], q2, jnp.zeros_like(q2)), kw)
            for hh in range(2):
                for i in range(NA_Q // NA_CHUNK):
                    r = slice(i * NA_CHUNK, (i + 1) * NA_CHUNK)
                    for k2, p in enumerate(_na_softmax_pieces(s_scr, bias_ref, hh, static_kind, i)):
                        p = jnp.zeros((NA_CHUNK, LANES), F32) if p is None else p
                        p_ref[hh, r, k2 * LANES:(k2 + 1) * LANES] = p.astype(p_ref.dtype)
            o_ref[...] = jnp.where(heads[0], _mm(p_ref[0], vw), _mm(p_ref[1], vw))

        for static_kind in range(3):
            pl.when(kind == static_kind)(functools.partial(block, static_kind))

    return pl.pallas_call(
        body, name="na_fwd", grid=(NA_HEADS // 2, nb),
        in_specs=[pl.BlockSpec((NA_Q, LANES), lambda hp, b: (b, hp)),
                  pl.BlockSpec((seq, LANES), lambda hp, b: (0, 4 + hp)),
                  pl.BlockSpec((seq, LANES), lambda hp, b: (0, 8 + hp)),
                  pl.BlockSpec((2, NA_TAB, GRID_W, LANES), lambda hp, b: (hp, 0, 0, 0))],
        out_specs=[pl.BlockSpec((NA_Q, LANES), lambda hp, b: (b, hp)),
                   pl.BlockSpec((2, NA_Q, NA_K), lambda hp, b: (hp, b, 0))],
        out_shape=[jax.ShapeDtypeStruct((seq, D_NA), F32), jax.ShapeDtypeStruct((NA_HEADS, seq, NA_K), qkv.dtype)],
        scratch_shapes=[pltpu.VMEM((2, NA_Q, NA_K), F32)],
        compiler_params=_params("arbitrary", "arbitrary"),
    )(qkv, qkv, qkv, bias)


def _na_bwd(qkv, do, probs, parts=()):
    seq = qkv.shape[0]
    rows = seq // GRID_W
    nb = rows // NA_QROWS
    ns = len(parts)

    def body(q_ref, k_ref, v_ref, do_ref, p_ref, *rest):
        part_refs, rest = rest[:ns], rest[ns:]
        (dq_ref, dk_ref, dv_ref, dbias_ref), rest = rest[:4], rest[4:]
        land_refs, rest = rest[:ns], rest[ns:]
        dp_scr, ds_scr = rest[:2]
        b = pl.program_id(1)
        start, kind = _na_block(b, nb, rows)
        if ns:
            hp = pl.program_id(0)
            _ride(_pair_copies(part_refs, land_refs, *rest[2:]), (hp == 0) & (b == 0),
                  (hp == NA_HEADS // 2 - 1) & (b == nb - 1))

        @pl.when(b == 0)
        def _():
            dk_ref[...] = jnp.zeros_like(dk_ref)
            dv_ref[...] = jnp.zeros_like(dv_ref)
            dbias_ref[...] = jnp.zeros_like(dbias_ref)

        def block(static_kind):
            q2 = q_ref[...] * NA_SCALE
            kw = k_ref[pl.ds(start, NA_K), :]
            vw = v_ref[pl.ds(start, NA_K), :]
            do2 = do_ref[...].astype(q2.dtype)
            heads = _na_heads()
            col = lambda k2: slice(k2 * LANES, (k2 + 1) * LANES)
            zero = jnp.zeros((NA_CHUNK, LANES), ds_scr.dtype)
            for hh in range(2):
                dp_scr[hh] = _nt(jnp.where(heads[hh], do2, jnp.zeros_like(do2)), vw)
            for hh in range(2):
                for i in range(NA_Q // NA_CHUNK):
                    r = slice(i * NA_CHUNK, (i + 1) * NA_CHUNK)
                    pieces = _na_pieces(static_kind, i)
                    ps = [None if mode is None else p_ref[hh, r, col(k2)].astype(F32)
                          for k2, (_, _, mode) in enumerate(pieces)]
                    dps = [None if p is None else dp_scr[hh, r, col(k2)] for k2, p in enumerate(ps)]
                    pdp = functools.reduce(jnp.add, [p * dp for p, dp in zip(ps, dps) if p is not None])
                    rowsum = jnp.sum(pdp, axis=-1, keepdims=True)
                    for k2, (t, q, _) in enumerate(pieces):
                        if ps[k2] is None:
                            ds_scr[hh, r, col(k2)] = zero
                            continue
                        ds = ps[k2] * (dps[k2] - rowsum)
                        dbias_ref[hh, t, q, :] += ds
                        ds_scr[hh, r, col(k2)] = ds.astype(ds_scr.dtype)
            dq_ref[...] = jnp.where(heads[0], _mm(ds_scr[0], kw), _mm(ds_scr[1], kw)) * NA_SCALE
            dk_ref[pl.ds(start, NA_K), :] += jnp.where(heads[0], _tn(ds_scr[0], q2), _tn(ds_scr[1], q2))
            dv_ref[pl.ds(start, NA_K), :] += jnp.where(heads[0], _tn(p_ref[0], do2), _tn(p_ref[1], do2))

        for static_kind in range(3):
            pl.when(kind == static_kind)(functools.partial(block, static_kind))

    out = pl.pallas_call(
        body, name="na_bwd", grid=(NA_HEADS // 2, nb),
        in_specs=[pl.BlockSpec((NA_Q, LANES), lambda hp, b: (b, hp)),
                  pl.BlockSpec((seq, LANES), lambda hp, b: (0, 4 + hp)),
                  pl.BlockSpec((seq, LANES), lambda hp, b: (0, 8 + hp)),
                  pl.BlockSpec((NA_Q, LANES), lambda hp, b: (b, hp)),
                  pl.BlockSpec((2, NA_Q, NA_K), lambda hp, b: (hp, b, 0))] + [_ANY()] * ns,
        out_specs=[pl.BlockSpec((NA_Q, LANES), lambda hp, b: (b, hp)),
                   pl.BlockSpec((seq, LANES), lambda hp, b: (0, hp)),
                   pl.BlockSpec((seq, LANES), lambda hp, b: (0, hp)),
                   pl.BlockSpec((2, NA_TAB, GRID_W, LANES), lambda hp, b: (hp, 0, 0, 0))] + [_ANY()] * ns,
        out_shape=[jax.ShapeDtypeStruct((seq, D_NA), F32), jax.ShapeDtypeStruct((seq, D_NA), F32),
                   jax.ShapeDtypeStruct((seq, D_NA), F32),
                   jax.ShapeDtypeStruct((NA_HEADS, NA_TAB, GRID_W, LANES), F32)]
        + [jax.ShapeDtypeStruct((4,) + pt.shape[1:], pt.dtype) for pt in parts],
        scratch_shapes=[pltpu.VMEM((2, NA_Q, NA_K), F32), pltpu.VMEM((2, NA_Q, NA_K), qkv.dtype)]
        + ([_dma_sems(ns, 4), _dma_sems(ns, 4)] if ns else []),
        compiler_params=_params("arbitrary", "arbitrary"),
    )(qkv, qkv, qkv, do, probs, *parts)
    return out[:4], list(out[4:])


def _local_step(x2, p2, tgt, g_pre, g_post, w_in_g, ssm, w_glu, b_glu, rpb, w_out, w_ple, g_ple, w_pg,
                distributed=False):
    (pw, bb, cc, dd), ssm_vjp = jax.vjp(_ssm_tables, *ssm)
    bias, bias_vjp = jax.vjp(_na_table, rpb)

    riders = (w_glu, w_out, w_ple, w_pg) if distributed else ()
    (u_p, zs, qkv, zn, hn), gathered = _fwd_in(x2, g_pre, w_in_g, riders)
    if distributed:
        w_glu, w_out, w_ple, w_pg = gathered
        w_glu = w_glu.reshape(D_SSM, D_SSM)
        w_out = w_out.reshape(D_MODEL, D_MODEL)
        w_pg = w_pg.reshape(D_MODEL, D_MODEL)
    yssm = _ssm_fwd(u_p, pw, bb, cc, dd)
    o, probs = _na_fwd(qkv, bias)
    dyssm, dzs, do, dzn, dh1, sq, d_wglu, d_bglu, d_wout, d_wple, d_wpg, d_gpost, d_gple = _mid(
        yssm, zs, o, zn, x2, p2, tgt, w_glu, b_glu, w_out, g_post, w_ple, g_ple, w_pg)
    mid_grads = [d_wglu.reshape(N_DEV, D_SSM // N_DEV, D_SSM), d_wout.reshape(N_DEV, D_MODEL // N_DEV, D_MODEL),
                 d_wple, d_wpg.reshape(N_DEV, D_MODEL // N_DEV, D_MODEL)]
    (dq, dk, dv, dbias), lands = _na_bwd(qkv, do, probs, mid_grads if distributed else ())
    (d_rpb,) = bias_vjp(dbias)
    chip_parts = _pair_sum(mid_grads, lands)[0] if distributed else ()
    (du_p, dpw, dbb, dcc, ddd), chip_lands = _ssm_bwd(u_p, dyssm, pw, bb, cc, dd, chip_parts)
    d_ssm = ssm_vjp((dpw, dbb, dcc, ddd))
    dx, d_win, d_gpre = _bwd_in(du_p, dzs, dq, dk, dv, dzn, hn, x2, dh1, g_pre, w_in_g)
    if distributed:
        d_wglu, d_wout, d_wple, d_wpg = chip_lands
    return sq, dx, dict(norm_pre=d_gpre, norm_post=d_gpost, w_in=d_win, ssm=d_ssm, w_glu=d_wglu, b_glu=d_bglu,
                        na_rpb=d_rpb, w_out=d_wout, w_ple=d_wple, ple_norm=d_gple, w_ple_gate=d_wpg)


def _all_gather(shard):
    m_per, n = shard.shape

    def body(x_ref, out_ref, send_sems, recv_sems, local_sem):
        x, y, c = _place()
        me, sibling = (x, y, c), (x, y, 1 - c)
        chips = [(1 - x, y), (x, 1 - y), (1 - x, 1 - y)]

        def rows(px, py, pc):
            return out_ref.at[pl.ds((4 * px + 2 * py + pc) * m_per, m_per), :]

        def copy(k, block, to, src=None):
            return pltpu.make_async_remote_copy(
                src_ref=rows(*block) if src is None else src, dst_ref=rows(*block),
                send_sem=send_sems.at[k], recv_sem=recv_sems.at[k], device_id=to, device_id_type=MESH)

        mine = pltpu.make_async_copy(x_ref, rows(*me), local_sem)
        mine.start()
        first = [copy(0, me, sibling, src=x_ref)]
        first += [copy(1 + j, me, (*chip, c), src=x_ref) for j, chip in enumerate(chips)]
        for cp in first:
            cp.start()
        passed = [copy(4 + j, (*chip, c), sibling) for j, chip in enumerate(chips)]
        for j, chip in enumerate(chips):
            copy(1 + j, (*chip, c), me).wait_recv()
            passed[j].start()
        copy(0, sibling, me).wait_recv()
        for j, chip in enumerate(chips):
            copy(4 + j, (*chip, 1 - c), me).wait_recv()
        for cp in first + passed:
            cp.wait_send()
        mine.wait()

    return pl.pallas_call(
        body, name="all_gather",
        out_shape=jax.ShapeDtypeStruct((N_DEV * m_per, n), shard.dtype),
        in_specs=[_whole_vmem()], out_specs=_whole_vmem(),
        scratch_shapes=[pltpu.SemaphoreType.DMA((7,)), pltpu.SemaphoreType.DMA((7,)), pltpu.SemaphoreType.DMA],
        compiler_params=pltpu.CompilerParams(vmem_limit_bytes=VMEM_LIMIT),
    )(shard)


def _exchange_first(pair_parts, all_parts):
    n1, n2 = len(pair_parts), len(all_parts)

    def body(*refs):
        ins, refs = refs[:n1 + n2], refs[n1 + n2:]
        outs, sems = refs[:n1 + n2], refs[n1 + n2:]
        copies = (_pair_copies(ins[:n1], outs[:n1], *sems[:2])
                  + _alltoall_copies(ins[n1:], outs[n1:], *sems[2:]))
        _start_all(copies)
        _wait_all(copies)

    out = pl.pallas_call(
        body, name="exchange_first",
        out_shape=[jax.ShapeDtypeStruct((4,) + pt.shape[1:], pt.dtype) for pt in pair_parts]
        + [jax.ShapeDtypeStruct(pt.shape, pt.dtype) for pt in all_parts],
        in_specs=[_ANY()] * (n1 + n2), out_specs=[_ANY()] * (n1 + n2),
        scratch_shapes=[_dma_sems(n1, 4), _dma_sems(n1, 4),
                        _dma_sems(n2, N_DEV - 1), _dma_sems(n2, N_DEV - 1), _dma_sems(n2)],
    )(*pair_parts, *all_parts)
    return list(out[:n1]), list(out[n1:])


def _pair_sum(parts, lands, all_lands=()):
    ns, na = len(parts), len(all_lands)

    def body(*refs):
        c = lax.axis_index("c")
        ins, outs = refs[:2 * ns + na], refs[2 * ns + na:]
        for part_ref, land_ref, out_ref in zip(ins[:ns], ins[ns:2 * ns], outs[:ns]):
            for q in range(4):
                out_ref[q] = (part_ref[q, c] + land_ref[q]).astype(BF16)
        for land_ref, out_ref in zip(ins[2 * ns:], outs[ns:]):
            acc = land_ref[0]
            for j in range(1, N_DEV):
                acc = acc + land_ref[j]
            out_ref[...] = acc

    out = pl.pallas_call(
        body, name="pair_sum",
        in_specs=[_whole_vmem()] * (2 * ns + na), out_specs=[_whole_vmem()] * (ns + na),
        out_shape=[jax.ShapeDtypeStruct(ld.shape, BF16) for ld in lands]
        + [jax.ShapeDtypeStruct(ld.shape[1:], ld.dtype) for ld in all_lands],
        compiler_params=pltpu.CompilerParams(vmem_limit_bytes=VMEM_LIMIT),
    )(*[pt.reshape((4, 2) + pt.shape[1:]) for pt in parts], *lands, *all_lands)
    return list(out[:ns]), list(out[ns:])


def _exchange_second(chip_parts, shards):
    n1, n2 = len(chip_parts), len(shards)

    def body(*refs):
        ins, refs = refs[:n1 + n2], refs[n1 + n2:]
        outs, sems = refs[:n1 + n2], refs[n1 + n2:]
        copies = (_chip_copies(ins[:n1], outs[:n1], *sems[:3])
                  + _gather_copies(ins[n1:], outs[n1:], *sems[3:]))
        _start_all(copies)
        _wait_all(copies)

    out = pl.pallas_call(
        body, name="exchange_second",
        out_shape=[jax.ShapeDtypeStruct(pt.shape, pt.dtype) for pt in chip_parts]
        + [jax.ShapeDtypeStruct((N_DEV,) + sh.shape, sh.dtype) for sh in shards],
        in_specs=[_ANY()] * (n1 + n2), out_specs=[_ANY()] * (n1 + n2),
        scratch_shapes=[_dma_sems(n1, 3), _dma_sems(n1, 3), _dma_sems(n1),
                        _dma_sems(n2, N_DEV - 1), _dma_sems(n2, N_DEV - 1), _dma_sems(n2)],
    )(*chip_parts, *shards)
    return list(out[:n1]), list(out[n1:])


ADAM_ROWS = 256


def _adamw(ws, gs, ms, vs, slab=None):
    n = len(ws)
    arrays = [g for g in gs if not isinstance(g, int)]
    n_in = 3 * n + len(arrays) + (slab is not None)

    def body(*refs):
        ins, outs = refs[:n_in], refs[n_in:]
        w_refs, m_refs, v_refs = ins[:n], ins[n:2 * n], ins[2 * n:3 * n]
        g_refs = iter(ins[3 * n:3 * n + len(arrays)])
        for t in range(n):
            w_ref, m_ref, v_ref = w_refs[t], m_refs[t], v_refs[t]
            go_ref, d_ref, nm_ref, nv_ref = outs[4 * t:4 * t + 4]
            g_ref = None if isinstance(gs[t], int) else next(g_refs)
            rows = w_ref.shape[0]
            for lo in range(0, rows, ADAM_ROWS):
                hi = min(lo + ADAM_ROWS, rows)
                r = slice(lo, hi)
                if g_ref is None:
                    g = ins[-1][gs[t] + lo:gs[t] + hi, :]
                elif len(g_ref.shape) == 3:
                    g = g_ref[0, r, :].astype(F32)
                    for j in range(1, g_ref.shape[0]):
                        g = g + g_ref[j, r, :].astype(F32)
                else:
                    g = g_ref[r, :]
                go_ref[r, :] = g
                nm = ADAM_B1 * m_ref[r, :] + (1.0 - ADAM_B1) * g
                nv = ADAM_B2 * v_ref[r, :] + (1.0 - ADAM_B2) * (g * g)
                m_hat = nm / (1.0 - ADAM_B1 ** ADAM_STEP)
                v_hat = nv / (1.0 - ADAM_B2 ** ADAM_STEP)
                d_ref[r, :] = -ADAM_LR * (m_hat / (jnp.sqrt(v_hat) + ADAM_EPS) + ADAM_WD * w_ref[r, :])
                nm_ref[r, :] = nm
                nv_ref[r, :] = nv

    out = pl.pallas_call(
        body, name="adamw",
        in_specs=[_whole_vmem()] * n_in, out_specs=[_whole_vmem()] * (4 * n),
        out_shape=[jax.ShapeDtypeStruct(w.shape, F32) for w in ws for _ in range(4)],
        compiler_params=pltpu.CompilerParams(vmem_limit_bytes=VMEM_LIMIT),
    )(*ws, *ms, *vs, *arrays, *([slab] if slab is not None else []))
    return [tuple(out[4 * t:4 * t + 4]) for t in range(n)]


_SLAB = 8 * LANES


def _flat_rows(a):
    flat = a.reshape(-1)
    pad = (-flat.shape[0]) % _SLAB
    if pad:
        flat = jnp.concatenate([flat, jnp.zeros((pad,), flat.dtype)])
    return flat.reshape(-1, LANES)


def _pack(arrays):
    slabs = [_flat_rows(a) for a in arrays]
    return jnp.concatenate(slabs, axis=0), [s.shape[0] for s in slabs]


SMALL = ('norm_pre', 'norm_post', 'ssm_a_re', 'ssm_a_im', 'ssm_log_dt', 'ssm_b_re', 'ssm_b_im', 'ssm_c_re',
         'ssm_c_im', 'ssm_d', 'b_glu', 'na_rpb', 'ple_norm')
BIG = ('w_in', 'w_glu', 'w_out', 'w_ple', 'w_ple_gate')
ORDER = ('norm_pre', 'norm_post', 'w_in', 'ssm_a_re', 'ssm_a_im', 'ssm_log_dt', 'ssm_b_re', 'ssm_b_im', 'ssm_c_re',
         'ssm_c_im', 'ssm_d', 'w_glu', 'b_glu', 'na_rpb', 'w_out', 'w_ple', 'ple_norm', 'w_ple_gate')


def kernel(x, p, norm_pre, norm_post, w_in, ssm_a_re, ssm_a_im, ssm_log_dt, ssm_b_re, ssm_b_im, ssm_c_re, ssm_c_im, ssm_d, w_glu, b_glu, na_rpb, w_out, w_ple, ple_norm, w_ple_gate, loss_target, m_norm_pre, m_norm_post, m_w_in, m_ssm_a_re, m_ssm_a_im, m_ssm_log_dt, m_ssm_b_re, m_ssm_b_im, m_ssm_c_re, m_ssm_c_im, m_ssm_d, m_w_glu, m_b_glu, m_na_rpb, m_w_out, m_w_ple, m_ple_norm, m_w_ple_gate, v_norm_pre, v_norm_post, v_w_in, v_ssm_a_re, v_ssm_a_im, v_ssm_log_dt, v_ssm_b_re, v_ssm_b_im, v_ssm_c_re, v_ssm_c_im, v_ssm_d, v_w_glu, v_b_glu, v_na_rpb, v_w_out, v_w_ple, v_ple_norm, v_w_ple_gate):
    args = dict(locals())
    weights = {n: args[n] for n in ORDER}
    mom_m = {n: args["m_" + n] for n in ORDER}
    mom_v = {n: args["v_" + n] for n in ORDER}

    w_in_g = _all_gather(w_in[0].astype(BF16)).reshape(N_DEV, D_MODEL, SHARD_IN)
    blocks = [weights[n][0].astype(BF16) for n in ('w_glu', 'w_out', 'w_ple', 'w_ple_gate')]

    ssm = tuple(weights[n][0] for n in ('ssm_a_re', 'ssm_a_im', 'ssm_log_dt', 'ssm_b_re', 'ssm_b_im',
                                        'ssm_c_re', 'ssm_c_im', 'ssm_d'))
    sq, dx, grads = _local_step(x[0], p[0, 0], loss_target[0], norm_pre, norm_post, w_in_g, ssm, blocks[0], b_glu,
                                na_rpb[0], blocks[1], blocks[2], ple_norm, blocks[3], distributed=True)
    loss_local = (0.5 / D_MODEL * jnp.sum(sq)).reshape(1)

    local = dict(norm_pre=grads['norm_pre'], norm_post=grads['norm_post'], b_glu=grads['b_glu'],
                 na_rpb=grads['na_rpb'][None], ple_norm=grads['ple_norm'])
    for n, g in zip(('ssm_a_re', 'ssm_a_im', 'ssm_log_dt', 'ssm_b_re', 'ssm_b_im', 'ssm_c_re', 'ssm_c_im', 'ssm_d'),
                    grads['ssm']):
        local[n] = g[None]

    small_flat, _ = _pack([local[n] for n in SMALL] + [loss_local])
    pad = (-small_flat.shape[0]) % (8 * N_DEV)
    if pad:
        small_flat = jnp.concatenate([small_flat, jnp.zeros((pad, LANES), F32)], axis=0)
    small_per = small_flat.shape[0] // N_DEV
    d_win = [grads['w_in']]
    lands, small_lands = _exchange_first(d_win, [small_flat.reshape(N_DEV, small_per, LANES)])
    chip_parts, small_sums = _pair_sum(d_win, lands, small_lands)
    chip_lands, small_full = _exchange_second(chip_parts, small_sums)
    grads['w_in'] = chip_lands[0]
    small_all = small_full[0].reshape(N_DEV * small_per, LANES)

    like = [weights[n] for n in SMALL]
    slabs = [[_flat_rows(t[n]) for n in SMALL] for t in (weights, mom_m, mom_v)]
    starts, at = [], 0
    for w_rows in slabs[0]:
        starts.append(at)
        at += w_rows.shape[0]
    loss = small_all[at, 0]
    results = _adamw([weights[n][0] for n in BIG] + slabs[0], [grads[n] for n in BIG] + starts,
                     [mom_m[n][0] for n in BIG] + slabs[1], [mom_v[n][0] for n in BIG] + slabs[2], slab=small_all)
    outs = {n: tuple(a[None] for a in res) for n, res in zip(BIG, results)}
    for n, a, res in zip(SMALL, like, results[len(BIG):]):
        outs[n] = tuple(r.reshape(-1)[:a.size].reshape(a.shape) for r in res)

    return (loss, dx[None], *[outs[n][0] for n in ORDER], *[outs[n][1] for n in ORDER],
            *[outs[n][2] for n in ORDER], *[outs[n][3] for n in ORDER])
```
